```python
import math
import jax
import jax.numpy as jnp
from jax import lax
import numpy as np

D_MODEL = 1024
BATCH = 4
SEQ = 8192
DEPTH = 1

N_META = 16
CHUNK = 64
EPS = 1e-6
HG_HEADS = 4
HG_DIM = 128
HG_WIDTH = HG_HEADS * HG_DIM
GD_HEADS = 4
GD_DK = 128
GD_DV = 128
CONV_W = 4
N_GROUPS = 4
EXPERTS_PER_GROUP = 8
N_EXPERTS = N_GROUPS * EXPERTS_PER_GROUP
TOP_K_FINE = 2
D_FF_EXPERT = 512
MOE_BLOCK = 128
PROJ_SIZES = (HG_WIDTH, HG_WIDTH, HG_WIDTH, HG_WIDTH,
              GD_HEADS * GD_DK, GD_HEADS * GD_DK, GD_HEADS * GD_DV, GD_HEADS * GD_DV,
              GD_HEADS, GD_HEADS, D_MODEL, D_MODEL)
D_PROJ = sum(PROJ_SIZES)

kernel_name = 'hybrid_hgrn2_gdn_hmoe_block'


def rmsnorm(x, g):
    xf = x.astype(jnp.float32)
    y = xf * lax.rsqrt(jnp.mean(xf * xf, axis=-1, keepdims=True) + EPS) * g.astype(jnp.float32)
    return y.astype(x.dtype)


def head_rmsnorm(o, g):
    return o * lax.rsqrt(jnp.mean(o * o, axis=-1, keepdims=True) + EPS) * g.astype(jnp.float32)


def l2norm(t):
    return t * lax.rsqrt(jnp.sum(t * t, axis=-1, keepdims=True) + EPS)


def causal_depthwise_conv(x, w):
    return lax.conv_general_dilated(
        x, w[:, None, :].astype(x.dtype), window_strides=(1,), padding=[(CONV_W - 1, 0)],
        dimension_numbers=('NWC', 'WIO', 'NWC'), feature_group_count=x.shape[-1])


def run_chunked(step, state0, seqs):
    bsz, length = seqs[0].shape[:2]
    n_chunks = (length - N_META) // CHUNK
    state, out_meta = step(state0, tuple(s[:, :N_META] for s in seqs))
    real = tuple(jnp.moveaxis(s[:, N_META:].reshape(bsz, n_chunks, CHUNK, *s.shape[2:]), 1, 0) for s in seqs)
    _, out_real = lax.scan(step, state, real)
    out_real = jnp.moveaxis(out_real, 0, 1).reshape(bsz, n_chunks * CHUNK, *out_real.shape[3:])
    return jnp.concatenate([out_meta, out_real], axis=1)


def hgrn2_chunk(S, inp):
    q, k, v, logf = inp
    C = q.shape[1]
    b = jnp.cumsum(logf, axis=1)
    incl = jnp.tril(jnp.ones((C, C), dtype=bool))
    decay = jnp.exp(jnp.where(incl[None, :, :, None, None], b[:, :, None] - b[:, None, :], -jnp.inf))
    attn = jnp.einsum('bthk,bshk,btshk->bhts', q, k, decay)
    o = jnp.einsum('bthk,bhkv->bthv', q * jnp.exp(b), S) + jnp.einsum('bhts,bshv->bthv', attn, v)
    b_end = b[:, -1:]
    S_new = jnp.exp(b_end[:, 0])[..., None] * S + jnp.einsum('bshk,bshv->bhkv', k * jnp.exp(b_end - b), v)
    return S_new, o


def gdn_chunk(S, inp):
    q, k, v, beta, g = inp
    C = q.shape[1]
    q, k, v = (t.transpose(0, 2, 1, 3) for t in (q, k, v))
    beta = beta.transpose(0, 2, 1)
    G = jnp.cumsum(g, axis=1).transpose(0, 2, 1)
    incl = jnp.tril(jnp.ones((C, C), dtype=bool))
    strict = jnp.tril(jnp.ones((C, C), dtype=bool), -1)
    ratio = jnp.exp(jnp.where(incl, G[..., :, None] - G[..., None, :], -jnp.inf))
    L = jnp.where(strict, beta[..., :, None] * ratio * jnp.einsum('bhtk,bhsk->bhts', k, k), 0.0)
    rhs = jnp.concatenate([beta[..., None] * v, (beta * jnp.exp(G))[..., None] * k], axis=-1)
    sol = lax.linalg.triangular_solve(jnp.eye(C, dtype=L.dtype) + L, rhs,
                                      left_side=True, lower=True, unit_diagonal=True)
    u0, w = sol[..., :GD_DV], sol[..., GD_DV:]
    u = u0 - jnp.einsum('bhtk,bhkv->bhtv', w, S)
    attn = jnp.einsum('bhtk,bhsk->bhts', q, k) * ratio
    o = jnp.exp(G)[..., None] * jnp.einsum('bhtk,bhkv->bhtv', q, S) + jnp.einsum('bhts,bhsv->bhtv', attn, u)
    G_end = G[..., -1:]
    S_new = jnp.exp(G_end)[..., None] * S + jnp.einsum('bhsk,bhsv->bhkv', k * jnp.exp(G_end - G)[..., None], u)
    return S_new, o.transpose(0, 2, 1, 3)


def token_mixer(h, lb, w_in, conv_w, A_log, dt_bias, hg_norm_g, gd_norm_g, hg_up, gd_up, w_out):
    f32 = jnp.float32
    bsz, length, _ = h.shape
    idx = []
    acc = 0
    for s in PROJ_SIZES[:-1]:
        acc += s
        idx.append(acc)
    proj = h @ w_in.astype(h.dtype)
    hq, hf, hi, hg, gq, gk, gv, gz, gbeta, ga, pre_ga, pre_gb = jnp.split(proj, idx, axis=-1)

    def heads(t, n_h, d):
        return t.reshape(bsz, length, n_h, d)
    fgate = lb + (1.0 - lb) * jax.nn.sigmoid(hf.astype(f32))
    q_a = heads(jax.nn.silu(hq.astype(f32)), HG_HEADS, HG_DIM)
    k_a = heads(1.0 - fgate, HG_HEADS, HG_DIM)
    logf = heads(jnp.log(fgate), HG_HEADS, HG_DIM)
    v_a = heads(hi.astype(f32), HG_HEADS, HG_DIM)
    S0a = jnp.zeros((bsz, HG_HEADS, HG_DIM, HG_DIM), f32)
    o_a = run_chunked(hgrn2_chunk, S0a, (q_a, k_a, logf and v_a, logf) if False else (q_a, k_a, v_a, logf))
    o_a = head_rmsnorm(o_a, hg_norm_g) * jax.nn.silu(heads(hg.astype(f32), HG_HEADS, HG_DIM))
    o_a = o_a.reshape(bsz, length, HG_WIDTH).astype(h.dtype)

    qkv = jax.nn.silu(causal_depthwise_conv(jnp.concatenate([gq, gk, gv], axis=-1), conv_w)).astype(f32)
    cq, ck, cv = jnp.split(qkv, [GD_HEADS * GD_DK, 2 * GD_HEADS * GD_DK], axis=-1)
    q_b = l2norm(heads(cq, GD_HEADS, GD_DK)) * (GD_DK ** -0.5)
    k_b = l2norm(heads(ck, GD_HEADS, GD_DK))
    v_b = heads(cv, GD_HEADS, GD_DV)
    beta = jax.nn.sigmoid(gbeta.astype(f32))
    g = -jnp.exp(A_log.astype(f32)) * jax.nn.softplus(ga.astype(f32) + dt_bias.astype(f32))
    S0b = jnp.zeros((bsz, GD_HEADS, GD_DK, GD_DV), f32)
    o_b = run_chunked(gdn_chunk, S0b, (q_b, k_b, v_b, beta, g))
    o_b = head_rmsnorm(o_b, gd_norm_g) * jax.nn.silu(heads(gz.astype(f32), GD_HEADS, GD_DV))
    o_b = o_b.reshape(bsz, length, GD_HEADS * GD_DV).astype(h.dtype)

    merged = (jax.nn.sigmoid(pre_ga) * (o_a @ hg_up.astype(h.dtype))
              + jax.nn.sigmoid(pre_gb) * (o_b @ gd_up.astype(h.dtype)))
    return merged @ w_out.astype(h.dtype)


def hierarchical_moe(h, rg_w, rg_b, re_w, re_b, w_gate, w_up, w_down):
    bsz, length, d = h.shape
    T = bsz * length
    xf = h.reshape(T, d)
    g_logits = (xf @ rg_w.astype(h.dtype)).astype(jnp.float32) + rg_b.astype(jnp.float32)
    g_prob = jax.nn.softmax(g_logits, axis=-1)
    grp = jnp.argmax(g_logits, axis=-1)
    p_grp = jnp.take_along_axis(g_prob, grp[:, None], axis=-1)
    e_logits = ((xf @ re_w.astype(h.dtype)).astype(jnp.float32) + re_b.astype(jnp.float32))
    e_logits = e_logits.reshape(T, N_GROUPS, EXPERTS_PER_GROUP)
    sel = jnp.take_along_axis(e_logits, grp[:, None, None], axis=1)[:, 0]
    top_p, top_i = lax.top_k(jax.nn.softmax(sel, axis=-1), TOP_K_FINE)
    weights = p_grp * top_p / jnp.sum(top_p, axis=-1, keepdims=True)
    expert_ids = grp[:, None] * EXPERTS_PER_GROUP + top_i

    A = T * TOP_K_FINE
    e_flat = expert_ids.reshape(A).astype(jnp.int32)
    tok_flat = jnp.repeat(jnp.arange(T, dtype=jnp.int32), TOP_K_FINE)
    w_flat = weights.reshape(A)
    order = jnp.argsort(e_flat)
    e_s, tok_s, w_s = e_flat[order], tok_flat[order], w_flat[order]
    counts = jnp.bincount(e_flat, length=N_EXPERTS)
    start = jnp.cumsum(counts) - counts
    padded = ((counts + MOE_BLOCK - 1) // MOE_BLOCK) * MOE_BLOCK
    pend = jnp.cumsum(padded)
    pstart = pend - padded
    dest = pstart[e_s] + (jnp.arange(A, dtype=jnp.int32) - start[e_s])
    n_blocks = -(-A // MOE_BLOCK) + N_EXPERTS
    P = n_blocks * MOE_BLOCK
    row_tok = jnp.full((P,), T, dtype=jnp.int32).at[dest].set(tok_s)
    x_ext = jnp.concatenate([xf, jnp.zeros((1, d), xf.dtype)], axis=0)
    x_buf = x_ext[row_tok].reshape(n_blocks, MOE_BLOCK, d)
    blk_expert = jnp.clip(jnp.searchsorted(pend, jnp.arange(n_blocks) * MOE_BLOCK, side='right'), 0, N_EXPERTS - 1)

    def expert_block(args):
        xb, e = args
        a = xb @ w_gate[e].astype(xb.dtype)
        u = xb @ w_up[e].astype(xb.dtype)
        return (jax.nn.silu(a) * u) @ w_down[e].astype(xb.dtype)

    y_buf = lax.map(expert_block, (x_buf, blk_expert)).reshape(P, d)
    y_s = (y_buf[dest].astype(jnp.float32) * w_s[:, None]).astype(h.dtype)
    out = jnp.zeros((T, d), h.dtype).at[tok_s].add(y_s)
    return out.reshape(bsz, length, d)


def setup_inputs(seed: int = 0) -> dict:
    key = jax.random.key(seed)
    ks = jax.random.split(key, 24)
    f32 = jnp.float32
    nrm = lambda k, shape, scale: jax.random.normal(k, shape, f32) * scale
    dt = jnp.exp(jax.random.uniform(ks[7], (DEPTH, GD_HEADS), f32, math.log(1e-3), math.log(1e-1)))
    return {
        'x': nrm(ks[0], (BATCH, SEQ, D_MODEL), 1.0),
        'meta_tokens': nrm(ks[1], (N_META, D_MODEL), 1.0),
        'hg_lb_logits': nrm(ks[2], (DEPTH + 1, HG_WIDTH), 0.5),
        'norm_mix_g': 1.0 + nrm(ks[3], (DEPTH, D_MODEL), 0.01),
        'w_in': nrm(ks[4], (DEPTH, D_MODEL, D_PROJ), D_MODEL ** -0.5),
        'gd_conv_w': nrm(ks[5], (DEPTH, CONV_W, 2 * GD_HEADS * GD_DK + GD_HEADS * GD_DV), CONV_W ** -0.5),
        'gd_A_log': jnp.log(jax.random.uniform(ks[6], (DEPTH, GD_HEADS), f32, 1.0, 16.0)),
        'gd_dt_bias': dt + jnp.log(-jnp.expm1(-dt)),
        'hg_norm_g': 1.0 + nrm(ks[8], (DEPTH, HG_DIM), 0.01),
        'gd_norm_g': 1.0 + nrm(ks[9], (DEPTH, GD_DV), 0.01),
        'hg_up': nrm(ks[10], (DEPTH, HG_WIDTH, D_MODEL), HG_WIDTH ** -0.5),
        'gd_up': nrm(ks[11], (DEPTH, GD_HEADS * GD_DV, D_MODEL), (GD_HEADS * GD_DV) ** -0.5),
        'w_out': nrm(ks[12], (DEPTH, D_MODEL, D_MODEL), D_MODEL ** -0.5),
        'norm_ffn_g': 1.0 + nrm(ks[13], (DEPTH, D_MODEL), 0.01),
        'router_group_w': nrm(ks[14], (DEPTH, D_MODEL, N_GROUPS), D_MODEL ** -0.5),
        'router_group_b': nrm(ks[15], (DEPTH, N_GROUPS), 0.01),
        'router_expert_w': nrm(ks[16], (DEPTH, D_MODEL, N_EXPERTS), D_MODEL ** -0.5),
        'router_expert_b': nrm(ks[17], (DEPTH, N_EXPERTS), 0.01),
        'w_gate': nrm(ks[18], (DEPTH, N_EXPERTS, D_MODEL, D_FF_EXPERT), D_MODEL ** -0.5),
        'w_up': nrm(ks[19], (DEPTH, N_EXPERTS, D_MODEL, D_FF_EXPERT), D_MODEL ** -0.5),
        'w_down': nrm(ks[20], (DEPTH, N_EXPERTS, D_FF_EXPERT, D_MODEL), D_FF_EXPERT ** -0.5),
        'final_norm_g': 1.0 + nrm(ks[21], (D_MODEL,), 0.01),
    }


def reference(x, meta_tokens, hg_lb_logits, norm_mix_g, w_in, gd_conv_w, gd_A_log, gd_dt_bias,
              hg_norm_g, gd_norm_g, hg_up, gd_up, w_out, norm_ffn_g, router_group_w, router_group_b,
              router_expert_w, router_expert_b, w_gate, w_up, w_down, final_norm_g):
    bsz = x.shape[0]
    meta = jnp.broadcast_to(meta_tokens[None].astype(x.dtype), (bsz, N_META, x.shape[-1]))
    h = jnp.concatenate([meta, x], axis=1)
    lb_table = jnp.cumsum(jax.nn.softmax(hg_lb_logits.astype(jnp.float32), axis=0), axis=0)
    for l in range(DEPTH):
        h = h + token_mixer(rmsnorm(h, norm_mix_g[l]), lb_table[l], w_in[l], gd_conv_w[l], gd_A_log[l],
                            gd_dt_bias[l], hg_norm_g[l], gd_norm_g[l], hg_up[l], gd_up[l], w_out[l])
        h = h + hierarchical_moe(rmsnorm(h, norm_ffn_g[l]), router_group_w[l], router_group_b[l],
                                 router_expert_w[l], router_expert_b[l], w_gate[l], w_up[l], w_down[l])
    h = rmsnorm(h, final_norm_g)
    return h[:, N_META:]
```

```python
import functools

import numpy as np
import jax
import jax.numpy as jnp
from jax import lax
from jax.experimental import pallas as pl
from jax.experimental.pallas import tpu as pltpu

F32 = jnp.float32
BF16 = jnp.bfloat16

D_MODEL = 1024
N_META = 16
CHUNK = 64
EPS = 1e-6
HEADS = 4
HEAD_DIM = 128
WIDTH = HEADS * HEAD_DIM
STACK = HEADS * CHUNK
CONV_W = 4
N_GROUPS = 4
EXPERTS_PER_GROUP = 8
N_EXPERTS = N_GROUPS * EXPERTS_PER_GROUP
D_FF = 512
N_BIG = 12 * WIDTH
LANES = 128
MOE_ROWS = 256
VMEM_LIMIT = 48 * 1024 * 1024


def _dot(a, b):
    return jnp.dot(a.astype(BF16), b.astype(BF16), preferred_element_type=F32)


def _dot_nt(a, b):
    return lax.dot_general(a.astype(BF16), b.astype(BF16), (((1,), (1,)), ((), ())),
                           preferred_element_type=F32)


def _dot_tn(a, b):
    return lax.dot_general(a.astype(BF16), b.astype(BF16), (((0,), (0,)), ((), ())),
                           preferred_element_type=F32)


def _split3(x):
    hi = x.astype(BF16)
    r1 = x - hi.astype(F32)
    mid = r1.astype(BF16)
    lo = (r1 - mid.astype(F32)).astype(BF16)
    return hi, mid, lo


def _sigmoid(x):
    return 1.0 / (1.0 + jnp.exp(-x))


def _silu(x):
    return x * _sigmoid(x)


def _softplus(x):
    return jnp.maximum(x, 0.0) + jnp.log(1.0 + jnp.exp(-jnp.abs(x)))


def _stack_heads(x):
    return jnp.concatenate([x[:, h * HEAD_DIM:(h + 1) * HEAD_DIM] for h in range(HEADS)], axis=0)


def _unstack_heads(x):
    return jnp.concatenate([x[h * CHUNK:(h + 1) * CHUNK, :] for h in range(HEADS)], axis=1)


def _roll_rows(x, shift):
    n = x.shape[0]
    return pltpu.roll(x, shift % n, axis=0)


def _norm_proj_kernel(x_ref, g_ref, w_ref, o_ref):
    x = x_ref[...]
    xn = x * lax.rsqrt(jnp.mean(x * x, axis=-1, keepdims=True) + EPS) * g_ref[...]
    o_ref[...] = jnp.dot(xn.astype(BF16), w_ref[...], preferred_element_type=F32)


def _norm_proj(x2d, g, w, tm, tn):
    m, k = x2d.shape
    n = w.shape[1]
    return pl.pallas_call(
        _norm_proj_kernel,
        grid=(n // tn, m // tm),
        in_specs=[pl.BlockSpec((tm, k), lambda j, i: (i, 0)),
                  pl.BlockSpec((1, k), lambda j, i: (0, 0)),
                  pl.BlockSpec((k, tn), lambda j, i: (0, j))],
        out_specs=pl.BlockSpec((tm, tn), lambda j, i: (i, j)),
        out_shape=jax.ShapeDtypeStruct((m, n), F32),
        compiler_params=pltpu.CompilerParams(
            dimension_semantics=("arbitrary", "arbitrary"), vmem_limit_bytes=VMEM_LIMIT),
        name="norm_proj",
    )(x2d, g, w)


HGRN_LEVELS = 6
HGRN_DIAG = HGRN_LEVELS


def _hgrn_level_table():
    r = np.arange(STACK)
    t, hd = r % CHUNK, r // CHUNK
    x = t[:, None] ^ t[None, :]
    lv = np.floor(np.log2(np.maximum(x, 1))).astype(np.int32)
    valid = (hd[:, None] == hd[None, :]) & (t[:, None] > t[None, :])
    lv = np.where(valid, lv, -1)
    lv = np.where(r[:, None] == r[None, :], HGRN_DIAG, lv)
    return jnp.asarray(lv, dtype=jnp.int32)


def _hgrn_kernel(lv_ref, lb_ref, g_ref, mq_ref, mf_ref, mi_ref, mg_ref,
                 rq_ref, rf_ref, ri_ref, rg_ref, o_ref, st_ref):
    c = pl.program_id(1)

    @pl.when(c == 0)
    def _():
        st_ref[...] = jnp.zeros_like(st_ref)

    is_meta = c == 0
    hq = jnp.where(is_meta, mq_ref[...], rq_ref[...])
    hf = jnp.where(is_meta, mf_ref[...], rf_ref[...])
    hi = jnp.where(is_meta, mi_ref[...], ri_ref[...])
    hg = jnp.where(is_meta, mg_ref[...], rg_ref[...])

    lb = lb_ref[...]
    f = lb + (1.0 - lb) * _sigmoid(hf)
    logf = jnp.log(f)
    kk = 1.0 - f
    q = _silu(hq)

    row = lax.broadcasted_iota(jnp.int32, (CHUNK, WIDTH), 0)
    b = logf
    s = 1
    while s < CHUNK:
        b = b + jnp.where(row >= s, _roll_rows(b, s), 0.0)
        s *= 2

    qs, ks, vs, bs = _stack_heads(q), _stack_heads(kk), _stack_heads(hi), _stack_heads(b)
    t = lax.broadcasted_iota(jnp.int32, (STACK, HEAD_DIM), 0) & (CHUNK - 1)
    lv = lv_ref[...]

    attn = jnp.where(lv == HGRN_DIAG, _dot_nt(qs, ks), 0.0)
    bref = bs
    bnext = _roll_rows(bs, -1)
    for li in range(HGRN_LEVELS):
        m = 1 << li
        if li > 0:
            half = m // 2
            upper = (t & (m - 1)) >= half
            bref = jnp.where(upper, _roll_rows(bref, half), bref)
            bnext = jnp.where(upper, bnext, _roll_rows(bnext, -half))
        odd = ((t >> li) & 1) == 1
        qt = jnp.where(odd, qs * jnp.exp(bs - bref), 0.0)
        kt = jnp.where(odd, 0.0, ks * jnp.exp(bnext - bs))
        attn = jnp.where(lv == li, _dot_nt(qt, kt), attn)

    o = _dot(attn, vs)
    qe = qs * jnp.exp(bs)
    o_inter = []
    for h in range(HEADS):
        sl = slice(h * CHUNK, (h + 1) * CHUNK)
        st = st_ref[h]
        o_inter.append(_dot_nt(qe[sl], st))
        b_end = bs[h * CHUNK + CHUNK - 1:h * CHUNK + CHUNK, :]
        ke = ks[sl] * jnp.exp(b_end - bs[sl])
        st_ref[h] = st * jnp.exp(b_end) + _dot_tn(vs[sl], ke)
    o = o + jnp.concatenate(o_inter, axis=0)

    o = o * lax.rsqrt(jnp.mean(o * o, axis=-1, keepdims=True) + EPS) * g_ref[...]
    o = o * _silu(_stack_heads(hg))
    o_ref[...] = _unstack_heads(o).astype(o_ref.dtype)


def _chunk_specs(col0):
    meta = [pl.BlockSpec((CHUNK, WIDTH), functools.partial(lambda b, c, j: (0, j), j=col0 + j))
            for j in range(4)]
    real = [pl.BlockSpec((None, CHUNK, WIDTH),
                         functools.partial(lambda b, c, j: (b, jnp.maximum(c - 1, 0), j), j=col0 + j))
            for j in range(4)]
    return meta + real


def _hgrn(proj_meta, proj_real, lb, norm_g):
    bsz, seq, _ = proj_real.shape
    n_chunks = seq // CHUNK + 1
    const = lambda b, c: (0, 0)
    return pl.pallas_call(
        _hgrn_kernel,
        grid=(bsz, n_chunks),
        in_specs=[pl.BlockSpec((STACK, STACK), const),
                  pl.BlockSpec((1, WIDTH), const),
                  pl.BlockSpec((1, HEAD_DIM), const)] + _chunk_specs(0),
        out_specs=pl.BlockSpec((None, CHUNK, WIDTH), lambda b, c: (b, jnp.maximum(c - 1, 0), 0)),
        out_shape=jax.ShapeDtypeStruct((bsz, seq, WIDTH), BF16),
        scratch_shapes=[pltpu.VMEM((HEADS, HEAD_DIM, HEAD_DIM), F32)],
        compiler_params=pltpu.CompilerParams(
            dimension_semantics=("arbitrary", "arbitrary"), vmem_limit_bytes=VMEM_LIMIT),
        name="hgrn",
    )(_hgrn_level_table(), lb, norm_g, *([proj_meta] * 4), *([proj_real] * 4))


def _gdn_cumsum_matrix():
    r = np.arange(STACK)
    u = (r[:, None] // CHUNK == r[None, :] // CHUNK) & (r[:, None] <= r[None, :])
    return jnp.asarray(u, dtype=BF16)


def _gdn_kernel(u_ref, alog_ref, dt_ref, cw_ref, g_ref, s_ref,
                mq_ref, mk_ref, mv_ref, mz_ref, rq_ref, rk_ref, rv_ref, rz_ref,
                o_ref, st_ref, prev_ref):
    c = pl.program_id(1)

    @pl.when(c == 0)
    def _():
        st_ref[...] = jnp.zeros_like(st_ref)
        prev_ref[...] = jnp.zeros_like(prev_ref)

    is_meta = c == 0
    xq = jnp.where(is_meta, mq_ref[...], rq_ref[...])
    xk = jnp.where(is_meta, mk_ref[...], rk_ref[...])
    xv = jnp.where(is_meta, mv_ref[...], rv_ref[...])
    xz = jnp.where(is_meta, mz_ref[...], rz_ref[...])

    x = jnp.concatenate([xq, xk, xv], axis=1)
    pv = prev_ref[...]
    row = lax.broadcasted_iota(jnp.int32, x.shape, 0)
    cw = cw_ref[...]
    acc = x * cw[CONV_W - 1:CONV_W, :]
    for d in range(1, CONV_W):
        xs = jnp.where(row >= d, _roll_rows(x, d), _roll_rows(pv, d))
        acc = acc + xs * cw[CONV_W - 1 - d:CONV_W - d, :]
    prev_ref[...] = x
    y = _silu(acc)

    qs = _stack_heads(y[:, :WIDTH])
    ks = _stack_heads(y[:, WIDTH:2 * WIDTH])
    vs = _stack_heads(y[:, 2 * WIDTH:])
    qs = qs * lax.rsqrt(jnp.sum(qs * qs, axis=-1, keepdims=True) + EPS) * (HEAD_DIM ** -0.5)
    ks = ks * lax.rsqrt(jnp.sum(ks * ks, axis=-1, keepdims=True) + EPS)

    srow = s_ref[...]
    beta_row = _sigmoid(srow[0:1, :])
    g_row = -jnp.exp(alog_ref[...]) * _softplus(srow[1:2, :] + dt_ref[...])
    g_hi, g_mid, g_lo = _split3(jnp.broadcast_to(g_row, (8, STACK)))
    u = u_ref[...]
    cum = (jnp.dot(g_hi, u, preferred_element_type=F32)
           + jnp.dot(g_mid, u, preferred_element_type=F32)
           + jnp.dot(g_lo, u, preferred_element_type=F32))
    cum_row = cum[0:1, :]

    r = lax.broadcasted_iota(jnp.int32, (STACK, STACK), 0)
    cc = lax.broadcasted_iota(jnp.int32, (STACK, STACK), 1)
    eye = r == cc
    cum_col = jnp.sum(jnp.where(eye, cum_row, 0.0), axis=1, keepdims=True)
    beta_col = jnp.sum(jnp.where(eye, beta_row, 0.0), axis=1, keepdims=True)
    same = (r >> 6) == (cc >> 6)
    lower = same & (r >= cc)
    strict = same & (r > cc)
    ratio = jnp.exp(jnp.minimum(cum_col - cum_row, 0.0))

    kkt = _dot_nt(ks, ks)
    qkt = _dot_nt(qs, ks)
    lm = jnp.where(strict, beta_col * ratio * kkt, 0.0)

    ident = eye.astype(F32)
    l0 = jnp.where((r >> 3) == (cc >> 3), lm, 0.0)
    l2 = _dot(l0, l0)
    l4 = _dot(l2, l2)
    xinv = _dot(_dot(ident - l0, ident + l2), ident + l4)
    for lvl in (3, 4, 5):
        off = ((r >> (lvl + 1)) == (cc >> (lvl + 1))) & ((r >> lvl) != (cc >> lvl))
        xinv = xinv - _dot(_dot(xinv, jnp.where(off, lm, 0.0)), xinv)

    decay_col = jnp.exp(cum_col)
    rhs = jnp.concatenate([beta_col * vs, (beta_col * decay_col) * ks], axis=1)
    sol = _dot(xinv, rhs)
    u0 = sol[:, :HEAD_DIM]
    w = sol[:, HEAD_DIM:]

    us, o_inter = [], []
    for h in range(HEADS):
        sl = slice(h * CHUNK, (h + 1) * CHUNK)
        st = st_ref[h]
        u_h = u0[sl] - _dot(w[sl], st)
        us.append(u_h)
        o_inter.append(decay_col[sl] * _dot(qs[sl], st))
        cum_end = cum_col[h * CHUNK + CHUNK - 1:h * CHUNK + CHUNK, :]
        kdec = ks[sl] * jnp.exp(cum_end - cum_col[sl])
        st_ref[h] = jnp.exp(cum_end) * st + _dot_tn(kdec, u_h)
    u_all = jnp.concatenate(us, axis=0)
    attn = jnp.where(lower, qkt * ratio, 0.0)
    o = jnp.concatenate(o_inter, axis=0) + _dot(attn, u_all)

    o = o * lax.rsqrt(jnp.mean(o * o, axis=-1, keepdims=True) + EPS) * g_ref[...]
    o = o * _silu(_stack_heads(xz))
    o_ref[...] = _unstack_heads(o).astype(o_ref.dtype)


def _gdn(proj_meta, proj_real, scal, a_log_row, dt_row, conv_w, norm_g):
    bsz, seq, _ = proj_real.shape
    n_chunks = seq // CHUNK + 1
    const = lambda b, c: (0, 0)
    return pl.pallas_call(
        _gdn_kernel,
        grid=(bsz, n_chunks),
        in_specs=[pl.BlockSpec((STACK, STACK), const),
                  pl.BlockSpec((1, STACK), const),
                  pl.BlockSpec((1, STACK), const),
                  pl.BlockSpec((CONV_W, 3 * WIDTH), const),
                  pl.BlockSpec((1, HEAD_DIM), const),
                  pl.BlockSpec((None, None, 2, STACK), lambda b, c: (b, c, 0, 0))] + _chunk_specs(4),
        out_specs=pl.BlockSpec((None, CHUNK, WIDTH), lambda b, c: (b, jnp.maximum(c - 1, 0), 0)),
        out_shape=jax.ShapeDtypeStruct((bsz, seq, WIDTH), BF16),
        scratch_shapes=[pltpu.VMEM((HEADS, HEAD_DIM, HEAD_DIM), F32),
                        pltpu.VMEM((CHUNK, 3 * WIDTH), F32)],
        compiler_params=pltpu.CompilerParams(
            dimension_semantics=("arbitrary", "arbitrary"), vmem_limit_bytes=VMEM_LIMIT),
        name="gdn",
    )(_gdn_cumsum_matrix(), a_log_row, dt_row, conv_w, norm_g, scal,
      *([proj_meta] * 4), *([proj_real] * 4))


def _merge_kernel(oa_ref, ob_ref, ga_ref, gb_ref, x_ref, hgup_ref, gdup_ref, wout_ref,
                  ng_ref, rwh_ref, rwl_ref, rb_ref, h2_ref, xn_ref, lg_ref):
    ua = jnp.dot(oa_ref[...], hgup_ref[...], preferred_element_type=F32)
    ub = jnp.dot(ob_ref[...], gdup_ref[...], preferred_element_type=F32)
    merged = _sigmoid(ga_ref[...]) * ua + _sigmoid(gb_ref[...]) * ub
    h2 = x_ref[...] + jnp.dot(merged.astype(BF16), wout_ref[...], preferred_element_type=F32)
    h2_ref[...] = h2
    xn = h2 * lax.rsqrt(jnp.mean(h2 * h2, axis=-1, keepdims=True) + EPS) * ng_ref[...]
    xn_ref[...] = xn
    xh = xn.astype(BF16)
    xl = (xn - xh.astype(F32)).astype(BF16)
    wh, wl = rwh_ref[...], rwl_ref[...]
    lg_ref[...] = (jnp.dot(xh, wh, preferred_element_type=F32)
                   + jnp.dot(xh, wl, preferred_element_type=F32)
                   + jnp.dot(xl, wh, preferred_element_type=F32)) + rb_ref[...]


def _merge(o_a, o_b, proj, x2d, hg_up, gd_up, w_out, norm_g, rw_hi, rw_lo, rb, tm):
    t = x2d.shape[0]
    row = lambda i: (i, 0)
    const = lambda i: (0, 0)
    return pl.pallas_call(
        _merge_kernel,
        grid=(t // tm,),
        in_specs=[pl.BlockSpec((tm, WIDTH), row),
                  pl.BlockSpec((tm, WIDTH), row),
                  pl.BlockSpec((tm, D_MODEL), lambda i: (i, 4)),
                  pl.BlockSpec((tm, D_MODEL), lambda i: (i, 5)),
                  pl.BlockSpec((tm, D_MODEL), row),
                  pl.BlockSpec((WIDTH, D_MODEL), const),
                  pl.BlockSpec((WIDTH, D_MODEL), const),
                  pl.BlockSpec((D_MODEL, D_MODEL), const),
                  pl.BlockSpec((1, D_MODEL), const),
                  pl.BlockSpec((D_MODEL, LANES), const),
                  pl.BlockSpec((D_MODEL, LANES), const),
                  pl.BlockSpec((1, LANES), const)],
        out_specs=[pl.BlockSpec((tm, D_MODEL), row),
                   pl.BlockSpec((tm, D_MODEL), row),
                   pl.BlockSpec((tm, LANES), row)],
        out_shape=[jax.ShapeDtypeStruct((t, D_MODEL), F32),
                   jax.ShapeDtypeStruct((t, D_MODEL), F32),
                   jax.ShapeDtypeStruct((t, LANES), F32)],
        compiler_params=pltpu.CompilerParams(
            dimension_semantics=("arbitrary",), vmem_limit_bytes=VMEM_LIMIT),
        name="merge",
    )(o_a, o_b, proj, proj, x2d, hg_up, gd_up, w_out, norm_g, rw_hi, rw_lo, rb)


def _moe_kernel(rt_ref, be_ref, nu_ref, x_hbm, wg_ref, wu_ref, wd_ref, y_ref, xbuf, sem):
    i = pl.program_id(0)

    def row_copy(tok, k):
        return pltpu.make_async_copy(x_hbm.at[pl.ds(tok, 1), :], xbuf.at[pl.ds(k, 1), :], sem)

    @pl.when(i < nu_ref[0])
    def _():
        base = i * MOE_ROWS

        def issue(k, carry):
            row_copy(rt_ref[base + k], k).start()
            return carry

        lax.fori_loop(0, MOE_ROWS, issue, 0)

        def wait(k, carry):
            row_copy(0, k).wait()
            return carry

        lax.fori_loop(0, MOE_ROWS, wait, 0)
        xb = xbuf[...].astype(BF16)
        a = jnp.dot(xb, wg_ref[...], preferred_element_type=F32)
        u = jnp.dot(xb, wu_ref[...], preferred_element_type=F32)
        y_ref[...] = jnp.dot((_silu(a) * u).astype(BF16), wd_ref[...], preferred_element_type=F32)

    @pl.when(i >= nu_ref[0])
    def _():
        y_ref[...] = jnp.zeros_like(y_ref)


def _moe(row_tok, blk_expert, n_used, xn, w_gate, w_up, w_down):
    n_blocks = blk_expert.shape[0]
    grid_spec = pltpu.PrefetchScalarGridSpec(
        num_scalar_prefetch=3,
        grid=(n_blocks,),
        in_specs=[pl.BlockSpec(memory_space=pl.ANY),
                  pl.BlockSpec((None, D_MODEL, D_FF), lambda i, rt, be, nu: (be[i], 0, 0)),
                  pl.BlockSpec((None, D_MODEL, D_FF), lambda i, rt, be, nu: (be[i], 0, 0)),
                  pl.BlockSpec((None, D_FF, D_MODEL), lambda i, rt, be, nu: (be[i], 0, 0))],
        out_specs=pl.BlockSpec((MOE_ROWS, D_MODEL), lambda i, rt, be, nu: (i, 0)),
        scratch_shapes=[pltpu.VMEM((MOE_ROWS, D_MODEL), F32), pltpu.SemaphoreType.DMA(())],
    )
    return pl.pallas_call(
        _moe_kernel,
        grid_spec=grid_spec,
        out_shape=jax.ShapeDtypeStruct((n_blocks * MOE_ROWS, D_MODEL), F32),
        compiler_params=pltpu.CompilerParams(
            dimension_semantics=("arbitrary",), vmem_limit_bytes=VMEM_LIMIT),
        name="moe",
    )(row_tok, blk_expert, n_used, xn, w_gate, w_up, w_down)


def _combine_kernel(dest_ref, h2_ref, rw_ref, g_ref, y_hbm, o_ref, ybuf, sem, *, tm):
    i = pl.program_id(0)
    base = i * (2 * tm)

    def row_copy(src, k):
        return pltpu.make_async_copy(y_hbm.at[pl.ds(src, 1), :], ybuf.at[pl.ds(k, 1), :], sem)

    def issue(k, carry):
        row_copy(dest_ref[base + k], k).start()
        return carry

    lax.fori_loop(0, 2 * tm, issue, 0)

    def wait(k, carry):
        row_copy(0, k).wait()
        return carry

    lax.fori_loop(0, 2 * tm, wait, 0)
    rw = rw_ref[...]
    h = h2_ref[...] + rw[:, 0:1] * ybuf[0:tm, :] + rw[:, 1:2] * ybuf[tm:2 * tm, :]
    o_ref[...] = h * lax.rsqrt(jnp.mean(h * h, axis=-1, keepdims=True) + EPS) * g_ref[...]


def _combine(dest, h2, rweights, final_g, y, tm):
    t = h2.shape[0]
    grid_spec = pltpu.PrefetchScalarGridSpec(
        num_scalar_prefetch=1,
        grid=(t // tm,),
        in_specs=[pl.BlockSpec((tm, D_MODEL), lambda i, d: (i, 0)),
                  pl.BlockSpec((tm, LANES), lambda i, d: (i, 0)),
                  pl.BlockSpec((1, D_MODEL), lambda i, d: (0, 0)),
                  pl.BlockSpec(memory_space=pl.ANY)],
        out_specs=pl.BlockSpec((tm, D_MODEL), lambda i, d: (i, 0)),
        scratch_shapes=[pltpu.VMEM((2 * tm, D_MODEL), F32), pltpu.SemaphoreType.DMA(())],
    )
    return pl.pallas_call(
        functools.partial(_combine_kernel, tm=tm),
        grid_spec=grid_spec,
        out_shape=jax.ShapeDtypeStruct((t, D_MODEL), F32),
        compiler_params=pltpu.CompilerParams(
            dimension_semantics=("arbitrary",), vmem_limit_bytes=VMEM_LIMIT),
        name="combine",
    )(dest, h2, rweights, final_g, y)


def _route(logits, t):
    gl = logits[:, :N_GROUPS]
    el = logits[:, N_GROUPS:N_GROUPS + N_EXPERTS].reshape(t, N_GROUPS, EXPERTS_PER_GROUP)
    g_prob = jax.nn.softmax(gl, axis=-1)
    grp = jnp.argmax(gl, axis=-1)
    p_grp = jnp.take_along_axis(g_prob, grp[:, None], axis=-1)
    sel = jnp.take_along_axis(el, grp[:, None, None], axis=1)[:, 0]
    top_p, top_i = lax.top_k(jax.nn.softmax(sel, axis=-1), 2)
    weights = p_grp * top_p / jnp.sum(top_p, axis=-1, keepdims=True)
    e_flat = (grp[:, None] * EXPERTS_PER_GROUP + top_i).reshape(2 * t).astype(jnp.int32)

    a = 2 * t
    tok_flat = jnp.repeat(jnp.arange(t, dtype=jnp.int32), 2)
    order = jnp.argsort(e_flat)
    e_s = e_flat[order]
    counts = jnp.bincount(e_flat, length=N_EXPERTS).astype(jnp.int32)
    start = jnp.cumsum(counts) - counts
    padded = ((counts + MOE_ROWS - 1) // MOE_ROWS) * MOE_ROWS
    pend = jnp.cumsum(padded)
    pstart = pend - padded
    dest_s = (pstart[e_s] + (jnp.arange(a, dtype=jnp.int32) - start[e_s])).astype(jnp.int32)
    n_blocks = a // MOE_ROWS + N_EXPERTS
    row_tok = jnp.zeros((n_blocks * MOE_ROWS,), jnp.int32).at[dest_s].set(tok_flat[order])
    dest = jnp.zeros((a,), jnp.int32).at[order].set(dest_s)
    blk_expert = jnp.clip(jnp.searchsorted(pend, jnp.arange(n_blocks, dtype=jnp.int32) * MOE_ROWS,
                                           side='right'), 0, N_EXPERTS - 1).astype(jnp.int32)
    n_used = (pend[-1] // MOE_ROWS).astype(jnp.int32).reshape(1)
    rweights = jnp.zeros((t, LANES), F32).at[:, :2].set(weights)
    return row_tok, blk_expert, n_used, dest, rweights


def kernel(x, meta_tokens, hg_lb_logits, norm_mix_g, w_in, gd_conv_w, gd_A_log, gd_dt_bias, hg_norm_g, gd_norm_g, hg_up, gd_up, w_out, norm_ffn_g, router_group_w, router_group_b, router_expert_w, router_expert_b, w_gate, w_up, w_down, final_norm_g):
    bsz, seq, d = x.shape
    t = bsz * seq
    x2d = x.reshape(t, d)

    lb = jnp.cumsum(jax.nn.softmax(hg_lb_logits.astype(F32), axis=0), axis=0)[0].reshape(1, WIDTH)
    w = w_in[0]
    n_qkvz = 8 * WIDTH
    w_big = jnp.concatenate([w[:, :n_qkvz], w[:, n_qkvz + 2 * HEADS:]], axis=1).astype(BF16)
    w_small = jnp.zeros((d, LANES), F32).at[:, :2 * HEADS].set(w[:, n_qkvz:n_qkvz + 2 * HEADS]).astype(BF16)
    g_mix = norm_mix_g[0].reshape(1, d)
    meta_blk = jnp.concatenate([jnp.zeros((CHUNK - N_META, d), F32), meta_tokens.astype(F32)], axis=0)

    proj = _norm_proj(x2d, g_mix, w_big, 512, 2048)
    small = _norm_proj(x2d, g_mix, w_small, 512, LANES)
    proj_meta = _norm_proj(meta_blk, g_mix, w_big, CHUNK, 2048)
    small_meta = _norm_proj(meta_blk, g_mix, w_small, CHUNK, LANES)

    proj3 = proj.reshape(bsz, seq, N_BIG)
    o_a = _hgrn(proj_meta, proj3, lb, hg_norm_g[0].reshape(1, HEAD_DIM))

    n_real = seq // CHUNK
    s_real = small[:, :2 * HEADS].reshape(bsz, n_real, CHUNK, 2, HEADS)
    s_real = s_real.transpose(0, 1, 3, 4, 2).reshape(bsz, n_real, 2, STACK)
    s_meta = small_meta[:, :2 * HEADS].reshape(1, 1, CHUNK, 2, HEADS)
    s_meta = jnp.broadcast_to(s_meta.transpose(0, 1, 3, 4, 2).reshape(1, 1, 2, STACK), (bsz, 1, 2, STACK))
    scal = jnp.concatenate([s_meta, s_real], axis=1)
    a_log_row = jnp.repeat(gd_A_log[0].astype(F32), CHUNK).reshape(1, STACK)
    dt_row = jnp.repeat(gd_dt_bias[0].astype(F32), CHUNK).reshape(1, STACK)
    o_b = _gdn(proj_meta, proj3, scal, a_log_row, dt_row, gd_conv_w[0].astype(F32),
               gd_norm_g[0].reshape(1, HEAD_DIM))

    rw = jnp.zeros((d, LANES), F32)
    rw = rw.at[:, :N_GROUPS].set(router_group_w[0]).at[:, N_GROUPS:N_GROUPS + N_EXPERTS].set(router_expert_w[0])
    rw_hi = rw.astype(BF16)
    rw_lo = (rw - rw_hi.astype(F32)).astype(BF16)
    rb = jnp.zeros((1, LANES), F32)
    rb = rb.at[0, :N_GROUPS].set(router_group_b[0]).at[0, N_GROUPS:N_GROUPS + N_EXPERTS].set(router_expert_b[0])
    h2, xn, logits = _merge(o_a.reshape(t, WIDTH), o_b.reshape(t, WIDTH), proj, x2d,
                            hg_up[0].astype(BF16), gd_up[0].astype(BF16), w_out[0].astype(BF16),
                            norm_ffn_g[0].reshape(1, d), rw_hi, rw_lo, rb, 256)

    row_tok, blk_expert, n_used, dest, rweights = _route(logits, t)
    y = _moe(row_tok, blk_expert, n_used, xn,
             w_gate[0].astype(BF16), w_up[0].astype(BF16), w_down[0].astype(BF16))
    tm_c = 256
    dest_tiles = dest.reshape(t // tm_c, tm_c, 2).transpose(0, 2, 1).reshape(2 * t)
    out = _combine(dest_tiles, h2, rweights, final_norm_g.reshape(1, d), y, tm_c)
    return out.reshape(bsz, seq, d)
```

```python
import functools

import numpy as np
import jax
import jax.numpy as jnp
from jax import lax
from jax.experimental import pallas as pl
from jax.experimental.pallas import tpu as pltpu

F32 = jnp.float32
BF16 = jnp.bfloat16

D_MODEL = 1024
N_META = 16
CHUNK = 64
EPS = 1e-6
HEADS = 4
HEAD_DIM = 128
WIDTH = HEADS * HEAD_DIM
STACK = HEADS * CHUNK
CONV_W = 4
N_GROUPS = 4
EXPERTS_PER_GROUP = 8
N_EXPERTS = N_GROUPS * EXPERTS_PER_GROUP
D_FF = 512
N_BIG = 12 * WIDTH
LANES = 128
SUBLANES = 8
MOE_ROWS = 256
VMEM_LIMIT = 48 * 1024 * 1024


def _bdot(a, b):
    return lax.dot_general(a.astype(BF16), b.astype(BF16), (((2,), (1,)), ((0,), (0,))),
                           preferred_element_type=F32)


def _bdot_nt(a, b):
    return lax.dot_general(a.astype(BF16), b.astype(BF16), (((2,), (2,)), ((0,), (0,))),
                           preferred_element_type=F32)


def _bdot_tn(a, b):
    return lax.dot_general(a.astype(BF16), b.astype(BF16), (((1,), (1,)), ((0,), (0,))),
                           preferred_element_type=F32)


def _split3(x):
    hi = x.astype(BF16)
    r1 = x - hi.astype(F32)
    mid = r1.astype(BF16)
    lo = (r1 - mid.astype(F32)).astype(BF16)
    return hi, mid, lo


def _sigmoid(x):
    return 1.0 / (1.0 + jnp.exp(-x))


def _silu(x):
    return x * _sigmoid(x)


def _softplus(x):
    return jnp.maximum(x, 0.0) + jnp.log(1.0 + jnp.exp(-jnp.abs(x)))


def _stack_heads(x, bsz):
    x3 = x.reshape(bsz, CHUNK, WIDTH)
    y = jnp.concatenate([x3[:, :, h * HEAD_DIM:(h + 1) * HEAD_DIM] for h in range(HEADS)], axis=1)
    return y.reshape(bsz * STACK, HEAD_DIM)


def _unstack_heads(x, bsz):
    x3 = x.reshape(bsz, STACK, HEAD_DIM)
    return jnp.concatenate([x3[:, h * CHUNK:(h + 1) * CHUNK, :] for h in range(HEADS)], axis=2)


def _roll_rows(x, shift):
    n = x.shape[0]
    return pltpu.roll(x, shift % n, axis=0)


def _load_chunk(is_meta, meta_ref, real_ref, bsz):
    x = jnp.where(is_meta, meta_ref[...][None], real_ref[...])
    return x.reshape(bsz * CHUNK, WIDTH)


def _norm_proj_kernel(x_ref, g_ref, w_ref, o_ref):
    x = x_ref[...]
    xn = x * lax.rsqrt(jnp.mean(x * x, axis=-1, keepdims=True) + EPS) * g_ref[...]
    o_ref[...] = jnp.dot(xn.astype(BF16), w_ref[...], preferred_element_type=F32)


def _norm_proj(x2d, g, w, tm, tn):
    m, k = x2d.shape
    n = w.shape[1]
    return pl.pallas_call(
        _norm_proj_kernel,
        grid=(n // tn, m // tm),
        in_specs=[pl.BlockSpec((tm, k), lambda j, i: (i, 0)),
                  pl.BlockSpec((1, k), lambda j, i: (0, 0)),
                  pl.BlockSpec((k, tn), lambda j, i: (0, j))],
        out_specs=pl.BlockSpec((tm, tn), lambda j, i: (i, j)),
        out_shape=jax.ShapeDtypeStruct((m, n), F32),
        compiler_params=pltpu.CompilerParams(
            dimension_semantics=("arbitrary", "arbitrary"), vmem_limit_bytes=VMEM_LIMIT),
        name="norm_proj",
    )(x2d, g, w)


HGRN_LEVELS = 6
HGRN_DIAG = HGRN_LEVELS


def _hgrn_level_table():
    r = np.arange(STACK)
    t, hd = r % CHUNK, r // CHUNK
    x = t[:, None] ^ t[None, :]
    lv = np.floor(np.log2(np.maximum(x, 1))).astype(np.int32)
    valid = (hd[:, None] == hd[None, :]) & (t[:, None] > t[None, :])
    lv = np.where(valid, lv, -1)
    lv = np.where(r[:, None] == r[None, :], HGRN_DIAG, lv)
    return jnp.asarray(lv, dtype=jnp.int32)


def _hgrn_kernel(lv_ref, lb_ref, g_ref, mq_ref, mf_ref, mi_ref, mg_ref,
                 rq_ref, rf_ref, ri_ref, rg_ref, o_ref, st_ref, *, bsz):
    c = pl.program_id(0)

    @pl.when(c == 0)
    def _():
        st_ref[...] = jnp.zeros_like(st_ref)

    is_meta = c == 0
    hq = _load_chunk(is_meta, mq_ref, rq_ref, bsz)
    hf = _load_chunk(is_meta, mf_ref, rf_ref, bsz)
    hi = _load_chunk(is_meta, mi_ref, ri_ref, bsz)
    hg = _load_chunk(is_meta, mg_ref, rg_ref, bsz)

    lb = lb_ref[...]
    f = lb + (1.0 - lb) * _sigmoid(hf)
    logf = jnp.log(f)
    kk = 1.0 - f
    q = _silu(hq)

    trow = lax.broadcasted_iota(jnp.int32, (bsz * CHUNK, WIDTH), 0) & (CHUNK - 1)
    b = logf
    s = 1
    while s < CHUNK:
        b = b + jnp.where(trow >= s, _roll_rows(b, s), 0.0)
        s *= 2

    qs, ks, vs, bs = (_stack_heads(a, bsz) for a in (q, kk, hi, b))
    t = lax.broadcasted_iota(jnp.int32, (bsz * STACK, HEAD_DIM), 0) & (CHUNK - 1)
    lv = lv_ref[...][None]
    to3 = lambda a: a.reshape(bsz, STACK, HEAD_DIM)

    attn = jnp.where(lv == HGRN_DIAG, _bdot_nt(to3(qs), to3(ks)), 0.0)
    bref = bs
    bnext = _roll_rows(bs, -1)
    for li in range(HGRN_LEVELS):
        m = 1 << li
        if li > 0:
            half = m // 2
            upper = (t & (m - 1)) >= half
            bref = jnp.where(upper, _roll_rows(bref, half), bref)
            bnext = jnp.where(upper, bnext, _roll_rows(bnext, -half))
        odd = ((t >> li) & 1) == 1
        qt = jnp.where(odd, qs * jnp.exp(bs - bref), 0.0)
        kt = jnp.where(odd, 0.0, ks * jnp.exp(bnext - bs))
        attn = jnp.where(lv == li, _bdot_nt(to3(qt), to3(kt)), attn)

    o = _bdot(attn, to3(vs))

    g = bsz * HEADS
    to_g = lambda a: a.reshape(g, CHUNK, HEAD_DIM)
    st = st_ref[...]
    bs_g = to_g(bs)
    o_inter = _bdot_nt(to_g(qs * jnp.exp(bs)), st)
    b_end = bs_g[:, CHUNK - 1:CHUNK, :]
    ke = to_g(ks) * jnp.exp(b_end - bs_g)
    st_ref[...] = st * jnp.exp(b_end) + _bdot_tn(to_g(vs), ke)
    o = o.reshape(bsz * STACK, HEAD_DIM) + o_inter.reshape(bsz * STACK, HEAD_DIM)

    o = o * lax.rsqrt(jnp.mean(o * o, axis=-1, keepdims=True) + EPS) * g_ref[...]
    o = o * _silu(_stack_heads(hg, bsz))
    o_ref[...] = _unstack_heads(o, bsz).astype(o_ref.dtype)


def _chunk_specs(bsz, col0):
    meta = [pl.BlockSpec((CHUNK, WIDTH), functools.partial(lambda c, j: (0, j), j=col0 + j))
            for j in range(4)]
    real = [pl.BlockSpec((bsz, CHUNK, WIDTH),
                         functools.partial(lambda c, j: (0, jnp.maximum(c - 1, 0), j), j=col0 + j))
            for j in range(4)]
    return meta + real


def _hgrn(proj_meta, proj_real, lb, norm_g):
    bsz, seq, _ = proj_real.shape
    n_chunks = seq // CHUNK + 1
    const = lambda c: (0, 0)
    return pl.pallas_call(
        functools.partial(_hgrn_kernel, bsz=bsz),
        grid=(n_chunks,),
        in_specs=[pl.BlockSpec((STACK, STACK), const),
                  pl.BlockSpec((1, WIDTH), const),
                  pl.BlockSpec((1, HEAD_DIM), const)] + _chunk_specs(bsz, 0),
        out_specs=pl.BlockSpec((bsz, CHUNK, WIDTH), lambda c: (0, jnp.maximum(c - 1, 0), 0)),
        out_shape=jax.ShapeDtypeStruct((bsz, seq, WIDTH), BF16),
        scratch_shapes=[pltpu.VMEM((bsz * HEADS, HEAD_DIM, HEAD_DIM), F32)],
        compiler_params=pltpu.CompilerParams(
            dimension_semantics=("arbitrary",), vmem_limit_bytes=VMEM_LIMIT),
        name="hgrn",
    )(_hgrn_level_table(), lb, norm_g, *([proj_meta] * 4), *([proj_real] * 4))


def _gdn_cumsum_matrix():
    r = np.arange(STACK)
    u = (r[:, None] // CHUNK == r[None, :] // CHUNK) & (r[:, None] <= r[None, :])
    return jnp.asarray(u, dtype=BF16)


def _gdn_kernel(u_ref, alog_ref, dt_ref, cw_ref, g_ref, s_ref,
                mq_ref, mk_ref, mv_ref, mz_ref, rq_ref, rk_ref, rv_ref, rz_ref,
                o_ref, st_ref, prev_ref, *, bsz):
    c = pl.program_id(0)

    @pl.when(c == 0)
    def _():
        st_ref[...] = jnp.zeros_like(st_ref)
        prev_ref[...] = jnp.zeros_like(prev_ref)

    is_meta = c == 0
    xq = _load_chunk(is_meta, mq_ref, rq_ref, bsz)
    xk = _load_chunk(is_meta, mk_ref, rk_ref, bsz)
    xv = _load_chunk(is_meta, mv_ref, rv_ref, bsz)
    xz = _load_chunk(is_meta, mz_ref, rz_ref, bsz)

    x = jnp.concatenate([xq, xk, xv], axis=1)
    pv = prev_ref[...]
    trow = lax.broadcasted_iota(jnp.int32, x.shape, 0) & (CHUNK - 1)
    cw = cw_ref[...]
    acc = x * cw[CONV_W - 1:CONV_W, :]
    for d in range(1, CONV_W):
        xs = jnp.where(trow >= d, _roll_rows(x, d), _roll_rows(pv, d - CHUNK))
        acc = acc + xs * cw[CONV_W - 1 - d:CONV_W - d, :]
    prev_ref[...] = x
    y = _silu(acc)

    qs = _stack_heads(y[:, :WIDTH], bsz)
    ks = _stack_heads(y[:, WIDTH:2 * WIDTH], bsz)
    vs = _stack_heads(y[:, 2 * WIDTH:], bsz)
    qs = qs * lax.rsqrt(jnp.sum(qs * qs, axis=-1, keepdims=True) + EPS) * (HEAD_DIM ** -0.5)
    ks = ks * lax.rsqrt(jnp.sum(ks * ks, axis=-1, keepdims=True) + EPS)
    to3 = lambda a: a.reshape(bsz, STACK, a.shape[-1])
    qs, ks, vs = to3(qs), to3(ks), to3(vs)

    srow = s_ref[...]
    beta_row = _sigmoid(srow[:, 0:1, :])
    g_row = -jnp.exp(alog_ref[...]) * _softplus(srow[:, 1:2, :] + dt_ref[...])
    g8 = jnp.broadcast_to(g_row, (bsz, SUBLANES, STACK)).reshape(bsz * SUBLANES, STACK)
    pieces = jnp.concatenate(_split3(g8), axis=0)
    cum = jnp.dot(pieces, u_ref[...], preferred_element_type=F32)
    n8 = bsz * SUBLANES
    cum = cum[0:n8] + cum[n8:2 * n8] + cum[2 * n8:]
    cum_row = cum.reshape(bsz, SUBLANES, STACK)[:, 0:1, :]

    r = lax.broadcasted_iota(jnp.int32, (STACK, STACK), 0)
    cc = lax.broadcasted_iota(jnp.int32, (STACK, STACK), 1)
    eye = (r == cc)[None]
    cum_col = jnp.sum(jnp.where(eye, cum_row, 0.0), axis=2, keepdims=True)
    beta_col = jnp.sum(jnp.where(eye, beta_row, 0.0), axis=2, keepdims=True)
    same = (r >> 6) == (cc >> 6)
    lower = (same & (r >= cc))[None]
    strict = (same & (r > cc))[None]
    ratio = jnp.exp(jnp.minimum(cum_col - cum_row, 0.0))

    kkt = _bdot_nt(ks, ks)
    qkt = _bdot_nt(qs, ks)
    lm = jnp.where(strict, beta_col * ratio * kkt, 0.0)

    ident = eye.astype(F32)
    l0 = jnp.where(((r >> 3) == (cc >> 3))[None], lm, 0.0)
    l2 = _bdot(l0, l0)
    l4 = _bdot(l2, l2)
    xinv = _bdot(_bdot(ident - l0, ident + l2), ident + l4)
    for lvl in (3, 4, 5):
        off = (((r >> (lvl + 1)) == (cc >> (lvl + 1))) & ((r >> lvl) != (cc >> lvl)))[None]
        xinv = xinv - _bdot(_bdot(xinv, jnp.where(off, lm, 0.0)), xinv)

    decay_col = jnp.exp(cum_col)
    rhs = jnp.concatenate([beta_col * vs, (beta_col * decay_col) * ks], axis=2)
    sol = _bdot(xinv, rhs)

    g = bsz * HEADS
    to_g = lambda a: a.reshape(g, CHUNK, a.shape[-1])
    st = st_ref[...]
    u = to_g(sol[:, :, :HEAD_DIM]) - _bdot(to_g(sol[:, :, HEAD_DIM:]), st)
    o_inter = to_g(decay_col) * _bdot(to_g(qs), st)
    cum_g = to_g(cum_col)
    cum_end = cum_g[:, CHUNK - 1:CHUNK, :]
    kdec = to_g(ks) * jnp.exp(cum_end - cum_g)
    st_ref[...] = jnp.exp(cum_end) * st + _bdot_tn(kdec, u)
    attn = jnp.where(lower, qkt * ratio, 0.0)
    o = o_inter.reshape(bsz, STACK, HEAD_DIM) + _bdot(attn, u.reshape(bsz, STACK, HEAD_DIM))
    o = o.reshape(bsz * STACK, HEAD_DIM)

    o = o * lax.rsqrt(jnp.mean(o * o, axis=-1, keepdims=True) + EPS) * g_ref[...]
    o = o * _silu(_stack_heads(xz, bsz))
    o_ref[...] = _unstack_heads(o, bsz).astype(o_ref.dtype)


def _gdn(proj_meta, proj_real, scal, a_log_row, dt_row, conv_w, norm_g):
    bsz, seq, _ = proj_real.shape
    n_chunks = seq // CHUNK + 1
    const = lambda c: (0, 0)
    return pl.pallas_call(
        functools.partial(_gdn_kernel, bsz=bsz),
        grid=(n_chunks,),
        in_specs=[pl.BlockSpec((STACK, STACK), const),
                  pl.BlockSpec((1, STACK), const),
                  pl.BlockSpec((1, STACK), const),
                  pl.BlockSpec((CONV_W, 3 * WIDTH), const),
                  pl.BlockSpec((1, HEAD_DIM), const),
                  pl.BlockSpec((bsz, None, 2, STACK), lambda c: (0, c, 0, 0))] + _chunk_specs(bsz, 4),
        out_specs=pl.BlockSpec((bsz, CHUNK, WIDTH), lambda c: (0, jnp.maximum(c - 1, 0), 0)),
        out_shape=jax.ShapeDtypeStruct((bsz, seq, WIDTH), BF16),
        scratch_shapes=[pltpu.VMEM((bsz * HEADS, HEAD_DIM, HEAD_DIM), F32),
                        pltpu.VMEM((bsz * CHUNK, 3 * WIDTH), F32)],
        compiler_params=pltpu.CompilerParams(
            dimension_semantics=("arbitrary",), vmem_limit_bytes=VMEM_LIMIT),
        name="gdn",
    )(_gdn_cumsum_matrix(), a_log_row, dt_row, conv_w, norm_g, scal,
      *([proj_meta] * 4), *([proj_real] * 4))


def _merge_kernel(oa_ref, ob_ref, ga_ref, gb_ref, x_ref, hgup_ref, gdup_ref, wout_ref,
                  ng_ref, rwh_ref, rwl_ref, rb_ref, h2_ref, xn_ref, lg_ref):
    ua = jnp.dot(oa_ref[...], hgup_ref[...], preferred_element_type=F32)
    ub = jnp.dot(ob_ref[...], gdup_ref[...], preferred_element_type=F32)
    merged = _sigmoid(ga_ref[...]) * ua + _sigmoid(gb_ref[...]) * ub
    h2 = x_ref[...] + jnp.dot(merged.astype(BF16), wout_ref[...], preferred_element_type=F32)
    h2_ref[...] = h2
    xn = h2 * lax.rsqrt(jnp.mean(h2 * h2, axis=-1, keepdims=True) + EPS) * ng_ref[...]
    xn_ref[...] = xn
    xh = xn.astype(BF16)
    xl = (xn - xh.astype(F32)).astype(BF16)
    wh, wl = rwh_ref[...], rwl_ref[...]
    lg_ref[...] = (jnp.dot(xh, wh, preferred_element_type=F32)
                   + jnp.dot(xh, wl, preferred_element_type=F32)
                   + jnp.dot(xl, wh, preferred_element_type=F32)) + rb_ref[...]


def _merge(o_a, o_b, proj, x2d, hg_up, gd_up, w_out, norm_g, rw_hi, rw_lo, rb, tm):
    t = x2d.shape[0]
    row = lambda i: (i, 0)
    const = lambda i: (0, 0)
    return pl.pallas_call(
        _merge_kernel,
        grid=(t // tm,),
        in_specs=[pl.BlockSpec((tm, WIDTH), row),
                  pl.BlockSpec((tm, WIDTH), row),
                  pl.BlockSpec((tm, D_MODEL), lambda i: (i, 4)),
                  pl.BlockSpec((tm, D_MODEL), lambda i: (i, 5)),
                  pl.BlockSpec((tm, D_MODEL), row),
                  pl.BlockSpec((WIDTH, D_MODEL), const),
                  pl.BlockSpec((WIDTH, D_MODEL), const),
                  pl.BlockSpec((D_MODEL, D_MODEL), const),
                  pl.BlockSpec((1, D_MODEL), const),
                  pl.BlockSpec((D_MODEL, LANES), const),
                  pl.BlockSpec((D_MODEL, LANES), const),
                  pl.BlockSpec((1, LANES), const)],
        out_specs=[pl.BlockSpec((tm, D_MODEL), row),
                   pl.BlockSpec((tm, D_MODEL), row),
                   pl.BlockSpec((tm, LANES), row)],
        out_shape=[jax.ShapeDtypeStruct((t, D_MODEL), F32),
                   jax.ShapeDtypeStruct((t, D_MODEL), F32),
                   jax.ShapeDtypeStruct((t, LANES), F32)],
        compiler_params=pltpu.CompilerParams(
            dimension_semantics=("arbitrary",), vmem_limit_bytes=VMEM_LIMIT),
        name="merge",
    )(o_a, o_b, proj, proj, x2d, hg_up, gd_up, w_out, norm_g, rw_hi, rw_lo, rb)


def _moe_kernel(rt_ref, orow_ref, be_ref, x_hbm, wg_ref, wu_ref, wd_ref, y_hbm,
                xbuf0, xbuf1, ybuf0, ybuf1, sem_in, sem_out):
    i = pl.program_id(0)
    n = pl.num_programs(0)
    xbufs, ybufs = (xbuf0, xbuf1), (ybuf0, ybuf1)

    def gather_row(slot, blk, k):
        tok = rt_ref[blk * MOE_ROWS + k]
        return pltpu.make_async_copy(x_hbm.at[pl.ds(tok, 1), :], xbufs[slot].at[pl.ds(k, 1), :],
                                     sem_in.at[slot])

    def scatter_row(slot, blk, k):
        dst = orow_ref[blk * MOE_ROWS + k]
        return pltpu.make_async_copy(ybufs[slot].at[pl.ds(k, 1), :], y_hbm.at[pl.ds(dst, 1), :],
                                     sem_out.at[slot])

    def wait_gather(slot):
        pltpu.make_async_copy(x_hbm.at[pl.ds(0, MOE_ROWS), :], xbufs[slot], sem_in.at[slot]).wait()

    def wait_scatter(slot):
        pltpu.make_async_copy(ybufs[slot], y_hbm.at[pl.ds(0, MOE_ROWS), :], sem_out.at[slot]).wait()

    @pl.when(i == 0)
    def _():
        for k in range(MOE_ROWS):
            gather_row(0, 0, k).start()

    def step(slot):
        wait_gather(slot)

        @pl.when(i >= 2)
        def _():
            wait_scatter(slot)

        for k in range(MOE_ROWS):
            gather_row(1 - slot, i + 1, k).start()
        xb = xbufs[slot][...].astype(BF16)
        a = jnp.dot(xb, wg_ref[...], preferred_element_type=F32)
        u = jnp.dot(xb, wu_ref[...], preferred_element_type=F32)
        ybufs[slot][...] = jnp.dot((_silu(a) * u).astype(BF16), wd_ref[...], preferred_element_type=F32)
        for k in range(MOE_ROWS):
            scatter_row(slot, i, k).start()

        @pl.when(i == n - 1)
        def _():
            wait_gather(1 - slot)
            wait_scatter(slot)

            @pl.when(i >= 1)
            def _():
                wait_scatter(1 - slot)

    @pl.when(i % 2 == 0)
    def _():
        step(0)

    @pl.when(i % 2 == 1)
    def _():
        step(1)


def _moe(row_tok, out_row, blk_expert, xn, w_gate, w_up, w_down, n_out_rows):
    n_blocks = blk_expert.shape[0]
    wspec = lambda shape: pl.BlockSpec((None,) + shape, lambda i, rt, orow, be: (be[i], 0, 0))
    grid_spec = pltpu.PrefetchScalarGridSpec(
        num_scalar_prefetch=3,
        grid=(n_blocks,),
        in_specs=[pl.BlockSpec(memory_space=pl.ANY),
                  wspec((D_MODEL, D_FF)), wspec((D_MODEL, D_FF)), wspec((D_FF, D_MODEL))],
        out_specs=pl.BlockSpec(memory_space=pl.ANY),
        scratch_shapes=[pltpu.VMEM((MOE_ROWS, D_MODEL), F32), pltpu.VMEM((MOE_ROWS, D_MODEL), F32),
                        pltpu.VMEM((MOE_ROWS, D_MODEL), F32), pltpu.VMEM((MOE_ROWS, D_MODEL), F32),
                        pltpu.SemaphoreType.DMA((2,)), pltpu.SemaphoreType.DMA((2,))],
    )
    return pl.pallas_call(
        _moe_kernel,
        grid_spec=grid_spec,
        out_shape=jax.ShapeDtypeStruct((n_out_rows, D_MODEL), F32),
        compiler_params=pltpu.CompilerParams(
            dimension_semantics=("arbitrary",), vmem_limit_bytes=VMEM_LIMIT),
        name="moe",
    )(row_tok, out_row, blk_expert, xn, w_gate, w_up, w_down)


def _combine_kernel(h2_ref, rw_ref, g_ref, y0_ref, y1_ref, o_ref):
    rw = rw_ref[...]
    h = h2_ref[...] + rw[:, 0:1] * y0_ref[...] + rw[:, 1:2] * y1_ref[...]
    o_ref[...] = h * lax.rsqrt(jnp.mean(h * h, axis=-1, keepdims=True) + EPS) * g_ref[...]


def _combine(h2, rweights, final_g, y, tm):
    t = h2.shape[0]
    row = lambda i: (i, 0)
    return pl.pallas_call(
        _combine_kernel,
        grid=(t // tm,),
        in_specs=[pl.BlockSpec((tm, D_MODEL), row),
                  pl.BlockSpec((tm, LANES), row),
                  pl.BlockSpec((1, D_MODEL), lambda i: (0, 0)),
                  pl.BlockSpec((tm, D_MODEL), row),
                  pl.BlockSpec((tm, D_MODEL), lambda i: (i + t // tm, 0))],
        out_specs=pl.BlockSpec((tm, D_MODEL), row),
        out_shape=jax.ShapeDtypeStruct((t, D_MODEL), F32),
        compiler_params=pltpu.CompilerParams(
            dimension_semantics=("arbitrary",), vmem_limit_bytes=VMEM_LIMIT),
        name="combine",
    )(h2, rweights, final_g, y, y)


def _route(logits, t):
    gl = logits[:, :N_GROUPS]
    el = logits[:, N_GROUPS:N_GROUPS + N_EXPERTS].reshape(t, N_GROUPS, EXPERTS_PER_GROUP)
    g_prob = jax.nn.softmax(gl, axis=-1)
    grp = jnp.argmax(gl, axis=-1)
    p_grp = jnp.take_along_axis(g_prob, grp[:, None], axis=-1)
    sel = jnp.take_along_axis(el, grp[:, None, None], axis=1)[:, 0]
    top_p, top_i = lax.top_k(jax.nn.softmax(sel, axis=-1), 2)
    weights = p_grp * top_p / jnp.sum(top_p, axis=-1, keepdims=True)
    e_flat = (grp[:, None] * EXPERTS_PER_GROUP + top_i).reshape(2 * t).astype(jnp.int32)

    a = 2 * t
    n_blocks = a // MOE_ROWS + N_EXPERTS
    experts = jnp.arange(N_EXPERTS, dtype=jnp.int32)
    order = jnp.argsort(e_flat).astype(jnp.int32)
    counts = jnp.sum((e_flat[:, None] == experts[None, :]).astype(jnp.int32), axis=0)
    start = jnp.cumsum(counts) - counts
    padded = ((counts + MOE_ROWS - 1) // MOE_ROWS) * MOE_ROWS
    pend = jnp.cumsum(padded)
    pstart = pend - padded
    blk_first = jnp.arange(n_blocks, dtype=jnp.int32) * MOE_ROWS
    blk_expert = jnp.minimum(jnp.sum((pend[None, :] <= blk_first[:, None]).astype(jnp.int32), axis=1),
                             N_EXPERTS - 1).astype(jnp.int32)
    p = jnp.arange(n_blocks * MOE_ROWS, dtype=jnp.int32)
    e_p = jnp.repeat(blk_expert, MOE_ROWS)
    j = p - pstart[e_p]
    valid = j < counts[e_p]
    src = order[jnp.clip(start[e_p] + j, 0, a - 1)]
    row_tok = jnp.where(valid, src // 2, 0)
    out_row = jnp.where(valid, (src % 2) * t + src // 2, a + p - (start[e_p] + counts[e_p]))
    row_tok = jnp.concatenate([row_tok, jnp.zeros((MOE_ROWS,), jnp.int32)])
    rweights = jnp.zeros((t, LANES), F32).at[:, :2].set(weights)
    return row_tok, out_row, blk_expert, rweights, n_blocks * MOE_ROWS


def kernel(x, meta_tokens, hg_lb_logits, norm_mix_g, w_in, gd_conv_w, gd_A_log, gd_dt_bias, hg_norm_g, gd_norm_g, hg_up, gd_up, w_out, norm_ffn_g, router_group_w, router_group_b, router_expert_w, router_expert_b, w_gate, w_up, w_down, final_norm_g):
    bsz, seq, d = x.shape
    t = bsz * seq
    x2d = x.reshape(t, d)

    lb = jnp.cumsum(jax.nn.softmax(hg_lb_logits.astype(F32), axis=0), axis=0)[0].reshape(1, WIDTH)
    w = w_in[0]
    n_qkvz = 8 * WIDTH
    w_big = jnp.concatenate([w[:, :n_qkvz], w[:, n_qkvz + 2 * HEADS:]], axis=1).astype(BF16)
    w_small = jnp.zeros((d, LANES), F32).at[:, :2 * HEADS].set(w[:, n_qkvz:n_qkvz + 2 * HEADS]).astype(BF16)
    g_mix = norm_mix_g[0].reshape(1, d)
    meta_blk = jnp.concatenate([jnp.zeros((CHUNK - N_META, d), F32), meta_tokens.astype(F32)], axis=0)

    proj = _norm_proj(x2d, g_mix, w_big, 512, 2048)
    small = _norm_proj(x2d, g_mix, w_small, 512, LANES)
    proj_meta = _norm_proj(meta_blk, g_mix, w_big, CHUNK, 2048)
    small_meta = _norm_proj(meta_blk, g_mix, w_small, CHUNK, LANES)

    proj3 = proj.reshape(bsz, seq, N_BIG)
    o_a = _hgrn(proj_meta, proj3, lb, hg_norm_g[0].reshape(1, HEAD_DIM))

    n_real = seq // CHUNK
    s_real = small[:, :2 * HEADS].reshape(bsz, n_real, CHUNK, 2, HEADS)
    s_real = s_real.transpose(0, 1, 3, 4, 2).reshape(bsz, n_real, 2, STACK)
    s_meta = small_meta[:, :2 * HEADS].reshape(1, 1, CHUNK, 2, HEADS)
    s_meta = jnp.broadcast_to(s_meta.transpose(0, 1, 3, 4, 2).reshape(1, 1, 2, STACK), (bsz, 1, 2, STACK))
    scal = jnp.concatenate([s_meta, s_real], axis=1)
    a_log_row = jnp.repeat(gd_A_log[0].astype(F32), CHUNK).reshape(1, STACK)
    dt_row = jnp.repeat(gd_dt_bias[0].astype(F32), CHUNK).reshape(1, STACK)
    o_b = _gdn(proj_meta, proj3, scal, a_log_row, dt_row, gd_conv_w[0].astype(F32),
               gd_norm_g[0].reshape(1, HEAD_DIM))

    rw = jnp.zeros((d, LANES), F32)
    rw = rw.at[:, :N_GROUPS].set(router_group_w[0]).at[:, N_GROUPS:N_GROUPS + N_EXPERTS].set(router_expert_w[0])
    rw_hi = rw.astype(BF16)
    rw_lo = (rw - rw_hi.astype(F32)).astype(BF16)
    rb = jnp.zeros((1, LANES), F32)
    rb = rb.at[0, :N_GROUPS].set(router_group_b[0]).at[0, N_GROUPS:N_GROUPS + N_EXPERTS].set(router_expert_b[0])
    h2, xn, logits = _merge(o_a.reshape(t, WIDTH), o_b.reshape(t, WIDTH), proj, x2d,
                            hg_up[0].astype(BF16), gd_up[0].astype(BF16), w_out[0].astype(BF16),
                            norm_ffn_g[0].reshape(1, d), rw_hi, rw_lo, rb, 256)

    row_tok, out_row, blk_expert, rweights, n_out_rows = _route(logits, t)
    y = _moe(row_tok, out_row, blk_expert, xn,
             w_gate[0].astype(BF16), w_up[0].astype(BF16), w_down[0].astype(BF16), n_out_rows)
    out = _combine(h2, rweights, final_norm_g.reshape(1, d), y, 256)
    return out.reshape(bsz, seq, d)
```

```python
import functools

import numpy as np
import jax
import jax.numpy as jnp
from jax import lax
from jax.experimental import pallas as pl
from jax.experimental.pallas import tpu as pltpu

F32 = jnp.float32
BF16 = jnp.bfloat16

D_MODEL = 1024
N_META = 16
CHUNK = 64
EPS = 1e-6
HEADS = 4
HEAD_DIM = 128
WIDTH = HEADS * HEAD_DIM
HEADS_PER_STACK = 2
STACK = HEADS_PER_STACK * CHUNK
N_STACKS = HEADS // HEADS_PER_STACK
CONV_W = 4
N_GROUPS = 4
EXPERTS_PER_GROUP = 8
N_EXPERTS = N_GROUPS * EXPERTS_PER_GROUP
D_FF = 512
N_BIG = 12 * WIDTH
LANES = 128
SUBLANES = 8
ROW_TILES = D_MODEL // LANES
MOE_ROWS = 256
PROJ_TM, PROJ_TN = 512, 2048
TOKEN_TM = 256
VMEM_LIMIT = 48 * 1024 * 1024


def _bdot(a, b):
    return lax.dot_general(a.astype(BF16), b.astype(BF16), (((2,), (1,)), ((0,), (0,))),
                           preferred_element_type=F32)


def _bdot_nt(a, b):
    return lax.dot_general(a.astype(BF16), b.astype(BF16), (((2,), (2,)), ((0,), (0,))),
                           preferred_element_type=F32)


def _bdot_tn(a, b):
    return lax.dot_general(a.astype(BF16), b.astype(BF16), (((1,), (1,)), ((0,), (0,))),
                           preferred_element_type=F32)


def _split3(x):
    hi = x.astype(BF16)
    r1 = x - hi.astype(F32)
    mid = r1.astype(BF16)
    lo = (r1 - mid.astype(F32)).astype(BF16)
    return hi, mid, lo


def _sigmoid(x):
    return 1.0 / (1.0 + jnp.exp(-x))


def _silu(x):
    return x * _sigmoid(x)


def _softplus(x):
    return jnp.maximum(x, 0.0) + jnp.log(1.0 + jnp.exp(-jnp.abs(x)))


def _stack_heads(x, bsz):
    x3 = x.reshape(bsz, CHUNK, WIDTH)
    y = jnp.concatenate([x3[:, :, h * HEAD_DIM:(h + 1) * HEAD_DIM] for h in range(HEADS)], axis=1)
    return y.reshape(bsz * HEADS * CHUNK, HEAD_DIM)


def _unstack_heads(x, bsz):
    x3 = x.reshape(bsz, HEADS * CHUNK, HEAD_DIM)
    return jnp.concatenate([x3[:, h * CHUNK:(h + 1) * CHUNK, :] for h in range(HEADS)], axis=2)


def _to_stacks(a, bsz):
    return a.reshape(bsz * N_STACKS, STACK, a.shape[-1])


def _to_heads(a, bsz):
    return a.reshape(bsz * HEADS, CHUNK, a.shape[-1])


def _roll_rows(x, shift):
    n = x.shape[0]
    return pltpu.roll(x, shift % n, axis=0)


def _load_chunk(is_meta, meta_ref, real_ref, bsz):
    x = jnp.where(is_meta, meta_ref[...][None], real_ref[...])
    return x.reshape(bsz * CHUNK, WIDTH)


def _read_row_tiles(ref):
    return jnp.concatenate([ref[:, s, :] for s in range(ROW_TILES)], axis=1)


def _write_row_tiles(ref, x):
    for s in range(ROW_TILES):
        ref[:, s, :] = x[:, s * LANES:(s + 1) * LANES]


def _norm_proj_kernel(x_ref, g_ref, lb_ref, w_ref, o_ref, *, activate):
    x = x_ref[...]
    xn = x * lax.rsqrt(jnp.mean(x * x, axis=-1, keepdims=True) + EPS) * g_ref[...]
    acc = jnp.dot(xn.astype(BF16), w_ref[...], preferred_element_type=F32)
    if not activate:
        o_ref[...] = acc
        return
    j = pl.program_id(0)
    w1, w2, w3 = WIDTH, 2 * WIDTH, 3 * WIDTH

    @pl.when(j == 0)
    def _():
        lb = lb_ref[...]
        o_ref[:, :w1] = _silu(acc[:, :w1])
        o_ref[:, w1:w2] = lb + (1.0 - lb) * _sigmoid(acc[:, w1:w2])
        o_ref[:, w2:w3] = acc[:, w2:w3]
        o_ref[:, w3:] = _silu(acc[:, w3:])

    @pl.when(j == 1)
    def _():
        o_ref[:, :w3] = acc[:, :w3]
        o_ref[:, w3:] = _silu(acc[:, w3:])

    @pl.when(j == 2)
    def _():
        o_ref[...] = _sigmoid(acc)


def _norm_proj(x2d, g, lb, w, tm, tn, activate):
    m, k = x2d.shape
    n = w.shape[1]
    assert not activate or (tn == 4 * WIDTH and n == 3 * tn)
    return pl.pallas_call(
        functools.partial(_norm_proj_kernel, activate=activate),
        grid=(n // tn, m // tm),
        in_specs=[pl.BlockSpec((tm, k), lambda j, i: (i, 0)),
                  pl.BlockSpec((1, k), lambda j, i: (0, 0)),
                  pl.BlockSpec((1, WIDTH), lambda j, i: (0, 0)),
                  pl.BlockSpec((k, tn), lambda j, i: (0, j))],
        out_specs=pl.BlockSpec((tm, tn), lambda j, i: (i, j)),
        out_shape=jax.ShapeDtypeStruct((m, n), F32),
        compiler_params=pltpu.CompilerParams(
            dimension_semantics=("arbitrary", "arbitrary"), vmem_limit_bytes=VMEM_LIMIT),
        name="norm_proj",
    )(x2d, g, lb, w)


HGRN_LEVELS = 6
HGRN_DIAG = HGRN_LEVELS


def _hgrn_level_table():
    r = np.arange(STACK)
    t, hd = r % CHUNK, r // CHUNK
    x = t[:, None] ^ t[None, :]
    lv = np.floor(np.log2(np.maximum(x, 1))).astype(np.int32)
    valid = (hd[:, None] == hd[None, :]) & (t[:, None] > t[None, :])
    lv = np.where(valid, lv, -1)
    lv = np.where(r[:, None] == r[None, :], HGRN_DIAG, lv)
    return jnp.asarray(lv, dtype=jnp.int32)


def _hgrn_kernel(lv_ref, g_ref, mq_ref, mf_ref, mi_ref, mg_ref,
                 rq_ref, rf_ref, ri_ref, rg_ref, o_ref, st_ref, *, bsz):
    c = pl.program_id(0)

    @pl.when(c == 0)
    def _():
        st_ref[...] = jnp.zeros_like(st_ref)

    is_meta = c == 0
    q = _load_chunk(is_meta, mq_ref, rq_ref, bsz)
    f = _load_chunk(is_meta, mf_ref, rf_ref, bsz)
    v = _load_chunk(is_meta, mi_ref, ri_ref, bsz)
    gate = _load_chunk(is_meta, mg_ref, rg_ref, bsz)

    trow = lax.broadcasted_iota(jnp.int32, (bsz * CHUNK, WIDTH), 0) & (CHUNK - 1)
    b = jnp.log(f)
    s = 1
    while s < CHUNK:
        b = b + jnp.where(trow >= s, _roll_rows(b, s), 0.0)
        s *= 2

    qs, ks, vs, bs = (_stack_heads(a, bsz) for a in (q, 1.0 - f, v, b))
    t = lax.broadcasted_iota(jnp.int32, qs.shape, 0) & (CHUNK - 1)
    lv = lv_ref[...][None]
    to3 = functools.partial(_to_stacks, bsz=bsz)

    attn = jnp.where(lv == HGRN_DIAG, _bdot_nt(to3(qs), to3(ks)), 0.0)
    bref = bs
    bnext = _roll_rows(bs, -1)
    for li in range(HGRN_LEVELS):
        m = 1 << li
        if li > 0:
            half = m // 2
            upper = (t & (m - 1)) >= half
            bref = jnp.where(upper, _roll_rows(bref, half), bref)
            bnext = jnp.where(upper, bnext, _roll_rows(bnext, -half))
        qt = qs * jnp.exp(bs - bref)
        kt = ks * jnp.exp(jnp.minimum(bnext - bs, 0.0))
        attn = jnp.where(lv == li, _bdot_nt(to3(qt), to3(kt)), attn)

    o = _bdot(attn, to3(vs))

    to_g = functools.partial(_to_heads, bsz=bsz)
    st = st_ref[...]
    bs_g = to_g(bs)
    o_inter = _bdot_nt(to_g(qs * jnp.exp(bs)), st)
    b_end = bs_g[:, CHUNK - 1:CHUNK, :]
    ke = to_g(ks) * jnp.exp(b_end - bs_g)
    st_ref[...] = st * jnp.exp(b_end) + _bdot_tn(to_g(vs), ke)
    o = (to_g(o) + o_inter).reshape(qs.shape)

    o = o * lax.rsqrt(jnp.mean(o * o, axis=-1, keepdims=True) + EPS) * g_ref[...]
    o = o * _stack_heads(gate, bsz)
    o_ref[...] = _unstack_heads(o, bsz).astype(o_ref.dtype)


def _chunk_specs(bsz, col0):
    meta = [pl.BlockSpec((CHUNK, WIDTH), functools.partial(lambda c, j: (0, j), j=col0 + j))
            for j in range(4)]
    real = [pl.BlockSpec((bsz, CHUNK, WIDTH),
                         functools.partial(lambda c, j: (0, jnp.maximum(c - 1, 0), j), j=col0 + j))
            for j in range(4)]
    return meta + real


def _hgrn(proj_meta, proj_real, norm_g):
    bsz, seq, _ = proj_real.shape
    n_chunks = seq // CHUNK + 1
    const = lambda c: (0, 0)
    return pl.pallas_call(
        functools.partial(_hgrn_kernel, bsz=bsz),
        grid=(n_chunks,),
        in_specs=[pl.BlockSpec((STACK, STACK), const),
                  pl.BlockSpec((1, HEAD_DIM), const)] + _chunk_specs(bsz, 0),
        out_specs=pl.BlockSpec((bsz, CHUNK, WIDTH), lambda c: (0, jnp.maximum(c - 1, 0), 0)),
        out_shape=jax.ShapeDtypeStruct((bsz, seq, WIDTH), BF16),
        scratch_shapes=[pltpu.VMEM((bsz * HEADS, HEAD_DIM, HEAD_DIM), F32)],
        compiler_params=pltpu.CompilerParams(
            dimension_semantics=("arbitrary",), vmem_limit_bytes=VMEM_LIMIT),
        name="hgrn",
    )(_hgrn_level_table(), norm_g, *([proj_meta] * 4), *([proj_real] * 4))


def _gdn_cumsum_matrix():
    r = np.arange(STACK)
    u = (r[:, None] // CHUNK == r[None, :] // CHUNK) & (r[:, None] <= r[None, :])
    return jnp.asarray(u, dtype=BF16)


def _gdn_kernel(u_ref, alog_ref, dt_ref, cw_ref, g_ref, s_ref,
                mq_ref, mk_ref, mv_ref, mz_ref, rq_ref, rk_ref, rv_ref, rz_ref,
                o_ref, st_ref, prev_ref, *, bsz):
    c = pl.program_id(0)

    @pl.when(c == 0)
    def _():
        st_ref[...] = jnp.zeros_like(st_ref)
        prev_ref[...] = jnp.zeros_like(prev_ref)

    is_meta = c == 0
    xq = _load_chunk(is_meta, mq_ref, rq_ref, bsz)
    xk = _load_chunk(is_meta, mk_ref, rk_ref, bsz)
    xv = _load_chunk(is_meta, mv_ref, rv_ref, bsz)
    gate = _load_chunk(is_meta, mz_ref, rz_ref, bsz)

    x = jnp.concatenate([xq, xk, xv], axis=1)
    pv = prev_ref[...]
    trow = lax.broadcasted_iota(jnp.int32, x.shape, 0) & (CHUNK - 1)
    cw = cw_ref[...]
    acc = x * cw[CONV_W - 1:CONV_W, :]
    for d in range(1, CONV_W):
        xs = jnp.where(trow >= d, _roll_rows(x, d), _roll_rows(pv, d - CHUNK))
        acc = acc + xs * cw[CONV_W - 1 - d:CONV_W - d, :]
    prev_ref[...] = x
    y = _silu(acc)

    qs = _stack_heads(y[:, :WIDTH], bsz)
    ks = _stack_heads(y[:, WIDTH:2 * WIDTH], bsz)
    vs = _stack_heads(y[:, 2 * WIDTH:], bsz)
    qs = qs * lax.rsqrt(jnp.sum(qs * qs, axis=-1, keepdims=True) + EPS) * (HEAD_DIM ** -0.5)
    ks = ks * lax.rsqrt(jnp.sum(ks * ks, axis=-1, keepdims=True) + EPS)
    to3 = functools.partial(_to_stacks, bsz=bsz)
    qs, ks, vs = to3(qs), to3(ks), to3(vs)
    n_st = bsz * N_STACKS

    srow = s_ref[...].reshape(n_st, 2, STACK)
    per_stack = lambda ref: jnp.broadcast_to(ref[...][None], (bsz, N_STACKS, 1, STACK)).reshape(n_st, 1, STACK)
    beta_row = _sigmoid(srow[:, 0:1, :])
    g_row = -jnp.exp(per_stack(alog_ref)) * _softplus(srow[:, 1:2, :] + per_stack(dt_ref))
    n8 = n_st * SUBLANES
    g8 = jnp.broadcast_to(g_row, (n_st, SUBLANES, STACK)).reshape(n8, STACK)
    pieces = jnp.concatenate(_split3(g8), axis=0)
    cum = jnp.dot(pieces, u_ref[...], preferred_element_type=F32)
    cum = cum[0:n8] + cum[n8:2 * n8] + cum[2 * n8:]
    cum_row = cum.reshape(n_st, SUBLANES, STACK)[:, 0:1, :]

    r = lax.broadcasted_iota(jnp.int32, (STACK, STACK), 0)
    cc = lax.broadcasted_iota(jnp.int32, (STACK, STACK), 1)
    eye = (r == cc)[None]
    cum_col = jnp.sum(jnp.where(eye, cum_row, 0.0), axis=2, keepdims=True)
    beta_col = jnp.sum(jnp.where(eye, beta_row, 0.0), axis=2, keepdims=True)
    same = (r >> 6) == (cc >> 6)
    lower = (same & (r >= cc))[None]
    strict = (same & (r > cc))[None]
    ratio = jnp.exp(jnp.minimum(cum_col - cum_row, 0.0))

    kkt = _bdot_nt(ks, ks)
    qkt = _bdot_nt(qs, ks)
    lm = jnp.where(strict, beta_col * ratio * kkt, 0.0)

    ident = eye.astype(F32)
    l0 = jnp.where(((r >> 3) == (cc >> 3))[None], lm, 0.0)
    l2 = _bdot(l0, l0)
    l4 = _bdot(l2, l2)
    xinv = _bdot(_bdot(ident - l0, ident + l2), ident + l4)
    for lvl in (3, 4, 5):
        off = (((r >> (lvl + 1)) == (cc >> (lvl + 1))) & ((r >> lvl) != (cc >> lvl)))[None]
        xinv = xinv - _bdot(_bdot(xinv, jnp.where(off, lm, 0.0)), xinv)

    decay_col = jnp.exp(cum_col)
    rhs = jnp.concatenate([beta_col * vs, (beta_col * decay_col) * ks], axis=2)
    sol = _bdot(xinv, rhs)

    to_g = functools.partial(_to_heads, bsz=bsz)
    st = st_ref[...]
    u = to_g(sol[:, :, :HEAD_DIM]) - _bdot(to_g(sol[:, :, HEAD_DIM:]), st)
    o_inter = to_g(decay_col) * _bdot(to_g(qs), st)
    cum_g = to_g(cum_col)
    cum_end = cum_g[:, CHUNK - 1:CHUNK, :]
    kdec = to_g(ks) * jnp.exp(cum_end - cum_g)
    st_ref[...] = jnp.exp(cum_end) * st + _bdot_tn(kdec, u)
    attn = jnp.where(lower, qkt * ratio, 0.0)
    o = o_inter + to_g(_bdot(attn, to3(u)))
    o = o.reshape(bsz * HEADS * CHUNK, HEAD_DIM)

    o = o * lax.rsqrt(jnp.mean(o * o, axis=-1, keepdims=True) + EPS) * g_ref[...]
    o = o * _stack_heads(gate, bsz)
    o_ref[...] = _unstack_heads(o, bsz).astype(o_ref.dtype)


def _gdn(proj_meta, proj_real, scal, a_log_row, dt_row, conv_w, norm_g):
    bsz, seq, _ = proj_real.shape
    n_chunks = seq // CHUNK + 1
    const = lambda c: (0, 0)
    const3 = lambda c: (0, 0, 0)
    return pl.pallas_call(
        functools.partial(_gdn_kernel, bsz=bsz),
        grid=(n_chunks,),
        in_specs=[pl.BlockSpec((STACK, STACK), const),
                  pl.BlockSpec((N_STACKS, 1, STACK), const3),
                  pl.BlockSpec((N_STACKS, 1, STACK), const3),
                  pl.BlockSpec((CONV_W, 3 * WIDTH), const),
                  pl.BlockSpec((1, HEAD_DIM), const),
                  pl.BlockSpec((bsz, None, N_STACKS, 2, STACK), lambda c: (0, c, 0, 0, 0))]
                 + _chunk_specs(bsz, 4),
        out_specs=pl.BlockSpec((bsz, CHUNK, WIDTH), lambda c: (0, jnp.maximum(c - 1, 0), 0)),
        out_shape=jax.ShapeDtypeStruct((bsz, seq, WIDTH), BF16),
        scratch_shapes=[pltpu.VMEM((bsz * HEADS, HEAD_DIM, HEAD_DIM), F32),
                        pltpu.VMEM((bsz * CHUNK, 3 * WIDTH), F32)],
        compiler_params=pltpu.CompilerParams(
            dimension_semantics=("arbitrary",), vmem_limit_bytes=VMEM_LIMIT),
        name="gdn",
    )(_gdn_cumsum_matrix(), a_log_row, dt_row, conv_w, norm_g, scal,
      *([proj_meta] * 4), *([proj_real] * 4))


def _merge_kernel(oa_ref, ob_ref, ga_ref, gb_ref, x_ref, hgup_ref, gdup_ref, wout_ref,
                  ng_ref, rwh_ref, rwl_ref, rb_ref, h2_ref, xn_ref, lg_ref):
    ua = jnp.dot(oa_ref[...], hgup_ref[...], preferred_element_type=F32)
    ub = jnp.dot(ob_ref[...], gdup_ref[...], preferred_element_type=F32)
    merged = ga_ref[...] * ua + gb_ref[...] * ub
    h2 = x_ref[...] + jnp.dot(merged.astype(BF16), wout_ref[...], preferred_element_type=F32)
    h2_ref[...] = h2
    xn = h2 * lax.rsqrt(jnp.mean(h2 * h2, axis=-1, keepdims=True) + EPS) * ng_ref[...]
    _write_row_tiles(xn_ref, xn)
    xh = xn.astype(BF16)
    xl = (xn - xh.astype(F32)).astype(BF16)
    wh, wl = rwh_ref[...], rwl_ref[...]
    lg_ref[...] = (jnp.dot(xh, wh, preferred_element_type=F32)
                   + jnp.dot(xh, wl, preferred_element_type=F32)
                   + jnp.dot(xl, wh, preferred_element_type=F32)) + rb_ref[...]


def _merge(o_a, o_b, proj, x2d, hg_up, gd_up, w_out, norm_g, rw_hi, rw_lo, rb):
    t = x2d.shape[0]
    tm = TOKEN_TM
    row = lambda i: (i, 0)
    const = lambda i: (0, 0)
    return pl.pallas_call(
        _merge_kernel,
        grid=(t // tm,),
        in_specs=[pl.BlockSpec((tm, WIDTH), row),
                  pl.BlockSpec((tm, WIDTH), row),
                  pl.BlockSpec((tm, D_MODEL), lambda i: (i, 4)),
                  pl.BlockSpec((tm, D_MODEL), lambda i: (i, 5)),
                  pl.BlockSpec((tm, D_MODEL), row),
                  pl.BlockSpec((WIDTH, D_MODEL), const),
                  pl.BlockSpec((WIDTH, D_MODEL), const),
                  pl.BlockSpec((D_MODEL, D_MODEL), const),
                  pl.BlockSpec((1, D_MODEL), const),
                  pl.BlockSpec((D_MODEL, LANES), const),
                  pl.BlockSpec((D_MODEL, LANES), const),
                  pl.BlockSpec((1, LANES), const)],
        out_specs=[pl.BlockSpec((tm, D_MODEL), row),
                   pl.BlockSpec((tm, ROW_TILES, LANES), lambda i: (i, 0, 0)),
                   pl.BlockSpec((tm, LANES), row)],
        out_shape=[jax.ShapeDtypeStruct((t, D_MODEL), F32),
                   jax.ShapeDtypeStruct((t, ROW_TILES, LANES), F32),
                   jax.ShapeDtypeStruct((t, LANES), F32)],
        compiler_params=pltpu.CompilerParams(
            dimension_semantics=("arbitrary",), vmem_limit_bytes=VMEM_LIMIT),
        name="merge",
    )(o_a, o_b, proj, proj, x2d, hg_up, gd_up, w_out, norm_g, rw_hi, rw_lo, rb)


def _moe_kernel(rt_ref, orow_ref, be_ref, x_hbm, wg_ref, wu_ref, wd_ref, y_hbm,
                xbuf0, xbuf1, ybuf0, ybuf1, sem_in, sem_out):
    i = pl.program_id(0)
    n = pl.num_programs(0)
    xbufs, ybufs = (xbuf0, xbuf1), (ybuf0, ybuf1)

    def gather_row(slot, blk, k):
        tok = rt_ref[blk * MOE_ROWS + k]
        return pltpu.make_async_copy(x_hbm.at[tok], xbufs[slot].at[k], sem_in.at[slot])

    def scatter_row(slot, blk, k):
        dst = orow_ref[blk * MOE_ROWS + k]
        return pltpu.make_async_copy(ybufs[slot].at[k], y_hbm.at[dst], sem_out.at[slot])

    def wait_gather(slot):
        pltpu.make_async_copy(x_hbm.at[pl.ds(0, MOE_ROWS)], xbufs[slot], sem_in.at[slot]).wait()

    def wait_scatter(slot):
        pltpu.make_async_copy(ybufs[slot], y_hbm.at[pl.ds(0, MOE_ROWS)], sem_out.at[slot]).wait()

    @pl.when(i == 0)
    def _():
        for k in range(MOE_ROWS):
            gather_row(0, 0, k).start()

    def step(slot):
        wait_gather(slot)

        @pl.when(i >= 2)
        def _():
            wait_scatter(slot)

        for k in range(MOE_ROWS):
            gather_row(1 - slot, i + 1, k).start()
        xb = _read_row_tiles(xbufs[slot]).astype(BF16)
        a = jnp.dot(xb, wg_ref[...], preferred_element_type=F32)
        u = jnp.dot(xb, wu_ref[...], preferred_element_type=F32)
        y = jnp.dot((_silu(a) * u).astype(BF16), wd_ref[...], preferred_element_type=F32)
        _write_row_tiles(ybufs[slot], y)
        for k in range(MOE_ROWS):
            scatter_row(slot, i, k).start()

        @pl.when(i == n - 1)
        def _():
            wait_gather(1 - slot)
            wait_scatter(slot)

            @pl.when(i >= 1)
            def _():
                wait_scatter(1 - slot)

    @pl.when(i % 2 == 0)
    def _():
        step(0)

    @pl.when(i % 2 == 1)
    def _():
        step(1)


def _moe(row_tok, out_row, blk_expert, xn, w_gate, w_up, w_down, n_out_rows):
    n_blocks = blk_expert.shape[0]
    wspec = lambda shape: pl.BlockSpec((None,) + shape, lambda i, rt, orow, be: (be[i], 0, 0))
    row_buf = pltpu.VMEM((MOE_ROWS, ROW_TILES, LANES), F32)
    grid_spec = pltpu.PrefetchScalarGridSpec(
        num_scalar_prefetch=3,
        grid=(n_blocks,),
        in_specs=[pl.BlockSpec(memory_space=pl.ANY),
                  wspec((D_MODEL, D_FF)), wspec((D_MODEL, D_FF)), wspec((D_FF, D_MODEL))],
        out_specs=pl.BlockSpec(memory_space=pl.ANY),
        scratch_shapes=[row_buf, row_buf, row_buf, row_buf,
                        pltpu.SemaphoreType.DMA((2,)), pltpu.SemaphoreType.DMA((2,))],
    )
    return pl.pallas_call(
        _moe_kernel,
        grid_spec=grid_spec,
        out_shape=jax.ShapeDtypeStruct((n_out_rows, ROW_TILES, LANES), F32),
        compiler_params=pltpu.CompilerParams(
            dimension_semantics=("arbitrary",), vmem_limit_bytes=VMEM_LIMIT),
        name="moe",
    )(row_tok, out_row, blk_expert, xn, w_gate, w_up, w_down)


def _combine_kernel(h2_ref, rw_ref, g_ref, y0_ref, y1_ref, o_ref):
    rw = rw_ref[...]
    h = h2_ref[...] + rw[:, 0:1] * _read_row_tiles(y0_ref) + rw[:, 1:2] * _read_row_tiles(y1_ref)
    o_ref[...] = h * lax.rsqrt(jnp.mean(h * h, axis=-1, keepdims=True) + EPS) * g_ref[...]


def _combine(h2, rweights, final_g, y):
    t = h2.shape[0]
    tm = TOKEN_TM
    row = lambda i: (i, 0)
    return pl.pallas_call(
        _combine_kernel,
        grid=(t // tm,),
        in_specs=[pl.BlockSpec((tm, D_MODEL), row),
                  pl.BlockSpec((tm, LANES), row),
                  pl.BlockSpec((1, D_MODEL), lambda i: (0, 0)),
                  pl.BlockSpec((tm, ROW_TILES, LANES), lambda i: (i, 0, 0)),
                  pl.BlockSpec((tm, ROW_TILES, LANES), lambda i: (i + t // tm, 0, 0))],
        out_specs=pl.BlockSpec((tm, D_MODEL), row),
        out_shape=jax.ShapeDtypeStruct((t, D_MODEL), F32),
        compiler_params=pltpu.CompilerParams(
            dimension_semantics=("arbitrary",), vmem_limit_bytes=VMEM_LIMIT),
        name="combine",
    )(h2, rweights, final_g, y, y)


def _route(logits, t):
    gl = logits[:, :N_GROUPS]
    el = logits[:, N_GROUPS:N_GROUPS + N_EXPERTS].reshape(t, N_GROUPS, EXPERTS_PER_GROUP)
    g_prob = jax.nn.softmax(gl, axis=-1)
    grp = jnp.argmax(gl, axis=-1)
    p_grp = jnp.take_along_axis(g_prob, grp[:, None], axis=-1)
    sel = jnp.take_along_axis(el, grp[:, None, None], axis=1)[:, 0]
    top_p, top_i = lax.top_k(jax.nn.softmax(sel, axis=-1), 2)
    weights = p_grp * top_p / jnp.sum(top_p, axis=-1, keepdims=True)
    e_flat = (grp[:, None] * EXPERTS_PER_GROUP + top_i).reshape(2 * t).astype(jnp.int32)

    a = 2 * t
    n_blocks = a // MOE_ROWS + N_EXPERTS
    experts = jnp.arange(N_EXPERTS, dtype=jnp.int32)
    order = jnp.argsort(e_flat).astype(jnp.int32)
    counts = jnp.sum((e_flat[:, None] == experts[None, :]).astype(jnp.int32), axis=0)
    start = jnp.cumsum(counts) - counts
    padded = ((counts + MOE_ROWS - 1) // MOE_ROWS) * MOE_ROWS
    pend = jnp.cumsum(padded)
    pstart = pend - padded
    blk_first = jnp.arange(n_blocks, dtype=jnp.int32) * MOE_ROWS
    blk_expert = jnp.minimum(jnp.sum((pend[None, :] <= blk_first[:, None]).astype(jnp.int32), axis=1),
                             N_EXPERTS - 1).astype(jnp.int32)
    p = jnp.arange(n_blocks * MOE_ROWS, dtype=jnp.int32)
    e_p = jnp.repeat(blk_expert, MOE_ROWS)
    j = p - pstart[e_p]
    valid = j < counts[e_p]
    src = order[jnp.clip(start[e_p] + j, 0, a - 1)]
    row_tok = jnp.where(valid, src // 2, 0)
    out_row = jnp.where(valid, (src % 2) * t + src // 2, a + p - (start[e_p] + counts[e_p]))
    row_tok = jnp.concatenate([row_tok, jnp.zeros((MOE_ROWS,), jnp.int32)])
    rweights = jnp.zeros((t, LANES), F32).at[:, :2].set(weights)
    return row_tok, out_row, blk_expert, rweights, n_blocks * MOE_ROWS


def kernel(x, meta_tokens, hg_lb_logits, norm_mix_g, w_in, gd_conv_w, gd_A_log, gd_dt_bias, hg_norm_g, gd_norm_g, hg_up, gd_up, w_out, norm_ffn_g, router_group_w, router_group_b, router_expert_w, router_expert_b, w_gate, w_up, w_down, final_norm_g):
    bsz, seq, d = x.shape
    t = bsz * seq
    x2d = x.reshape(t, d)

    lb = jnp.cumsum(jax.nn.softmax(hg_lb_logits.astype(F32), axis=0), axis=0)[0].reshape(1, WIDTH)
    w = w_in[0]
    n_qkvz = 8 * WIDTH
    w_big = jnp.concatenate([w[:, :n_qkvz], w[:, n_qkvz + 2 * HEADS:]], axis=1).astype(BF16)
    w_small = jnp.zeros((d, LANES), F32).at[:, :2 * HEADS].set(w[:, n_qkvz:n_qkvz + 2 * HEADS]).astype(BF16)
    g_mix = norm_mix_g[0].reshape(1, d)
    meta_blk = jnp.concatenate([jnp.zeros((CHUNK - N_META, d), F32), meta_tokens.astype(F32)], axis=0)

    proj = _norm_proj(x2d, g_mix, lb, w_big, PROJ_TM, PROJ_TN, True)
    small = _norm_proj(x2d, g_mix, lb, w_small, PROJ_TM, LANES, False)
    proj_meta = _norm_proj(meta_blk, g_mix, lb, w_big, CHUNK, PROJ_TN, True)
    small_meta = _norm_proj(meta_blk, g_mix, lb, w_small, CHUNK, LANES, False)

    proj3 = proj.reshape(bsz, seq, N_BIG)
    o_a = _hgrn(proj_meta, proj3, hg_norm_g[0].reshape(1, HEAD_DIM))

    def scalar_rows(s, nb, nc):
        s = s[:, :2 * HEADS].reshape(nb, nc, CHUNK, 2, N_STACKS, HEADS_PER_STACK)
        return s.transpose(0, 1, 4, 3, 5, 2).reshape(nb, nc, N_STACKS, 2, STACK)

    s_meta = jnp.broadcast_to(scalar_rows(small_meta, 1, 1), (bsz, 1, N_STACKS, 2, STACK))
    scal = jnp.concatenate([s_meta, scalar_rows(small, bsz, seq // CHUNK)], axis=1)
    a_log_row = jnp.repeat(gd_A_log[0].astype(F32), CHUNK).reshape(N_STACKS, 1, STACK)
    dt_row = jnp.repeat(gd_dt_bias[0].astype(F32), CHUNK).reshape(N_STACKS, 1, STACK)
    o_b = _gdn(proj_meta, proj3, scal, a_log_row, dt_row, gd_conv_w[0].astype(F32),
               gd_norm_g[0].reshape(1, HEAD_DIM))

    rw = jnp.zeros((d, LANES), F32)
    rw = rw.at[:, :N_GROUPS].set(router_group_w[0]).at[:, N_GROUPS:N_GROUPS + N_EXPERTS].set(router_expert_w[0])
    rw_hi = rw.astype(BF16)
    rw_lo = (rw - rw_hi.astype(F32)).astype(BF16)
    rb = jnp.zeros((1, LANES), F32)
    rb = rb.at[0, :N_GROUPS].set(router_group_b[0]).at[0, N_GROUPS:N_GROUPS + N_EXPERTS].set(router_expert_b[0])
    h2, xn, logits = _merge(o_a.reshape(t, WIDTH), o_b.reshape(t, WIDTH), proj, x2d,
                            hg_up[0].astype(BF16), gd_up[0].astype(BF16), w_out[0].astype(BF16),
                            norm_ffn_g[0].reshape(1, d), rw_hi, rw_lo, rb)

    row_tok, out_row, blk_expert, rweights, n_out_rows = _route(logits, t)
    y = _moe(row_tok, out_row, blk_expert, xn,
             w_gate[0].astype(BF16), w_up[0].astype(BF16), w_down[0].astype(BF16), n_out_rows)
    out = _combine(h2, rweights, final_norm_g.reshape(1, d), y)
    return out.reshape(bsz, seq, d)
```

```python
import functools

import numpy as np
import jax
import jax.numpy as jnp
from jax import lax
from jax.experimental import pallas as pl
from jax.experimental.pallas import tpu as pltpu

F32 = jnp.float32
BF16 = jnp.bfloat16

D_MODEL = 1024
N_META = 16
CHUNK = 64
EPS = 1e-6
HEADS = 4
HEAD_DIM = 128
WIDTH = HEADS * HEAD_DIM
HEADS_PER_STACK = 2
STACK = HEADS_PER_STACK * CHUNK
N_STACKS = HEADS // HEADS_PER_STACK
CONV_W = 4
N_GROUPS = 4
EXPERTS_PER_GROUP = 8
N_EXPERTS = N_GROUPS * EXPERTS_PER_GROUP
D_FF = 512
N_BIG = 12 * WIDTH
LANES = 128
SUBLANES = 8
ROW_TILES = D_MODEL // LANES
MOE_ROWS = 256
PROJ_TM, PROJ_TN = 512, 2048
TOKEN_TM = 256
VMEM_LIMIT = 48 * 1024 * 1024


def _bdot(a, b):
    return lax.dot_general(a.astype(BF16), b.astype(BF16), (((2,), (1,)), ((0,), (0,))),
                           preferred_element_type=F32)


def _bdot_nt(a, b):
    return lax.dot_general(a.astype(BF16), b.astype(BF16), (((2,), (2,)), ((0,), (0,))),
                           preferred_element_type=F32)


def _bdot_tn(a, b):
    return lax.dot_general(a.astype(BF16), b.astype(BF16), (((1,), (1,)), ((0,), (0,))),
                           preferred_element_type=F32)


def _split3(x):
    hi = x.astype(BF16)
    r1 = x - hi.astype(F32)
    mid = r1.astype(BF16)
    lo = (r1 - mid.astype(F32)).astype(BF16)
    return hi, mid, lo


def _sigmoid(x):
    return 1.0 / (1.0 + jnp.exp(-x))


def _silu(x):
    return x * _sigmoid(x)


def _softplus(x):
    return jnp.maximum(x, 0.0) + jnp.log(1.0 + jnp.exp(-jnp.abs(x)))


def _stack_heads(x, bsz):
    x3 = x.reshape(bsz, CHUNK, WIDTH)
    y = jnp.concatenate([x3[:, :, h * HEAD_DIM:(h + 1) * HEAD_DIM] for h in range(HEADS)], axis=1)
    return y.reshape(bsz * HEADS * CHUNK, HEAD_DIM)


def _unstack_heads(x, bsz):
    x3 = x.reshape(bsz, HEADS * CHUNK, HEAD_DIM)
    return jnp.concatenate([x3[:, h * CHUNK:(h + 1) * CHUNK, :] for h in range(HEADS)], axis=2)


def _to_stacks(a, bsz):
    return a.reshape(bsz * N_STACKS, STACK, a.shape[-1])


def _to_heads(a, bsz):
    return a.reshape(bsz * HEADS, CHUNK, a.shape[-1])


def _roll_rows(x, shift):
    n = x.shape[0]
    return pltpu.roll(x, shift % n, axis=0)


def _load_chunk(is_meta, meta_ref, real_ref, bsz):
    x = jnp.where(is_meta, meta_ref[...][None], real_ref[...])
    return x.reshape(bsz * CHUNK, WIDTH)


def _read_row_tiles(ref):
    return jnp.concatenate([ref[:, s, :] for s in range(ROW_TILES)], axis=1)


def _write_row_tiles(ref, x):
    for s in range(ROW_TILES):
        ref[:, s, :] = x[:, s * LANES:(s + 1) * LANES]


def _norm_proj_kernel(x_ref, g_ref, lb_ref, w_ref, o_ref, wbf_ref, *, group):
    @pl.when(pl.program_id(0) == 0)
    def _():
        wbf_ref[...] = w_ref[...].astype(BF16)

    x = x_ref[...]
    xn = x * lax.rsqrt(jnp.mean(x * x, axis=-1, keepdims=True) + EPS) * g_ref[...]
    acc = jnp.dot(xn.astype(BF16), wbf_ref[...], preferred_element_type=F32)
    w1, w2, w3, w4 = WIDTH, 2 * WIDTH, 3 * WIDTH, 4 * WIDTH
    if group == "hgrn":
        lb = lb_ref[...]
        o_ref[:, :w1] = _silu(acc[:, :w1])
        o_ref[:, w1:w2] = lb + (1.0 - lb) * _sigmoid(acc[:, w1:w2])
        o_ref[:, w2:w3] = acc[:, w2:w3]
        o_ref[:, w3:] = _silu(acc[:, w3:])
    elif group == "gdn":
        o_ref[:, :w3] = acc[:, :w3]
        o_ref[:, w3:w4] = _silu(acc[:, w3:w4])
        o_ref[:, w4:] = acc[:, w4:]
    else:
        o_ref[...] = _sigmoid(acc)


def _norm_proj(x2d, g, lb, w, tm, group):
    m, k = x2d.shape
    n = w.shape[1]
    const = lambda i: (0, 0)
    return pl.pallas_call(
        functools.partial(_norm_proj_kernel, group=group),
        grid=(m // tm,),
        in_specs=[pl.BlockSpec((tm, k), lambda i: (i, 0)),
                  pl.BlockSpec((1, k), const),
                  pl.BlockSpec((1, WIDTH), const),
                  pl.BlockSpec((k, n), const)],
        out_specs=pl.BlockSpec((tm, n), lambda i: (i, 0)),
        out_shape=jax.ShapeDtypeStruct((m, n), F32),
        scratch_shapes=[pltpu.VMEM((k, n), BF16)],
        compiler_params=pltpu.CompilerParams(
            dimension_semantics=("arbitrary",), vmem_limit_bytes=VMEM_LIMIT),
        name="norm_proj_" + group,
    )(x2d, g, lb, w)


HGRN_LEVELS = 6
HGRN_DIAG = HGRN_LEVELS


def _hgrn_level_table():
    r = np.arange(STACK)
    t, hd = r % CHUNK, r // CHUNK
    x = t[:, None] ^ t[None, :]
    lv = np.floor(np.log2(np.maximum(x, 1))).astype(np.int32)
    valid = (hd[:, None] == hd[None, :]) & (t[:, None] > t[None, :])
    lv = np.where(valid, lv, -1)
    lv = np.where(r[:, None] == r[None, :], HGRN_DIAG, lv)
    return jnp.asarray(lv, dtype=jnp.int32)


def _hgrn_kernel(lv_ref, g_ref, mq_ref, mf_ref, mi_ref, mg_ref,
                 rq_ref, rf_ref, ri_ref, rg_ref, o_ref, st_ref, *, bsz):
    c = pl.program_id(0)

    @pl.when(c == 0)
    def _():
        st_ref[...] = jnp.zeros_like(st_ref)

    is_meta = c == 0
    q = _load_chunk(is_meta, mq_ref, rq_ref, bsz)
    f = _load_chunk(is_meta, mf_ref, rf_ref, bsz)
    v = _load_chunk(is_meta, mi_ref, ri_ref, bsz)
    gate = _load_chunk(is_meta, mg_ref, rg_ref, bsz)

    trow = lax.broadcasted_iota(jnp.int32, (bsz * CHUNK, WIDTH), 0) & (CHUNK - 1)
    b = jnp.log(f)
    s = 1
    while s < CHUNK:
        b = b + jnp.where(trow >= s, _roll_rows(b, s), 0.0)
        s *= 2

    qs, ks, vs, bs = (_stack_heads(a, bsz) for a in (q, 1.0 - f, v, b))
    t = lax.broadcasted_iota(jnp.int32, qs.shape, 0) & (CHUNK - 1)
    lv = lv_ref[...][None]
    to3 = functools.partial(_to_stacks, bsz=bsz)

    attn = jnp.where(lv == HGRN_DIAG, _bdot_nt(to3(qs), to3(ks)), 0.0)
    bref = bs
    bnext = _roll_rows(bs, -1)
    for li in range(HGRN_LEVELS):
        m = 1 << li
        if li > 0:
            half = m // 2
            upper = (t & (m - 1)) >= half
            bref = jnp.where(upper, _roll_rows(bref, half), bref)
            bnext = jnp.where(upper, bnext, _roll_rows(bnext, -half))
        qt = qs * jnp.exp(bs - bref)
        kt = ks * jnp.exp(jnp.minimum(bnext - bs, 0.0))
        attn = jnp.where(lv == li, _bdot_nt(to3(qt), to3(kt)), attn)

    o = _bdot(attn, to3(vs))

    to_g = functools.partial(_to_heads, bsz=bsz)
    st = st_ref[...]
    bs_g = to_g(bs)
    o_inter = _bdot_nt(to_g(qs * jnp.exp(bs)), st)
    b_end = bs_g[:, CHUNK - 1:CHUNK, :]
    ke = to_g(ks) * jnp.exp(b_end - bs_g)
    st_ref[...] = st * jnp.exp(b_end) + _bdot_tn(to_g(vs), ke)
    o = (to_g(o) + o_inter).reshape(qs.shape)

    o = o * lax.rsqrt(jnp.mean(o * o, axis=-1, keepdims=True) + EPS) * g_ref[...]
    o = o * _stack_heads(gate, bsz)
    o_ref[...] = _unstack_heads(o, bsz).astype(o_ref.dtype)


def _chunk_specs(bsz, col0):
    meta = [pl.BlockSpec((CHUNK, WIDTH), functools.partial(lambda c, j: (0, j), j=col0 + j))
            for j in range(4)]
    real = [pl.BlockSpec((bsz, CHUNK, WIDTH),
                         functools.partial(lambda c, j: (0, jnp.maximum(c - 1, 0), j), j=col0 + j))
            for j in range(4)]
    return meta + real


def _hgrn(proj_meta, proj_real, norm_g):
    bsz, seq, _ = proj_real.shape
    n_chunks = seq // CHUNK + 1
    const = lambda c: (0, 0)
    return pl.pallas_call(
        functools.partial(_hgrn_kernel, bsz=bsz),
        grid=(n_chunks,),
        in_specs=[pl.BlockSpec((STACK, STACK), const),
                  pl.BlockSpec((1, HEAD_DIM), const)] + _chunk_specs(bsz, 0),
        out_specs=pl.BlockSpec((bsz, CHUNK, WIDTH), lambda c: (0, jnp.maximum(c - 1, 0), 0)),
        out_shape=jax.ShapeDtypeStruct((bsz, seq, WIDTH), BF16),
        scratch_shapes=[pltpu.VMEM((bsz * HEADS, HEAD_DIM, HEAD_DIM), F32)],
        compiler_params=pltpu.CompilerParams(
            dimension_semantics=("arbitrary",), vmem_limit_bytes=VMEM_LIMIT),
        name="hgrn",
    )(_hgrn_level_table(), norm_g, *([proj_meta] * 4), *([proj_real] * 4))


def _gdn_cumsum_matrix():
    r = np.arange(STACK)
    u = (r[:, None] // CHUNK == r[None, :] // CHUNK) & (r[:, None] <= r[None, :])
    return jnp.asarray(u, dtype=BF16)


def _gdn_kernel(u_ref, alog_ref, dt_ref, cw_ref, g_ref, s_ref,
                mq_ref, mk_ref, mv_ref, mz_ref, rq_ref, rk_ref, rv_ref, rz_ref,
                o_ref, st_ref, prev_ref, *, bsz):
    c = pl.program_id(0)

    @pl.when(c == 0)
    def _():
        st_ref[...] = jnp.zeros_like(st_ref)
        prev_ref[...] = jnp.zeros_like(prev_ref)

    is_meta = c == 0
    xq = _load_chunk(is_meta, mq_ref, rq_ref, bsz)
    xk = _load_chunk(is_meta, mk_ref, rk_ref, bsz)
    xv = _load_chunk(is_meta, mv_ref, rv_ref, bsz)
    gate = _load_chunk(is_meta, mz_ref, rz_ref, bsz)

    x = jnp.concatenate([xq, xk, xv], axis=1)
    pv = prev_ref[...]
    trow = lax.broadcasted_iota(jnp.int32, x.shape, 0) & (CHUNK - 1)
    cw = cw_ref[...]
    acc = x * cw[CONV_W - 1:CONV_W, :]
    for d in range(1, CONV_W):
        xs = jnp.where(trow >= d, _roll_rows(x, d), _roll_rows(pv, d - CHUNK))
        acc = acc + xs * cw[CONV_W - 1 - d:CONV_W - d, :]
    prev_ref[...] = x
    y = _silu(acc)

    qs = _stack_heads(y[:, :WIDTH], bsz)
    ks = _stack_heads(y[:, WIDTH:2 * WIDTH], bsz)
    vs = _stack_heads(y[:, 2 * WIDTH:], bsz)
    qs = qs * lax.rsqrt(jnp.sum(qs * qs, axis=-1, keepdims=True) + EPS) * (HEAD_DIM ** -0.5)
    ks = ks * lax.rsqrt(jnp.sum(ks * ks, axis=-1, keepdims=True) + EPS)
    to3 = functools.partial(_to_stacks, bsz=bsz)
    qs, ks, vs = to3(qs), to3(ks), to3(vs)
    n_st = bsz * N_STACKS

    srow = s_ref[...].reshape(n_st, 2, STACK)
    per_stack = lambda ref: jnp.broadcast_to(ref[...][None], (bsz, N_STACKS, 1, STACK)).reshape(n_st, 1, STACK)
    beta_row = _sigmoid(srow[:, 0:1, :])
    g_row = -jnp.exp(per_stack(alog_ref)) * _softplus(srow[:, 1:2, :] + per_stack(dt_ref))
    n8 = n_st * SUBLANES
    g8 = jnp.broadcast_to(g_row, (n_st, SUBLANES, STACK)).reshape(n8, STACK)
    pieces = jnp.concatenate(_split3(g8), axis=0)
    cum = jnp.dot(pieces, u_ref[...], preferred_element_type=F32)
    cum = cum[0:n8] + cum[n8:2 * n8] + cum[2 * n8:]
    cum_row = cum.reshape(n_st, SUBLANES, STACK)[:, 0:1, :]

    r = lax.broadcasted_iota(jnp.int32, (STACK, STACK), 0)
    cc = lax.broadcasted_iota(jnp.int32, (STACK, STACK), 1)
    eye = (r == cc)[None]
    cum_col = jnp.sum(jnp.where(eye, cum_row, 0.0), axis=2, keepdims=True)
    beta_col = jnp.sum(jnp.where(eye, beta_row, 0.0), axis=2, keepdims=True)
    same = (r >> 6) == (cc >> 6)
    lower = (same & (r >= cc))[None]
    strict = (same & (r > cc))[None]
    ratio = jnp.exp(jnp.minimum(cum_col - cum_row, 0.0))

    kkt = _bdot_nt(ks, ks)
    qkt = _bdot_nt(qs, ks)
    lm = jnp.where(strict, beta_col * ratio * kkt, 0.0)

    ident = eye.astype(F32)
    l0 = jnp.where(((r >> 3) == (cc >> 3))[None], lm, 0.0)
    l2 = _bdot(l0, l0)
    l4 = _bdot(l2, l2)
    xinv = _bdot(_bdot(ident - l0, ident + l2), ident + l4)
    for lvl in (3, 4, 5):
        off = (((r >> (lvl + 1)) == (cc >> (lvl + 1))) & ((r >> lvl) != (cc >> lvl)))[None]
        xinv = xinv - _bdot(_bdot(xinv, jnp.where(off, lm, 0.0)), xinv)

    decay_col = jnp.exp(cum_col)
    rhs = jnp.concatenate([beta_col * vs, (beta_col * decay_col) * ks], axis=2)
    sol = _bdot(xinv, rhs)

    to_g = functools.partial(_to_heads, bsz=bsz)
    st = st_ref[...]
    u = to_g(sol[:, :, :HEAD_DIM]) - _bdot(to_g(sol[:, :, HEAD_DIM:]), st)
    o_inter = to_g(decay_col) * _bdot(to_g(qs), st)
    cum_g = to_g(cum_col)
    cum_end = cum_g[:, CHUNK - 1:CHUNK, :]
    kdec = to_g(ks) * jnp.exp(cum_end - cum_g)
    st_ref[...] = jnp.exp(cum_end) * st + _bdot_tn(kdec, u)
    attn = jnp.where(lower, qkt * ratio, 0.0)
    o = o_inter + to_g(_bdot(attn, to3(u)))
    o = o.reshape(bsz * HEADS * CHUNK, HEAD_DIM)

    o = o * lax.rsqrt(jnp.mean(o * o, axis=-1, keepdims=True) + EPS) * g_ref[...]
    o = o * _stack_heads(gate, bsz)
    o_ref[...] = _unstack_heads(o, bsz).astype(o_ref.dtype)


def _gdn(proj_meta, proj_real, scal, a_log_row, dt_row, conv_w, norm_g):
    bsz, seq, _ = proj_real.shape
    n_chunks = seq // CHUNK + 1
    const = lambda c: (0, 0)
    const3 = lambda c: (0, 0, 0)
    return pl.pallas_call(
        functools.partial(_gdn_kernel, bsz=bsz),
        grid=(n_chunks,),
        in_specs=[pl.BlockSpec((STACK, STACK), const),
                  pl.BlockSpec((N_STACKS, 1, STACK), const3),
                  pl.BlockSpec((N_STACKS, 1, STACK), const3),
                  pl.BlockSpec((CONV_W, 3 * WIDTH), const),
                  pl.BlockSpec((1, HEAD_DIM), const),
                  pl.BlockSpec((bsz, None, N_STACKS, 2, STACK), lambda c: (0, c, 0, 0, 0))]
                 + _chunk_specs(bsz, 0),
        out_specs=pl.BlockSpec((bsz, CHUNK, WIDTH), lambda c: (0, jnp.maximum(c - 1, 0), 0)),
        out_shape=jax.ShapeDtypeStruct((bsz, seq, WIDTH), BF16),
        scratch_shapes=[pltpu.VMEM((bsz * HEADS, HEAD_DIM, HEAD_DIM), F32),
                        pltpu.VMEM((bsz * CHUNK, 3 * WIDTH), F32)],
        compiler_params=pltpu.CompilerParams(
            dimension_semantics=("arbitrary",), vmem_limit_bytes=VMEM_LIMIT),
        name="gdn",
    )(_gdn_cumsum_matrix(), a_log_row, dt_row, conv_w, norm_g, scal,
      *([proj_meta] * 4), *([proj_real] * 4))


def _merge_kernel(oa_ref, ob_ref, ga_ref, gb_ref, x_ref, hgup_ref, gdup_ref, wout_ref,
                  ng_ref, rwh_ref, rwl_ref, rb_ref, h2_ref, xn_ref, lg_ref):
    ua = jnp.dot(oa_ref[...], hgup_ref[...], preferred_element_type=F32)
    ub = jnp.dot(ob_ref[...], gdup_ref[...], preferred_element_type=F32)
    merged = ga_ref[...] * ua + gb_ref[...] * ub
    h2 = x_ref[...] + jnp.dot(merged.astype(BF16), wout_ref[...], preferred_element_type=F32)
    h2_ref[...] = h2
    xn = h2 * lax.rsqrt(jnp.mean(h2 * h2, axis=-1, keepdims=True) + EPS) * ng_ref[...]
    _write_row_tiles(xn_ref, xn)
    xh = xn.astype(BF16)
    xl = (xn - xh.astype(F32)).astype(BF16)
    wh, wl = rwh_ref[...], rwl_ref[...]
    lg_ref[...] = (jnp.dot(xh, wh, preferred_element_type=F32)
                   + jnp.dot(xh, wl, preferred_element_type=F32)
                   + jnp.dot(xl, wh, preferred_element_type=F32)) + rb_ref[...]


def _merge(o_a, o_b, proj, x2d, hg_up, gd_up, w_out, norm_g, rw_hi, rw_lo, rb):
    t = x2d.shape[0]
    tm = TOKEN_TM
    row = lambda i: (i, 0)
    const = lambda i: (0, 0)
    return pl.pallas_call(
        _merge_kernel,
        grid=(t // tm,),
        in_specs=[pl.BlockSpec((tm, WIDTH), row),
                  pl.BlockSpec((tm, WIDTH), row),
                  pl.BlockSpec((tm, D_MODEL), lambda i: (i, 0)),
                  pl.BlockSpec((tm, D_MODEL), lambda i: (i, 1)),
                  pl.BlockSpec((tm, D_MODEL), row),
                  pl.BlockSpec((WIDTH, D_MODEL), const),
                  pl.BlockSpec((WIDTH, D_MODEL), const),
                  pl.BlockSpec((D_MODEL, D_MODEL), const),
                  pl.BlockSpec((1, D_MODEL), const),
                  pl.BlockSpec((D_MODEL, LANES), const),
                  pl.BlockSpec((D_MODEL, LANES), const),
                  pl.BlockSpec((1, LANES), const)],
        out_specs=[pl.BlockSpec((tm, D_MODEL), row),
                   pl.BlockSpec((tm, ROW_TILES, LANES), lambda i: (i, 0, 0)),
                   pl.BlockSpec((tm, LANES), row)],
        out_shape=[jax.ShapeDtypeStruct((t, D_MODEL), F32),
                   jax.ShapeDtypeStruct((t, ROW_TILES, LANES), F32),
                   jax.ShapeDtypeStruct((t, LANES), F32)],
        compiler_params=pltpu.CompilerParams(
            dimension_semantics=("arbitrary",), vmem_limit_bytes=VMEM_LIMIT),
        name="merge",
    )(o_a, o_b, proj, proj, x2d, hg_up, gd_up, w_out, norm_g, rw_hi, rw_lo, rb)


def _moe_kernel(rt_ref, orow_ref, be_ref, x_hbm, wg_ref, wu_ref, wd_ref, y_hbm,
                xbuf0, xbuf1, ybuf0, ybuf1, sem_in, sem_out):
    i = pl.program_id(0)
    n = pl.num_programs(0)
    xbufs, ybufs = (xbuf0, xbuf1), (ybuf0, ybuf1)

    def gather_row(slot, blk, k):
        tok = rt_ref[blk * MOE_ROWS + k]
        return pltpu.make_async_copy(x_hbm.at[tok], xbufs[slot].at[k], sem_in.at[slot])

    def scatter_row(slot, blk, k):
        dst = orow_ref[blk * MOE_ROWS + k]
        return pltpu.make_async_copy(ybufs[slot].at[k], y_hbm.at[dst], sem_out.at[slot])

    def wait_gather(slot):
        pltpu.make_async_copy(x_hbm.at[pl.ds(0, MOE_ROWS)], xbufs[slot], sem_in.at[slot]).wait()

    def wait_scatter(slot):
        pltpu.make_async_copy(ybufs[slot], y_hbm.at[pl.ds(0, MOE_ROWS)], sem_out.at[slot]).wait()

    @pl.when(i == 0)
    def _():
        for k in range(MOE_ROWS):
            gather_row(0, 0, k).start(priority=k % 2)

    def step(slot):
        wait_gather(slot)

        @pl.when(i >= 2)
        def _():
            wait_scatter(slot)

        for k in range(MOE_ROWS):
            gather_row(1 - slot, i + 1, k).start(priority=k % 2)
        xb = _read_row_tiles(xbufs[slot]).astype(BF16)
        a = jnp.dot(xb, wg_ref[...].astype(BF16), preferred_element_type=F32)
        u = jnp.dot(xb, wu_ref[...].astype(BF16), preferred_element_type=F32)
        y = jnp.dot((_silu(a) * u).astype(BF16), wd_ref[...].astype(BF16), preferred_element_type=F32)
        _write_row_tiles(ybufs[slot], y)
        for k in range(MOE_ROWS):
            scatter_row(slot, i, k).start(priority=k % 2)

        @pl.when(i == n - 1)
        def _():
            wait_gather(1 - slot)
            wait_scatter(slot)

            @pl.when(i >= 1)
            def _():
                wait_scatter(1 - slot)

    @pl.when(i % 2 == 0)
    def _():
        step(0)

    @pl.when(i % 2 == 1)
    def _():
        step(1)


def _moe(row_tok, out_row, blk_expert, xn, w_gate, w_up, w_down, n_out_rows):
    n_blocks = blk_expert.shape[0]
    wspec = lambda shape: pl.BlockSpec((None,) + shape, lambda i, rt, orow, be: (be[i], 0, 0))
    row_buf = pltpu.VMEM((MOE_ROWS, ROW_TILES, LANES), F32)
    grid_spec = pltpu.PrefetchScalarGridSpec(
        num_scalar_prefetch=3,
        grid=(n_blocks,),
        in_specs=[pl.BlockSpec(memory_space=pl.ANY),
                  wspec((D_MODEL, D_FF)), wspec((D_MODEL, D_FF)), wspec((D_FF, D_MODEL))],
        out_specs=pl.BlockSpec(memory_space=pl.ANY),
        scratch_shapes=[row_buf, row_buf, row_buf, row_buf,
                        pltpu.SemaphoreType.DMA((2,)), pltpu.SemaphoreType.DMA((2,))],
    )
    return pl.pallas_call(
        _moe_kernel,
        grid_spec=grid_spec,
        out_shape=jax.ShapeDtypeStruct((n_out_rows, ROW_TILES, LANES), F32),
        compiler_params=pltpu.CompilerParams(
            dimension_semantics=("arbitrary",), vmem_limit_bytes=VMEM_LIMIT),
        name="moe",
    )(row_tok, out_row, blk_expert, xn, w_gate, w_up, w_down)


def _combine_kernel(h2_ref, rw_ref, g_ref, y0_ref, y1_ref, o_ref):
    rw = rw_ref[...]
    h = h2_ref[...] + rw[:, 0:1] * _read_row_tiles(y0_ref) + rw[:, 1:2] * _read_row_tiles(y1_ref)
    o_ref[...] = h * lax.rsqrt(jnp.mean(h * h, axis=-1, keepdims=True) + EPS) * g_ref[...]


def _combine(h2, rweights, final_g, y):
    t = h2.shape[0]
    tm = TOKEN_TM
    row = lambda i: (i, 0)
    return pl.pallas_call(
        _combine_kernel,
        grid=(t // tm,),
        in_specs=[pl.BlockSpec((tm, D_MODEL), row),
                  pl.BlockSpec((tm, LANES), row),
                  pl.BlockSpec((1, D_MODEL), lambda i: (0, 0)),
                  pl.BlockSpec((tm, ROW_TILES, LANES), lambda i: (i, 0, 0)),
                  pl.BlockSpec((tm, ROW_TILES, LANES), lambda i: (i + t // tm, 0, 0))],
        out_specs=pl.BlockSpec((tm, D_MODEL), row),
        out_shape=jax.ShapeDtypeStruct((t, D_MODEL), F32),
        compiler_params=pltpu.CompilerParams(
            dimension_semantics=("arbitrary",), vmem_limit_bytes=VMEM_LIMIT),
        name="combine",
    )(h2, rweights, final_g, y, y)


def _route(logits, t):
    gl = logits[:, :N_GROUPS]
    el = logits[:, N_GROUPS:N_GROUPS + N_EXPERTS].reshape(t, N_GROUPS, EXPERTS_PER_GROUP)
    g_prob = jax.nn.softmax(gl, axis=-1)
    grp = jnp.argmax(gl, axis=-1)
    p_grp = jnp.take_along_axis(g_prob, grp[:, None], axis=-1)
    sel = jnp.take_along_axis(el, grp[:, None, None], axis=1)[:, 0]
    top_p, top_i = lax.top_k(jax.nn.softmax(sel, axis=-1), 2)
    weights = p_grp * top_p / jnp.sum(top_p, axis=-1, keepdims=True)
    e_flat = (grp[:, None] * EXPERTS_PER_GROUP + top_i).reshape(2 * t).astype(jnp.int32)

    a = 2 * t
    n_blocks = a // MOE_ROWS + N_EXPERTS
    experts = jnp.arange(N_EXPERTS, dtype=jnp.int32)
    order = jnp.argsort(e_flat).astype(jnp.int32)
    counts = jnp.sum((e_flat[:, None] == experts[None, :]).astype(jnp.int32), axis=0)
    start = jnp.cumsum(counts) - counts
    padded = ((counts + MOE_ROWS - 1) // MOE_ROWS) * MOE_ROWS
    pend = jnp.cumsum(padded)
    pstart = pend - padded
    blk_first = jnp.arange(n_blocks, dtype=jnp.int32) * MOE_ROWS
    blk_expert = jnp.minimum(jnp.sum((pend[None, :] <= blk_first[:, None]).astype(jnp.int32), axis=1),
                             N_EXPERTS - 1).astype(jnp.int32)
    p = jnp.arange(n_blocks * MOE_ROWS, dtype=jnp.int32)
    e_p = jnp.repeat(blk_expert, MOE_ROWS)
    j = p - pstart[e_p]
    valid = j < counts[e_p]
    src = order[jnp.clip(start[e_p] + j, 0, a - 1)]
    row_tok = jnp.where(valid, src // 2, 0)
    out_row = jnp.where(valid, (src % 2) * t + src // 2, a + p - (start[e_p] + counts[e_p]))
    row_tok = jnp.concatenate([row_tok, jnp.zeros((MOE_ROWS,), jnp.int32)])
    rweights = jnp.zeros((t, LANES), F32).at[:, :2].set(weights)
    return row_tok, out_row, blk_expert, rweights, n_blocks * MOE_ROWS


def kernel(x, meta_tokens, hg_lb_logits, norm_mix_g, w_in, gd_conv_w, gd_A_log, gd_dt_bias, hg_norm_g, gd_norm_g, hg_up, gd_up, w_out, norm_ffn_g, router_group_w, router_group_b, router_expert_w, router_expert_b, w_gate, w_up, w_down, final_norm_g):
    bsz, seq, d = x.shape
    t = bsz * seq
    x2d = x.reshape(t, d)

    lb = jnp.cumsum(jax.nn.softmax(hg_lb_logits.astype(F32), axis=0), axis=0)[0].reshape(1, WIDTH)
    w = w_in[0].astype(F32)
    n_h, n_g = 4 * WIDTH, 8 * WIDTH
    n_s = n_g + 2 * HEADS
    w_groups = {
        "hgrn": w[:, :n_h],
        "gdn": jnp.concatenate([w[:, n_h:n_s], jnp.zeros((d, LANES - 2 * HEADS), F32)], axis=1),
        "gates": w[:, n_s:],
    }
    g_mix = norm_mix_g[0].reshape(1, d)
    meta_blk = jnp.concatenate([jnp.zeros((CHUNK - N_META, d), F32), meta_tokens.astype(F32)], axis=0)

    proj = {k: _norm_proj(x2d, g_mix, lb, wk, PROJ_TM, k) for k, wk in w_groups.items()}
    proj_meta = {k: _norm_proj(meta_blk, g_mix, lb, w_groups[k], CHUNK, k) for k in ("hgrn", "gdn")}

    o_a = _hgrn(proj_meta["hgrn"], proj["hgrn"].reshape(bsz, seq, -1), hg_norm_g[0].reshape(1, HEAD_DIM))

    def scalar_rows(p, nb, nc):
        s = p[:, 4 * WIDTH:4 * WIDTH + 2 * HEADS].reshape(nb, nc, CHUNK, 2, N_STACKS, HEADS_PER_STACK)
        return s.transpose(0, 1, 4, 3, 5, 2).reshape(nb, nc, N_STACKS, 2, STACK)

    s_meta = jnp.broadcast_to(scalar_rows(proj_meta["gdn"], 1, 1), (bsz, 1, N_STACKS, 2, STACK))
    scal = jnp.concatenate([s_meta, scalar_rows(proj["gdn"], bsz, seq // CHUNK)], axis=1)
    a_log_row = jnp.repeat(gd_A_log[0].astype(F32), CHUNK).reshape(N_STACKS, 1, STACK)
    dt_row = jnp.repeat(gd_dt_bias[0].astype(F32), CHUNK).reshape(N_STACKS, 1, STACK)
    o_b = _gdn(proj_meta["gdn"], proj["gdn"].reshape(bsz, seq, -1), scal, a_log_row, dt_row,
               gd_conv_w[0].astype(F32), gd_norm_g[0].reshape(1, HEAD_DIM))

    rw = jnp.zeros((d, LANES), F32)
    rw = rw.at[:, :N_GROUPS].set(router_group_w[0]).at[:, N_GROUPS:N_GROUPS + N_EXPERTS].set(router_expert_w[0])
    rw_hi = rw.astype(BF16)
    rw_lo = (rw - rw_hi.astype(F32)).astype(BF16)
    rb = jnp.zeros((1, LANES), F32)
    rb = rb.at[0, :N_GROUPS].set(router_group_b[0]).at[0, N_GROUPS:N_GROUPS + N_EXPERTS].set(router_expert_b[0])
    h2, xn, logits = _merge(o_a.reshape(t, WIDTH), o_b.reshape(t, WIDTH), proj["gates"], x2d,
                            hg_up[0].astype(BF16), gd_up[0].astype(BF16), w_out[0].astype(BF16),
                            norm_ffn_g[0].reshape(1, d), rw_hi, rw_lo, rb)

    row_tok, out_row, blk_expert, rweights, n_out_rows = _route(logits, t)
    y = _moe(row_tok, out_row, blk_expert, xn,
             w_gate[0].astype(F32), w_up[0].astype(F32), w_down[0].astype(F32), n_out_rows)
    out = _combine(h2, rweights, final_norm_g.reshape(1, d), y)
    return out.reshape(bsz, seq, d)
```

```python
import functools

import numpy as np
import jax
import jax.numpy as jnp
from jax import lax
from jax.experimental import pallas as pl
from jax.experimental.pallas import tpu as pltpu
from jax.experimental.pallas import tpu_sc as plsc

F32 = jnp.float32
BF16 = jnp.bfloat16
U32 = jnp.uint32

D_MODEL = 1024
N_META = 16
CHUNK = 64
EPS = 1e-6
HEADS = 4
HEAD_DIM = 128
WIDTH = HEADS * HEAD_DIM
HEADS_PER_STACK = 2
STACK = HEADS_PER_STACK * CHUNK
N_STACKS = HEADS // HEADS_PER_STACK
CONV_W = 4
N_GROUPS = 4
EXPERTS_PER_GROUP = 8
N_EXPERTS = N_GROUPS * EXPERTS_PER_GROUP
D_FF = 512
N_BIG = 12 * WIDTH
LANES = 128
SUBLANES = 8
HALF = D_MODEL // 2
MOE_ROWS = 256
SC_CORES, SC_SUBCORES = 2, 16
SC_WORKERS = SC_CORES * SC_SUBCORES
SC_CHUNK = 128
PROJ_TM, PROJ_TN = 512, 2048
TOKEN_TM = 256
VMEM_LIMIT = 48 * 1024 * 1024


def _bdot(a, b):
    return lax.dot_general(a.astype(BF16), b.astype(BF16), (((2,), (1,)), ((0,), (0,))),
                           preferred_element_type=F32)


def _bdot_nt(a, b):
    return lax.dot_general(a.astype(BF16), b.astype(BF16), (((2,), (2,)), ((0,), (0,))),
                           preferred_element_type=F32)


def _bdot_tn(a, b):
    return lax.dot_general(a.astype(BF16), b.astype(BF16), (((1,), (1,)), ((0,), (0,))),
                           preferred_element_type=F32)


def _split3(x):
    hi = x.astype(BF16)
    r1 = x - hi.astype(F32)
    mid = r1.astype(BF16)
    lo = (r1 - mid.astype(F32)).astype(BF16)
    return hi, mid, lo


def _sigmoid(x):
    return 1.0 / (1.0 + jnp.exp(-x))


def _silu(x):
    return x * _sigmoid(x)


def _softplus(x):
    return jnp.maximum(x, 0.0) + jnp.log(1.0 + jnp.exp(-jnp.abs(x)))


def _stack_heads(x, bsz):
    x3 = x.reshape(bsz, CHUNK, WIDTH)
    y = jnp.concatenate([x3[:, :, h * HEAD_DIM:(h + 1) * HEAD_DIM] for h in range(HEADS)], axis=1)
    return y.reshape(bsz * HEADS * CHUNK, HEAD_DIM)


def _unstack_heads(x, bsz):
    x3 = x.reshape(bsz, HEADS * CHUNK, HEAD_DIM)
    return jnp.concatenate([x3[:, h * CHUNK:(h + 1) * CHUNK, :] for h in range(HEADS)], axis=2)


def _to_stacks(a, bsz):
    return a.reshape(bsz * N_STACKS, STACK, a.shape[-1])


def _to_heads(a, bsz):
    return a.reshape(bsz * HEADS, CHUNK, a.shape[-1])


def _roll_rows(x, shift):
    n = x.shape[0]
    return pltpu.roll(x, shift % n, axis=0)


def _load_chunk(is_meta, meta_ref, real_ref, bsz):
    x = jnp.where(is_meta, meta_ref[...][None], real_ref[...])
    return x.reshape(bsz * CHUNK, WIDTH)


def _pack_halves(x):
    lo = lax.bitcast_convert_type(x[:, :HALF].astype(BF16).astype(F32), U32)
    hi = lax.bitcast_convert_type(x[:, HALF:].astype(BF16).astype(F32), U32)
    return (lo >> 16) | (hi & jnp.uint32(0xFFFF0000))


def _unpack_halves(p):
    lo = lax.bitcast_convert_type(p << 16, F32)
    hi = lax.bitcast_convert_type(p & jnp.uint32(0xFFFF0000), F32)
    return jnp.concatenate([lo, hi], axis=1)


def _norm_proj_kernel(x_ref, g_ref, lb_ref, w_ref, o_ref, wbf_ref, *, group):
    @pl.when(pl.program_id(0) == 0)
    def _():
        wbf_ref[...] = w_ref[...].astype(BF16)

    x = x_ref[...]
    xn = x * lax.rsqrt(jnp.mean(x * x, axis=-1, keepdims=True) + EPS) * g_ref[...]
    acc = jnp.dot(xn.astype(BF16), wbf_ref[...], preferred_element_type=F32)
    w1, w2, w3, w4 = WIDTH, 2 * WIDTH, 3 * WIDTH, 4 * WIDTH
    if group == "hgrn":
        lb = lb_ref[...]
        o_ref[:, :w1] = _silu(acc[:, :w1])
        o_ref[:, w1:w2] = lb + (1.0 - lb) * _sigmoid(acc[:, w1:w2])
        o_ref[:, w2:w3] = acc[:, w2:w3]
        o_ref[:, w3:] = _silu(acc[:, w3:])
    elif group == "gdn":
        o_ref[:, :w3] = acc[:, :w3]
        o_ref[:, w3:w4] = _silu(acc[:, w3:w4])
        o_ref[:, w4:] = acc[:, w4:]
    else:
        o_ref[...] = _sigmoid(acc)


def _norm_proj(x2d, g, lb, w, tm, group):
    m, k = x2d.shape
    n = w.shape[1]
    const = lambda i: (0, 0)
    return pl.pallas_call(
        functools.partial(_norm_proj_kernel, group=group),
        grid=(m // tm,),
        in_specs=[pl.BlockSpec((tm, k), lambda i: (i, 0)),
                  pl.BlockSpec((1, k), const),
                  pl.BlockSpec((1, WIDTH), const),
                  pl.BlockSpec((k, n), const)],
        out_specs=pl.BlockSpec((tm, n), lambda i: (i, 0)),
        out_shape=jax.ShapeDtypeStruct((m, n), F32),
        scratch_shapes=[pltpu.VMEM((k, n), BF16)],
        compiler_params=pltpu.CompilerParams(
            dimension_semantics=("arbitrary",), vmem_limit_bytes=VMEM_LIMIT),
        name="norm_proj_" + group,
    )(x2d, g, lb, w)


HGRN_LEVELS = 6
HGRN_DIAG = HGRN_LEVELS


def _hgrn_level_table():
    r = np.arange(STACK)
    t, hd = r % CHUNK, r // CHUNK
    x = t[:, None] ^ t[None, :]
    lv = np.floor(np.log2(np.maximum(x, 1))).astype(np.int32)
    valid = (hd[:, None] == hd[None, :]) & (t[:, None] > t[None, :])
    lv = np.where(valid, lv, -1)
    lv = np.where(r[:, None] == r[None, :], HGRN_DIAG, lv)
    return jnp.asarray(lv, dtype=jnp.int32)


def _hgrn_kernel(lv_ref, g_ref, mq_ref, mf_ref, mi_ref, mg_ref,
                 rq_ref, rf_ref, ri_ref, rg_ref, o_ref, st_ref, *, bsz):
    c = pl.program_id(0)

    @pl.when(c == 0)
    def _():
        st_ref[...] = jnp.zeros_like(st_ref)

    is_meta = c == 0
    q = _load_chunk(is_meta, mq_ref, rq_ref, bsz)
    f = _load_chunk(is_meta, mf_ref, rf_ref, bsz)
    v = _load_chunk(is_meta, mi_ref, ri_ref, bsz)
    gate = _load_chunk(is_meta, mg_ref, rg_ref, bsz)

    trow = lax.broadcasted_iota(jnp.int32, (bsz * CHUNK, WIDTH), 0) & (CHUNK - 1)
    b = jnp.log(f)
    s = 1
    while s < CHUNK:
        b = b + jnp.where(trow >= s, _roll_rows(b, s), 0.0)
        s *= 2

    qs, ks, vs, bs = (_stack_heads(a, bsz) for a in (q, 1.0 - f, v, b))
    t = lax.broadcasted_iota(jnp.int32, qs.shape, 0) & (CHUNK - 1)
    lv = lv_ref[...][None]
    to3 = functools.partial(_to_stacks, bsz=bsz)

    attn = jnp.where(lv == HGRN_DIAG, _bdot_nt(to3(qs), to3(ks)), 0.0)
    bref = bs
    bnext = _roll_rows(bs, -1)
    for li in range(HGRN_LEVELS):
        m = 1 << li
        if li > 0:
            half = m // 2
            upper = (t & (m - 1)) >= half
            bref = jnp.where(upper, _roll_rows(bref, half), bref)
            bnext = jnp.where(upper, bnext, _roll_rows(bnext, -half))
        qt = qs * jnp.exp(bs - bref)
        kt = ks * jnp.exp(jnp.minimum(bnext - bs, 0.0))
        attn = jnp.where(lv == li, _bdot_nt(to3(qt), to3(kt)), attn)

    o = _bdot(attn, to3(vs))

    to_g = functools.partial(_to_heads, bsz=bsz)
    st = st_ref[...]
    bs_g = to_g(bs)
    o_inter = _bdot_nt(to_g(qs * jnp.exp(bs)), st)
    b_end = bs_g[:, CHUNK - 1:CHUNK, :]
    ke = to_g(ks) * jnp.exp(b_end - bs_g)
    st_ref[...] = st * jnp.exp(b_end) + _bdot_tn(to_g(vs), ke)
    o = (to_g(o) + o_inter).reshape(qs.shape)

    o = o * lax.rsqrt(jnp.mean(o * o, axis=-1, keepdims=True) + EPS) * g_ref[...]
    o = o * _stack_heads(gate, bsz)
    o_ref[...] = _unstack_heads(o, bsz).astype(o_ref.dtype)


def _chunk_specs(bsz, col0):
    meta = [pl.BlockSpec((CHUNK, WIDTH), functools.partial(lambda c, j: (0, j), j=col0 + j))
            for j in range(4)]
    real = [pl.BlockSpec((bsz, CHUNK, WIDTH),
                         functools.partial(lambda c, j: (0, jnp.maximum(c - 1, 0), j), j=col0 + j))
            for j in range(4)]
    return meta + real


def _hgrn(proj_meta, proj_real, norm_g):
    bsz, seq, _ = proj_real.shape
    n_chunks = seq // CHUNK + 1
    const = lambda c: (0, 0)
    return pl.pallas_call(
        functools.partial(_hgrn_kernel, bsz=bsz),
        grid=(n_chunks,),
        in_specs=[pl.BlockSpec((STACK, STACK), const),
                  pl.BlockSpec((1, HEAD_DIM), const)] + _chunk_specs(bsz, 0),
        out_specs=pl.BlockSpec((bsz, CHUNK, WIDTH), lambda c: (0, jnp.maximum(c - 1, 0), 0)),
        out_shape=jax.ShapeDtypeStruct((bsz, seq, WIDTH), BF16),
        scratch_shapes=[pltpu.VMEM((bsz * HEADS, HEAD_DIM, HEAD_DIM), F32)],
        compiler_params=pltpu.CompilerParams(
            dimension_semantics=("arbitrary",), vmem_limit_bytes=VMEM_LIMIT),
        name="hgrn",
    )(_hgrn_level_table(), norm_g, *([proj_meta] * 4), *([proj_real] * 4))


def _gdn_cumsum_matrix():
    r = np.arange(STACK)
    u = (r[:, None] // CHUNK == r[None, :] // CHUNK) & (r[:, None] <= r[None, :])
    return jnp.asarray(u, dtype=BF16)


def _gdn_kernel(u_ref, alog_ref, dt_ref, cw_ref, g_ref, s_ref,
                mq_ref, mk_ref, mv_ref, mz_ref, rq_ref, rk_ref, rv_ref, rz_ref,
                o_ref, st_ref, prev_ref, *, bsz):
    c = pl.program_id(0)

    @pl.when(c == 0)
    def _():
        st_ref[...] = jnp.zeros_like(st_ref)
        prev_ref[...] = jnp.zeros_like(prev_ref)

    is_meta = c == 0
    xq = _load_chunk(is_meta, mq_ref, rq_ref, bsz)
    xk = _load_chunk(is_meta, mk_ref, rk_ref, bsz)
    xv = _load_chunk(is_meta, mv_ref, rv_ref, bsz)
    gate = _load_chunk(is_meta, mz_ref, rz_ref, bsz)

    x = jnp.concatenate([xq, xk, xv], axis=1)
    pv = prev_ref[...]
    trow = lax.broadcasted_iota(jnp.int32, x.shape, 0) & (CHUNK - 1)
    cw = cw_ref[...]
    acc = x * cw[CONV_W - 1:CONV_W, :]
    for d in range(1, CONV_W):
        xs = jnp.where(trow >= d, _roll_rows(x, d), _roll_rows(pv, d - CHUNK))
        acc = acc + xs * cw[CONV_W - 1 - d:CONV_W - d, :]
    prev_ref[...] = x
    y = _silu(acc)

    qs = _stack_heads(y[:, :WIDTH], bsz)
    ks = _stack_heads(y[:, WIDTH:2 * WIDTH], bsz)
    vs = _stack_heads(y[:, 2 * WIDTH:], bsz)
    qs = qs * lax.rsqrt(jnp.sum(qs * qs, axis=-1, keepdims=True) + EPS) * (HEAD_DIM ** -0.5)
    ks = ks * lax.rsqrt(jnp.sum(ks * ks, axis=-1, keepdims=True) + EPS)
    to3 = functools.partial(_to_stacks, bsz=bsz)
    qs, ks, vs = to3(qs), to3(ks), to3(vs)
    n_st = bsz * N_STACKS

    srow = s_ref[...].reshape(n_st, 2, STACK)
    per_stack = lambda ref: jnp.broadcast_to(ref[...][None], (bsz, N_STACKS, 1, STACK)).reshape(n_st, 1, STACK)
    beta_row = _sigmoid(srow[:, 0:1, :])
    g_row = -jnp.exp(per_stack(alog_ref)) * _softplus(srow[:, 1:2, :] + per_stack(dt_ref))
    n8 = n_st * SUBLANES
    g8 = jnp.broadcast_to(g_row, (n_st, SUBLANES, STACK)).reshape(n8, STACK)
    pieces = jnp.concatenate(_split3(g8), axis=0)
    cum = jnp.dot(pieces, u_ref[...], preferred_element_type=F32)
    cum = cum[0:n8] + cum[n8:2 * n8] + cum[2 * n8:]
    cum_row = cum.reshape(n_st, SUBLANES, STACK)[:, 0:1, :]

    r = lax.broadcasted_iota(jnp.int32, (STACK, STACK), 0)
    cc = lax.broadcasted_iota(jnp.int32, (STACK, STACK), 1)
    eye = (r == cc)[None]
    cum_col = jnp.sum(jnp.where(eye, cum_row, 0.0), axis=2, keepdims=True)
    beta_col = jnp.sum(jnp.where(eye, beta_row, 0.0), axis=2, keepdims=True)
    same = (r >> 6) == (cc >> 6)
    lower = (same & (r >= cc))[None]
    strict = (same & (r > cc))[None]
    ratio = jnp.exp(jnp.minimum(cum_col - cum_row, 0.0))

    kkt = _bdot_nt(ks, ks)
    qkt = _bdot_nt(qs, ks)
    lm = jnp.where(strict, beta_col * ratio * kkt, 0.0)

    ident = eye.astype(F32)
    l0 = jnp.where(((r >> 3) == (cc >> 3))[None], lm, 0.0)
    l2 = _bdot(l0, l0)
    l4 = _bdot(l2, l2)
    xinv = _bdot(_bdot(ident - l0, ident + l2), ident + l4)
    for lvl in (3, 4, 5):
        off = (((r >> (lvl + 1)) == (cc >> (lvl + 1))) & ((r >> lvl) != (cc >> lvl)))[None]
        xinv = xinv - _bdot(_bdot(xinv, jnp.where(off, lm, 0.0)), xinv)

    decay_col = jnp.exp(cum_col)
    rhs = jnp.concatenate([beta_col * vs, (beta_col * decay_col) * ks], axis=2)
    sol = _bdot(xinv, rhs)

    to_g = functools.partial(_to_heads, bsz=bsz)
    st = st_ref[...]
    u = to_g(sol[:, :, :HEAD_DIM]) - _bdot(to_g(sol[:, :, HEAD_DIM:]), st)
    o_inter = to_g(decay_col) * _bdot(to_g(qs), st)
    cum_g = to_g(cum_col)
    cum_end = cum_g[:, CHUNK - 1:CHUNK, :]
    kdec = to_g(ks) * jnp.exp(cum_end - cum_g)
    st_ref[...] = jnp.exp(cum_end) * st + _bdot_tn(kdec, u)
    attn = jnp.where(lower, qkt * ratio, 0.0)
    o = o_inter + to_g(_bdot(attn, to3(u)))
    o = o.reshape(bsz * HEADS * CHUNK, HEAD_DIM)

    o = o * lax.rsqrt(jnp.mean(o * o, axis=-1, keepdims=True) + EPS) * g_ref[...]
    o = o * _stack_heads(gate, bsz)
    o_ref[...] = _unstack_heads(o, bsz).astype(o_ref.dtype)


def _gdn(proj_meta, proj_real, scal, a_log_row, dt_row, conv_w, norm_g):
    bsz, seq, _ = proj_real.shape
    n_chunks = seq // CHUNK + 1
    const = lambda c: (0, 0)
    const3 = lambda c: (0, 0, 0)
    return pl.pallas_call(
        functools.partial(_gdn_kernel, bsz=bsz),
        grid=(n_chunks,),
        in_specs=[pl.BlockSpec((STACK, STACK), const),
                  pl.BlockSpec((N_STACKS, 1, STACK), const3),
                  pl.BlockSpec((N_STACKS, 1, STACK), const3),
                  pl.BlockSpec((CONV_W, 3 * WIDTH), const),
                  pl.BlockSpec((1, HEAD_DIM), const),
                  pl.BlockSpec((bsz, None, N_STACKS, 2, STACK), lambda c: (0, c, 0, 0, 0))]
                 + _chunk_specs(bsz, 0),
        out_specs=pl.BlockSpec((bsz, CHUNK, WIDTH), lambda c: (0, jnp.maximum(c - 1, 0), 0)),
        out_shape=jax.ShapeDtypeStruct((bsz, seq, WIDTH), BF16),
        scratch_shapes=[pltpu.VMEM((bsz * HEADS, HEAD_DIM, HEAD_DIM), F32),
                        pltpu.VMEM((bsz * CHUNK, 3 * WIDTH), F32)],
        compiler_params=pltpu.CompilerParams(
            dimension_semantics=("arbitrary",), vmem_limit_bytes=VMEM_LIMIT),
        name="gdn",
    )(_gdn_cumsum_matrix(), a_log_row, dt_row, conv_w, norm_g, scal,
      *([proj_meta] * 4), *([proj_real] * 4))


def _merge_kernel(oa_ref, ob_ref, ga_ref, gb_ref, x_ref, hgup_ref, gdup_ref, wout_ref,
                  ng_ref, rwh_ref, rwl_ref, rb_ref, h2_ref, xn_ref, lg_ref):
    ua = jnp.dot(oa_ref[...], hgup_ref[...], preferred_element_type=F32)
    ub = jnp.dot(ob_ref[...], gdup_ref[...], preferred_element_type=F32)
    merged = ga_ref[...] * ua + gb_ref[...] * ub
    h2 = x_ref[...] + jnp.dot(merged.astype(BF16), wout_ref[...], preferred_element_type=F32)
    h2_ref[...] = h2
    xn = h2 * lax.rsqrt(jnp.mean(h2 * h2, axis=-1, keepdims=True) + EPS) * ng_ref[...]
    xn_ref[...] = _pack_halves(xn)
    xh = xn.astype(BF16)
    xl = (xn - xh.astype(F32)).astype(BF16)
    wh, wl = rwh_ref[...], rwl_ref[...]
    lg_ref[...] = (jnp.dot(xh, wh, preferred_element_type=F32)
                   + jnp.dot(xh, wl, preferred_element_type=F32)
                   + jnp.dot(xl, wh, preferred_element_type=F32)) + rb_ref[...]


def _merge(o_a, o_b, proj, x2d, hg_up, gd_up, w_out, norm_g, rw_hi, rw_lo, rb):
    t = x2d.shape[0]
    tm = TOKEN_TM
    row = lambda i: (i, 0)
    const = lambda i: (0, 0)
    return pl.pallas_call(
        _merge_kernel,
        grid=(t // tm,),
        in_specs=[pl.BlockSpec((tm, WIDTH), row),
                  pl.BlockSpec((tm, WIDTH), row),
                  pl.BlockSpec((tm, D_MODEL), lambda i: (i, 0)),
                  pl.BlockSpec((tm, D_MODEL), lambda i: (i, 1)),
                  pl.BlockSpec((tm, D_MODEL), row),
                  pl.BlockSpec((WIDTH, D_MODEL), const),
                  pl.BlockSpec((WIDTH, D_MODEL), const),
                  pl.BlockSpec((D_MODEL, D_MODEL), const),
                  pl.BlockSpec((1, D_MODEL), const),
                  pl.BlockSpec((D_MODEL, LANES), const),
                  pl.BlockSpec((D_MODEL, LANES), const),
                  pl.BlockSpec((1, LANES), const)],
        out_specs=[pl.BlockSpec((tm, D_MODEL), row),
                   pl.BlockSpec((tm, HALF), row),
                   pl.BlockSpec((tm, LANES), row)],
        out_shape=[jax.ShapeDtypeStruct((t, D_MODEL), F32),
                   jax.ShapeDtypeStruct((t, HALF), U32),
                   jax.ShapeDtypeStruct((t, LANES), F32)],
        compiler_params=pltpu.CompilerParams(
            dimension_semantics=("arbitrary",), vmem_limit_bytes=VMEM_LIMIT),
        name="merge",
    )(o_a, o_b, proj, proj, x2d, hg_up, gd_up, w_out, norm_g, rw_hi, rw_lo, rb)


def _sc_gather(table, idx):
    n_idx = idx.shape[0]
    cols = table.shape[1]
    per_worker = n_idx // SC_WORKERS
    assert n_idx % (SC_WORKERS * SC_CHUNK) == 0
    mesh = plsc.VectorSubcoreMesh(core_axis_name="c", subcore_axis_name="s")

    @functools.partial(
        pl.kernel, mesh=mesh,
        out_type=jax.ShapeDtypeStruct((n_idx, cols), table.dtype),
        scratch_types=[pltpu.VMEM((SC_CHUNK,), jnp.int32),
                       pltpu.VMEM((SC_CHUNK, cols), table.dtype),
                       pltpu.SemaphoreType.DMA],
    )
    def gather(table_hbm, idx_hbm, out_hbm, idx_v, rows_v, sem):
        worker = lax.axis_index("s") * SC_CORES + lax.axis_index("c")
        base = worker * per_worker

        @pl.loop(0, per_worker // SC_CHUNK)
        def _(c):
            off = pl.multiple_of(base + c * SC_CHUNK, SC_CHUNK)
            pltpu.sync_copy(idx_hbm.at[pl.ds(off, SC_CHUNK)], idx_v)
            pltpu.async_copy(table_hbm.at[idx_v], rows_v, sem).wait()
            pltpu.sync_copy(rows_v, out_hbm.at[pl.ds(off, SC_CHUNK)])

    return gather(table, idx)


def _moe_kernel(be_ref, nu_ref, x_ref, wg_ref, wu_ref, wd_ref, y_ref):
    i = pl.program_id(0)

    @pl.when(i < nu_ref[0])
    def _():
        xb = _unpack_halves(x_ref[...]).astype(BF16)
        a = jnp.dot(xb, wg_ref[...].astype(BF16), preferred_element_type=F32)
        u = jnp.dot(xb, wu_ref[...].astype(BF16), preferred_element_type=F32)
        y = jnp.dot((_silu(a) * u).astype(BF16), wd_ref[...].astype(BF16), preferred_element_type=F32)
        y_ref[...] = _pack_halves(y)

    @pl.when(i >= nu_ref[0])
    def _():
        y_ref[...] = jnp.zeros_like(y_ref)


def _moe(blk_expert, n_used, x_sorted, w_gate, w_up, w_down):
    n_blocks = blk_expert.shape[0]
    wspec = lambda shape: pl.BlockSpec((None,) + shape, lambda i, be, nu: (be[i], 0, 0))
    rows = pl.BlockSpec((MOE_ROWS, HALF), lambda i, be, nu: (i, 0))
    grid_spec = pltpu.PrefetchScalarGridSpec(
        num_scalar_prefetch=2,
        grid=(n_blocks,),
        in_specs=[rows, wspec((D_MODEL, D_FF)), wspec((D_MODEL, D_FF)), wspec((D_FF, D_MODEL))],
        out_specs=rows,
    )
    return pl.pallas_call(
        _moe_kernel,
        grid_spec=grid_spec,
        out_shape=jax.ShapeDtypeStruct(x_sorted.shape, U32),
        compiler_params=pltpu.CompilerParams(
            dimension_semantics=("arbitrary",), vmem_limit_bytes=VMEM_LIMIT),
        name="moe",
    )(blk_expert, n_used, x_sorted, w_gate, w_up, w_down)


def _combine_kernel(h2_ref, rw_ref, g_ref, y0_ref, y1_ref, o_ref):
    rw = rw_ref[...]
    h = h2_ref[...] + rw[:, 0:1] * _unpack_halves(y0_ref[...]) + rw[:, 1:2] * _unpack_halves(y1_ref[...])
    o_ref[...] = h * lax.rsqrt(jnp.mean(h * h, axis=-1, keepdims=True) + EPS) * g_ref[...]


def _combine(h2, rweights, final_g, y):
    t = h2.shape[0]
    tm = TOKEN_TM
    row = lambda i: (i, 0)
    return pl.pallas_call(
        _combine_kernel,
        grid=(t // tm,),
        in_specs=[pl.BlockSpec((tm, D_MODEL), row),
                  pl.BlockSpec((tm, LANES), row),
                  pl.BlockSpec((1, D_MODEL), lambda i: (0, 0)),
                  pl.BlockSpec((tm, HALF), row),
                  pl.BlockSpec((tm, HALF), lambda i: (i + t // tm, 0))],
        out_specs=pl.BlockSpec((tm, D_MODEL), row),
        out_shape=jax.ShapeDtypeStruct((t, D_MODEL), F32),
        compiler_params=pltpu.CompilerParams(
            dimension_semantics=("arbitrary",), vmem_limit_bytes=VMEM_LIMIT),
        name="combine",
    )(h2, rweights, final_g, y, y)


def _route(logits, t):
    gl = logits[:, :N_GROUPS]
    el = logits[:, N_GROUPS:N_GROUPS + N_EXPERTS].reshape(t, N_GROUPS, EXPERTS_PER_GROUP)
    g_prob = jax.nn.softmax(gl, axis=-1)
    grp = jnp.argmax(gl, axis=-1)
    p_grp = jnp.take_along_axis(g_prob, grp[:, None], axis=-1)
    sel = jnp.take_along_axis(el, grp[:, None, None], axis=1)[:, 0]
    top_p, top_i = lax.top_k(jax.nn.softmax(sel, axis=-1), 2)
    weights = p_grp * top_p / jnp.sum(top_p, axis=-1, keepdims=True)
    e_flat = (grp[:, None] * EXPERTS_PER_GROUP + top_i).reshape(2 * t).astype(jnp.int32)

    a = 2 * t
    n_blocks = a // MOE_ROWS + N_EXPERTS
    experts = jnp.arange(N_EXPERTS, dtype=jnp.int32)
    order = jnp.argsort(e_flat).astype(jnp.int32)
    counts = jnp.sum((e_flat[:, None] == experts[None, :]).astype(jnp.int32), axis=0)
    start = jnp.cumsum(counts) - counts
    padded = ((counts + MOE_ROWS - 1) // MOE_ROWS) * MOE_ROWS
    pend = jnp.cumsum(padded)
    pstart = pend - padded
    blk_first = jnp.arange(n_blocks, dtype=jnp.int32) * MOE_ROWS
    blk_expert = jnp.minimum(jnp.sum((pend[None, :] <= blk_first[:, None]).astype(jnp.int32), axis=1),
                             N_EXPERTS - 1).astype(jnp.int32)
    p = jnp.arange(n_blocks * MOE_ROWS, dtype=jnp.int32)
    e_p = jnp.repeat(blk_expert, MOE_ROWS)
    j = p - pstart[e_p]
    valid = j < counts[e_p]
    src = order[jnp.clip(start[e_p] + j, 0, a - 1)]
    row_tok = jnp.where(valid, src // 2, 0)
    rank = jnp.argsort(order).astype(jnp.int32)
    dest = pstart[e_flat] + rank - start[e_flat]
    dest = dest.reshape(t, 2).T.reshape(a)
    n_used = (pend[-1] // MOE_ROWS).astype(jnp.int32).reshape(1)
    rweights = jnp.zeros((t, LANES), F32).at[:, :2].set(weights)
    return row_tok, dest, blk_expert, n_used, rweights


def kernel(x, meta_tokens, hg_lb_logits, norm_mix_g, w_in, gd_conv_w, gd_A_log, gd_dt_bias, hg_norm_g, gd_norm_g, hg_up, gd_up, w_out, norm_ffn_g, router_group_w, router_group_b, router_expert_w, router_expert_b, w_gate, w_up, w_down, final_norm_g):
    bsz, seq, d = x.shape
    t = bsz * seq
    x2d = x.reshape(t, d)

    lb = jnp.cumsum(jax.nn.softmax(hg_lb_logits.astype(F32), axis=0), axis=0)[0].reshape(1, WIDTH)
    w = w_in[0].astype(F32)
    n_h, n_g = 4 * WIDTH, 8 * WIDTH
    n_s = n_g + 2 * HEADS
    w_groups = {
        "hgrn": w[:, :n_h],
        "gdn": jnp.concatenate([w[:, n_h:n_s], jnp.zeros((d, LANES - 2 * HEADS), F32)], axis=1),
        "gates": w[:, n_s:],
    }
    g_mix = norm_mix_g[0].reshape(1, d)
    meta_blk = jnp.concatenate([jnp.zeros((CHUNK - N_META, d), F32), meta_tokens.astype(F32)], axis=0)

    proj = {k: _norm_proj(x2d, g_mix, lb, wk, PROJ_TM, k) for k, wk in w_groups.items()}
    proj_meta = {k: _norm_proj(meta_blk, g_mix, lb, w_groups[k], CHUNK, k) for k in ("hgrn", "gdn")}

    o_a = _hgrn(proj_meta["hgrn"], proj["hgrn"].reshape(bsz, seq, -1), hg_norm_g[0].reshape(1, HEAD_DIM))

    def scalar_rows(p, nb, nc):
        s = p[:, 4 * WIDTH:4 * WIDTH + 2 * HEADS].reshape(nb, nc, CHUNK, 2, N_STACKS, HEADS_PER_STACK)
        return s.transpose(0, 1, 4, 3, 5, 2).reshape(nb, nc, N_STACKS, 2, STACK)

    s_meta = jnp.broadcast_to(scalar_rows(proj_meta["gdn"], 1, 1), (bsz, 1, N_STACKS, 2, STACK))
    scal = jnp.concatenate([s_meta, scalar_rows(proj["gdn"], bsz, seq // CHUNK)], axis=1)
    a_log_row = jnp.repeat(gd_A_log[0].astype(F32), CHUNK).reshape(N_STACKS, 1, STACK)
    dt_row = jnp.repeat(gd_dt_bias[0].astype(F32), CHUNK).reshape(N_STACKS, 1, STACK)
    o_b = _gdn(proj_meta["gdn"], proj["gdn"].reshape(bsz, seq, -1), scal, a_log_row, dt_row,
               gd_conv_w[0].astype(F32), gd_norm_g[0].reshape(1, HEAD_DIM))

    rw = jnp.zeros((d, LANES), F32)
    rw = rw.at[:, :N_GROUPS].set(router_group_w[0]).at[:, N_GROUPS:N_GROUPS + N_EXPERTS].set(router_expert_w[0])
    rw_hi = rw.astype(BF16)
    rw_lo = (rw - rw_hi.astype(F32)).astype(BF16)
    rb = jnp.zeros((1, LANES), F32)
    rb = rb.at[0, :N_GROUPS].set(router_group_b[0]).at[0, N_GROUPS:N_GROUPS + N_EXPERTS].set(router_expert_b[0])
    h2, xn, logits = _merge(o_a.reshape(t, WIDTH), o_b.reshape(t, WIDTH), proj["gates"], x2d,
                            hg_up[0].astype(BF16), gd_up[0].astype(BF16), w_out[0].astype(BF16),
                            norm_ffn_g[0].reshape(1, d), rw_hi, rw_lo, rb)

    row_tok, dest, blk_expert, n_used, rweights = _route(logits, t)
    x_sorted = _sc_gather(xn, row_tok)
    y_sorted = _moe(blk_expert, n_used, x_sorted,
                    w_gate[0].astype(F32), w_up[0].astype(F32), w_down[0].astype(F32))
    y_tok = _sc_gather(y_sorted, dest)
    out = _combine(h2, rweights, final_norm_g.reshape(1, d), y_tok)
    return out.reshape(bsz, seq, d)
```

```python
import functools

import numpy as np
import jax
import jax.numpy as jnp
from jax import lax
from jax.experimental import pallas as pl
from jax.experimental.pallas import tpu as pltpu
from jax.experimental.pallas import tpu_sc as plsc

F32 = jnp.float32
BF16 = jnp.bfloat16
U32 = jnp.uint32

D_MODEL = 1024
N_META = 16
CHUNK = 64
EPS = 1e-6
HEADS = 4
HEAD_DIM = 128
WIDTH = HEADS * HEAD_DIM
HEADS_PER_STACK = 2
STACK = HEADS_PER_STACK * CHUNK
N_STACKS = HEADS // HEADS_PER_STACK
CONV_W = 4
N_GROUPS = 4
EXPERTS_PER_GROUP = 8
N_EXPERTS = N_GROUPS * EXPERTS_PER_GROUP
D_FF = 512
N_BIG = 12 * WIDTH
LANES = 128
SUBLANES = 8
HALF = D_MODEL // 2
MOE_ROWS = 256
SC_CORES, SC_SUBCORES = 2, 16
SC_WORKERS = SC_CORES * SC_SUBCORES
SC_CHUNK = 128
PROJ_TM, PROJ_TN = 512, 2048
TOKEN_TM = 256
VMEM_LIMIT = 48 * 1024 * 1024


def _bdot(a, b):
    return lax.dot_general(a.astype(BF16), b.astype(BF16), (((2,), (1,)), ((0,), (0,))),
                           preferred_element_type=F32)


def _bdot_nt(a, b):
    return lax.dot_general(a.astype(BF16), b.astype(BF16), (((2,), (2,)), ((0,), (0,))),
                           preferred_element_type=F32)


def _bdot_tn(a, b):
    return lax.dot_general(a.astype(BF16), b.astype(BF16), (((1,), (1,)), ((0,), (0,))),
                           preferred_element_type=F32)


def _split3(x):
    hi = x.astype(BF16)
    r1 = x - hi.astype(F32)
    mid = r1.astype(BF16)
    lo = (r1 - mid.astype(F32)).astype(BF16)
    return hi, mid, lo


def _sigmoid(x):
    return 1.0 / (1.0 + jnp.exp(-x))


def _silu(x):
    return x * _sigmoid(x)


def _softplus(x):
    return jnp.maximum(x, 0.0) + jnp.log(1.0 + jnp.exp(-jnp.abs(x)))


def _stack_heads(x, bsz):
    x3 = x.reshape(bsz, CHUNK, WIDTH)
    y = jnp.concatenate([x3[:, :, h * HEAD_DIM:(h + 1) * HEAD_DIM] for h in range(HEADS)], axis=1)
    return y.reshape(bsz * HEADS * CHUNK, HEAD_DIM)


def _unstack_heads(x, bsz):
    x3 = x.reshape(bsz, HEADS * CHUNK, HEAD_DIM)
    return jnp.concatenate([x3[:, h * CHUNK:(h + 1) * CHUNK, :] for h in range(HEADS)], axis=2)


def _to_stacks(a, bsz):
    return a.reshape(bsz * N_STACKS, STACK, a.shape[-1])


def _to_heads(a, bsz):
    return a.reshape(bsz * HEADS, CHUNK, a.shape[-1])


def _roll_rows(x, shift):
    n = x.shape[0]
    return pltpu.roll(x, shift % n, axis=0)


def _load_chunk(is_meta, meta_ref, real_ref, bsz):
    x = jnp.where(is_meta, meta_ref[...][None], real_ref[...])
    return x.reshape(bsz * CHUNK, WIDTH)


def _pack_halves(x):
    lo = lax.bitcast_convert_type(x[:, :HALF].astype(BF16).astype(F32), U32)
    hi = lax.bitcast_convert_type(x[:, HALF:].astype(BF16).astype(F32), U32)
    return (lo >> 16) | (hi & jnp.uint32(0xFFFF0000))


def _unpack_halves(p):
    lo = lax.bitcast_convert_type(p << 16, F32)
    hi = lax.bitcast_convert_type(p & jnp.uint32(0xFFFF0000), F32)
    return jnp.concatenate([lo, hi], axis=1)


def _norm_proj_kernel(x_ref, g_ref, lb_ref, w_ref, o_ref, wbf_ref, *, group):
    @pl.when(pl.program_id(0) == 0)
    def _():
        wbf_ref[...] = w_ref[...].astype(BF16)

    x = x_ref[...]
    xn = x * lax.rsqrt(jnp.mean(x * x, axis=-1, keepdims=True) + EPS) * g_ref[...]
    acc = jnp.dot(xn.astype(BF16), wbf_ref[...], preferred_element_type=F32)
    w1, w2, w3, w4 = WIDTH, 2 * WIDTH, 3 * WIDTH, 4 * WIDTH
    if group == "hgrn":
        lb = lb_ref[...]
        o_ref[:, :w1] = _silu(acc[:, :w1])
        o_ref[:, w1:w2] = lb + (1.0 - lb) * _sigmoid(acc[:, w1:w2])
        o_ref[:, w2:w3] = acc[:, w2:w3]
        o_ref[:, w3:] = _silu(acc[:, w3:])
    elif group == "gdn":
        o_ref[:, :w3] = acc[:, :w3]
        o_ref[:, w3:w4] = _silu(acc[:, w3:w4])
        o_ref[:, w4:] = acc[:, w4:]
    else:
        o_ref[...] = _sigmoid(acc)


def _norm_proj(x2d, g, lb, w, tm, group):
    m, k = x2d.shape
    n = w.shape[1]
    const = lambda i: (0, 0)
    return pl.pallas_call(
        functools.partial(_norm_proj_kernel, group=group),
        grid=(m // tm,),
        in_specs=[pl.BlockSpec((tm, k), lambda i: (i, 0)),
                  pl.BlockSpec((1, k), const),
                  pl.BlockSpec((1, WIDTH), const),
                  pl.BlockSpec((k, n), const)],
        out_specs=pl.BlockSpec((tm, n), lambda i: (i, 0)),
        out_shape=jax.ShapeDtypeStruct((m, n), F32),
        scratch_shapes=[pltpu.VMEM((k, n), BF16)],
        compiler_params=pltpu.CompilerParams(
            dimension_semantics=("arbitrary",), vmem_limit_bytes=VMEM_LIMIT),
        name="norm_proj_" + group,
    )(x2d, g, lb, w)


HGRN_LEVELS = 6
HGRN_DIAG = HGRN_LEVELS


def _hgrn_level_table():
    r = np.arange(STACK)
    t, hd = r % CHUNK, r // CHUNK
    x = t[:, None] ^ t[None, :]
    lv = np.floor(np.log2(np.maximum(x, 1))).astype(np.int32)
    valid = (hd[:, None] == hd[None, :]) & (t[:, None] > t[None, :])
    lv = np.where(valid, lv, -1)
    lv = np.where(r[:, None] == r[None, :], HGRN_DIAG, lv)
    return jnp.asarray(lv, dtype=jnp.int32)


def _hgrn_kernel(lv_ref, g_ref, mq_ref, mf_ref, mi_ref, mg_ref,
                 rq_ref, rf_ref, ri_ref, rg_ref, o_ref, st_ref, *, bsz):
    c = pl.program_id(0)

    @pl.when(c == 0)
    def _():
        st_ref[...] = jnp.zeros_like(st_ref)

    is_meta = c == 0
    q = _load_chunk(is_meta, mq_ref, rq_ref, bsz)
    f = _load_chunk(is_meta, mf_ref, rf_ref, bsz)
    v = _load_chunk(is_meta, mi_ref, ri_ref, bsz)
    gate = _load_chunk(is_meta, mg_ref, rg_ref, bsz)

    trow = lax.broadcasted_iota(jnp.int32, (bsz * CHUNK, WIDTH), 0) & (CHUNK - 1)
    b = jnp.log(f)
    s = 1
    while s < CHUNK:
        b = b + jnp.where(trow >= s, _roll_rows(b, s), 0.0)
        s *= 2

    qs, ks, vs, bs = (_stack_heads(a, bsz) for a in (q, 1.0 - f, v, b))
    t = lax.broadcasted_iota(jnp.int32, qs.shape, 0) & (CHUNK - 1)
    lv = lv_ref[...][None]
    to3 = functools.partial(_to_stacks, bsz=bsz)

    attn = jnp.where(lv == HGRN_DIAG, _bdot_nt(to3(qs), to3(ks)), 0.0)
    bref = bs
    bnext = _roll_rows(bs, -1)
    for li in range(HGRN_LEVELS):
        m = 1 << li
        if li > 0:
            half = m // 2
            upper = (t & (m - 1)) >= half
            bref = jnp.where(upper, _roll_rows(bref, half), bref)
            bnext = jnp.where(upper, bnext, _roll_rows(bnext, -half))
        qt = qs * jnp.exp(bs - bref)
        kt = ks * jnp.exp(jnp.minimum(bnext - bs, 0.0))
        attn = jnp.where(lv == li, _bdot_nt(to3(qt), to3(kt)), attn)

    o = _bdot(attn, to3(vs))

    to_g = functools.partial(_to_heads, bsz=bsz)
    st = st_ref[...]
    bs_g = to_g(bs)
    o_inter = _bdot_nt(to_g(qs * jnp.exp(bs)), st)
    b_end = bs_g[:, CHUNK - 1:CHUNK, :]
    ke = to_g(ks) * jnp.exp(b_end - bs_g)
    st_ref[...] = st * jnp.exp(b_end) + _bdot_tn(to_g(vs), ke)
    o = (to_g(o) + o_inter).reshape(qs.shape)

    o = o * lax.rsqrt(jnp.mean(o * o, axis=-1, keepdims=True) + EPS) * g_ref[...]
    o = o * _stack_heads(gate, bsz)
    o_ref[...] = _unstack_heads(o, bsz).astype(o_ref.dtype)


def _chunk_specs(bsz, col0):
    meta = [pl.BlockSpec((CHUNK, WIDTH), functools.partial(lambda c, j: (0, j), j=col0 + j))
            for j in range(4)]
    real = [pl.BlockSpec((bsz, CHUNK, WIDTH),
                         functools.partial(lambda c, j: (0, jnp.maximum(c - 1, 0), j), j=col0 + j))
            for j in range(4)]
    return meta + real


def _hgrn(proj_meta, proj_real, norm_g):
    bsz, seq, _ = proj_real.shape
    n_chunks = seq // CHUNK + 1
    const = lambda c: (0, 0)
    return pl.pallas_call(
        functools.partial(_hgrn_kernel, bsz=bsz),
        grid=(n_chunks,),
        in_specs=[pl.BlockSpec((STACK, STACK), const),
                  pl.BlockSpec((1, HEAD_DIM), const)] + _chunk_specs(bsz, 0),
        out_specs=pl.BlockSpec((bsz, CHUNK, WIDTH), lambda c: (0, jnp.maximum(c - 1, 0), 0)),
        out_shape=jax.ShapeDtypeStruct((bsz, seq, WIDTH), BF16),
        scratch_shapes=[pltpu.VMEM((bsz * HEADS, HEAD_DIM, HEAD_DIM), F32)],
        compiler_params=pltpu.CompilerParams(
            dimension_semantics=("arbitrary",), vmem_limit_bytes=VMEM_LIMIT),
        name="hgrn",
    )(_hgrn_level_table(), norm_g, *([proj_meta] * 4), *([proj_real] * 4))


def _gdn_cumsum_matrix():
    r = np.arange(STACK)
    u = (r[:, None] // CHUNK == r[None, :] // CHUNK) & (r[:, None] <= r[None, :])
    return jnp.asarray(u, dtype=BF16)


def _gdn_kernel(u_ref, alog_ref, dt_ref, cw_ref, g_ref, s_ref,
                mq_ref, mk_ref, mv_ref, mz_ref, rq_ref, rk_ref, rv_ref, rz_ref,
                o_ref, st_ref, prev_ref, *, bsz):
    c = pl.program_id(0)

    @pl.when(c == 0)
    def _():
        st_ref[...] = jnp.zeros_like(st_ref)
        prev_ref[...] = jnp.zeros_like(prev_ref)

    is_meta = c == 0
    xq = _load_chunk(is_meta, mq_ref, rq_ref, bsz)
    xk = _load_chunk(is_meta, mk_ref, rk_ref, bsz)
    xv = _load_chunk(is_meta, mv_ref, rv_ref, bsz)
    gate = _load_chunk(is_meta, mz_ref, rz_ref, bsz)

    x = jnp.concatenate([xq, xk, xv], axis=1)
    pv = prev_ref[...]
    trow = lax.broadcasted_iota(jnp.int32, x.shape, 0) & (CHUNK - 1)
    cw = cw_ref[...]
    acc = x * cw[CONV_W - 1:CONV_W, :]
    for d in range(1, CONV_W):
        xs = jnp.where(trow >= d, _roll_rows(x, d), _roll_rows(pv, d - CHUNK))
        acc = acc + xs * cw[CONV_W - 1 - d:CONV_W - d, :]
    prev_ref[...] = x
    y = _silu(acc)

    qs = _stack_heads(y[:, :WIDTH], bsz)
    ks = _stack_heads(y[:, WIDTH:2 * WIDTH], bsz)
    vs = _stack_heads(y[:, 2 * WIDTH:], bsz)
    qs = qs * lax.rsqrt(jnp.sum(qs * qs, axis=-1, keepdims=True) + EPS) * (HEAD_DIM ** -0.5)
    ks = ks * lax.rsqrt(jnp.sum(ks * ks, axis=-1, keepdims=True) + EPS)
    to3 = functools.partial(_to_stacks, bsz=bsz)
    qs, ks, vs = to3(qs), to3(ks), to3(vs)
    n_st = bsz * N_STACKS

    srow = s_ref[...].reshape(n_st, 2, STACK)
    per_stack = lambda ref: jnp.broadcast_to(ref[...][None], (bsz, N_STACKS, 1, STACK)).reshape(n_st, 1, STACK)
    beta_row = _sigmoid(srow[:, 0:1, :])
    g_row = -jnp.exp(per_stack(alog_ref)) * _softplus(srow[:, 1:2, :] + per_stack(dt_ref))
    n8 = n_st * SUBLANES
    g8 = jnp.broadcast_to(g_row, (n_st, SUBLANES, STACK)).reshape(n8, STACK)
    pieces = jnp.concatenate(_split3(g8), axis=0)
    cum = jnp.dot(pieces, u_ref[...], preferred_element_type=F32)
    cum = cum[0:n8] + cum[n8:2 * n8] + cum[2 * n8:]
    cum_row = cum.reshape(n_st, SUBLANES, STACK)[:, 0:1, :]

    r = lax.broadcasted_iota(jnp.int32, (STACK, STACK), 0)
    cc = lax.broadcasted_iota(jnp.int32, (STACK, STACK), 1)
    eye = (r == cc)[None]
    cum_col = jnp.sum(jnp.where(eye, cum_row, 0.0), axis=2, keepdims=True)
    beta_col = jnp.sum(jnp.where(eye, beta_row, 0.0), axis=2, keepdims=True)
    same = (r >> 6) == (cc >> 6)
    lower = (same & (r >= cc))[None]
    strict = (same & (r > cc))[None]
    ratio = jnp.exp(jnp.minimum(cum_col - cum_row, 0.0))

    kkt = _bdot_nt(ks, ks)
    qkt = _bdot_nt(qs, ks)
    lm = jnp.where(strict, beta_col * ratio * kkt, 0.0)

    ident = eye.astype(F32)
    l0 = jnp.where(((r >> 3) == (cc >> 3))[None], lm, 0.0)
    l2 = _bdot(l0, l0)
    l4 = _bdot(l2, l2)
    xinv = _bdot(_bdot(ident - l0, ident + l2), ident + l4)
    for lvl in (3, 4, 5):
        off = (((r >> (lvl + 1)) == (cc >> (lvl + 1))) & ((r >> lvl) != (cc >> lvl)))[None]
        xinv = xinv - _bdot(_bdot(xinv, jnp.where(off, lm, 0.0)), xinv)

    decay_col = jnp.exp(cum_col)
    rhs = jnp.concatenate([beta_col * vs, (beta_col * decay_col) * ks], axis=2)
    sol = _bdot(xinv, rhs)

    to_g = functools.partial(_to_heads, bsz=bsz)
    st = st_ref[...]
    u = to_g(sol[:, :, :HEAD_DIM]) - _bdot(to_g(sol[:, :, HEAD_DIM:]), st)
    o_inter = to_g(decay_col) * _bdot(to_g(qs), st)
    cum_g = to_g(cum_col)
    cum_end = cum_g[:, CHUNK - 1:CHUNK, :]
    kdec = to_g(ks) * jnp.exp(cum_end - cum_g)
    st_ref[...] = jnp.exp(cum_end) * st + _bdot_tn(kdec, u)
    attn = jnp.where(lower, qkt * ratio, 0.0)
    o = o_inter + to_g(_bdot(attn, to3(u)))
    o = o.reshape(bsz * HEADS * CHUNK, HEAD_DIM)

    o = o * lax.rsqrt(jnp.mean(o * o, axis=-1, keepdims=True) + EPS) * g_ref[...]
    o = o * _stack_heads(gate, bsz)
    o_ref[...] = _unstack_heads(o, bsz).astype(o_ref.dtype)


def _gdn(proj_meta, proj_real, scal, a_log_row, dt_row, conv_w, norm_g):
    bsz, seq, _ = proj_real.shape
    n_chunks = seq // CHUNK + 1
    const = lambda c: (0, 0)
    const3 = lambda c: (0, 0, 0)
    return pl.pallas_call(
        functools.partial(_gdn_kernel, bsz=bsz),
        grid=(n_chunks,),
        in_specs=[pl.BlockSpec((STACK, STACK), const),
                  pl.BlockSpec((N_STACKS, 1, STACK), const3),
                  pl.BlockSpec((N_STACKS, 1, STACK), const3),
                  pl.BlockSpec((CONV_W, 3 * WIDTH), const),
                  pl.BlockSpec((1, HEAD_DIM), const),
                  pl.BlockSpec((bsz, None, N_STACKS, 2, STACK), lambda c: (0, c, 0, 0, 0))]
                 + _chunk_specs(bsz, 0),
        out_specs=pl.BlockSpec((bsz, CHUNK, WIDTH), lambda c: (0, jnp.maximum(c - 1, 0), 0)),
        out_shape=jax.ShapeDtypeStruct((bsz, seq, WIDTH), BF16),
        scratch_shapes=[pltpu.VMEM((bsz * HEADS, HEAD_DIM, HEAD_DIM), F32),
                        pltpu.VMEM((bsz * CHUNK, 3 * WIDTH), F32)],
        compiler_params=pltpu.CompilerParams(
            dimension_semantics=("arbitrary",), vmem_limit_bytes=VMEM_LIMIT),
        name="gdn",
    )(_gdn_cumsum_matrix(), a_log_row, dt_row, conv_w, norm_g, scal,
      *([proj_meta] * 4), *([proj_real] * 4))


ROUTE_LANE0 = N_GROUPS
ROUTE_NEG = -1e30


def _route_tile(lg, ltri, carry):
    lane = lax.broadcasted_iota(jnp.int32, lg.shape, 1).astype(F32)
    first = lambda mask: jnp.min(jnp.where(mask, lane, float(LANES)), axis=1, keepdims=True)
    top = lambda mask: jnp.max(jnp.where(mask, lg, ROUTE_NEG), axis=1, keepdims=True)

    is_g = lane < N_GROUPS
    gmax = top(is_g)
    grp = first(is_g & (lg == gmax))
    p_grp = 1.0 / jnp.sum(jnp.where(is_g, jnp.exp(lg - gmax), 0.0), axis=1, keepdims=True)

    lo = ROUTE_LANE0 + grp * EXPERTS_PER_GROUP
    in_grp = (lane >= lo) & (lane < lo + EXPERTS_PER_GROUP)
    m1 = top(in_grp)
    i1 = first(in_grp & (lg == m1))
    rest = in_grp & (lane != i1)
    m2 = top(rest)
    i2 = first(rest & (lg == m2))
    r = jnp.exp(m2 - m1)
    w0 = p_grp / (1.0 + r)
    w1 = p_grp * r / (1.0 + r)

    hot0 = (lane == i1).astype(F32)
    hot1 = (lane == i2).astype(F32)
    both = hot0 + hot1
    before = jnp.dot(ltri, both.astype(BF16), preferred_element_type=F32) + carry
    rank0 = jnp.sum(hot0 * before, axis=1, keepdims=True)
    rank1 = jnp.sum(hot1 * before, axis=1, keepdims=True)
    cols = (w0, w1, i1 - ROUTE_LANE0, i2 - ROUTE_LANE0, rank0, rank1)
    info = jnp.zeros_like(lg)
    for k, col in enumerate(cols):
        info = jnp.where(lane == k, col, info)
    return info, carry + jnp.sum(both, axis=0, keepdims=True)


def _merge_kernel(oa_ref, ob_ref, ga_ref, gb_ref, x_ref, hgup_ref, gdup_ref, wout_ref,
                  ng_ref, rwh_ref, rwl_ref, rb_ref, ltri_ref, h2_ref, xn_ref, info_ref, cnt_ref, carry_ref):
    @pl.when(pl.program_id(0) == 0)
    def _():
        carry_ref[...] = jnp.zeros_like(carry_ref)

    ua = jnp.dot(oa_ref[...], hgup_ref[...], preferred_element_type=F32)
    ub = jnp.dot(ob_ref[...], gdup_ref[...], preferred_element_type=F32)
    merged = ga_ref[...] * ua + gb_ref[...] * ub
    h2 = x_ref[...] + jnp.dot(merged.astype(BF16), wout_ref[...], preferred_element_type=F32)
    h2_ref[...] = h2
    xn = h2 * lax.rsqrt(jnp.mean(h2 * h2, axis=-1, keepdims=True) + EPS) * ng_ref[...]
    xn_ref[...] = _pack_halves(xn)
    xh = xn.astype(BF16)
    xl = (xn - xh.astype(F32)).astype(BF16)
    wh, wl = rwh_ref[...], rwl_ref[...]
    lg = (jnp.dot(xh, wh, preferred_element_type=F32)
          + jnp.dot(xh, wl, preferred_element_type=F32)
          + jnp.dot(xl, wh, preferred_element_type=F32)) + rb_ref[...]
    info, carry = _route_tile(lg, ltri_ref[...], carry_ref[...])
    info_ref[...] = info
    carry_ref[...] = carry
    cnt_ref[...] = carry


def _merge(o_a, o_b, proj, x2d, hg_up, gd_up, w_out, norm_g, rw_hi, rw_lo, rb):
    t = x2d.shape[0]
    tm = TOKEN_TM
    row = lambda i: (i, 0)
    const = lambda i: (0, 0)
    ltri = jnp.asarray(np.tril(np.ones((tm, tm), np.float32), -1), dtype=BF16)
    return pl.pallas_call(
        _merge_kernel,
        grid=(t // tm,),
        in_specs=[pl.BlockSpec((tm, WIDTH), row),
                  pl.BlockSpec((tm, WIDTH), row),
                  pl.BlockSpec((tm, D_MODEL), lambda i: (i, 0)),
                  pl.BlockSpec((tm, D_MODEL), lambda i: (i, 1)),
                  pl.BlockSpec((tm, D_MODEL), row),
                  pl.BlockSpec((WIDTH, D_MODEL), const),
                  pl.BlockSpec((WIDTH, D_MODEL), const),
                  pl.BlockSpec((D_MODEL, D_MODEL), const),
                  pl.BlockSpec((1, D_MODEL), const),
                  pl.BlockSpec((D_MODEL, LANES), const),
                  pl.BlockSpec((D_MODEL, LANES), const),
                  pl.BlockSpec((1, LANES), const),
                  pl.BlockSpec((tm, tm), const)],
        out_specs=[pl.BlockSpec((tm, D_MODEL), row),
                   pl.BlockSpec((tm, HALF), row),
                   pl.BlockSpec((tm, LANES), row),
                   pl.BlockSpec((1, LANES), const)],
        out_shape=[jax.ShapeDtypeStruct((t, D_MODEL), F32),
                   jax.ShapeDtypeStruct((t, HALF), U32),
                   jax.ShapeDtypeStruct((t, LANES), F32),
                   jax.ShapeDtypeStruct((1, LANES), F32)],
        scratch_shapes=[pltpu.VMEM((1, LANES), F32)],
        compiler_params=pltpu.CompilerParams(
            dimension_semantics=("arbitrary",), vmem_limit_bytes=VMEM_LIMIT),
        name="merge",
    )(o_a, o_b, proj, proj, x2d, hg_up, gd_up, w_out, norm_g, rw_hi, rw_lo, rb, ltri)


def _sc_gather(table, idx):
    n_idx = idx.shape[0]
    cols = table.shape[1]
    per_worker = n_idx // SC_WORKERS
    assert n_idx % (SC_WORKERS * SC_CHUNK) == 0
    mesh = plsc.VectorSubcoreMesh(core_axis_name="c", subcore_axis_name="s")

    @functools.partial(
        pl.kernel, mesh=mesh,
        out_type=jax.ShapeDtypeStruct((n_idx, cols), table.dtype),
        scratch_types=[pltpu.VMEM((SC_CHUNK,), jnp.int32),
                       pltpu.VMEM((SC_CHUNK, cols), table.dtype),
                       pltpu.SemaphoreType.DMA],
    )
    def gather(table_hbm, idx_hbm, out_hbm, idx_v, rows_v, sem):
        worker = lax.axis_index("s") * SC_CORES + lax.axis_index("c")
        base = worker * per_worker

        @pl.loop(0, per_worker // SC_CHUNK)
        def _(c):
            off = pl.multiple_of(base + c * SC_CHUNK, SC_CHUNK)
            pltpu.sync_copy(idx_hbm.at[pl.ds(off, SC_CHUNK)], idx_v)
            pltpu.async_copy(table_hbm.at[idx_v], rows_v, sem).wait()
            pltpu.sync_copy(rows_v, out_hbm.at[pl.ds(off, SC_CHUNK)])

    return gather(table, idx)


def _sc_dispatch(rows, dest, n_out):
    t, cols = rows.shape
    per_worker = t // SC_WORKERS
    assert t % (SC_WORKERS * SC_CHUNK) == 0 and dest.shape[0] == 2 * t
    mesh = plsc.VectorSubcoreMesh(core_axis_name="c", subcore_axis_name="s")

    @functools.partial(
        pl.kernel, mesh=mesh,
        out_type=jax.ShapeDtypeStruct((n_out, cols), rows.dtype),
        scratch_types=[pltpu.VMEM((SC_CHUNK,), jnp.int32),
                       pltpu.VMEM((SC_CHUNK, cols), rows.dtype)],
    )
    def dispatch(rows_hbm, dest_hbm, out_hbm, idx_v, rows_v):
        worker = lax.axis_index("s") * SC_CORES + lax.axis_index("c")
        base = worker * per_worker

        @pl.loop(0, per_worker // SC_CHUNK)
        def _(c):
            off = pl.multiple_of(base + c * SC_CHUNK, SC_CHUNK)
            pltpu.sync_copy(rows_hbm.at[pl.ds(off, SC_CHUNK)], rows_v)
            for slot in range(2):
                pltpu.sync_copy(dest_hbm.at[pl.ds(slot * t + off, SC_CHUNK)], idx_v)
                pltpu.sync_copy(rows_v, out_hbm.at[idx_v])

    return dispatch(rows, dest)


def _moe_kernel(be_ref, nu_ref, x_ref, wg_ref, wu_ref, wd_ref, y_ref):
    i = pl.program_id(0)

    @pl.when(i < nu_ref[0])
    def _():
        xb = _unpack_halves(x_ref[...]).astype(BF16)
        a = jnp.dot(xb, wg_ref[...].astype(BF16), preferred_element_type=F32)
        u = jnp.dot(xb, wu_ref[...].astype(BF16), preferred_element_type=F32)
        y = jnp.dot((_silu(a) * u).astype(BF16), wd_ref[...].astype(BF16), preferred_element_type=F32)
        y_ref[...] = _pack_halves(y)

    @pl.when(i >= nu_ref[0])
    def _():
        y_ref[...] = jnp.zeros_like(y_ref)


def _moe(blk_expert, n_used, x_sorted, w_gate, w_up, w_down):
    n_blocks = blk_expert.shape[0]
    wspec = lambda shape: pl.BlockSpec((None,) + shape, lambda i, be, nu: (be[i], 0, 0))
    rows = pl.BlockSpec((MOE_ROWS, HALF), lambda i, be, nu: (i, 0))
    grid_spec = pltpu.PrefetchScalarGridSpec(
        num_scalar_prefetch=2,
        grid=(n_blocks,),
        in_specs=[rows, wspec((D_MODEL, D_FF)), wspec((D_MODEL, D_FF)), wspec((D_FF, D_MODEL))],
        out_specs=rows,
    )
    return pl.pallas_call(
        _moe_kernel,
        grid_spec=grid_spec,
        out_shape=jax.ShapeDtypeStruct(x_sorted.shape, U32),
        compiler_params=pltpu.CompilerParams(
            dimension_semantics=("arbitrary",), vmem_limit_bytes=VMEM_LIMIT),
        name="moe",
    )(blk_expert, n_used, x_sorted, w_gate, w_up, w_down)


def _combine_kernel(h2_ref, rw_ref, g_ref, y0_ref, y1_ref, o_ref):
    rw = rw_ref[...]
    h = h2_ref[...] + rw[:, 0:1] * _unpack_halves(y0_ref[...]) + rw[:, 1:2] * _unpack_halves(y1_ref[...])
    o_ref[...] = h * lax.rsqrt(jnp.mean(h * h, axis=-1, keepdims=True) + EPS) * g_ref[...]


def _combine(h2, rweights, final_g, y):
    t = h2.shape[0]
    tm = TOKEN_TM
    row = lambda i: (i, 0)
    return pl.pallas_call(
        _combine_kernel,
        grid=(t // tm,),
        in_specs=[pl.BlockSpec((tm, D_MODEL), row),
                  pl.BlockSpec((tm, LANES), row),
                  pl.BlockSpec((1, D_MODEL), lambda i: (0, 0)),
                  pl.BlockSpec((tm, HALF), row),
                  pl.BlockSpec((tm, HALF), lambda i: (i + t // tm, 0))],
        out_specs=pl.BlockSpec((tm, D_MODEL), row),
        out_shape=jax.ShapeDtypeStruct((t, D_MODEL), F32),
        compiler_params=pltpu.CompilerParams(
            dimension_semantics=("arbitrary",), vmem_limit_bytes=VMEM_LIMIT),
        name="combine",
    )(h2, rweights, final_g, y, y)


def _block_layout(info, counts_row, t):
    n_blocks = 2 * t // MOE_ROWS + N_EXPERTS
    counts = counts_row[0, ROUTE_LANE0:ROUTE_LANE0 + N_EXPERTS].astype(jnp.int32)
    padded = ((counts + MOE_ROWS - 1) // MOE_ROWS) * MOE_ROWS
    pend = jnp.cumsum(padded)
    pstart = pend - padded
    blk_first = jnp.arange(n_blocks, dtype=jnp.int32) * MOE_ROWS
    blk_expert = jnp.minimum(jnp.sum((pend[None, :] <= blk_first[:, None]).astype(jnp.int32), axis=1),
                             N_EXPERTS - 1).astype(jnp.int32)
    n_used = (pend[-1] // MOE_ROWS).astype(jnp.int32).reshape(1)
    expert = info[:, 2:4].astype(jnp.int32)
    rank = info[:, 4:6].astype(jnp.int32)
    dest = (pstart[expert] + rank).T.reshape(2 * t)
    return dest, blk_expert, n_used, n_blocks * MOE_ROWS


def kernel(x, meta_tokens, hg_lb_logits, norm_mix_g, w_in, gd_conv_w, gd_A_log, gd_dt_bias, hg_norm_g, gd_norm_g, hg_up, gd_up, w_out, norm_ffn_g, router_group_w, router_group_b, router_expert_w, router_expert_b, w_gate, w_up, w_down, final_norm_g):
    bsz, seq, d = x.shape
    t = bsz * seq
    x2d = x.reshape(t, d)

    lb = jnp.cumsum(jax.nn.softmax(hg_lb_logits.astype(F32), axis=0), axis=0)[0].reshape(1, WIDTH)
    w = w_in[0].astype(F32)
    n_h, n_g = 4 * WIDTH, 8 * WIDTH
    n_s = n_g + 2 * HEADS
    w_groups = {
        "hgrn": w[:, :n_h],
        "gdn": jnp.concatenate([w[:, n_h:n_s], jnp.zeros((d, LANES - 2 * HEADS), F32)], axis=1),
        "gates": w[:, n_s:],
    }
    g_mix = norm_mix_g[0].reshape(1, d)
    meta_blk = jnp.concatenate([jnp.zeros((CHUNK - N_META, d), F32), meta_tokens.astype(F32)], axis=0)

    proj = {k: _norm_proj(x2d, g_mix, lb, wk, PROJ_TM, k) for k, wk in w_groups.items()}
    proj_meta = {k: _norm_proj(meta_blk, g_mix, lb, w_groups[k], CHUNK, k) for k in ("hgrn", "gdn")}

    o_a = _hgrn(proj_meta["hgrn"], proj["hgrn"].reshape(bsz, seq, -1), hg_norm_g[0].reshape(1, HEAD_DIM))

    def scalar_rows(p, nb, nc):
        s = p[:, 4 * WIDTH:4 * WIDTH + 2 * HEADS].reshape(nb, nc, CHUNK, 2, N_STACKS, HEADS_PER_STACK)
        return s.transpose(0, 1, 4, 3, 5, 2).reshape(nb, nc, N_STACKS, 2, STACK)

    s_meta = jnp.broadcast_to(scalar_rows(proj_meta["gdn"], 1, 1), (bsz, 1, N_STACKS, 2, STACK))
    scal = jnp.concatenate([s_meta, scalar_rows(proj["gdn"], bsz, seq // CHUNK)], axis=1)
    a_log_row = jnp.repeat(gd_A_log[0].astype(F32), CHUNK).reshape(N_STACKS, 1, STACK)
    dt_row = jnp.repeat(gd_dt_bias[0].astype(F32), CHUNK).reshape(N_STACKS, 1, STACK)
    o_b = _gdn(proj_meta["gdn"], proj["gdn"].reshape(bsz, seq, -1), scal, a_log_row, dt_row,
               gd_conv_w[0].astype(F32), gd_norm_g[0].reshape(1, HEAD_DIM))

    rw = jnp.zeros((d, LANES), F32)
    rw = rw.at[:, :N_GROUPS].set(router_group_w[0]).at[:, N_GROUPS:N_GROUPS + N_EXPERTS].set(router_expert_w[0])
    rw_hi = rw.astype(BF16)
    rw_lo = (rw - rw_hi.astype(F32)).astype(BF16)
    rb = jnp.zeros((1, LANES), F32)
    rb = rb.at[0, :N_GROUPS].set(router_group_b[0]).at[0, N_GROUPS:N_GROUPS + N_EXPERTS].set(router_expert_b[0])
    h2, xn, info, counts = _merge(o_a.reshape(t, WIDTH), o_b.reshape(t, WIDTH), proj["gates"], x2d,
                                  hg_up[0].astype(BF16), gd_up[0].astype(BF16), w_out[0].astype(BF16),
                                  norm_ffn_g[0].reshape(1, d), rw_hi, rw_lo, rb)

    dest, blk_expert, n_used, n_rows = _block_layout(info, counts, t)
    x_sorted = _sc_dispatch(xn, dest, n_rows)
    y_sorted = _moe(blk_expert, n_used, x_sorted,
                    w_gate[0].astype(F32), w_up[0].astype(F32), w_down[0].astype(F32))
    y_tok = _sc_gather(y_sorted, dest)
    out = _combine(h2, info, final_norm_g.reshape(1, d), y_tok)
    return out.reshape(bsz, seq, d)
```

```python
import functools

import numpy as np
import jax
import jax.numpy as jnp
from jax import lax
from jax.experimental import pallas as pl
from jax.experimental.pallas import tpu as pltpu
from jax.experimental.pallas import tpu_sc as plsc

F32 = jnp.float32
BF16 = jnp.bfloat16
U32 = jnp.uint32

D_MODEL = 1024
N_META = 16
CHUNK = 64
EPS = 1e-6
HEADS = 4
HEAD_DIM = 128
WIDTH = HEADS * HEAD_DIM
HEADS_PER_STACK = 2
STACK = HEADS_PER_STACK * CHUNK
N_STACKS = HEADS // HEADS_PER_STACK
CONV_W = 4
N_GROUPS = 4
EXPERTS_PER_GROUP = 8
N_EXPERTS = N_GROUPS * EXPERTS_PER_GROUP
D_FF = 512
N_BIG = 12 * WIDTH
LANES = 128
SUBLANES = 8
HALF = D_MODEL // 2
MOE_ROWS = 512
CAST_SLAB = 128
SC_CORES, SC_SUBCORES = 2, 16
SC_WORKERS = SC_CORES * SC_SUBCORES
SC_CHUNK = 128
PROJ_TM, PROJ_TN = 512, 2048
TOKEN_TM = 512
VMEM_LIMIT = 48 * 1024 * 1024


def _bdot(a, b):
    return lax.dot_general(a.astype(BF16), b.astype(BF16), (((2,), (1,)), ((0,), (0,))),
                           preferred_element_type=F32)


def _bdot_nt(a, b):
    return lax.dot_general(a.astype(BF16), b.astype(BF16), (((2,), (2,)), ((0,), (0,))),
                           preferred_element_type=F32)


def _bdot_tn(a, b):
    return lax.dot_general(a.astype(BF16), b.astype(BF16), (((1,), (1,)), ((0,), (0,))),
                           preferred_element_type=F32)


def _split3(x):
    hi = x.astype(BF16)
    r1 = x - hi.astype(F32)
    mid = r1.astype(BF16)
    lo = (r1 - mid.astype(F32)).astype(BF16)
    return hi, mid, lo


def _sigmoid(x):
    return 1.0 / (1.0 + jnp.exp(-x))


def _silu(x):
    return x * _sigmoid(x)


def _softplus(x):
    return jnp.maximum(x, 0.0) + jnp.log(1.0 + jnp.exp(-jnp.abs(x)))


def _stack_heads(x, bsz):
    x3 = x.reshape(bsz, CHUNK, WIDTH)
    y = jnp.concatenate([x3[:, :, h * HEAD_DIM:(h + 1) * HEAD_DIM] for h in range(HEADS)], axis=1)
    return y.reshape(bsz * HEADS * CHUNK, HEAD_DIM)


def _unstack_heads(x, bsz):
    x3 = x.reshape(bsz, HEADS * CHUNK, HEAD_DIM)
    return jnp.concatenate([x3[:, h * CHUNK:(h + 1) * CHUNK, :] for h in range(HEADS)], axis=2)


def _to_stacks(a, bsz):
    return a.reshape(bsz * N_STACKS, STACK, a.shape[-1])


def _to_heads(a, bsz):
    return a.reshape(bsz * HEADS, CHUNK, a.shape[-1])


def _roll_rows(x, shift):
    n = x.shape[0]
    return pltpu.roll(x, shift % n, axis=0)


def _load_chunk(is_meta, meta_ref, real_ref, bsz):
    x = jnp.where(is_meta, meta_ref[...][None], real_ref[...])
    return x.reshape(bsz * CHUNK, WIDTH)


def _cast_rows(src_ref, dst_ref):
    slab = CAST_SLAB
    rows = src_ref.shape[0]
    assert rows % slab == 0

    def body(k, carry):
        r = pl.multiple_of(k * slab, slab)
        dst_ref[pl.ds(r, slab), :] = src_ref[pl.ds(r, slab), :].astype(dst_ref.dtype)
        return carry

    lax.fori_loop(0, rows // slab, body, 0)


def _pack_halves(x):
    lo = lax.bitcast_convert_type(x[:, :HALF].astype(BF16).astype(F32), U32)
    hi = lax.bitcast_convert_type(x[:, HALF:].astype(BF16).astype(F32), U32)
    return (lo >> 16) | (hi & jnp.uint32(0xFFFF0000))


def _unpack_halves(p):
    lo = lax.bitcast_convert_type(p << 16, F32)
    hi = lax.bitcast_convert_type(p & jnp.uint32(0xFFFF0000), F32)
    return jnp.concatenate([lo, hi], axis=1)


def _norm_proj_kernel(x_ref, g_ref, lb_ref, w_ref, o_ref, wbf_ref, *, group):
    @pl.when(pl.program_id(0) == 0)
    def _():
        _cast_rows(w_ref, wbf_ref)

    x = x_ref[...]
    xn = x * lax.rsqrt(jnp.mean(x * x, axis=-1, keepdims=True) + EPS) * g_ref[...]
    acc = jnp.dot(xn.astype(BF16), wbf_ref[...], preferred_element_type=F32)
    w1, w2, w3, w4 = WIDTH, 2 * WIDTH, 3 * WIDTH, 4 * WIDTH
    if group == "hgrn":
        lb = lb_ref[...]
        o_ref[:, :w1] = _silu(acc[:, :w1])
        o_ref[:, w1:w2] = lb + (1.0 - lb) * _sigmoid(acc[:, w1:w2])
        o_ref[:, w2:w3] = acc[:, w2:w3]
        o_ref[:, w3:] = _silu(acc[:, w3:])
    elif group == "gdn":
        o_ref[:, :w3] = acc[:, :w3]
        o_ref[:, w3:w4] = _silu(acc[:, w3:w4])
        o_ref[:, w4:] = acc[:, w4:]
    else:
        o_ref[...] = _sigmoid(acc)


def _norm_proj(x2d, g, lb, w, tm, group):
    m, k = x2d.shape
    n = w.shape[1]
    const = lambda i: (0, 0)
    return pl.pallas_call(
        functools.partial(_norm_proj_kernel, group=group),
        grid=(m // tm,),
        in_specs=[pl.BlockSpec((tm, k), lambda i: (i, 0)),
                  pl.BlockSpec((1, k), const),
                  pl.BlockSpec((1, WIDTH), const),
                  pl.BlockSpec((k, n), const)],
        out_specs=pl.BlockSpec((tm, n), lambda i: (i, 0)),
        out_shape=jax.ShapeDtypeStruct((m, n), F32),
        scratch_shapes=[pltpu.VMEM((k, n), BF16)],
        compiler_params=pltpu.CompilerParams(
            dimension_semantics=("arbitrary",), vmem_limit_bytes=VMEM_LIMIT),
        name="norm_proj_" + group,
    )(x2d, g, lb, w)


HGRN_LEVELS = 6
HGRN_DIAG = HGRN_LEVELS


def _hgrn_level_table():
    r = np.arange(STACK)
    t, hd = r % CHUNK, r // CHUNK
    x = t[:, None] ^ t[None, :]
    lv = np.floor(np.log2(np.maximum(x, 1))).astype(np.int32)
    valid = (hd[:, None] == hd[None, :]) & (t[:, None] > t[None, :])
    lv = np.where(valid, lv, -1)
    lv = np.where(r[:, None] == r[None, :], HGRN_DIAG, lv)
    return jnp.asarray(lv, dtype=jnp.int32)


def _hgrn_kernel(lv_ref, g_ref, mq_ref, mf_ref, mi_ref, mg_ref,
                 rq_ref, rf_ref, ri_ref, rg_ref, o_ref, st_ref, *, bsz):
    c = pl.program_id(0)

    @pl.when(c == 0)
    def _():
        st_ref[...] = jnp.zeros_like(st_ref)

    is_meta = c == 0
    q = _load_chunk(is_meta, mq_ref, rq_ref, bsz)
    f = _load_chunk(is_meta, mf_ref, rf_ref, bsz)
    v = _load_chunk(is_meta, mi_ref, ri_ref, bsz)
    gate = _load_chunk(is_meta, mg_ref, rg_ref, bsz)

    per_seq = lambda a: a.reshape(bsz, CHUNK, WIDTH)
    trow = lax.broadcasted_iota(jnp.int32, (1, CHUNK, WIDTH), 1)
    b = jnp.log(f)
    s = 1
    while s < CHUNK:
        b = (per_seq(b) + jnp.where(trow >= s, per_seq(_roll_rows(b, s)), 0.0)).reshape(b.shape)
        s *= 2

    qs, ks, vs, bs = (_stack_heads(a, bsz) for a in (q, 1.0 - f, v, b))
    lv = lv_ref[...][None]
    to3 = functools.partial(_to_stacks, bsz=bsz)
    to_g = functools.partial(_to_heads, bsz=bsz)
    roll_g = lambda a, shift: to_g(_roll_rows(a.reshape(qs.shape), shift))
    t = lax.broadcasted_iota(jnp.int32, (1, CHUNK, HEAD_DIM), 1)
    qs_g, ks_g, bs_g = to_g(qs), to_g(ks), to_g(bs)

    attn = jnp.where(lv == HGRN_DIAG, _bdot_nt(to3(qs), to3(ks)), 0.0)
    bref = bs_g
    bnext = roll_g(bs_g, -1)
    for li in range(HGRN_LEVELS):
        m = 1 << li
        if li > 0:
            half = m // 2
            upper = (t & (m - 1)) >= half
            bref = jnp.where(upper, roll_g(bref, half), bref)
            bnext = jnp.where(upper, bnext, roll_g(bnext, -half))
        odd = ((t >> li) & 1) == 1
        both = jnp.where(odd, qs_g, ks_g) * jnp.exp(jnp.where(odd, bs_g - bref, bnext - bs_g))
        both = to3(both.reshape(qs.shape)).astype(BF16)
        attn = jnp.where(lv == li, _bdot_nt(both, both), attn)

    o = _bdot(attn, to3(vs))

    st = st_ref[...]
    o_inter = _bdot_nt(qs_g * jnp.exp(bs_g), st)
    b_end = bs_g[:, CHUNK - 1:CHUNK, :]
    ke = to_g(ks) * jnp.exp(b_end - bs_g)
    st_ref[...] = st * jnp.exp(b_end) + _bdot_tn(to_g(vs), ke)
    o = (to_g(o) + o_inter).reshape(qs.shape)

    o = o * lax.rsqrt(jnp.mean(o * o, axis=-1, keepdims=True) + EPS) * g_ref[...]
    o = o * _stack_heads(gate, bsz)
    o_ref[...] = _unstack_heads(o, bsz).astype(o_ref.dtype)


def _chunk_specs(bsz, col0):
    meta = [pl.BlockSpec((CHUNK, WIDTH), functools.partial(lambda c, j: (0, j), j=col0 + j))
            for j in range(4)]
    real = [pl.BlockSpec((bsz, CHUNK, WIDTH),
                         functools.partial(lambda c, j: (0, jnp.maximum(c - 1, 0), j), j=col0 + j))
            for j in range(4)]
    return meta + real


def _hgrn(proj_meta, proj_real, norm_g):
    bsz, seq, _ = proj_real.shape
    n_chunks = seq // CHUNK + 1
    const = lambda c: (0, 0)
    return pl.pallas_call(
        functools.partial(_hgrn_kernel, bsz=bsz),
        grid=(n_chunks,),
        in_specs=[pl.BlockSpec((STACK, STACK), const),
                  pl.BlockSpec((1, HEAD_DIM), const)] + _chunk_specs(bsz, 0),
        out_specs=pl.BlockSpec((bsz, CHUNK, WIDTH), lambda c: (0, jnp.maximum(c - 1, 0), 0)),
        out_shape=jax.ShapeDtypeStruct((bsz, seq, WIDTH), BF16),
        scratch_shapes=[pltpu.VMEM((bsz * HEADS, HEAD_DIM, HEAD_DIM), F32)],
        compiler_params=pltpu.CompilerParams(
            dimension_semantics=("arbitrary",), vmem_limit_bytes=VMEM_LIMIT),
        name="hgrn",
    )(_hgrn_level_table(), norm_g, *([proj_meta] * 4), *([proj_real] * 4))


def _gdn_cumsum_matrix():
    r = np.arange(STACK)
    u = (r[:, None] // CHUNK == r[None, :] // CHUNK) & (r[:, None] <= r[None, :])
    return jnp.asarray(u, dtype=BF16)


def _gdn_kernel(u_ref, alog_ref, dt_ref, cw_ref, g_ref, s_ref,
                mq_ref, mk_ref, mv_ref, mz_ref, rq_ref, rk_ref, rv_ref, rz_ref,
                o_ref, st_ref, prev_ref, *, bsz):
    c = pl.program_id(0)

    @pl.when(c == 0)
    def _():
        st_ref[...] = jnp.zeros_like(st_ref)
        prev_ref[...] = jnp.zeros_like(prev_ref)

    is_meta = c == 0
    xq = _load_chunk(is_meta, mq_ref, rq_ref, bsz)
    xk = _load_chunk(is_meta, mk_ref, rk_ref, bsz)
    xv = _load_chunk(is_meta, mv_ref, rv_ref, bsz)
    gate = _load_chunk(is_meta, mz_ref, rz_ref, bsz)

    x = jnp.concatenate([xq, xk, xv], axis=1)
    pv = prev_ref[...]
    trow = lax.broadcasted_iota(jnp.int32, x.shape, 0) & (CHUNK - 1)
    cw = cw_ref[...]
    acc = x * cw[CONV_W - 1:CONV_W, :]
    for d in range(1, CONV_W):
        xs = jnp.where(trow >= d, _roll_rows(x, d), _roll_rows(pv, d - CHUNK))
        acc = acc + xs * cw[CONV_W - 1 - d:CONV_W - d, :]
    prev_ref[...] = x
    y = _silu(acc)

    qs = _stack_heads(y[:, :WIDTH], bsz)
    ks = _stack_heads(y[:, WIDTH:2 * WIDTH], bsz)
    vs = _stack_heads(y[:, 2 * WIDTH:], bsz)
    qs = qs * lax.rsqrt(jnp.sum(qs * qs, axis=-1, keepdims=True) + EPS) * (HEAD_DIM ** -0.5)
    ks = ks * lax.rsqrt(jnp.sum(ks * ks, axis=-1, keepdims=True) + EPS)
    to3 = functools.partial(_to_stacks, bsz=bsz)
    qs, ks, vs = to3(qs), to3(ks), to3(vs)
    n_st = bsz * N_STACKS

    srow = s_ref[...].reshape(n_st, 2, STACK)
    per_stack = lambda ref: jnp.broadcast_to(ref[...][None], (bsz, N_STACKS, 1, STACK)).reshape(n_st, 1, STACK)
    beta_row = _sigmoid(srow[:, 0:1, :])
    g_row = -jnp.exp(per_stack(alog_ref)) * _softplus(srow[:, 1:2, :] + per_stack(dt_ref))
    n8 = n_st * SUBLANES
    g8 = jnp.broadcast_to(g_row, (n_st, SUBLANES, STACK)).reshape(n8, STACK)
    pieces = jnp.concatenate(_split3(g8), axis=0)
    cum = jnp.dot(pieces, u_ref[...], preferred_element_type=F32)
    cum = cum[0:n8] + cum[n8:2 * n8] + cum[2 * n8:]
    cum_row = cum.reshape(n_st, SUBLANES, STACK)[:, 0:1, :]

    r = lax.broadcasted_iota(jnp.int32, (STACK, STACK), 0)
    cc = lax.broadcasted_iota(jnp.int32, (STACK, STACK), 1)
    eye = (r == cc)[None]
    cum_col = jnp.sum(jnp.where(eye, cum_row, 0.0), axis=2, keepdims=True)
    beta_col = jnp.sum(jnp.where(eye, beta_row, 0.0), axis=2, keepdims=True)
    same = (r >> 6) == (cc >> 6)
    lower = (same & (r >= cc))[None]
    strict = (same & (r > cc))[None]
    ratio = jnp.exp(jnp.minimum(cum_col - cum_row, 0.0))

    kkt = _bdot_nt(ks, ks)
    qkt = _bdot_nt(qs, ks)
    lm = jnp.where(strict, beta_col * ratio * kkt, 0.0)

    ident = eye.astype(F32)
    l0 = jnp.where(((r >> 3) == (cc >> 3))[None], lm, 0.0)
    l2 = _bdot(l0, l0)
    l4 = _bdot(l2, l2)
    xinv = _bdot(_bdot(ident - l0, ident + l2), ident + l4)
    for lvl in (3, 4, 5):
        off = (((r >> (lvl + 1)) == (cc >> (lvl + 1))) & ((r >> lvl) != (cc >> lvl)))[None]
        xinv = xinv - _bdot(_bdot(xinv, jnp.where(off, lm, 0.0)), xinv)

    decay_col = jnp.exp(cum_col)
    rhs = jnp.concatenate([beta_col * vs, (beta_col * decay_col) * ks], axis=2)
    sol = _bdot(xinv, rhs)

    to_g = functools.partial(_to_heads, bsz=bsz)
    st = st_ref[...]
    u = to_g(sol[:, :, :HEAD_DIM]) - _bdot(to_g(sol[:, :, HEAD_DIM:]), st)
    o_inter = to_g(decay_col) * _bdot(to_g(qs), st)
    cum_g = to_g(cum_col)
    cum_end = cum_g[:, CHUNK - 1:CHUNK, :]
    kdec = to_g(ks) * jnp.exp(cum_end - cum_g)
    st_ref[...] = jnp.exp(cum_end) * st + _bdot_tn(kdec, u)
    attn = jnp.where(lower, qkt * ratio, 0.0)
    o = o_inter + to_g(_bdot(attn, to3(u)))
    o = o.reshape(bsz * HEADS * CHUNK, HEAD_DIM)

    o = o * lax.rsqrt(jnp.mean(o * o, axis=-1, keepdims=True) + EPS) * g_ref[...]
    o = o * _stack_heads(gate, bsz)
    o_ref[...] = _unstack_heads(o, bsz).astype(o_ref.dtype)


def _gdn(proj_meta, proj_real, scal, a_log_row, dt_row, conv_w, norm_g):
    bsz, seq, _ = proj_real.shape
    n_chunks = seq // CHUNK + 1
    const = lambda c: (0, 0)
    const3 = lambda c: (0, 0, 0)
    return pl.pallas_call(
        functools.partial(_gdn_kernel, bsz=bsz),
        grid=(n_chunks,),
        in_specs=[pl.BlockSpec((STACK, STACK), const),
                  pl.BlockSpec((N_STACKS, 1, STACK), const3),
                  pl.BlockSpec((N_STACKS, 1, STACK), const3),
                  pl.BlockSpec((CONV_W, 3 * WIDTH), const),
                  pl.BlockSpec((1, HEAD_DIM), const),
                  pl.BlockSpec((bsz, None, N_STACKS, 2, STACK), lambda c: (0, c, 0, 0, 0))]
                 + _chunk_specs(bsz, 0),
        out_specs=pl.BlockSpec((bsz, CHUNK, WIDTH), lambda c: (0, jnp.maximum(c - 1, 0), 0)),
        out_shape=jax.ShapeDtypeStruct((bsz, seq, WIDTH), BF16),
        scratch_shapes=[pltpu.VMEM((bsz * HEADS, HEAD_DIM, HEAD_DIM), F32),
                        pltpu.VMEM((bsz * CHUNK, 3 * WIDTH), F32)],
        compiler_params=pltpu.CompilerParams(
            dimension_semantics=("arbitrary",), vmem_limit_bytes=VMEM_LIMIT),
        name="gdn",
    )(_gdn_cumsum_matrix(), a_log_row, dt_row, conv_w, norm_g, scal,
      *([proj_meta] * 4), *([proj_real] * 4))


ROUTE_LANE0 = N_GROUPS
ROUTE_NEG = -1e30


def _route_tile(lg, ltri, carry):
    lane = lax.broadcasted_iota(jnp.int32, lg.shape, 1).astype(F32)
    first = lambda mask: jnp.min(jnp.where(mask, lane, float(LANES)), axis=1, keepdims=True)
    top = lambda mask: jnp.max(jnp.where(mask, lg, ROUTE_NEG), axis=1, keepdims=True)

    is_g = lane < N_GROUPS
    gmax = top(is_g)
    grp = first(is_g & (lg == gmax))
    p_grp = 1.0 / jnp.sum(jnp.where(is_g, jnp.exp(lg - gmax), 0.0), axis=1, keepdims=True)

    lo = ROUTE_LANE0 + grp * EXPERTS_PER_GROUP
    in_grp = (lane >= lo) & (lane < lo + EXPERTS_PER_GROUP)
    m1 = top(in_grp)
    i1 = first(in_grp & (lg == m1))
    rest = in_grp & (lane != i1)
    m2 = top(rest)
    i2 = first(rest & (lg == m2))
    r = jnp.exp(m2 - m1)
    w0 = p_grp / (1.0 + r)
    w1 = p_grp * r / (1.0 + r)

    hot0 = (lane == i1).astype(F32)
    hot1 = (lane == i2).astype(F32)
    both = hot0 + hot1
    before = jnp.dot(ltri, both.astype(BF16), preferred_element_type=F32) + carry
    rank0 = jnp.sum(hot0 * before, axis=1, keepdims=True)
    rank1 = jnp.sum(hot1 * before, axis=1, keepdims=True)
    cols = (w0, w1, i1 - ROUTE_LANE0, i2 - ROUTE_LANE0, rank0, rank1)
    info = jnp.zeros_like(lg)
    for k, col in enumerate(cols):
        info = jnp.where(lane == k, col, info)
    return info, carry + jnp.sum(both, axis=0, keepdims=True)


def _merge_kernel(oa_ref, ob_ref, ga_ref, gb_ref, x_ref, hgup_ref, gdup_ref, wout_ref,
                  ng_ref, rwh_ref, rwl_ref, rb_ref, ltri_ref, h2_ref, xn_ref, info_ref, cnt_ref, carry_ref):
    @pl.when(pl.program_id(0) == 0)
    def _():
        carry_ref[...] = jnp.zeros_like(carry_ref)

    ua = jnp.dot(oa_ref[...], hgup_ref[...], preferred_element_type=F32)
    ub = jnp.dot(ob_ref[...], gdup_ref[...], preferred_element_type=F32)
    merged = ga_ref[...] * ua + gb_ref[...] * ub
    h2 = x_ref[...] + jnp.dot(merged.astype(BF16), wout_ref[...], preferred_element_type=F32)
    h2_ref[...] = h2
    xn = h2 * lax.rsqrt(jnp.mean(h2 * h2, axis=-1, keepdims=True) + EPS) * ng_ref[...]
    xn_ref[...] = _pack_halves(xn)
    xh = xn.astype(BF16)
    xl = (xn - xh.astype(F32)).astype(BF16)
    wh, wl = rwh_ref[...], rwl_ref[...]
    lg = (jnp.dot(xh, wh, preferred_element_type=F32)
          + jnp.dot(xh, wl, preferred_element_type=F32)
          + jnp.dot(xl, wh, preferred_element_type=F32)) + rb_ref[...]
    info, carry = _route_tile(lg, ltri_ref[...], carry_ref[...])
    info_ref[...] = info
    carry_ref[...] = carry
    cnt_ref[...] = carry


def _merge(o_a, o_b, proj, x2d, hg_up, gd_up, w_out, norm_g, rw_hi, rw_lo, rb):
    t = x2d.shape[0]
    tm = TOKEN_TM
    row = lambda i: (i, 0)
    const = lambda i: (0, 0)
    ltri = jnp.asarray(np.tril(np.ones((tm, tm), np.float32), -1), dtype=BF16)
    return pl.pallas_call(
        _merge_kernel,
        grid=(t // tm,),
        in_specs=[pl.BlockSpec((tm, WIDTH), row),
                  pl.BlockSpec((tm, WIDTH), row),
                  pl.BlockSpec((tm, D_MODEL), lambda i: (i, 0)),
                  pl.BlockSpec((tm, D_MODEL), lambda i: (i, 1)),
                  pl.BlockSpec((tm, D_MODEL), row),
                  pl.BlockSpec((WIDTH, D_MODEL), const),
                  pl.BlockSpec((WIDTH, D_MODEL), const),
                  pl.BlockSpec((D_MODEL, D_MODEL), const),
                  pl.BlockSpec((1, D_MODEL), const),
                  pl.BlockSpec((D_MODEL, LANES), const),
                  pl.BlockSpec((D_MODEL, LANES), const),
                  pl.BlockSpec((1, LANES), const),
                  pl.BlockSpec((tm, tm), const)],
        out_specs=[pl.BlockSpec((tm, D_MODEL), row),
                   pl.BlockSpec((tm, HALF), row),
                   pl.BlockSpec((tm, LANES), row),
                   pl.BlockSpec((1, LANES), const)],
        out_shape=[jax.ShapeDtypeStruct((t, D_MODEL), F32),
                   jax.ShapeDtypeStruct((t, HALF), U32),
                   jax.ShapeDtypeStruct((t, LANES), F32),
                   jax.ShapeDtypeStruct((1, LANES), F32)],
        scratch_shapes=[pltpu.VMEM((1, LANES), F32)],
        compiler_params=pltpu.CompilerParams(
            dimension_semantics=("arbitrary",), vmem_limit_bytes=VMEM_LIMIT),
        name="merge",
    )(o_a, o_b, proj, proj, x2d, hg_up, gd_up, w_out, norm_g, rw_hi, rw_lo, rb, ltri)


def _sc_gather(table, idx):
    n_idx = idx.shape[0]
    cols = table.shape[1]
    per_worker = n_idx // SC_WORKERS
    assert n_idx % (SC_WORKERS * SC_CHUNK) == 0
    mesh = plsc.VectorSubcoreMesh(core_axis_name="c", subcore_axis_name="s")

    @functools.partial(
        pl.kernel, mesh=mesh,
        out_type=jax.ShapeDtypeStruct((n_idx, cols), table.dtype),
        scratch_types=[pltpu.VMEM((SC_CHUNK,), jnp.int32),
                       pltpu.VMEM((SC_CHUNK, cols), table.dtype),
                       pltpu.SemaphoreType.DMA],
    )
    def gather(table_hbm, idx_hbm, out_hbm, idx_v, rows_v, sem):
        worker = lax.axis_index("s") * SC_CORES + lax.axis_index("c")
        base = worker * per_worker

        @pl.loop(0, per_worker // SC_CHUNK)
        def _(c):
            off = pl.multiple_of(base + c * SC_CHUNK, SC_CHUNK)
            pltpu.sync_copy(idx_hbm.at[pl.ds(off, SC_CHUNK)], idx_v)
            pltpu.async_copy(table_hbm.at[idx_v], rows_v, sem).wait()
            pltpu.sync_copy(rows_v, out_hbm.at[pl.ds(off, SC_CHUNK)])

    return gather(table, idx)


def _sc_dispatch(rows, dest, n_out):
    t, cols = rows.shape
    per_worker = t // SC_WORKERS
    assert t % (SC_WORKERS * SC_CHUNK) == 0 and dest.shape[0] == 2 * t
    mesh = plsc.VectorSubcoreMesh(core_axis_name="c", subcore_axis_name="s")

    @functools.partial(
        pl.kernel, mesh=mesh,
        out_type=jax.ShapeDtypeStruct((n_out, cols), rows.dtype),
        scratch_types=[pltpu.VMEM((SC_CHUNK,), jnp.int32),
                       pltpu.VMEM((SC_CHUNK, cols), rows.dtype)],
    )
    def dispatch(rows_hbm, dest_hbm, out_hbm, idx_v, rows_v):
        worker = lax.axis_index("s") * SC_CORES + lax.axis_index("c")
        base = worker * per_worker

        @pl.loop(0, per_worker // SC_CHUNK)
        def _(c):
            off = pl.multiple_of(base + c * SC_CHUNK, SC_CHUNK)
            pltpu.sync_copy(rows_hbm.at[pl.ds(off, SC_CHUNK)], rows_v)
            for slot in range(2):
                pltpu.sync_copy(dest_hbm.at[pl.ds(slot * t + off, SC_CHUNK)], idx_v)
                pltpu.sync_copy(rows_v, out_hbm.at[idx_v])

    return dispatch(rows, dest)


def _moe_kernel(be_ref, nu_ref, x_ref, wg_ref, wu_ref, wd_ref, y_ref, wg_bf, wu_bf, wd_bf):
    i = pl.program_id(0)

    @pl.when((i == 0) | (be_ref[i] != be_ref[jnp.maximum(i - 1, 0)]))
    def _():
        _cast_rows(wg_ref, wg_bf)
        _cast_rows(wu_ref, wu_bf)
        _cast_rows(wd_ref, wd_bf)

    @pl.when(i < nu_ref[0])
    def _():
        xb = _unpack_halves(x_ref[...]).astype(BF16)
        a = jnp.dot(xb, wg_bf[...], preferred_element_type=F32)
        u = jnp.dot(xb, wu_bf[...], preferred_element_type=F32)
        y = jnp.dot((_silu(a) * u).astype(BF16), wd_bf[...], preferred_element_type=F32)
        y_ref[...] = _pack_halves(y)

    @pl.when(i >= nu_ref[0])
    def _():
        y_ref[...] = jnp.zeros_like(y_ref)


def _moe(blk_expert, n_used, x_sorted, w_gate, w_up, w_down):
    n_blocks = blk_expert.shape[0]
    wspec = lambda shape: pl.BlockSpec((None,) + shape, lambda i, be, nu: (be[i], 0, 0))
    rows = pl.BlockSpec((MOE_ROWS, HALF), lambda i, be, nu: (i, 0))
    grid_spec = pltpu.PrefetchScalarGridSpec(
        num_scalar_prefetch=2,
        grid=(n_blocks,),
        in_specs=[rows, wspec((D_MODEL, D_FF)), wspec((D_MODEL, D_FF)), wspec((D_FF, D_MODEL))],
        out_specs=rows,
        scratch_shapes=[pltpu.VMEM((D_MODEL, D_FF), BF16), pltpu.VMEM((D_MODEL, D_FF), BF16),
                        pltpu.VMEM((D_FF, D_MODEL), BF16)],
    )
    return pl.pallas_call(
        _moe_kernel,
        grid_spec=grid_spec,
        out_shape=jax.ShapeDtypeStruct(x_sorted.shape, U32),
        compiler_params=pltpu.CompilerParams(
            dimension_semantics=("arbitrary",), vmem_limit_bytes=VMEM_LIMIT),
        name="moe",
    )(blk_expert, n_used, x_sorted, w_gate, w_up, w_down)


def _combine_kernel(h2_ref, rw_ref, g_ref, y0_ref, y1_ref, o_ref):
    rw = rw_ref[...]
    h = h2_ref[...] + rw[:, 0:1] * _unpack_halves(y0_ref[...]) + rw[:, 1:2] * _unpack_halves(y1_ref[...])
    o_ref[...] = h * lax.rsqrt(jnp.mean(h * h, axis=-1, keepdims=True) + EPS) * g_ref[...]


def _combine(h2, rweights, final_g, y):
    t = h2.shape[0]
    tm = TOKEN_TM
    row = lambda i: (i, 0)
    return pl.pallas_call(
        _combine_kernel,
        grid=(t // tm,),
        in_specs=[pl.BlockSpec((tm, D_MODEL), row),
                  pl.BlockSpec((tm, LANES), row),
                  pl.BlockSpec((1, D_MODEL), lambda i: (0, 0)),
                  pl.BlockSpec((tm, HALF), row),
                  pl.BlockSpec((tm, HALF), lambda i: (i + t // tm, 0))],
        out_specs=pl.BlockSpec((tm, D_MODEL), row),
        out_shape=jax.ShapeDtypeStruct((t, D_MODEL), F32),
        compiler_params=pltpu.CompilerParams(
            dimension_semantics=("arbitrary",), vmem_limit_bytes=VMEM_LIMIT),
        name="combine",
    )(h2, rweights, final_g, y, y)


def _block_layout(info, counts_row, t):
    n_blocks = 2 * t // MOE_ROWS + N_EXPERTS
    counts = counts_row[0, ROUTE_LANE0:ROUTE_LANE0 + N_EXPERTS].astype(jnp.int32)
    padded = ((counts + MOE_ROWS - 1) // MOE_ROWS) * MOE_ROWS
    pend = jnp.cumsum(padded)
    pstart = pend - padded
    blk_first = jnp.arange(n_blocks, dtype=jnp.int32) * MOE_ROWS
    blk_expert = jnp.minimum(jnp.sum((pend[None, :] <= blk_first[:, None]).astype(jnp.int32), axis=1),
                             N_EXPERTS - 1).astype(jnp.int32)
    n_used = (pend[-1] // MOE_ROWS).astype(jnp.int32).reshape(1)
    expert = info[:, 2:4].astype(jnp.int32)
    rank = info[:, 4:6].astype(jnp.int32)
    experts = jnp.arange(N_EXPERTS, dtype=jnp.int32)
    first_row = jnp.sum(jnp.where(expert[:, :, None] == experts, pstart, 0), axis=-1)
    dest = (first_row + rank).T.reshape(2 * t)
    return dest, blk_expert, n_used, n_blocks * MOE_ROWS


def kernel(x, meta_tokens, hg_lb_logits, norm_mix_g, w_in, gd_conv_w, gd_A_log, gd_dt_bias, hg_norm_g, gd_norm_g, hg_up, gd_up, w_out, norm_ffn_g, router_group_w, router_group_b, router_expert_w, router_expert_b, w_gate, w_up, w_down, final_norm_g):
    bsz, seq, d = x.shape
    t = bsz * seq
    x2d = x.reshape(t, d)

    lb = jnp.cumsum(jax.nn.softmax(hg_lb_logits.astype(F32), axis=0), axis=0)[0].reshape(1, WIDTH)
    w = w_in[0].astype(F32)
    n_h, n_g = 4 * WIDTH, 8 * WIDTH
    n_s = n_g + 2 * HEADS
    w_groups = {
        "hgrn": w[:, :n_h],
        "gdn": jnp.concatenate([w[:, n_h:n_s], jnp.zeros((d, LANES - 2 * HEADS), F32)], axis=1),
        "gates": w[:, n_s:],
    }
    g_mix = norm_mix_g[0].reshape(1, d)
    meta_blk = jnp.concatenate([jnp.zeros((CHUNK - N_META, d), F32), meta_tokens.astype(F32)], axis=0)

    proj = {k: _norm_proj(x2d, g_mix, lb, wk, PROJ_TM, k) for k, wk in w_groups.items()}
    proj_meta = {k: _norm_proj(meta_blk, g_mix, lb, w_groups[k], CHUNK, k) for k in ("hgrn", "gdn")}

    o_a = _hgrn(proj_meta["hgrn"], proj["hgrn"].reshape(bsz, seq, -1), hg_norm_g[0].reshape(1, HEAD_DIM))

    def scalar_rows(p, nb, nc):
        s = p[:, 4 * WIDTH:4 * WIDTH + 2 * HEADS].reshape(nb, nc, CHUNK, 2, N_STACKS, HEADS_PER_STACK)
        return s.transpose(0, 1, 4, 3, 5, 2).reshape(nb, nc, N_STACKS, 2, STACK)

    s_meta = jnp.broadcast_to(scalar_rows(proj_meta["gdn"], 1, 1), (bsz, 1, N_STACKS, 2, STACK))
    scal = jnp.concatenate([s_meta, scalar_rows(proj["gdn"], bsz, seq // CHUNK)], axis=1)
    a_log_row = jnp.repeat(gd_A_log[0].astype(F32), CHUNK).reshape(N_STACKS, 1, STACK)
    dt_row = jnp.repeat(gd_dt_bias[0].astype(F32), CHUNK).reshape(N_STACKS, 1, STACK)
    o_b = _gdn(proj_meta["gdn"], proj["gdn"].reshape(bsz, seq, -1), scal, a_log_row, dt_row,
               gd_conv_w[0].astype(F32), gd_norm_g[0].reshape(1, HEAD_DIM))

    rw = jnp.zeros((d, LANES), F32)
    rw = rw.at[:, :N_GROUPS].set(router_group_w[0]).at[:, N_GROUPS:N_GROUPS + N_EXPERTS].set(router_expert_w[0])
    rw_hi = rw.astype(BF16)
    rw_lo = (rw - rw_hi.astype(F32)).astype(BF16)
    rb = jnp.zeros((1, LANES), F32)
    rb = rb.at[0, :N_GROUPS].set(router_group_b[0]).at[0, N_GROUPS:N_GROUPS + N_EXPERTS].set(router_expert_b[0])
    h2, xn, info, counts = _merge(o_a.reshape(t, WIDTH), o_b.reshape(t, WIDTH), proj["gates"], x2d,
                                  hg_up[0].astype(BF16), gd_up[0].astype(BF16), w_out[0].astype(BF16),
                                  norm_ffn_g[0].reshape(1, d), rw_hi, rw_lo, rb)

    dest, blk_expert, n_used, n_rows = _block_layout(info, counts, t)
    x_sorted = _sc_dispatch(xn, dest, n_rows)
    y_sorted = _moe(blk_expert, n_used, x_sorted,
                    w_gate[0].astype(F32), w_up[0].astype(F32), w_down[0].astype(F32))
    y_tok = _sc_gather(y_sorted, dest)
    out = _combine(h2, info, final_norm_g.reshape(1, d), y_tok)
    return out.reshape(bsz, seq, d)
```

```python
import functools

import numpy as np
import jax
import jax.numpy as jnp
from jax import lax
from jax.experimental import pallas as pl
from jax.experimental.pallas import tpu as pltpu
from jax.experimental.pallas import tpu_sc as plsc

F32 = jnp.float32
BF16 = jnp.bfloat16
U32 = jnp.uint32

D_MODEL = 1024
N_META = 16
CHUNK = 64
EPS = 1e-6
HEADS = 4
HEAD_DIM = 128
WIDTH = HEADS * HEAD_DIM
HEADS_PER_STACK = 2
STACK = HEADS_PER_STACK * CHUNK
N_STACKS = HEADS // HEADS_PER_STACK
CONV_W = 4
N_GROUPS = 4
EXPERTS_PER_GROUP = 8
N_EXPERTS = N_GROUPS * EXPERTS_PER_GROUP
D_FF = 512
LANES = 128
SUBLANES = 8
HALF = D_MODEL // 2
MOE_ROWS = 512
CAST_SLAB = 128
SC_CORES, SC_SUBCORES = 2, 16
SC_WORKERS = SC_CORES * SC_SUBCORES
SC_CHUNK = 128
PROJ_TM = 1024
TOKEN_TM = 512
VMEM_LIMIT = 48 * 1024 * 1024


def _bdot(a, b):
    return lax.dot_general(a.astype(BF16), b.astype(BF16), (((2,), (1,)), ((0,), (0,))),
                           preferred_element_type=F32)


def _bdot_nt(a, b):
    return lax.dot_general(a.astype(BF16), b.astype(BF16), (((2,), (2,)), ((0,), (0,))),
                           preferred_element_type=F32)


def _bdot_tn(a, b):
    return lax.dot_general(a.astype(BF16), b.astype(BF16), (((1,), (1,)), ((0,), (0,))),
                           preferred_element_type=F32)


def _split3(x):
    hi = x.astype(BF16)
    r1 = x - hi.astype(F32)
    mid = r1.astype(BF16)
    lo = (r1 - mid.astype(F32)).astype(BF16)
    return hi, mid, lo


def _sigmoid(x):
    return 1.0 / (1.0 + jnp.exp(-x))


def _silu(x):
    return x * _sigmoid(x)


def _softplus(x):
    return jnp.maximum(x, 0.0) + jnp.log(1.0 + jnp.exp(-jnp.abs(x)))


def _stack_heads(x, bsz):
    x3 = x.reshape(bsz, CHUNK, WIDTH)
    y = jnp.concatenate([x3[:, :, h * HEAD_DIM:(h + 1) * HEAD_DIM] for h in range(HEADS)], axis=1)
    return y.reshape(bsz * HEADS * CHUNK, HEAD_DIM)


def _unstack_heads(x, bsz):
    x3 = x.reshape(bsz, HEADS * CHUNK, HEAD_DIM)
    return jnp.concatenate([x3[:, h * CHUNK:(h + 1) * CHUNK, :] for h in range(HEADS)], axis=2)


def _to_stacks(a, bsz):
    return a.reshape(bsz * N_STACKS, STACK, a.shape[-1])


def _to_heads(a, bsz):
    return a.reshape(bsz * HEADS, CHUNK, a.shape[-1])


def _roll_rows(x, shift):
    n = x.shape[0]
    return pltpu.roll(x, shift % n, axis=0)


def _load_chunk(is_meta, meta_ref, real_ref, bsz):
    x = jnp.where(is_meta, meta_ref[...][None], real_ref[...])
    return x.reshape(bsz * CHUNK, WIDTH)


def _cast_rows(src_ref, dst_ref):
    slab = CAST_SLAB
    rows = src_ref.shape[0]
    assert rows % slab == 0

    def body(k, carry):
        r = pl.multiple_of(k * slab, slab)
        dst_ref[pl.ds(r, slab), :] = src_ref[pl.ds(r, slab), :].astype(dst_ref.dtype)
        return carry

    lax.fori_loop(0, rows // slab, body, 0)


def _pack_halves(x):
    lo = lax.bitcast_convert_type(x[:, :HALF].astype(BF16).astype(F32), U32)
    hi = lax.bitcast_convert_type(x[:, HALF:].astype(BF16).astype(F32), U32)
    return (lo >> 16) | (hi & jnp.uint32(0xFFFF0000))


def _unpack_halves(p):
    lo = lax.bitcast_convert_type(p << 16, F32)
    hi = lax.bitcast_convert_type(p & jnp.uint32(0xFFFF0000), F32)
    return jnp.concatenate([lo, hi], axis=1)


def _norm_proj_kernel(x_ref, g_ref, lb_ref, w_ref, o_ref, wbf_ref, *, group):
    @pl.when(pl.program_id(0) == 0)
    def _():
        _cast_rows(w_ref, wbf_ref)

    x = x_ref[...]
    xn = x * lax.rsqrt(jnp.mean(x * x, axis=-1, keepdims=True) + EPS) * g_ref[...]
    acc = jnp.dot(xn.astype(BF16), wbf_ref[...], preferred_element_type=F32)
    w1, w2, w3, w4 = WIDTH, 2 * WIDTH, 3 * WIDTH, 4 * WIDTH
    if group == "hgrn":
        lb = lb_ref[...]
        o_ref[:, :w1] = _silu(acc[:, :w1])
        o_ref[:, w1:w2] = lb + (1.0 - lb) * _sigmoid(acc[:, w1:w2])
        o_ref[:, w2:w3] = acc[:, w2:w3]
        o_ref[:, w3:] = _silu(acc[:, w3:])
    elif group == "gdn":
        o_ref[:, :w3] = acc[:, :w3]
        o_ref[:, w3:w4] = _silu(acc[:, w3:w4])
        o_ref[:, w4:] = acc[:, w4:]
    else:
        o_ref[...] = _sigmoid(acc)


def _norm_proj(x2d, g, lb, w, tm, group):
    m, k = x2d.shape
    n = w.shape[1]
    tm = min(tm, m)
    const = lambda i: (0, 0)
    return pl.pallas_call(
        functools.partial(_norm_proj_kernel, group=group),
        grid=(m // tm,),
        in_specs=[pl.BlockSpec((tm, k), lambda i: (i, 0)),
                  pl.BlockSpec((1, k), const),
                  pl.BlockSpec((1, WIDTH), const),
                  pl.BlockSpec((k, n), const, pipeline_mode=pl.Buffered(1))],
        out_specs=pl.BlockSpec((tm, n), lambda i: (i, 0)),
        out_shape=jax.ShapeDtypeStruct((m, n), F32),
        scratch_shapes=[pltpu.VMEM((k, n), BF16)],
        compiler_params=pltpu.CompilerParams(
            dimension_semantics=("arbitrary",), vmem_limit_bytes=VMEM_LIMIT),
        name="norm_proj_" + group,
    )(x2d, g, lb, w)


HGRN_LEVELS = 6
HGRN_DIAG = HGRN_LEVELS


def _hgrn_level_table():
    r = np.arange(STACK)
    t, hd = r % CHUNK, r // CHUNK
    x = t[:, None] ^ t[None, :]
    lv = np.floor(np.log2(np.maximum(x, 1))).astype(np.int32)
    valid = (hd[:, None] == hd[None, :]) & (t[:, None] > t[None, :])
    lv = np.where(valid, lv, -1)
    lv = np.where(r[:, None] == r[None, :], HGRN_DIAG, lv)
    return jnp.asarray(lv, dtype=jnp.int32)


def _hgrn_kernel(lv_ref, g_ref, mq_ref, mf_ref, mi_ref, mg_ref,
                 rq_ref, rf_ref, ri_ref, rg_ref, o_ref, st_ref, *, bsz):
    c = pl.program_id(0)

    @pl.when(c == 0)
    def _():
        st_ref[...] = jnp.zeros_like(st_ref)

    is_meta = c == 0
    q = _load_chunk(is_meta, mq_ref, rq_ref, bsz)
    f = _load_chunk(is_meta, mf_ref, rf_ref, bsz)
    v = _load_chunk(is_meta, mi_ref, ri_ref, bsz)
    gate = _load_chunk(is_meta, mg_ref, rg_ref, bsz)

    per_seq = lambda a: a.reshape(bsz, CHUNK, WIDTH)
    trow = lax.broadcasted_iota(jnp.int32, (1, CHUNK, WIDTH), 1)
    b = jnp.log(f)
    s = 1
    while s < CHUNK:
        b = (per_seq(b) + jnp.where(trow >= s, per_seq(_roll_rows(b, s)), 0.0)).reshape(b.shape)
        s *= 2

    qs, ks, vs, bs = (_stack_heads(a, bsz) for a in (q, 1.0 - f, v, b))
    lv = lv_ref[...][None]
    to3 = functools.partial(_to_stacks, bsz=bsz)
    to_g = functools.partial(_to_heads, bsz=bsz)
    roll_g = lambda a, shift: to_g(_roll_rows(a.reshape(qs.shape), shift))
    t = lax.broadcasted_iota(jnp.int32, (1, CHUNK, HEAD_DIM), 1)
    qs_g, ks_g, bs_g = to_g(qs), to_g(ks), to_g(bs)

    attn = jnp.where(lv == HGRN_DIAG, _bdot_nt(to3(qs), to3(ks)), 0.0)
    bref = bs_g
    bnext = roll_g(bs_g, -1)
    for li in range(HGRN_LEVELS):
        m = 1 << li
        if li > 0:
            half = m // 2
            upper = (t & (m - 1)) >= half
            bref = jnp.where(upper, roll_g(bref, half), bref)
            bnext = jnp.where(upper, bnext, roll_g(bnext, -half))
        odd = ((t >> li) & 1) == 1
        both = jnp.where(odd, qs_g, ks_g) * jnp.exp(jnp.where(odd, bs_g - bref, bnext - bs_g))
        both = to3(both.reshape(qs.shape)).astype(BF16)
        attn = jnp.where(lv == li, _bdot_nt(both, both), attn)

    o = _bdot(attn, to3(vs))

    st = st_ref[...]
    o_inter = _bdot_nt(qs_g * jnp.exp(bs_g), st)
    b_end = bs_g[:, CHUNK - 1:CHUNK, :]
    ke = to_g(ks) * jnp.exp(b_end - bs_g)
    st_ref[...] = st * jnp.exp(b_end) + _bdot_tn(to_g(vs), ke)
    o = (to_g(o) + o_inter).reshape(qs.shape)

    o = o * lax.rsqrt(jnp.mean(o * o, axis=-1, keepdims=True) + EPS) * g_ref[...]
    o = o * _stack_heads(gate, bsz)
    o_ref[...] = _unstack_heads(o, bsz).astype(o_ref.dtype)


def _chunk_specs(bsz, col0):
    meta = [pl.BlockSpec((CHUNK, WIDTH), functools.partial(lambda c, j: (0, j), j=col0 + j))
            for j in range(4)]
    real = [pl.BlockSpec((bsz, CHUNK, WIDTH),
                         functools.partial(lambda c, j: (0, jnp.maximum(c - 1, 0), j), j=col0 + j))
            for j in range(4)]
    return meta + real


def _gdn_cumsum_matrix():
    r = np.arange(STACK)
    u = (r[:, None] // CHUNK == r[None, :] // CHUNK) & (r[:, None] <= r[None, :])
    return jnp.asarray(u, dtype=BF16)


def _gdn_kernel(u_ref, alog_ref, dt_ref, cw_ref, g_ref, s_ref,
                mq_ref, mk_ref, mv_ref, mz_ref, rq_ref, rk_ref, rv_ref, rz_ref,
                o_ref, st_ref, prev_ref, *, bsz):
    c = pl.program_id(0)

    @pl.when(c == 0)
    def _():
        st_ref[...] = jnp.zeros_like(st_ref)
        prev_ref[...] = jnp.zeros_like(prev_ref)

    is_meta = c == 0
    xq = _load_chunk(is_meta, mq_ref, rq_ref, bsz)
    xk = _load_chunk(is_meta, mk_ref, rk_ref, bsz)
    xv = _load_chunk(is_meta, mv_ref, rv_ref, bsz)
    gate = _load_chunk(is_meta, mz_ref, rz_ref, bsz)

    x = jnp.concatenate([xq, xk, xv], axis=1)
    pv = prev_ref[...]
    trow = lax.broadcasted_iota(jnp.int32, x.shape, 0) & (CHUNK - 1)
    cw = cw_ref[...]
    acc = x * cw[CONV_W - 1:CONV_W, :]
    for d in range(1, CONV_W):
        xs = jnp.where(trow >= d, _roll_rows(x, d), _roll_rows(pv, d - CHUNK))
        acc = acc + xs * cw[CONV_W - 1 - d:CONV_W - d, :]
    prev_ref[...] = x
    y = _silu(acc)

    qs = _stack_heads(y[:, :WIDTH], bsz)
    ks = _stack_heads(y[:, WIDTH:2 * WIDTH], bsz)
    vs = _stack_heads(y[:, 2 * WIDTH:], bsz)
    qs = qs * lax.rsqrt(jnp.sum(qs * qs, axis=-1, keepdims=True) + EPS) * (HEAD_DIM ** -0.5)
    ks = ks * lax.rsqrt(jnp.sum(ks * ks, axis=-1, keepdims=True) + EPS)
    to3 = functools.partial(_to_stacks, bsz=bsz)
    qs, ks, vs = to3(qs), to3(ks), to3(vs)
    n_st = bsz * N_STACKS

    srow = s_ref[...].reshape(n_st, 2, STACK)
    per_stack = lambda ref: jnp.broadcast_to(ref[...][None], (bsz, N_STACKS, 1, STACK)).reshape(n_st, 1, STACK)
    beta_row = _sigmoid(srow[:, 0:1, :])
    g_row = -jnp.exp(per_stack(alog_ref)) * _softplus(srow[:, 1:2, :] + per_stack(dt_ref))
    n8 = n_st * SUBLANES
    g8 = jnp.broadcast_to(g_row, (n_st, SUBLANES, STACK)).reshape(n8, STACK)
    pieces = jnp.concatenate(_split3(g8), axis=0)
    cum = jnp.dot(pieces, u_ref[...], preferred_element_type=F32)
    cum = cum[0:n8] + cum[n8:2 * n8] + cum[2 * n8:]
    cum_row = cum.reshape(n_st, SUBLANES, STACK)[:, 0:1, :]

    r = lax.broadcasted_iota(jnp.int32, (STACK, STACK), 0)
    cc = lax.broadcasted_iota(jnp.int32, (STACK, STACK), 1)
    eye = (r == cc)[None]
    cum_col = jnp.sum(jnp.where(eye, cum_row, 0.0), axis=2, keepdims=True)
    beta_col = jnp.sum(jnp.where(eye, beta_row, 0.0), axis=2, keepdims=True)
    same = (r >> 6) == (cc >> 6)
    lower = (same & (r >= cc))[None]
    strict = (same & (r > cc))[None]
    ratio = jnp.exp(jnp.minimum(cum_col - cum_row, 0.0))

    kkt = _bdot_nt(ks, ks)
    qkt = _bdot_nt(qs, ks)
    lm = jnp.where(strict, beta_col * ratio * kkt, 0.0)

    ident = eye.astype(F32)
    l0 = jnp.where(((r >> 3) == (cc >> 3))[None], lm, 0.0)
    l2 = _bdot(l0, l0)
    l4 = _bdot(l2, l2)
    xinv = _bdot(_bdot(ident - l0, ident + l2), ident + l4)
    for lvl in (3, 4, 5):
        off = (((r >> (lvl + 1)) == (cc >> (lvl + 1))) & ((r >> lvl) != (cc >> lvl)))[None]
        xinv = xinv - _bdot(_bdot(xinv, jnp.where(off, lm, 0.0)), xinv)

    decay_col = jnp.exp(cum_col)
    rhs = jnp.concatenate([beta_col * vs, (beta_col * decay_col) * ks], axis=2)
    sol = _bdot(xinv, rhs)

    to_g = functools.partial(_to_heads, bsz=bsz)
    st = st_ref[...]
    u = to_g(sol[:, :, :HEAD_DIM]) - _bdot(to_g(sol[:, :, HEAD_DIM:]), st)
    o_inter = to_g(decay_col) * _bdot(to_g(qs), st)
    cum_g = to_g(cum_col)
    cum_end = cum_g[:, CHUNK - 1:CHUNK, :]
    kdec = to_g(ks) * jnp.exp(cum_end - cum_g)
    st_ref[...] = jnp.exp(cum_end) * st + _bdot_tn(kdec, u)
    attn = jnp.where(lower, qkt * ratio, 0.0)
    o = o_inter + to_g(_bdot(attn, to3(u)))
    o = o.reshape(bsz * HEADS * CHUNK, HEAD_DIM)

    o = o * lax.rsqrt(jnp.mean(o * o, axis=-1, keepdims=True) + EPS) * g_ref[...]
    o = o * _stack_heads(gate, bsz)
    o_ref[...] = _unstack_heads(o, bsz).astype(o_ref.dtype)


N_HGRN_INPUTS = 2 + 8
N_GDN_INPUTS = 6 + 8


def _mixers_kernel(*refs, bsz):
    h_in = refs[:N_HGRN_INPUTS]
    g_in = refs[N_HGRN_INPUTS:N_HGRN_INPUTS + N_GDN_INPUTS]
    oa_ref, ob_ref, sth_ref, stg_ref, prev_ref = refs[N_HGRN_INPUTS + N_GDN_INPUTS:]
    _hgrn_kernel(*h_in, oa_ref, sth_ref, bsz=bsz)
    _gdn_kernel(*g_in, ob_ref, stg_ref, prev_ref, bsz=bsz)


def _mixers(hgrn_meta, hgrn_real, hgrn_norm_g, gdn_meta, gdn_real, scal, a_log_row, dt_row, conv_w, gdn_norm_g):
    bsz, seq, _ = hgrn_real.shape
    n_chunks = seq // CHUNK + 1
    const = lambda c: (0, 0)
    const3 = lambda c: (0, 0, 0)
    out_spec = pl.BlockSpec((bsz, CHUNK, WIDTH), lambda c: (0, jnp.maximum(c - 1, 0), 0))
    out_shape = jax.ShapeDtypeStruct((bsz, seq, WIDTH), BF16)
    state = pltpu.VMEM((bsz * HEADS, HEAD_DIM, HEAD_DIM), F32)
    hgrn_specs = [pl.BlockSpec((STACK, STACK), const),
                  pl.BlockSpec((1, HEAD_DIM), const)] + _chunk_specs(bsz, 0)
    gdn_specs = [pl.BlockSpec((STACK, STACK), const),
                 pl.BlockSpec((N_STACKS, 1, STACK), const3),
                 pl.BlockSpec((N_STACKS, 1, STACK), const3),
                 pl.BlockSpec((CONV_W, 3 * WIDTH), const),
                 pl.BlockSpec((1, HEAD_DIM), const),
                 pl.BlockSpec((bsz, None, N_STACKS, 2, STACK), lambda c: (0, c, 0, 0, 0))] + _chunk_specs(bsz, 0)
    assert len(hgrn_specs) == N_HGRN_INPUTS and len(gdn_specs) == N_GDN_INPUTS
    return pl.pallas_call(
        functools.partial(_mixers_kernel, bsz=bsz),
        grid=(n_chunks,),
        in_specs=hgrn_specs + gdn_specs,
        out_specs=[out_spec, out_spec],
        out_shape=[out_shape, out_shape],
        scratch_shapes=[state, state, pltpu.VMEM((bsz * CHUNK, 3 * WIDTH), F32)],
        compiler_params=pltpu.CompilerParams(
            dimension_semantics=("arbitrary",), vmem_limit_bytes=VMEM_LIMIT),
        name="mixers",
    )(_hgrn_level_table(), hgrn_norm_g, *([hgrn_meta] * 4), *([hgrn_real] * 4),
      _gdn_cumsum_matrix(), a_log_row, dt_row, conv_w, gdn_norm_g, scal,
      *([gdn_meta] * 4), *([gdn_real] * 4))


ROUTE_LANE0 = N_GROUPS
ROUTE_NEG = -1e30


def _route_tile(lg, ltri, carry):
    lane = lax.broadcasted_iota(jnp.int32, lg.shape, 1).astype(F32)
    first = lambda mask: jnp.min(jnp.where(mask, lane, float(LANES)), axis=1, keepdims=True)
    top = lambda mask: jnp.max(jnp.where(mask, lg, ROUTE_NEG), axis=1, keepdims=True)

    is_g = lane < N_GROUPS
    gmax = top(is_g)
    grp = first(is_g & (lg == gmax))
    p_grp = 1.0 / jnp.sum(jnp.where(is_g, jnp.exp(lg - gmax), 0.0), axis=1, keepdims=True)

    lo = ROUTE_LANE0 + grp * EXPERTS_PER_GROUP
    in_grp = (lane >= lo) & (lane < lo + EXPERTS_PER_GROUP)
    m1 = top(in_grp)
    i1 = first(in_grp & (lg == m1))
    rest = in_grp & (lane != i1)
    m2 = top(rest)
    i2 = first(rest & (lg == m2))
    r = jnp.exp(m2 - m1)
    w0 = p_grp / (1.0 + r)
    w1 = p_grp * r / (1.0 + r)

    hot0 = (lane == i1).astype(F32)
    hot1 = (lane == i2).astype(F32)
    both = hot0 + hot1
    before = jnp.dot(ltri, both.astype(BF16), preferred_element_type=F32) + carry
    rank0 = jnp.sum(hot0 * before, axis=1, keepdims=True)
    rank1 = jnp.sum(hot1 * before, axis=1, keepdims=True)
    cols = (w0, w1, i1 - ROUTE_LANE0, i2 - ROUTE_LANE0, rank0, rank1)
    info = jnp.zeros_like(lg)
    for k, col in enumerate(cols):
        info = jnp.where(lane == k, col, info)
    return info, carry + jnp.sum(both, axis=0, keepdims=True)


def _merge_kernel(oa_ref, ob_ref, ga_ref, gb_ref, x_ref, hgup_ref, gdup_ref, wout_ref,
                  ng_ref, rwh_ref, rwl_ref, rb_ref, ltri_ref, h2_ref, xn_ref, info_ref, cnt_ref, carry_ref):
    @pl.when(pl.program_id(0) == 0)
    def _():
        carry_ref[...] = jnp.zeros_like(carry_ref)

    ua = jnp.dot(oa_ref[...], hgup_ref[...], preferred_element_type=F32)
    ub = jnp.dot(ob_ref[...], gdup_ref[...], preferred_element_type=F32)
    merged = ga_ref[...] * ua + gb_ref[...] * ub
    h2 = x_ref[...] + jnp.dot(merged.astype(BF16), wout_ref[...], preferred_element_type=F32)
    h2_ref[...] = h2
    xn = h2 * lax.rsqrt(jnp.mean(h2 * h2, axis=-1, keepdims=True) + EPS) * ng_ref[...]
    xn_ref[...] = _pack_halves(xn)
    xh = xn.astype(BF16)
    xl = (xn - xh.astype(F32)).astype(BF16)
    wh, wl = rwh_ref[...], rwl_ref[...]
    lg = (jnp.dot(xh, wh, preferred_element_type=F32)
          + jnp.dot(xh, wl, preferred_element_type=F32)
          + jnp.dot(xl, wh, preferred_element_type=F32)) + rb_ref[...]
    info, carry = _route_tile(lg, ltri_ref[...], carry_ref[...])
    info_ref[...] = info
    carry_ref[...] = carry
    cnt_ref[...] = carry


def _merge(o_a, o_b, proj, x2d, hg_up, gd_up, w_out, norm_g, rw_hi, rw_lo, rb):
    t = x2d.shape[0]
    tm = TOKEN_TM
    row = lambda i: (i, 0)
    const = lambda i: (0, 0)
    ltri = jnp.asarray(np.tril(np.ones((tm, tm), np.float32), -1), dtype=BF16)
    return pl.pallas_call(
        _merge_kernel,
        grid=(t // tm,),
        in_specs=[pl.BlockSpec((tm, WIDTH), row),
                  pl.BlockSpec((tm, WIDTH), row),
                  pl.BlockSpec((tm, D_MODEL), lambda i: (i, 0)),
                  pl.BlockSpec((tm, D_MODEL), lambda i: (i, 1)),
                  pl.BlockSpec((tm, D_MODEL), row),
                  pl.BlockSpec((WIDTH, D_MODEL), const),
                  pl.BlockSpec((WIDTH, D_MODEL), const),
                  pl.BlockSpec((D_MODEL, D_MODEL), const),
                  pl.BlockSpec((1, D_MODEL), const),
                  pl.BlockSpec((D_MODEL, LANES), const),
                  pl.BlockSpec((D_MODEL, LANES), const),
                  pl.BlockSpec((1, LANES), const),
                  pl.BlockSpec((tm, tm), const)],
        out_specs=[pl.BlockSpec((tm, D_MODEL), row),
                   pl.BlockSpec((tm, HALF), row),
                   pl.BlockSpec((tm, LANES), row),
                   pl.BlockSpec((1, LANES), const)],
        out_shape=[jax.ShapeDtypeStruct((t, D_MODEL), F32),
                   jax.ShapeDtypeStruct((t, HALF), U32),
                   jax.ShapeDtypeStruct((t, LANES), F32),
                   jax.ShapeDtypeStruct((1, LANES), F32)],
        scratch_shapes=[pltpu.VMEM((1, LANES), F32)],
        compiler_params=pltpu.CompilerParams(
            dimension_semantics=("arbitrary",), vmem_limit_bytes=VMEM_LIMIT),
        name="merge",
    )(o_a, o_b, proj, proj, x2d, hg_up, gd_up, w_out, norm_g, rw_hi, rw_lo, rb, ltri)


def _sc_gather(table, idx):
    n_idx = idx.shape[0]
    cols = table.shape[1]
    per_worker = n_idx // SC_WORKERS
    assert n_idx % (SC_WORKERS * SC_CHUNK) == 0
    mesh = plsc.VectorSubcoreMesh(core_axis_name="c", subcore_axis_name="s")

    @functools.partial(
        pl.kernel, mesh=mesh,
        out_type=jax.ShapeDtypeStruct((n_idx, cols), table.dtype),
        scratch_types=[pltpu.VMEM((SC_CHUNK,), jnp.int32),
                       pltpu.VMEM((SC_CHUNK, cols), table.dtype),
                       pltpu.SemaphoreType.DMA],
    )
    def gather(table_hbm, idx_hbm, out_hbm, idx_v, rows_v, sem):
        worker = lax.axis_index("s") * SC_CORES + lax.axis_index("c")
        base = worker * per_worker

        @pl.loop(0, per_worker // SC_CHUNK)
        def _(c):
            off = pl.multiple_of(base + c * SC_CHUNK, SC_CHUNK)
            pltpu.sync_copy(idx_hbm.at[pl.ds(off, SC_CHUNK)], idx_v)
            pltpu.async_copy(table_hbm.at[idx_v], rows_v, sem).wait()
            pltpu.sync_copy(rows_v, out_hbm.at[pl.ds(off, SC_CHUNK)])

    return gather(table, idx)


def _sc_dispatch(rows, dest, n_out):
    t, cols = rows.shape
    per_worker = t // SC_WORKERS
    assert t % (SC_WORKERS * SC_CHUNK) == 0 and dest.shape[0] == 2 * t
    mesh = plsc.VectorSubcoreMesh(core_axis_name="c", subcore_axis_name="s")

    @functools.partial(
        pl.kernel, mesh=mesh,
        out_type=jax.ShapeDtypeStruct((n_out, cols), rows.dtype),
        scratch_types=[pltpu.VMEM((SC_CHUNK,), jnp.int32),
                       pltpu.VMEM((SC_CHUNK, cols), rows.dtype)],
    )
    def dispatch(rows_hbm, dest_hbm, out_hbm, idx_v, rows_v):
        worker = lax.axis_index("s") * SC_CORES + lax.axis_index("c")
        base = worker * per_worker

        @pl.loop(0, per_worker // SC_CHUNK)
        def _(c):
            off = pl.multiple_of(base + c * SC_CHUNK, SC_CHUNK)
            pltpu.sync_copy(rows_hbm.at[pl.ds(off, SC_CHUNK)], rows_v)
            for slot in range(2):
                pltpu.sync_copy(dest_hbm.at[pl.ds(slot * t + off, SC_CHUNK)], idx_v)
                pltpu.sync_copy(rows_v, out_hbm.at[idx_v])

    return dispatch(rows, dest)


def _moe_kernel(be_ref, nu_ref, x_ref, wg_ref, wu_ref, wd_ref, y_ref, wg_bf, wu_bf, wd_bf):
    i = pl.program_id(0)

    @pl.when((i == 0) | (be_ref[i] != be_ref[jnp.maximum(i - 1, 0)]))
    def _():
        _cast_rows(wg_ref, wg_bf)
        _cast_rows(wu_ref, wu_bf)
        _cast_rows(wd_ref, wd_bf)

    @pl.when(i < nu_ref[0])
    def _():
        xb = _unpack_halves(x_ref[...]).astype(BF16)
        a = jnp.dot(xb, wg_bf[...], preferred_element_type=F32)
        u = jnp.dot(xb, wu_bf[...], preferred_element_type=F32)
        y = jnp.dot((_silu(a) * u).astype(BF16), wd_bf[...], preferred_element_type=F32)
        y_ref[...] = _pack_halves(y)

    @pl.when(i >= nu_ref[0])
    def _():
        y_ref[...] = jnp.zeros_like(y_ref)


def _moe(blk_expert, n_used, x_sorted, w_gate, w_up, w_down):
    n_blocks = blk_expert.shape[0]
    wspec = lambda shape: pl.BlockSpec((None,) + shape, lambda i, be, nu: (be[i], 0, 0))
    rows = pl.BlockSpec((MOE_ROWS, HALF), lambda i, be, nu: (i, 0))
    grid_spec = pltpu.PrefetchScalarGridSpec(
        num_scalar_prefetch=2,
        grid=(n_blocks,),
        in_specs=[rows, wspec((D_MODEL, D_FF)), wspec((D_MODEL, D_FF)), wspec((D_FF, D_MODEL))],
        out_specs=rows,
        scratch_shapes=[pltpu.VMEM((D_MODEL, D_FF), BF16), pltpu.VMEM((D_MODEL, D_FF), BF16),
                        pltpu.VMEM((D_FF, D_MODEL), BF16)],
    )
    return pl.pallas_call(
        _moe_kernel,
        grid_spec=grid_spec,
        out_shape=jax.ShapeDtypeStruct(x_sorted.shape, U32),
        compiler_params=pltpu.CompilerParams(
            dimension_semantics=("arbitrary",), vmem_limit_bytes=VMEM_LIMIT),
        name="moe",
    )(blk_expert, n_used, x_sorted, w_gate, w_up, w_down)


def _combine_kernel(h2_ref, rw_ref, g_ref, y0_ref, y1_ref, o_ref):
    rw = rw_ref[...]
    h = h2_ref[...] + rw[:, 0:1] * _unpack_halves(y0_ref[...]) + rw[:, 1:2] * _unpack_halves(y1_ref[...])
    o_ref[...] = h * lax.rsqrt(jnp.mean(h * h, axis=-1, keepdims=True) + EPS) * g_ref[...]


def _combine(h2, rweights, final_g, y):
    t = h2.shape[0]
    tm = TOKEN_TM
    row = lambda i: (i, 0)
    return pl.pallas_call(
        _combine_kernel,
        grid=(t // tm,),
        in_specs=[pl.BlockSpec((tm, D_MODEL), row),
                  pl.BlockSpec((tm, LANES), row),
                  pl.BlockSpec((1, D_MODEL), lambda i: (0, 0)),
                  pl.BlockSpec((tm, HALF), row),
                  pl.BlockSpec((tm, HALF), lambda i: (i + t // tm, 0))],
        out_specs=pl.BlockSpec((tm, D_MODEL), row),
        out_shape=jax.ShapeDtypeStruct((t, D_MODEL), F32),
        compiler_params=pltpu.CompilerParams(
            dimension_semantics=("arbitrary",), vmem_limit_bytes=VMEM_LIMIT),
        name="combine",
    )(h2, rweights, final_g, y, y)


def _block_layout(info, counts_row, t):
    n_blocks = 2 * t // MOE_ROWS + N_EXPERTS
    counts = counts_row[0, ROUTE_LANE0:ROUTE_LANE0 + N_EXPERTS].astype(jnp.int32)
    padded = ((counts + MOE_ROWS - 1) // MOE_ROWS) * MOE_ROWS
    pend = jnp.cumsum(padded)
    pstart = pend - padded
    blk_first = jnp.arange(n_blocks, dtype=jnp.int32) * MOE_ROWS
    blk_expert = jnp.minimum(jnp.sum((pend[None, :] <= blk_first[:, None]).astype(jnp.int32), axis=1),
                             N_EXPERTS - 1).astype(jnp.int32)
    n_used = (pend[-1] // MOE_ROWS).astype(jnp.int32).reshape(1)
    expert = info[:, 2:4].astype(jnp.int32)
    rank = info[:, 4:6].astype(jnp.int32)
    experts = jnp.arange(N_EXPERTS, dtype=jnp.int32)
    first_row = jnp.sum(jnp.where(expert[:, :, None] == experts, pstart, 0), axis=-1)
    dest = (first_row + rank).T.reshape(2 * t)
    return dest, blk_expert, n_used, n_blocks * MOE_ROWS


def kernel(x, meta_tokens, hg_lb_logits, norm_mix_g, w_in, gd_conv_w, gd_A_log, gd_dt_bias, hg_norm_g, gd_norm_g, hg_up, gd_up, w_out, norm_ffn_g, router_group_w, router_group_b, router_expert_w, router_expert_b, w_gate, w_up, w_down, final_norm_g):
    bsz, seq, d = x.shape
    t = bsz * seq
    x2d = x.reshape(t, d)

    lb = jnp.cumsum(jax.nn.softmax(hg_lb_logits.astype(F32), axis=0), axis=0)[0].reshape(1, WIDTH)
    w = w_in[0].astype(F32)
    n_h, n_g = 4 * WIDTH, 8 * WIDTH
    n_s = n_g + 2 * HEADS
    w_groups = {
        "hgrn": w[:, :n_h],
        "gdn": jnp.concatenate([w[:, n_h:n_s], jnp.zeros((d, LANES - 2 * HEADS), F32)], axis=1),
        "gates": w[:, n_s:],
    }
    g_mix = norm_mix_g[0].reshape(1, d)
    meta_blk = jnp.concatenate([jnp.zeros((CHUNK - N_META, d), F32), meta_tokens.astype(F32)], axis=0)

    proj = {k: _norm_proj(x2d, g_mix, lb, wk, PROJ_TM, k) for k, wk in w_groups.items()}
    proj_meta = {k: _norm_proj(meta_blk, g_mix, lb, w_groups[k], CHUNK, k) for k in ("hgrn", "gdn")}

    def scalar_rows(p, nb, nc):
        s = p[:, 4 * WIDTH:4 * WIDTH + 2 * HEADS].reshape(nb, nc, CHUNK, 2, N_STACKS, HEADS_PER_STACK)
        return s.transpose(0, 1, 4, 3, 5, 2).reshape(nb, nc, N_STACKS, 2, STACK)

    s_meta = jnp.broadcast_to(scalar_rows(proj_meta["gdn"], 1, 1), (bsz, 1, N_STACKS, 2, STACK))
    scal = jnp.concatenate([s_meta, scalar_rows(proj["gdn"], bsz, seq // CHUNK)], axis=1)
    a_log_row = jnp.repeat(gd_A_log[0].astype(F32), CHUNK).reshape(N_STACKS, 1, STACK)
    dt_row = jnp.repeat(gd_dt_bias[0].astype(F32), CHUNK).reshape(N_STACKS, 1, STACK)
    o_a, o_b = _mixers(proj_meta["hgrn"], proj["hgrn"].reshape(bsz, seq, -1), hg_norm_g[0].reshape(1, HEAD_DIM),
                       proj_meta["gdn"], proj["gdn"].reshape(bsz, seq, -1), scal, a_log_row, dt_row,
                       gd_conv_w[0].astype(F32), gd_norm_g[0].reshape(1, HEAD_DIM))

    rw = jnp.zeros((d, LANES), F32)
    rw = rw.at[:, :N_GROUPS].set(router_group_w[0]).at[:, N_GROUPS:N_GROUPS + N_EXPERTS].set(router_expert_w[0])
    rw_hi = rw.astype(BF16)
    rw_lo = (rw - rw_hi.astype(F32)).astype(BF16)
    rb = jnp.zeros((1, LANES), F32)
    rb = rb.at[0, :N_GROUPS].set(router_group_b[0]).at[0, N_GROUPS:N_GROUPS + N_EXPERTS].set(router_expert_b[0])
    h2, xn, info, counts = _merge(o_a.reshape(t, WIDTH), o_b.reshape(t, WIDTH), proj["gates"], x2d,
                                  hg_up[0].astype(BF16), gd_up[0].astype(BF16), w_out[0].astype(BF16),
                                  norm_ffn_g[0].reshape(1, d), rw_hi, rw_lo, rb)

    dest, blk_expert, n_used, n_rows = _block_layout(info, counts, t)
    x_sorted = _sc_dispatch(xn, dest, n_rows)
    y_sorted = _moe(blk_expert, n_used, x_sorted,
                    w_gate[0].astype(F32), w_up[0].astype(F32), w_down[0].astype(F32))
    y_tok = _sc_gather(y_sorted, dest)
    out = _combine(h2, info, final_norm_g.reshape(1, d), y_tok)
    return out.reshape(bsz, seq, d)
```

```python
import functools

import numpy as np
import jax
import jax.numpy as jnp
from jax import lax
from jax.experimental import pallas as pl
from jax.experimental.pallas import tpu as pltpu
from jax.experimental.pallas import tpu_sc as plsc

F32 = jnp.float32
BF16 = jnp.bfloat16
U32 = jnp.uint32

D_MODEL = 1024
N_META = 16
CHUNK = 64
EPS = 1e-6
HEADS = 4
HEAD_DIM = 128
WIDTH = HEADS * HEAD_DIM
HEADS_PER_STACK = 2
STACK = HEADS_PER_STACK * CHUNK
N_STACKS = HEADS // HEADS_PER_STACK
CONV_W = 4
N_GROUPS = 4
EXPERTS_PER_GROUP = 8
N_EXPERTS = N_GROUPS * EXPERTS_PER_GROUP
D_FF = 512
LANES = 128
SUBLANES = 8
HALF = D_MODEL // 2
MOE_ROWS = 512
CAST_SLAB = 128
SC_CORES, SC_SUBCORES = 2, 16
SC_WORKERS = SC_CORES * SC_SUBCORES
SC_CHUNK = 128
PROJ_TM = 1024
TOKEN_TM = 512
VMEM_LIMIT = 48 * 1024 * 1024


def _bdot(a, b):
    return lax.dot_general(a.astype(BF16), b.astype(BF16), (((2,), (1,)), ((0,), (0,))),
                           preferred_element_type=F32)


def _bdot_nt(a, b):
    return lax.dot_general(a.astype(BF16), b.astype(BF16), (((2,), (2,)), ((0,), (0,))),
                           preferred_element_type=F32)


def _bdot_tn(a, b):
    return lax.dot_general(a.astype(BF16), b.astype(BF16), (((1,), (1,)), ((0,), (0,))),
                           preferred_element_type=F32)


def _split3(x):
    hi = x.astype(BF16)
    r1 = x - hi.astype(F32)
    mid = r1.astype(BF16)
    lo = (r1 - mid.astype(F32)).astype(BF16)
    return hi, mid, lo


def _sigmoid(x):
    return 1.0 / (1.0 + jnp.exp(-x))


def _silu(x):
    return x * _sigmoid(x)


def _softplus(x):
    return jnp.maximum(x, 0.0) + jnp.log(1.0 + jnp.exp(-jnp.abs(x)))


def _stack_heads(x, bsz):
    x3 = x.reshape(bsz, CHUNK, WIDTH)
    y = jnp.concatenate([x3[:, :, h * HEAD_DIM:(h + 1) * HEAD_DIM] for h in range(HEADS)], axis=1)
    return y.reshape(bsz * HEADS * CHUNK, HEAD_DIM)


def _unstack_heads(x, bsz):
    x3 = x.reshape(bsz, HEADS * CHUNK, HEAD_DIM)
    return jnp.concatenate([x3[:, h * CHUNK:(h + 1) * CHUNK, :] for h in range(HEADS)], axis=2)


def _to_stacks(a, bsz):
    return a.reshape(bsz * N_STACKS, STACK, a.shape[-1])


def _to_heads(a, bsz):
    return a.reshape(bsz * HEADS, CHUNK, a.shape[-1])


def _roll_rows(x, shift):
    n = x.shape[0]
    return pltpu.roll(x, shift % n, axis=0)


def _load_chunk(is_meta, meta_ref, real_ref, bsz):
    x = jnp.where(is_meta, meta_ref[...][None], real_ref[...])
    return x.reshape(bsz * CHUNK, WIDTH)


def _cast_rows(src_ref, dst_ref):
    slab = CAST_SLAB
    rows = src_ref.shape[0]
    assert rows % slab == 0

    def body(k, carry):
        r = pl.multiple_of(k * slab, slab)
        dst_ref[pl.ds(r, slab), :] = src_ref[pl.ds(r, slab), :].astype(dst_ref.dtype)
        return carry

    lax.fori_loop(0, rows // slab, body, 0)


def _pack_halves(x):
    lo = lax.bitcast_convert_type(x[:, :HALF].astype(BF16).astype(F32), U32)
    hi = lax.bitcast_convert_type(x[:, HALF:].astype(BF16).astype(F32), U32)
    return (lo >> 16) | (hi & jnp.uint32(0xFFFF0000))


def _unpack_halves(p):
    lo = lax.bitcast_convert_type(p << 16, F32)
    hi = lax.bitcast_convert_type(p & jnp.uint32(0xFFFF0000), F32)
    return jnp.concatenate([lo, hi], axis=1)


def _normed_bf16(x_ref, g_ref):
    x = x_ref[...]
    return (x * lax.rsqrt(jnp.mean(x * x, axis=-1, keepdims=True) + EPS) * g_ref[...]).astype(BF16)


def _norm_proj_kernel(x_ref, g_ref, lb_ref, w_ref, o_ref, wbf_ref, *, group):
    @pl.when(pl.program_id(0) == 0)
    def _():
        _cast_rows(w_ref, wbf_ref)

    acc = jnp.dot(_normed_bf16(x_ref, g_ref), wbf_ref[...], preferred_element_type=F32)
    w1, w2, w3 = WIDTH, 2 * WIDTH, 3 * WIDTH
    if group == "hgrn":
        lb = lb_ref[...]
        o_ref[:, :w1] = _silu(acc[:, :w1])
        o_ref[:, w1:w2] = lb + (1.0 - lb) * _sigmoid(acc[:, w1:w2])
        o_ref[:, w2:w3] = acc[:, w2:w3]
        o_ref[:, w3:] = _silu(acc[:, w3:])
    else:
        o_ref[...] = _sigmoid(acc)


def _norm_proj_conv_kernel(x_ref, g_ref, w_ref, cw_ref, hist_ref, o_ref, tail_ref, wbf_ref, carry_ref,
                           *, steps_per_seq):
    i = pl.program_id(0)

    @pl.when(i == 0)
    def _():
        _cast_rows(w_ref, wbf_ref)

    xn = _normed_bf16(x_ref, g_ref)
    w3, w4 = 3 * WIDTH, 4 * WIDTH
    raw = jnp.dot(xn, wbf_ref[:, :w3], preferred_element_type=F32)
    rest = jnp.dot(xn, wbf_ref[:, w3:], preferred_element_type=F32)
    rows = raw.shape[0]
    hist = jnp.where(i % steps_per_seq == 0, hist_ref[...], carry_ref[...])
    cw = cw_ref[...]
    tap = lambda d: cw[CONV_W - 1 - d:CONV_W - d, :]
    body = raw * tap(0)
    head = raw[:SUBLANES]
    r8 = lax.broadcasted_iota(jnp.int32, head.shape, 0)
    first = head * tap(0)
    for d in range(1, CONV_W):
        body = body + _roll_rows(raw, d) * tap(d)
        first = first + jnp.where(r8 >= d, _roll_rows(head, d), _roll_rows(hist, d)) * tap(d)
    o_ref[:SUBLANES, :w3] = _silu(first)
    o_ref[SUBLANES:, :w3] = _silu(body[SUBLANES:])
    o_ref[:, w3:w4] = _silu(rest[:, :WIDTH])
    o_ref[:, w4:] = rest[:, WIDTH:]
    tail = raw[rows - SUBLANES:]
    carry_ref[...] = tail
    tail_ref[...] = tail


def _proj_call(kernel_fn, name, x2d, tm, n, operands, specs, extra_out=(), extra_scratch=()):
    m, k = x2d.shape
    tm = min(tm, m)
    out_specs = [pl.BlockSpec((tm, n), lambda i: (i, 0))] + [s for s, _ in extra_out]
    out_shape = [jax.ShapeDtypeStruct((m, n), F32)] + [s for _, s in extra_out]
    return pl.pallas_call(
        kernel_fn,
        grid=(m // tm,),
        in_specs=[pl.BlockSpec((tm, k), lambda i: (i, 0))] + specs,
        out_specs=out_specs,
        out_shape=out_shape,
        scratch_shapes=[pltpu.VMEM((k, n), BF16)] + list(extra_scratch),
        compiler_params=pltpu.CompilerParams(
            dimension_semantics=("arbitrary",), vmem_limit_bytes=VMEM_LIMIT),
        name=name,
    )(x2d, *operands)


def _weight_spec(k, n):
    return pl.BlockSpec((k, n), lambda i: (0, 0), pipeline_mode=pl.Buffered(1))


def _norm_proj(x2d, g, lb, w, tm, group):
    k, n = w.shape
    const = lambda i: (0, 0)
    specs = [pl.BlockSpec((1, k), const), pl.BlockSpec((1, WIDTH), const), _weight_spec(k, n)]
    return _proj_call(functools.partial(_norm_proj_kernel, group=group), "norm_proj_" + group,
                      x2d, tm, n, (g, lb, w), specs)[0]


def _norm_proj_conv(x2d, g, w, conv_w, hist, tm, rows_per_seq):
    k, n = w.shape
    const = lambda i: (0, 0)
    tm = min(tm, rows_per_seq)
    assert rows_per_seq % tm == 0 and x2d.shape[0] % rows_per_seq == 0
    tail_shape = (SUBLANES, 3 * WIDTH)
    specs = [pl.BlockSpec((1, k), const), _weight_spec(k, n),
             pl.BlockSpec((CONV_W, 3 * WIDTH), const), pl.BlockSpec(tail_shape, const)]
    return _proj_call(functools.partial(_norm_proj_conv_kernel, steps_per_seq=rows_per_seq // tm),
                      "norm_proj_gdn", x2d, tm, n, (g, w, conv_w, hist), specs,
                      extra_out=[(pl.BlockSpec(tail_shape, const), jax.ShapeDtypeStruct(tail_shape, F32))],
                      extra_scratch=[pltpu.VMEM(tail_shape, F32)])


HGRN_LEVELS = 6
HGRN_DIAG = HGRN_LEVELS


def _hgrn_level_table():
    r = np.arange(STACK)
    t, hd = r % CHUNK, r // CHUNK
    x = t[:, None] ^ t[None, :]
    lv = np.floor(np.log2(np.maximum(x, 1))).astype(np.int32)
    valid = (hd[:, None] == hd[None, :]) & (t[:, None] > t[None, :])
    lv = np.where(valid, lv, -1)
    lv = np.where(r[:, None] == r[None, :], HGRN_DIAG, lv)
    return jnp.asarray(lv, dtype=jnp.int32)


def _hgrn_kernel(lv_ref, g_ref, mq_ref, mf_ref, mi_ref, mg_ref,
                 rq_ref, rf_ref, ri_ref, rg_ref, o_ref, st_ref, *, bsz):
    c = pl.program_id(0)

    @pl.when(c == 0)
    def _():
        st_ref[...] = jnp.zeros_like(st_ref)

    is_meta = c == 0
    q = _load_chunk(is_meta, mq_ref, rq_ref, bsz)
    f = _load_chunk(is_meta, mf_ref, rf_ref, bsz)
    v = _load_chunk(is_meta, mi_ref, ri_ref, bsz)
    gate = _load_chunk(is_meta, mg_ref, rg_ref, bsz)

    per_seq = lambda a: a.reshape(bsz, CHUNK, WIDTH)
    trow = lax.broadcasted_iota(jnp.int32, (1, CHUNK, WIDTH), 1)
    b = jnp.log(f)
    s = 1
    while s < CHUNK:
        b = (per_seq(b) + jnp.where(trow >= s, per_seq(_roll_rows(b, s)), 0.0)).reshape(b.shape)
        s *= 2

    qs, ks, vs, bs = (_stack_heads(a, bsz) for a in (q, 1.0 - f, v, b))
    lv = lv_ref[...][None]
    to3 = functools.partial(_to_stacks, bsz=bsz)
    to_g = functools.partial(_to_heads, bsz=bsz)
    roll_g = lambda a, shift: to_g(_roll_rows(a.reshape(qs.shape), shift))
    t = lax.broadcasted_iota(jnp.int32, (1, CHUNK, HEAD_DIM), 1)
    qs_g, ks_g, bs_g = to_g(qs), to_g(ks), to_g(bs)

    attn = jnp.where(lv == HGRN_DIAG, _bdot_nt(to3(qs), to3(ks)), 0.0)
    bref = bs_g
    bnext = roll_g(bs_g, -1)
    for li in range(HGRN_LEVELS):
        m = 1 << li
        if li > 0:
            half = m // 2
            upper = (t & (m - 1)) >= half
            bref = jnp.where(upper, roll_g(bref, half), bref)
            bnext = jnp.where(upper, bnext, roll_g(bnext, -half))
        odd = ((t >> li) & 1) == 1
        both = jnp.where(odd, qs_g, ks_g) * jnp.exp(jnp.where(odd, bs_g - bref, bnext - bs_g))
        both = to3(both.reshape(qs.shape)).astype(BF16)
        attn = jnp.where(lv == li, _bdot_nt(both, both), attn)

    o = _bdot(attn, to3(vs))

    st = st_ref[...]
    o_inter = _bdot_nt(qs_g * jnp.exp(bs_g), st)
    b_end = bs_g[:, CHUNK - 1:CHUNK, :]
    ke = to_g(ks) * jnp.exp(b_end - bs_g)
    st_ref[...] = st * jnp.exp(b_end) + _bdot_tn(to_g(vs), ke)
    o = (to_g(o) + o_inter).reshape(qs.shape)

    o = o * lax.rsqrt(jnp.mean(o * o, axis=-1, keepdims=True) + EPS) * g_ref[...]
    o = o * _stack_heads(gate, bsz)
    o_ref[...] = _unstack_heads(o, bsz).astype(o_ref.dtype)


def _chunk_specs(bsz, col0):
    meta = [pl.BlockSpec((CHUNK, WIDTH), functools.partial(lambda c, j: (0, j), j=col0 + j))
            for j in range(4)]
    real = [pl.BlockSpec((bsz, CHUNK, WIDTH),
                         functools.partial(lambda c, j: (0, jnp.maximum(c - 1, 0), j), j=col0 + j))
            for j in range(4)]
    return meta + real


def _gdn_cumsum_matrix():
    r = np.arange(STACK)
    u = (r[:, None] // CHUNK == r[None, :] // CHUNK) & (r[:, None] <= r[None, :])
    return jnp.asarray(u, dtype=BF16)


def _gdn_kernel(u_ref, alog_ref, dt_ref, g_ref, s_ref,
                mq_ref, mk_ref, mv_ref, mz_ref, rq_ref, rk_ref, rv_ref, rz_ref,
                o_ref, st_ref, *, bsz):
    c = pl.program_id(0)

    @pl.when(c == 0)
    def _():
        st_ref[...] = jnp.zeros_like(st_ref)

    is_meta = c == 0
    qs = _stack_heads(_load_chunk(is_meta, mq_ref, rq_ref, bsz), bsz)
    ks = _stack_heads(_load_chunk(is_meta, mk_ref, rk_ref, bsz), bsz)
    vs = _stack_heads(_load_chunk(is_meta, mv_ref, rv_ref, bsz), bsz)
    gate = _load_chunk(is_meta, mz_ref, rz_ref, bsz)
    qs = qs * lax.rsqrt(jnp.sum(qs * qs, axis=-1, keepdims=True) + EPS) * (HEAD_DIM ** -0.5)
    ks = ks * lax.rsqrt(jnp.sum(ks * ks, axis=-1, keepdims=True) + EPS)
    to3 = functools.partial(_to_stacks, bsz=bsz)
    qs, ks, vs = to3(qs), to3(ks), to3(vs)
    n_st = bsz * N_STACKS

    srow = s_ref[...].reshape(n_st, 2, STACK)
    per_stack = lambda ref: jnp.broadcast_to(ref[...][None], (bsz, N_STACKS, 1, STACK)).reshape(n_st, 1, STACK)
    beta_row = _sigmoid(srow[:, 0:1, :])
    g_row = -jnp.exp(per_stack(alog_ref)) * _softplus(srow[:, 1:2, :] + per_stack(dt_ref))
    n8 = n_st * SUBLANES
    g8 = jnp.broadcast_to(g_row, (n_st, SUBLANES, STACK)).reshape(n8, STACK)
    pieces = jnp.concatenate(_split3(g8), axis=0)
    cum = jnp.dot(pieces, u_ref[...], preferred_element_type=F32)
    cum = cum[0:n8] + cum[n8:2 * n8] + cum[2 * n8:]
    cum_row = cum.reshape(n_st, SUBLANES, STACK)[:, 0:1, :]

    r = lax.broadcasted_iota(jnp.int32, (STACK, STACK), 0)
    cc = lax.broadcasted_iota(jnp.int32, (STACK, STACK), 1)
    eye = (r == cc)[None]
    cum_col = jnp.sum(jnp.where(eye, cum_row, 0.0), axis=2, keepdims=True)
    beta_col = jnp.sum(jnp.where(eye, beta_row, 0.0), axis=2, keepdims=True)
    same = (r >> 6) == (cc >> 6)
    lower = (same & (r >= cc))[None]
    strict = (same & (r > cc))[None]
    ratio = jnp.exp(jnp.minimum(cum_col - cum_row, 0.0))

    kkt = _bdot_nt(ks, ks)
    qkt = _bdot_nt(qs, ks)
    lm = jnp.where(strict, beta_col * ratio * kkt, 0.0)

    ident = eye.astype(F32)
    l0 = jnp.where(((r >> 3) == (cc >> 3))[None], lm, 0.0)
    l2 = _bdot(l0, l0)
    l4 = _bdot(l2, l2)
    xinv = _bdot(_bdot(ident - l0, ident + l2), ident + l4)
    for lvl in (3, 4, 5):
        off = (((r >> (lvl + 1)) == (cc >> (lvl + 1))) & ((r >> lvl) != (cc >> lvl)))[None]
        xinv = xinv - _bdot(_bdot(xinv, jnp.where(off, lm, 0.0)), xinv)

    decay_col = jnp.exp(cum_col)
    rhs = jnp.concatenate([beta_col * vs, (beta_col * decay_col) * ks], axis=2)
    sol = _bdot(xinv, rhs)

    to_g = functools.partial(_to_heads, bsz=bsz)
    st = st_ref[...]
    u = to_g(sol[:, :, :HEAD_DIM]) - _bdot(to_g(sol[:, :, HEAD_DIM:]), st)
    o_inter = to_g(decay_col) * _bdot(to_g(qs), st)
    cum_g = to_g(cum_col)
    cum_end = cum_g[:, CHUNK - 1:CHUNK, :]
    kdec = to_g(ks) * jnp.exp(cum_end - cum_g)
    st_ref[...] = jnp.exp(cum_end) * st + _bdot_tn(kdec, u)
    attn = jnp.where(lower, qkt * ratio, 0.0)
    o = o_inter + to_g(_bdot(attn, to3(u)))
    o = o.reshape(bsz * HEADS * CHUNK, HEAD_DIM)

    o = o * lax.rsqrt(jnp.mean(o * o, axis=-1, keepdims=True) + EPS) * g_ref[...]
    o = o * _stack_heads(gate, bsz)
    o_ref[...] = _unstack_heads(o, bsz).astype(o_ref.dtype)


def _recurrence_call(kernel_fn, name, proj_meta, proj_real, operands, specs):
    bsz, seq, _ = proj_real.shape
    return pl.pallas_call(
        functools.partial(kernel_fn, bsz=bsz),
        grid=(seq // CHUNK + 1,),
        in_specs=specs + _chunk_specs(bsz, 0),
        out_specs=pl.BlockSpec((bsz, CHUNK, WIDTH), lambda c: (0, jnp.maximum(c - 1, 0), 0)),
        out_shape=jax.ShapeDtypeStruct((bsz, seq, WIDTH), BF16),
        scratch_shapes=[pltpu.VMEM((bsz * HEADS, HEAD_DIM, HEAD_DIM), F32)],
        compiler_params=pltpu.CompilerParams(
            dimension_semantics=("arbitrary",), vmem_limit_bytes=VMEM_LIMIT),
        name=name,
    )(*operands, *([proj_meta] * 4), *([proj_real] * 4))


def _hgrn(proj_meta, proj_real, norm_g):
    const = lambda c: (0, 0)
    specs = [pl.BlockSpec((STACK, STACK), const), pl.BlockSpec((1, HEAD_DIM), const)]
    return _recurrence_call(_hgrn_kernel, "hgrn", proj_meta, proj_real,
                            (_hgrn_level_table(), norm_g), specs)


def _gdn(proj_meta, proj_real, scal, a_log_row, dt_row, norm_g):
    bsz = proj_real.shape[0]
    const = lambda c: (0, 0)
    const3 = lambda c: (0, 0, 0)
    specs = [pl.BlockSpec((STACK, STACK), const),
             pl.BlockSpec((N_STACKS, 1, STACK), const3),
             pl.BlockSpec((N_STACKS, 1, STACK), const3),
             pl.BlockSpec((1, HEAD_DIM), const),
             pl.BlockSpec((bsz, None, N_STACKS, 2, STACK), lambda c: (0, c, 0, 0, 0))]
    return _recurrence_call(_gdn_kernel, "gdn", proj_meta, proj_real,
                            (_gdn_cumsum_matrix(), a_log_row, dt_row, norm_g, scal), specs)


ROUTE_LANE0 = N_GROUPS
ROUTE_NEG = -1e30


def _route_tile(lg, ltri, carry):
    lane = lax.broadcasted_iota(jnp.int32, lg.shape, 1).astype(F32)
    first = lambda mask: jnp.min(jnp.where(mask, lane, float(LANES)), axis=1, keepdims=True)
    top = lambda mask: jnp.max(jnp.where(mask, lg, ROUTE_NEG), axis=1, keepdims=True)

    is_g = lane < N_GROUPS
    gmax = top(is_g)
    grp = first(is_g & (lg == gmax))
    p_grp = 1.0 / jnp.sum(jnp.where(is_g, jnp.exp(lg - gmax), 0.0), axis=1, keepdims=True)

    lo = ROUTE_LANE0 + grp * EXPERTS_PER_GROUP
    in_grp = (lane >= lo) & (lane < lo + EXPERTS_PER_GROUP)
    m1 = top(in_grp)
    i1 = first(in_grp & (lg == m1))
    rest = in_grp & (lane != i1)
    m2 = top(rest)
    i2 = first(rest & (lg == m2))
    r = jnp.exp(m2 - m1)
    w0 = p_grp / (1.0 + r)
    w1 = p_grp * r / (1.0 + r)

    hot0 = (lane == i1).astype(F32)
    hot1 = (lane == i2).astype(F32)
    both = hot0 + hot1
    before = jnp.dot(ltri, both.astype(BF16), preferred_element_type=F32) + carry
    rank0 = jnp.sum(hot0 * before, axis=1, keepdims=True)
    rank1 = jnp.sum(hot1 * before, axis=1, keepdims=True)
    cols = (w0, w1, i1 - ROUTE_LANE0, i2 - ROUTE_LANE0, rank0, rank1)
    info = jnp.zeros_like(lg)
    for k, col in enumerate(cols):
        info = jnp.where(lane == k, col, info)
    return info, carry + jnp.sum(both, axis=0, keepdims=True)


def _merge_kernel(oa_ref, ob_ref, ga_ref, gb_ref, x_ref, hgup_ref, gdup_ref, wout_ref,
                  ng_ref, rwh_ref, rwl_ref, rb_ref, ltri_ref, h2_ref, xn_ref, info_ref, cnt_ref, carry_ref):
    @pl.when(pl.program_id(0) == 0)
    def _():
        carry_ref[...] = jnp.zeros_like(carry_ref)

    ua = jnp.dot(oa_ref[...], hgup_ref[...], preferred_element_type=F32)
    ub = jnp.dot(ob_ref[...], gdup_ref[...], preferred_element_type=F32)
    merged = ga_ref[...] * ua + gb_ref[...] * ub
    h2 = x_ref[...] + jnp.dot(merged.astype(BF16), wout_ref[...], preferred_element_type=F32)
    h2_ref[...] = h2
    xn = h2 * lax.rsqrt(jnp.mean(h2 * h2, axis=-1, keepdims=True) + EPS) * ng_ref[...]
    xn_ref[...] = _pack_halves(xn)
    xh = xn.astype(BF16)
    xl = (xn - xh.astype(F32)).astype(BF16)
    wh, wl = rwh_ref[...], rwl_ref[...]
    lg = (jnp.dot(xh, wh, preferred_element_type=F32)
          + jnp.dot(xh, wl, preferred_element_type=F32)
          + jnp.dot(xl, wh, preferred_element_type=F32)) + rb_ref[...]
    info, carry = _route_tile(lg, ltri_ref[...], carry_ref[...])
    info_ref[...] = info
    carry_ref[...] = carry
    cnt_ref[...] = carry


def _merge(o_a, o_b, proj, x2d, hg_up, gd_up, w_out, norm_g, rw_hi, rw_lo, rb):
    t = x2d.shape[0]
    tm = TOKEN_TM
    row = lambda i: (i, 0)
    const = lambda i: (0, 0)
    ltri = jnp.asarray(np.tril(np.ones((tm, tm), np.float32), -1), dtype=BF16)
    return pl.pallas_call(
        _merge_kernel,
        grid=(t // tm,),
        in_specs=[pl.BlockSpec((tm, WIDTH), row),
                  pl.BlockSpec((tm, WIDTH), row),
                  pl.BlockSpec((tm, D_MODEL), lambda i: (i, 0)),
                  pl.BlockSpec((tm, D_MODEL), lambda i: (i, 1)),
                  pl.BlockSpec((tm, D_MODEL), row),
                  pl.BlockSpec((WIDTH, D_MODEL), const),
                  pl.BlockSpec((WIDTH, D_MODEL), const),
                  pl.BlockSpec((D_MODEL, D_MODEL), const),
                  pl.BlockSpec((1, D_MODEL), const),
                  pl.BlockSpec((D_MODEL, LANES), const),
                  pl.BlockSpec((D_MODEL, LANES), const),
                  pl.BlockSpec((1, LANES), const),
                  pl.BlockSpec((tm, tm), const)],
        out_specs=[pl.BlockSpec((tm, D_MODEL), row),
                   pl.BlockSpec((tm, HALF), row),
                   pl.BlockSpec((tm, LANES), row),
                   pl.BlockSpec((1, LANES), const)],
        out_shape=[jax.ShapeDtypeStruct((t, D_MODEL), F32),
                   jax.ShapeDtypeStruct((t, HALF), U32),
                   jax.ShapeDtypeStruct((t, LANES), F32),
                   jax.ShapeDtypeStruct((1, LANES), F32)],
        scratch_shapes=[pltpu.VMEM((1, LANES), F32)],
        compiler_params=pltpu.CompilerParams(
            dimension_semantics=("arbitrary",), vmem_limit_bytes=VMEM_LIMIT),
        name="merge",
    )(o_a, o_b, proj, proj, x2d, hg_up, gd_up, w_out, norm_g, rw_hi, rw_lo, rb, ltri)


def _sc_gather(table, idx):
    n_idx = idx.shape[0]
    cols = table.shape[1]
    per_worker = n_idx // SC_WORKERS
    assert n_idx % (SC_WORKERS * SC_CHUNK) == 0
    mesh = plsc.VectorSubcoreMesh(core_axis_name="c", subcore_axis_name="s")

    @functools.partial(
        pl.kernel, mesh=mesh,
        out_type=jax.ShapeDtypeStruct((n_idx, cols), table.dtype),
        scratch_types=[pltpu.VMEM((SC_CHUNK,), jnp.int32),
                       pltpu.VMEM((SC_CHUNK, cols), table.dtype),
                       pltpu.SemaphoreType.DMA],
    )
    def gather(table_hbm, idx_hbm, out_hbm, idx_v, rows_v, sem):
        worker = lax.axis_index("s") * SC_CORES + lax.axis_index("c")
        base = worker * per_worker

        @pl.loop(0, per_worker // SC_CHUNK)
        def _(c):
            off = pl.multiple_of(base + c * SC_CHUNK, SC_CHUNK)
            pltpu.sync_copy(idx_hbm.at[pl.ds(off, SC_CHUNK)], idx_v)
            pltpu.async_copy(table_hbm.at[idx_v], rows_v, sem).wait()
            pltpu.sync_copy(rows_v, out_hbm.at[pl.ds(off, SC_CHUNK)])

    return gather(table, idx)


def _sc_dispatch(rows, dest, n_out):
    t, cols = rows.shape
    per_worker = t // SC_WORKERS
    assert t % (SC_WORKERS * SC_CHUNK) == 0 and dest.shape[0] == 2 * t
    mesh = plsc.VectorSubcoreMesh(core_axis_name="c", subcore_axis_name="s")

    @functools.partial(
        pl.kernel, mesh=mesh,
        out_type=jax.ShapeDtypeStruct((n_out, cols), rows.dtype),
        scratch_types=[pltpu.VMEM((SC_CHUNK,), jnp.int32),
                       pltpu.VMEM((SC_CHUNK, cols), rows.dtype)],
    )
    def dispatch(rows_hbm, dest_hbm, out_hbm, idx_v, rows_v):
        worker = lax.axis_index("s") * SC_CORES + lax.axis_index("c")
        base = worker * per_worker

        @pl.loop(0, per_worker // SC_CHUNK)
        def _(c):
            off = pl.multiple_of(base + c * SC_CHUNK, SC_CHUNK)
            pltpu.sync_copy(rows_hbm.at[pl.ds(off, SC_CHUNK)], rows_v)
            for slot in range(2):
                pltpu.sync_copy(dest_hbm.at[pl.ds(slot * t + off, SC_CHUNK)], idx_v)
                pltpu.sync_copy(rows_v, out_hbm.at[idx_v])

    return dispatch(rows, dest)


def _moe_kernel(be_ref, nu_ref, x_ref, wg_ref, wu_ref, wd_ref, y_ref, wg_bf, wu_bf, wd_bf):
    i = pl.program_id(0)

    @pl.when((i == 0) | (be_ref[i] != be_ref[jnp.maximum(i - 1, 0)]))
    def _():
        _cast_rows(wg_ref, wg_bf)
        _cast_rows(wu_ref, wu_bf)
        _cast_rows(wd_ref, wd_bf)

    @pl.when(i < nu_ref[0])
    def _():
        xb = _unpack_halves(x_ref[...]).astype(BF16)
        a = jnp.dot(xb, wg_bf[...], preferred_element_type=F32)
        u = jnp.dot(xb, wu_bf[...], preferred_element_type=F32)
        y = jnp.dot((_silu(a) * u).astype(BF16), wd_bf[...], preferred_element_type=F32)
        y_ref[...] = _pack_halves(y)

    @pl.when(i >= nu_ref[0])
    def _():
        y_ref[...] = jnp.zeros_like(y_ref)


def _moe(blk_expert, n_used, x_sorted, w_gate, w_up, w_down):
    n_blocks = blk_expert.shape[0]
    wspec = lambda shape: pl.BlockSpec((None,) + shape, lambda i, be, nu: (be[i], 0, 0))
    rows = pl.BlockSpec((MOE_ROWS, HALF), lambda i, be, nu: (i, 0))
    grid_spec = pltpu.PrefetchScalarGridSpec(
        num_scalar_prefetch=2,
        grid=(n_blocks,),
        in_specs=[rows, wspec((D_MODEL, D_FF)), wspec((D_MODEL, D_FF)), wspec((D_FF, D_MODEL))],
        out_specs=rows,
        scratch_shapes=[pltpu.VMEM((D_MODEL, D_FF), BF16), pltpu.VMEM((D_MODEL, D_FF), BF16),
                        pltpu.VMEM((D_FF, D_MODEL), BF16)],
    )
    return pl.pallas_call(
        _moe_kernel,
        grid_spec=grid_spec,
        out_shape=jax.ShapeDtypeStruct(x_sorted.shape, U32),
        compiler_params=pltpu.CompilerParams(
            dimension_semantics=("arbitrary",), vmem_limit_bytes=VMEM_LIMIT),
        name="moe",
    )(blk_expert, n_used, x_sorted, w_gate, w_up, w_down)


def _combine_kernel(h2_ref, rw_ref, g_ref, y0_ref, y1_ref, o_ref):
    rw = rw_ref[...]
    h = h2_ref[...] + rw[:, 0:1] * _unpack_halves(y0_ref[...]) + rw[:, 1:2] * _unpack_halves(y1_ref[...])
    o_ref[...] = h * lax.rsqrt(jnp.mean(h * h, axis=-1, keepdims=True) + EPS) * g_ref[...]


def _combine(h2, rweights, final_g, y):
    t = h2.shape[0]
    tm = TOKEN_TM
    row = lambda i: (i, 0)
    return pl.pallas_call(
        _combine_kernel,
        grid=(t // tm,),
        in_specs=[pl.BlockSpec((tm, D_MODEL), row),
                  pl.BlockSpec((tm, LANES), row),
                  pl.BlockSpec((1, D_MODEL), lambda i: (0, 0)),
                  pl.BlockSpec((tm, HALF), row),
                  pl.BlockSpec((tm, HALF), lambda i: (i + t // tm, 0))],
        out_specs=pl.BlockSpec((tm, D_MODEL), row),
        out_shape=jax.ShapeDtypeStruct((t, D_MODEL), F32),
        compiler_params=pltpu.CompilerParams(
            dimension_semantics=("arbitrary",), vmem_limit_bytes=VMEM_LIMIT),
        name="combine",
    )(h2, rweights, final_g, y, y)


def _block_layout(info, counts_row, t):
    n_blocks = 2 * t // MOE_ROWS + N_EXPERTS
    counts = counts_row[0, ROUTE_LANE0:ROUTE_LANE0 + N_EXPERTS].astype(jnp.int32)
    padded = ((counts + MOE_ROWS - 1) // MOE_ROWS) * MOE_ROWS
    pend = jnp.cumsum(padded)
    pstart = pend - padded
    blk_first = jnp.arange(n_blocks, dtype=jnp.int32) * MOE_ROWS
    blk_expert = jnp.minimum(jnp.sum((pend[None, :] <= blk_first[:, None]).astype(jnp.int32), axis=1),
                             N_EXPERTS - 1).astype(jnp.int32)
    n_used = (pend[-1] // MOE_ROWS).astype(jnp.int32).reshape(1)
    expert = info[:, 2:4].astype(jnp.int32)
    rank = info[:, 4:6].astype(jnp.int32)
    experts = jnp.arange(N_EXPERTS, dtype=jnp.int32)
    first_row = jnp.sum(jnp.where(expert[:, :, None] == experts, pstart, 0), axis=-1)
    dest = (first_row + rank).T.reshape(2 * t)
    return dest, blk_expert, n_used, n_blocks * MOE_ROWS


def kernel(x, meta_tokens, hg_lb_logits, norm_mix_g, w_in, gd_conv_w, gd_A_log, gd_dt_bias, hg_norm_g, gd_norm_g, hg_up, gd_up, w_out, norm_ffn_g, router_group_w, router_group_b, router_expert_w, router_expert_b, w_gate, w_up, w_down, final_norm_g):
    bsz, seq, d = x.shape
    t = bsz * seq
    x2d = x.reshape(t, d)

    lb = jnp.cumsum(jax.nn.softmax(hg_lb_logits.astype(F32), axis=0), axis=0)[0].reshape(1, WIDTH)
    w = w_in[0].astype(F32)
    n_h, n_g = 4 * WIDTH, 8 * WIDTH
    n_s = n_g + 2 * HEADS
    w_groups = {
        "hgrn": w[:, :n_h],
        "gdn": jnp.concatenate([w[:, n_h:n_s], jnp.zeros((d, LANES - 2 * HEADS), F32)], axis=1),
        "gates": w[:, n_s:],
    }
    g_mix = norm_mix_g[0].reshape(1, d)
    meta_blk = jnp.concatenate([jnp.zeros((CHUNK - N_META, d), F32), meta_tokens.astype(F32)], axis=0)

    conv_w = gd_conv_w[0].astype(F32)
    no_history = jnp.zeros((SUBLANES, 3 * WIDTH), F32)
    proj, proj_meta = {}, {}
    for k in ("hgrn", "gates"):
        proj[k] = _norm_proj(x2d, g_mix, lb, w_groups[k], PROJ_TM, k)
    proj_meta["hgrn"] = _norm_proj(meta_blk, g_mix, lb, w_groups["hgrn"], CHUNK, "hgrn")
    proj_meta["gdn"], meta_tail = _norm_proj_conv(meta_blk, g_mix, w_groups["gdn"], conv_w, no_history, CHUNK, CHUNK)
    proj["gdn"], _ = _norm_proj_conv(x2d, g_mix, w_groups["gdn"], conv_w, meta_tail, PROJ_TM, seq)

    def scalar_rows(p, nb, nc):
        s = p[:, 4 * WIDTH:4 * WIDTH + 2 * HEADS].reshape(nb, nc, CHUNK, 2, N_STACKS, HEADS_PER_STACK)
        return s.transpose(0, 1, 4, 3, 5, 2).reshape(nb, nc, N_STACKS, 2, STACK)

    s_meta = jnp.broadcast_to(scalar_rows(proj_meta["gdn"], 1, 1), (bsz, 1, N_STACKS, 2, STACK))
    scal = jnp.concatenate([s_meta, scalar_rows(proj["gdn"], bsz, seq // CHUNK)], axis=1)
    a_log_row = jnp.repeat(gd_A_log[0].astype(F32), CHUNK).reshape(N_STACKS, 1, STACK)
    dt_row = jnp.repeat(gd_dt_bias[0].astype(F32), CHUNK).reshape(N_STACKS, 1, STACK)
    o_a = _hgrn(proj_meta["hgrn"], proj["hgrn"].reshape(bsz, seq, -1), hg_norm_g[0].reshape(1, HEAD_DIM))
    o_b = _gdn(proj_meta["gdn"], proj["gdn"].reshape(bsz, seq, -1), scal, a_log_row, dt_row,
               gd_norm_g[0].reshape(1, HEAD_DIM))

    rw = jnp.zeros((d, LANES), F32)
    rw = rw.at[:, :N_GROUPS].set(router_group_w[0]).at[:, N_GROUPS:N_GROUPS + N_EXPERTS].set(router_expert_w[0])
    rw_hi = rw.astype(BF16)
    rw_lo = (rw - rw_hi.astype(F32)).astype(BF16)
    rb = jnp.zeros((1, LANES), F32)
    rb = rb.at[0, :N_GROUPS].set(router_group_b[0]).at[0, N_GROUPS:N_GROUPS + N_EXPERTS].set(router_expert_b[0])
    h2, xn, info, counts = _merge(o_a.reshape(t, WIDTH), o_b.reshape(t, WIDTH), proj["gates"], x2d,
                                  hg_up[0].astype(BF16), gd_up[0].astype(BF16), w_out[0].astype(BF16),
                                  norm_ffn_g[0].reshape(1, d), rw_hi, rw_lo, rb)

    dest, blk_expert, n_used, n_rows = _block_layout(info, counts, t)
    x_sorted = _sc_dispatch(xn, dest, n_rows)
    y_sorted = _moe(blk_expert, n_used, x_sorted,
                    w_gate[0].astype(F32), w_up[0].astype(F32), w_down[0].astype(F32))
    y_tok = _sc_gather(y_sorted, dest)
    out = _combine(h2, info, final_norm_g.reshape(1, d), y_tok)
    return out.reshape(bsz, seq, d)
```

```python
import functools

import numpy as np
import jax
import jax.numpy as jnp
from jax import lax
from jax.experimental import pallas as pl
from jax.experimental.pallas import tpu as pltpu
from jax.experimental.pallas import tpu_sc as plsc

F32 = jnp.float32
BF16 = jnp.bfloat16
U32 = jnp.uint32

D_MODEL = 1024
N_META = 16
CHUNK = 64
EPS = 1e-6
HEADS = 4
HEAD_DIM = 128
WIDTH = HEADS * HEAD_DIM
HEADS_PER_STACK = 2
STACK = HEADS_PER_STACK * CHUNK
N_STACKS = HEADS // HEADS_PER_STACK
CONV_W = 4
N_GROUPS = 4
EXPERTS_PER_GROUP = 8
N_EXPERTS = N_GROUPS * EXPERTS_PER_GROUP
D_FF = 512
LANES = 128
SUBLANES = 8
HALF = D_MODEL // 2
MOE_ROWS = 512
CAST_SLAB = 128
SC_CORES, SC_SUBCORES = 2, 16
SC_WORKERS = SC_CORES * SC_SUBCORES
SC_CHUNK = 128
PROJ_TM = 1024
CHUNKS_PER_STEP = 2
TOKEN_TM = 512
VMEM_LIMIT = 48 * 1024 * 1024


def _bdot(a, b):
    return lax.dot_general(a.astype(BF16), b.astype(BF16), (((2,), (1,)), ((0,), (0,))),
                           preferred_element_type=F32)


def _bdot_nt(a, b):
    return lax.dot_general(a.astype(BF16), b.astype(BF16), (((2,), (2,)), ((0,), (0,))),
                           preferred_element_type=F32)


def _bdot_tn(a, b):
    return lax.dot_general(a.astype(BF16), b.astype(BF16), (((1,), (1,)), ((0,), (0,))),
                           preferred_element_type=F32)


def _split3(x):
    hi = x.astype(BF16)
    r1 = x - hi.astype(F32)
    mid = r1.astype(BF16)
    lo = (r1 - mid.astype(F32)).astype(BF16)
    return hi, mid, lo


def _sigmoid(x):
    return 1.0 / (1.0 + jnp.exp(-x))


def _silu(x):
    return x * _sigmoid(x)


def _softplus(x):
    return jnp.maximum(x, 0.0) + jnp.log(1.0 + jnp.exp(-jnp.abs(x)))


def _stack_heads(x, bsz):
    x3 = x.reshape(bsz, CHUNK, WIDTH)
    y = jnp.concatenate([x3[:, :, h * HEAD_DIM:(h + 1) * HEAD_DIM] for h in range(HEADS)], axis=1)
    return y.reshape(bsz * HEADS * CHUNK, HEAD_DIM)


def _unstack_heads(x, bsz):
    x3 = x.reshape(bsz, HEADS * CHUNK, HEAD_DIM)
    return jnp.concatenate([x3[:, h * CHUNK:(h + 1) * CHUNK, :] for h in range(HEADS)], axis=2)


def _to_stacks(a, bsz):
    return a.reshape(bsz * N_STACKS, STACK, a.shape[-1])


def _to_heads(a, bsz):
    return a.reshape(bsz * HEADS, CHUNK, a.shape[-1])


def _roll_rows(x, shift):
    n = x.shape[0]
    return pltpu.roll(x, shift % n, axis=0)


def _chunks_first(x, nc):
    return jnp.concatenate([x[:, j * CHUNK:(j + 1) * CHUNK] for j in range(nc)], axis=0)


def _chunks_first_scalars(s, nc):
    return jnp.concatenate([s[:, j] for j in range(nc)], axis=0)


def _load_chunks(ref, nc):
    x = _chunks_first(ref[...], nc)
    return x.reshape(x.shape[0] * CHUNK, WIDTH)


def _store_chunks(ref, x, nb, nc):
    ref[...] = jnp.concatenate([x[j * nb:(j + 1) * nb] for j in range(nc)], axis=1).astype(ref.dtype)


def _cast_rows(src_ref, dst_ref):
    slab = CAST_SLAB
    rows = src_ref.shape[0]
    assert rows % slab == 0

    def body(k, carry):
        r = pl.multiple_of(k * slab, slab)
        dst_ref[pl.ds(r, slab), :] = src_ref[pl.ds(r, slab), :].astype(dst_ref.dtype)
        return carry

    lax.fori_loop(0, rows // slab, body, 0)


def _pack_halves(x):
    lo = lax.bitcast_convert_type(x[:, :HALF].astype(BF16).astype(F32), U32)
    hi = lax.bitcast_convert_type(x[:, HALF:].astype(BF16).astype(F32), U32)
    return (lo >> 16) | (hi & jnp.uint32(0xFFFF0000))


def _unpack_halves(p):
    lo = lax.bitcast_convert_type(p << 16, F32)
    hi = lax.bitcast_convert_type(p & jnp.uint32(0xFFFF0000), F32)
    return jnp.concatenate([lo, hi], axis=1)


def _normed_bf16(x_ref, g_ref):
    x = x_ref[...]
    return (x * lax.rsqrt(jnp.mean(x * x, axis=-1, keepdims=True) + EPS) * g_ref[...]).astype(BF16)


def _norm_proj_kernel(x_ref, g_ref, lb_ref, w_ref, o_ref, wbf_ref, *, group):
    @pl.when(pl.program_id(0) == 0)
    def _():
        _cast_rows(w_ref, wbf_ref)

    acc = jnp.dot(_normed_bf16(x_ref, g_ref), wbf_ref[...], preferred_element_type=F32)
    w1, w2, w3 = WIDTH, 2 * WIDTH, 3 * WIDTH
    if group == "hgrn":
        lb = lb_ref[...]
        o_ref[:, :w1] = _silu(acc[:, :w1])
        o_ref[:, w1:w2] = lb + (1.0 - lb) * _sigmoid(acc[:, w1:w2])
        o_ref[:, w2:w3] = acc[:, w2:w3]
        o_ref[:, w3:] = _silu(acc[:, w3:])
    else:
        o_ref[...] = _sigmoid(acc)


def _norm_proj_conv_kernel(x_ref, g_ref, w_ref, cw_ref, hist_ref, o_ref, tail_ref, wbf_ref, carry_ref,
                           *, steps_per_seq):
    i = pl.program_id(0)

    @pl.when(i == 0)
    def _():
        _cast_rows(w_ref, wbf_ref)

    xn = _normed_bf16(x_ref, g_ref)
    w3, w4 = 3 * WIDTH, 4 * WIDTH
    raw = jnp.dot(xn, wbf_ref[:, :w3], preferred_element_type=F32)
    rest = jnp.dot(xn, wbf_ref[:, w3:], preferred_element_type=F32)
    rows = raw.shape[0]
    hist = jnp.where(i % steps_per_seq == 0, hist_ref[...], carry_ref[...])
    cw = cw_ref[...]
    tap = lambda d: cw[CONV_W - 1 - d:CONV_W - d, :]
    body = raw * tap(0)
    head = raw[:SUBLANES]
    r8 = lax.broadcasted_iota(jnp.int32, head.shape, 0)
    first = head * tap(0)
    for d in range(1, CONV_W):
        body = body + _roll_rows(raw, d) * tap(d)
        first = first + jnp.where(r8 >= d, _roll_rows(head, d), _roll_rows(hist, d)) * tap(d)
    o_ref[:SUBLANES, :w3] = _silu(first)
    o_ref[SUBLANES:, :w3] = _silu(body[SUBLANES:])
    o_ref[:, w3:w4] = _silu(rest[:, :WIDTH])
    o_ref[:, w4:] = rest[:, WIDTH:]
    tail = raw[rows - SUBLANES:]
    carry_ref[...] = tail
    tail_ref[...] = tail


def _proj_call(kernel_fn, name, x2d, tm, n, operands, specs, extra_out=(), extra_scratch=()):
    m, k = x2d.shape
    tm = min(tm, m)
    out_specs = [pl.BlockSpec((tm, n), lambda i: (i, 0))] + [s for s, _ in extra_out]
    out_shape = [jax.ShapeDtypeStruct((m, n), F32)] + [s for _, s in extra_out]
    return pl.pallas_call(
        kernel_fn,
        grid=(m // tm,),
        in_specs=[pl.BlockSpec((tm, k), lambda i: (i, 0))] + specs,
        out_specs=out_specs,
        out_shape=out_shape,
        scratch_shapes=[pltpu.VMEM((k, n), BF16)] + list(extra_scratch),
        compiler_params=pltpu.CompilerParams(
            dimension_semantics=("arbitrary",), vmem_limit_bytes=VMEM_LIMIT),
        name=name,
    )(x2d, *operands)


def _weight_spec(k, n):
    return pl.BlockSpec((k, n), lambda i: (0, 0), pipeline_mode=pl.Buffered(1))


def _norm_proj(x2d, g, lb, w, tm, group):
    k, n = w.shape
    const = lambda i: (0, 0)
    specs = [pl.BlockSpec((1, k), const), pl.BlockSpec((1, WIDTH), const), _weight_spec(k, n)]
    return _proj_call(functools.partial(_norm_proj_kernel, group=group), "norm_proj_" + group,
                      x2d, tm, n, (g, lb, w), specs)[0]


def _norm_proj_conv(x2d, g, w, conv_w, hist, tm, rows_per_seq):
    k, n = w.shape
    const = lambda i: (0, 0)
    tm = min(tm, rows_per_seq)
    assert rows_per_seq % tm == 0 and x2d.shape[0] % rows_per_seq == 0
    tail_shape = (SUBLANES, 3 * WIDTH)
    specs = [pl.BlockSpec((1, k), const), _weight_spec(k, n),
             pl.BlockSpec((CONV_W, 3 * WIDTH), const), pl.BlockSpec(tail_shape, const)]
    return _proj_call(functools.partial(_norm_proj_conv_kernel, steps_per_seq=rows_per_seq // tm),
                      "norm_proj_gdn", x2d, tm, n, (g, w, conv_w, hist), specs,
                      extra_out=[(pl.BlockSpec(tail_shape, const), jax.ShapeDtypeStruct(tail_shape, F32))],
                      extra_scratch=[pltpu.VMEM(tail_shape, F32)])


HGRN_LEVELS = 6
HGRN_DIAG = HGRN_LEVELS


def _hgrn_level_table():
    r = np.arange(STACK)
    t, hd = r % CHUNK, r // CHUNK
    x = t[:, None] ^ t[None, :]
    lv = np.floor(np.log2(np.maximum(x, 1))).astype(np.int32)
    valid = (hd[:, None] == hd[None, :]) & (t[:, None] > t[None, :])
    lv = np.where(valid, lv, -1)
    lv = np.where(r[:, None] == r[None, :], HGRN_DIAG, lv)
    return jnp.asarray(lv, dtype=jnp.int32)


def _hgrn_kernel(lv_ref, g_ref, st0_ref, q_ref, f_ref, v_ref, gate_ref, o_ref, stout_ref, st_ref, *, nb, nc):
    @pl.when(pl.program_id(0) == 0)
    def _():
        st_ref[...] = st0_ref[...]

    bsz = nc * nb
    q = _load_chunks(q_ref, nc)
    f = _load_chunks(f_ref, nc)
    v = _load_chunks(v_ref, nc)
    gate = _load_chunks(gate_ref, nc)

    per_seq = lambda a: a.reshape(bsz, CHUNK, WIDTH)
    trow = lax.broadcasted_iota(jnp.int32, (1, CHUNK, WIDTH), 1)
    b = jnp.log(f)
    s = 1
    while s < CHUNK:
        b = (per_seq(b) + jnp.where(trow >= s, per_seq(_roll_rows(b, s)), 0.0)).reshape(b.shape)
        s *= 2

    qs, ks, vs, bs = (_stack_heads(a, bsz) for a in (q, 1.0 - f, v, b))
    lv = lv_ref[...][None]
    to3 = functools.partial(_to_stacks, bsz=bsz)
    to_g = functools.partial(_to_heads, bsz=bsz)
    roll_g = lambda a, shift: to_g(_roll_rows(a.reshape(qs.shape), shift))
    t = lax.broadcasted_iota(jnp.int32, (1, CHUNK, HEAD_DIM), 1)
    qs_g, ks_g, bs_g = to_g(qs), to_g(ks), to_g(bs)

    attn = jnp.where(lv == HGRN_DIAG, _bdot_nt(to3(qs), to3(ks)), 0.0)
    bref = bs_g
    bnext = roll_g(bs_g, -1)
    for li in range(HGRN_LEVELS):
        m = 1 << li
        if li > 0:
            half = m // 2
            upper = (t & (m - 1)) >= half
            bref = jnp.where(upper, roll_g(bref, half), bref)
            bnext = jnp.where(upper, bnext, roll_g(bnext, -half))
        odd = ((t >> li) & 1) == 1
        both = jnp.where(odd, qs_g, ks_g) * jnp.exp(jnp.where(odd, bs_g - bref, bnext - bs_g))
        both = to3(both.reshape(qs.shape)).astype(BF16)
        attn = jnp.where(lv == li, _bdot_nt(both, both), attn)

    o = _bdot(attn, to3(vs))

    n_g = nb * HEADS
    st = st_ref[...]
    qe = qs_g * jnp.exp(bs_g)
    b_end = bs_g[:, CHUNK - 1:CHUNK, :]
    ke = ks_g * jnp.exp(b_end - bs_g)
    vs_g = to_g(vs)
    o_inter = []
    for j in range(nc):
        sl = slice(j * n_g, (j + 1) * n_g)
        o_inter.append(_bdot_nt(qe[sl], st))
        st = st * jnp.exp(b_end[sl]) + _bdot_tn(vs_g[sl], ke[sl])
    st_ref[...] = st
    stout_ref[...] = st
    o = (to_g(o) + jnp.concatenate(o_inter, axis=0)).reshape(qs.shape)

    o = o * lax.rsqrt(jnp.mean(o * o, axis=-1, keepdims=True) + EPS) * g_ref[...]
    o = o * _stack_heads(gate, bsz)
    _store_chunks(o_ref, _unstack_heads(o, bsz), nb, nc)


def _gdn_cumsum_matrix():
    r = np.arange(STACK)
    u = (r[:, None] // CHUNK == r[None, :] // CHUNK) & (r[:, None] <= r[None, :])
    return jnp.asarray(u, dtype=BF16)


def _gdn_kernel(u_ref, alog_ref, dt_ref, g_ref, s_ref, st0_ref, q_ref, k_ref, v_ref, z_ref,
                o_ref, stout_ref, st_ref, *, nb, nc):
    @pl.when(pl.program_id(0) == 0)
    def _():
        st_ref[...] = st0_ref[...]

    bsz = nc * nb
    qs = _stack_heads(_load_chunks(q_ref, nc), bsz)
    ks = _stack_heads(_load_chunks(k_ref, nc), bsz)
    vs = _stack_heads(_load_chunks(v_ref, nc), bsz)
    gate = _load_chunks(z_ref, nc)
    qs = qs * lax.rsqrt(jnp.sum(qs * qs, axis=-1, keepdims=True) + EPS) * (HEAD_DIM ** -0.5)
    ks = ks * lax.rsqrt(jnp.sum(ks * ks, axis=-1, keepdims=True) + EPS)
    to3 = functools.partial(_to_stacks, bsz=bsz)
    qs, ks, vs = to3(qs), to3(ks), to3(vs)
    n_st = bsz * N_STACKS

    srow = _chunks_first_scalars(s_ref[...], nc).reshape(n_st, 2, STACK)
    per_stack = lambda ref: jnp.broadcast_to(ref[...][None], (bsz, N_STACKS, 1, STACK)).reshape(n_st, 1, STACK)
    beta_row = _sigmoid(srow[:, 0:1, :])
    g_row = -jnp.exp(per_stack(alog_ref)) * _softplus(srow[:, 1:2, :] + per_stack(dt_ref))
    n8 = n_st * SUBLANES
    g8 = jnp.broadcast_to(g_row, (n_st, SUBLANES, STACK)).reshape(n8, STACK)
    pieces = jnp.concatenate(_split3(g8), axis=0)
    cum = jnp.dot(pieces, u_ref[...], preferred_element_type=F32)
    cum = cum[0:n8] + cum[n8:2 * n8] + cum[2 * n8:]
    cum_row = cum.reshape(n_st, SUBLANES, STACK)[:, 0:1, :]

    r = lax.broadcasted_iota(jnp.int32, (STACK, STACK), 0)
    cc = lax.broadcasted_iota(jnp.int32, (STACK, STACK), 1)
    eye = (r == cc)[None]
    cum_col = jnp.sum(jnp.where(eye, cum_row, 0.0), axis=2, keepdims=True)
    beta_col = jnp.sum(jnp.where(eye, beta_row, 0.0), axis=2, keepdims=True)
    same = (r >> 6) == (cc >> 6)
    lower = (same & (r >= cc))[None]
    strict = (same & (r > cc))[None]
    ratio = jnp.exp(jnp.minimum(cum_col - cum_row, 0.0))

    kkt = _bdot_nt(ks, ks)
    qkt = _bdot_nt(qs, ks)
    lm = jnp.where(strict, beta_col * ratio * kkt, 0.0)

    ident = eye.astype(F32)
    l0 = jnp.where(((r >> 3) == (cc >> 3))[None], lm, 0.0)
    l2 = _bdot(l0, l0)
    l4 = _bdot(l2, l2)
    xinv = _bdot(_bdot(ident - l0, ident + l2), ident + l4)
    for lvl in (3, 4, 5):
        off = (((r >> (lvl + 1)) == (cc >> (lvl + 1))) & ((r >> lvl) != (cc >> lvl)))[None]
        xinv = xinv - _bdot(_bdot(xinv, jnp.where(off, lm, 0.0)), xinv)

    decay_col = jnp.exp(cum_col)
    rhs = jnp.concatenate([beta_col * vs, (beta_col * decay_col) * ks], axis=2)
    sol = _bdot(xinv, rhs)

    to_g = functools.partial(_to_heads, bsz=bsz)
    n_g = nb * HEADS
    st = st_ref[...]
    u0_g, w_g, q_g = to_g(sol[:, :, :HEAD_DIM]), to_g(sol[:, :, HEAD_DIM:]), to_g(qs)
    cum_g = to_g(cum_col)
    cum_end = cum_g[:, CHUNK - 1:CHUNK, :]
    kdec = to_g(ks) * jnp.exp(cum_end - cum_g)
    decay_g = to_g(decay_col)
    us, o_inter = [], []
    for j in range(nc):
        sl = slice(j * n_g, (j + 1) * n_g)
        u_j = u0_g[sl] - _bdot(w_g[sl], st)
        us.append(u_j)
        o_inter.append(decay_g[sl] * _bdot(q_g[sl], st))
        st = jnp.exp(cum_end[sl]) * st + _bdot_tn(kdec[sl], u_j)
    st_ref[...] = st
    stout_ref[...] = st
    attn = jnp.where(lower, qkt * ratio, 0.0)
    u = jnp.concatenate(us, axis=0).reshape(bsz * HEADS * CHUNK, HEAD_DIM)
    o = jnp.concatenate(o_inter, axis=0) + to_g(_bdot(attn, to3(u)))
    o = o.reshape(bsz * HEADS * CHUNK, HEAD_DIM)

    o = o * lax.rsqrt(jnp.mean(o * o, axis=-1, keepdims=True) + EPS) * g_ref[...]
    o = o * _stack_heads(gate, bsz)
    _store_chunks(o_ref, _unstack_heads(o, bsz), nb, nc)


def _recurrence_call(kernel_fn, name, proj, state0, operands, specs, nc):
    nb, length, _ = proj.shape
    rows = nc * CHUNK
    state_shape = (nb * HEADS, HEAD_DIM, HEAD_DIM)
    state_spec = pl.BlockSpec(state_shape, lambda c: (0, 0, 0))
    chunk_specs = [pl.BlockSpec((nb, rows, WIDTH), functools.partial(lambda c, j: (0, c, j), j=j))
                   for j in range(4)]
    return pl.pallas_call(
        functools.partial(kernel_fn, nb=nb, nc=nc),
        grid=(length // rows,),
        in_specs=specs + [state_spec] + chunk_specs,
        out_specs=[pl.BlockSpec((nb, rows, WIDTH), lambda c: (0, c, 0)), state_spec],
        out_shape=[jax.ShapeDtypeStruct((nb, length, WIDTH), BF16),
                   jax.ShapeDtypeStruct(state_shape, F32)],
        scratch_shapes=[pltpu.VMEM(state_shape, F32)],
        compiler_params=pltpu.CompilerParams(
            dimension_semantics=("arbitrary",), vmem_limit_bytes=VMEM_LIMIT),
        name=name,
    )(*operands, state0, *([proj] * 4))


def _hgrn(proj, state0, norm_g, nc):
    const = lambda c: (0, 0)
    specs = [pl.BlockSpec((STACK, STACK), const), pl.BlockSpec((1, HEAD_DIM), const)]
    return _recurrence_call(_hgrn_kernel, "hgrn", proj, state0, (_hgrn_level_table(), norm_g), specs, nc)


def _gdn(proj, state0, scal, a_log_row, dt_row, norm_g, nc):
    nb = proj.shape[0]
    const = lambda c: (0, 0)
    const3 = lambda c: (0, 0, 0)
    specs = [pl.BlockSpec((STACK, STACK), const),
             pl.BlockSpec((N_STACKS, 1, STACK), const3),
             pl.BlockSpec((N_STACKS, 1, STACK), const3),
             pl.BlockSpec((1, HEAD_DIM), const),
             pl.BlockSpec((nb, nc, N_STACKS, 2, STACK), lambda c: (0, c, 0, 0, 0))]
    return _recurrence_call(_gdn_kernel, "gdn", proj, state0,
                            (_gdn_cumsum_matrix(), a_log_row, dt_row, norm_g, scal), specs, nc)


ROUTE_LANE0 = N_GROUPS
ROUTE_NEG = -1e30


def _route_tile(lg, ltri, carry):
    lane = lax.broadcasted_iota(jnp.int32, lg.shape, 1).astype(F32)
    first = lambda mask: jnp.min(jnp.where(mask, lane, float(LANES)), axis=1, keepdims=True)
    top = lambda mask: jnp.max(jnp.where(mask, lg, ROUTE_NEG), axis=1, keepdims=True)

    is_g = lane < N_GROUPS
    gmax = top(is_g)
    grp = first(is_g & (lg == gmax))
    p_grp = 1.0 / jnp.sum(jnp.where(is_g, jnp.exp(lg - gmax), 0.0), axis=1, keepdims=True)

    lo = ROUTE_LANE0 + grp * EXPERTS_PER_GROUP
    in_grp = (lane >= lo) & (lane < lo + EXPERTS_PER_GROUP)
    m1 = top(in_grp)
    i1 = first(in_grp & (lg == m1))
    rest = in_grp & (lane != i1)
    m2 = top(rest)
    i2 = first(rest & (lg == m2))
    r = jnp.exp(m2 - m1)
    w0 = p_grp / (1.0 + r)
    w1 = p_grp * r / (1.0 + r)

    hot0 = (lane == i1).astype(F32)
    hot1 = (lane == i2).astype(F32)
    both = hot0 + hot1
    before = jnp.dot(ltri, both.astype(BF16), preferred_element_type=F32) + carry
    rank0 = jnp.sum(hot0 * before, axis=1, keepdims=True)
    rank1 = jnp.sum(hot1 * before, axis=1, keepdims=True)
    cols = (w0, w1, i1 - ROUTE_LANE0, i2 - ROUTE_LANE0, rank0, rank1)
    info = jnp.zeros_like(lg)
    for k, col in enumerate(cols):
        info = jnp.where(lane == k, col, info)
    return info, carry + jnp.sum(both, axis=0, keepdims=True)


def _merge_kernel(oa_ref, ob_ref, ga_ref, gb_ref, x_ref, hgup_ref, gdup_ref, wout_ref,
                  ng_ref, rwh_ref, rwl_ref, rb_ref, ltri_ref, h2_ref, xn_ref, info_ref, cnt_ref, carry_ref):
    @pl.when(pl.program_id(0) == 0)
    def _():
        carry_ref[...] = jnp.zeros_like(carry_ref)

    ua = jnp.dot(oa_ref[...], hgup_ref[...], preferred_element_type=F32)
    ub = jnp.dot(ob_ref[...], gdup_ref[...], preferred_element_type=F32)
    merged = ga_ref[...] * ua + gb_ref[...] * ub
    h2 = x_ref[...] + jnp.dot(merged.astype(BF16), wout_ref[...], preferred_element_type=F32)
    h2_ref[...] = h2
    xn = h2 * lax.rsqrt(jnp.mean(h2 * h2, axis=-1, keepdims=True) + EPS) * ng_ref[...]
    xn_ref[...] = _pack_halves(xn)
    xh = xn.astype(BF16)
    xl = (xn - xh.astype(F32)).astype(BF16)
    wh, wl = rwh_ref[...], rwl_ref[...]
    lg = (jnp.dot(xh, wh, preferred_element_type=F32)
          + jnp.dot(xh, wl, preferred_element_type=F32)
          + jnp.dot(xl, wh, preferred_element_type=F32)) + rb_ref[...]
    info, carry = _route_tile(lg, ltri_ref[...], carry_ref[...])
    info_ref[...] = info
    carry_ref[...] = carry
    cnt_ref[...] = carry


def _merge(o_a, o_b, proj, x2d, hg_up, gd_up, w_out, norm_g, rw_hi, rw_lo, rb):
    t = x2d.shape[0]
    tm = TOKEN_TM
    row = lambda i: (i, 0)
    const = lambda i: (0, 0)
    ltri = jnp.asarray(np.tril(np.ones((tm, tm), np.float32), -1), dtype=BF16)
    return pl.pallas_call(
        _merge_kernel,
        grid=(t // tm,),
        in_specs=[pl.BlockSpec((tm, WIDTH), row),
                  pl.BlockSpec((tm, WIDTH), row),
                  pl.BlockSpec((tm, D_MODEL), lambda i: (i, 0)),
                  pl.BlockSpec((tm, D_MODEL), lambda i: (i, 1)),
                  pl.BlockSpec((tm, D_MODEL), row),
                  pl.BlockSpec((WIDTH, D_MODEL), const),
                  pl.BlockSpec((WIDTH, D_MODEL), const),
                  pl.BlockSpec((D_MODEL, D_MODEL), const),
                  pl.BlockSpec((1, D_MODEL), const),
                  pl.BlockSpec((D_MODEL, LANES), const),
                  pl.BlockSpec((D_MODEL, LANES), const),
                  pl.BlockSpec((1, LANES), const),
                  pl.BlockSpec((tm, tm), const)],
        out_specs=[pl.BlockSpec((tm, D_MODEL), row),
                   pl.BlockSpec((tm, HALF), row),
                   pl.BlockSpec((tm, LANES), row),
                   pl.BlockSpec((1, LANES), const)],
        out_shape=[jax.ShapeDtypeStruct((t, D_MODEL), F32),
                   jax.ShapeDtypeStruct((t, HALF), U32),
                   jax.ShapeDtypeStruct((t, LANES), F32),
                   jax.ShapeDtypeStruct((1, LANES), F32)],
        scratch_shapes=[pltpu.VMEM((1, LANES), F32)],
        compiler_params=pltpu.CompilerParams(
            dimension_semantics=("arbitrary",), vmem_limit_bytes=VMEM_LIMIT),
        name="merge",
    )(o_a, o_b, proj, proj, x2d, hg_up, gd_up, w_out, norm_g, rw_hi, rw_lo, rb, ltri)


def _sc_gather(table, idx):
    n_idx = idx.shape[0]
    cols = table.shape[1]
    per_worker = n_idx // SC_WORKERS
    assert n_idx % (SC_WORKERS * SC_CHUNK) == 0
    mesh = plsc.VectorSubcoreMesh(core_axis_name="c", subcore_axis_name="s")

    @functools.partial(
        pl.kernel, mesh=mesh,
        out_type=jax.ShapeDtypeStruct((n_idx, cols), table.dtype),
        scratch_types=[pltpu.VMEM((SC_CHUNK,), jnp.int32),
                       pltpu.VMEM((SC_CHUNK, cols), table.dtype),
                       pltpu.SemaphoreType.DMA],
    )
    def gather(table_hbm, idx_hbm, out_hbm, idx_v, rows_v, sem):
        worker = lax.axis_index("s") * SC_CORES + lax.axis_index("c")
        base = worker * per_worker

        @pl.loop(0, per_worker // SC_CHUNK)
        def _(c):
            off = pl.multiple_of(base + c * SC_CHUNK, SC_CHUNK)
            pltpu.sync_copy(idx_hbm.at[pl.ds(off, SC_CHUNK)], idx_v)
            pltpu.async_copy(table_hbm.at[idx_v], rows_v, sem).wait()
            pltpu.sync_copy(rows_v, out_hbm.at[pl.ds(off, SC_CHUNK)])

    return gather(table, idx)


def _sc_dispatch(rows, dest, n_out):
    t, cols = rows.shape
    per_worker = t // SC_WORKERS
    assert t % (SC_WORKERS * SC_CHUNK) == 0 and dest.shape[0] == 2 * t
    mesh = plsc.VectorSubcoreMesh(core_axis_name="c", subcore_axis_name="s")

    @functools.partial(
        pl.kernel, mesh=mesh,
        out_type=jax.ShapeDtypeStruct((n_out, cols), rows.dtype),
        scratch_types=[pltpu.VMEM((SC_CHUNK,), jnp.int32),
                       pltpu.VMEM((SC_CHUNK, cols), rows.dtype)],
    )
    def dispatch(rows_hbm, dest_hbm, out_hbm, idx_v, rows_v):
        worker = lax.axis_index("s") * SC_CORES + lax.axis_index("c")
        base = worker * per_worker

        @pl.loop(0, per_worker // SC_CHUNK)
        def _(c):
            off = pl.multiple_of(base + c * SC_CHUNK, SC_CHUNK)
            pltpu.sync_copy(rows_hbm.at[pl.ds(off, SC_CHUNK)], rows_v)
            for slot in range(2):
                pltpu.sync_copy(dest_hbm.at[pl.ds(slot * t + off, SC_CHUNK)], idx_v)
                pltpu.sync_copy(rows_v, out_hbm.at[idx_v])

    return dispatch(rows, dest)


def _moe_kernel(be_ref, nu_ref, x_ref, wg_ref, wu_ref, wd_ref, y_ref, wg_bf, wu_bf, wd_bf):
    i = pl.program_id(0)

    @pl.when((i == 0) | (be_ref[i] != be_ref[jnp.maximum(i - 1, 0)]))
    def _():
        _cast_rows(wg_ref, wg_bf)
        _cast_rows(wu_ref, wu_bf)
        _cast_rows(wd_ref, wd_bf)

    @pl.when(i < nu_ref[0])
    def _():
        xb = _unpack_halves(x_ref[...]).astype(BF16)
        a = jnp.dot(xb, wg_bf[...], preferred_element_type=F32)
        u = jnp.dot(xb, wu_bf[...], preferred_element_type=F32)
        y = jnp.dot((_silu(a) * u).astype(BF16), wd_bf[...], preferred_element_type=F32)
        y_ref[...] = _pack_halves(y)

    @pl.when(i >= nu_ref[0])
    def _():
        y_ref[...] = jnp.zeros_like(y_ref)


def _moe(blk_expert, n_used, x_sorted, w_gate, w_up, w_down):
    n_blocks = blk_expert.shape[0]
    wspec = lambda shape: pl.BlockSpec((None,) + shape, lambda i, be, nu: (be[i], 0, 0))
    rows = pl.BlockSpec((MOE_ROWS, HALF), lambda i, be, nu: (i, 0))
    grid_spec = pltpu.PrefetchScalarGridSpec(
        num_scalar_prefetch=2,
        grid=(n_blocks,),
        in_specs=[rows, wspec((D_MODEL, D_FF)), wspec((D_MODEL, D_FF)), wspec((D_FF, D_MODEL))],
        out_specs=rows,
        scratch_shapes=[pltpu.VMEM((D_MODEL, D_FF), BF16), pltpu.VMEM((D_MODEL, D_FF), BF16),
                        pltpu.VMEM((D_FF, D_MODEL), BF16)],
    )
    return pl.pallas_call(
        _moe_kernel,
        grid_spec=grid_spec,
        out_shape=jax.ShapeDtypeStruct(x_sorted.shape, U32),
        compiler_params=pltpu.CompilerParams(
            dimension_semantics=("arbitrary",), vmem_limit_bytes=VMEM_LIMIT),
        name="moe",
    )(blk_expert, n_used, x_sorted, w_gate, w_up, w_down)


def _combine_kernel(h2_ref, rw_ref, g_ref, y0_ref, y1_ref, o_ref):
    rw = rw_ref[...]
    h = h2_ref[...] + rw[:, 0:1] * _unpack_halves(y0_ref[...]) + rw[:, 1:2] * _unpack_halves(y1_ref[...])
    o_ref[...] = h * lax.rsqrt(jnp.mean(h * h, axis=-1, keepdims=True) + EPS) * g_ref[...]


def _combine(h2, rweights, final_g, y):
    t = h2.shape[0]
    tm = TOKEN_TM
    row = lambda i: (i, 0)
    return pl.pallas_call(
        _combine_kernel,
        grid=(t // tm,),
        in_specs=[pl.BlockSpec((tm, D_MODEL), row),
                  pl.BlockSpec((tm, LANES), row),
                  pl.BlockSpec((1, D_MODEL), lambda i: (0, 0)),
                  pl.BlockSpec((tm, HALF), row),
                  pl.BlockSpec((tm, HALF), lambda i: (i + t // tm, 0))],
        out_specs=pl.BlockSpec((tm, D_MODEL), row),
        out_shape=jax.ShapeDtypeStruct((t, D_MODEL), F32),
        compiler_params=pltpu.CompilerParams(
            dimension_semantics=("arbitrary",), vmem_limit_bytes=VMEM_LIMIT),
        name="combine",
    )(h2, rweights, final_g, y, y)


def _block_layout(info, counts_row, t):
    n_blocks = 2 * t // MOE_ROWS + N_EXPERTS
    counts = counts_row[0, ROUTE_LANE0:ROUTE_LANE0 + N_EXPERTS].astype(jnp.int32)
    padded = ((counts + MOE_ROWS - 1) // MOE_ROWS) * MOE_ROWS
    pend = jnp.cumsum(padded)
    pstart = pend - padded
    blk_first = jnp.arange(n_blocks, dtype=jnp.int32) * MOE_ROWS
    blk_expert = jnp.minimum(jnp.sum((pend[None, :] <= blk_first[:, None]).astype(jnp.int32), axis=1),
                             N_EXPERTS - 1).astype(jnp.int32)
    n_used = (pend[-1] // MOE_ROWS).astype(jnp.int32).reshape(1)
    expert = info[:, 2:4].astype(jnp.int32)
    rank = info[:, 4:6].astype(jnp.int32)
    experts = jnp.arange(N_EXPERTS, dtype=jnp.int32)
    first_row = jnp.sum(jnp.where(expert[:, :, None] == experts, pstart, 0), axis=-1)
    dest = (first_row + rank).T.reshape(2 * t)
    return dest, blk_expert, n_used, n_blocks * MOE_ROWS


def kernel(x, meta_tokens, hg_lb_logits, norm_mix_g, w_in, gd_conv_w, gd_A_log, gd_dt_bias, hg_norm_g, gd_norm_g, hg_up, gd_up, w_out, norm_ffn_g, router_group_w, router_group_b, router_expert_w, router_expert_b, w_gate, w_up, w_down, final_norm_g):
    bsz, seq, d = x.shape
    t = bsz * seq
    x2d = x.reshape(t, d)

    lb = jnp.cumsum(jax.nn.softmax(hg_lb_logits.astype(F32), axis=0), axis=0)[0].reshape(1, WIDTH)
    w = w_in[0].astype(F32)
    n_h, n_g = 4 * WIDTH, 8 * WIDTH
    n_s = n_g + 2 * HEADS
    w_groups = {
        "hgrn": w[:, :n_h],
        "gdn": jnp.concatenate([w[:, n_h:n_s], jnp.zeros((d, LANES - 2 * HEADS), F32)], axis=1),
        "gates": w[:, n_s:],
    }
    g_mix = norm_mix_g[0].reshape(1, d)
    meta_blk = jnp.concatenate([jnp.zeros((CHUNK - N_META, d), F32), meta_tokens.astype(F32)], axis=0)

    conv_w = gd_conv_w[0].astype(F32)
    no_history = jnp.zeros((SUBLANES, 3 * WIDTH), F32)
    proj, proj_meta = {}, {}
    for k in ("hgrn", "gates"):
        proj[k] = _norm_proj(x2d, g_mix, lb, w_groups[k], PROJ_TM, k)
    proj_meta["hgrn"] = _norm_proj(meta_blk, g_mix, lb, w_groups["hgrn"], CHUNK, "hgrn")
    proj_meta["gdn"], meta_tail = _norm_proj_conv(meta_blk, g_mix, w_groups["gdn"], conv_w, no_history, CHUNK, CHUNK)
    proj["gdn"], _ = _norm_proj_conv(x2d, g_mix, w_groups["gdn"], conv_w, meta_tail, PROJ_TM, seq)

    def scalar_rows(p, nb, nc):
        s = p[:, 4 * WIDTH:4 * WIDTH + 2 * HEADS].reshape(nb, nc, CHUNK, 2, N_STACKS, HEADS_PER_STACK)
        return s.transpose(0, 1, 4, 3, 5, 2).reshape(nb, nc, N_STACKS, 2, STACK)

    a_log_row = jnp.repeat(gd_A_log[0].astype(F32), CHUNK).reshape(N_STACKS, 1, STACK)
    dt_row = jnp.repeat(gd_dt_bias[0].astype(F32), CHUNK).reshape(N_STACKS, 1, STACK)
    hg_g, gd_g = hg_norm_g[0].reshape(1, HEAD_DIM), gd_norm_g[0].reshape(1, HEAD_DIM)

    zero_state = jnp.zeros((HEADS, HEAD_DIM, HEAD_DIM), F32)
    per_seq = lambda st: jnp.tile(st, (bsz, 1, 1))
    _, hg_state = _hgrn(proj_meta["hgrn"][None], zero_state, hg_g, 1)
    o_a, _ = _hgrn(proj["hgrn"].reshape(bsz, seq, -1), per_seq(hg_state), hg_g, CHUNKS_PER_STEP)
    _, gd_state = _gdn(proj_meta["gdn"][None], zero_state, scalar_rows(proj_meta["gdn"], 1, 1),
                       a_log_row, dt_row, gd_g, 1)
    o_b, _ = _gdn(proj["gdn"].reshape(bsz, seq, -1), per_seq(gd_state),
                  scalar_rows(proj["gdn"], bsz, seq // CHUNK), a_log_row, dt_row, gd_g, CHUNKS_PER_STEP)

    rw = jnp.zeros((d, LANES), F32)
    rw = rw.at[:, :N_GROUPS].set(router_group_w[0]).at[:, N_GROUPS:N_GROUPS + N_EXPERTS].set(router_expert_w[0])
    rw_hi = rw.astype(BF16)
    rw_lo = (rw - rw_hi.astype(F32)).astype(BF16)
    rb = jnp.zeros((1, LANES), F32)
    rb = rb.at[0, :N_GROUPS].set(router_group_b[0]).at[0, N_GROUPS:N_GROUPS + N_EXPERTS].set(router_expert_b[0])
    h2, xn, info, counts = _merge(o_a.reshape(t, WIDTH), o_b.reshape(t, WIDTH), proj["gates"], x2d,
                                  hg_up[0].astype(BF16), gd_up[0].astype(BF16), w_out[0].astype(BF16),
                                  norm_ffn_g[0].reshape(1, d), rw_hi, rw_lo, rb)

    dest, blk_expert, n_used, n_rows = _block_layout(info, counts, t)
    x_sorted = _sc_dispatch(xn, dest, n_rows)
    y_sorted = _moe(blk_expert, n_used, x_sorted,
                    w_gate[0].astype(F32), w_up[0].astype(F32), w_down[0].astype(F32))
    y_tok = _sc_gather(y_sorted, dest)
    out = _combine(h2, info, final_norm_g.reshape(1, d), y_tok)
    return out.reshape(bsz, seq, d)
```

```python
import functools

import numpy as np
import jax
import jax.numpy as jnp
from jax import lax
from jax.experimental import pallas as pl
from jax.experimental.pallas import tpu as pltpu
from jax.experimental.pallas import tpu_sc as plsc

F32 = jnp.float32
BF16 = jnp.bfloat16
U32 = jnp.uint32

D_MODEL = 1024
N_META = 16
CHUNK = 64
EPS = 1e-6
HEADS = 4
HEAD_DIM = 128
WIDTH = HEADS * HEAD_DIM
HEADS_PER_STACK = 2
STACK = HEADS_PER_STACK * CHUNK
N_STACKS = HEADS // HEADS_PER_STACK
CONV_W = 4
N_GROUPS = 4
EXPERTS_PER_GROUP = 8
N_EXPERTS = N_GROUPS * EXPERTS_PER_GROUP
D_FF = 512
LANES = 128
SUBLANES = 8
HALF = D_MODEL // 2
MOE_ROWS = 512
CAST_SLAB = 128
SC_CORES, SC_SUBCORES = 2, 16
SC_WORKERS = SC_CORES * SC_SUBCORES
SC_CHUNK = 128
PROJ_TM = 1024
CHUNKS_PER_STEP = 2
TOKEN_TM = 512
VMEM_LIMIT = 48 * 1024 * 1024


def _bdot(a, b):
    return lax.dot_general(a.astype(BF16), b.astype(BF16), (((2,), (1,)), ((0,), (0,))),
                           preferred_element_type=F32)


def _bdot_nt(a, b):
    return lax.dot_general(a.astype(BF16), b.astype(BF16), (((2,), (2,)), ((0,), (0,))),
                           preferred_element_type=F32)


def _bdot_tn(a, b):
    return lax.dot_general(a.astype(BF16), b.astype(BF16), (((1,), (1,)), ((0,), (0,))),
                           preferred_element_type=F32)


def _split3(x):
    hi = x.astype(BF16)
    r1 = x - hi.astype(F32)
    mid = r1.astype(BF16)
    lo = (r1 - mid.astype(F32)).astype(BF16)
    return hi, mid, lo


def _sigmoid(x):
    return 1.0 / (1.0 + jnp.exp(-x))


def _silu(x):
    return x * _sigmoid(x)


def _softplus(x):
    return jnp.maximum(x, 0.0) + jnp.log(1.0 + jnp.exp(-jnp.abs(x)))


def _stack_heads(x, bsz):
    x3 = x.reshape(bsz, CHUNK, WIDTH)
    y = jnp.concatenate([x3[:, :, h * HEAD_DIM:(h + 1) * HEAD_DIM] for h in range(HEADS)], axis=1)
    return y.reshape(bsz * HEADS * CHUNK, HEAD_DIM)


def _unstack_heads(x, bsz):
    x3 = x.reshape(bsz, HEADS * CHUNK, HEAD_DIM)
    return jnp.concatenate([x3[:, h * CHUNK:(h + 1) * CHUNK, :] for h in range(HEADS)], axis=2)


def _to_stacks(a, bsz):
    return a.reshape(bsz * N_STACKS, STACK, a.shape[-1])


def _to_heads(a, bsz):
    return a.reshape(bsz * HEADS, CHUNK, a.shape[-1])


def _roll_rows(x, shift):
    n = x.shape[0]
    return pltpu.roll(x, shift % n, axis=0)


def _chunks_first(x, nc):
    return jnp.concatenate([x[:, j * CHUNK:(j + 1) * CHUNK] for j in range(nc)], axis=0)


def _chunks_first_scalars(s, nc):
    return jnp.concatenate([s[:, j] for j in range(nc)], axis=0)


def _load_chunks(ref, nc):
    x = _chunks_first(ref[...], nc)
    return x.reshape(x.shape[0] * CHUNK, WIDTH)


def _store_chunks(ref, x, nb, nc):
    ref[...] = jnp.concatenate([x[j * nb:(j + 1) * nb] for j in range(nc)], axis=1).astype(ref.dtype)


def _cast_rows(src_ref, dst_ref):
    slab = CAST_SLAB
    rows = src_ref.shape[0]
    assert rows % slab == 0

    def body(k, carry):
        r = pl.multiple_of(k * slab, slab)
        dst_ref[pl.ds(r, slab), :] = src_ref[pl.ds(r, slab), :].astype(dst_ref.dtype)
        return carry

    lax.fori_loop(0, rows // slab, body, 0)


def _pack_halves(x):
    lo = lax.bitcast_convert_type(x[:, :HALF].astype(BF16).astype(F32), U32)
    hi = lax.bitcast_convert_type(x[:, HALF:].astype(BF16).astype(F32), U32)
    return (lo >> 16) | (hi & jnp.uint32(0xFFFF0000))


def _unpack_halves(p):
    lo = lax.bitcast_convert_type(p << 16, F32)
    hi = lax.bitcast_convert_type(p & jnp.uint32(0xFFFF0000), F32)
    return jnp.concatenate([lo, hi], axis=1)


def _normed_bf16(x_ref, g_ref):
    x = x_ref[...]
    return (x * lax.rsqrt(jnp.mean(x * x, axis=-1, keepdims=True) + EPS) * g_ref[...]).astype(BF16)


def _norm_proj_hgrn_kernel(x_ref, g_ref, lb_ref, w_ref, o_ref, wbf_ref):
    @pl.when(pl.program_id(0) == 0)
    def _():
        _cast_rows(w_ref, wbf_ref)

    acc = jnp.dot(_normed_bf16(x_ref, g_ref), wbf_ref[...], preferred_element_type=F32)
    w1, w2, w3 = WIDTH, 2 * WIDTH, 3 * WIDTH
    lb = lb_ref[...]
    o_ref[:, :w1] = _silu(acc[:, :w1])
    o_ref[:, w1:w2] = lb + (1.0 - lb) * _sigmoid(acc[:, w1:w2])
    o_ref[:, w2:w3] = acc[:, w2:w3]
    o_ref[:, w3:] = _silu(acc[:, w3:])


def _norm_proj_conv_kernel(x_ref, g_ref, w_ref, cw_ref, hist_ref, o_ref, tail_ref, wbf_ref, carry_ref,
                           *, steps_per_seq):
    i = pl.program_id(0)

    @pl.when(i == 0)
    def _():
        _cast_rows(w_ref, wbf_ref)

    xn = _normed_bf16(x_ref, g_ref)
    w3, w4 = 3 * WIDTH, 4 * WIDTH
    raw = jnp.dot(xn, wbf_ref[:, :w3], preferred_element_type=F32)
    rest = jnp.dot(xn, wbf_ref[:, w3:], preferred_element_type=F32)
    rows = raw.shape[0]
    hist = jnp.where(i % steps_per_seq == 0, hist_ref[...], carry_ref[...])
    cw = cw_ref[...]
    tap = lambda d: cw[CONV_W - 1 - d:CONV_W - d, :]
    body = raw * tap(0)
    head = raw[:SUBLANES]
    r8 = lax.broadcasted_iota(jnp.int32, head.shape, 0)
    first = head * tap(0)
    for d in range(1, CONV_W):
        body = body + _roll_rows(raw, d) * tap(d)
        first = first + jnp.where(r8 >= d, _roll_rows(head, d), _roll_rows(hist, d)) * tap(d)
    o_ref[:SUBLANES, :w3] = _silu(first)
    o_ref[SUBLANES:, :w3] = _silu(body[SUBLANES:])
    o_ref[:, w3:w4] = _silu(rest[:, :WIDTH])
    o_ref[:, w4:] = rest[:, WIDTH:]
    tail = raw[rows - SUBLANES:]
    carry_ref[...] = tail
    tail_ref[...] = tail


def _proj_call(kernel_fn, name, x2d, tm, n, operands, specs, extra_out=(), extra_scratch=()):
    m, k = x2d.shape
    tm = min(tm, m)
    out_specs = [pl.BlockSpec((tm, n), lambda i: (i, 0))] + [s for s, _ in extra_out]
    out_shape = [jax.ShapeDtypeStruct((m, n), F32)] + [s for _, s in extra_out]
    return pl.pallas_call(
        kernel_fn,
        grid=(m // tm,),
        in_specs=[pl.BlockSpec((tm, k), lambda i: (i, 0))] + specs,
        out_specs=out_specs,
        out_shape=out_shape,
        scratch_shapes=[pltpu.VMEM((k, n), BF16)] + list(extra_scratch),
        compiler_params=pltpu.CompilerParams(
            dimension_semantics=("arbitrary",), vmem_limit_bytes=VMEM_LIMIT),
        name=name,
    )(x2d, *operands)


def _weight_spec(k, n):
    return pl.BlockSpec((k, n), lambda i: (0, 0), pipeline_mode=pl.Buffered(1))


def _norm_proj_hgrn(x2d, g, lb, w, tm):
    k, n = w.shape
    const = lambda i: (0, 0)
    specs = [pl.BlockSpec((1, k), const), pl.BlockSpec((1, WIDTH), const), _weight_spec(k, n)]
    return _proj_call(_norm_proj_hgrn_kernel, "norm_proj_hgrn", x2d, tm, n, (g, lb, w), specs)[0]


def _norm_proj_conv(x2d, g, w, conv_w, hist, tm, rows_per_seq):
    k, n = w.shape
    const = lambda i: (0, 0)
    tm = min(tm, rows_per_seq)
    assert rows_per_seq % tm == 0 and x2d.shape[0] % rows_per_seq == 0
    tail_shape = (SUBLANES, 3 * WIDTH)
    specs = [pl.BlockSpec((1, k), const), _weight_spec(k, n),
             pl.BlockSpec((CONV_W, 3 * WIDTH), const), pl.BlockSpec(tail_shape, const)]
    return _proj_call(functools.partial(_norm_proj_conv_kernel, steps_per_seq=rows_per_seq // tm),
                      "norm_proj_gdn", x2d, tm, n, (g, w, conv_w, hist), specs,
                      extra_out=[(pl.BlockSpec(tail_shape, const), jax.ShapeDtypeStruct(tail_shape, F32))],
                      extra_scratch=[pltpu.VMEM(tail_shape, F32)])


HGRN_LEVELS = 6
HGRN_DIAG = HGRN_LEVELS


def _hgrn_level_table():
    r = np.arange(STACK)
    t, hd = r % CHUNK, r // CHUNK
    x = t[:, None] ^ t[None, :]
    lv = np.floor(np.log2(np.maximum(x, 1))).astype(np.int32)
    valid = (hd[:, None] == hd[None, :]) & (t[:, None] > t[None, :])
    lv = np.where(valid, lv, -1)
    lv = np.where(r[:, None] == r[None, :], HGRN_DIAG, lv)
    return jnp.asarray(lv, dtype=jnp.int32)


def _hgrn_kernel(lv_ref, g_ref, st0_ref, q_ref, f_ref, v_ref, gate_ref, o_ref, stout_ref, st_ref, *, nb, nc):
    @pl.when(pl.program_id(0) == 0)
    def _():
        st_ref[...] = st0_ref[...]

    bsz = nc * nb
    q = _load_chunks(q_ref, nc)
    f = _load_chunks(f_ref, nc)
    v = _load_chunks(v_ref, nc)
    gate = _load_chunks(gate_ref, nc)

    per_seq = lambda a: a.reshape(bsz, CHUNK, WIDTH)
    trow = lax.broadcasted_iota(jnp.int32, (1, CHUNK, WIDTH), 1)
    b = jnp.log(f)
    s = 1
    while s < CHUNK:
        b = (per_seq(b) + jnp.where(trow >= s, per_seq(_roll_rows(b, s)), 0.0)).reshape(b.shape)
        s *= 2

    qs, ks, vs, bs = (_stack_heads(a, bsz) for a in (q, 1.0 - f, v, b))
    lv = lv_ref[...][None]
    to3 = functools.partial(_to_stacks, bsz=bsz)
    to_g = functools.partial(_to_heads, bsz=bsz)
    roll_g = lambda a, shift: to_g(_roll_rows(a.reshape(qs.shape), shift))
    t = lax.broadcasted_iota(jnp.int32, (1, CHUNK, HEAD_DIM), 1)
    qs_g, ks_g, bs_g = to_g(qs), to_g(ks), to_g(bs)

    attn = jnp.where(lv == HGRN_DIAG, _bdot_nt(to3(qs), to3(ks)), 0.0)
    bref = bs_g
    bnext = roll_g(bs_g, -1)
    for li in range(HGRN_LEVELS):
        m = 1 << li
        if li > 0:
            half = m // 2
            upper = (t & (m - 1)) >= half
            bref = jnp.where(upper, roll_g(bref, half), bref)
            bnext = jnp.where(upper, bnext, roll_g(bnext, -half))
        odd = ((t >> li) & 1) == 1
        both = jnp.where(odd, qs_g, ks_g) * jnp.exp(jnp.where(odd, bs_g - bref, bnext - bs_g))
        both = to3(both.reshape(qs.shape)).astype(BF16)
        attn = jnp.where(lv == li, _bdot_nt(both, both), attn)

    o = _bdot(attn, to3(vs))

    n_g = nb * HEADS
    st = st_ref[...]
    qe = qs_g * jnp.exp(bs_g)
    b_end = bs_g[:, CHUNK - 1:CHUNK, :]
    ke = ks_g * jnp.exp(b_end - bs_g)
    vs_g = to_g(vs)
    o_inter = []
    for j in range(nc):
        sl = slice(j * n_g, (j + 1) * n_g)
        o_inter.append(_bdot_nt(qe[sl], st))
        st = st * jnp.exp(b_end[sl]) + _bdot_tn(vs_g[sl], ke[sl])
    st_ref[...] = st
    stout_ref[...] = st
    o = (to_g(o) + jnp.concatenate(o_inter, axis=0)).reshape(qs.shape)

    o = o * lax.rsqrt(jnp.mean(o * o, axis=-1, keepdims=True) + EPS) * g_ref[...]
    o = o * _stack_heads(gate, bsz)
    _store_chunks(o_ref, _unstack_heads(o, bsz), nb, nc)


def _gdn_cumsum_matrix():
    r = np.arange(STACK)
    u = (r[:, None] // CHUNK == r[None, :] // CHUNK) & (r[:, None] <= r[None, :])
    return jnp.asarray(u, dtype=BF16)


def _gdn_kernel(u_ref, alog_ref, dt_ref, g_ref, s_ref, st0_ref, q_ref, k_ref, v_ref, z_ref,
                o_ref, stout_ref, st_ref, *, nb, nc):
    @pl.when(pl.program_id(0) == 0)
    def _():
        st_ref[...] = st0_ref[...]

    bsz = nc * nb
    qs = _stack_heads(_load_chunks(q_ref, nc), bsz)
    ks = _stack_heads(_load_chunks(k_ref, nc), bsz)
    vs = _stack_heads(_load_chunks(v_ref, nc), bsz)
    gate = _load_chunks(z_ref, nc)
    qs = qs * lax.rsqrt(jnp.sum(qs * qs, axis=-1, keepdims=True) + EPS) * (HEAD_DIM ** -0.5)
    ks = ks * lax.rsqrt(jnp.sum(ks * ks, axis=-1, keepdims=True) + EPS)
    to3 = functools.partial(_to_stacks, bsz=bsz)
    qs, ks, vs = to3(qs), to3(ks), to3(vs)
    n_st = bsz * N_STACKS

    srow = _chunks_first_scalars(s_ref[...], nc).reshape(n_st, 2, STACK)
    per_stack = lambda ref: jnp.broadcast_to(ref[...][None], (bsz, N_STACKS, 1, STACK)).reshape(n_st, 1, STACK)
    beta_row = _sigmoid(srow[:, 0:1, :])
    g_row = -jnp.exp(per_stack(alog_ref)) * _softplus(srow[:, 1:2, :] + per_stack(dt_ref))
    n8 = n_st * SUBLANES
    g8 = jnp.broadcast_to(g_row, (n_st, SUBLANES, STACK)).reshape(n8, STACK)
    pieces = jnp.concatenate(_split3(g8), axis=0)
    cum = jnp.dot(pieces, u_ref[...], preferred_element_type=F32)
    cum = cum[0:n8] + cum[n8:2 * n8] + cum[2 * n8:]
    cum_row = cum.reshape(n_st, SUBLANES, STACK)[:, 0:1, :]

    r = lax.broadcasted_iota(jnp.int32, (STACK, STACK), 0)
    cc = lax.broadcasted_iota(jnp.int32, (STACK, STACK), 1)
    eye = (r == cc)[None]
    cum_col = jnp.sum(jnp.where(eye, cum_row, 0.0), axis=2, keepdims=True)
    beta_col = jnp.sum(jnp.where(eye, beta_row, 0.0), axis=2, keepdims=True)
    same = (r >> 6) == (cc >> 6)
    lower = (same & (r >= cc))[None]
    strict = (same & (r > cc))[None]
    ratio = jnp.exp(jnp.minimum(cum_col - cum_row, 0.0))

    kkt = _bdot_nt(ks, ks)
    qkt = _bdot_nt(qs, ks)
    lm = jnp.where(strict, beta_col * ratio * kkt, 0.0)

    ident = eye.astype(F32)
    l0 = jnp.where(((r >> 3) == (cc >> 3))[None], lm, 0.0)
    l2 = _bdot(l0, l0)
    l4 = _bdot(l2, l2)
    xinv = _bdot(_bdot(ident - l0, ident + l2), ident + l4)
    for lvl in (3, 4, 5):
        off = (((r >> (lvl + 1)) == (cc >> (lvl + 1))) & ((r >> lvl) != (cc >> lvl)))[None]
        xinv = xinv - _bdot(_bdot(xinv, jnp.where(off, lm, 0.0)), xinv)

    decay_col = jnp.exp(cum_col)
    rhs = jnp.concatenate([beta_col * vs, (beta_col * decay_col) * ks], axis=2)
    sol = _bdot(xinv, rhs)

    to_g = functools.partial(_to_heads, bsz=bsz)
    n_g = nb * HEADS
    st = st_ref[...]
    u0_g, w_g, q_g = to_g(sol[:, :, :HEAD_DIM]), to_g(sol[:, :, HEAD_DIM:]), to_g(qs)
    cum_g = to_g(cum_col)
    cum_end = cum_g[:, CHUNK - 1:CHUNK, :]
    kdec = to_g(ks) * jnp.exp(cum_end - cum_g)
    decay_g = to_g(decay_col)
    us, o_inter = [], []
    for j in range(nc):
        sl = slice(j * n_g, (j + 1) * n_g)
        u_j = u0_g[sl] - _bdot(w_g[sl], st)
        us.append(u_j)
        o_inter.append(decay_g[sl] * _bdot(q_g[sl], st))
        st = jnp.exp(cum_end[sl]) * st + _bdot_tn(kdec[sl], u_j)
    st_ref[...] = st
    stout_ref[...] = st
    attn = jnp.where(lower, qkt * ratio, 0.0)
    u = jnp.concatenate(us, axis=0).reshape(bsz * HEADS * CHUNK, HEAD_DIM)
    o = jnp.concatenate(o_inter, axis=0) + to_g(_bdot(attn, to3(u)))
    o = o.reshape(bsz * HEADS * CHUNK, HEAD_DIM)

    o = o * lax.rsqrt(jnp.mean(o * o, axis=-1, keepdims=True) + EPS) * g_ref[...]
    o = o * _stack_heads(gate, bsz)
    _store_chunks(o_ref, _unstack_heads(o, bsz), nb, nc)


def _recurrence_call(kernel_fn, name, proj, state0, operands, specs, nc):
    nb, length, _ = proj.shape
    rows = nc * CHUNK
    state_shape = (nb * HEADS, HEAD_DIM, HEAD_DIM)
    state_spec = pl.BlockSpec(state_shape, lambda c: (0, 0, 0))
    chunk_specs = [pl.BlockSpec((nb, rows, WIDTH), functools.partial(lambda c, j: (0, c, j), j=j))
                   for j in range(4)]
    return pl.pallas_call(
        functools.partial(kernel_fn, nb=nb, nc=nc),
        grid=(length // rows,),
        in_specs=specs + [state_spec] + chunk_specs,
        out_specs=[pl.BlockSpec((nb, rows, WIDTH), lambda c: (0, c, 0)), state_spec],
        out_shape=[jax.ShapeDtypeStruct((nb, length, WIDTH), BF16),
                   jax.ShapeDtypeStruct(state_shape, F32)],
        scratch_shapes=[pltpu.VMEM(state_shape, F32)],
        compiler_params=pltpu.CompilerParams(
            dimension_semantics=("arbitrary",), vmem_limit_bytes=VMEM_LIMIT),
        name=name,
    )(*operands, state0, *([proj] * 4))


def _hgrn(proj, state0, norm_g, nc):
    const = lambda c: (0, 0)
    specs = [pl.BlockSpec((STACK, STACK), const), pl.BlockSpec((1, HEAD_DIM), const)]
    return _recurrence_call(_hgrn_kernel, "hgrn", proj, state0, (_hgrn_level_table(), norm_g), specs, nc)


def _gdn(proj, state0, scal, a_log_row, dt_row, norm_g, nc):
    nb = proj.shape[0]
    const = lambda c: (0, 0)
    const3 = lambda c: (0, 0, 0)
    specs = [pl.BlockSpec((STACK, STACK), const),
             pl.BlockSpec((N_STACKS, 1, STACK), const3),
             pl.BlockSpec((N_STACKS, 1, STACK), const3),
             pl.BlockSpec((1, HEAD_DIM), const),
             pl.BlockSpec((nb, nc, N_STACKS, 2, STACK), lambda c: (0, c, 0, 0, 0))]
    return _recurrence_call(_gdn_kernel, "gdn", proj, state0,
                            (_gdn_cumsum_matrix(), a_log_row, dt_row, norm_g, scal), specs, nc)


ROUTE_LANE0 = N_GROUPS
ROUTE_NEG = -1e30


def _route_tile(lg, ltri, carry):
    lane = lax.broadcasted_iota(jnp.int32, lg.shape, 1).astype(F32)
    first = lambda mask: jnp.min(jnp.where(mask, lane, float(LANES)), axis=1, keepdims=True)
    top = lambda mask: jnp.max(jnp.where(mask, lg, ROUTE_NEG), axis=1, keepdims=True)

    is_g = lane < N_GROUPS
    gmax = top(is_g)
    grp = first(is_g & (lg == gmax))
    p_grp = 1.0 / jnp.sum(jnp.where(is_g, jnp.exp(lg - gmax), 0.0), axis=1, keepdims=True)

    lo = ROUTE_LANE0 + grp * EXPERTS_PER_GROUP
    in_grp = (lane >= lo) & (lane < lo + EXPERTS_PER_GROUP)
    m1 = top(in_grp)
    i1 = first(in_grp & (lg == m1))
    rest = in_grp & (lane != i1)
    m2 = top(rest)
    i2 = first(rest & (lg == m2))
    r = jnp.exp(m2 - m1)
    w0 = p_grp / (1.0 + r)
    w1 = p_grp * r / (1.0 + r)

    hot0 = (lane == i1).astype(F32)
    hot1 = (lane == i2).astype(F32)
    both = hot0 + hot1
    before = jnp.dot(ltri, both.astype(BF16), preferred_element_type=F32) + carry
    rank0 = jnp.sum(hot0 * before, axis=1, keepdims=True)
    rank1 = jnp.sum(hot1 * before, axis=1, keepdims=True)
    cols = (w0, w1, i1 - ROUTE_LANE0, i2 - ROUTE_LANE0, rank0, rank1)
    info = jnp.zeros_like(lg)
    for k, col in enumerate(cols):
        info = jnp.where(lane == k, col, info)
    return info, carry + jnp.sum(both, axis=0, keepdims=True)


def _merge_kernel(oa_ref, ob_ref, x_ref, mg_ref, wgate_ref, hgup_ref, gdup_ref, wout_ref,
                  ng_ref, rwh_ref, rwl_ref, rb_ref, ltri_ref, h2_ref, xn_ref, info_ref, cnt_ref,
                  wgate_bf, carry_ref):
    @pl.when(pl.program_id(0) == 0)
    def _():
        carry_ref[...] = jnp.zeros_like(carry_ref)
        _cast_rows(wgate_ref, wgate_bf)

    gates = _sigmoid(jnp.dot(_normed_bf16(x_ref, mg_ref), wgate_bf[...], preferred_element_type=F32))
    ua = jnp.dot(oa_ref[...], hgup_ref[...], preferred_element_type=F32)
    ub = jnp.dot(ob_ref[...], gdup_ref[...], preferred_element_type=F32)
    merged = gates[:, :D_MODEL] * ua + gates[:, D_MODEL:] * ub
    h2 = x_ref[...] + jnp.dot(merged.astype(BF16), wout_ref[...], preferred_element_type=F32)
    h2_ref[...] = h2
    xn = h2 * lax.rsqrt(jnp.mean(h2 * h2, axis=-1, keepdims=True) + EPS) * ng_ref[...]
    xn_ref[...] = _pack_halves(xn)
    xh = xn.astype(BF16)
    xl = (xn - xh.astype(F32)).astype(BF16)
    wh, wl = rwh_ref[...], rwl_ref[...]
    lg = (jnp.dot(xh, wh, preferred_element_type=F32)
          + jnp.dot(xh, wl, preferred_element_type=F32)
          + jnp.dot(xl, wh, preferred_element_type=F32)) + rb_ref[...]
    info, carry = _route_tile(lg, ltri_ref[...], carry_ref[...])
    info_ref[...] = info
    carry_ref[...] = carry
    cnt_ref[...] = carry


def _merge(o_a, o_b, x2d, mix_g, w_gates, hg_up, gd_up, w_out, norm_g, rw_hi, rw_lo, rb):
    t = x2d.shape[0]
    tm = TOKEN_TM
    row = lambda i: (i, 0)
    const = lambda i: (0, 0)
    ltri = jnp.asarray(np.tril(np.ones((tm, tm), np.float32), -1), dtype=BF16)
    return pl.pallas_call(
        _merge_kernel,
        grid=(t // tm,),
        in_specs=[pl.BlockSpec((tm, WIDTH), row),
                  pl.BlockSpec((tm, WIDTH), row),
                  pl.BlockSpec((tm, D_MODEL), row),
                  pl.BlockSpec((1, D_MODEL), const),
                  _weight_spec(D_MODEL, 2 * D_MODEL),
                  pl.BlockSpec((WIDTH, D_MODEL), const),
                  pl.BlockSpec((WIDTH, D_MODEL), const),
                  pl.BlockSpec((D_MODEL, D_MODEL), const),
                  pl.BlockSpec((1, D_MODEL), const),
                  pl.BlockSpec((D_MODEL, LANES), const),
                  pl.BlockSpec((D_MODEL, LANES), const),
                  pl.BlockSpec((1, LANES), const),
                  pl.BlockSpec((tm, tm), const)],
        out_specs=[pl.BlockSpec((tm, D_MODEL), row),
                   pl.BlockSpec((tm, HALF), row),
                   pl.BlockSpec((tm, LANES), row),
                   pl.BlockSpec((1, LANES), const)],
        out_shape=[jax.ShapeDtypeStruct((t, D_MODEL), F32),
                   jax.ShapeDtypeStruct((t, HALF), U32),
                   jax.ShapeDtypeStruct((t, LANES), F32),
                   jax.ShapeDtypeStruct((1, LANES), F32)],
        scratch_shapes=[pltpu.VMEM((D_MODEL, 2 * D_MODEL), BF16), pltpu.VMEM((1, LANES), F32)],
        compiler_params=pltpu.CompilerParams(
            dimension_semantics=("arbitrary",), vmem_limit_bytes=VMEM_LIMIT),
        name="merge",
    )(o_a, o_b, x2d, mix_g, w_gates, hg_up, gd_up, w_out, norm_g, rw_hi, rw_lo, rb, ltri)


def _sc_gather(table, idx):
    n_idx = idx.shape[0]
    cols = table.shape[1]
    per_worker = n_idx // SC_WORKERS
    assert n_idx % (SC_WORKERS * SC_CHUNK) == 0
    mesh = plsc.VectorSubcoreMesh(core_axis_name="c", subcore_axis_name="s")

    @functools.partial(
        pl.kernel, mesh=mesh,
        out_type=jax.ShapeDtypeStruct((n_idx, cols), table.dtype),
        scratch_types=[pltpu.VMEM((SC_CHUNK,), jnp.int32),
                       pltpu.VMEM((SC_CHUNK, cols), table.dtype),
                       pltpu.SemaphoreType.DMA],
    )
    def gather(table_hbm, idx_hbm, out_hbm, idx_v, rows_v, sem):
        worker = lax.axis_index("s") * SC_CORES + lax.axis_index("c")
        base = worker * per_worker

        @pl.loop(0, per_worker // SC_CHUNK)
        def _(c):
            off = pl.multiple_of(base + c * SC_CHUNK, SC_CHUNK)
            pltpu.sync_copy(idx_hbm.at[pl.ds(off, SC_CHUNK)], idx_v)
            pltpu.async_copy(table_hbm.at[idx_v], rows_v, sem).wait()
            pltpu.sync_copy(rows_v, out_hbm.at[pl.ds(off, SC_CHUNK)])

    return gather(table, idx)


def _sc_dispatch(rows, dest, n_out):
    t, cols = rows.shape
    per_worker = t // SC_WORKERS
    assert t % (SC_WORKERS * SC_CHUNK) == 0 and dest.shape[0] == 2 * t
    mesh = plsc.VectorSubcoreMesh(core_axis_name="c", subcore_axis_name="s")

    @functools.partial(
        pl.kernel, mesh=mesh,
        out_type=jax.ShapeDtypeStruct((n_out, cols), rows.dtype),
        scratch_types=[pltpu.VMEM((SC_CHUNK,), jnp.int32),
                       pltpu.VMEM((SC_CHUNK, cols), rows.dtype)],
    )
    def dispatch(rows_hbm, dest_hbm, out_hbm, idx_v, rows_v):
        worker = lax.axis_index("s") * SC_CORES + lax.axis_index("c")
        base = worker * per_worker

        @pl.loop(0, per_worker // SC_CHUNK)
        def _(c):
            off = pl.multiple_of(base + c * SC_CHUNK, SC_CHUNK)
            pltpu.sync_copy(rows_hbm.at[pl.ds(off, SC_CHUNK)], rows_v)
            for slot in range(2):
                pltpu.sync_copy(dest_hbm.at[pl.ds(slot * t + off, SC_CHUNK)], idx_v)
                pltpu.sync_copy(rows_v, out_hbm.at[idx_v])

    return dispatch(rows, dest)


def _moe_kernel(be_ref, nu_ref, x_ref, wg_ref, wu_ref, wd_ref, y_ref, wg_bf, wu_bf, wd_bf):
    i = pl.program_id(0)

    @pl.when((i == 0) | (be_ref[i] != be_ref[jnp.maximum(i - 1, 0)]))
    def _():
        _cast_rows(wg_ref, wg_bf)
        _cast_rows(wu_ref, wu_bf)
        _cast_rows(wd_ref, wd_bf)

    @pl.when(i < nu_ref[0])
    def _():
        xb = _unpack_halves(x_ref[...]).astype(BF16)
        a = jnp.dot(xb, wg_bf[...], preferred_element_type=F32)
        u = jnp.dot(xb, wu_bf[...], preferred_element_type=F32)
        y = jnp.dot((_silu(a) * u).astype(BF16), wd_bf[...], preferred_element_type=F32)
        y_ref[...] = _pack_halves(y)

    @pl.when(i >= nu_ref[0])
    def _():
        y_ref[...] = jnp.zeros_like(y_ref)


def _moe(blk_expert, n_used, x_sorted, w_gate, w_up, w_down):
    n_blocks = blk_expert.shape[0]
    wspec = lambda shape: pl.BlockSpec((None,) + shape, lambda i, be, nu: (be[i], 0, 0))
    rows = pl.BlockSpec((MOE_ROWS, HALF), lambda i, be, nu: (i, 0))
    grid_spec = pltpu.PrefetchScalarGridSpec(
        num_scalar_prefetch=2,
        grid=(n_blocks,),
        in_specs=[rows, wspec((D_MODEL, D_FF)), wspec((D_MODEL, D_FF)), wspec((D_FF, D_MODEL))],
        out_specs=rows,
        scratch_shapes=[pltpu.VMEM((D_MODEL, D_FF), BF16), pltpu.VMEM((D_MODEL, D_FF), BF16),
                        pltpu.VMEM((D_FF, D_MODEL), BF16)],
    )
    return pl.pallas_call(
        _moe_kernel,
        grid_spec=grid_spec,
        out_shape=jax.ShapeDtypeStruct(x_sorted.shape, U32),
        compiler_params=pltpu.CompilerParams(
            dimension_semantics=("arbitrary",), vmem_limit_bytes=VMEM_LIMIT),
        name="moe",
    )(blk_expert, n_used, x_sorted, w_gate, w_up, w_down)


def _combine_kernel(h2_ref, rw_ref, g_ref, y0_ref, y1_ref, o_ref):
    rw = rw_ref[...]
    h = h2_ref[...] + rw[:, 0:1] * _unpack_halves(y0_ref[...]) + rw[:, 1:2] * _unpack_halves(y1_ref[...])
    o_ref[...] = h * lax.rsqrt(jnp.mean(h * h, axis=-1, keepdims=True) + EPS) * g_ref[...]


def _combine(h2, rweights, final_g, y):
    t = h2.shape[0]
    tm = TOKEN_TM
    row = lambda i: (i, 0)
    return pl.pallas_call(
        _combine_kernel,
        grid=(t // tm,),
        in_specs=[pl.BlockSpec((tm, D_MODEL), row),
                  pl.BlockSpec((tm, LANES), row),
                  pl.BlockSpec((1, D_MODEL), lambda i: (0, 0)),
                  pl.BlockSpec((tm, HALF), row),
                  pl.BlockSpec((tm, HALF), lambda i: (i + t // tm, 0))],
        out_specs=pl.BlockSpec((tm, D_MODEL), row),
        out_shape=jax.ShapeDtypeStruct((t, D_MODEL), F32),
        compiler_params=pltpu.CompilerParams(
            dimension_semantics=("arbitrary",), vmem_limit_bytes=VMEM_LIMIT),
        name="combine",
    )(h2, rweights, final_g, y, y)


def _block_layout(info, counts_row, t):
    n_blocks = 2 * t // MOE_ROWS + N_EXPERTS
    counts = counts_row[0, ROUTE_LANE0:ROUTE_LANE0 + N_EXPERTS].astype(jnp.int32)
    padded = ((counts + MOE_ROWS - 1) // MOE_ROWS) * MOE_ROWS
    pend = jnp.cumsum(padded)
    pstart = pend - padded
    blk_first = jnp.arange(n_blocks, dtype=jnp.int32) * MOE_ROWS
    blk_expert = jnp.minimum(jnp.sum((pend[None, :] <= blk_first[:, None]).astype(jnp.int32), axis=1),
                             N_EXPERTS - 1).astype(jnp.int32)
    n_used = (pend[-1] // MOE_ROWS).astype(jnp.int32).reshape(1)
    expert = info[:, 2:4].astype(jnp.int32)
    rank = info[:, 4:6].astype(jnp.int32)
    experts = jnp.arange(N_EXPERTS, dtype=jnp.int32)
    first_row = jnp.sum(jnp.where(expert[:, :, None] == experts, pstart, 0), axis=-1)
    dest = (first_row + rank).T.reshape(2 * t)
    return dest, blk_expert, n_used, n_blocks * MOE_ROWS


def kernel(x, meta_tokens, hg_lb_logits, norm_mix_g, w_in, gd_conv_w, gd_A_log, gd_dt_bias, hg_norm_g, gd_norm_g, hg_up, gd_up, w_out, norm_ffn_g, router_group_w, router_group_b, router_expert_w, router_expert_b, w_gate, w_up, w_down, final_norm_g):
    bsz, seq, d = x.shape
    t = bsz * seq
    x2d = x.reshape(t, d)

    lb = jnp.cumsum(jax.nn.softmax(hg_lb_logits.astype(F32), axis=0), axis=0)[0].reshape(1, WIDTH)
    w = w_in[0].astype(F32)
    n_h, n_g = 4 * WIDTH, 8 * WIDTH
    n_s = n_g + 2 * HEADS
    w_groups = {
        "hgrn": w[:, :n_h],
        "gdn": jnp.concatenate([w[:, n_h:n_s], jnp.zeros((d, LANES - 2 * HEADS), F32)], axis=1),
        "gates": w[:, n_s:],
    }
    g_mix = norm_mix_g[0].reshape(1, d)
    meta_blk = jnp.concatenate([jnp.zeros((CHUNK - N_META, d), F32), meta_tokens.astype(F32)], axis=0)

    conv_w = gd_conv_w[0].astype(F32)
    no_history = jnp.zeros((SUBLANES, 3 * WIDTH), F32)
    proj, proj_meta = {}, {}
    proj["hgrn"] = _norm_proj_hgrn(x2d, g_mix, lb, w_groups["hgrn"], PROJ_TM)
    proj_meta["hgrn"] = _norm_proj_hgrn(meta_blk, g_mix, lb, w_groups["hgrn"], CHUNK)
    proj_meta["gdn"], meta_tail = _norm_proj_conv(meta_blk, g_mix, w_groups["gdn"], conv_w, no_history, CHUNK, CHUNK)
    proj["gdn"], _ = _norm_proj_conv(x2d, g_mix, w_groups["gdn"], conv_w, meta_tail, PROJ_TM, seq)

    def scalar_rows(p, nb, nc):
        s = p[:, 4 * WIDTH:4 * WIDTH + 2 * HEADS].reshape(nb, nc, CHUNK, 2, N_STACKS, HEADS_PER_STACK)
        return s.transpose(0, 1, 4, 3, 5, 2).reshape(nb, nc, N_STACKS, 2, STACK)

    a_log_row = jnp.repeat(gd_A_log[0].astype(F32), CHUNK).reshape(N_STACKS, 1, STACK)
    dt_row = jnp.repeat(gd_dt_bias[0].astype(F32), CHUNK).reshape(N_STACKS, 1, STACK)
    hg_g, gd_g = hg_norm_g[0].reshape(1, HEAD_DIM), gd_norm_g[0].reshape(1, HEAD_DIM)

    zero_state = jnp.zeros((HEADS, HEAD_DIM, HEAD_DIM), F32)
    per_seq = lambda st: jnp.tile(st, (bsz, 1, 1))
    _, hg_state = _hgrn(proj_meta["hgrn"][None], zero_state, hg_g, 1)
    o_a, _ = _hgrn(proj["hgrn"].reshape(bsz, seq, -1), per_seq(hg_state), hg_g, CHUNKS_PER_STEP)
    _, gd_state = _gdn(proj_meta["gdn"][None], zero_state, scalar_rows(proj_meta["gdn"], 1, 1),
                       a_log_row, dt_row, gd_g, 1)
    o_b, _ = _gdn(proj["gdn"].reshape(bsz, seq, -1), per_seq(gd_state),
                  scalar_rows(proj["gdn"], bsz, seq // CHUNK), a_log_row, dt_row, gd_g, CHUNKS_PER_STEP)

    rw = jnp.zeros((d, LANES), F32)
    rw = rw.at[:, :N_GROUPS].set(router_group_w[0]).at[:, N_GROUPS:N_GROUPS + N_EXPERTS].set(router_expert_w[0])
    rw_hi = rw.astype(BF16)
    rw_lo = (rw - rw_hi.astype(F32)).astype(BF16)
    rb = jnp.zeros((1, LANES), F32)
    rb = rb.at[0, :N_GROUPS].set(router_group_b[0]).at[0, N_GROUPS:N_GROUPS + N_EXPERTS].set(router_expert_b[0])
    h2, xn, info, counts = _merge(o_a.reshape(t, WIDTH), o_b.reshape(t, WIDTH), x2d, g_mix, w_groups["gates"],
                                  hg_up[0].astype(BF16), gd_up[0].astype(BF16), w_out[0].astype(BF16),
                                  norm_ffn_g[0].reshape(1, d), rw_hi, rw_lo, rb)

    dest, blk_expert, n_used, n_rows = _block_layout(info, counts, t)
    x_sorted = _sc_dispatch(xn, dest, n_rows)
    y_sorted = _moe(blk_expert, n_used, x_sorted,
                    w_gate[0].astype(F32), w_up[0].astype(F32), w_down[0].astype(F32))
    y_tok = _sc_gather(y_sorted, dest)
    out = _combine(h2, info, final_norm_g.reshape(1, d), y_tok)
    return out.reshape(bsz, seq, d)
```

```python
import functools

import numpy as np
import jax
import jax.numpy as jnp
from jax import lax
from jax.experimental import pallas as pl
from jax.experimental.pallas import tpu as pltpu
from jax.experimental.pallas import tpu_sc as plsc

F32 = jnp.float32
BF16 = jnp.bfloat16
U32 = jnp.uint32

D_MODEL = 1024
N_META = 16
CHUNK = 64
EPS = 1e-6
HEADS = 4
HEAD_DIM = 128
WIDTH = HEADS * HEAD_DIM
HEADS_PER_STACK = 2
STACK = HEADS_PER_STACK * CHUNK
N_STACKS = HEADS // HEADS_PER_STACK
CONV_W = 4
N_GROUPS = 4
EXPERTS_PER_GROUP = 8
N_EXPERTS = N_GROUPS * EXPERTS_PER_GROUP
D_FF = 512
LANES = 128
SUBLANES = 8
HALF = D_MODEL // 2
MOE_ROWS = 512
CAST_SLAB = 128
SC_CORES, SC_SUBCORES = 2, 16
SC_WORKERS = SC_CORES * SC_SUBCORES
SC_CHUNK = 128
PROJ_TM = 1024
W_HGRN, W_GDN, W_SMALL, W_GATES, W_ALL = 0, 4 * WIDTH, 8 * WIDTH, 12 * WIDTH, 16 * WIDTH
CHUNKS_PER_STEP = 2
TOKEN_TM = 512
COMBINE_PARTS = 4
VMEM_LIMIT = 48 * 1024 * 1024


def _bdot(a, b):
    return lax.dot_general(a.astype(BF16), b.astype(BF16), (((2,), (1,)), ((0,), (0,))),
                           preferred_element_type=F32)


def _bdot_nt(a, b):
    return lax.dot_general(a.astype(BF16), b.astype(BF16), (((2,), (2,)), ((0,), (0,))),
                           preferred_element_type=F32)


def _bdot_tn(a, b):
    return lax.dot_general(a.astype(BF16), b.astype(BF16), (((1,), (1,)), ((0,), (0,))),
                           preferred_element_type=F32)


def _split3(x):
    hi = x.astype(BF16)
    r1 = x - hi.astype(F32)
    mid = r1.astype(BF16)
    lo = (r1 - mid.astype(F32)).astype(BF16)
    return hi, mid, lo


def _sigmoid(x):
    return 1.0 / (1.0 + jnp.exp(-x))


def _silu(x):
    return x * _sigmoid(x)


def _softplus(x):
    return jnp.maximum(x, 0.0) + jnp.log(1.0 + jnp.exp(-jnp.abs(x)))


def _stack_heads(x, bsz):
    x3 = x.reshape(bsz, CHUNK, WIDTH)
    y = jnp.concatenate([x3[:, :, h * HEAD_DIM:(h + 1) * HEAD_DIM] for h in range(HEADS)], axis=1)
    return y.reshape(bsz * HEADS * CHUNK, HEAD_DIM)


def _unstack_heads(x, bsz):
    x3 = x.reshape(bsz, HEADS * CHUNK, HEAD_DIM)
    return jnp.concatenate([x3[:, h * CHUNK:(h + 1) * CHUNK, :] for h in range(HEADS)], axis=2)


def _to_stacks(a, bsz):
    return a.reshape(bsz * N_STACKS, STACK, a.shape[-1])


def _to_heads(a, bsz):
    return a.reshape(bsz * HEADS, CHUNK, a.shape[-1])


def _roll_rows(x, shift):
    n = x.shape[0]
    return pltpu.roll(x, shift % n, axis=0)


def _chunks_first(x, nc):
    return jnp.concatenate([x[:, j * CHUNK:(j + 1) * CHUNK] for j in range(nc)], axis=0)


def _chunks_first_scalars(s, nc):
    return jnp.concatenate([s[:, j] for j in range(nc)], axis=0)


def _load_chunks(ref, nc):
    x = _chunks_first(ref[...], nc)
    return x.reshape(x.shape[0] * CHUNK, WIDTH)


def _store_chunks(ref, x, nb, nc):
    ref[...] = jnp.concatenate([x[j * nb:(j + 1) * nb] for j in range(nc)], axis=1).astype(ref.dtype)


def _cast_rows(src_ref, dst_ref, col0=0):
    slab = CAST_SLAB
    rows, cols = src_ref.shape
    assert rows % slab == 0

    def body(k, carry):
        r = pl.multiple_of(k * slab, slab)
        dst_ref[pl.ds(r, slab), col0:col0 + cols] = src_ref[pl.ds(r, slab), :].astype(dst_ref.dtype)
        return carry

    lax.fori_loop(0, rows // slab, body, 0)


def _pack_halves(x):
    lo = lax.bitcast_convert_type(x[:, :HALF].astype(BF16).astype(F32), U32)
    hi = lax.bitcast_convert_type(x[:, HALF:].astype(BF16).astype(F32), U32)
    return (lo >> 16) | (hi & jnp.uint32(0xFFFF0000))


def _unpack_halves(p):
    lo = lax.bitcast_convert_type(p << 16, F32)
    hi = lax.bitcast_convert_type(p & jnp.uint32(0xFFFF0000), F32)
    return jnp.concatenate([lo, hi], axis=1)


def _normed_bf16(x_ref, g_ref):
    x = x_ref[...]
    return (x * lax.rsqrt(jnp.mean(x * x, axis=-1, keepdims=True) + EPS) * g_ref[...]).astype(BF16)


def _norm_proj_hgrn_kernel(x_ref, g_ref, lb_ref, w_ref, o_ref, wbf_ref):
    @pl.when(pl.program_id(0) == 0)
    def _():
        _cast_rows(w_ref, wbf_ref)

    acc = jnp.dot(_normed_bf16(x_ref, g_ref), wbf_ref[...], preferred_element_type=F32)
    w1, w2, w3 = WIDTH, 2 * WIDTH, 3 * WIDTH
    lb = lb_ref[...]
    o_ref[:, :w1] = _silu(acc[:, :w1])
    o_ref[:, w1:w2] = lb + (1.0 - lb) * _sigmoid(acc[:, w1:w2])
    o_ref[:, w2:w3] = acc[:, w2:w3]
    o_ref[:, w3:] = _silu(acc[:, w3:])


def _norm_proj_conv_kernel(x_ref, g_ref, w_ref, ws_ref, cw_ref, hist_ref, o_ref, tail_ref, wbf_ref, carry_ref,
                           *, steps_per_seq):
    i = pl.program_id(0)

    @pl.when(i == 0)
    def _():
        _cast_rows(w_ref, wbf_ref)
        _cast_rows(ws_ref, wbf_ref, 4 * WIDTH)

    xn = _normed_bf16(x_ref, g_ref)
    w3, w4 = 3 * WIDTH, 4 * WIDTH
    raw = jnp.dot(xn, wbf_ref[:, :w3], preferred_element_type=F32)
    rest = jnp.dot(xn, wbf_ref[:, w3:], preferred_element_type=F32)
    rows = raw.shape[0]
    hist = jnp.where(i % steps_per_seq == 0, hist_ref[...], carry_ref[...])
    cw = cw_ref[...]
    tap = lambda d: cw[CONV_W - 1 - d:CONV_W - d, :]
    body = raw * tap(0)
    head = raw[:SUBLANES]
    r8 = lax.broadcasted_iota(jnp.int32, head.shape, 0)
    first = head * tap(0)
    for d in range(1, CONV_W):
        body = body + _roll_rows(raw, d) * tap(d)
        first = first + jnp.where(r8 >= d, _roll_rows(head, d), _roll_rows(hist, d)) * tap(d)
    o_ref[:SUBLANES, :w3] = _silu(first)
    o_ref[SUBLANES:, :w3] = _silu(body[SUBLANES:])
    o_ref[:, w3:w4] = _silu(rest[:, :WIDTH])
    o_ref[:, w4:] = rest[:, WIDTH:]
    tail = raw[rows - SUBLANES:]
    carry_ref[...] = tail
    tail_ref[...] = tail


def _proj_call(kernel_fn, name, x2d, tm, n, operands, specs, extra_out=(), extra_scratch=()):
    m, k = x2d.shape
    tm = min(tm, m)
    out_specs = [pl.BlockSpec((tm, n), lambda i: (i, 0))] + [s for s, _ in extra_out]
    out_shape = [jax.ShapeDtypeStruct((m, n), F32)] + [s for _, s in extra_out]
    return pl.pallas_call(
        kernel_fn,
        grid=(m // tm,),
        in_specs=[pl.BlockSpec((tm, k), lambda i: (i, 0))] + specs,
        out_specs=out_specs,
        out_shape=out_shape,
        scratch_shapes=[pltpu.VMEM((k, n), BF16)] + list(extra_scratch),
        compiler_params=pltpu.CompilerParams(
            dimension_semantics=("arbitrary",), vmem_limit_bytes=VMEM_LIMIT),
        name=name,
    )(x2d, *operands)


def _weight_spec(k, n, col0):
    assert col0 % n == 0
    return pl.BlockSpec((k, n), lambda i: (0, col0 // n), pipeline_mode=pl.Buffered(1))


def _norm_proj_hgrn(x2d, g, lb, w, tm):
    k, n = w.shape[0], 4 * WIDTH
    const = lambda i: (0, 0)
    specs = [pl.BlockSpec((1, k), const), pl.BlockSpec((1, WIDTH), const), _weight_spec(k, n, W_HGRN)]
    return _proj_call(_norm_proj_hgrn_kernel, "norm_proj_hgrn", x2d, tm, n, (g, lb, w), specs)[0]


def _norm_proj_conv(x2d, g, w, conv_w, hist, tm, rows_per_seq):
    k, n = w.shape[0], 4 * WIDTH + LANES
    const = lambda i: (0, 0)
    tm = min(tm, rows_per_seq)
    assert rows_per_seq % tm == 0 and x2d.shape[0] % rows_per_seq == 0
    tail_shape = (SUBLANES, 3 * WIDTH)
    specs = [pl.BlockSpec((1, k), const), _weight_spec(k, 4 * WIDTH, W_GDN), _weight_spec(k, LANES, W_SMALL),
             pl.BlockSpec((CONV_W, 3 * WIDTH), const), pl.BlockSpec(tail_shape, const)]
    return _proj_call(functools.partial(_norm_proj_conv_kernel, steps_per_seq=rows_per_seq // tm),
                      "norm_proj_gdn", x2d, tm, n, (g, w, w, conv_w, hist), specs,
                      extra_out=[(pl.BlockSpec(tail_shape, const), jax.ShapeDtypeStruct(tail_shape, F32))],
                      extra_scratch=[pltpu.VMEM(tail_shape, F32)])


HGRN_LEVELS = 6
HGRN_DIAG = HGRN_LEVELS


def _hgrn_level_table():
    r = np.arange(STACK)
    t, hd = r % CHUNK, r // CHUNK
    x = t[:, None] ^ t[None, :]
    lv = np.floor(np.log2(np.maximum(x, 1))).astype(np.int32)
    valid = (hd[:, None] == hd[None, :]) & (t[:, None] > t[None, :])
    lv = np.where(valid, lv, -1)
    lv = np.where(r[:, None] == r[None, :], HGRN_DIAG, lv)
    return jnp.asarray(lv, dtype=jnp.int32)


def _hgrn_kernel(lv_ref, g_ref, st0_ref, q_ref, f_ref, v_ref, gate_ref, o_ref, stout_ref, st_ref, *, nb, nc):
    @pl.when(pl.program_id(0) == 0)
    def _():
        st_ref[...] = st0_ref[...]

    bsz = nc * nb
    q = _load_chunks(q_ref, nc)
    f = _load_chunks(f_ref, nc)
    v = _load_chunks(v_ref, nc)
    gate = _load_chunks(gate_ref, nc)

    per_seq = lambda a: a.reshape(bsz, CHUNK, WIDTH)
    trow = lax.broadcasted_iota(jnp.int32, (1, CHUNK, WIDTH), 1)
    b = jnp.log(f)
    s = 1
    while s < CHUNK:
        b = (per_seq(b) + jnp.where(trow >= s, per_seq(_roll_rows(b, s)), 0.0)).reshape(b.shape)
        s *= 2

    qs, ks, vs, bs = (_stack_heads(a, bsz) for a in (q, 1.0 - f, v, b))
    lv = lv_ref[...][None]
    to3 = functools.partial(_to_stacks, bsz=bsz)
    to_g = functools.partial(_to_heads, bsz=bsz)
    roll_g = lambda a, shift: to_g(_roll_rows(a.reshape(qs.shape), shift))
    t = lax.broadcasted_iota(jnp.int32, (1, CHUNK, HEAD_DIM), 1)
    qs_g, ks_g, bs_g = to_g(qs), to_g(ks), to_g(bs)

    attn = jnp.where(lv == HGRN_DIAG, _bdot_nt(to3(qs), to3(ks)), 0.0)
    bref = bs_g
    bnext = roll_g(bs_g, -1)
    for li in range(HGRN_LEVELS):
        m = 1 << li
        if li > 0:
            half = m // 2
            upper = (t & (m - 1)) >= half
            bref = jnp.where(upper, roll_g(bref, half), bref)
            bnext = jnp.where(upper, bnext, roll_g(bnext, -half))
        odd = ((t >> li) & 1) == 1
        both = jnp.where(odd, qs_g, ks_g) * jnp.exp(jnp.where(odd, bs_g - bref, bnext - bs_g))
        both = to3(both.reshape(qs.shape)).astype(BF16)
        attn = jnp.where(lv == li, _bdot_nt(both, both), attn)

    o = _bdot(attn, to3(vs))

    n_g = nb * HEADS
    st = st_ref[...]
    qe = qs_g * jnp.exp(bs_g)
    b_end = bs_g[:, CHUNK - 1:CHUNK, :]
    ke = ks_g * jnp.exp(b_end - bs_g)
    vs_g = to_g(vs)
    o_inter = []
    for j in range(nc):
        sl = slice(j * n_g, (j + 1) * n_g)
        o_inter.append(_bdot_nt(qe[sl], st))
        st = st * jnp.exp(b_end[sl]) + _bdot_tn(vs_g[sl], ke[sl])
    st_ref[...] = st
    stout_ref[...] = st
    o = (to_g(o) + jnp.concatenate(o_inter, axis=0)).reshape(qs.shape)

    o = o * lax.rsqrt(jnp.mean(o * o, axis=-1, keepdims=True) + EPS) * g_ref[...]
    o = o * _stack_heads(gate, bsz)
    _store_chunks(o_ref, _unstack_heads(o, bsz), nb, nc)


def _gdn_cumsum_matrix():
    r = np.arange(STACK)
    u = (r[:, None] // CHUNK == r[None, :] // CHUNK) & (r[:, None] <= r[None, :])
    return jnp.asarray(u, dtype=BF16)


def _gdn_kernel(u_ref, alog_ref, dt_ref, g_ref, s_ref, st0_ref, q_ref, k_ref, v_ref, z_ref,
                o_ref, stout_ref, st_ref, *, nb, nc):
    @pl.when(pl.program_id(0) == 0)
    def _():
        st_ref[...] = st0_ref[...]

    bsz = nc * nb
    qs = _stack_heads(_load_chunks(q_ref, nc), bsz)
    ks = _stack_heads(_load_chunks(k_ref, nc), bsz)
    vs = _stack_heads(_load_chunks(v_ref, nc), bsz)
    gate = _load_chunks(z_ref, nc)
    qs = qs * lax.rsqrt(jnp.sum(qs * qs, axis=-1, keepdims=True) + EPS) * (HEAD_DIM ** -0.5)
    ks = ks * lax.rsqrt(jnp.sum(ks * ks, axis=-1, keepdims=True) + EPS)
    to3 = functools.partial(_to_stacks, bsz=bsz)
    qs, ks, vs = to3(qs), to3(ks), to3(vs)
    n_st = bsz * N_STACKS

    srow = _chunks_first_scalars(s_ref[...], nc).reshape(n_st, 2, STACK)
    per_stack = lambda ref: jnp.broadcast_to(ref[...][None], (bsz, N_STACKS, 1, STACK)).reshape(n_st, 1, STACK)
    beta_row = _sigmoid(srow[:, 0:1, :])
    g_row = -jnp.exp(per_stack(alog_ref)) * _softplus(srow[:, 1:2, :] + per_stack(dt_ref))
    n8 = n_st * SUBLANES
    g8 = jnp.broadcast_to(g_row, (n_st, SUBLANES, STACK)).reshape(n8, STACK)
    pieces = jnp.concatenate(_split3(g8), axis=0)
    cum = jnp.dot(pieces, u_ref[...], preferred_element_type=F32)
    cum = cum[0:n8] + cum[n8:2 * n8] + cum[2 * n8:]
    cum_row = cum.reshape(n_st, SUBLANES, STACK)[:, 0:1, :]

    r = lax.broadcasted_iota(jnp.int32, (STACK, STACK), 0)
    cc = lax.broadcasted_iota(jnp.int32, (STACK, STACK), 1)
    eye = (r == cc)[None]
    cum_col = jnp.sum(jnp.where(eye, cum_row, 0.0), axis=2, keepdims=True)
    beta_col = jnp.sum(jnp.where(eye, beta_row, 0.0), axis=2, keepdims=True)
    same = (r >> 6) == (cc >> 6)
    lower = (same & (r >= cc))[None]
    strict = (same & (r > cc))[None]
    ratio = jnp.exp(jnp.minimum(cum_col - cum_row, 0.0))

    kkt = _bdot_nt(ks, ks)
    qkt = _bdot_nt(qs, ks)
    lm = jnp.where(strict, beta_col * ratio * kkt, 0.0)

    ident = eye.astype(F32)
    l0 = jnp.where(((r >> 3) == (cc >> 3))[None], lm, 0.0)
    l2 = _bdot(l0, l0)
    l4 = _bdot(l2, l2)
    xinv = _bdot(_bdot(ident - l0, ident + l2), ident + l4)
    for lvl in (3, 4, 5):
        off = (((r >> (lvl + 1)) == (cc >> (lvl + 1))) & ((r >> lvl) != (cc >> lvl)))[None]
        xinv = xinv - _bdot(_bdot(xinv, jnp.where(off, lm, 0.0)), xinv)

    decay_col = jnp.exp(cum_col)
    rhs = jnp.concatenate([beta_col * vs, (beta_col * decay_col) * ks], axis=2)
    sol = _bdot(xinv, rhs)

    to_g = functools.partial(_to_heads, bsz=bsz)
    n_g = nb * HEADS
    st = st_ref[...]
    u0_g, w_g, q_g = to_g(sol[:, :, :HEAD_DIM]), to_g(sol[:, :, HEAD_DIM:]), to_g(qs)
    cum_g = to_g(cum_col)
    cum_end = cum_g[:, CHUNK - 1:CHUNK, :]
    kdec = to_g(ks) * jnp.exp(cum_end - cum_g)
    decay_g = to_g(decay_col)
    us, o_inter = [], []
    for j in range(nc):
        sl = slice(j * n_g, (j + 1) * n_g)
        u_j = u0_g[sl] - _bdot(w_g[sl], st)
        us.append(u_j)
        o_inter.append(decay_g[sl] * _bdot(q_g[sl], st))
        st = jnp.exp(cum_end[sl]) * st + _bdot_tn(kdec[sl], u_j)
    st_ref[...] = st
    stout_ref[...] = st
    attn = jnp.where(lower, qkt * ratio, 0.0)
    u = jnp.concatenate(us, axis=0).reshape(bsz * HEADS * CHUNK, HEAD_DIM)
    o = jnp.concatenate(o_inter, axis=0) + to_g(_bdot(attn, to3(u)))
    o = o.reshape(bsz * HEADS * CHUNK, HEAD_DIM)

    o = o * lax.rsqrt(jnp.mean(o * o, axis=-1, keepdims=True) + EPS) * g_ref[...]
    o = o * _stack_heads(gate, bsz)
    _store_chunks(o_ref, _unstack_heads(o, bsz), nb, nc)


def _recurrence_call(kernel_fn, name, proj, state0, operands, specs, nc):
    nb, length, _ = proj.shape
    rows = nc * CHUNK
    state_shape = (nb * HEADS, HEAD_DIM, HEAD_DIM)
    state_spec = pl.BlockSpec(state_shape, lambda c: (0, 0, 0))
    chunk_specs = [pl.BlockSpec((nb, rows, WIDTH), functools.partial(lambda c, j: (0, c, j), j=j))
                   for j in range(4)]
    return pl.pallas_call(
        functools.partial(kernel_fn, nb=nb, nc=nc),
        grid=(length // rows,),
        in_specs=specs + [state_spec] + chunk_specs,
        out_specs=[pl.BlockSpec((nb, rows, WIDTH), lambda c: (0, c, 0)), state_spec],
        out_shape=[jax.ShapeDtypeStruct((nb, length, WIDTH), BF16),
                   jax.ShapeDtypeStruct(state_shape, F32)],
        scratch_shapes=[pltpu.VMEM(state_shape, F32)],
        compiler_params=pltpu.CompilerParams(
            dimension_semantics=("arbitrary",), vmem_limit_bytes=VMEM_LIMIT),
        name=name,
    )(*operands, state0, *([proj] * 4))


def _hgrn(proj, state0, norm_g, nc):
    const = lambda c: (0, 0)
    specs = [pl.BlockSpec((STACK, STACK), const), pl.BlockSpec((1, HEAD_DIM), const)]
    return _recurrence_call(_hgrn_kernel, "hgrn", proj, state0, (_hgrn_level_table(), norm_g), specs, nc)


def _gdn(proj, state0, scal, a_log_row, dt_row, norm_g, nc):
    nb = proj.shape[0]
    const = lambda c: (0, 0)
    const3 = lambda c: (0, 0, 0)
    specs = [pl.BlockSpec((STACK, STACK), const),
             pl.BlockSpec((N_STACKS, 1, STACK), const3),
             pl.BlockSpec((N_STACKS, 1, STACK), const3),
             pl.BlockSpec((1, HEAD_DIM), const),
             pl.BlockSpec((nb, nc, N_STACKS, 2, STACK), lambda c: (0, c, 0, 0, 0))]
    return _recurrence_call(_gdn_kernel, "gdn", proj, state0,
                            (_gdn_cumsum_matrix(), a_log_row, dt_row, norm_g, scal), specs, nc)


ROUTE_LANE0 = N_GROUPS
ROUTE_NEG = -1e30


def _route_tile(lg, ltri, carry):
    lane = lax.broadcasted_iota(jnp.int32, lg.shape, 1).astype(F32)
    first = lambda mask: jnp.min(jnp.where(mask, lane, float(LANES)), axis=1, keepdims=True)
    top = lambda mask: jnp.max(jnp.where(mask, lg, ROUTE_NEG), axis=1, keepdims=True)

    is_g = lane < N_GROUPS
    gmax = top(is_g)
    grp = first(is_g & (lg == gmax))
    p_grp = 1.0 / jnp.sum(jnp.where(is_g, jnp.exp(lg - gmax), 0.0), axis=1, keepdims=True)

    lo = ROUTE_LANE0 + grp * EXPERTS_PER_GROUP
    in_grp = (lane >= lo) & (lane < lo + EXPERTS_PER_GROUP)
    m1 = top(in_grp)
    i1 = first(in_grp & (lg == m1))
    rest = in_grp & (lane != i1)
    m2 = top(rest)
    i2 = first(rest & (lg == m2))
    r = jnp.exp(m2 - m1)
    w0 = p_grp / (1.0 + r)
    w1 = p_grp * r / (1.0 + r)

    hot0 = (lane == i1).astype(F32)
    hot1 = (lane == i2).astype(F32)
    both = hot0 + hot1
    before = jnp.dot(ltri, both.astype(BF16), preferred_element_type=F32) + carry
    rank0 = jnp.sum(hot0 * before, axis=1, keepdims=True)
    rank1 = jnp.sum(hot1 * before, axis=1, keepdims=True)
    cols = (w0, w1, i1 - ROUTE_LANE0, i2 - ROUTE_LANE0, rank0, rank1)
    info = jnp.zeros_like(lg)
    for k, col in enumerate(cols):
        info = jnp.where(lane == k, col, info)
    return info, carry + jnp.sum(both, axis=0, keepdims=True)


def _merge_kernel(oa_ref, ob_ref, x_ref, mg_ref, wgate_ref, hgup_ref, gdup_ref, wout_ref,
                  ng_ref, rwh_ref, rwl_ref, rb_ref, ltri_ref, h2_ref, xn_ref, info_ref, ids_ref, cnt_ref,
                  wgate_bf, carry_ref):
    @pl.when(pl.program_id(0) == 0)
    def _():
        carry_ref[...] = jnp.zeros_like(carry_ref)
        _cast_rows(wgate_ref, wgate_bf)

    gates = _sigmoid(jnp.dot(_normed_bf16(x_ref, mg_ref), wgate_bf[...], preferred_element_type=F32))
    ua = jnp.dot(oa_ref[...], hgup_ref[...], preferred_element_type=F32)
    ub = jnp.dot(ob_ref[...], gdup_ref[...], preferred_element_type=F32)
    merged = gates[:, :D_MODEL] * ua + gates[:, D_MODEL:] * ub
    h2 = x_ref[...] + jnp.dot(merged.astype(BF16), wout_ref[...], preferred_element_type=F32)
    h2_ref[...] = h2
    xn = h2 * lax.rsqrt(jnp.mean(h2 * h2, axis=-1, keepdims=True) + EPS) * ng_ref[...]
    xn_ref[...] = _pack_halves(xn)
    xh = xn.astype(BF16)
    xl = (xn - xh.astype(F32)).astype(BF16)
    wh, wl = rwh_ref[...], rwl_ref[...]
    lg = (jnp.dot(xh, wh, preferred_element_type=F32)
          + jnp.dot(xh, wl, preferred_element_type=F32)
          + jnp.dot(xl, wh, preferred_element_type=F32)) + rb_ref[...]
    info, carry = _route_tile(lg, ltri_ref[...], carry_ref[...])
    info_ref[...] = info
    ids_ref[...] = info.T[:SUBLANES].astype(jnp.int32)
    carry_ref[...] = carry
    cnt_ref[...] = carry


def _merge(o_a, o_b, x2d, mix_g, w_gates, hg_up, gd_up, w_out, norm_g, rw_hi, rw_lo, rb):
    t = x2d.shape[0]
    tm = TOKEN_TM
    row = lambda i: (i, 0)
    const = lambda i: (0, 0)
    ltri = jnp.asarray(np.tril(np.ones((tm, tm), np.float32), -1), dtype=BF16)
    return pl.pallas_call(
        _merge_kernel,
        grid=(t // tm,),
        in_specs=[pl.BlockSpec((tm, WIDTH), row),
                  pl.BlockSpec((tm, WIDTH), row),
                  pl.BlockSpec((tm, D_MODEL), row),
                  pl.BlockSpec((1, D_MODEL), const),
                  _weight_spec(D_MODEL, 2 * D_MODEL, W_GATES),
                  pl.BlockSpec((WIDTH, D_MODEL), const),
                  pl.BlockSpec((WIDTH, D_MODEL), const),
                  pl.BlockSpec((D_MODEL, D_MODEL), const),
                  pl.BlockSpec((1, D_MODEL), const),
                  pl.BlockSpec((D_MODEL, LANES), const),
                  pl.BlockSpec((D_MODEL, LANES), const),
                  pl.BlockSpec((1, LANES), const),
                  pl.BlockSpec((tm, tm), const)],
        out_specs=[pl.BlockSpec((tm, D_MODEL), row),
                   pl.BlockSpec((tm, HALF), row),
                   pl.BlockSpec((tm, LANES), row),
                   pl.BlockSpec((SUBLANES, tm), lambda i: (0, i)),
                   pl.BlockSpec((1, LANES), const)],
        out_shape=[jax.ShapeDtypeStruct((t, D_MODEL), F32),
                   jax.ShapeDtypeStruct((t, HALF), U32),
                   jax.ShapeDtypeStruct((t, LANES), F32),
                   jax.ShapeDtypeStruct((SUBLANES, t), jnp.int32),
                   jax.ShapeDtypeStruct((1, LANES), F32)],
        scratch_shapes=[pltpu.VMEM((D_MODEL, 2 * D_MODEL), BF16), pltpu.VMEM((1, LANES), F32)],
        compiler_params=pltpu.CompilerParams(
            dimension_semantics=("arbitrary",), vmem_limit_bytes=VMEM_LIMIT),
        name="merge",
    )(o_a, o_b, x2d, mix_g, w_gates, hg_up, gd_up, w_out, norm_g, rw_hi, rw_lo, rb, ltri)


def _sc_gather(table, idx):
    n_idx = idx.shape[0]
    cols = table.shape[1]
    per_worker = n_idx // SC_WORKERS
    assert n_idx % (SC_WORKERS * SC_CHUNK) == 0
    mesh = plsc.VectorSubcoreMesh(core_axis_name="c", subcore_axis_name="s")

    @functools.partial(
        pl.kernel, mesh=mesh,
        out_type=jax.ShapeDtypeStruct((n_idx, cols), table.dtype),
        scratch_types=[pltpu.VMEM((SC_CHUNK,), jnp.int32),
                       pltpu.VMEM((SC_CHUNK, cols), table.dtype),
                       pltpu.SemaphoreType.DMA],
    )
    def gather(table_hbm, idx_hbm, out_hbm, idx_v, rows_v, sem):
        worker = lax.axis_index("s") * SC_CORES + lax.axis_index("c")
        base = worker * per_worker

        @pl.loop(0, per_worker // SC_CHUNK)
        def _(c):
            off = pl.multiple_of(base + c * SC_CHUNK, SC_CHUNK)
            pltpu.sync_copy(idx_hbm.at[pl.ds(off, SC_CHUNK)], idx_v)
            pltpu.async_copy(table_hbm.at[idx_v], rows_v, sem).wait()
            pltpu.sync_copy(rows_v, out_hbm.at[pl.ds(off, SC_CHUNK)])

    return gather(table, idx)


def _sc_dispatch(rows, dest, n_out):
    t, cols = rows.shape
    per_worker = t // SC_WORKERS
    assert t % (SC_WORKERS * SC_CHUNK) == 0 and dest.shape[0] == 2 * t
    mesh = plsc.VectorSubcoreMesh(core_axis_name="c", subcore_axis_name="s")

    @functools.partial(
        pl.kernel, mesh=mesh,
        out_type=jax.ShapeDtypeStruct((n_out, cols), rows.dtype),
        scratch_types=[pltpu.VMEM((SC_CHUNK,), jnp.int32),
                       pltpu.VMEM((SC_CHUNK, cols), rows.dtype)],
    )
    def dispatch(rows_hbm, dest_hbm, out_hbm, idx_v, rows_v):
        worker = lax.axis_index("s") * SC_CORES + lax.axis_index("c")
        base = worker * per_worker

        @pl.loop(0, per_worker // SC_CHUNK)
        def _(c):
            off = pl.multiple_of(base + c * SC_CHUNK, SC_CHUNK)
            pltpu.sync_copy(rows_hbm.at[pl.ds(off, SC_CHUNK)], rows_v)
            for slot in range(2):
                pltpu.sync_copy(dest_hbm.at[pl.ds(slot * t + off, SC_CHUNK)], idx_v)
                pltpu.sync_copy(rows_v, out_hbm.at[idx_v])

    return dispatch(rows, dest)


def _moe_kernel(be_ref, nu_ref, x_ref, wg_ref, wu_ref, wd_ref, y_ref, wg_bf, wu_bf, wd_bf):
    i = pl.program_id(0)

    @pl.when((i == 0) | (be_ref[i] != be_ref[jnp.maximum(i - 1, 0)]))
    def _():
        _cast_rows(wg_ref, wg_bf)
        _cast_rows(wu_ref, wu_bf)
        _cast_rows(wd_ref, wd_bf)

    @pl.when(i < nu_ref[0])
    def _():
        xb = _unpack_halves(x_ref[...]).astype(BF16)
        a = jnp.dot(xb, wg_bf[...], preferred_element_type=F32)
        u = jnp.dot(xb, wu_bf[...], preferred_element_type=F32)
        y = jnp.dot((_silu(a) * u).astype(BF16), wd_bf[...], preferred_element_type=F32)
        y_ref[...] = _pack_halves(y)

    @pl.when(i >= nu_ref[0])
    def _():
        y_ref[...] = jnp.zeros_like(y_ref)


def _moe(blk_expert, n_used, x_sorted, w_gate, w_up, w_down):
    n_blocks = blk_expert.shape[0]
    wspec = lambda shape: pl.BlockSpec((None,) + shape, lambda i, be, nu: (be[i], 0, 0))
    rows = pl.BlockSpec((MOE_ROWS, HALF), lambda i, be, nu: (i, 0))
    grid_spec = pltpu.PrefetchScalarGridSpec(
        num_scalar_prefetch=2,
        grid=(n_blocks,),
        in_specs=[rows, wspec((D_MODEL, D_FF)), wspec((D_MODEL, D_FF)), wspec((D_FF, D_MODEL))],
        out_specs=rows,
        scratch_shapes=[pltpu.VMEM((D_MODEL, D_FF), BF16), pltpu.VMEM((D_MODEL, D_FF), BF16),
                        pltpu.VMEM((D_FF, D_MODEL), BF16)],
    )
    return pl.pallas_call(
        _moe_kernel,
        grid_spec=grid_spec,
        out_shape=jax.ShapeDtypeStruct(x_sorted.shape, U32),
        compiler_params=pltpu.CompilerParams(
            dimension_semantics=("arbitrary",), vmem_limit_bytes=VMEM_LIMIT),
        name="moe",
    )(blk_expert, n_used, x_sorted, w_gate, w_up, w_down)


def _combine_kernel(h2_ref, rw_ref, g_ref, y0_ref, y1_ref, *rest):
    o_ref = rest[-1]
    rw = rw_ref[...]
    h = h2_ref[...] + rw[:, 0:1] * _unpack_halves(y0_ref[...]) + rw[:, 1:2] * _unpack_halves(y1_ref[...])
    o_ref[...] = h * lax.rsqrt(jnp.mean(h * h, axis=-1, keepdims=True) + EPS) * g_ref[...]


def _combine(h2, rweights, final_g, y_part, part, n_parts, out_prev):
    t = h2.shape[0]
    tm = TOKEN_TM
    steps = t // n_parts // tm
    here = lambda i: (i + part * steps, 0)
    in_specs = [pl.BlockSpec((tm, D_MODEL), here),
                pl.BlockSpec((tm, LANES), here),
                pl.BlockSpec((1, D_MODEL), lambda i: (0, 0)),
                pl.BlockSpec((tm, HALF), lambda i: (i, 0)),
                pl.BlockSpec((tm, HALF), lambda i: (i + steps, 0))]
    operands = [h2, rweights, final_g, y_part, y_part]
    aliases = {}
    if out_prev is not None:
        in_specs.append(pl.BlockSpec(memory_space=pl.ANY))
        operands.append(out_prev)
        aliases = {len(operands) - 1: 0}
    return pl.pallas_call(
        _combine_kernel,
        grid=(steps,),
        in_specs=in_specs,
        out_specs=pl.BlockSpec((tm, D_MODEL), here),
        out_shape=jax.ShapeDtypeStruct((t, D_MODEL), F32),
        input_output_aliases=aliases,
        compiler_params=pltpu.CompilerParams(
            dimension_semantics=("arbitrary",), vmem_limit_bytes=VMEM_LIMIT),
        name="combine",
    )(*operands)


def _block_layout(ids, counts_row, t):
    n_blocks = 2 * t // MOE_ROWS + N_EXPERTS
    counts = counts_row[0, ROUTE_LANE0:ROUTE_LANE0 + N_EXPERTS].astype(jnp.int32)
    padded = ((counts + MOE_ROWS - 1) // MOE_ROWS) * MOE_ROWS
    pend = jnp.cumsum(padded)
    pstart = pend - padded
    blk_first = jnp.arange(n_blocks, dtype=jnp.int32) * MOE_ROWS
    blk_expert = jnp.minimum(jnp.sum((pend[None, :] <= blk_first[:, None]).astype(jnp.int32), axis=1),
                             N_EXPERTS - 1).astype(jnp.int32)
    n_used = (pend[-1] // MOE_ROWS).astype(jnp.int32).reshape(1)
    expert, rank = ids[2:4], ids[4:6]
    experts = jnp.arange(N_EXPERTS, dtype=jnp.int32)
    first_row = jnp.sum(jnp.where(expert[:, :, None] == experts, pstart, 0), axis=-1)
    dest = (first_row + rank).reshape(2 * t)
    return dest, blk_expert, n_used, n_blocks * MOE_ROWS


def kernel(x, meta_tokens, hg_lb_logits, norm_mix_g, w_in, gd_conv_w, gd_A_log, gd_dt_bias, hg_norm_g, gd_norm_g, hg_up, gd_up, w_out, norm_ffn_g, router_group_w, router_group_b, router_expert_w, router_expert_b, w_gate, w_up, w_down, final_norm_g):
    bsz, seq, d = x.shape
    t = bsz * seq
    x2d = x.reshape(t, d)

    lb = jnp.cumsum(jax.nn.softmax(hg_lb_logits.astype(F32), axis=0), axis=0)[0].reshape(1, WIDTH)
    w = w_in[0].astype(F32)
    n_s = W_SMALL + 2 * HEADS
    w_all = jnp.concatenate([w[:, :n_s], jnp.zeros((d, W_GATES - n_s), F32), w[:, n_s:]], axis=1)
    g_mix = norm_mix_g[0].reshape(1, d)
    meta_blk = jnp.concatenate([jnp.zeros((CHUNK - N_META, d), F32), meta_tokens.astype(F32)], axis=0)

    conv_w = gd_conv_w[0].astype(F32)
    no_history = jnp.zeros((SUBLANES, 3 * WIDTH), F32)
    proj, proj_meta = {}, {}
    proj["hgrn"] = _norm_proj_hgrn(x2d, g_mix, lb, w_all, PROJ_TM)
    proj_meta["hgrn"] = _norm_proj_hgrn(meta_blk, g_mix, lb, w_all, CHUNK)
    proj_meta["gdn"], meta_tail = _norm_proj_conv(meta_blk, g_mix, w_all, conv_w, no_history, CHUNK, CHUNK)
    proj["gdn"], _ = _norm_proj_conv(x2d, g_mix, w_all, conv_w, meta_tail, PROJ_TM, seq)

    def scalar_rows(p, nb, nc):
        s = p[:, 4 * WIDTH:4 * WIDTH + 2 * HEADS].reshape(nb, nc, CHUNK, 2, N_STACKS, HEADS_PER_STACK)
        return s.transpose(0, 1, 4, 3, 5, 2).reshape(nb, nc, N_STACKS, 2, STACK)

    a_log_row = jnp.repeat(gd_A_log[0].astype(F32), CHUNK).reshape(N_STACKS, 1, STACK)
    dt_row = jnp.repeat(gd_dt_bias[0].astype(F32), CHUNK).reshape(N_STACKS, 1, STACK)
    hg_g, gd_g = hg_norm_g[0].reshape(1, HEAD_DIM), gd_norm_g[0].reshape(1, HEAD_DIM)

    zero_state = jnp.zeros((HEADS, HEAD_DIM, HEAD_DIM), F32)
    per_seq = lambda st: jnp.tile(st, (bsz, 1, 1))
    _, hg_state = _hgrn(proj_meta["hgrn"][None], zero_state, hg_g, 1)
    o_a, _ = _hgrn(proj["hgrn"].reshape(bsz, seq, -1), per_seq(hg_state), hg_g, CHUNKS_PER_STEP)
    _, gd_state = _gdn(proj_meta["gdn"][None], zero_state, scalar_rows(proj_meta["gdn"], 1, 1),
                       a_log_row, dt_row, gd_g, 1)
    o_b, _ = _gdn(proj["gdn"].reshape(bsz, seq, -1), per_seq(gd_state),
                  scalar_rows(proj["gdn"], bsz, seq // CHUNK), a_log_row, dt_row, gd_g, CHUNKS_PER_STEP)

    rw = jnp.zeros((d, LANES), F32)
    rw = rw.at[:, :N_GROUPS].set(router_group_w[0]).at[:, N_GROUPS:N_GROUPS + N_EXPERTS].set(router_expert_w[0])
    rw_hi = rw.astype(BF16)
    rw_lo = (rw - rw_hi.astype(F32)).astype(BF16)
    rb = jnp.zeros((1, LANES), F32)
    rb = rb.at[0, :N_GROUPS].set(router_group_b[0]).at[0, N_GROUPS:N_GROUPS + N_EXPERTS].set(router_expert_b[0])
    h2, xn, info, ids, counts = _merge(o_a.reshape(t, WIDTH), o_b.reshape(t, WIDTH), x2d, g_mix, w_all,
                                  hg_up[0].astype(BF16), gd_up[0].astype(BF16), w_out[0].astype(BF16),
                                  norm_ffn_g[0].reshape(1, d), rw_hi, rw_lo, rb)

    dest, blk_expert, n_used, n_rows = _block_layout(ids, counts, t)
    x_sorted = _sc_dispatch(xn, dest, n_rows)
    y_sorted = _moe(blk_expert, n_used, x_sorted,
                    w_gate[0].astype(F32), w_up[0].astype(F32), w_down[0].astype(F32))
    n_parts = max(1, min(COMBINE_PARTS, t // TOKEN_TM))
    tp = t // n_parts
    out = None
    for part in range(n_parts):
        lo = part * tp
        idx = jnp.concatenate([dest[lo:lo + tp], dest[t + lo:t + lo + tp]])
        y_part = _sc_gather(y_sorted, idx)
        out = _combine(h2, info, final_norm_g.reshape(1, d), y_part, part, n_parts, out)
    return out.reshape(bsz, seq, d)
```

```python
import functools

import numpy as np
import jax
import jax.numpy as jnp
from jax import lax
from jax.experimental import pallas as pl
from jax.experimental.pallas import tpu as pltpu
from jax.experimental.pallas import tpu_sc as plsc

F32 = jnp.float32
BF16 = jnp.bfloat16
U32 = jnp.uint32

D_MODEL = 1024
N_META = 16
CHUNK = 64
EPS = 1e-6
HEADS = 4
HEAD_DIM = 128
WIDTH = HEADS * HEAD_DIM
HEADS_PER_STACK = 2
STACK = HEADS_PER_STACK * CHUNK
N_STACKS = HEADS // HEADS_PER_STACK
CONV_W = 4
N_GROUPS = 4
EXPERTS_PER_GROUP = 8
N_EXPERTS = N_GROUPS * EXPERTS_PER_GROUP
D_FF = 512
LANES = 128
SUBLANES = 8
HALF = D_MODEL // 2
MOE_ROWS = 512
CAST_SLAB = 128
SC_CORES, SC_SUBCORES = 2, 16
SC_WORKERS = SC_CORES * SC_SUBCORES
SC_CHUNK = 128
PROJ_TM = 1024
W_HGRN, W_GDN, W_SMALL = 0, 4 * WIDTH, 8 * WIDTH
CHUNKS_PER_STEP = 2
TOKEN_TM = 512
VMEM_LIMIT = 48 * 1024 * 1024


def _bdot(a, b):
    return lax.dot_general(a.astype(BF16), b.astype(BF16), (((2,), (1,)), ((0,), (0,))),
                           preferred_element_type=F32)


def _bdot_nt(a, b):
    return lax.dot_general(a.astype(BF16), b.astype(BF16), (((2,), (2,)), ((0,), (0,))),
                           preferred_element_type=F32)


def _bdot_tn(a, b):
    return lax.dot_general(a.astype(BF16), b.astype(BF16), (((1,), (1,)), ((0,), (0,))),
                           preferred_element_type=F32)


def _split3(x):
    hi = x.astype(BF16)
    r1 = x - hi.astype(F32)
    mid = r1.astype(BF16)
    lo = (r1 - mid.astype(F32)).astype(BF16)
    return hi, mid, lo


def _sigmoid(x):
    return 1.0 / (1.0 + jnp.exp(-x))


def _silu(x):
    return x * _sigmoid(x)


def _softplus(x):
    return jnp.maximum(x, 0.0) + jnp.log(1.0 + jnp.exp(-jnp.abs(x)))


def _stack_heads(x, bsz):
    x3 = x.reshape(bsz, CHUNK, WIDTH)
    y = jnp.concatenate([x3[:, :, h * HEAD_DIM:(h + 1) * HEAD_DIM] for h in range(HEADS)], axis=1)
    return y.reshape(bsz * HEADS * CHUNK, HEAD_DIM)


def _unstack_heads(x, bsz):
    x3 = x.reshape(bsz, HEADS * CHUNK, HEAD_DIM)
    return jnp.concatenate([x3[:, h * CHUNK:(h + 1) * CHUNK, :] for h in range(HEADS)], axis=2)


def _to_stacks(a, bsz):
    return a.reshape(bsz * N_STACKS, STACK, a.shape[-1])


def _to_heads(a, bsz):
    return a.reshape(bsz * HEADS, CHUNK, a.shape[-1])


def _roll_rows(x, shift):
    n = x.shape[0]
    return pltpu.roll(x, shift % n, axis=0)


def _chunks_first(x, nc):
    return jnp.concatenate([x[:, j * CHUNK:(j + 1) * CHUNK] for j in range(nc)], axis=0)


def _chunks_first_scalars(s, nc):
    return jnp.concatenate([s[:, j] for j in range(nc)], axis=0)


def _load_chunks(ref, nc):
    x = _chunks_first(ref[...], nc)
    return x.reshape(x.shape[0] * CHUNK, WIDTH)


def _store_chunks(ref, x, nb, nc):
    ref[...] = jnp.concatenate([x[j * nb:(j + 1) * nb] for j in range(nc)], axis=1).astype(ref.dtype)


def _cast_rows(src_ref, dst_ref, col0=0):
    slab = CAST_SLAB
    rows, cols = src_ref.shape
    assert rows % slab == 0

    def body(k, carry):
        r = pl.multiple_of(k * slab, slab)
        dst_ref[pl.ds(r, slab), col0:col0 + cols] = src_ref[pl.ds(r, slab), :].astype(dst_ref.dtype)
        return carry

    lax.fori_loop(0, rows // slab, body, 0)


def _pack_halves(x):
    lo = lax.bitcast_convert_type(x[:, :HALF].astype(BF16).astype(F32), U32)
    hi = lax.bitcast_convert_type(x[:, HALF:].astype(BF16).astype(F32), U32)
    return (lo >> 16) | (hi & jnp.uint32(0xFFFF0000))


def _unpack_halves(p):
    lo = lax.bitcast_convert_type(p << 16, F32)
    hi = lax.bitcast_convert_type(p & jnp.uint32(0xFFFF0000), F32)
    return jnp.concatenate([lo, hi], axis=1)


def _normed_bf16(x_ref, g_ref):
    x = x_ref[...]
    return (x * lax.rsqrt(jnp.mean(x * x, axis=-1, keepdims=True) + EPS) * g_ref[...]).astype(BF16)


def _norm_proj_hgrn_kernel(x_ref, g_ref, lb_ref, w_ref, o_ref, wbf_ref):
    @pl.when(pl.program_id(0) == 0)
    def _():
        _cast_rows(w_ref, wbf_ref)

    acc = jnp.dot(_normed_bf16(x_ref, g_ref), wbf_ref[...], preferred_element_type=F32)
    w1, w2, w3 = WIDTH, 2 * WIDTH, 3 * WIDTH
    lb = lb_ref[...]
    o_ref[:, :w1] = _silu(acc[:, :w1])
    o_ref[:, w1:w2] = lb + (1.0 - lb) * _sigmoid(acc[:, w1:w2])
    o_ref[:, w2:w3] = acc[:, w2:w3]
    o_ref[:, w3:] = _silu(acc[:, w3:])


def _norm_proj_conv_kernel(x_ref, g_ref, w_ref, ws_ref, cw_ref, hist_ref, o_ref, tail_ref, wbf_ref, carry_ref,
                           *, steps_per_seq):
    i = pl.program_id(0)

    @pl.when(i == 0)
    def _():
        _cast_rows(w_ref, wbf_ref)
        _cast_rows(ws_ref, wbf_ref, 4 * WIDTH)

    xn = _normed_bf16(x_ref, g_ref)
    w3, w4 = 3 * WIDTH, 4 * WIDTH
    raw = jnp.dot(xn, wbf_ref[:, :w3], preferred_element_type=F32)
    rest = jnp.dot(xn, wbf_ref[:, w3:], preferred_element_type=F32)
    rows = raw.shape[0]
    hist = jnp.where(i % steps_per_seq == 0, hist_ref[...], carry_ref[...])
    cw = cw_ref[...]
    tap = lambda d: cw[CONV_W - 1 - d:CONV_W - d, :]
    body = raw * tap(0)
    head = raw[:SUBLANES]
    r8 = lax.broadcasted_iota(jnp.int32, head.shape, 0)
    first = head * tap(0)
    for d in range(1, CONV_W):
        body = body + _roll_rows(raw, d) * tap(d)
        first = first + jnp.where(r8 >= d, _roll_rows(head, d), _roll_rows(hist, d)) * tap(d)
    o_ref[:SUBLANES, :w3] = _silu(first)
    o_ref[SUBLANES:, :w3] = _silu(body[SUBLANES:])
    o_ref[:, w3:w4] = _silu(rest[:, :WIDTH])
    o_ref[:, w4:] = rest[:, WIDTH:]
    tail = raw[rows - SUBLANES:]
    carry_ref[...] = tail
    tail_ref[...] = tail


def _proj_call(kernel_fn, name, x2d, tm, n, operands, specs, extra_out=(), extra_scratch=()):
    m, k = x2d.shape
    tm = min(tm, m)
    out_specs = [pl.BlockSpec((tm, n), lambda i: (i, 0))] + [s for s, _ in extra_out]
    out_shape = [jax.ShapeDtypeStruct((m, n), F32)] + [s for _, s in extra_out]
    return pl.pallas_call(
        kernel_fn,
        grid=(m // tm,),
        in_specs=[pl.BlockSpec((tm, k), lambda i: (i, 0))] + specs,
        out_specs=out_specs,
        out_shape=out_shape,
        scratch_shapes=[pltpu.VMEM((k, n), BF16)] + list(extra_scratch),
        compiler_params=pltpu.CompilerParams(
            dimension_semantics=("arbitrary",), vmem_limit_bytes=VMEM_LIMIT),
        name=name,
    )(x2d, *operands)


def _weight_spec(k, n, col0):
    assert col0 % n == 0
    return pl.BlockSpec((None, k, n), lambda i: (0, 0, col0 // n), pipeline_mode=pl.Buffered(1))


def _norm_proj_hgrn(x2d, g, lb, w, tm):
    k, n = w.shape[1], 4 * WIDTH
    const = lambda i: (0, 0)
    specs = [pl.BlockSpec((1, k), const), pl.BlockSpec((1, WIDTH), const), _weight_spec(k, n, W_HGRN)]
    return _proj_call(_norm_proj_hgrn_kernel, "norm_proj_hgrn", x2d, tm, n, (g, lb, w), specs)[0]


def _norm_proj_conv(x2d, g, w, conv_w, hist, tm, rows_per_seq):
    k, n = w.shape[1], 4 * WIDTH + LANES
    const = lambda i: (0, 0)
    tm = min(tm, rows_per_seq)
    assert rows_per_seq % tm == 0 and x2d.shape[0] % rows_per_seq == 0
    tail_shape = (SUBLANES, 3 * WIDTH)
    specs = [pl.BlockSpec((1, k), const), _weight_spec(k, 4 * WIDTH, W_GDN), _weight_spec(k, LANES, W_SMALL),
             pl.BlockSpec((CONV_W, 3 * WIDTH), const), pl.BlockSpec(tail_shape, const)]
    return _proj_call(functools.partial(_norm_proj_conv_kernel, steps_per_seq=rows_per_seq // tm),
                      "norm_proj_gdn", x2d, tm, n, (g, w, w, conv_w, hist), specs,
                      extra_out=[(pl.BlockSpec(tail_shape, const), jax.ShapeDtypeStruct(tail_shape, F32))],
                      extra_scratch=[pltpu.VMEM(tail_shape, F32)])


HGRN_LEVELS = 6
HGRN_DIAG = HGRN_LEVELS


def _hgrn_level_table():
    r = np.arange(STACK)
    t, hd = r % CHUNK, r // CHUNK
    x = t[:, None] ^ t[None, :]
    lv = np.floor(np.log2(np.maximum(x, 1))).astype(np.int32)
    valid = (hd[:, None] == hd[None, :]) & (t[:, None] > t[None, :])
    lv = np.where(valid, lv, -1)
    lv = np.where(r[:, None] == r[None, :], HGRN_DIAG, lv)
    return jnp.asarray(lv, dtype=jnp.int32)


def _hgrn_kernel(lv_ref, g_ref, st0_ref, q_ref, f_ref, v_ref, gate_ref, o_ref, stout_ref, st_ref, *, nb, nc):
    @pl.when(pl.program_id(0) == 0)
    def _():
        st_ref[...] = st0_ref[...]

    bsz = nc * nb
    q = _load_chunks(q_ref, nc)
    f = _load_chunks(f_ref, nc)
    v = _load_chunks(v_ref, nc)
    gate = _load_chunks(gate_ref, nc)

    per_seq = lambda a: a.reshape(bsz, CHUNK, WIDTH)
    trow = lax.broadcasted_iota(jnp.int32, (1, CHUNK, WIDTH), 1)
    b = jnp.log(f)
    s = 1
    while s < CHUNK:
        b = (per_seq(b) + jnp.where(trow >= s, per_seq(_roll_rows(b, s)), 0.0)).reshape(b.shape)
        s *= 2

    qs, ks, vs, bs = (_stack_heads(a, bsz) for a in (q, 1.0 - f, v, b))
    lv = lv_ref[...][None]
    to3 = functools.partial(_to_stacks, bsz=bsz)
    to_g = functools.partial(_to_heads, bsz=bsz)
    roll_g = lambda a, shift: to_g(_roll_rows(a.reshape(qs.shape), shift))
    t = lax.broadcasted_iota(jnp.int32, (1, CHUNK, HEAD_DIM), 1)
    qs_g, ks_g, bs_g = to_g(qs), to_g(ks), to_g(bs)

    attn = jnp.where(lv == HGRN_DIAG, _bdot_nt(to3(qs), to3(ks)), 0.0)
    bref = bs_g
    bnext = roll_g(bs_g, -1)
    for li in range(HGRN_LEVELS):
        m = 1 << li
        if li > 0:
            half = m // 2
            upper = (t & (m - 1)) >= half
            bref = jnp.where(upper, roll_g(bref, half), bref)
            bnext = jnp.where(upper, bnext, roll_g(bnext, -half))
        odd = ((t >> li) & 1) == 1
        both = jnp.where(odd, qs_g, ks_g) * jnp.exp(jnp.where(odd, bs_g - bref, bnext - bs_g))
        both = to3(both.reshape(qs.shape)).astype(BF16)
        attn = jnp.where(lv == li, _bdot_nt(both, both), attn)

    o = _bdot(attn, to3(vs))

    n_g = nb * HEADS
    st = st_ref[...]
    qe = qs_g * jnp.exp(bs_g)
    b_end = bs_g[:, CHUNK - 1:CHUNK, :]
    ke = ks_g * jnp.exp(b_end - bs_g)
    vs_g = to_g(vs)
    o_inter = []
    for j in range(nc):
        sl = slice(j * n_g, (j + 1) * n_g)
        o_inter.append(_bdot_nt(qe[sl], st))
        st = st * jnp.exp(b_end[sl]) + _bdot_tn(vs_g[sl], ke[sl])
    st_ref[...] = st
    stout_ref[...] = st
    o = (to_g(o) + jnp.concatenate(o_inter, axis=0)).reshape(qs.shape)

    o = o * lax.rsqrt(jnp.mean(o * o, axis=-1, keepdims=True) + EPS) * g_ref[...]
    o = o * _stack_heads(gate, bsz)
    _store_chunks(o_ref, _unstack_heads(o, bsz), nb, nc)


def _gdn_cumsum_matrix():
    r = np.arange(STACK)
    u = (r[:, None] // CHUNK == r[None, :] // CHUNK) & (r[:, None] <= r[None, :])
    return jnp.asarray(u, dtype=BF16)


def _gdn_kernel(u_ref, alog_ref, dt_ref, g_ref, s_ref, st0_ref, q_ref, k_ref, v_ref, z_ref,
                o_ref, stout_ref, st_ref, *, nb, nc):
    @pl.when(pl.program_id(0) == 0)
    def _():
        st_ref[...] = st0_ref[...]

    bsz = nc * nb
    qs = _stack_heads(_load_chunks(q_ref, nc), bsz)
    ks = _stack_heads(_load_chunks(k_ref, nc), bsz)
    vs = _stack_heads(_load_chunks(v_ref, nc), bsz)
    gate = _load_chunks(z_ref, nc)
    qs = qs * lax.rsqrt(jnp.sum(qs * qs, axis=-1, keepdims=True) + EPS) * (HEAD_DIM ** -0.5)
    ks = ks * lax.rsqrt(jnp.sum(ks * ks, axis=-1, keepdims=True) + EPS)
    to3 = functools.partial(_to_stacks, bsz=bsz)
    qs, ks, vs = to3(qs), to3(ks), to3(vs)
    n_st = bsz * N_STACKS

    srow = _chunks_first_scalars(s_ref[...], nc).reshape(n_st, 2, STACK)
    per_stack = lambda ref: jnp.broadcast_to(ref[...][None], (bsz, N_STACKS, 1, STACK)).reshape(n_st, 1, STACK)
    beta_row = _sigmoid(srow[:, 0:1, :])
    g_row = -jnp.exp(per_stack(alog_ref)) * _softplus(srow[:, 1:2, :] + per_stack(dt_ref))
    n8 = n_st * SUBLANES
    g8 = jnp.broadcast_to(g_row, (n_st, SUBLANES, STACK)).reshape(n8, STACK)
    pieces = jnp.concatenate(_split3(g8), axis=0)
    cum = jnp.dot(pieces, u_ref[...], preferred_element_type=F32)
    cum = cum[0:n8] + cum[n8:2 * n8] + cum[2 * n8:]
    cum_row = cum.reshape(n_st, SUBLANES, STACK)[:, 0:1, :]

    r = lax.broadcasted_iota(jnp.int32, (STACK, STACK), 0)
    cc = lax.broadcasted_iota(jnp.int32, (STACK, STACK), 1)
    eye = (r == cc)[None]
    cum_col = jnp.sum(jnp.where(eye, cum_row, 0.0), axis=2, keepdims=True)
    beta_col = jnp.sum(jnp.where(eye, beta_row, 0.0), axis=2, keepdims=True)
    same = (r >> 6) == (cc >> 6)
    lower = (same & (r >= cc))[None]
    strict = (same & (r > cc))[None]
    ratio = jnp.exp(jnp.minimum(cum_col - cum_row, 0.0))

    kkt = _bdot_nt(ks, ks)
    qkt = _bdot_nt(qs, ks)
    lm = jnp.where(strict, beta_col * ratio * kkt, 0.0)

    ident = eye.astype(F32)
    l0 = jnp.where(((r >> 3) == (cc >> 3))[None], lm, 0.0)
    l2 = _bdot(l0, l0)
    l4 = _bdot(l2, l2)
    xinv = _bdot(_bdot(ident - l0, ident + l2), ident + l4)
    for lvl in (3, 4, 5):
        off = (((r >> (lvl + 1)) == (cc >> (lvl + 1))) & ((r >> lvl) != (cc >> lvl)))[None]
        xinv = xinv - _bdot(_bdot(xinv, jnp.where(off, lm, 0.0)), xinv)

    decay_col = jnp.exp(cum_col)
    rhs = jnp.concatenate([beta_col * vs, (beta_col * decay_col) * ks], axis=2)
    sol = _bdot(xinv, rhs)

    to_g = functools.partial(_to_heads, bsz=bsz)
    n_g = nb * HEADS
    st = st_ref[...]
    u0_g, w_g, q_g = to_g(sol[:, :, :HEAD_DIM]), to_g(sol[:, :, HEAD_DIM:]), to_g(qs)
    cum_g = to_g(cum_col)
    cum_end = cum_g[:, CHUNK - 1:CHUNK, :]
    kdec = to_g(ks) * jnp.exp(cum_end - cum_g)
    decay_g = to_g(decay_col)
    us, o_inter = [], []
    for j in range(nc):
        sl = slice(j * n_g, (j + 1) * n_g)
        u_j = u0_g[sl] - _bdot(w_g[sl], st)
        us.append(u_j)
        o_inter.append(decay_g[sl] * _bdot(q_g[sl], st))
        st = jnp.exp(cum_end[sl]) * st + _bdot_tn(kdec[sl], u_j)
    st_ref[...] = st
    stout_ref[...] = st
    attn = jnp.where(lower, qkt * ratio, 0.0)
    u = jnp.concatenate(us, axis=0).reshape(bsz * HEADS * CHUNK, HEAD_DIM)
    o = jnp.concatenate(o_inter, axis=0) + to_g(_bdot(attn, to3(u)))
    o = o.reshape(bsz * HEADS * CHUNK, HEAD_DIM)

    o = o * lax.rsqrt(jnp.mean(o * o, axis=-1, keepdims=True) + EPS) * g_ref[...]
    o = o * _stack_heads(gate, bsz)
    _store_chunks(o_ref, _unstack_heads(o, bsz), nb, nc)


def _recurrence_call(kernel_fn, name, proj, state0, operands, specs, nc):
    nb, length, _ = proj.shape
    rows = nc * CHUNK
    state_shape = (nb * HEADS, HEAD_DIM, HEAD_DIM)
    state_spec = pl.BlockSpec(state_shape, lambda c: (0, 0, 0))
    chunk_specs = [pl.BlockSpec((nb, rows, WIDTH), functools.partial(lambda c, j: (0, c, j), j=j))
                   for j in range(4)]
    return pl.pallas_call(
        functools.partial(kernel_fn, nb=nb, nc=nc),
        grid=(length // rows,),
        in_specs=specs + [state_spec] + chunk_specs,
        out_specs=[pl.BlockSpec((nb, rows, WIDTH), lambda c: (0, c, 0)), state_spec],
        out_shape=[jax.ShapeDtypeStruct((nb, length, WIDTH), BF16),
                   jax.ShapeDtypeStruct(state_shape, F32)],
        scratch_shapes=[pltpu.VMEM(state_shape, F32)],
        compiler_params=pltpu.CompilerParams(
            dimension_semantics=("arbitrary",), vmem_limit_bytes=VMEM_LIMIT),
        name=name,
    )(*operands, state0, *([proj] * 4))


def _hgrn(proj, state0, norm_g, nc):
    const = lambda c: (0, 0)
    specs = [pl.BlockSpec((STACK, STACK), const), pl.BlockSpec((1, HEAD_DIM), const)]
    return _recurrence_call(_hgrn_kernel, "hgrn", proj, state0, (_hgrn_level_table(), norm_g), specs, nc)


def _gdn(proj, state0, scal, a_log_row, dt_row, norm_g, nc):
    nb = proj.shape[0]
    const = lambda c: (0, 0)
    const3 = lambda c: (0, 0, 0)
    specs = [pl.BlockSpec((STACK, STACK), const),
             pl.BlockSpec((N_STACKS, 1, STACK), const3),
             pl.BlockSpec((N_STACKS, 1, STACK), const3),
             pl.BlockSpec((1, HEAD_DIM), const),
             pl.BlockSpec((nb, nc, N_STACKS, 2, STACK), lambda c: (0, c, 0, 0, 0))]
    return _recurrence_call(_gdn_kernel, "gdn", proj, state0,
                            (_gdn_cumsum_matrix(), a_log_row, dt_row, norm_g, scal), specs, nc)


ROUTE_LANE0 = N_GROUPS
ROUTE_NEG = -1e30


def _route_tile(lg, ltri, carry):
    lane = lax.broadcasted_iota(jnp.int32, lg.shape, 1).astype(F32)
    first = lambda mask: jnp.min(jnp.where(mask, lane, float(LANES)), axis=1, keepdims=True)
    top = lambda mask: jnp.max(jnp.where(mask, lg, ROUTE_NEG), axis=1, keepdims=True)

    is_g = lane < N_GROUPS
    gmax = top(is_g)
    grp = first(is_g & (lg == gmax))
    p_grp = 1.0 / jnp.sum(jnp.where(is_g, jnp.exp(lg - gmax), 0.0), axis=1, keepdims=True)

    lo = ROUTE_LANE0 + grp * EXPERTS_PER_GROUP
    in_grp = (lane >= lo) & (lane < lo + EXPERTS_PER_GROUP)
    m1 = top(in_grp)
    i1 = first(in_grp & (lg == m1))
    rest = in_grp & (lane != i1)
    m2 = top(rest)
    i2 = first(rest & (lg == m2))
    r = jnp.exp(m2 - m1)
    w0 = p_grp / (1.0 + r)
    w1 = p_grp * r / (1.0 + r)

    hot0 = (lane == i1).astype(F32)
    hot1 = (lane == i2).astype(F32)
    both = hot0 + hot1
    before = jnp.dot(ltri, both.astype(BF16), preferred_element_type=F32) + carry
    rank0 = jnp.sum(hot0 * before, axis=1, keepdims=True)
    rank1 = jnp.sum(hot1 * before, axis=1, keepdims=True)
    cols = (w0, w1, i1 - ROUTE_LANE0, i2 - ROUTE_LANE0, rank0, rank1)
    info = jnp.zeros_like(lg)
    for k, col in enumerate(cols):
        info = jnp.where(lane == k, col, info)
    return info, carry + jnp.sum(both, axis=0, keepdims=True)


def _merge_kernel(oa_ref, ob_ref, x_ref, mg_ref, wgate_ref, hgup_ref, gdup_ref, wout_ref,
                  ng_ref, rwh_ref, rwl_ref, rb_ref, ltri_ref, h2_ref, xn_ref, info_ref, ids_ref, cnt_ref,
                  wgate_bf, carry_ref):
    @pl.when(pl.program_id(0) == 0)
    def _():
        carry_ref[...] = jnp.zeros_like(carry_ref)
        _cast_rows(wgate_ref, wgate_bf)

    gates = _sigmoid(jnp.dot(_normed_bf16(x_ref, mg_ref), wgate_bf[...], preferred_element_type=F32))
    ua = jnp.dot(oa_ref[...], hgup_ref[...], preferred_element_type=F32)
    ub = jnp.dot(ob_ref[...], gdup_ref[...], preferred_element_type=F32)
    merged = gates[:, :D_MODEL] * ua + gates[:, D_MODEL:] * ub
    h2 = x_ref[...] + jnp.dot(merged.astype(BF16), wout_ref[...], preferred_element_type=F32)
    h2_ref[...] = h2
    xn = h2 * lax.rsqrt(jnp.mean(h2 * h2, axis=-1, keepdims=True) + EPS) * ng_ref[...]
    xn_ref[...] = _pack_halves(xn)
    xh = xn.astype(BF16)
    xl = (xn - xh.astype(F32)).astype(BF16)
    wh, wl = rwh_ref[...], rwl_ref[...]
    lg = (jnp.dot(xh, wh, preferred_element_type=F32)
          + jnp.dot(xh, wl, preferred_element_type=F32)
          + jnp.dot(xl, wh, preferred_element_type=F32)) + rb_ref[...]
    info, carry = _route_tile(lg, ltri_ref[...], carry_ref[...])
    info_ref[...] = info
    ids_ref[...] = info.T[:SUBLANES].astype(jnp.int32)
    carry_ref[...] = carry
    cnt_ref[...] = carry


def _merge(o_a, o_b, x2d, mix_g, w_gates, hg_up, gd_up, w_out, norm_g, rw_hi, rw_lo, rb):
    t = x2d.shape[0]
    tm = TOKEN_TM
    row = lambda i: (i, 0)
    const = lambda i: (0, 0)
    ltri = jnp.asarray(np.tril(np.ones((tm, tm), np.float32), -1), dtype=BF16)
    return pl.pallas_call(
        _merge_kernel,
        grid=(t // tm,),
        in_specs=[pl.BlockSpec((tm, WIDTH), row),
                  pl.BlockSpec((tm, WIDTH), row),
                  pl.BlockSpec((tm, D_MODEL), row),
                  pl.BlockSpec((1, D_MODEL), const),
                  pl.BlockSpec((D_MODEL, 2 * D_MODEL), const, pipeline_mode=pl.Buffered(1)),
                  pl.BlockSpec((WIDTH, D_MODEL), const),
                  pl.BlockSpec((WIDTH, D_MODEL), const),
                  pl.BlockSpec((D_MODEL, D_MODEL), const),
                  pl.BlockSpec((1, D_MODEL), const),
                  pl.BlockSpec((D_MODEL, LANES), const),
                  pl.BlockSpec((D_MODEL, LANES), const),
                  pl.BlockSpec((1, LANES), const),
                  pl.BlockSpec((tm, tm), const)],
        out_specs=[pl.BlockSpec((tm, D_MODEL), row),
                   pl.BlockSpec((tm, HALF), row),
                   pl.BlockSpec((tm, LANES), row),
                   pl.BlockSpec((SUBLANES, tm), lambda i: (0, i)),
                   pl.BlockSpec((1, LANES), const)],
        out_shape=[jax.ShapeDtypeStruct((t, D_MODEL), F32),
                   jax.ShapeDtypeStruct((t, HALF), U32),
                   jax.ShapeDtypeStruct((t, LANES), F32),
                   jax.ShapeDtypeStruct((SUBLANES, t), jnp.int32),
                   jax.ShapeDtypeStruct((1, LANES), F32)],
        scratch_shapes=[pltpu.VMEM((D_MODEL, 2 * D_MODEL), BF16), pltpu.VMEM((1, LANES), F32)],
        compiler_params=pltpu.CompilerParams(
            dimension_semantics=("arbitrary",), vmem_limit_bytes=VMEM_LIMIT),
        name="merge",
    )(o_a, o_b, x2d, mix_g, w_gates, hg_up, gd_up, w_out, norm_g, rw_hi, rw_lo, rb, ltri)


def _sc_gather(table, idx):
    n_idx = idx.shape[0]
    cols = table.shape[1]
    per_worker = n_idx // SC_WORKERS
    assert n_idx % (SC_WORKERS * SC_CHUNK) == 0
    mesh = plsc.VectorSubcoreMesh(core_axis_name="c", subcore_axis_name="s")

    @functools.partial(
        pl.kernel, mesh=mesh,
        out_type=jax.ShapeDtypeStruct((n_idx, cols), table.dtype),
        scratch_types=[pltpu.VMEM((SC_CHUNK,), jnp.int32),
                       pltpu.VMEM((SC_CHUNK, cols), table.dtype),
                       pltpu.SemaphoreType.DMA],
    )
    def gather(table_hbm, idx_hbm, out_hbm, idx_v, rows_v, sem):
        worker = lax.axis_index("s") * SC_CORES + lax.axis_index("c")
        base = worker * per_worker

        @pl.loop(0, per_worker // SC_CHUNK)
        def _(c):
            off = pl.multiple_of(base + c * SC_CHUNK, SC_CHUNK)
            pltpu.sync_copy(idx_hbm.at[pl.ds(off, SC_CHUNK)], idx_v)
            pltpu.async_copy(table_hbm.at[idx_v], rows_v, sem).wait()
            pltpu.sync_copy(rows_v, out_hbm.at[pl.ds(off, SC_CHUNK)])

    return gather(table, idx)


def _sc_dispatch(rows, dest, n_out):
    t, cols = rows.shape
    per_worker = t // SC_WORKERS
    assert t % (SC_WORKERS * SC_CHUNK) == 0 and dest.shape[0] == 2 * t
    mesh = plsc.VectorSubcoreMesh(core_axis_name="c", subcore_axis_name="s")

    @functools.partial(
        pl.kernel, mesh=mesh,
        out_type=jax.ShapeDtypeStruct((n_out, cols), rows.dtype),
        scratch_types=[pltpu.VMEM((SC_CHUNK,), jnp.int32),
                       pltpu.VMEM((SC_CHUNK, cols), rows.dtype)],
    )
    def dispatch(rows_hbm, dest_hbm, out_hbm, idx_v, rows_v):
        worker = lax.axis_index("s") * SC_CORES + lax.axis_index("c")
        base = worker * per_worker

        @pl.loop(0, per_worker // SC_CHUNK)
        def _(c):
            off = pl.multiple_of(base + c * SC_CHUNK, SC_CHUNK)
            pltpu.sync_copy(rows_hbm.at[pl.ds(off, SC_CHUNK)], rows_v)
            for slot in range(2):
                pltpu.sync_copy(dest_hbm.at[pl.ds(slot * t + off, SC_CHUNK)], idx_v)
                pltpu.sync_copy(rows_v, out_hbm.at[idx_v])

    return dispatch(rows, dest)


def _moe_kernel(be_ref, nu_ref, x_ref, wg_ref, wu_ref, wd_ref, y_ref, wg_bf, wu_bf, wd_bf):
    i = pl.program_id(0)

    @pl.when((i == 0) | (be_ref[i] != be_ref[jnp.maximum(i - 1, 0)]))
    def _():
        _cast_rows(wg_ref, wg_bf)
        _cast_rows(wu_ref, wu_bf)
        _cast_rows(wd_ref, wd_bf)

    @pl.when(i < nu_ref[0])
    def _():
        xb = _unpack_halves(x_ref[...]).astype(BF16)
        a = jnp.dot(xb, wg_bf[...], preferred_element_type=F32)
        u = jnp.dot(xb, wu_bf[...], preferred_element_type=F32)
        y = jnp.dot((_silu(a) * u).astype(BF16), wd_bf[...], preferred_element_type=F32)
        y_ref[...] = _pack_halves(y)

    @pl.when(i >= nu_ref[0])
    def _():
        y_ref[...] = jnp.zeros_like(y_ref)


def _moe(blk_expert, n_used, x_sorted, w_gate, w_up, w_down):
    n_blocks = blk_expert.shape[0]
    wspec = lambda shape: pl.BlockSpec((None,) + shape, lambda i, be, nu: (be[i], 0, 0))
    rows = pl.BlockSpec((MOE_ROWS, HALF), lambda i, be, nu: (i, 0))
    grid_spec = pltpu.PrefetchScalarGridSpec(
        num_scalar_prefetch=2,
        grid=(n_blocks,),
        in_specs=[rows, wspec((D_MODEL, D_FF)), wspec((D_MODEL, D_FF)), wspec((D_FF, D_MODEL))],
        out_specs=rows,
        scratch_shapes=[pltpu.VMEM((D_MODEL, D_FF), BF16), pltpu.VMEM((D_MODEL, D_FF), BF16),
                        pltpu.VMEM((D_FF, D_MODEL), BF16)],
    )
    return pl.pallas_call(
        _moe_kernel,
        grid_spec=grid_spec,
        out_shape=jax.ShapeDtypeStruct(x_sorted.shape, U32),
        compiler_params=pltpu.CompilerParams(
            dimension_semantics=("arbitrary",), vmem_limit_bytes=VMEM_LIMIT),
        name="moe",
    )(blk_expert, n_used, x_sorted, w_gate, w_up, w_down)


def _combine_kernel(h2_ref, rw_ref, g_ref, y0_ref, y1_ref, o_ref):
    rw = rw_ref[...]
    h = h2_ref[...] + rw[:, 0:1] * _unpack_halves(y0_ref[...]) + rw[:, 1:2] * _unpack_halves(y1_ref[...])
    o_ref[...] = h * lax.rsqrt(jnp.mean(h * h, axis=-1, keepdims=True) + EPS) * g_ref[...]


def _combine(h2, rweights, final_g, y):
    t = h2.shape[0]
    tm = TOKEN_TM
    row = lambda i: (i, 0)
    return pl.pallas_call(
        _combine_kernel,
        grid=(t // tm,),
        in_specs=[pl.BlockSpec((tm, D_MODEL), row),
                  pl.BlockSpec((tm, LANES), row),
                  pl.BlockSpec((1, D_MODEL), lambda i: (0, 0)),
                  pl.BlockSpec((tm, HALF), row),
                  pl.BlockSpec((tm, HALF), lambda i: (i + t // tm, 0))],
        out_specs=pl.BlockSpec((tm, D_MODEL), row),
        out_shape=jax.ShapeDtypeStruct((t, D_MODEL), F32),
        compiler_params=pltpu.CompilerParams(
            dimension_semantics=("arbitrary",), vmem_limit_bytes=VMEM_LIMIT),
        name="combine",
    )(h2, rweights, final_g, y, y)


def _block_layout(ids, counts_row, t):
    n_blocks = 2 * t // MOE_ROWS + N_EXPERTS
    counts = counts_row[0, ROUTE_LANE0:ROUTE_LANE0 + N_EXPERTS].astype(jnp.int32)
    padded = ((counts + MOE_ROWS - 1) // MOE_ROWS) * MOE_ROWS
    pend = jnp.cumsum(padded)
    pstart = pend - padded
    blk_first = jnp.arange(n_blocks, dtype=jnp.int32) * MOE_ROWS
    blk_expert = jnp.minimum(jnp.sum((pend[None, :] <= blk_first[:, None]).astype(jnp.int32), axis=1),
                             N_EXPERTS - 1).astype(jnp.int32)
    n_used = (pend[-1] // MOE_ROWS).astype(jnp.int32).reshape(1)
    expert, rank = ids[2:4], ids[4:6]
    experts = jnp.arange(N_EXPERTS, dtype=jnp.int32)
    first_row = jnp.sum(jnp.where(expert[:, :, None] == experts, pstart, 0), axis=-1)
    dest = (first_row + rank).reshape(2 * t)
    return dest, blk_expert, n_used, n_blocks * MOE_ROWS


def kernel(x, meta_tokens, hg_lb_logits, norm_mix_g, w_in, gd_conv_w, gd_A_log, gd_dt_bias, hg_norm_g, gd_norm_g, hg_up, gd_up, w_out, norm_ffn_g, router_group_w, router_group_b, router_expert_w, router_expert_b, w_gate, w_up, w_down, final_norm_g):
    bsz, seq, d = x.shape
    t = bsz * seq
    x2d = x.reshape(t, d)

    lb = jnp.cumsum(jax.nn.softmax(hg_lb_logits.astype(F32), axis=0), axis=0)[0].reshape(1, WIDTH)
    w_all = w_in.astype(F32)
    w_gates = w_all[0, :, W_SMALL + 2 * HEADS:]
    g_mix = norm_mix_g[0].reshape(1, d)
    meta_blk = jnp.concatenate([jnp.zeros((CHUNK - N_META, d), F32), meta_tokens.astype(F32)], axis=0)

    conv_w = gd_conv_w[0].astype(F32)
    no_history = jnp.zeros((SUBLANES, 3 * WIDTH), F32)
    proj, proj_meta = {}, {}
    proj["hgrn"] = _norm_proj_hgrn(x2d, g_mix, lb, w_all, PROJ_TM)
    proj_meta["hgrn"] = _norm_proj_hgrn(meta_blk, g_mix, lb, w_all, CHUNK)
    proj_meta["gdn"], meta_tail = _norm_proj_conv(meta_blk, g_mix, w_all, conv_w, no_history, CHUNK, CHUNK)
    proj["gdn"], _ = _norm_proj_conv(x2d, g_mix, w_all, conv_w, meta_tail, PROJ_TM, seq)

    def scalar_rows(p, nb, nc):
        s = p[:, 4 * WIDTH:4 * WIDTH + 2 * HEADS].reshape(nb, nc, CHUNK, 2, N_STACKS, HEADS_PER_STACK)
        return s.transpose(0, 1, 4, 3, 5, 2).reshape(nb, nc, N_STACKS, 2, STACK)

    a_log_row = jnp.repeat(gd_A_log[0].astype(F32), CHUNK).reshape(N_STACKS, 1, STACK)
    dt_row = jnp.repeat(gd_dt_bias[0].astype(F32), CHUNK).reshape(N_STACKS, 1, STACK)
    hg_g, gd_g = hg_norm_g[0].reshape(1, HEAD_DIM), gd_norm_g[0].reshape(1, HEAD_DIM)

    zero_state = jnp.zeros((HEADS, HEAD_DIM, HEAD_DIM), F32)
    per_seq = lambda st: jnp.tile(st, (bsz, 1, 1))
    _, hg_state = _hgrn(proj_meta["hgrn"][None], zero_state, hg_g, 1)
    o_a, _ = _hgrn(proj["hgrn"].reshape(bsz, seq, -1), per_seq(hg_state), hg_g, CHUNKS_PER_STEP)
    _, gd_state = _gdn(proj_meta["gdn"][None], zero_state, scalar_rows(proj_meta["gdn"], 1, 1),
                       a_log_row, dt_row, gd_g, 1)
    o_b, _ = _gdn(proj["gdn"].reshape(bsz, seq, -1), per_seq(gd_state),
                  scalar_rows(proj["gdn"], bsz, seq // CHUNK), a_log_row, dt_row, gd_g, CHUNKS_PER_STEP)

    rw = jnp.zeros((d, LANES), F32)
    rw = rw.at[:, :N_GROUPS].set(router_group_w[0]).at[:, N_GROUPS:N_GROUPS + N_EXPERTS].set(router_expert_w[0])
    rw_hi = rw.astype(BF16)
    rw_lo = (rw - rw_hi.astype(F32)).astype(BF16)
    rb = jnp.zeros((1, LANES), F32)
    rb = rb.at[0, :N_GROUPS].set(router_group_b[0]).at[0, N_GROUPS:N_GROUPS + N_EXPERTS].set(router_expert_b[0])
    h2, xn, info, ids, counts = _merge(o_a.reshape(t, WIDTH), o_b.reshape(t, WIDTH), x2d, g_mix, w_gates,
                                  hg_up[0].astype(BF16), gd_up[0].astype(BF16), w_out[0].astype(BF16),
                                  norm_ffn_g[0].reshape(1, d), rw_hi, rw_lo, rb)

    dest, blk_expert, n_used, n_rows = _block_layout(ids, counts, t)
    x_sorted = _sc_dispatch(xn, dest, n_rows)
    y_sorted = _moe(blk_expert, n_used, x_sorted,
                    w_gate[0].astype(F32), w_up[0].astype(F32), w_down[0].astype(F32))
    y_tok = _sc_gather(y_sorted, dest)
    out = _combine(h2, info, final_norm_g.reshape(1, d), y_tok)
    return out.reshape(bsz, seq, d)
```

```python
import functools

import numpy as np
import jax
import jax.numpy as jnp
from jax import lax
from jax.experimental import pallas as pl
from jax.experimental.pallas import tpu as pltpu
from jax.experimental.pallas import tpu_sc as plsc

F32 = jnp.float32
BF16 = jnp.bfloat16
U32 = jnp.uint32

D_MODEL = 1024
N_META = 16
CHUNK = 64
EPS = 1e-6
HEADS = 4
HEAD_DIM = 128
WIDTH = HEADS * HEAD_DIM
HEADS_PER_STACK = 2
STACK = HEADS_PER_STACK * CHUNK
N_STACKS = HEADS // HEADS_PER_STACK
CONV_W = 4
N_GROUPS = 4
EXPERTS_PER_GROUP = 8
N_EXPERTS = N_GROUPS * EXPERTS_PER_GROUP
D_FF = 512
LANES = 128
SUBLANES = 8
HALF = D_MODEL // 2
MOE_ROWS = 512
CAST_SLAB = 128
SC_CORES, SC_SUBCORES = 2, 16
SC_WORKERS = SC_CORES * SC_SUBCORES
SC_CHUNK = 128
PROJ_TM = 1024
W_HGRN, W_GDN, W_SMALL = 0, 4 * WIDTH, 8 * WIDTH
CHUNKS_PER_STEP = 2
TOKEN_TM = 512
VMEM_LIMIT = 48 * 1024 * 1024


def _bdot(a, b):
    return lax.dot_general(a.astype(BF16), b.astype(BF16), (((2,), (1,)), ((0,), (0,))),
                           preferred_element_type=F32)


def _bdot_nt(a, b):
    return lax.dot_general(a.astype(BF16), b.astype(BF16), (((2,), (2,)), ((0,), (0,))),
                           preferred_element_type=F32)


def _bdot_tn(a, b):
    return lax.dot_general(a.astype(BF16), b.astype(BF16), (((1,), (1,)), ((0,), (0,))),
                           preferred_element_type=F32)


def _split3(x):
    hi = x.astype(BF16)
    r1 = x - hi.astype(F32)
    mid = r1.astype(BF16)
    lo = (r1 - mid.astype(F32)).astype(BF16)
    return hi, mid, lo


def _sigmoid(x):
    return 1.0 / (1.0 + jnp.exp(-x))


def _silu(x):
    return x * _sigmoid(x)


def _softplus(x):
    return jnp.maximum(x, 0.0) + jnp.log(1.0 + jnp.exp(-jnp.abs(x)))


def _stack_heads(x, bsz):
    x3 = x.reshape(bsz, CHUNK, WIDTH)
    y = jnp.concatenate([x3[:, :, h * HEAD_DIM:(h + 1) * HEAD_DIM] for h in range(HEADS)], axis=1)
    return y.reshape(bsz * HEADS * CHUNK, HEAD_DIM)


def _unstack_heads(x, bsz):
    x3 = x.reshape(bsz, HEADS * CHUNK, HEAD_DIM)
    return jnp.concatenate([x3[:, h * CHUNK:(h + 1) * CHUNK, :] for h in range(HEADS)], axis=2)


def _to_stacks(a, bsz):
    return a.reshape(bsz * N_STACKS, STACK, a.shape[-1])


def _to_heads(a, bsz):
    return a.reshape(bsz * HEADS, CHUNK, a.shape[-1])


def _roll_rows(x, shift):
    n = x.shape[0]
    return pltpu.roll(x, shift % n, axis=0)


def _chunks_first(x, nc):
    return jnp.concatenate([x[:, j * CHUNK:(j + 1) * CHUNK] for j in range(nc)], axis=0)


def _chunks_first_scalars(s, nc):
    return jnp.concatenate([s[:, j] for j in range(nc)], axis=0)


def _load_chunks(ref, nc):
    x = _chunks_first(ref[...], nc)
    return x.reshape(x.shape[0] * CHUNK, WIDTH)


def _store_chunks(ref, x, nb, nc):
    ref[...] = jnp.concatenate([x[j * nb:(j + 1) * nb] for j in range(nc)], axis=1).astype(ref.dtype)


def _cast_rows(src_ref, dst_ref, col0=0):
    slab = CAST_SLAB
    rows, cols = src_ref.shape
    assert rows % slab == 0

    def body(k, carry):
        r = pl.multiple_of(k * slab, slab)
        dst_ref[pl.ds(r, slab), col0:col0 + cols] = src_ref[pl.ds(r, slab), :].astype(dst_ref.dtype)
        return carry

    lax.fori_loop(0, rows // slab, body, 0)


def _pack_halves(x):
    lo = lax.bitcast_convert_type(x[:, :HALF].astype(BF16).astype(F32), U32)
    hi = lax.bitcast_convert_type(x[:, HALF:].astype(BF16).astype(F32), U32)
    return (lo >> 16) | (hi & jnp.uint32(0xFFFF0000))


def _unpack_halves(p):
    lo = lax.bitcast_convert_type(p << 16, F32)
    hi = lax.bitcast_convert_type(p & jnp.uint32(0xFFFF0000), F32)
    return jnp.concatenate([lo, hi], axis=1)


def _normed_bf16(x_ref, g_ref):
    x = x_ref[...]
    return (x * lax.rsqrt(jnp.mean(x * x, axis=-1, keepdims=True) + EPS) * g_ref[...]).astype(BF16)


def _norm_proj_hgrn_kernel(x_ref, g_ref, lb_ref, w_ref, o_ref, wbf_ref):
    @pl.when(pl.program_id(0) == 0)
    def _():
        _cast_rows(w_ref, wbf_ref)

    acc = jnp.dot(_normed_bf16(x_ref, g_ref), wbf_ref[...], preferred_element_type=F32)
    w1, w2, w3 = WIDTH, 2 * WIDTH, 3 * WIDTH
    lb = lb_ref[...]
    o_ref[:, :w1] = _silu(acc[:, :w1])
    o_ref[:, w1:w2] = lb + (1.0 - lb) * _sigmoid(acc[:, w1:w2])
    o_ref[:, w2:w3] = acc[:, w2:w3]
    o_ref[:, w3:] = _silu(acc[:, w3:])


def _norm_proj_conv_kernel(x_ref, g_ref, w_ref, ws_ref, cw_ref, hist_ref, o_ref, tail_ref, wbf_ref, carry_ref,
                           *, steps_per_seq):
    i = pl.program_id(0)

    @pl.when(i == 0)
    def _():
        _cast_rows(w_ref, wbf_ref)
        _cast_rows(ws_ref, wbf_ref, 4 * WIDTH)

    xn = _normed_bf16(x_ref, g_ref)
    w3, w4 = 3 * WIDTH, 4 * WIDTH
    raw = jnp.dot(xn, wbf_ref[:, :w3], preferred_element_type=F32)
    rest = jnp.dot(xn, wbf_ref[:, w3:], preferred_element_type=F32)
    rows = raw.shape[0]
    hist = jnp.where(i % steps_per_seq == 0, hist_ref[...], carry_ref[...])
    cw = cw_ref[...]
    tap = lambda d: cw[CONV_W - 1 - d:CONV_W - d, :]
    body = raw * tap(0)
    head = raw[:SUBLANES]
    r8 = lax.broadcasted_iota(jnp.int32, head.shape, 0)
    first = head * tap(0)
    for d in range(1, CONV_W):
        body = body + _roll_rows(raw, d) * tap(d)
        first = first + jnp.where(r8 >= d, _roll_rows(head, d), _roll_rows(hist, d)) * tap(d)
    o_ref[:SUBLANES, :w3] = _silu(first)
    o_ref[SUBLANES:, :w3] = _silu(body[SUBLANES:])
    o_ref[:, w3:w4] = _silu(rest[:, :WIDTH])
    o_ref[:, w4:] = rest[:, WIDTH:]
    tail = raw[rows - SUBLANES:]
    carry_ref[...] = tail
    tail_ref[...] = tail


def _proj_call(kernel_fn, name, x2d, tm, n, operands, specs, extra_out=(), extra_scratch=()):
    m, k = x2d.shape
    tm = min(tm, m)
    out_specs = [pl.BlockSpec((tm, n), lambda i: (i, 0))] + [s for s, _ in extra_out]
    out_shape = [jax.ShapeDtypeStruct((m, n), F32)] + [s for _, s in extra_out]
    return pl.pallas_call(
        kernel_fn,
        grid=(m // tm,),
        in_specs=[pl.BlockSpec((tm, k), lambda i: (i, 0))] + specs,
        out_specs=out_specs,
        out_shape=out_shape,
        scratch_shapes=[pltpu.VMEM((k, n), BF16)] + list(extra_scratch),
        compiler_params=pltpu.CompilerParams(
            dimension_semantics=("arbitrary",), vmem_limit_bytes=VMEM_LIMIT),
        name=name,
    )(x2d, *operands)


def _weight_spec(k, n, col0):
    assert col0 % n == 0
    return pl.BlockSpec((None, k, n), lambda i: (0, 0, col0 // n), pipeline_mode=pl.Buffered(1))


def _norm_proj_hgrn(x2d, g, lb, w, tm):
    k, n = w.shape[1], 4 * WIDTH
    const = lambda i: (0, 0)
    specs = [pl.BlockSpec((1, k), const), pl.BlockSpec((1, WIDTH), const), _weight_spec(k, n, W_HGRN)]
    return _proj_call(_norm_proj_hgrn_kernel, "norm_proj_hgrn", x2d, tm, n, (g, lb, w), specs)[0]


def _norm_proj_conv(x2d, g, w, conv_w, hist, tm, rows_per_seq):
    k, n = w.shape[1], 4 * WIDTH + LANES
    const = lambda i: (0, 0)
    tm = min(tm, rows_per_seq)
    assert rows_per_seq % tm == 0 and x2d.shape[0] % rows_per_seq == 0
    tail_shape = (SUBLANES, 3 * WIDTH)
    specs = [pl.BlockSpec((1, k), const), _weight_spec(k, 4 * WIDTH, W_GDN), _weight_spec(k, LANES, W_SMALL),
             pl.BlockSpec((CONV_W, 3 * WIDTH), const), pl.BlockSpec(tail_shape, const)]
    return _proj_call(functools.partial(_norm_proj_conv_kernel, steps_per_seq=rows_per_seq // tm),
                      "norm_proj_gdn", x2d, tm, n, (g, w, w, conv_w, hist), specs,
                      extra_out=[(pl.BlockSpec(tail_shape, const), jax.ShapeDtypeStruct(tail_shape, F32))],
                      extra_scratch=[pltpu.VMEM(tail_shape, F32)])


HGRN_LEVELS = 6
HGRN_DIAG = HGRN_LEVELS


def _hgrn_level_table():
    r = np.arange(STACK)
    t, hd = r % CHUNK, r // CHUNK
    x = t[:, None] ^ t[None, :]
    lv = np.floor(np.log2(np.maximum(x, 1))).astype(np.int32)
    valid = (hd[:, None] == hd[None, :]) & (t[:, None] > t[None, :])
    lv = np.where(valid, lv, -1)
    lv = np.where(r[:, None] == r[None, :], HGRN_DIAG, lv)
    return jnp.asarray(lv, dtype=jnp.int32)


def _hgrn_kernel(lv_ref, g_ref, st0_ref, q_ref, f_ref, v_ref, gate_ref, o_ref, stout_ref, st_ref, *, nb, nc):
    @pl.when(pl.program_id(0) == 0)
    def _():
        st_ref[...] = st0_ref[...]

    bsz = nc * nb
    q = _load_chunks(q_ref, nc)
    f = _load_chunks(f_ref, nc)
    v = _load_chunks(v_ref, nc)
    gate = _load_chunks(gate_ref, nc)

    per_seq = lambda a: a.reshape(bsz, CHUNK, WIDTH)
    trow = lax.broadcasted_iota(jnp.int32, (1, CHUNK, WIDTH), 1)
    b = jnp.log(f)
    s = 1
    while s < CHUNK:
        b = (per_seq(b) + jnp.where(trow >= s, per_seq(_roll_rows(b, s)), 0.0)).reshape(b.shape)
        s *= 2

    qs, ks, vs, bs = (_stack_heads(a, bsz) for a in (q, 1.0 - f, v, b))
    lv = lv_ref[...][None]
    to3 = functools.partial(_to_stacks, bsz=bsz)
    to_g = functools.partial(_to_heads, bsz=bsz)
    roll_g = lambda a, shift: to_g(_roll_rows(a.reshape(qs.shape), shift))
    t = lax.broadcasted_iota(jnp.int32, (1, CHUNK, HEAD_DIM), 1)
    qs_g, ks_g, bs_g = to_g(qs), to_g(ks), to_g(bs)

    attn = jnp.where(lv == HGRN_DIAG, _bdot_nt(to3(qs), to3(ks)), 0.0)
    bref = bs_g
    bnext = roll_g(bs_g, -1)
    for li in range(HGRN_LEVELS):
        m = 1 << li
        if li > 0:
            half = m // 2
            upper = (t & (m - 1)) >= half
            bref = jnp.where(upper, roll_g(bref, half), bref)
            bnext = jnp.where(upper, bnext, roll_g(bnext, -half))
        odd = ((t >> li) & 1) == 1
        both = jnp.where(odd, qs_g, ks_g) * jnp.exp(jnp.where(odd, bs_g - bref, bnext - bs_g))
        both = to3(both.reshape(qs.shape)).astype(BF16)
        attn = jnp.where(lv == li, _bdot_nt(both, both), attn)

    o = _bdot(attn, to3(vs))

    n_g = nb * HEADS
    st = st_ref[...]
    qe = qs_g * jnp.exp(bs_g)
    b_end = bs_g[:, CHUNK - 1:CHUNK, :]
    ke = ks_g * jnp.exp(b_end - bs_g)
    vs_g = to_g(vs)
    o_inter = []
    for j in range(nc):
        sl = slice(j * n_g, (j + 1) * n_g)
        o_inter.append(_bdot_nt(qe[sl], st))
        st = st * jnp.exp(b_end[sl]) + _bdot_tn(vs_g[sl], ke[sl])
    st_ref[...] = st
    stout_ref[...] = st
    o = (to_g(o) + jnp.concatenate(o_inter, axis=0)).reshape(qs.shape)

    o = o * lax.rsqrt(jnp.mean(o * o, axis=-1, keepdims=True) + EPS) * g_ref[...]
    o = o * _stack_heads(gate, bsz)
    _store_chunks(o_ref, _unstack_heads(o, bsz), nb, nc)


def _gdn_cumsum_matrix():
    r = np.arange(STACK)
    u = (r[:, None] // CHUNK == r[None, :] // CHUNK) & (r[:, None] <= r[None, :])
    return jnp.asarray(u, dtype=BF16)


def _gdn_kernel(u_ref, alog_ref, dt_ref, g_ref, s_ref, st0_ref, q_ref, k_ref, v_ref, z_ref,
                o_ref, stout_ref, st_ref, *, nb, nc):
    @pl.when(pl.program_id(0) == 0)
    def _():
        st_ref[...] = st0_ref[...]

    bsz = nc * nb
    qs = _stack_heads(_load_chunks(q_ref, nc), bsz)
    ks = _stack_heads(_load_chunks(k_ref, nc), bsz)
    vs = _stack_heads(_load_chunks(v_ref, nc), bsz)
    gate = _load_chunks(z_ref, nc)
    qs = qs * lax.rsqrt(jnp.sum(qs * qs, axis=-1, keepdims=True) + EPS) * (HEAD_DIM ** -0.5)
    ks = ks * lax.rsqrt(jnp.sum(ks * ks, axis=-1, keepdims=True) + EPS)
    to3 = functools.partial(_to_stacks, bsz=bsz)
    qs, ks, vs = to3(qs), to3(ks), to3(vs)
    n_st = bsz * N_STACKS

    srow = _chunks_first_scalars(s_ref[...], nc).reshape(n_st, 2, STACK)
    per_stack = lambda ref: jnp.broadcast_to(ref[...][None], (bsz, N_STACKS, 1, STACK)).reshape(n_st, 1, STACK)
    beta_row = _sigmoid(srow[:, 0:1, :])
    g_row = -jnp.exp(per_stack(alog_ref)) * _softplus(srow[:, 1:2, :] + per_stack(dt_ref))
    n8 = n_st * SUBLANES
    g8 = jnp.broadcast_to(g_row, (n_st, SUBLANES, STACK)).reshape(n8, STACK)
    pieces = jnp.concatenate(_split3(g8), axis=0)
    cum = jnp.dot(pieces, u_ref[...], preferred_element_type=F32)
    cum = cum[0:n8] + cum[n8:2 * n8] + cum[2 * n8:]
    cum_row = cum.reshape(n_st, SUBLANES, STACK)[:, 0:1, :]

    r = lax.broadcasted_iota(jnp.int32, (STACK, STACK), 0)
    cc = lax.broadcasted_iota(jnp.int32, (STACK, STACK), 1)
    eye = (r == cc)[None]
    cum_col = jnp.sum(jnp.where(eye, cum_row, 0.0), axis=2, keepdims=True)
    beta_col = jnp.sum(jnp.where(eye, beta_row, 0.0), axis=2, keepdims=True)
    same = (r >> 6) == (cc >> 6)
    lower = (same & (r >= cc))[None]
    strict = (same & (r > cc))[None]
    ratio = jnp.exp(jnp.minimum(cum_col - cum_row, 0.0))

    kkt = _bdot_nt(ks, ks)
    qkt = _bdot_nt(qs, ks)
    lm = jnp.where(strict, beta_col * ratio * kkt, 0.0)

    ident = eye.astype(F32)
    l0 = jnp.where(((r >> 3) == (cc >> 3))[None], lm, 0.0)
    l2 = _bdot(l0, l0)
    l4 = _bdot(l2, l2)
    xinv = _bdot(_bdot(ident - l0, ident + l2), ident + l4)
    for lvl in (3, 4, 5):
        off = (((r >> (lvl + 1)) == (cc >> (lvl + 1))) & ((r >> lvl) != (cc >> lvl)))[None]
        xinv = xinv - _bdot(_bdot(xinv, jnp.where(off, lm, 0.0)), xinv)

    decay_col = jnp.exp(cum_col)
    rhs = jnp.concatenate([beta_col * vs, (beta_col * decay_col) * ks], axis=2)
    sol = _bdot(xinv, rhs)

    to_g = functools.partial(_to_heads, bsz=bsz)
    n_g = nb * HEADS
    st = st_ref[...]
    u0_g, w_g, q_g = to_g(sol[:, :, :HEAD_DIM]), to_g(sol[:, :, HEAD_DIM:]), to_g(qs)
    cum_g = to_g(cum_col)
    cum_end = cum_g[:, CHUNK - 1:CHUNK, :]
    kdec = to_g(ks) * jnp.exp(cum_end - cum_g)
    decay_g = to_g(decay_col)
    us, o_inter = [], []
    for j in range(nc):
        sl = slice(j * n_g, (j + 1) * n_g)
        u_j = u0_g[sl] - _bdot(w_g[sl], st)
        us.append(u_j)
        o_inter.append(decay_g[sl] * _bdot(q_g[sl], st))
        st = jnp.exp(cum_end[sl]) * st + _bdot_tn(kdec[sl], u_j)
    st_ref[...] = st
    stout_ref[...] = st
    attn = jnp.where(lower, qkt * ratio, 0.0)
    u = jnp.concatenate(us, axis=0).reshape(bsz * HEADS * CHUNK, HEAD_DIM)
    o = jnp.concatenate(o_inter, axis=0) + to_g(_bdot(attn, to3(u)))
    o = o.reshape(bsz * HEADS * CHUNK, HEAD_DIM)

    o = o * lax.rsqrt(jnp.mean(o * o, axis=-1, keepdims=True) + EPS) * g_ref[...]
    o = o * _stack_heads(gate, bsz)
    _store_chunks(o_ref, _unstack_heads(o, bsz), nb, nc)


def _recurrence_call(kernel_fn, name, proj, state0, operands, specs, nc):
    nb, length, _ = proj.shape
    rows = nc * CHUNK
    state_shape = (nb * HEADS, HEAD_DIM, HEAD_DIM)
    state_spec = pl.BlockSpec(state_shape, lambda c: (0, 0, 0))
    chunk_specs = [pl.BlockSpec((nb, rows, WIDTH), functools.partial(lambda c, j: (0, c, j), j=j))
                   for j in range(4)]
    return pl.pallas_call(
        functools.partial(kernel_fn, nb=nb, nc=nc),
        grid=(length // rows,),
        in_specs=specs + [state_spec] + chunk_specs,
        out_specs=[pl.BlockSpec((nb, rows, WIDTH), lambda c: (0, c, 0)), state_spec],
        out_shape=[jax.ShapeDtypeStruct((nb, length, WIDTH), BF16),
                   jax.ShapeDtypeStruct(state_shape, F32)],
        scratch_shapes=[pltpu.VMEM(state_shape, F32)],
        compiler_params=pltpu.CompilerParams(
            dimension_semantics=("arbitrary",), vmem_limit_bytes=VMEM_LIMIT),
        name=name,
    )(*operands, state0, *([proj] * 4))


def _hgrn(proj, state0, norm_g, nc):
    const = lambda c: (0, 0)
    specs = [pl.BlockSpec((STACK, STACK), const), pl.BlockSpec((1, HEAD_DIM), const)]
    return _recurrence_call(_hgrn_kernel, "hgrn", proj, state0, (_hgrn_level_table(), norm_g), specs, nc)


def _gdn(proj, state0, scal, a_log_row, dt_row, norm_g, nc):
    nb = proj.shape[0]
    const = lambda c: (0, 0)
    const3 = lambda c: (0, 0, 0)
    specs = [pl.BlockSpec((STACK, STACK), const),
             pl.BlockSpec((N_STACKS, 1, STACK), const3),
             pl.BlockSpec((N_STACKS, 1, STACK), const3),
             pl.BlockSpec((1, HEAD_DIM), const),
             pl.BlockSpec((nb, nc, N_STACKS, 2, STACK), lambda c: (0, c, 0, 0, 0))]
    return _recurrence_call(_gdn_kernel, "gdn", proj, state0,
                            (_gdn_cumsum_matrix(), a_log_row, dt_row, norm_g, scal), specs, nc)


ROUTE_LANE0 = N_GROUPS
ROUTE_NEG = -1e30


def _route_tile(lg, ltri, carry):
    lane = lax.broadcasted_iota(jnp.int32, lg.shape, 1).astype(F32)
    first = lambda mask: jnp.min(jnp.where(mask, lane, float(LANES)), axis=1, keepdims=True)
    top = lambda mask: jnp.max(jnp.where(mask, lg, ROUTE_NEG), axis=1, keepdims=True)

    is_g = lane < N_GROUPS
    gmax = top(is_g)
    grp = first(is_g & (lg == gmax))
    p_grp = 1.0 / jnp.sum(jnp.where(is_g, jnp.exp(lg - gmax), 0.0), axis=1, keepdims=True)

    lo = ROUTE_LANE0 + grp * EXPERTS_PER_GROUP
    in_grp = (lane >= lo) & (lane < lo + EXPERTS_PER_GROUP)
    m1 = top(in_grp)
    i1 = first(in_grp & (lg == m1))
    rest = in_grp & (lane != i1)
    m2 = top(rest)
    i2 = first(rest & (lg == m2))
    r = jnp.exp(m2 - m1)
    w0 = p_grp / (1.0 + r)
    w1 = p_grp * r / (1.0 + r)

    hot0 = (lane == i1).astype(F32)
    hot1 = (lane == i2).astype(F32)
    both = hot0 + hot1
    before = jnp.dot(ltri, both.astype(BF16), preferred_element_type=F32) + carry
    rank0 = jnp.sum(hot0 * before, axis=1, keepdims=True)
    rank1 = jnp.sum(hot1 * before, axis=1, keepdims=True)
    cols = (w0, w1, i1 - ROUTE_LANE0, i2 - ROUTE_LANE0, rank0, rank1)
    info = jnp.zeros_like(lg)
    for k, col in enumerate(cols):
        info = jnp.where(lane == k, col, info)
    return info, carry + jnp.sum(both, axis=0, keepdims=True)


def _merge_kernel(oa_ref, ob_ref, x_ref, mg_ref, wgate_ref, hgup_ref, gdup_ref, wout_ref,
                  ng_ref, rw_ref, rb_ref, ltri_ref, h2_ref, xn_ref, info_ref, ids_ref, cnt_ref,
                  wgate_bf, carry_ref):
    @pl.when(pl.program_id(0) == 0)
    def _():
        carry_ref[...] = jnp.zeros_like(carry_ref)
        _cast_rows(wgate_ref, wgate_bf)

    gates = _sigmoid(jnp.dot(_normed_bf16(x_ref, mg_ref), wgate_bf[...], preferred_element_type=F32))
    ua = jnp.dot(oa_ref[...], hgup_ref[...], preferred_element_type=F32)
    ub = jnp.dot(ob_ref[...], gdup_ref[...], preferred_element_type=F32)
    merged = gates[:, :D_MODEL] * ua + gates[:, D_MODEL:] * ub
    h2 = x_ref[...] + jnp.dot(merged.astype(BF16), wout_ref[...], preferred_element_type=F32)
    h2_ref[...] = h2
    xn = h2 * lax.rsqrt(jnp.mean(h2 * h2, axis=-1, keepdims=True) + EPS) * ng_ref[...]
    xn_ref[...] = _pack_halves(xn)
    lg = jnp.dot(xn.astype(BF16), rw_ref[...], preferred_element_type=F32) + rb_ref[...]
    info, carry = _route_tile(lg, ltri_ref[...], carry_ref[...])
    info_ref[...] = info
    ids_ref[...] = info.T[:SUBLANES].astype(jnp.int32)
    carry_ref[...] = carry
    cnt_ref[...] = carry


def _merge(o_a, o_b, x2d, mix_g, w_gates, hg_up, gd_up, w_out, norm_g, rw, rb):
    t = x2d.shape[0]
    tm = TOKEN_TM
    row = lambda i: (i, 0)
    const = lambda i: (0, 0)
    ltri = jnp.asarray(np.tril(np.ones((tm, tm), np.float32), -1), dtype=BF16)
    return pl.pallas_call(
        _merge_kernel,
        grid=(t // tm,),
        in_specs=[pl.BlockSpec((tm, WIDTH), row),
                  pl.BlockSpec((tm, WIDTH), row),
                  pl.BlockSpec((tm, D_MODEL), row),
                  pl.BlockSpec((1, D_MODEL), const),
                  pl.BlockSpec((D_MODEL, 2 * D_MODEL), const, pipeline_mode=pl.Buffered(1)),
                  pl.BlockSpec((WIDTH, D_MODEL), const),
                  pl.BlockSpec((WIDTH, D_MODEL), const),
                  pl.BlockSpec((D_MODEL, D_MODEL), const),
                  pl.BlockSpec((1, D_MODEL), const),
                  pl.BlockSpec((D_MODEL, LANES), const),
                  pl.BlockSpec((1, LANES), const),
                  pl.BlockSpec((tm, tm), const)],
        out_specs=[pl.BlockSpec((tm, D_MODEL), row),
                   pl.BlockSpec((tm, HALF), row),
                   pl.BlockSpec((tm, LANES), row),
                   pl.BlockSpec((SUBLANES, tm), lambda i: (0, i)),
                   pl.BlockSpec((1, LANES), const)],
        out_shape=[jax.ShapeDtypeStruct((t, D_MODEL), F32),
                   jax.ShapeDtypeStruct((t, HALF), U32),
                   jax.ShapeDtypeStruct((t, LANES), F32),
                   jax.ShapeDtypeStruct((SUBLANES, t), jnp.int32),
                   jax.ShapeDtypeStruct((1, LANES), F32)],
        scratch_shapes=[pltpu.VMEM((D_MODEL, 2 * D_MODEL), BF16), pltpu.VMEM((1, LANES), F32)],
        compiler_params=pltpu.CompilerParams(
            dimension_semantics=("arbitrary",), vmem_limit_bytes=VMEM_LIMIT),
        name="merge",
    )(o_a, o_b, x2d, mix_g, w_gates, hg_up, gd_up, w_out, norm_g, rw, rb, ltri)


def _sc_gather(table, idx):
    n_idx = idx.shape[0]
    cols = table.shape[1]
    per_worker = n_idx // SC_WORKERS
    assert n_idx % (SC_WORKERS * SC_CHUNK) == 0
    mesh = plsc.VectorSubcoreMesh(core_axis_name="c", subcore_axis_name="s")

    @functools.partial(
        pl.kernel, mesh=mesh,
        out_type=jax.ShapeDtypeStruct((n_idx, cols), table.dtype),
        scratch_types=[pltpu.VMEM((SC_CHUNK,), jnp.int32),
                       pltpu.VMEM((SC_CHUNK, cols), table.dtype),
                       pltpu.SemaphoreType.DMA],
    )
    def gather(table_hbm, idx_hbm, out_hbm, idx_v, rows_v, sem):
        worker = lax.axis_index("s") * SC_CORES + lax.axis_index("c")
        base = worker * per_worker

        @pl.loop(0, per_worker // SC_CHUNK)
        def _(c):
            off = pl.multiple_of(base + c * SC_CHUNK, SC_CHUNK)
            pltpu.sync_copy(idx_hbm.at[pl.ds(off, SC_CHUNK)], idx_v)
            pltpu.async_copy(table_hbm.at[idx_v], rows_v, sem).wait()
            pltpu.sync_copy(rows_v, out_hbm.at[pl.ds(off, SC_CHUNK)])

    return gather(table, idx)


def _sc_dispatch(rows, dest, n_out):
    t, cols = rows.shape
    per_worker = t // SC_WORKERS
    assert t % (SC_WORKERS * SC_CHUNK) == 0 and dest.shape[0] == 2 * t
    mesh = plsc.VectorSubcoreMesh(core_axis_name="c", subcore_axis_name="s")

    @functools.partial(
        pl.kernel, mesh=mesh,
        out_type=jax.ShapeDtypeStruct((n_out, cols), rows.dtype),
        scratch_types=[pltpu.VMEM((SC_CHUNK,), jnp.int32),
                       pltpu.VMEM((SC_CHUNK, cols), rows.dtype)],
    )
    def dispatch(rows_hbm, dest_hbm, out_hbm, idx_v, rows_v):
        worker = lax.axis_index("s") * SC_CORES + lax.axis_index("c")
        base = worker * per_worker

        @pl.loop(0, per_worker // SC_CHUNK)
        def _(c):
            off = pl.multiple_of(base + c * SC_CHUNK, SC_CHUNK)
            pltpu.sync_copy(rows_hbm.at[pl.ds(off, SC_CHUNK)], rows_v)
            for slot in range(2):
                pltpu.sync_copy(dest_hbm.at[pl.ds(slot * t + off, SC_CHUNK)], idx_v)
                pltpu.sync_copy(rows_v, out_hbm.at[idx_v])

    return dispatch(rows, dest)


def _moe_kernel(be_ref, nu_ref, x_ref, wg_ref, wu_ref, wd_ref, y_ref, wg_bf, wu_bf, wd_bf):
    i = pl.program_id(0)

    @pl.when((i == 0) | (be_ref[i] != be_ref[jnp.maximum(i - 1, 0)]))
    def _():
        _cast_rows(wg_ref, wg_bf)
        _cast_rows(wu_ref, wu_bf)
        _cast_rows(wd_ref, wd_bf)

    @pl.when(i < nu_ref[0])
    def _():
        xb = _unpack_halves(x_ref[...]).astype(BF16)
        a = jnp.dot(xb, wg_bf[...], preferred_element_type=F32)
        u = jnp.dot(xb, wu_bf[...], preferred_element_type=F32)
        y = jnp.dot((_silu(a) * u).astype(BF16), wd_bf[...], preferred_element_type=F32)
        y_ref[...] = _pack_halves(y)

    @pl.when(i >= nu_ref[0])
    def _():
        y_ref[...] = jnp.zeros_like(y_ref)


def _moe(blk_expert, n_used, x_sorted, w_gate, w_up, w_down):
    n_blocks = blk_expert.shape[0]
    wspec = lambda shape: pl.BlockSpec((None,) + shape, lambda i, be, nu: (be[i], 0, 0))
    rows = pl.BlockSpec((MOE_ROWS, HALF), lambda i, be, nu: (i, 0))
    grid_spec = pltpu.PrefetchScalarGridSpec(
        num_scalar_prefetch=2,
        grid=(n_blocks,),
        in_specs=[rows, wspec((D_MODEL, D_FF)), wspec((D_MODEL, D_FF)), wspec((D_FF, D_MODEL))],
        out_specs=rows,
        scratch_shapes=[pltpu.VMEM((D_MODEL, D_FF), BF16), pltpu.VMEM((D_MODEL, D_FF), BF16),
                        pltpu.VMEM((D_FF, D_MODEL), BF16)],
    )
    return pl.pallas_call(
        _moe_kernel,
        grid_spec=grid_spec,
        out_shape=jax.ShapeDtypeStruct(x_sorted.shape, U32),
        compiler_params=pltpu.CompilerParams(
            dimension_semantics=("arbitrary",), vmem_limit_bytes=VMEM_LIMIT),
        name="moe",
    )(blk_expert, n_used, x_sorted, w_gate, w_up, w_down)


def _combine_kernel(h2_ref, rw_ref, g_ref, y0_ref, y1_ref, o_ref):
    rw = rw_ref[...]
    h = h2_ref[...] + rw[:, 0:1] * _unpack_halves(y0_ref[...]) + rw[:, 1:2] * _unpack_halves(y1_ref[...])
    o_ref[...] = h * lax.rsqrt(jnp.mean(h * h, axis=-1, keepdims=True) + EPS) * g_ref[...]


def _combine(h2, rweights, final_g, y):
    t = h2.shape[0]
    tm = TOKEN_TM
    row = lambda i: (i, 0)
    return pl.pallas_call(
        _combine_kernel,
        grid=(t // tm,),
        in_specs=[pl.BlockSpec((tm, D_MODEL), row),
                  pl.BlockSpec((tm, LANES), row),
                  pl.BlockSpec((1, D_MODEL), lambda i: (0, 0)),
                  pl.BlockSpec((tm, HALF), row),
                  pl.BlockSpec((tm, HALF), lambda i: (i + t // tm, 0))],
        out_specs=pl.BlockSpec((tm, D_MODEL), row),
        out_shape=jax.ShapeDtypeStruct((t, D_MODEL), F32),
        compiler_params=pltpu.CompilerParams(
            dimension_semantics=("arbitrary",), vmem_limit_bytes=VMEM_LIMIT),
        name="combine",
    )(h2, rweights, final_g, y, y)


def _block_layout(ids, counts_row, t):
    n_blocks = 2 * t // MOE_ROWS + N_EXPERTS
    counts = counts_row[0, ROUTE_LANE0:ROUTE_LANE0 + N_EXPERTS].astype(jnp.int32)
    padded = ((counts + MOE_ROWS - 1) // MOE_ROWS) * MOE_ROWS
    pend = jnp.cumsum(padded)
    pstart = pend - padded
    blk_first = jnp.arange(n_blocks, dtype=jnp.int32) * MOE_ROWS
    blk_expert = jnp.minimum(jnp.sum((pend[None, :] <= blk_first[:, None]).astype(jnp.int32), axis=1),
                             N_EXPERTS - 1).astype(jnp.int32)
    n_used = (pend[-1] // MOE_ROWS).astype(jnp.int32).reshape(1)
    expert, rank = ids[2:4], ids[4:6]
    experts = jnp.arange(N_EXPERTS, dtype=jnp.int32)
    first_row = jnp.sum(jnp.where(expert[:, :, None] == experts, pstart, 0), axis=-1)
    dest = (first_row + rank).reshape(2 * t)
    return dest, blk_expert, n_used, n_blocks * MOE_ROWS


def kernel(x, meta_tokens, hg_lb_logits, norm_mix_g, w_in, gd_conv_w, gd_A_log, gd_dt_bias, hg_norm_g, gd_norm_g, hg_up, gd_up, w_out, norm_ffn_g, router_group_w, router_group_b, router_expert_w, router_expert_b, w_gate, w_up, w_down, final_norm_g):
    bsz, seq, d = x.shape
    t = bsz * seq
    x2d = x.reshape(t, d)

    lb = jnp.cumsum(jax.nn.softmax(hg_lb_logits.astype(F32), axis=0), axis=0)[0].reshape(1, WIDTH)
    w_all = w_in.astype(F32)
    w_gates = w_all[0, :, W_SMALL + 2 * HEADS:]
    g_mix = norm_mix_g[0].reshape(1, d)
    meta_blk = jnp.concatenate([jnp.zeros((CHUNK - N_META, d), F32), meta_tokens.astype(F32)], axis=0)

    conv_w = gd_conv_w[0].astype(F32)
    no_history = jnp.zeros((SUBLANES, 3 * WIDTH), F32)
    proj, proj_meta = {}, {}
    proj["hgrn"] = _norm_proj_hgrn(x2d, g_mix, lb, w_all, PROJ_TM)
    proj_meta["hgrn"] = _norm_proj_hgrn(meta_blk, g_mix, lb, w_all, CHUNK)
    proj_meta["gdn"], meta_tail = _norm_proj_conv(meta_blk, g_mix, w_all, conv_w, no_history, CHUNK, CHUNK)
    proj["gdn"], _ = _norm_proj_conv(x2d, g_mix, w_all, conv_w, meta_tail, PROJ_TM, seq)

    def scalar_rows(p, nb, nc):
        s = p[:, 4 * WIDTH:4 * WIDTH + 2 * HEADS].reshape(nb, nc, CHUNK, 2, N_STACKS, HEADS_PER_STACK)
        return s.transpose(0, 1, 4, 3, 5, 2).reshape(nb, nc, N_STACKS, 2, STACK)

    a_log_row = jnp.repeat(gd_A_log[0].astype(F32), CHUNK).reshape(N_STACKS, 1, STACK)
    dt_row = jnp.repeat(gd_dt_bias[0].astype(F32), CHUNK).reshape(N_STACKS, 1, STACK)
    hg_g, gd_g = hg_norm_g[0].reshape(1, HEAD_DIM), gd_norm_g[0].reshape(1, HEAD_DIM)

    zero_state = jnp.zeros((HEADS, HEAD_DIM, HEAD_DIM), F32)
    per_seq = lambda st: jnp.tile(st, (bsz, 1, 1))
    _, hg_state = _hgrn(proj_meta["hgrn"][None], zero_state, hg_g, 1)
    o_a, _ = _hgrn(proj["hgrn"].reshape(bsz, seq, -1), per_seq(hg_state), hg_g, CHUNKS_PER_STEP)
    _, gd_state = _gdn(proj_meta["gdn"][None], zero_state, scalar_rows(proj_meta["gdn"], 1, 1),
                       a_log_row, dt_row, gd_g, 1)
    o_b, _ = _gdn(proj["gdn"].reshape(bsz, seq, -1), per_seq(gd_state),
                  scalar_rows(proj["gdn"], bsz, seq // CHUNK), a_log_row, dt_row, gd_g, CHUNKS_PER_STEP)

    rw = jnp.zeros((d, LANES), F32)
    rw = rw.at[:, :N_GROUPS].set(router_group_w[0]).at[:, N_GROUPS:N_GROUPS + N_EXPERTS].set(router_expert_w[0])
    rb = jnp.zeros((1, LANES), F32)
    rb = rb.at[0, :N_GROUPS].set(router_group_b[0]).at[0, N_GROUPS:N_GROUPS + N_EXPERTS].set(router_expert_b[0])
    h2, xn, info, ids, counts = _merge(o_a.reshape(t, WIDTH), o_b.reshape(t, WIDTH), x2d, g_mix, w_gates,
                                  hg_up[0].astype(BF16), gd_up[0].astype(BF16), w_out[0].astype(BF16),
                                  norm_ffn_g[0].reshape(1, d), rw.astype(BF16), rb)

    dest, blk_expert, n_used, n_rows = _block_layout(ids, counts, t)
    x_sorted = _sc_dispatch(xn, dest, n_rows)
    y_sorted = _moe(blk_expert, n_used, x_sorted,
                    w_gate[0].astype(F32), w_up[0].astype(F32), w_down[0].astype(F32))
    y_tok = _sc_gather(y_sorted, dest)
    out = _combine(h2, info, final_norm_g.reshape(1, d), y_tok)
    return out.reshape(bsz, seq, d)
```

```python
import functools

import numpy as np
import jax
import jax.numpy as jnp
from jax import lax
from jax.experimental import pallas as pl
from jax.experimental.pallas import tpu as pltpu
from jax.experimental.pallas import tpu_sc as plsc

F32 = jnp.float32
BF16 = jnp.bfloat16
U32 = jnp.uint32

D_MODEL = 1024
N_META = 16
CHUNK = 64
EPS = 1e-6
HEADS = 4
HEAD_DIM = 128
WIDTH = HEADS * HEAD_DIM
HEADS_PER_STACK = 2
STACK = HEADS_PER_STACK * CHUNK
N_STACKS = HEADS // HEADS_PER_STACK
CONV_W = 4
N_GROUPS = 4
EXPERTS_PER_GROUP = 8
N_EXPERTS = N_GROUPS * EXPERTS_PER_GROUP
D_FF = 512
LANES = 128
SUBLANES = 8
HALF = D_MODEL // 2
MOE_ROWS = 512
CAST_SLAB = 128
SC_CORES, SC_SUBCORES = 2, 16
SC_WORKERS = SC_CORES * SC_SUBCORES
SC_CHUNK = 128
PROJ_TM = 1024
W_HGRN, W_GDN, W_SMALL = 0, 4 * WIDTH, 8 * WIDTH
HGRN_CHUNKS, GDN_CHUNKS = 4, 2
TOKEN_TM = 512
COMBINE_TM = 1024
VMEM_LIMIT = 48 * 1024 * 1024


def _bdot(a, b):
    return lax.dot_general(a.astype(BF16), b.astype(BF16), (((2,), (1,)), ((0,), (0,))),
                           preferred_element_type=F32)


def _bdot_nt(a, b):
    return lax.dot_general(a.astype(BF16), b.astype(BF16), (((2,), (2,)), ((0,), (0,))),
                           preferred_element_type=F32)


def _bdot_tn(a, b):
    return lax.dot_general(a.astype(BF16), b.astype(BF16), (((1,), (1,)), ((0,), (0,))),
                           preferred_element_type=F32)


def _split3(x):
    hi = x.astype(BF16)
    r1 = x - hi.astype(F32)
    mid = r1.astype(BF16)
    lo = (r1 - mid.astype(F32)).astype(BF16)
    return hi, mid, lo


def _sigmoid(x):
    return 1.0 / (1.0 + jnp.exp(-x))


def _silu(x):
    return x * _sigmoid(x)


def _softplus(x):
    return jnp.maximum(x, 0.0) + jnp.log(1.0 + jnp.exp(-jnp.abs(x)))


def _stack_heads(x, bsz):
    x3 = x.reshape(bsz, CHUNK, WIDTH)
    y = jnp.concatenate([x3[:, :, h * HEAD_DIM:(h + 1) * HEAD_DIM] for h in range(HEADS)], axis=1)
    return y.reshape(bsz * HEADS * CHUNK, HEAD_DIM)


def _unstack_heads(x, bsz):
    x3 = x.reshape(bsz, HEADS * CHUNK, HEAD_DIM)
    return jnp.concatenate([x3[:, h * CHUNK:(h + 1) * CHUNK, :] for h in range(HEADS)], axis=2)


def _to_stacks(a, bsz):
    return a.reshape(bsz * N_STACKS, STACK, a.shape[-1])


def _to_heads(a, bsz):
    return a.reshape(bsz * HEADS, CHUNK, a.shape[-1])


def _roll_rows(x, shift):
    n = x.shape[0]
    return pltpu.roll(x, shift % n, axis=0)


def _chunks_first(x, nc):
    return jnp.concatenate([x[:, j * CHUNK:(j + 1) * CHUNK] for j in range(nc)], axis=0)


def _chunks_first_scalars(s, nc):
    return jnp.concatenate([s[:, j] for j in range(nc)], axis=0)


def _load_chunks(ref, nc):
    x = _chunks_first(ref[...], nc)
    return x.reshape(x.shape[0] * CHUNK, WIDTH)


def _store_chunks(ref, x, nb, nc):
    ref[...] = jnp.concatenate([x[j * nb:(j + 1) * nb] for j in range(nc)], axis=1).astype(ref.dtype)


def _cast_rows(src_ref, dst_ref, col0=0):
    slab = CAST_SLAB
    rows, cols = src_ref.shape
    assert rows % slab == 0

    def body(k, carry):
        r = pl.multiple_of(k * slab, slab)
        dst_ref[pl.ds(r, slab), col0:col0 + cols] = src_ref[pl.ds(r, slab), :].astype(dst_ref.dtype)
        return carry

    lax.fori_loop(0, rows // slab, body, 0)


def _pack_halves(x):
    lo = lax.bitcast_convert_type(x[:, :HALF].astype(BF16).astype(F32), U32)
    hi = lax.bitcast_convert_type(x[:, HALF:].astype(BF16).astype(F32), U32)
    return (lo >> 16) | (hi & jnp.uint32(0xFFFF0000))


def _unpack_halves(p):
    lo = lax.bitcast_convert_type(p << 16, F32)
    hi = lax.bitcast_convert_type(p & jnp.uint32(0xFFFF0000), F32)
    return jnp.concatenate([lo, hi], axis=1)


def _normed_bf16(x_ref, g_ref):
    x = x_ref[...]
    return (x * lax.rsqrt(jnp.mean(x * x, axis=-1, keepdims=True) + EPS) * g_ref[...]).astype(BF16)


def _norm_proj_hgrn_kernel(x_ref, g_ref, lb_ref, w_ref, o_ref, wbf_ref):
    @pl.when(pl.program_id(0) == 0)
    def _():
        _cast_rows(w_ref, wbf_ref)

    acc = jnp.dot(_normed_bf16(x_ref, g_ref), wbf_ref[...], preferred_element_type=F32)
    w1, w2, w3 = WIDTH, 2 * WIDTH, 3 * WIDTH
    lb = lb_ref[...]
    o_ref[:, :w1] = _silu(acc[:, :w1])
    o_ref[:, w1:w2] = lb + (1.0 - lb) * _sigmoid(acc[:, w1:w2])
    o_ref[:, w2:w3] = acc[:, w2:w3]
    o_ref[:, w3:] = _silu(acc[:, w3:])


def _norm_proj_conv_kernel(x_ref, g_ref, w_ref, ws_ref, cw_ref, hist_ref, o_ref, tail_ref, wbf_ref, carry_ref,
                           *, steps_per_seq):
    i = pl.program_id(0)

    @pl.when(i == 0)
    def _():
        _cast_rows(w_ref, wbf_ref)
        _cast_rows(ws_ref, wbf_ref, 4 * WIDTH)

    xn = _normed_bf16(x_ref, g_ref)
    w3, w4 = 3 * WIDTH, 4 * WIDTH
    raw = jnp.dot(xn, wbf_ref[:, :w3], preferred_element_type=F32)
    rest = jnp.dot(xn, wbf_ref[:, w3:], preferred_element_type=F32)
    rows = raw.shape[0]
    hist = jnp.where(i % steps_per_seq == 0, hist_ref[...], carry_ref[...])
    cw = cw_ref[...]
    tap = lambda d: cw[CONV_W - 1 - d:CONV_W - d, :]
    body = raw * tap(0)
    head = raw[:SUBLANES]
    r8 = lax.broadcasted_iota(jnp.int32, head.shape, 0)
    first = head * tap(0)
    for d in range(1, CONV_W):
        body = body + _roll_rows(raw, d) * tap(d)
        first = first + jnp.where(r8 >= d, _roll_rows(head, d), _roll_rows(hist, d)) * tap(d)
    o_ref[:SUBLANES, :w3] = _silu(first)
    o_ref[SUBLANES:, :w3] = _silu(body[SUBLANES:])
    o_ref[:, w3:w4] = _silu(rest[:, :WIDTH])
    o_ref[:, w4:] = rest[:, WIDTH:]
    tail = raw[rows - SUBLANES:]
    carry_ref[...] = tail
    tail_ref[...] = tail


def _proj_call(kernel_fn, name, x2d, tm, n, operands, specs, extra_out=(), extra_scratch=()):
    m, k = x2d.shape
    tm = min(tm, m)
    out_specs = [pl.BlockSpec((tm, n), lambda i: (i, 0))] + [s for s, _ in extra_out]
    out_shape = [jax.ShapeDtypeStruct((m, n), F32)] + [s for _, s in extra_out]
    return pl.pallas_call(
        kernel_fn,
        grid=(m // tm,),
        in_specs=[pl.BlockSpec((tm, k), lambda i: (i, 0))] + specs,
        out_specs=out_specs,
        out_shape=out_shape,
        scratch_shapes=[pltpu.VMEM((k, n), BF16)] + list(extra_scratch),
        compiler_params=pltpu.CompilerParams(
            dimension_semantics=("arbitrary",), vmem_limit_bytes=VMEM_LIMIT),
        name=name,
    )(x2d, *operands)


def _weight_spec(k, n, col0):
    assert col0 % n == 0
    return pl.BlockSpec((None, k, n), lambda i: (0, 0, col0 // n), pipeline_mode=pl.Buffered(1))


def _norm_proj_hgrn(x2d, g, lb, w, tm):
    k, n = w.shape[1], 4 * WIDTH
    const = lambda i: (0, 0)
    specs = [pl.BlockSpec((1, k), const), pl.BlockSpec((1, WIDTH), const), _weight_spec(k, n, W_HGRN)]
    return _proj_call(_norm_proj_hgrn_kernel, "norm_proj_hgrn", x2d, tm, n, (g, lb, w), specs)[0]


def _norm_proj_conv(x2d, g, w, conv_w, hist, tm, rows_per_seq):
    k, n = w.shape[1], 4 * WIDTH + LANES
    const = lambda i: (0, 0)
    tm = min(tm, rows_per_seq)
    assert rows_per_seq % tm == 0 and x2d.shape[0] % rows_per_seq == 0
    tail_shape = (SUBLANES, 3 * WIDTH)
    specs = [pl.BlockSpec((1, k), const), _weight_spec(k, 4 * WIDTH, W_GDN), _weight_spec(k, LANES, W_SMALL),
             pl.BlockSpec((CONV_W, 3 * WIDTH), const), pl.BlockSpec(tail_shape, const)]
    return _proj_call(functools.partial(_norm_proj_conv_kernel, steps_per_seq=rows_per_seq // tm),
                      "norm_proj_gdn", x2d, tm, n, (g, w, w, conv_w, hist), specs,
                      extra_out=[(pl.BlockSpec(tail_shape, const), jax.ShapeDtypeStruct(tail_shape, F32))],
                      extra_scratch=[pltpu.VMEM(tail_shape, F32)])


HGRN_LEVELS = 6
HGRN_DIAG = HGRN_LEVELS


def _hgrn_level_table():
    r = np.arange(STACK)
    t, hd = r % CHUNK, r // CHUNK
    x = t[:, None] ^ t[None, :]
    lv = np.floor(np.log2(np.maximum(x, 1))).astype(np.int32)
    valid = (hd[:, None] == hd[None, :]) & (t[:, None] > t[None, :])
    lv = np.where(valid, lv, -1)
    lv = np.where(r[:, None] == r[None, :], HGRN_DIAG, lv)
    return jnp.asarray(lv, dtype=jnp.int32)


def _hgrn_kernel(lv_ref, g_ref, st0_ref, q_ref, f_ref, v_ref, gate_ref, o_ref, stout_ref, st_ref, *, nb, nc):
    @pl.when(pl.program_id(0) == 0)
    def _():
        st_ref[...] = st0_ref[...]

    bsz = nc * nb
    q = _load_chunks(q_ref, nc)
    f = _load_chunks(f_ref, nc)
    v = _load_chunks(v_ref, nc)
    gate = _load_chunks(gate_ref, nc)

    per_seq = lambda a: a.reshape(bsz, CHUNK, WIDTH)
    trow = lax.broadcasted_iota(jnp.int32, (1, CHUNK, WIDTH), 1)
    b = jnp.log(f)
    s = 1
    while s < CHUNK:
        b = (per_seq(b) + jnp.where(trow >= s, per_seq(_roll_rows(b, s)), 0.0)).reshape(b.shape)
        s *= 2

    qs, ks, vs, bs = (_stack_heads(a, bsz) for a in (q, 1.0 - f, v, b))
    lv = lv_ref[...][None]
    to3 = functools.partial(_to_stacks, bsz=bsz)
    to_g = functools.partial(_to_heads, bsz=bsz)
    roll_g = lambda a, shift: to_g(_roll_rows(a.reshape(qs.shape), shift))
    t = lax.broadcasted_iota(jnp.int32, (1, CHUNK, HEAD_DIM), 1)
    qs_g, ks_g, bs_g = to_g(qs), to_g(ks), to_g(bs)

    attn = jnp.where(lv == HGRN_DIAG, _bdot_nt(to3(qs), to3(ks)), 0.0)
    bref = bs_g
    bnext = roll_g(bs_g, -1)
    for li in range(HGRN_LEVELS):
        m = 1 << li
        if li > 0:
            half = m // 2
            upper = (t & (m - 1)) >= half
            bref = jnp.where(upper, roll_g(bref, half), bref)
            bnext = jnp.where(upper, bnext, roll_g(bnext, -half))
        odd = ((t >> li) & 1) == 1
        both = jnp.where(odd, qs_g, ks_g) * jnp.exp(jnp.where(odd, bs_g - bref, bnext - bs_g))
        both = to3(both.reshape(qs.shape)).astype(BF16)
        attn = jnp.where(lv == li, _bdot_nt(both, both), attn)

    o = _bdot(attn, to3(vs))

    n_g = nb * HEADS
    st = st_ref[...]
    qe = qs_g * jnp.exp(bs_g)
    b_end = bs_g[:, CHUNK - 1:CHUNK, :]
    ke = ks_g * jnp.exp(b_end - bs_g)
    vs_g = to_g(vs)
    o_inter = []
    for j in range(nc):
        sl = slice(j * n_g, (j + 1) * n_g)
        o_inter.append(_bdot_nt(qe[sl], st))
        st = st * jnp.exp(b_end[sl]) + _bdot_tn(vs_g[sl], ke[sl])
    st_ref[...] = st
    stout_ref[...] = st
    o = (to_g(o) + jnp.concatenate(o_inter, axis=0)).reshape(qs.shape)

    o = o * lax.rsqrt(jnp.mean(o * o, axis=-1, keepdims=True) + EPS) * g_ref[...]
    o = o * _stack_heads(gate, bsz)
    _store_chunks(o_ref, _unstack_heads(o, bsz), nb, nc)


def _gdn_cumsum_matrix():
    r = np.arange(STACK)
    u = (r[:, None] // CHUNK == r[None, :] // CHUNK) & (r[:, None] <= r[None, :])
    return jnp.asarray(u, dtype=BF16)


def _gdn_kernel(u_ref, alog_ref, dt_ref, g_ref, s_ref, st0_ref, q_ref, k_ref, v_ref, z_ref,
                o_ref, stout_ref, st_ref, *, nb, nc):
    @pl.when(pl.program_id(0) == 0)
    def _():
        st_ref[...] = st0_ref[...]

    bsz = nc * nb
    qs = _stack_heads(_load_chunks(q_ref, nc), bsz)
    ks = _stack_heads(_load_chunks(k_ref, nc), bsz)
    vs = _stack_heads(_load_chunks(v_ref, nc), bsz)
    gate = _load_chunks(z_ref, nc)
    qs = qs * lax.rsqrt(jnp.sum(qs * qs, axis=-1, keepdims=True) + EPS) * (HEAD_DIM ** -0.5)
    ks = ks * lax.rsqrt(jnp.sum(ks * ks, axis=-1, keepdims=True) + EPS)
    to3 = functools.partial(_to_stacks, bsz=bsz)
    qs, ks, vs = to3(qs), to3(ks), to3(vs)
    n_st = bsz * N_STACKS

    srow = _chunks_first_scalars(s_ref[...], nc).reshape(n_st, 2, STACK)
    per_stack = lambda ref: jnp.broadcast_to(ref[...][None], (bsz, N_STACKS, 1, STACK)).reshape(n_st, 1, STACK)
    beta_row = _sigmoid(srow[:, 0:1, :])
    g_row = -jnp.exp(per_stack(alog_ref)) * _softplus(srow[:, 1:2, :] + per_stack(dt_ref))
    n8 = n_st * SUBLANES
    g8 = jnp.broadcast_to(g_row, (n_st, SUBLANES, STACK)).reshape(n8, STACK)
    pieces = jnp.concatenate(_split3(g8), axis=0)
    cum = jnp.dot(pieces, u_ref[...], preferred_element_type=F32)
    cum = cum[0:n8] + cum[n8:2 * n8] + cum[2 * n8:]
    cum_row = cum.reshape(n_st, SUBLANES, STACK)[:, 0:1, :]

    r = lax.broadcasted_iota(jnp.int32, (STACK, STACK), 0)
    cc = lax.broadcasted_iota(jnp.int32, (STACK, STACK), 1)
    eye = (r == cc)[None]
    cum_col = jnp.sum(jnp.where(eye, cum_row, 0.0), axis=2, keepdims=True)
    beta_col = jnp.sum(jnp.where(eye, beta_row, 0.0), axis=2, keepdims=True)
    same = (r >> 6) == (cc >> 6)
    lower = (same & (r >= cc))[None]
    strict = (same & (r > cc))[None]
    ratio = jnp.exp(jnp.minimum(cum_col - cum_row, 0.0))

    kkt = _bdot_nt(ks, ks)
    qkt = _bdot_nt(qs, ks)
    lm = jnp.where(strict, beta_col * ratio * kkt, 0.0)

    ident = eye.astype(F32)
    l0 = jnp.where(((r >> 3) == (cc >> 3))[None], lm, 0.0)
    l2 = _bdot(l0, l0)
    l4 = _bdot(l2, l2)
    xinv = _bdot(_bdot(ident - l0, ident + l2), ident + l4)
    for lvl in (3, 4, 5):
        off = (((r >> (lvl + 1)) == (cc >> (lvl + 1))) & ((r >> lvl) != (cc >> lvl)))[None]
        xinv = xinv - _bdot(_bdot(xinv, jnp.where(off, lm, 0.0)), xinv)

    decay_col = jnp.exp(cum_col)
    rhs = jnp.concatenate([beta_col * vs, (beta_col * decay_col) * ks], axis=2)
    sol = _bdot(xinv, rhs)

    to_g = functools.partial(_to_heads, bsz=bsz)
    n_g = nb * HEADS
    st = st_ref[...]
    u0_g, w_g, q_g = to_g(sol[:, :, :HEAD_DIM]), to_g(sol[:, :, HEAD_DIM:]), to_g(qs)
    cum_g = to_g(cum_col)
    cum_end = cum_g[:, CHUNK - 1:CHUNK, :]
    kdec = to_g(ks) * jnp.exp(cum_end - cum_g)
    decay_g = to_g(decay_col)
    us, o_inter = [], []
    for j in range(nc):
        sl = slice(j * n_g, (j + 1) * n_g)
        u_j = u0_g[sl] - _bdot(w_g[sl], st)
        us.append(u_j)
        o_inter.append(decay_g[sl] * _bdot(q_g[sl], st))
        st = jnp.exp(cum_end[sl]) * st + _bdot_tn(kdec[sl], u_j)
    st_ref[...] = st
    stout_ref[...] = st
    attn = jnp.where(lower, qkt * ratio, 0.0)
    u = jnp.concatenate(us, axis=0).reshape(bsz * HEADS * CHUNK, HEAD_DIM)
    o = jnp.concatenate(o_inter, axis=0) + to_g(_bdot(attn, to3(u)))
    o = o.reshape(bsz * HEADS * CHUNK, HEAD_DIM)

    o = o * lax.rsqrt(jnp.mean(o * o, axis=-1, keepdims=True) + EPS) * g_ref[...]
    o = o * _stack_heads(gate, bsz)
    _store_chunks(o_ref, _unstack_heads(o, bsz), nb, nc)


def _recurrence_call(kernel_fn, name, proj, state0, operands, specs, nc):
    nb, length, _ = proj.shape
    rows = nc * CHUNK
    state_shape = (nb * HEADS, HEAD_DIM, HEAD_DIM)
    state_spec = pl.BlockSpec(state_shape, lambda c: (0, 0, 0))
    chunk_specs = [pl.BlockSpec((nb, rows, WIDTH), functools.partial(lambda c, j: (0, c, j), j=j))
                   for j in range(4)]
    return pl.pallas_call(
        functools.partial(kernel_fn, nb=nb, nc=nc),
        grid=(length // rows,),
        in_specs=specs + [state_spec] + chunk_specs,
        out_specs=[pl.BlockSpec((nb, rows, WIDTH), lambda c: (0, c, 0)), state_spec],
        out_shape=[jax.ShapeDtypeStruct((nb, length, WIDTH), BF16),
                   jax.ShapeDtypeStruct(state_shape, F32)],
        scratch_shapes=[pltpu.VMEM(state_shape, F32)],
        compiler_params=pltpu.CompilerParams(
            dimension_semantics=("arbitrary",), vmem_limit_bytes=VMEM_LIMIT),
        name=name,
    )(*operands, state0, *([proj] * 4))


def _hgrn(proj, state0, norm_g, nc):
    const = lambda c: (0, 0)
    specs = [pl.BlockSpec((STACK, STACK), const), pl.BlockSpec((1, HEAD_DIM), const)]
    return _recurrence_call(_hgrn_kernel, "hgrn", proj, state0, (_hgrn_level_table(), norm_g), specs, nc)


def _gdn(proj, state0, scal, a_log_row, dt_row, norm_g, nc):
    nb = proj.shape[0]
    const = lambda c: (0, 0)
    const3 = lambda c: (0, 0, 0)
    specs = [pl.BlockSpec((STACK, STACK), const),
             pl.BlockSpec((N_STACKS, 1, STACK), const3),
             pl.BlockSpec((N_STACKS, 1, STACK), const3),
             pl.BlockSpec((1, HEAD_DIM), const),
             pl.BlockSpec((nb, nc, N_STACKS, 2, STACK), lambda c: (0, c, 0, 0, 0))]
    return _recurrence_call(_gdn_kernel, "gdn", proj, state0,
                            (_gdn_cumsum_matrix(), a_log_row, dt_row, norm_g, scal), specs, nc)


ROUTE_LANE0 = N_GROUPS
ROUTE_NEG = -1e30


def _route_tile(lg, ltri, carry):
    lane = lax.broadcasted_iota(jnp.int32, lg.shape, 1).astype(F32)
    first = lambda mask: jnp.min(jnp.where(mask, lane, float(LANES)), axis=1, keepdims=True)
    top = lambda mask: jnp.max(jnp.where(mask, lg, ROUTE_NEG), axis=1, keepdims=True)

    is_g = lane < N_GROUPS
    gmax = top(is_g)
    grp = first(is_g & (lg == gmax))
    p_grp = 1.0 / jnp.sum(jnp.where(is_g, jnp.exp(lg - gmax), 0.0), axis=1, keepdims=True)

    lo = ROUTE_LANE0 + grp * EXPERTS_PER_GROUP
    in_grp = (lane >= lo) & (lane < lo + EXPERTS_PER_GROUP)
    m1 = top(in_grp)
    i1 = first(in_grp & (lg == m1))
    rest = in_grp & (lane != i1)
    m2 = top(rest)
    i2 = first(rest & (lg == m2))
    r = jnp.exp(m2 - m1)
    w0 = p_grp / (1.0 + r)
    w1 = p_grp * r / (1.0 + r)

    hot0 = (lane == i1).astype(F32)
    hot1 = (lane == i2).astype(F32)
    both = hot0 + hot1
    before = jnp.dot(ltri, both.astype(BF16), preferred_element_type=F32) + carry
    rank0 = jnp.sum(hot0 * before, axis=1, keepdims=True)
    rank1 = jnp.sum(hot1 * before, axis=1, keepdims=True)
    cols = (w0, w1, i1 - ROUTE_LANE0, i2 - ROUTE_LANE0, rank0, rank1)
    info = jnp.zeros_like(lg)
    for k, col in enumerate(cols):
        info = jnp.where(lane == k, col, info)
    return info, carry + jnp.sum(both, axis=0, keepdims=True)


def _merge_kernel(oa_ref, ob_ref, x_ref, mg_ref, wgate_ref, hgup_ref, gdup_ref, wout_ref,
                  ng_ref, rw_ref, rb_ref, ltri_ref, h2_ref, xn_ref, info_ref, ids_ref, cnt_ref,
                  wgate_bf, carry_ref):
    @pl.when(pl.program_id(0) == 0)
    def _():
        carry_ref[...] = jnp.zeros_like(carry_ref)
        _cast_rows(wgate_ref, wgate_bf)

    gates = _sigmoid(jnp.dot(_normed_bf16(x_ref, mg_ref), wgate_bf[...], preferred_element_type=F32))
    ua = jnp.dot(oa_ref[...], hgup_ref[...], preferred_element_type=F32)
    ub = jnp.dot(ob_ref[...], gdup_ref[...], preferred_element_type=F32)
    merged = gates[:, :D_MODEL] * ua + gates[:, D_MODEL:] * ub
    h2 = x_ref[...] + jnp.dot(merged.astype(BF16), wout_ref[...], preferred_element_type=F32)
    h2_ref[...] = h2
    xn = h2 * lax.rsqrt(jnp.mean(h2 * h2, axis=-1, keepdims=True) + EPS) * ng_ref[...]
    xn_ref[...] = _pack_halves(xn)
    lg = jnp.dot(xn.astype(BF16), rw_ref[...], preferred_element_type=F32) + rb_ref[...]
    info, carry = _route_tile(lg, ltri_ref[...], carry_ref[...])
    info_ref[...] = info
    ids_ref[...] = info.T[:SUBLANES].astype(jnp.int32)
    carry_ref[...] = carry
    cnt_ref[...] = carry


def _merge(o_a, o_b, x2d, mix_g, w_gates, hg_up, gd_up, w_out, norm_g, rw, rb):
    t = x2d.shape[0]
    tm = TOKEN_TM
    row = lambda i: (i, 0)
    const = lambda i: (0, 0)
    ltri = jnp.asarray(np.tril(np.ones((tm, tm), np.float32), -1), dtype=BF16)
    return pl.pallas_call(
        _merge_kernel,
        grid=(t // tm,),
        in_specs=[pl.BlockSpec((tm, WIDTH), row),
                  pl.BlockSpec((tm, WIDTH), row),
                  pl.BlockSpec((tm, D_MODEL), row),
                  pl.BlockSpec((1, D_MODEL), const),
                  pl.BlockSpec((D_MODEL, 2 * D_MODEL), const, pipeline_mode=pl.Buffered(1)),
                  pl.BlockSpec((WIDTH, D_MODEL), const),
                  pl.BlockSpec((WIDTH, D_MODEL), const),
                  pl.BlockSpec((D_MODEL, D_MODEL), const),
                  pl.BlockSpec((1, D_MODEL), const),
                  pl.BlockSpec((D_MODEL, LANES), const),
                  pl.BlockSpec((1, LANES), const),
                  pl.BlockSpec((tm, tm), const)],
        out_specs=[pl.BlockSpec((tm, D_MODEL), row),
                   pl.BlockSpec((tm, HALF), row),
                   pl.BlockSpec((tm, LANES), row),
                   pl.BlockSpec((SUBLANES, tm), lambda i: (0, i)),
                   pl.BlockSpec((1, LANES), const)],
        out_shape=[jax.ShapeDtypeStruct((t, D_MODEL), F32),
                   jax.ShapeDtypeStruct((t, HALF), U32),
                   jax.ShapeDtypeStruct((t, LANES), F32),
                   jax.ShapeDtypeStruct((SUBLANES, t), jnp.int32),
                   jax.ShapeDtypeStruct((1, LANES), F32)],
        scratch_shapes=[pltpu.VMEM((D_MODEL, 2 * D_MODEL), BF16), pltpu.VMEM((1, LANES), F32)],
        compiler_params=pltpu.CompilerParams(
            dimension_semantics=("arbitrary",), vmem_limit_bytes=VMEM_LIMIT),
        name="merge",
    )(o_a, o_b, x2d, mix_g, w_gates, hg_up, gd_up, w_out, norm_g, rw, rb, ltri)


def _sc_gather(table, idx):
    n_idx = idx.shape[0]
    cols = table.shape[1]
    per_worker = n_idx // SC_WORKERS
    assert n_idx % (SC_WORKERS * SC_CHUNK) == 0
    mesh = plsc.VectorSubcoreMesh(core_axis_name="c", subcore_axis_name="s")

    @functools.partial(
        pl.kernel, mesh=mesh,
        out_type=jax.ShapeDtypeStruct((n_idx, cols), table.dtype),
        scratch_types=[pltpu.VMEM((SC_CHUNK,), jnp.int32),
                       pltpu.VMEM((SC_CHUNK, cols), table.dtype),
                       pltpu.SemaphoreType.DMA],
    )
    def gather(table_hbm, idx_hbm, out_hbm, idx_v, rows_v, sem):
        worker = lax.axis_index("s") * SC_CORES + lax.axis_index("c")
        base = worker * per_worker

        @pl.loop(0, per_worker // SC_CHUNK)
        def _(c):
            off = pl.multiple_of(base + c * SC_CHUNK, SC_CHUNK)
            pltpu.sync_copy(idx_hbm.at[pl.ds(off, SC_CHUNK)], idx_v)
            pltpu.async_copy(table_hbm.at[idx_v], rows_v, sem).wait()
            pltpu.sync_copy(rows_v, out_hbm.at[pl.ds(off, SC_CHUNK)])

    return gather(table, idx)


def _sc_dispatch(rows, dest, n_out):
    t, cols = rows.shape
    per_worker = t // SC_WORKERS
    assert t % (SC_WORKERS * SC_CHUNK) == 0 and dest.shape[0] == 2 * t
    mesh = plsc.VectorSubcoreMesh(core_axis_name="c", subcore_axis_name="s")

    @functools.partial(
        pl.kernel, mesh=mesh,
        out_type=jax.ShapeDtypeStruct((n_out, cols), rows.dtype),
        scratch_types=[pltpu.VMEM((SC_CHUNK,), jnp.int32),
                       pltpu.VMEM((SC_CHUNK, cols), rows.dtype)],
    )
    def dispatch(rows_hbm, dest_hbm, out_hbm, idx_v, rows_v):
        worker = lax.axis_index("s") * SC_CORES + lax.axis_index("c")
        base = worker * per_worker

        @pl.loop(0, per_worker // SC_CHUNK)
        def _(c):
            off = pl.multiple_of(base + c * SC_CHUNK, SC_CHUNK)
            pltpu.sync_copy(rows_hbm.at[pl.ds(off, SC_CHUNK)], rows_v)
            for slot in range(2):
                pltpu.sync_copy(dest_hbm.at[pl.ds(slot * t + off, SC_CHUNK)], idx_v)
                pltpu.sync_copy(rows_v, out_hbm.at[idx_v])

    return dispatch(rows, dest)


def _moe_kernel(be_ref, nu_ref, x_ref, wg_ref, wu_ref, wd_ref, y_ref, wg_bf, wu_bf, wd_bf):
    i = pl.program_id(0)

    @pl.when((i == 0) | (be_ref[i] != be_ref[jnp.maximum(i - 1, 0)]))
    def _():
        _cast_rows(wg_ref, wg_bf)
        _cast_rows(wu_ref, wu_bf)
        _cast_rows(wd_ref, wd_bf)

    @pl.when(i < nu_ref[0])
    def _():
        xb = _unpack_halves(x_ref[...]).astype(BF16)
        a = jnp.dot(xb, wg_bf[...], preferred_element_type=F32)
        u = jnp.dot(xb, wu_bf[...], preferred_element_type=F32)
        y = jnp.dot((_silu(a) * u).astype(BF16), wd_bf[...], preferred_element_type=F32)
        y_ref[...] = _pack_halves(y)

    @pl.when(i >= nu_ref[0])
    def _():
        y_ref[...] = jnp.zeros_like(y_ref)


def _moe(blk_expert, n_used, x_sorted, w_gate, w_up, w_down):
    n_blocks = blk_expert.shape[0]
    wspec = lambda shape: pl.BlockSpec((None,) + shape, lambda i, be, nu: (be[i], 0, 0))
    rows = pl.BlockSpec((MOE_ROWS, HALF), lambda i, be, nu: (i, 0))
    grid_spec = pltpu.PrefetchScalarGridSpec(
        num_scalar_prefetch=2,
        grid=(n_blocks,),
        in_specs=[rows, wspec((D_MODEL, D_FF)), wspec((D_MODEL, D_FF)), wspec((D_FF, D_MODEL))],
        out_specs=rows,
        scratch_shapes=[pltpu.VMEM((D_MODEL, D_FF), BF16), pltpu.VMEM((D_MODEL, D_FF), BF16),
                        pltpu.VMEM((D_FF, D_MODEL), BF16)],
    )
    return pl.pallas_call(
        _moe_kernel,
        grid_spec=grid_spec,
        out_shape=jax.ShapeDtypeStruct(x_sorted.shape, U32),
        compiler_params=pltpu.CompilerParams(
            dimension_semantics=("arbitrary",), vmem_limit_bytes=VMEM_LIMIT),
        name="moe",
    )(blk_expert, n_used, x_sorted, w_gate, w_up, w_down)


def _combine_kernel(h2_ref, rw_ref, g_ref, y0_ref, y1_ref, o_ref):
    rw = rw_ref[...]
    h = h2_ref[...] + rw[:, 0:1] * _unpack_halves(y0_ref[...]) + rw[:, 1:2] * _unpack_halves(y1_ref[...])
    o_ref[...] = h * lax.rsqrt(jnp.mean(h * h, axis=-1, keepdims=True) + EPS) * g_ref[...]


def _combine(h2, rweights, final_g, y):
    t = h2.shape[0]
    tm = min(COMBINE_TM, t)
    row = lambda i: (i, 0)
    return pl.pallas_call(
        _combine_kernel,
        grid=(t // tm,),
        in_specs=[pl.BlockSpec((tm, D_MODEL), row),
                  pl.BlockSpec((tm, LANES), row),
                  pl.BlockSpec((1, D_MODEL), lambda i: (0, 0)),
                  pl.BlockSpec((tm, HALF), row),
                  pl.BlockSpec((tm, HALF), lambda i: (i + t // tm, 0))],
        out_specs=pl.BlockSpec((tm, D_MODEL), row),
        out_shape=jax.ShapeDtypeStruct((t, D_MODEL), F32),
        compiler_params=pltpu.CompilerParams(
            dimension_semantics=("arbitrary",), vmem_limit_bytes=VMEM_LIMIT),
        name="combine",
    )(h2, rweights, final_g, y, y)


def _block_layout(ids, counts_row, t):
    n_blocks = 2 * t // MOE_ROWS + N_EXPERTS
    counts = counts_row[0, ROUTE_LANE0:ROUTE_LANE0 + N_EXPERTS].astype(jnp.int32)
    padded = ((counts + MOE_ROWS - 1) // MOE_ROWS) * MOE_ROWS
    pend = jnp.cumsum(padded)
    pstart = pend - padded
    blk_first = jnp.arange(n_blocks, dtype=jnp.int32) * MOE_ROWS
    blk_expert = jnp.minimum(jnp.sum((pend[None, :] <= blk_first[:, None]).astype(jnp.int32), axis=1),
                             N_EXPERTS - 1).astype(jnp.int32)
    n_used = (pend[-1] // MOE_ROWS).astype(jnp.int32).reshape(1)
    expert, rank = ids[2:4], ids[4:6]
    experts = jnp.arange(N_EXPERTS, dtype=jnp.int32)
    first_row = jnp.sum(jnp.where(expert[:, :, None] == experts, pstart, 0), axis=-1)
    dest = (first_row + rank).reshape(2 * t)
    return dest, blk_expert, n_used, n_blocks * MOE_ROWS


def kernel(x, meta_tokens, hg_lb_logits, norm_mix_g, w_in, gd_conv_w, gd_A_log, gd_dt_bias, hg_norm_g, gd_norm_g, hg_up, gd_up, w_out, norm_ffn_g, router_group_w, router_group_b, router_expert_w, router_expert_b, w_gate, w_up, w_down, final_norm_g):
    bsz, seq, d = x.shape
    t = bsz * seq
    x2d = x.reshape(t, d)

    lb = jnp.cumsum(jax.nn.softmax(hg_lb_logits.astype(F32), axis=0), axis=0)[0].reshape(1, WIDTH)
    w_all = w_in.astype(F32)
    w_gates = w_all[0, :, W_SMALL + 2 * HEADS:]
    g_mix = norm_mix_g[0].reshape(1, d)
    meta_blk = jnp.concatenate([jnp.zeros((CHUNK - N_META, d), F32), meta_tokens.astype(F32)], axis=0)

    conv_w = gd_conv_w[0].astype(F32)
    no_history = jnp.zeros((SUBLANES, 3 * WIDTH), F32)
    proj, proj_meta = {}, {}
    proj["hgrn"] = _norm_proj_hgrn(x2d, g_mix, lb, w_all, PROJ_TM)
    proj_meta["hgrn"] = _norm_proj_hgrn(meta_blk, g_mix, lb, w_all, CHUNK)
    proj_meta["gdn"], meta_tail = _norm_proj_conv(meta_blk, g_mix, w_all, conv_w, no_history, CHUNK, CHUNK)
    proj["gdn"], _ = _norm_proj_conv(x2d, g_mix, w_all, conv_w, meta_tail, PROJ_TM, seq)

    def scalar_rows(p, nb, nc):
        s = p[:, 4 * WIDTH:4 * WIDTH + 2 * HEADS].reshape(nb, nc, CHUNK, 2, N_STACKS, HEADS_PER_STACK)
        return s.transpose(0, 1, 4, 3, 5, 2).reshape(nb, nc, N_STACKS, 2, STACK)

    a_log_row = jnp.repeat(gd_A_log[0].astype(F32), CHUNK).reshape(N_STACKS, 1, STACK)
    dt_row = jnp.repeat(gd_dt_bias[0].astype(F32), CHUNK).reshape(N_STACKS, 1, STACK)
    hg_g, gd_g = hg_norm_g[0].reshape(1, HEAD_DIM), gd_norm_g[0].reshape(1, HEAD_DIM)

    zero_state = jnp.zeros((HEADS, HEAD_DIM, HEAD_DIM), F32)
    per_seq = lambda st: jnp.tile(st, (bsz, 1, 1))
    _, hg_state = _hgrn(proj_meta["hgrn"][None], zero_state, hg_g, 1)
    o_a, _ = _hgrn(proj["hgrn"].reshape(bsz, seq, -1), per_seq(hg_state), hg_g, HGRN_CHUNKS)
    _, gd_state = _gdn(proj_meta["gdn"][None], zero_state, scalar_rows(proj_meta["gdn"], 1, 1),
                       a_log_row, dt_row, gd_g, 1)
    o_b, _ = _gdn(proj["gdn"].reshape(bsz, seq, -1), per_seq(gd_state),
                  scalar_rows(proj["gdn"], bsz, seq // CHUNK), a_log_row, dt_row, gd_g, GDN_CHUNKS)

    rw = jnp.zeros((d, LANES), F32)
    rw = rw.at[:, :N_GROUPS].set(router_group_w[0]).at[:, N_GROUPS:N_GROUPS + N_EXPERTS].set(router_expert_w[0])
    rb = jnp.zeros((1, LANES), F32)
    rb = rb.at[0, :N_GROUPS].set(router_group_b[0]).at[0, N_GROUPS:N_GROUPS + N_EXPERTS].set(router_expert_b[0])
    h2, xn, info, ids, counts = _merge(o_a.reshape(t, WIDTH), o_b.reshape(t, WIDTH), x2d, g_mix, w_gates,
                                  hg_up[0].astype(BF16), gd_up[0].astype(BF16), w_out[0].astype(BF16),
                                  norm_ffn_g[0].reshape(1, d), rw.astype(BF16), rb)

    dest, blk_expert, n_used, n_rows = _block_layout(ids, counts, t)
    x_sorted = _sc_dispatch(xn, dest, n_rows)
    y_sorted = _moe(blk_expert, n_used, x_sorted,
                    w_gate[0].astype(F32), w_up[0].astype(F32), w_down[0].astype(F32))
    y_tok = _sc_gather(y_sorted, dest)
    out = _combine(h2, info, final_norm_g.reshape(1, d), y_tok)
    return out.reshape(bsz, seq, d)
```

```python
import functools

import numpy as np
import jax
import jax.numpy as jnp
from jax import lax
from jax.experimental import pallas as pl
from jax.experimental.pallas import tpu as pltpu
from jax.experimental.pallas import tpu_sc as plsc

F32 = jnp.float32
BF16 = jnp.bfloat16
U32 = jnp.uint32

D_MODEL = 1024
N_META = 16
CHUNK = 64
EPS = 1e-6
HEADS = 4
HEAD_DIM = 128
WIDTH = HEADS * HEAD_DIM
HEADS_PER_STACK = 2
STACK = HEADS_PER_STACK * CHUNK
N_STACKS = HEADS // HEADS_PER_STACK
CONV_W = 4
N_GROUPS = 4
EXPERTS_PER_GROUP = 8
N_EXPERTS = N_GROUPS * EXPERTS_PER_GROUP
D_FF = 512
LANES = 128
SUBLANES = 8
HALF = D_MODEL // 2
MOE_ROWS = 512
CAST_SLAB = 128
SC_CORES, SC_SUBCORES = 2, 16
SC_WORKERS = SC_CORES * SC_SUBCORES
SC_CHUNK = 128
PROJ_TM = 1024
W_HGRN, W_GDN, W_SMALL = 0, 4 * WIDTH, 8 * WIDTH
HGRN_CHUNKS, GDN_CHUNKS = 4, 2
TOKEN_TM = 512
COMBINE_TM = 1024
VMEM_LIMIT = 56 * 1024 * 1024


def _bdot(a, b):
    return lax.dot_general(a.astype(BF16), b.astype(BF16), (((2,), (1,)), ((0,), (0,))),
                           preferred_element_type=F32)


def _bdot_nt(a, b):
    return lax.dot_general(a.astype(BF16), b.astype(BF16), (((2,), (2,)), ((0,), (0,))),
                           preferred_element_type=F32)


def _bdot_tn(a, b):
    return lax.dot_general(a.astype(BF16), b.astype(BF16), (((1,), (1,)), ((0,), (0,))),
                           preferred_element_type=F32)


def _split3(x):
    hi = x.astype(BF16)
    r1 = x - hi.astype(F32)
    mid = r1.astype(BF16)
    lo = (r1 - mid.astype(F32)).astype(BF16)
    return hi, mid, lo


def _sigmoid(x):
    return 1.0 / (1.0 + jnp.exp(-x))


def _silu(x):
    return x * _sigmoid(x)


def _softplus(x):
    return jnp.maximum(x, 0.0) + jnp.log(1.0 + jnp.exp(-jnp.abs(x)))


def _stack_heads(x, bsz):
    x3 = x.reshape(bsz, CHUNK, WIDTH)
    y = jnp.concatenate([x3[:, :, h * HEAD_DIM:(h + 1) * HEAD_DIM] for h in range(HEADS)], axis=1)
    return y.reshape(bsz * HEADS * CHUNK, HEAD_DIM)


def _unstack_heads(x, bsz):
    x3 = x.reshape(bsz, HEADS * CHUNK, HEAD_DIM)
    return jnp.concatenate([x3[:, h * CHUNK:(h + 1) * CHUNK, :] for h in range(HEADS)], axis=2)


def _to_stacks(a, bsz):
    return a.reshape(bsz * N_STACKS, STACK, a.shape[-1])


def _to_heads(a, bsz):
    return a.reshape(bsz * HEADS, CHUNK, a.shape[-1])


def _roll_rows(x, shift):
    n = x.shape[0]
    return pltpu.roll(x, shift % n, axis=0)


def _chunks_first(x, nc):
    return jnp.concatenate([x[:, j * CHUNK:(j + 1) * CHUNK] for j in range(nc)], axis=0)


def _chunks_first_scalars(s, nc):
    return jnp.concatenate([s[:, j] for j in range(nc)], axis=0)


def _load_chunks(ref, nc):
    x = _chunks_first(ref[...], nc)
    return x.reshape(x.shape[0] * CHUNK, WIDTH)


def _store_chunks(ref, x, nb, nc):
    ref[...] = jnp.concatenate([x[j * nb:(j + 1) * nb] for j in range(nc)], axis=1).astype(ref.dtype)


def _cast_rows(src_ref, dst_ref, col0=0):
    slab = CAST_SLAB
    rows, cols = src_ref.shape
    assert rows % slab == 0

    def body(k, carry):
        r = pl.multiple_of(k * slab, slab)
        dst_ref[pl.ds(r, slab), col0:col0 + cols] = src_ref[pl.ds(r, slab), :].astype(dst_ref.dtype)
        return carry

    lax.fori_loop(0, rows // slab, body, 0)


def _pack_halves(x):
    lo = lax.bitcast_convert_type(x[:, :HALF].astype(BF16).astype(F32), U32)
    hi = lax.bitcast_convert_type(x[:, HALF:].astype(BF16).astype(F32), U32)
    return (lo >> 16) | (hi & jnp.uint32(0xFFFF0000))


def _unpack_halves(p):
    lo = lax.bitcast_convert_type(p << 16, F32)
    hi = lax.bitcast_convert_type(p & jnp.uint32(0xFFFF0000), F32)
    return jnp.concatenate([lo, hi], axis=1)


def _normed_bf16(x_ref, g_ref):
    x = x_ref[...]
    return (x * lax.rsqrt(jnp.mean(x * x, axis=-1, keepdims=True) + EPS) * g_ref[...]).astype(BF16)


def _norm_proj_hgrn_kernel(x_ref, g_ref, lb_ref, w_ref, o_ref, wbf_ref):
    @pl.when(pl.program_id(0) == 0)
    def _():
        _cast_rows(w_ref, wbf_ref)

    acc = jnp.dot(_normed_bf16(x_ref, g_ref), wbf_ref[...], preferred_element_type=F32)
    w1, w2, w3 = WIDTH, 2 * WIDTH, 3 * WIDTH
    lb = lb_ref[...]
    o_ref[:, :w1] = _silu(acc[:, :w1])
    o_ref[:, w1:w2] = lb + (1.0 - lb) * _sigmoid(acc[:, w1:w2])
    o_ref[:, w2:w3] = acc[:, w2:w3]
    o_ref[:, w3:] = _silu(acc[:, w3:])


def _zero_rows(ref):
    slab = min(CAST_SLAB, ref.shape[0])
    assert ref.shape[0] % slab == 0

    def body(k, carry):
        ref[pl.ds(pl.multiple_of(k * slab, slab), slab), :] = jnp.zeros((slab, ref.shape[1]), ref.dtype)
        return carry

    lax.fori_loop(0, ref.shape[0] // slab, body, 0)


def _norm_proj_conv_kernel(x_ref, g_ref, w_ref, ws_ref, cw_ref, hist_ref, o_ref, tail_ref,
                           wbf_ref, acc_ref, carry_ref, *, steps_per_seq):
    i = pl.program_id(0)

    @pl.when(i == 0)
    def _():
        _cast_rows(w_ref, wbf_ref)
        _cast_rows(ws_ref, wbf_ref, 4 * WIDTH)
        _zero_rows(acc_ref)
        carry_ref[...] = jnp.zeros_like(carry_ref)

    xn = _normed_bf16(x_ref, g_ref)
    w3, w4 = 3 * WIDTH, 4 * WIDTH
    rows = x_ref.shape[0]
    first_of_seq = (i - 1) % steps_per_seq == 0
    cw = cw_ref[...]
    r8 = lax.broadcasted_iota(jnp.int32, (SUBLANES, WIDTH), 0)
    for c0 in range(0, w3, WIDTH):
        cols = slice(c0, c0 + WIDTH)
        raw = acc_ref[:, cols]
        hist = jnp.where(first_of_seq, hist_ref[:, cols], carry_ref[:, cols])
        tap = lambda d: cw[CONV_W - 1 - d:CONV_W - d, cols]
        body = raw * tap(0)
        head = raw[:SUBLANES]
        first = head * tap(0)
        for d in range(1, CONV_W):
            body = body + _roll_rows(raw, d) * tap(d)
            first = first + jnp.where(r8 >= d, _roll_rows(head, d), _roll_rows(hist, d)) * tap(d)
        o_ref[:SUBLANES, cols] = _silu(first)
        o_ref[SUBLANES:, cols] = _silu(body[SUBLANES:])
        carry_ref[:, cols] = raw[rows - SUBLANES:]
        acc_ref[:, cols] = jnp.dot(xn, wbf_ref[:, cols], preferred_element_type=F32)
    rest = acc_ref[:, w3:]
    o_ref[:, w3:w4] = _silu(rest[:, :WIDTH])
    o_ref[:, w4:] = rest[:, WIDTH:]
    acc_ref[:, w3:] = jnp.dot(xn, wbf_ref[:, w3:], preferred_element_type=F32)
    tail_ref[...] = carry_ref[...]


def _proj_call(kernel_fn, name, x2d, tm, n, operands, specs, extra_out=(), extra_scratch=(), delayed=False):
    m, k = x2d.shape
    tm = min(tm, m)
    tiles = m // tm
    x_map = (lambda i: (jnp.minimum(i, tiles - 1), 0)) if delayed else (lambda i: (i, 0))
    o_map = (lambda i: (jnp.maximum(i - 1, 0), 0)) if delayed else (lambda i: (i, 0))
    out_specs = [pl.BlockSpec((tm, n), o_map)] + [s for s, _ in extra_out]
    out_shape = [jax.ShapeDtypeStruct((m, n), F32)] + [s for _, s in extra_out]
    return pl.pallas_call(
        kernel_fn,
        grid=(tiles + int(delayed),),
        in_specs=[pl.BlockSpec((tm, k), x_map)] + specs,
        out_specs=out_specs,
        out_shape=out_shape,
        scratch_shapes=[pltpu.VMEM((k, n), BF16)] + list(extra_scratch),
        compiler_params=pltpu.CompilerParams(
            dimension_semantics=("arbitrary",), vmem_limit_bytes=VMEM_LIMIT),
        name=name,
    )(x2d, *operands)


def _weight_spec(k, n, col0):
    assert col0 % n == 0
    return pl.BlockSpec((None, k, n), lambda i: (0, 0, col0 // n), pipeline_mode=pl.Buffered(1))


def _norm_proj_hgrn(x2d, g, lb, w, tm):
    k, n = w.shape[1], 4 * WIDTH
    const = lambda i: (0, 0)
    specs = [pl.BlockSpec((1, k), const), pl.BlockSpec((1, WIDTH), const), _weight_spec(k, n, W_HGRN)]
    return _proj_call(_norm_proj_hgrn_kernel, "norm_proj_hgrn", x2d, tm, n, (g, lb, w), specs)[0]


def _norm_proj_conv(x2d, g, w, conv_w, hist, tm, rows_per_seq):
    k, n = w.shape[1], 4 * WIDTH + LANES
    const = lambda i: (0, 0)
    tm = min(tm, rows_per_seq)
    assert rows_per_seq % tm == 0 and x2d.shape[0] % rows_per_seq == 0
    tail_shape = (SUBLANES, 3 * WIDTH)
    specs = [pl.BlockSpec((1, k), const), _weight_spec(k, 4 * WIDTH, W_GDN), _weight_spec(k, LANES, W_SMALL),
             pl.BlockSpec((CONV_W, 3 * WIDTH), const), pl.BlockSpec(tail_shape, const)]
    return _proj_call(functools.partial(_norm_proj_conv_kernel, steps_per_seq=rows_per_seq // tm),
                      "norm_proj_gdn", x2d, tm, n, (g, w, w, conv_w, hist), specs,
                      extra_out=[(pl.BlockSpec(tail_shape, const), jax.ShapeDtypeStruct(tail_shape, F32))],
                      extra_scratch=[pltpu.VMEM((tm, n), F32), pltpu.VMEM(tail_shape, F32)], delayed=True)


HGRN_LEVELS = 6
HGRN_DIAG = HGRN_LEVELS


def _hgrn_level_table():
    r = np.arange(STACK)
    t, hd = r % CHUNK, r // CHUNK
    x = t[:, None] ^ t[None, :]
    lv = np.floor(np.log2(np.maximum(x, 1))).astype(np.int32)
    valid = (hd[:, None] == hd[None, :]) & (t[:, None] > t[None, :])
    lv = np.where(valid, lv, -1)
    lv = np.where(r[:, None] == r[None, :], HGRN_DIAG, lv)
    return jnp.asarray(lv, dtype=jnp.int32)


def _hgrn_kernel(lv_ref, g_ref, st0_ref, q_ref, f_ref, v_ref, gate_ref, o_ref, stout_ref, st_ref, *, nb, nc):
    @pl.when(pl.program_id(0) == 0)
    def _():
        st_ref[...] = st0_ref[...]

    bsz = nc * nb
    q = _load_chunks(q_ref, nc)
    f = _load_chunks(f_ref, nc)
    v = _load_chunks(v_ref, nc)
    gate = _load_chunks(gate_ref, nc)

    per_seq = lambda a: a.reshape(bsz, CHUNK, WIDTH)
    trow = lax.broadcasted_iota(jnp.int32, (1, CHUNK, WIDTH), 1)
    b = jnp.log(f)
    s = 1
    while s < CHUNK:
        b = (per_seq(b) + jnp.where(trow >= s, per_seq(_roll_rows(b, s)), 0.0)).reshape(b.shape)
        s *= 2

    qs, ks, vs, bs = (_stack_heads(a, bsz) for a in (q, 1.0 - f, v, b))
    lv = lv_ref[...][None]
    to3 = functools.partial(_to_stacks, bsz=bsz)
    to_g = functools.partial(_to_heads, bsz=bsz)
    roll_g = lambda a, shift: to_g(_roll_rows(a.reshape(qs.shape), shift))
    t = lax.broadcasted_iota(jnp.int32, (1, CHUNK, HEAD_DIM), 1)
    qs_g, ks_g, bs_g = to_g(qs), to_g(ks), to_g(bs)

    attn = jnp.where(lv == HGRN_DIAG, _bdot_nt(to3(qs), to3(ks)), 0.0)
    bref = bs_g
    bnext = roll_g(bs_g, -1)
    for li in range(HGRN_LEVELS):
        m = 1 << li
        if li > 0:
            half = m // 2
            upper = (t & (m - 1)) >= half
            bref = jnp.where(upper, roll_g(bref, half), bref)
            bnext = jnp.where(upper, bnext, roll_g(bnext, -half))
        odd = ((t >> li) & 1) == 1
        both = jnp.where(odd, qs_g, ks_g) * jnp.exp(jnp.where(odd, bs_g - bref, bnext - bs_g))
        both = to3(both.reshape(qs.shape)).astype(BF16)
        attn = jnp.where(lv == li, _bdot_nt(both, both), attn)

    o = _bdot(attn, to3(vs))

    n_g = nb * HEADS
    st = st_ref[...]
    qe = qs_g * jnp.exp(bs_g)
    b_end = bs_g[:, CHUNK - 1:CHUNK, :]
    ke = ks_g * jnp.exp(b_end - bs_g)
    vs_g = to_g(vs)
    o_inter = []
    for j in range(nc):
        sl = slice(j * n_g, (j + 1) * n_g)
        o_inter.append(_bdot_nt(qe[sl], st))
        st = st * jnp.exp(b_end[sl]) + _bdot_tn(vs_g[sl], ke[sl])
    st_ref[...] = st
    stout_ref[...] = st
    o = (to_g(o) + jnp.concatenate(o_inter, axis=0)).reshape(qs.shape)

    o = o * lax.rsqrt(jnp.mean(o * o, axis=-1, keepdims=True) + EPS) * g_ref[...]
    o = o * _stack_heads(gate, bsz)
    _store_chunks(o_ref, _unstack_heads(o, bsz), nb, nc)


def _gdn_cumsum_matrix():
    r = np.arange(STACK)
    u = (r[:, None] // CHUNK == r[None, :] // CHUNK) & (r[:, None] <= r[None, :])
    return jnp.asarray(u, dtype=BF16)


def _gdn_kernel(u_ref, alog_ref, dt_ref, g_ref, s_ref, st0_ref, q_ref, k_ref, v_ref, z_ref,
                o_ref, stout_ref, st_ref, *, nb, nc):
    @pl.when(pl.program_id(0) == 0)
    def _():
        st_ref[...] = st0_ref[...]

    bsz = nc * nb
    qs = _stack_heads(_load_chunks(q_ref, nc), bsz)
    ks = _stack_heads(_load_chunks(k_ref, nc), bsz)
    vs = _stack_heads(_load_chunks(v_ref, nc), bsz)
    gate = _load_chunks(z_ref, nc)
    qs = qs * lax.rsqrt(jnp.sum(qs * qs, axis=-1, keepdims=True) + EPS) * (HEAD_DIM ** -0.5)
    ks = ks * lax.rsqrt(jnp.sum(ks * ks, axis=-1, keepdims=True) + EPS)
    to3 = functools.partial(_to_stacks, bsz=bsz)
    qs, ks, vs = to3(qs), to3(ks), to3(vs)
    n_st = bsz * N_STACKS

    srow = _chunks_first_scalars(s_ref[...], nc).reshape(n_st, 2, STACK)
    per_stack = lambda ref: jnp.broadcast_to(ref[...][None], (bsz, N_STACKS, 1, STACK)).reshape(n_st, 1, STACK)
    beta_row = _sigmoid(srow[:, 0:1, :])
    g_row = -jnp.exp(per_stack(alog_ref)) * _softplus(srow[:, 1:2, :] + per_stack(dt_ref))
    n8 = n_st * SUBLANES
    g8 = jnp.broadcast_to(g_row, (n_st, SUBLANES, STACK)).reshape(n8, STACK)
    pieces = jnp.concatenate(_split3(g8), axis=0)
    cum = jnp.dot(pieces, u_ref[...], preferred_element_type=F32)
    cum = cum[0:n8] + cum[n8:2 * n8] + cum[2 * n8:]
    cum_row = cum.reshape(n_st, SUBLANES, STACK)[:, 0:1, :]

    r = lax.broadcasted_iota(jnp.int32, (STACK, STACK), 0)
    cc = lax.broadcasted_iota(jnp.int32, (STACK, STACK), 1)
    eye = (r == cc)[None]
    cum_col = jnp.sum(jnp.where(eye, cum_row, 0.0), axis=2, keepdims=True)
    beta_col = jnp.sum(jnp.where(eye, beta_row, 0.0), axis=2, keepdims=True)
    same = (r >> 6) == (cc >> 6)
    lower = (same & (r >= cc))[None]
    strict = (same & (r > cc))[None]
    ratio = jnp.exp(jnp.minimum(cum_col - cum_row, 0.0))

    kkt = _bdot_nt(ks, ks)
    qkt = _bdot_nt(qs, ks)
    lm = jnp.where(strict, beta_col * ratio * kkt, 0.0)

    ident = eye.astype(F32)
    l0 = jnp.where(((r >> 3) == (cc >> 3))[None], lm, 0.0)
    l2 = _bdot(l0, l0)
    l4 = _bdot(l2, l2)
    xinv = _bdot(_bdot(ident - l0, ident + l2), ident + l4)
    for lvl in (3, 4, 5):
        off = (((r >> (lvl + 1)) == (cc >> (lvl + 1))) & ((r >> lvl) != (cc >> lvl)))[None]
        xinv = xinv - _bdot(_bdot(xinv, jnp.where(off, lm, 0.0)), xinv)

    decay_col = jnp.exp(cum_col)
    rhs = jnp.concatenate([beta_col * vs, (beta_col * decay_col) * ks], axis=2)
    sol = _bdot(xinv, rhs)

    to_g = functools.partial(_to_heads, bsz=bsz)
    n_g = nb * HEADS
    st = st_ref[...]
    u0_g, w_g, q_g = to_g(sol[:, :, :HEAD_DIM]), to_g(sol[:, :, HEAD_DIM:]), to_g(qs)
    cum_g = to_g(cum_col)
    cum_end = cum_g[:, CHUNK - 1:CHUNK, :]
    kdec = to_g(ks) * jnp.exp(cum_end - cum_g)
    decay_g = to_g(decay_col)
    us, o_inter = [], []
    for j in range(nc):
        sl = slice(j * n_g, (j + 1) * n_g)
        u_j = u0_g[sl] - _bdot(w_g[sl], st)
        us.append(u_j)
        o_inter.append(decay_g[sl] * _bdot(q_g[sl], st))
        st = jnp.exp(cum_end[sl]) * st + _bdot_tn(kdec[sl], u_j)
    st_ref[...] = st
    stout_ref[...] = st
    attn = jnp.where(lower, qkt * ratio, 0.0)
    u = jnp.concatenate(us, axis=0).reshape(bsz * HEADS * CHUNK, HEAD_DIM)
    o = jnp.concatenate(o_inter, axis=0) + to_g(_bdot(attn, to3(u)))
    o = o.reshape(bsz * HEADS * CHUNK, HEAD_DIM)

    o = o * lax.rsqrt(jnp.mean(o * o, axis=-1, keepdims=True) + EPS) * g_ref[...]
    o = o * _stack_heads(gate, bsz)
    _store_chunks(o_ref, _unstack_heads(o, bsz), nb, nc)


def _recurrence_call(kernel_fn, name, proj, state0, operands, specs, nc):
    nb, length, _ = proj.shape
    rows = nc * CHUNK
    state_shape = (nb * HEADS, HEAD_DIM, HEAD_DIM)
    state_spec = pl.BlockSpec(state_shape, lambda c: (0, 0, 0))
    chunk_specs = [pl.BlockSpec((nb, rows, WIDTH), functools.partial(lambda c, j: (0, c, j), j=j))
                   for j in range(4)]
    return pl.pallas_call(
        functools.partial(kernel_fn, nb=nb, nc=nc),
        grid=(length // rows,),
        in_specs=specs + [state_spec] + chunk_specs,
        out_specs=[pl.BlockSpec((nb, rows, WIDTH), lambda c: (0, c, 0)), state_spec],
        out_shape=[jax.ShapeDtypeStruct((nb, length, WIDTH), BF16),
                   jax.ShapeDtypeStruct(state_shape, F32)],
        scratch_shapes=[pltpu.VMEM(state_shape, F32)],
        compiler_params=pltpu.CompilerParams(
            dimension_semantics=("arbitrary",), vmem_limit_bytes=VMEM_LIMIT),
        name=name,
    )(*operands, state0, *([proj] * 4))


def _hgrn(proj, state0, norm_g, nc):
    const = lambda c: (0, 0)
    specs = [pl.BlockSpec((STACK, STACK), const), pl.BlockSpec((1, HEAD_DIM), const)]
    return _recurrence_call(_hgrn_kernel, "hgrn", proj, state0, (_hgrn_level_table(), norm_g), specs, nc)


def _gdn(proj, state0, scal, a_log_row, dt_row, norm_g, nc):
    nb = proj.shape[0]
    const = lambda c: (0, 0)
    const3 = lambda c: (0, 0, 0)
    specs = [pl.BlockSpec((STACK, STACK), const),
             pl.BlockSpec((N_STACKS, 1, STACK), const3),
             pl.BlockSpec((N_STACKS, 1, STACK), const3),
             pl.BlockSpec((1, HEAD_DIM), const),
             pl.BlockSpec((nb, nc, N_STACKS, 2, STACK), lambda c: (0, c, 0, 0, 0))]
    return _recurrence_call(_gdn_kernel, "gdn", proj, state0,
                            (_gdn_cumsum_matrix(), a_log_row, dt_row, norm_g, scal), specs, nc)


ROUTE_LANE0 = N_GROUPS
ROUTE_NEG = -1e30


def _route_tile(lg, ltri, carry):
    lane = lax.broadcasted_iota(jnp.int32, lg.shape, 1).astype(F32)
    first = lambda mask: jnp.min(jnp.where(mask, lane, float(LANES)), axis=1, keepdims=True)
    top = lambda mask: jnp.max(jnp.where(mask, lg, ROUTE_NEG), axis=1, keepdims=True)

    is_g = lane < N_GROUPS
    gmax = top(is_g)
    grp = first(is_g & (lg == gmax))
    p_grp = 1.0 / jnp.sum(jnp.where(is_g, jnp.exp(lg - gmax), 0.0), axis=1, keepdims=True)

    lo = ROUTE_LANE0 + grp * EXPERTS_PER_GROUP
    in_grp = (lane >= lo) & (lane < lo + EXPERTS_PER_GROUP)
    m1 = top(in_grp)
    i1 = first(in_grp & (lg == m1))
    rest = in_grp & (lane != i1)
    m2 = top(rest)
    i2 = first(rest & (lg == m2))
    r = jnp.exp(m2 - m1)
    w0 = p_grp / (1.0 + r)
    w1 = p_grp * r / (1.0 + r)

    hot0 = (lane == i1).astype(F32)
    hot1 = (lane == i2).astype(F32)
    both = hot0 + hot1
    before = jnp.dot(ltri, both.astype(BF16), preferred_element_type=F32) + carry
    rank0 = jnp.sum(hot0 * before, axis=1, keepdims=True)
    rank1 = jnp.sum(hot1 * before, axis=1, keepdims=True)
    cols = (w0, w1, i1 - ROUTE_LANE0, i2 - ROUTE_LANE0, rank0, rank1)
    info = jnp.zeros_like(lg)
    for k, col in enumerate(cols):
        info = jnp.where(lane == k, col, info)
    return info, carry + jnp.sum(both, axis=0, keepdims=True)


def _merge_kernel(oa_ref, ob_ref, x_ref, mg_ref, wgate_ref, hgup_ref, gdup_ref, wout_ref,
                  ng_ref, rw_ref, rb_ref, ltri_ref, h2_ref, xn_ref, info_ref, ids_ref, cnt_ref,
                  wgate_bf, carry_ref):
    @pl.when(pl.program_id(0) == 0)
    def _():
        carry_ref[...] = jnp.zeros_like(carry_ref)
        _cast_rows(wgate_ref, wgate_bf)

    gates = _sigmoid(jnp.dot(_normed_bf16(x_ref, mg_ref), wgate_bf[...], preferred_element_type=F32))
    ua = jnp.dot(oa_ref[...], hgup_ref[...], preferred_element_type=F32)
    ub = jnp.dot(ob_ref[...], gdup_ref[...], preferred_element_type=F32)
    merged = gates[:, :D_MODEL] * ua + gates[:, D_MODEL:] * ub
    h2 = x_ref[...] + jnp.dot(merged.astype(BF16), wout_ref[...], preferred_element_type=F32)
    h2_ref[...] = h2
    xn = h2 * lax.rsqrt(jnp.mean(h2 * h2, axis=-1, keepdims=True) + EPS) * ng_ref[...]
    xn_ref[...] = _pack_halves(xn)
    lg = jnp.dot(xn.astype(BF16), rw_ref[...], preferred_element_type=F32) + rb_ref[...]
    info, carry = _route_tile(lg, ltri_ref[...], carry_ref[...])
    info_ref[...] = info
    ids_ref[...] = info.T[:SUBLANES].astype(jnp.int32)
    carry_ref[...] = carry
    cnt_ref[...] = carry


def _merge(o_a, o_b, x2d, mix_g, w_gates, hg_up, gd_up, w_out, norm_g, rw, rb):
    t = x2d.shape[0]
    tm = TOKEN_TM
    row = lambda i: (i, 0)
    const = lambda i: (0, 0)
    ltri = jnp.asarray(np.tril(np.ones((tm, tm), np.float32), -1), dtype=BF16)
    return pl.pallas_call(
        _merge_kernel,
        grid=(t // tm,),
        in_specs=[pl.BlockSpec((tm, WIDTH), row),
                  pl.BlockSpec((tm, WIDTH), row),
                  pl.BlockSpec((tm, D_MODEL), row),
                  pl.BlockSpec((1, D_MODEL), const),
                  pl.BlockSpec((D_MODEL, 2 * D_MODEL), const, pipeline_mode=pl.Buffered(1)),
                  pl.BlockSpec((WIDTH, D_MODEL), const),
                  pl.BlockSpec((WIDTH, D_MODEL), const),
                  pl.BlockSpec((D_MODEL, D_MODEL), const),
                  pl.BlockSpec((1, D_MODEL), const),
                  pl.BlockSpec((D_MODEL, LANES), const),
                  pl.BlockSpec((1, LANES), const),
                  pl.BlockSpec((tm, tm), const)],
        out_specs=[pl.BlockSpec((tm, D_MODEL), row),
                   pl.BlockSpec((tm, HALF), row),
                   pl.BlockSpec((tm, LANES), row),
                   pl.BlockSpec((SUBLANES, tm), lambda i: (0, i)),
                   pl.BlockSpec((1, LANES), const)],
        out_shape=[jax.ShapeDtypeStruct((t, D_MODEL), F32),
                   jax.ShapeDtypeStruct((t, HALF), U32),
                   jax.ShapeDtypeStruct((t, LANES), F32),
                   jax.ShapeDtypeStruct((SUBLANES, t), jnp.int32),
                   jax.ShapeDtypeStruct((1, LANES), F32)],
        scratch_shapes=[pltpu.VMEM((D_MODEL, 2 * D_MODEL), BF16), pltpu.VMEM((1, LANES), F32)],
        compiler_params=pltpu.CompilerParams(
            dimension_semantics=("arbitrary",), vmem_limit_bytes=VMEM_LIMIT),
        name="merge",
    )(o_a, o_b, x2d, mix_g, w_gates, hg_up, gd_up, w_out, norm_g, rw, rb, ltri)


def _sc_gather(table, idx):
    n_idx = idx.shape[0]
    cols = table.shape[1]
    per_worker = n_idx // SC_WORKERS
    assert n_idx % (SC_WORKERS * SC_CHUNK) == 0
    mesh = plsc.VectorSubcoreMesh(core_axis_name="c", subcore_axis_name="s")

    @functools.partial(
        pl.kernel, mesh=mesh,
        out_type=jax.ShapeDtypeStruct((n_idx, cols), table.dtype),
        scratch_types=[pltpu.VMEM((SC_CHUNK,), jnp.int32),
                       pltpu.VMEM((SC_CHUNK, cols), table.dtype),
                       pltpu.SemaphoreType.DMA],
    )
    def gather(table_hbm, idx_hbm, out_hbm, idx_v, rows_v, sem):
        worker = lax.axis_index("s") * SC_CORES + lax.axis_index("c")
        base = worker * per_worker

        @pl.loop(0, per_worker // SC_CHUNK)
        def _(c):
            off = pl.multiple_of(base + c * SC_CHUNK, SC_CHUNK)
            pltpu.sync_copy(idx_hbm.at[pl.ds(off, SC_CHUNK)], idx_v)
            pltpu.async_copy(table_hbm.at[idx_v], rows_v, sem).wait()
            pltpu.sync_copy(rows_v, out_hbm.at[pl.ds(off, SC_CHUNK)])

    return gather(table, idx)


def _sc_dispatch(rows, dest, n_out):
    t, cols = rows.shape
    per_worker = t // SC_WORKERS
    assert t % (SC_WORKERS * SC_CHUNK) == 0 and dest.shape[0] == 2 * t
    mesh = plsc.VectorSubcoreMesh(core_axis_name="c", subcore_axis_name="s")

    @functools.partial(
        pl.kernel, mesh=mesh,
        out_type=jax.ShapeDtypeStruct((n_out, cols), rows.dtype),
        scratch_types=[pltpu.VMEM((SC_CHUNK,), jnp.int32),
                       pltpu.VMEM((SC_CHUNK, cols), rows.dtype)],
    )
    def dispatch(rows_hbm, dest_hbm, out_hbm, idx_v, rows_v):
        worker = lax.axis_index("s") * SC_CORES + lax.axis_index("c")
        base = worker * per_worker

        @pl.loop(0, per_worker // SC_CHUNK)
        def _(c):
            off = pl.multiple_of(base + c * SC_CHUNK, SC_CHUNK)
            pltpu.sync_copy(rows_hbm.at[pl.ds(off, SC_CHUNK)], rows_v)
            for slot in range(2):
                pltpu.sync_copy(dest_hbm.at[pl.ds(slot * t + off, SC_CHUNK)], idx_v)
                pltpu.sync_copy(rows_v, out_hbm.at[idx_v])

    return dispatch(rows, dest)


def _moe_kernel(be_ref, nu_ref, x_ref, wg_ref, wu_ref, wd_ref, y_ref, wg_bf, wu_bf, wd_bf):
    i = pl.program_id(0)

    @pl.when((i == 0) | (be_ref[i] != be_ref[jnp.maximum(i - 1, 0)]))
    def _():
        _cast_rows(wg_ref, wg_bf)
        _cast_rows(wu_ref, wu_bf)
        _cast_rows(wd_ref, wd_bf)

    @pl.when(i < nu_ref[0])
    def _():
        xb = _unpack_halves(x_ref[...]).astype(BF16)
        a = jnp.dot(xb, wg_bf[...], preferred_element_type=F32)
        u = jnp.dot(xb, wu_bf[...], preferred_element_type=F32)
        y = jnp.dot((_silu(a) * u).astype(BF16), wd_bf[...], preferred_element_type=F32)
        y_ref[...] = _pack_halves(y)

    @pl.when(i >= nu_ref[0])
    def _():
        y_ref[...] = jnp.zeros_like(y_ref)


def _moe(blk_expert, n_used, x_sorted, w_gate, w_up, w_down):
    n_blocks = blk_expert.shape[0]
    wspec = lambda shape: pl.BlockSpec((None,) + shape, lambda i, be, nu: (be[i], 0, 0))
    rows = pl.BlockSpec((MOE_ROWS, HALF), lambda i, be, nu: (i, 0))
    grid_spec = pltpu.PrefetchScalarGridSpec(
        num_scalar_prefetch=2,
        grid=(n_blocks,),
        in_specs=[rows, wspec((D_MODEL, D_FF)), wspec((D_MODEL, D_FF)), wspec((D_FF, D_MODEL))],
        out_specs=rows,
        scratch_shapes=[pltpu.VMEM((D_MODEL, D_FF), BF16), pltpu.VMEM((D_MODEL, D_FF), BF16),
                        pltpu.VMEM((D_FF, D_MODEL), BF16)],
    )
    return pl.pallas_call(
        _moe_kernel,
        grid_spec=grid_spec,
        out_shape=jax.ShapeDtypeStruct(x_sorted.shape, U32),
        compiler_params=pltpu.CompilerParams(
            dimension_semantics=("arbitrary",), vmem_limit_bytes=VMEM_LIMIT),
        name="moe",
    )(blk_expert, n_used, x_sorted, w_gate, w_up, w_down)


def _combine_kernel(h2_ref, rw_ref, g_ref, y0_ref, y1_ref, o_ref):
    rw = rw_ref[...]
    h = h2_ref[...] + rw[:, 0:1] * _unpack_halves(y0_ref[...]) + rw[:, 1:2] * _unpack_halves(y1_ref[...])
    o_ref[...] = h * lax.rsqrt(jnp.mean(h * h, axis=-1, keepdims=True) + EPS) * g_ref[...]


def _combine(h2, rweights, final_g, y):
    t = h2.shape[0]
    tm = min(COMBINE_TM, t)
    row = lambda i: (i, 0)
    return pl.pallas_call(
        _combine_kernel,
        grid=(t // tm,),
        in_specs=[pl.BlockSpec((tm, D_MODEL), row),
                  pl.BlockSpec((tm, LANES), row),
                  pl.BlockSpec((1, D_MODEL), lambda i: (0, 0)),
                  pl.BlockSpec((tm, HALF), row),
                  pl.BlockSpec((tm, HALF), lambda i: (i + t // tm, 0))],
        out_specs=pl.BlockSpec((tm, D_MODEL), row),
        out_shape=jax.ShapeDtypeStruct((t, D_MODEL), F32),
        compiler_params=pltpu.CompilerParams(
            dimension_semantics=("arbitrary",), vmem_limit_bytes=VMEM_LIMIT),
        name="combine",
    )(h2, rweights, final_g, y, y)


def _block_layout(ids, counts_row, t):
    n_blocks = 2 * t // MOE_ROWS + N_EXPERTS
    counts = counts_row[0, ROUTE_LANE0:ROUTE_LANE0 + N_EXPERTS].astype(jnp.int32)
    padded = ((counts + MOE_ROWS - 1) // MOE_ROWS) * MOE_ROWS
    pend = jnp.cumsum(padded)
    pstart = pend - padded
    blk_first = jnp.arange(n_blocks, dtype=jnp.int32) * MOE_ROWS
    blk_expert = jnp.minimum(jnp.sum((pend[None, :] <= blk_first[:, None]).astype(jnp.int32), axis=1),
                             N_EXPERTS - 1).astype(jnp.int32)
    n_used = (pend[-1] // MOE_ROWS).astype(jnp.int32).reshape(1)
    expert, rank = ids[2:4], ids[4:6]
    experts = jnp.arange(N_EXPERTS, dtype=jnp.int32)
    first_row = jnp.sum(jnp.where(expert[:, :, None] == experts, pstart, 0), axis=-1)
    dest = (first_row + rank).reshape(2 * t)
    return dest, blk_expert, n_used, n_blocks * MOE_ROWS


def kernel(x, meta_tokens, hg_lb_logits, norm_mix_g, w_in, gd_conv_w, gd_A_log, gd_dt_bias, hg_norm_g, gd_norm_g, hg_up, gd_up, w_out, norm_ffn_g, router_group_w, router_group_b, router_expert_w, router_expert_b, w_gate, w_up, w_down, final_norm_g):
    bsz, seq, d = x.shape
    t = bsz * seq
    x2d = x.reshape(t, d)

    lb = jnp.cumsum(jax.nn.softmax(hg_lb_logits.astype(F32), axis=0), axis=0)[0].reshape(1, WIDTH)
    w_all = w_in.astype(F32)
    w_gates = w_all[0, :, W_SMALL + 2 * HEADS:]
    g_mix = norm_mix_g[0].reshape(1, d)
    meta_blk = jnp.concatenate([jnp.zeros((CHUNK - N_META, d), F32), meta_tokens.astype(F32)], axis=0)

    conv_w = gd_conv_w[0].astype(F32)
    no_history = jnp.zeros((SUBLANES, 3 * WIDTH), F32)
    proj, proj_meta = {}, {}
    proj["hgrn"] = _norm_proj_hgrn(x2d, g_mix, lb, w_all, PROJ_TM)
    proj_meta["hgrn"] = _norm_proj_hgrn(meta_blk, g_mix, lb, w_all, CHUNK)
    proj_meta["gdn"], meta_tail = _norm_proj_conv(meta_blk, g_mix, w_all, conv_w, no_history, CHUNK, CHUNK)
    proj["gdn"], _ = _norm_proj_conv(x2d, g_mix, w_all, conv_w, meta_tail, PROJ_TM, seq)

    def scalar_rows(p, nb, nc):
        s = p[:, 4 * WIDTH:4 * WIDTH + 2 * HEADS].reshape(nb, nc, CHUNK, 2, N_STACKS, HEADS_PER_STACK)
        return s.transpose(0, 1, 4, 3, 5, 2).reshape(nb, nc, N_STACKS, 2, STACK)

    a_log_row = jnp.repeat(gd_A_log[0].astype(F32), CHUNK).reshape(N_STACKS, 1, STACK)
    dt_row = jnp.repeat(gd_dt_bias[0].astype(F32), CHUNK).reshape(N_STACKS, 1, STACK)
    hg_g, gd_g = hg_norm_g[0].reshape(1, HEAD_DIM), gd_norm_g[0].reshape(1, HEAD_DIM)

    zero_state = jnp.zeros((HEADS, HEAD_DIM, HEAD_DIM), F32)
    per_seq = lambda st: jnp.tile(st, (bsz, 1, 1))
    _, hg_state = _hgrn(proj_meta["hgrn"][None], zero_state, hg_g, 1)
    o_a, _ = _hgrn(proj["hgrn"].reshape(bsz, seq, -1), per_seq(hg_state), hg_g, HGRN_CHUNKS)
    _, gd_state = _gdn(proj_meta["gdn"][None], zero_state, scalar_rows(proj_meta["gdn"], 1, 1),
                       a_log_row, dt_row, gd_g, 1)
    o_b, _ = _gdn(proj["gdn"].reshape(bsz, seq, -1), per_seq(gd_state),
                  scalar_rows(proj["gdn"], bsz, seq // CHUNK), a_log_row, dt_row, gd_g, GDN_CHUNKS)

    rw = jnp.zeros((d, LANES), F32)
    rw = rw.at[:, :N_GROUPS].set(router_group_w[0]).at[:, N_GROUPS:N_GROUPS + N_EXPERTS].set(router_expert_w[0])
    rb = jnp.zeros((1, LANES), F32)
    rb = rb.at[0, :N_GROUPS].set(router_group_b[0]).at[0, N_GROUPS:N_GROUPS + N_EXPERTS].set(router_expert_b[0])
    h2, xn, info, ids, counts = _merge(o_a.reshape(t, WIDTH), o_b.reshape(t, WIDTH), x2d, g_mix, w_gates,
                                  hg_up[0].astype(BF16), gd_up[0].astype(BF16), w_out[0].astype(BF16),
                                  norm_ffn_g[0].reshape(1, d), rw.astype(BF16), rb)

    dest, blk_expert, n_used, n_rows = _block_layout(ids, counts, t)
    x_sorted = _sc_dispatch(xn, dest, n_rows)
    y_sorted = _moe(blk_expert, n_used, x_sorted,
                    w_gate[0].astype(F32), w_up[0].astype(F32), w_down[0].astype(F32))
    y_tok = _sc_gather(y_sorted, dest)
    out = _combine(h2, info, final_norm_g.reshape(1, d), y_tok)
    return out.reshape(bsz, seq, d)
```

```python
import functools

import numpy as np
import jax
import jax.numpy as jnp
from jax import lax
from jax.experimental import pallas as pl
from jax.experimental.pallas import tpu as pltpu
from jax.experimental.pallas import tpu_sc as plsc

F32 = jnp.float32
BF16 = jnp.bfloat16
U32 = jnp.uint32

D_MODEL = 1024
N_META = 16
CHUNK = 64
EPS = 1e-6
HEADS = 4
HEAD_DIM = 128
WIDTH = HEADS * HEAD_DIM
HEADS_PER_STACK = 2
STACK = HEADS_PER_STACK * CHUNK
N_STACKS = HEADS // HEADS_PER_STACK
CONV_W = 4
N_GROUPS = 4
EXPERTS_PER_GROUP = 8
N_EXPERTS = N_GROUPS * EXPERTS_PER_GROUP
D_FF = 512
LANES = 128
SUBLANES = 8
HALF = D_MODEL // 2
MOE_ROWS = 512
CAST_SLAB = 128
SC_CORES, SC_SUBCORES = 2, 16
SC_WORKERS = SC_CORES * SC_SUBCORES
SC_CHUNK = 128
PROJ_TM = 1024
W_HGRN, W_GDN, W_SMALL = 0, 4 * WIDTH, 8 * WIDTH
HGRN_CHUNKS, GDN_CHUNKS = 4, 2
TOKEN_TM = 512
COMBINE_TM = 1024
VMEM_LIMIT = 56 * 1024 * 1024


def _bdot(a, b):
    return lax.dot_general(a.astype(BF16), b.astype(BF16), (((2,), (1,)), ((0,), (0,))),
                           preferred_element_type=F32)


def _bdot_nt(a, b):
    return lax.dot_general(a.astype(BF16), b.astype(BF16), (((2,), (2,)), ((0,), (0,))),
                           preferred_element_type=F32)


def _bdot_tn(a, b):
    return lax.dot_general(a.astype(BF16), b.astype(BF16), (((1,), (1,)), ((0,), (0,))),
                           preferred_element_type=F32)


def _split3(x):
    hi = x.astype(BF16)
    r1 = x - hi.astype(F32)
    mid = r1.astype(BF16)
    lo = (r1 - mid.astype(F32)).astype(BF16)
    return hi, mid, lo


def _sigmoid(x):
    return 1.0 / (1.0 + jnp.exp(-x))


def _silu(x):
    return x * _sigmoid(x)


def _softplus(x):
    return jnp.maximum(x, 0.0) + jnp.log(1.0 + jnp.exp(-jnp.abs(x)))


def _stack_heads(x, bsz):
    x3 = x.reshape(bsz, CHUNK, WIDTH)
    y = jnp.concatenate([x3[:, :, h * HEAD_DIM:(h + 1) * HEAD_DIM] for h in range(HEADS)], axis=1)
    return y.reshape(bsz * HEADS * CHUNK, HEAD_DIM)


def _unstack_heads(x, bsz):
    x3 = x.reshape(bsz, HEADS * CHUNK, HEAD_DIM)
    return jnp.concatenate([x3[:, h * CHUNK:(h + 1) * CHUNK, :] for h in range(HEADS)], axis=2)


def _to_stacks(a, bsz):
    return a.reshape(bsz * N_STACKS, STACK, a.shape[-1])


def _to_heads(a, bsz):
    return a.reshape(bsz * HEADS, CHUNK, a.shape[-1])


def _roll_rows(x, shift):
    n = x.shape[0]
    return pltpu.roll(x, shift % n, axis=0)


def _chunks_first(x, nc):
    return jnp.concatenate([x[:, j * CHUNK:(j + 1) * CHUNK] for j in range(nc)], axis=0)


def _chunks_first_scalars(s, nc):
    return jnp.concatenate([s[:, j] for j in range(nc)], axis=0)


def _load_chunks(ref, nc):
    x = _chunks_first(ref[...], nc)
    return x.reshape(x.shape[0] * CHUNK, WIDTH)


def _store_chunks(ref, x, nb, nc):
    ref[...] = jnp.concatenate([x[j * nb:(j + 1) * nb] for j in range(nc)], axis=1).astype(ref.dtype)


def _cast_rows(src_ref, dst_ref, col0=0):
    slab = CAST_SLAB
    rows, cols = src_ref.shape
    assert rows % slab == 0

    def body(k, carry):
        r = pl.multiple_of(k * slab, slab)
        dst_ref[pl.ds(r, slab), col0:col0 + cols] = src_ref[pl.ds(r, slab), :].astype(dst_ref.dtype)
        return carry

    lax.fori_loop(0, rows // slab, body, 0)


def _pack_halves(x):
    lo = lax.bitcast_convert_type(x[:, :HALF].astype(BF16).astype(F32), U32)
    hi = lax.bitcast_convert_type(x[:, HALF:].astype(BF16).astype(F32), U32)
    return (lo >> 16) | (hi & jnp.uint32(0xFFFF0000))


def _unpack_halves(p):
    lo = lax.bitcast_convert_type(p << 16, F32)
    hi = lax.bitcast_convert_type(p & jnp.uint32(0xFFFF0000), F32)
    return jnp.concatenate([lo, hi], axis=1)


def _normed_bf16(x_ref, g_ref):
    x = x_ref[...]
    return (x * lax.rsqrt(jnp.mean(x * x, axis=-1, keepdims=True) + EPS) * g_ref[...]).astype(BF16)


def _norm_proj_hgrn_kernel(x_ref, g_ref, lb_ref, w_ref, o_ref, wbf_ref):
    @pl.when(pl.program_id(0) == 0)
    def _():
        _cast_rows(w_ref, wbf_ref)

    acc = jnp.dot(_normed_bf16(x_ref, g_ref), wbf_ref[...], preferred_element_type=F32)
    w1, w2, w3 = WIDTH, 2 * WIDTH, 3 * WIDTH
    lb = lb_ref[...]
    o_ref[:, :w1] = _silu(acc[:, :w1])
    o_ref[:, w1:w2] = lb + (1.0 - lb) * _sigmoid(acc[:, w1:w2])
    o_ref[:, w2:w3] = acc[:, w2:w3]
    o_ref[:, w3:] = _silu(acc[:, w3:])


def _zero_rows(ref):
    slab = min(CAST_SLAB, ref.shape[0])
    assert ref.shape[0] % slab == 0

    def body(k, carry):
        ref[pl.ds(pl.multiple_of(k * slab, slab), slab), :] = jnp.zeros((slab, ref.shape[1]), ref.dtype)
        return carry

    lax.fori_loop(0, ref.shape[0] // slab, body, 0)


def _norm_proj_conv_kernel(x_ref, g_ref, w_ref, ws_ref, cw_ref, hist_ref, o_ref, tail_ref,
                           wbf_ref, acc_ref, carry_ref, *, steps_per_seq):
    i = pl.program_id(0)

    @pl.when(i == 0)
    def _():
        _cast_rows(w_ref, wbf_ref)
        _cast_rows(ws_ref, wbf_ref, 4 * WIDTH)
        _zero_rows(acc_ref)
        carry_ref[...] = jnp.zeros_like(carry_ref)

    xn = _normed_bf16(x_ref, g_ref)
    w3, w4 = 3 * WIDTH, 4 * WIDTH
    rows = x_ref.shape[0]
    first_of_seq = (i - 1) % steps_per_seq == 0
    cw = cw_ref[...]
    r8 = lax.broadcasted_iota(jnp.int32, (SUBLANES, WIDTH), 0)
    for c0 in range(0, w3, WIDTH):
        cols = slice(c0, c0 + WIDTH)
        raw = acc_ref[:, cols]
        hist = jnp.where(first_of_seq, hist_ref[:, cols], carry_ref[:, cols])
        tap = lambda d: cw[CONV_W - 1 - d:CONV_W - d, cols]
        body = raw * tap(0)
        head = raw[:SUBLANES]
        first = head * tap(0)
        for d in range(1, CONV_W):
            body = body + _roll_rows(raw, d) * tap(d)
            first = first + jnp.where(r8 >= d, _roll_rows(head, d), _roll_rows(hist, d)) * tap(d)
        o_ref[:SUBLANES, cols] = _silu(first)
        o_ref[SUBLANES:, cols] = _silu(body[SUBLANES:])
        carry_ref[:, cols] = raw[rows - SUBLANES:]
        acc_ref[:, cols] = jnp.dot(xn, wbf_ref[:, cols], preferred_element_type=F32)
    rest = acc_ref[:, w3:]
    o_ref[:, w3:w4] = _silu(rest[:, :WIDTH])
    o_ref[:, w4:] = rest[:, WIDTH:]
    acc_ref[:, w3:] = jnp.dot(xn, wbf_ref[:, w3:], preferred_element_type=F32)
    tail_ref[...] = carry_ref[...]


def _proj_call(kernel_fn, name, x2d, tm, n, operands, specs, extra_out=(), extra_scratch=(), delayed=False):
    m, k = x2d.shape
    tm = min(tm, m)
    tiles = m // tm
    x_map = (lambda i: (jnp.minimum(i, tiles - 1), 0)) if delayed else (lambda i: (i, 0))
    o_map = (lambda i: (jnp.maximum(i - 1, 0), 0)) if delayed else (lambda i: (i, 0))
    out_specs = [pl.BlockSpec((tm, n), o_map)] + [s for s, _ in extra_out]
    out_shape = [jax.ShapeDtypeStruct((m, n), F32)] + [s for _, s in extra_out]
    return pl.pallas_call(
        kernel_fn,
        grid=(tiles + int(delayed),),
        in_specs=[pl.BlockSpec((tm, k), x_map)] + specs,
        out_specs=out_specs,
        out_shape=out_shape,
        scratch_shapes=[pltpu.VMEM((k, n), BF16)] + list(extra_scratch),
        compiler_params=pltpu.CompilerParams(
            dimension_semantics=("arbitrary",), vmem_limit_bytes=VMEM_LIMIT),
        name=name,
    )(x2d, *operands)


def _weight_spec(k, n, col0):
    assert col0 % n == 0
    return pl.BlockSpec((None, k, n), lambda i: (0, 0, col0 // n), pipeline_mode=pl.Buffered(1))


def _norm_proj_hgrn(x2d, g, lb, w, tm):
    k, n = w.shape[1], 4 * WIDTH
    const = lambda i: (0, 0)
    specs = [pl.BlockSpec((1, k), const), pl.BlockSpec((1, WIDTH), const), _weight_spec(k, n, W_HGRN)]
    return _proj_call(_norm_proj_hgrn_kernel, "norm_proj_hgrn", x2d, tm, n, (g, lb, w), specs)[0]


def _norm_proj_conv(x2d, g, w, conv_w, hist, tm, rows_per_seq):
    k, n = w.shape[1], 4 * WIDTH + LANES
    const = lambda i: (0, 0)
    tm = min(tm, rows_per_seq)
    assert rows_per_seq % tm == 0 and x2d.shape[0] % rows_per_seq == 0
    tail_shape = (SUBLANES, 3 * WIDTH)
    specs = [pl.BlockSpec((1, k), const), _weight_spec(k, 4 * WIDTH, W_GDN), _weight_spec(k, LANES, W_SMALL),
             pl.BlockSpec((CONV_W, 3 * WIDTH), const), pl.BlockSpec(tail_shape, const)]
    return _proj_call(functools.partial(_norm_proj_conv_kernel, steps_per_seq=rows_per_seq // tm),
                      "norm_proj_gdn", x2d, tm, n, (g, w, w, conv_w, hist), specs,
                      extra_out=[(pl.BlockSpec(tail_shape, const), jax.ShapeDtypeStruct(tail_shape, F32))],
                      extra_scratch=[pltpu.VMEM((tm, n), F32), pltpu.VMEM(tail_shape, F32)], delayed=True)


HGRN_LEVELS = 6
HGRN_DIAG = HGRN_LEVELS


def _hgrn_level_table():
    r = np.arange(STACK)
    t, hd = r % CHUNK, r // CHUNK
    x = t[:, None] ^ t[None, :]
    lv = np.floor(np.log2(np.maximum(x, 1))).astype(np.int32)
    valid = (hd[:, None] == hd[None, :]) & (t[:, None] > t[None, :])
    lv = np.where(valid, lv, -1)
    lv = np.where(r[:, None] == r[None, :], HGRN_DIAG, lv)
    return jnp.asarray(lv, dtype=jnp.int32)


def _hgrn_kernel(lv_ref, g_ref, st0_ref, q_ref, f_ref, v_ref, gate_ref, o_ref, stout_ref, st_ref, *, nb, nc):
    @pl.when(pl.program_id(0) == 0)
    def _():
        st_ref[...] = st0_ref[...]

    bsz = nc * nb
    q = _load_chunks(q_ref, nc)
    f = _load_chunks(f_ref, nc)
    v = _load_chunks(v_ref, nc)
    gate = _load_chunks(gate_ref, nc)

    per_seq = lambda a: a.reshape(bsz, CHUNK, WIDTH)
    trow = lax.broadcasted_iota(jnp.int32, (1, CHUNK, WIDTH), 1)
    b = jnp.log(f)
    s = 1
    while s < CHUNK:
        b = (per_seq(b) + jnp.where(trow >= s, per_seq(_roll_rows(b, s)), 0.0)).reshape(b.shape)
        s *= 2

    qs, ks, vs, bs = (_stack_heads(a, bsz) for a in (q, 1.0 - f, v, b))
    lv = lv_ref[...][None]
    to3 = functools.partial(_to_stacks, bsz=bsz)
    to_g = functools.partial(_to_heads, bsz=bsz)
    roll_g = lambda a, shift: to_g(_roll_rows(a.reshape(qs.shape), shift))
    t = lax.broadcasted_iota(jnp.int32, (1, CHUNK, HEAD_DIM), 1)
    qs_g, ks_g, bs_g = to_g(qs), to_g(ks), to_g(bs)

    attn = jnp.where(lv == HGRN_DIAG, _bdot_nt(to3(qs), to3(ks)), 0.0)
    bref = bs_g
    bnext = roll_g(bs_g, -1)
    for li in range(HGRN_LEVELS):
        m = 1 << li
        if li > 0:
            half = m // 2
            upper = (t & (m - 1)) >= half
            bref = jnp.where(upper, roll_g(bref, half), bref)
            bnext = jnp.where(upper, bnext, roll_g(bnext, -half))
        odd = ((t >> li) & 1) == 1
        both = jnp.where(odd, qs_g, ks_g) * jnp.exp(jnp.where(odd, bs_g - bref, bnext - bs_g))
        both = to3(both.reshape(qs.shape)).astype(BF16)
        attn = jnp.where(lv == li, _bdot_nt(both, both), attn)

    o = _bdot(attn, to3(vs))

    n_g = nb * HEADS
    st = st_ref[...]
    qe = qs_g * jnp.exp(bs_g)
    b_end = bs_g[:, CHUNK - 1:CHUNK, :]
    ke = ks_g * jnp.exp(b_end - bs_g)
    vs_g = to_g(vs)
    o_inter = []
    for j in range(nc):
        sl = slice(j * n_g, (j + 1) * n_g)
        o_inter.append(_bdot_nt(qe[sl], st))
        st = st * jnp.exp(b_end[sl]) + _bdot_tn(vs_g[sl], ke[sl])
    st_ref[...] = st
    stout_ref[...] = st
    o = (to_g(o) + jnp.concatenate(o_inter, axis=0)).reshape(qs.shape)

    o = o * lax.rsqrt(jnp.mean(o * o, axis=-1, keepdims=True) + EPS) * g_ref[...]
    o = o * _stack_heads(gate, bsz)
    _store_chunks(o_ref, _unstack_heads(o, bsz), nb, nc)


def _gdn_cumsum_matrix():
    r = np.arange(STACK)
    u = (r[:, None] // CHUNK == r[None, :] // CHUNK) & (r[:, None] <= r[None, :])
    return jnp.asarray(u, dtype=BF16)


def _gdn_kernel(u_ref, alog_ref, dt_ref, g_ref, s_ref, st0_ref, q_ref, k_ref, v_ref, z_ref,
                o_ref, stout_ref, st_ref, *, nb, nc):
    @pl.when(pl.program_id(0) == 0)
    def _():
        st_ref[...] = st0_ref[...]

    bsz = nc * nb
    qs = _stack_heads(_load_chunks(q_ref, nc), bsz)
    ks = _stack_heads(_load_chunks(k_ref, nc), bsz)
    vs = _stack_heads(_load_chunks(v_ref, nc), bsz)
    gate = _load_chunks(z_ref, nc)
    qs = qs * lax.rsqrt(jnp.sum(qs * qs, axis=-1, keepdims=True) + EPS) * (HEAD_DIM ** -0.5)
    ks = ks * lax.rsqrt(jnp.sum(ks * ks, axis=-1, keepdims=True) + EPS)
    to3 = functools.partial(_to_stacks, bsz=bsz)
    qs, ks, vs = to3(qs), to3(ks), to3(vs)
    n_st = bsz * N_STACKS

    srow = _chunks_first_scalars(s_ref[...], nc).reshape(n_st, 2, STACK)
    per_stack = lambda ref: jnp.broadcast_to(ref[...][None], (bsz, N_STACKS, 1, STACK)).reshape(n_st, 1, STACK)
    beta_row = _sigmoid(srow[:, 0:1, :])
    g_row = -jnp.exp(per_stack(alog_ref)) * _softplus(srow[:, 1:2, :] + per_stack(dt_ref))
    n8 = n_st * SUBLANES
    g8 = jnp.broadcast_to(g_row, (n_st, SUBLANES, STACK)).reshape(n8, STACK)
    pieces = jnp.concatenate(_split3(g8), axis=0)
    cum = jnp.dot(pieces, u_ref[...], preferred_element_type=F32)
    cum = cum[0:n8] + cum[n8:2 * n8] + cum[2 * n8:]
    cum_row = cum.reshape(n_st, SUBLANES, STACK)[:, 0:1, :]

    r = lax.broadcasted_iota(jnp.int32, (STACK, STACK), 0)
    cc = lax.broadcasted_iota(jnp.int32, (STACK, STACK), 1)
    eye = (r == cc)[None]
    cum_col = jnp.sum(jnp.where(eye, cum_row, 0.0), axis=2, keepdims=True)
    beta_col = jnp.sum(jnp.where(eye, beta_row, 0.0), axis=2, keepdims=True)
    same = (r >> 6) == (cc >> 6)
    lower = (same & (r >= cc))[None]
    strict = (same & (r > cc))[None]
    ratio = jnp.exp(jnp.minimum(cum_col - cum_row, 0.0))

    kkt = _bdot_nt(ks, ks)
    qkt = _bdot_nt(qs, ks)
    lm = jnp.where(strict, beta_col * ratio * kkt, 0.0)

    ident = eye.astype(F32)
    l0 = jnp.where(((r >> 3) == (cc >> 3))[None], lm, 0.0)
    l2 = _bdot(l0, l0)
    l4 = _bdot(l2, l2)
    xinv = _bdot(_bdot(ident - l0, ident + l2), ident + l4)
    for lvl in (3, 4, 5):
        off = (((r >> (lvl + 1)) == (cc >> (lvl + 1))) & ((r >> lvl) != (cc >> lvl)))[None]
        xinv = xinv - _bdot(_bdot(xinv, jnp.where(off, lm, 0.0)), xinv)

    decay_col = jnp.exp(cum_col)
    rhs = jnp.concatenate([beta_col * vs, (beta_col * decay_col) * ks], axis=2)
    sol = _bdot(xinv, rhs)

    to_g = functools.partial(_to_heads, bsz=bsz)
    n_g = nb * HEADS
    st = st_ref[...]
    u0_g, w_g, q_g = to_g(sol[:, :, :HEAD_DIM]), to_g(sol[:, :, HEAD_DIM:]), to_g(qs)
    cum_g = to_g(cum_col)
    cum_end = cum_g[:, CHUNK - 1:CHUNK, :]
    kdec = to_g(ks) * jnp.exp(cum_end - cum_g)
    decay_g = to_g(decay_col)
    us, o_inter = [], []
    for j in range(nc):
        sl = slice(j * n_g, (j + 1) * n_g)
        u_j = u0_g[sl] - _bdot(w_g[sl], st)
        us.append(u_j)
        o_inter.append(decay_g[sl] * _bdot(q_g[sl], st))
        st = jnp.exp(cum_end[sl]) * st + _bdot_tn(kdec[sl], u_j)
    st_ref[...] = st
    stout_ref[...] = st
    attn = jnp.where(lower, qkt * ratio, 0.0)
    u = jnp.concatenate(us, axis=0).reshape(bsz * HEADS * CHUNK, HEAD_DIM)
    o = jnp.concatenate(o_inter, axis=0) + to_g(_bdot(attn, to3(u)))
    o = o.reshape(bsz * HEADS * CHUNK, HEAD_DIM)

    o = o * lax.rsqrt(jnp.mean(o * o, axis=-1, keepdims=True) + EPS) * g_ref[...]
    o = o * _stack_heads(gate, bsz)
    _store_chunks(o_ref, _unstack_heads(o, bsz), nb, nc)


def _recurrence_call(kernel_fn, name, proj, state0, operands, specs, nc):
    nb, length, _ = proj.shape
    rows = nc * CHUNK
    state_shape = (nb * HEADS, HEAD_DIM, HEAD_DIM)
    state_spec = pl.BlockSpec(state_shape, lambda c: (0, 0, 0))
    chunk_specs = [pl.BlockSpec((nb, rows, WIDTH), functools.partial(lambda c, j: (0, c, j), j=j))
                   for j in range(4)]
    return pl.pallas_call(
        functools.partial(kernel_fn, nb=nb, nc=nc),
        grid=(length // rows,),
        in_specs=specs + [state_spec] + chunk_specs,
        out_specs=[pl.BlockSpec((nb, rows, WIDTH), lambda c: (0, c, 0)), state_spec],
        out_shape=[jax.ShapeDtypeStruct((nb, length, WIDTH), BF16),
                   jax.ShapeDtypeStruct(state_shape, F32)],
        scratch_shapes=[pltpu.VMEM(state_shape, F32)],
        compiler_params=pltpu.CompilerParams(
            dimension_semantics=("arbitrary",), vmem_limit_bytes=VMEM_LIMIT),
        name=name,
    )(*operands, state0, *([proj] * 4))


def _hgrn(proj, state0, norm_g, nc):
    const = lambda c: (0, 0)
    specs = [pl.BlockSpec((STACK, STACK), const), pl.BlockSpec((1, HEAD_DIM), const)]
    return _recurrence_call(_hgrn_kernel, "hgrn", proj, state0, (_hgrn_level_table(), norm_g), specs, nc)


def _gdn(proj, state0, scal, a_log_row, dt_row, norm_g, nc):
    nb = proj.shape[0]
    const = lambda c: (0, 0)
    const3 = lambda c: (0, 0, 0)
    specs = [pl.BlockSpec((STACK, STACK), const),
             pl.BlockSpec((N_STACKS, 1, STACK), const3),
             pl.BlockSpec((N_STACKS, 1, STACK), const3),
             pl.BlockSpec((1, HEAD_DIM), const),
             pl.BlockSpec((nb, nc, N_STACKS, 2, STACK), lambda c: (0, c, 0, 0, 0))]
    return _recurrence_call(_gdn_kernel, "gdn", proj, state0,
                            (_gdn_cumsum_matrix(), a_log_row, dt_row, norm_g, scal), specs, nc)


ROUTE_LANE0 = N_GROUPS
ROUTE_NEG = -1e30


def _route_tile(lg, ltri, carry):
    lane = lax.broadcasted_iota(jnp.int32, lg.shape, 1).astype(F32)
    first = lambda mask: jnp.min(jnp.where(mask, lane, float(LANES)), axis=1, keepdims=True)
    top = lambda mask: jnp.max(jnp.where(mask, lg, ROUTE_NEG), axis=1, keepdims=True)

    is_g = lane < N_GROUPS
    gmax = top(is_g)
    grp = first(is_g & (lg == gmax))
    p_grp = 1.0 / jnp.sum(jnp.where(is_g, jnp.exp(lg - gmax), 0.0), axis=1, keepdims=True)

    lo = ROUTE_LANE0 + grp * EXPERTS_PER_GROUP
    in_grp = (lane >= lo) & (lane < lo + EXPERTS_PER_GROUP)
    m1 = top(in_grp)
    i1 = first(in_grp & (lg == m1))
    rest = in_grp & (lane != i1)
    m2 = top(rest)
    i2 = first(rest & (lg == m2))
    r = jnp.exp(m2 - m1)
    w0 = p_grp / (1.0 + r)
    w1 = p_grp * r / (1.0 + r)

    hot0 = (lane == i1).astype(F32)
    hot1 = (lane == i2).astype(F32)
    both = hot0 + hot1
    before = jnp.dot(ltri, both.astype(BF16), preferred_element_type=F32) + carry
    rank0 = jnp.sum(hot0 * before, axis=1, keepdims=True)
    rank1 = jnp.sum(hot1 * before, axis=1, keepdims=True)
    cols = (w0, w1, i1 - ROUTE_LANE0, i2 - ROUTE_LANE0, rank0, rank1)
    info = jnp.zeros_like(lg)
    for k, col in enumerate(cols):
        info = jnp.where(lane == k, col, info)
    return info, carry + jnp.sum(both, axis=0, keepdims=True)


def _merge_kernel(oa_ref, ob_ref, x_ref, xprev_ref, mg_ref, wgate_ref, hgup_ref, gdup_ref, wout_ref,
                  ng_ref, rw_ref, rb_ref, ltri_ref, h2_ref, xn_ref, info_ref, ids_ref, cnt_ref,
                  wgate_bf, gates_ref, ua_ref, ub_ref, carry_ref):
    i = pl.program_id(0)

    @pl.when(i == 0)
    def _():
        carry_ref[...] = jnp.zeros_like(carry_ref)
        _cast_rows(wgate_ref, wgate_bf)
        _zero_rows(gates_ref)
        _zero_rows(ua_ref)
        _zero_rows(ub_ref)

    gates = _sigmoid(gates_ref[...])
    merged = (gates[:, :D_MODEL] * ua_ref[...] + gates[:, D_MODEL:] * ub_ref[...]).astype(BF16)
    gates_ref[...] = jnp.dot(_normed_bf16(x_ref, mg_ref), wgate_bf[...], preferred_element_type=F32)
    h2 = xprev_ref[...] + jnp.dot(merged, wout_ref[...], preferred_element_type=F32)
    h2_ref[...] = h2
    xn = h2 * lax.rsqrt(jnp.mean(h2 * h2, axis=-1, keepdims=True) + EPS) * ng_ref[...]
    xn_ref[...] = _pack_halves(xn)
    lg = jnp.dot(xn.astype(BF16), rw_ref[...], preferred_element_type=F32) + rb_ref[...]
    ua_ref[...] = jnp.dot(oa_ref[...], hgup_ref[...], preferred_element_type=F32)
    ub_ref[...] = jnp.dot(ob_ref[...], gdup_ref[...], preferred_element_type=F32)
    old = carry_ref[...]
    info, carry = _route_tile(lg, ltri_ref[...], old)
    info_ref[...] = info
    ids_ref[...] = info.T[:SUBLANES].astype(jnp.int32)
    carry = jnp.where(i > 0, carry, old)
    carry_ref[...] = carry
    cnt_ref[...] = carry


def _merge(o_a, o_b, x2d, mix_g, w_gates, hg_up, gd_up, w_out, norm_g, rw, rb):
    t = x2d.shape[0]
    tm = TOKEN_TM
    tiles = t // tm
    new = lambda i: (jnp.minimum(i, tiles - 1), 0)
    row = lambda i: (jnp.maximum(i - 1, 0), 0)
    const = lambda i: (0, 0)
    ltri = jnp.asarray(np.tril(np.ones((tm, tm), np.float32), -1), dtype=BF16)
    return pl.pallas_call(
        _merge_kernel,
        grid=(tiles + 1,),
        in_specs=[pl.BlockSpec((tm, WIDTH), new),
                  pl.BlockSpec((tm, WIDTH), new),
                  pl.BlockSpec((tm, D_MODEL), new),
                  pl.BlockSpec((tm, D_MODEL), row),
                  pl.BlockSpec((1, D_MODEL), const),
                  pl.BlockSpec((D_MODEL, 2 * D_MODEL), const, pipeline_mode=pl.Buffered(1)),
                  pl.BlockSpec((WIDTH, D_MODEL), const),
                  pl.BlockSpec((WIDTH, D_MODEL), const),
                  pl.BlockSpec((D_MODEL, D_MODEL), const),
                  pl.BlockSpec((1, D_MODEL), const),
                  pl.BlockSpec((D_MODEL, LANES), const),
                  pl.BlockSpec((1, LANES), const),
                  pl.BlockSpec((tm, tm), const)],
        out_specs=[pl.BlockSpec((tm, D_MODEL), row),
                   pl.BlockSpec((tm, HALF), row),
                   pl.BlockSpec((tm, LANES), row),
                   pl.BlockSpec((SUBLANES, tm), lambda i: (0, jnp.maximum(i - 1, 0))),
                   pl.BlockSpec((1, LANES), const)],
        out_shape=[jax.ShapeDtypeStruct((t, D_MODEL), F32),
                   jax.ShapeDtypeStruct((t, HALF), U32),
                   jax.ShapeDtypeStruct((t, LANES), F32),
                   jax.ShapeDtypeStruct((SUBLANES, t), jnp.int32),
                   jax.ShapeDtypeStruct((1, LANES), F32)],
        scratch_shapes=[pltpu.VMEM((D_MODEL, 2 * D_MODEL), BF16), pltpu.VMEM((tm, 2 * D_MODEL), F32),
                        pltpu.VMEM((tm, D_MODEL), F32), pltpu.VMEM((tm, D_MODEL), F32),
                        pltpu.VMEM((1, LANES), F32)],
        compiler_params=pltpu.CompilerParams(
            dimension_semantics=("arbitrary",), vmem_limit_bytes=VMEM_LIMIT),
        name="merge",
    )(o_a, o_b, x2d, x2d, mix_g, w_gates, hg_up, gd_up, w_out, norm_g, rw, rb, ltri)


def _sc_gather(table, idx):
    n_idx = idx.shape[0]
    cols = table.shape[1]
    per_worker = n_idx // SC_WORKERS
    assert n_idx % (SC_WORKERS * SC_CHUNK) == 0
    mesh = plsc.VectorSubcoreMesh(core_axis_name="c", subcore_axis_name="s")

    @functools.partial(
        pl.kernel, mesh=mesh,
        out_type=jax.ShapeDtypeStruct((n_idx, cols), table.dtype),
        scratch_types=[pltpu.VMEM((SC_CHUNK,), jnp.int32),
                       pltpu.VMEM((SC_CHUNK, cols), table.dtype),
                       pltpu.SemaphoreType.DMA],
    )
    def gather(table_hbm, idx_hbm, out_hbm, idx_v, rows_v, sem):
        worker = lax.axis_index("s") * SC_CORES + lax.axis_index("c")
        base = worker * per_worker

        @pl.loop(0, per_worker // SC_CHUNK)
        def _(c):
            off = pl.multiple_of(base + c * SC_CHUNK, SC_CHUNK)
            pltpu.sync_copy(idx_hbm.at[pl.ds(off, SC_CHUNK)], idx_v)
            pltpu.async_copy(table_hbm.at[idx_v], rows_v, sem).wait()
            pltpu.sync_copy(rows_v, out_hbm.at[pl.ds(off, SC_CHUNK)])

    return gather(table, idx)


def _sc_dispatch(rows, dest, n_out):
    t, cols = rows.shape
    per_worker = t // SC_WORKERS
    assert t % (SC_WORKERS * SC_CHUNK) == 0 and dest.shape[0] == 2 * t
    mesh = plsc.VectorSubcoreMesh(core_axis_name="c", subcore_axis_name="s")

    @functools.partial(
        pl.kernel, mesh=mesh,
        out_type=jax.ShapeDtypeStruct((n_out, cols), rows.dtype),
        scratch_types=[pltpu.VMEM((SC_CHUNK,), jnp.int32),
                       pltpu.VMEM((SC_CHUNK, cols), rows.dtype)],
    )
    def dispatch(rows_hbm, dest_hbm, out_hbm, idx_v, rows_v):
        worker = lax.axis_index("s") * SC_CORES + lax.axis_index("c")
        base = worker * per_worker

        @pl.loop(0, per_worker // SC_CHUNK)
        def _(c):
            off = pl.multiple_of(base + c * SC_CHUNK, SC_CHUNK)
            pltpu.sync_copy(rows_hbm.at[pl.ds(off, SC_CHUNK)], rows_v)
            for slot in range(2):
                pltpu.sync_copy(dest_hbm.at[pl.ds(slot * t + off, SC_CHUNK)], idx_v)
                pltpu.sync_copy(rows_v, out_hbm.at[idx_v])

    return dispatch(rows, dest)


def _moe_kernel(be_ref, nu_ref, x_ref, wg_ref, wu_ref, wd_ref, y_ref, wg_bf, wu_bf, wd_bf):
    i = pl.program_id(0)

    @pl.when((i == 0) | (be_ref[i] != be_ref[jnp.maximum(i - 1, 0)]))
    def _():
        _cast_rows(wg_ref, wg_bf)
        _cast_rows(wu_ref, wu_bf)
        _cast_rows(wd_ref, wd_bf)

    @pl.when(i < nu_ref[0])
    def _():
        xb = _unpack_halves(x_ref[...]).astype(BF16)
        a = jnp.dot(xb, wg_bf[...], preferred_element_type=F32)
        u = jnp.dot(xb, wu_bf[...], preferred_element_type=F32)
        y = jnp.dot((_silu(a) * u).astype(BF16), wd_bf[...], preferred_element_type=F32)
        y_ref[...] = _pack_halves(y)

    @pl.when(i >= nu_ref[0])
    def _():
        y_ref[...] = jnp.zeros_like(y_ref)


def _moe(blk_expert, n_used, x_sorted, w_gate, w_up, w_down):
    n_blocks = blk_expert.shape[0]
    wspec = lambda shape: pl.BlockSpec((None,) + shape, lambda i, be, nu: (be[i], 0, 0))
    rows = pl.BlockSpec((MOE_ROWS, HALF), lambda i, be, nu: (i, 0))
    grid_spec = pltpu.PrefetchScalarGridSpec(
        num_scalar_prefetch=2,
        grid=(n_blocks,),
        in_specs=[rows, wspec((D_MODEL, D_FF)), wspec((D_MODEL, D_FF)), wspec((D_FF, D_MODEL))],
        out_specs=rows,
        scratch_shapes=[pltpu.VMEM((D_MODEL, D_FF), BF16), pltpu.VMEM((D_MODEL, D_FF), BF16),
                        pltpu.VMEM((D_FF, D_MODEL), BF16)],
    )
    return pl.pallas_call(
        _moe_kernel,
        grid_spec=grid_spec,
        out_shape=jax.ShapeDtypeStruct(x_sorted.shape, U32),
        compiler_params=pltpu.CompilerParams(
            dimension_semantics=("arbitrary",), vmem_limit_bytes=VMEM_LIMIT),
        name="moe",
    )(blk_expert, n_used, x_sorted, w_gate, w_up, w_down)


def _combine_kernel(h2_ref, rw_ref, g_ref, y0_ref, y1_ref, o_ref):
    rw = rw_ref[...]
    h = h2_ref[...] + rw[:, 0:1] * _unpack_halves(y0_ref[...]) + rw[:, 1:2] * _unpack_halves(y1_ref[...])
    o_ref[...] = h * lax.rsqrt(jnp.mean(h * h, axis=-1, keepdims=True) + EPS) * g_ref[...]


def _combine(h2, rweights, final_g, y):
    t = h2.shape[0]
    tm = min(COMBINE_TM, t)
    row = lambda i: (i, 0)
    return pl.pallas_call(
        _combine_kernel,
        grid=(t // tm,),
        in_specs=[pl.BlockSpec((tm, D_MODEL), row),
                  pl.BlockSpec((tm, LANES), row),
                  pl.BlockSpec((1, D_MODEL), lambda i: (0, 0)),
                  pl.BlockSpec((tm, HALF), row),
                  pl.BlockSpec((tm, HALF), lambda i: (i + t // tm, 0))],
        out_specs=pl.BlockSpec((tm, D_MODEL), row),
        out_shape=jax.ShapeDtypeStruct((t, D_MODEL), F32),
        compiler_params=pltpu.CompilerParams(
            dimension_semantics=("arbitrary",), vmem_limit_bytes=VMEM_LIMIT),
        name="combine",
    )(h2, rweights, final_g, y, y)


def _block_layout(ids, counts_row, t):
    n_blocks = 2 * t // MOE_ROWS + N_EXPERTS
    counts = counts_row[0, ROUTE_LANE0:ROUTE_LANE0 + N_EXPERTS].astype(jnp.int32)
    padded = ((counts + MOE_ROWS - 1) // MOE_ROWS) * MOE_ROWS
    pend = jnp.cumsum(padded)
    pstart = pend - padded
    blk_first = jnp.arange(n_blocks, dtype=jnp.int32) * MOE_ROWS
    blk_expert = jnp.minimum(jnp.sum((pend[None, :] <= blk_first[:, None]).astype(jnp.int32), axis=1),
                             N_EXPERTS - 1).astype(jnp.int32)
    n_used = (pend[-1] // MOE_ROWS).astype(jnp.int32).reshape(1)
    expert, rank = ids[2:4], ids[4:6]
    experts = jnp.arange(N_EXPERTS, dtype=jnp.int32)
    first_row = jnp.sum(jnp.where(expert[:, :, None] == experts, pstart, 0), axis=-1)
    dest = (first_row + rank).reshape(2 * t)
    return dest, blk_expert, n_used, n_blocks * MOE_ROWS


def kernel(x, meta_tokens, hg_lb_logits, norm_mix_g, w_in, gd_conv_w, gd_A_log, gd_dt_bias, hg_norm_g, gd_norm_g, hg_up, gd_up, w_out, norm_ffn_g, router_group_w, router_group_b, router_expert_w, router_expert_b, w_gate, w_up, w_down, final_norm_g):
    bsz, seq, d = x.shape
    t = bsz * seq
    x2d = x.reshape(t, d)

    lb = jnp.cumsum(jax.nn.softmax(hg_lb_logits.astype(F32), axis=0), axis=0)[0].reshape(1, WIDTH)
    w_all = w_in.astype(F32)
    w_gates = w_all[0, :, W_SMALL + 2 * HEADS:]
    g_mix = norm_mix_g[0].reshape(1, d)
    meta_blk = jnp.concatenate([jnp.zeros((CHUNK - N_META, d), F32), meta_tokens.astype(F32)], axis=0)

    conv_w = gd_conv_w[0].astype(F32)
    no_history = jnp.zeros((SUBLANES, 3 * WIDTH), F32)
    proj, proj_meta = {}, {}
    proj["hgrn"] = _norm_proj_hgrn(x2d, g_mix, lb, w_all, PROJ_TM)
    proj_meta["hgrn"] = _norm_proj_hgrn(meta_blk, g_mix, lb, w_all, CHUNK)
    proj_meta["gdn"], meta_tail = _norm_proj_conv(meta_blk, g_mix, w_all, conv_w, no_history, CHUNK, CHUNK)
    proj["gdn"], _ = _norm_proj_conv(x2d, g_mix, w_all, conv_w, meta_tail, PROJ_TM, seq)

    def scalar_rows(p, nb, nc):
        s = p[:, 4 * WIDTH:4 * WIDTH + 2 * HEADS].reshape(nb, nc, CHUNK, 2, N_STACKS, HEADS_PER_STACK)
        return s.transpose(0, 1, 4, 3, 5, 2).reshape(nb, nc, N_STACKS, 2, STACK)

    a_log_row = jnp.repeat(gd_A_log[0].astype(F32), CHUNK).reshape(N_STACKS, 1, STACK)
    dt_row = jnp.repeat(gd_dt_bias[0].astype(F32), CHUNK).reshape(N_STACKS, 1, STACK)
    hg_g, gd_g = hg_norm_g[0].reshape(1, HEAD_DIM), gd_norm_g[0].reshape(1, HEAD_DIM)

    zero_state = jnp.zeros((HEADS, HEAD_DIM, HEAD_DIM), F32)
    per_seq = lambda st: jnp.tile(st, (bsz, 1, 1))
    _, hg_state = _hgrn(proj_meta["hgrn"][None], zero_state, hg_g, 1)
    o_a, _ = _hgrn(proj["hgrn"].reshape(bsz, seq, -1), per_seq(hg_state), hg_g, HGRN_CHUNKS)
    _, gd_state = _gdn(proj_meta["gdn"][None], zero_state, scalar_rows(proj_meta["gdn"], 1, 1),
                       a_log_row, dt_row, gd_g, 1)
    o_b, _ = _gdn(proj["gdn"].reshape(bsz, seq, -1), per_seq(gd_state),
                  scalar_rows(proj["gdn"], bsz, seq // CHUNK), a_log_row, dt_row, gd_g, GDN_CHUNKS)

    rw = jnp.zeros((d, LANES), F32)
    rw = rw.at[:, :N_GROUPS].set(router_group_w[0]).at[:, N_GROUPS:N_GROUPS + N_EXPERTS].set(router_expert_w[0])
    rb = jnp.zeros((1, LANES), F32)
    rb = rb.at[0, :N_GROUPS].set(router_group_b[0]).at[0, N_GROUPS:N_GROUPS + N_EXPERTS].set(router_expert_b[0])
    h2, xn, info, ids, counts = _merge(o_a.reshape(t, WIDTH), o_b.reshape(t, WIDTH), x2d, g_mix, w_gates,
                                  hg_up[0].astype(BF16), gd_up[0].astype(BF16), w_out[0].astype(BF16),
                                  norm_ffn_g[0].reshape(1, d), rw.astype(BF16), rb)

    dest, blk_expert, n_used, n_rows = _block_layout(ids, counts, t)
    x_sorted = _sc_dispatch(xn, dest, n_rows)
    y_sorted = _moe(blk_expert, n_used, x_sorted,
                    w_gate[0].astype(F32), w_up[0].astype(F32), w_down[0].astype(F32))
    y_tok = _sc_gather(y_sorted, dest)
    out = _combine(h2, info, final_norm_g.reshape(1, d), y_tok)
    return out.reshape(bsz, seq, d)
```

```python
import functools

import numpy as np
import jax
import jax.numpy as jnp
from jax import lax
from jax.experimental import pallas as pl
from jax.experimental.pallas import tpu as pltpu
from jax.experimental.pallas import tpu_sc as plsc

F32 = jnp.float32
BF16 = jnp.bfloat16
U32 = jnp.uint32

D_MODEL = 1024
N_META = 16
CHUNK_BITS = 6
CHUNK = 1 << CHUNK_BITS
EPS = 1e-6
HEADS = 4
HEAD_DIM = 128
WIDTH = HEADS * HEAD_DIM
HEADS_PER_STACK = 2
STACK = HEADS_PER_STACK * CHUNK
N_STACKS = HEADS // HEADS_PER_STACK
CONV_W = 4
NEUMANN_BITS = 3
N_GROUPS = 4
EXPERTS_PER_GROUP = 8
N_EXPERTS = N_GROUPS * EXPERTS_PER_GROUP
D_FF = 512
LANES = 128
SUBLANES = 8
HALF = D_MODEL // 2
MOE_ROWS = 512
CAST_SLAB = 128
SC_CORES, SC_SUBCORES = 2, 16
SC_WORKERS = SC_CORES * SC_SUBCORES
SC_CHUNK = 128
PROJ_TM = 1024
W_HGRN, W_GDN, W_SMALL = 0, 4 * WIDTH, 8 * WIDTH
HGRN_CHUNKS, GDN_CHUNKS = 4, 2
TOKEN_TM = 512
COMBINE_TM = 1024
VMEM_LIMIT = 56 * 1024 * 1024


def _bdot(a, b):
    return lax.dot_general(a.astype(BF16), b.astype(BF16), (((2,), (1,)), ((0,), (0,))),
                           preferred_element_type=F32)


def _bdot_nt(a, b):
    return lax.dot_general(a.astype(BF16), b.astype(BF16), (((2,), (2,)), ((0,), (0,))),
                           preferred_element_type=F32)


def _bdot_tn(a, b):
    return lax.dot_general(a.astype(BF16), b.astype(BF16), (((1,), (1,)), ((0,), (0,))),
                           preferred_element_type=F32)


def _split3(x):
    hi = x.astype(BF16)
    r1 = x - hi.astype(F32)
    mid = r1.astype(BF16)
    lo = (r1 - mid.astype(F32)).astype(BF16)
    return hi, mid, lo


def _sigmoid(x):
    return 1.0 / (1.0 + jnp.exp(-x))


def _silu(x):
    return x * _sigmoid(x)


def _softplus(x):
    return jnp.maximum(x, 0.0) + jnp.log(1.0 + jnp.exp(-jnp.abs(x)))


def _stack_heads(x, bsz):
    x3 = x.reshape(bsz, CHUNK, WIDTH)
    y = jnp.concatenate([x3[:, :, h * HEAD_DIM:(h + 1) * HEAD_DIM] for h in range(HEADS)], axis=1)
    return y.reshape(bsz * HEADS * CHUNK, HEAD_DIM)


def _unstack_heads(x, bsz):
    x3 = x.reshape(bsz, HEADS * CHUNK, HEAD_DIM)
    return jnp.concatenate([x3[:, h * CHUNK:(h + 1) * CHUNK, :] for h in range(HEADS)], axis=2)


def _to_stacks(a, bsz):
    return a.reshape(bsz * N_STACKS, STACK, a.shape[-1])


def _to_heads(a, bsz):
    return a.reshape(bsz * HEADS, CHUNK, a.shape[-1])


def _roll_rows(x, shift):
    n = x.shape[0]
    return pltpu.roll(x, shift % n, axis=0)


def _chunks_first(x, nc):
    return jnp.concatenate([x[:, j * CHUNK:(j + 1) * CHUNK] for j in range(nc)], axis=0)


def _chunks_first_scalars(s, nc):
    return jnp.concatenate([s[:, j] for j in range(nc)], axis=0)


def _load_chunks(ref, nc):
    x = _chunks_first(ref[...], nc)
    return x.reshape(x.shape[0] * CHUNK, WIDTH)


def _store_chunks(ref, x, nb, nc):
    ref[...] = jnp.concatenate([x[j * nb:(j + 1) * nb] for j in range(nc)], axis=1).astype(ref.dtype)


def _cast_rows(src_ref, dst_ref, col0=0):
    slab = CAST_SLAB
    rows, cols = src_ref.shape
    assert rows % slab == 0

    def body(k, carry):
        r = pl.multiple_of(k * slab, slab)
        dst_ref[pl.ds(r, slab), col0:col0 + cols] = src_ref[pl.ds(r, slab), :].astype(dst_ref.dtype)
        return carry

    lax.fori_loop(0, rows // slab, body, 0)


def _pack_halves(x):
    lo = lax.bitcast_convert_type(x[:, :HALF].astype(BF16).astype(F32), U32)
    hi = lax.bitcast_convert_type(x[:, HALF:].astype(BF16).astype(F32), U32)
    return (lo >> 16) | (hi & jnp.uint32(0xFFFF0000))


def _unpack_halves(p):
    lo = lax.bitcast_convert_type(p << 16, F32)
    hi = lax.bitcast_convert_type(p & jnp.uint32(0xFFFF0000), F32)
    return jnp.concatenate([lo, hi], axis=1)


def _normed_bf16(x_ref, g_ref):
    x = x_ref[...]
    return (x * lax.rsqrt(jnp.mean(x * x, axis=-1, keepdims=True) + EPS) * g_ref[...]).astype(BF16)


def _norm_proj_hgrn_kernel(x_ref, g_ref, lb_ref, w_ref, o_ref, wbf_ref):
    @pl.when(pl.program_id(0) == 0)
    def _():
        _cast_rows(w_ref, wbf_ref)

    acc = jnp.dot(_normed_bf16(x_ref, g_ref), wbf_ref[...], preferred_element_type=F32)
    w1, w2, w3 = WIDTH, 2 * WIDTH, 3 * WIDTH
    lb = lb_ref[...]
    o_ref[:, :w1] = _silu(acc[:, :w1])
    o_ref[:, w1:w2] = lb + (1.0 - lb) * _sigmoid(acc[:, w1:w2])
    o_ref[:, w2:w3] = acc[:, w2:w3]
    o_ref[:, w3:] = _silu(acc[:, w3:])


def _zero_rows(ref):
    slab = min(CAST_SLAB, ref.shape[0])
    assert ref.shape[0] % slab == 0

    def body(k, carry):
        ref[pl.ds(pl.multiple_of(k * slab, slab), slab), :] = jnp.zeros((slab, ref.shape[1]), ref.dtype)
        return carry

    lax.fori_loop(0, ref.shape[0] // slab, body, 0)


def _norm_proj_conv_kernel(x_ref, g_ref, w_ref, ws_ref, cw_ref, hist_ref, o_ref, tail_ref,
                           wbf_ref, acc_ref, carry_ref, *, steps_per_seq):
    i = pl.program_id(0)

    @pl.when(i == 0)
    def _():
        _cast_rows(w_ref, wbf_ref)
        _cast_rows(ws_ref, wbf_ref, 4 * WIDTH)
        _zero_rows(acc_ref)
        carry_ref[...] = jnp.zeros_like(carry_ref)

    xn = _normed_bf16(x_ref, g_ref)
    w3, w4 = 3 * WIDTH, 4 * WIDTH
    rows = x_ref.shape[0]
    first_of_seq = (i - 1) % steps_per_seq == 0
    cw = cw_ref[...]
    r8 = lax.broadcasted_iota(jnp.int32, (SUBLANES, WIDTH), 0)
    for c0 in range(0, w3, WIDTH):
        cols = slice(c0, c0 + WIDTH)
        raw = acc_ref[:, cols]
        hist = jnp.where(first_of_seq, hist_ref[:, cols], carry_ref[:, cols])
        tap = lambda d: cw[CONV_W - 1 - d:CONV_W - d, cols]
        body = raw * tap(0)
        head = raw[:SUBLANES]
        first = head * tap(0)
        for d in range(1, CONV_W):
            body = body + _roll_rows(raw, d) * tap(d)
            first = first + jnp.where(r8 >= d, _roll_rows(head, d), _roll_rows(hist, d)) * tap(d)
        o_ref[:SUBLANES, cols] = _silu(first)
        o_ref[SUBLANES:, cols] = _silu(body[SUBLANES:])
        carry_ref[:, cols] = raw[rows - SUBLANES:]
        acc_ref[:, cols] = jnp.dot(xn, wbf_ref[:, cols], preferred_element_type=F32)
    rest = acc_ref[:, w3:]
    o_ref[:, w3:w4] = _silu(rest[:, :WIDTH])
    o_ref[:, w4:] = rest[:, WIDTH:]
    acc_ref[:, w3:] = jnp.dot(xn, wbf_ref[:, w3:], preferred_element_type=F32)
    tail_ref[...] = carry_ref[...]


def _proj_call(kernel_fn, name, x2d, tm, n, operands, specs, extra_out=(), extra_scratch=(), delayed=False):
    m, k = x2d.shape
    tm = min(tm, m)
    tiles = m // tm
    x_map = (lambda i: (jnp.minimum(i, tiles - 1), 0)) if delayed else (lambda i: (i, 0))
    o_map = (lambda i: (jnp.maximum(i - 1, 0), 0)) if delayed else (lambda i: (i, 0))
    out_specs = [pl.BlockSpec((tm, n), o_map)] + [s for s, _ in extra_out]
    out_shape = [jax.ShapeDtypeStruct((m, n), F32)] + [s for _, s in extra_out]
    return pl.pallas_call(
        kernel_fn,
        grid=(tiles + int(delayed),),
        in_specs=[pl.BlockSpec((tm, k), x_map)] + specs,
        out_specs=out_specs,
        out_shape=out_shape,
        scratch_shapes=[pltpu.VMEM((k, n), BF16)] + list(extra_scratch),
        compiler_params=pltpu.CompilerParams(
            dimension_semantics=("arbitrary",), vmem_limit_bytes=VMEM_LIMIT),
        name=name,
    )(x2d, *operands)


def _weight_spec(k, n, col0):
    assert col0 % n == 0
    return pl.BlockSpec((None, k, n), lambda i: (0, 0, col0 // n), pipeline_mode=pl.Buffered(1))


def _norm_proj_hgrn(x2d, g, lb, w, tm):
    k, n = w.shape[1], 4 * WIDTH
    const = lambda i: (0, 0)
    specs = [pl.BlockSpec((1, k), const), pl.BlockSpec((1, WIDTH), const), _weight_spec(k, n, W_HGRN)]
    return _proj_call(_norm_proj_hgrn_kernel, "norm_proj_hgrn", x2d, tm, n, (g, lb, w), specs)[0]


def _norm_proj_conv(x2d, g, w, conv_w, hist, tm, rows_per_seq):
    k, n = w.shape[1], 4 * WIDTH + LANES
    const = lambda i: (0, 0)
    tm = min(tm, rows_per_seq)
    assert rows_per_seq % tm == 0 and x2d.shape[0] % rows_per_seq == 0
    tail_shape = (SUBLANES, 3 * WIDTH)
    specs = [pl.BlockSpec((1, k), const), _weight_spec(k, 4 * WIDTH, W_GDN), _weight_spec(k, LANES, W_SMALL),
             pl.BlockSpec((CONV_W, 3 * WIDTH), const), pl.BlockSpec(tail_shape, const)]
    return _proj_call(functools.partial(_norm_proj_conv_kernel, steps_per_seq=rows_per_seq // tm),
                      "norm_proj_gdn", x2d, tm, n, (g, w, w, conv_w, hist), specs,
                      extra_out=[(pl.BlockSpec(tail_shape, const), jax.ShapeDtypeStruct(tail_shape, F32))],
                      extra_scratch=[pltpu.VMEM((tm, n), F32), pltpu.VMEM(tail_shape, F32)], delayed=True)


HGRN_LEVELS = 6
HGRN_DIAG = HGRN_LEVELS


def _hgrn_level_table():
    r = np.arange(STACK)
    t, hd = r % CHUNK, r // CHUNK
    x = t[:, None] ^ t[None, :]
    lv = np.floor(np.log2(np.maximum(x, 1))).astype(np.int32)
    valid = (hd[:, None] == hd[None, :]) & (t[:, None] > t[None, :])
    lv = np.where(valid, lv, -1)
    lv = np.where(r[:, None] == r[None, :], HGRN_DIAG, lv)
    return jnp.asarray(lv, dtype=jnp.int32)


def _hgrn_kernel(lv_ref, g_ref, st0_ref, q_ref, f_ref, v_ref, gate_ref, o_ref, stout_ref, st_ref, *, nb, nc):
    @pl.when(pl.program_id(0) == 0)
    def _():
        st_ref[...] = st0_ref[...]

    bsz = nc * nb
    q = _load_chunks(q_ref, nc)
    f = _load_chunks(f_ref, nc)
    v = _load_chunks(v_ref, nc)
    gate = _load_chunks(gate_ref, nc)

    per_seq = lambda a: a.reshape(bsz, CHUNK, WIDTH)
    trow = lax.broadcasted_iota(jnp.int32, (1, CHUNK, WIDTH), 1)
    b = jnp.log(f)
    s = 1
    while s < CHUNK:
        b = (per_seq(b) + jnp.where(trow >= s, per_seq(_roll_rows(b, s)), 0.0)).reshape(b.shape)
        s *= 2

    qs, ks, vs, bs = (_stack_heads(a, bsz) for a in (q, 1.0 - f, v, b))
    lv = lv_ref[...][None]
    to3 = functools.partial(_to_stacks, bsz=bsz)
    to_g = functools.partial(_to_heads, bsz=bsz)
    roll_g = lambda a, shift: to_g(_roll_rows(a.reshape(qs.shape), shift))
    t = lax.broadcasted_iota(jnp.int32, (1, CHUNK, HEAD_DIM), 1)
    qs_g, ks_g, bs_g = to_g(qs), to_g(ks), to_g(bs)

    attn = jnp.where(lv == HGRN_DIAG, _bdot_nt(to3(qs), to3(ks)), 0.0)
    bref = bs_g
    bnext = roll_g(bs_g, -1)
    for li in range(HGRN_LEVELS):
        m = 1 << li
        if li > 0:
            half = m // 2
            upper = (t & (m - 1)) >= half
            bref = jnp.where(upper, roll_g(bref, half), bref)
            bnext = jnp.where(upper, bnext, roll_g(bnext, -half))
        odd = ((t >> li) & 1) == 1
        both = jnp.where(odd, qs_g, ks_g) * jnp.exp(jnp.where(odd, bs_g - bref, bnext - bs_g))
        both = to3(both.reshape(qs.shape)).astype(BF16)
        attn = jnp.where(lv == li, _bdot_nt(both, both), attn)

    o = _bdot(attn, to3(vs))

    n_g = nb * HEADS
    st = st_ref[...]
    qe = qs_g * jnp.exp(bs_g)
    b_end = bs_g[:, CHUNK - 1:CHUNK, :]
    ke = ks_g * jnp.exp(b_end - bs_g)
    vs_g = to_g(vs)
    o_inter = []
    for j in range(nc):
        sl = slice(j * n_g, (j + 1) * n_g)
        o_inter.append(_bdot_nt(qe[sl], st))
        st = st * jnp.exp(b_end[sl]) + _bdot_tn(vs_g[sl], ke[sl])
    st_ref[...] = st
    stout_ref[...] = st
    o = (to_g(o) + jnp.concatenate(o_inter, axis=0)).reshape(qs.shape)

    o = o * lax.rsqrt(jnp.mean(o * o, axis=-1, keepdims=True) + EPS) * g_ref[...]
    o = o * _stack_heads(gate, bsz)
    _store_chunks(o_ref, _unstack_heads(o, bsz), nb, nc)


def _gdn_cumsum_matrix():
    r = np.arange(STACK)
    u = (r[:, None] // CHUNK == r[None, :] // CHUNK) & (r[:, None] <= r[None, :])
    return jnp.asarray(u, dtype=BF16)


def _gdn_kernel(u_ref, alog_ref, dt_ref, g_ref, s_ref, st0_ref, q_ref, k_ref, v_ref, z_ref,
                o_ref, stout_ref, st_ref, *, nb, nc):
    @pl.when(pl.program_id(0) == 0)
    def _():
        st_ref[...] = st0_ref[...]

    bsz = nc * nb
    qs = _stack_heads(_load_chunks(q_ref, nc), bsz)
    ks = _stack_heads(_load_chunks(k_ref, nc), bsz)
    vs = _stack_heads(_load_chunks(v_ref, nc), bsz)
    gate = _load_chunks(z_ref, nc)
    qs = qs * lax.rsqrt(jnp.sum(qs * qs, axis=-1, keepdims=True) + EPS) * (HEAD_DIM ** -0.5)
    ks = ks * lax.rsqrt(jnp.sum(ks * ks, axis=-1, keepdims=True) + EPS)
    to3 = functools.partial(_to_stacks, bsz=bsz)
    qs, ks, vs = to3(qs), to3(ks), to3(vs)
    n_st = bsz * N_STACKS

    srow = _chunks_first_scalars(s_ref[...], nc).reshape(n_st, 2, STACK)
    per_stack = lambda ref: jnp.broadcast_to(ref[...][None], (bsz, N_STACKS, 1, STACK)).reshape(n_st, 1, STACK)
    beta_row = _sigmoid(srow[:, 0:1, :])
    g_row = -jnp.exp(per_stack(alog_ref)) * _softplus(srow[:, 1:2, :] + per_stack(dt_ref))
    n8 = n_st * SUBLANES
    g8 = jnp.broadcast_to(g_row, (n_st, SUBLANES, STACK)).reshape(n8, STACK)
    pieces = jnp.concatenate(_split3(g8), axis=0)
    cum = jnp.dot(pieces, u_ref[...], preferred_element_type=F32)
    cum = cum[0:n8] + cum[n8:2 * n8] + cum[2 * n8:]
    cum_row = cum.reshape(n_st, SUBLANES, STACK)[:, 0:1, :]

    r = lax.broadcasted_iota(jnp.int32, (STACK, STACK), 0)
    cc = lax.broadcasted_iota(jnp.int32, (STACK, STACK), 1)
    eye = (r == cc)[None]
    cum_col = jnp.sum(jnp.where(eye, cum_row, 0.0), axis=2, keepdims=True)
    beta_col = jnp.sum(jnp.where(eye, beta_row, 0.0), axis=2, keepdims=True)
    same = (r >> CHUNK_BITS) == (cc >> CHUNK_BITS)
    lower = (same & (r >= cc))[None]
    strict = (same & (r > cc))[None]
    ratio = jnp.exp(jnp.minimum(cum_col - cum_row, 0.0))

    kkt = _bdot_nt(ks, ks)
    qkt = _bdot_nt(qs, ks)
    lm = jnp.where(strict, beta_col * ratio * kkt, 0.0)

    ident = eye.astype(F32)
    l0 = jnp.where(((r >> NEUMANN_BITS) == (cc >> NEUMANN_BITS))[None], lm, 0.0)
    l2 = _bdot(l0, l0)
    l4 = _bdot(l2, l2)
    xinv = _bdot(_bdot(ident - l0, ident + l2), ident + l4)
    for bits in range(NEUMANN_BITS, CHUNK_BITS):
        off = (((r >> (bits + 1)) == (cc >> (bits + 1))) & ((r >> bits) != (cc >> bits)))[None]
        xinv = xinv - _bdot(_bdot(xinv, jnp.where(off, lm, 0.0)), xinv)

    decay_col = jnp.exp(cum_col)
    rhs = jnp.concatenate([beta_col * vs, (beta_col * decay_col) * ks], axis=2)
    sol = _bdot(xinv, rhs)

    to_g = functools.partial(_to_heads, bsz=bsz)
    n_g = nb * HEADS
    st = st_ref[...]
    u0_g, w_g, q_g = to_g(sol[:, :, :HEAD_DIM]), to_g(sol[:, :, HEAD_DIM:]), to_g(qs)
    cum_g = to_g(cum_col)
    cum_end = cum_g[:, CHUNK - 1:CHUNK, :]
    kdec = to_g(ks) * jnp.exp(cum_end - cum_g)
    decay_g = to_g(decay_col)
    us, o_inter = [], []
    for j in range(nc):
        sl = slice(j * n_g, (j + 1) * n_g)
        u_j = u0_g[sl] - _bdot(w_g[sl], st)
        us.append(u_j)
        o_inter.append(decay_g[sl] * _bdot(q_g[sl], st))
        st = jnp.exp(cum_end[sl]) * st + _bdot_tn(kdec[sl], u_j)
    st_ref[...] = st
    stout_ref[...] = st
    attn = jnp.where(lower, qkt * ratio, 0.0)
    u = jnp.concatenate(us, axis=0).reshape(bsz * HEADS * CHUNK, HEAD_DIM)
    o = jnp.concatenate(o_inter, axis=0) + to_g(_bdot(attn, to3(u)))
    o = o.reshape(bsz * HEADS * CHUNK, HEAD_DIM)

    o = o * lax.rsqrt(jnp.mean(o * o, axis=-1, keepdims=True) + EPS) * g_ref[...]
    o = o * _stack_heads(gate, bsz)
    _store_chunks(o_ref, _unstack_heads(o, bsz), nb, nc)


def _recurrence_call(kernel_fn, name, proj, state0, operands, specs, nc):
    nb, length, _ = proj.shape
    rows = nc * CHUNK
    state_shape = (nb * HEADS, HEAD_DIM, HEAD_DIM)
    state_spec = pl.BlockSpec(state_shape, lambda c: (0, 0, 0))
    chunk_specs = [pl.BlockSpec((nb, rows, WIDTH), functools.partial(lambda c, j: (0, c, j), j=j))
                   for j in range(4)]
    return pl.pallas_call(
        functools.partial(kernel_fn, nb=nb, nc=nc),
        grid=(length // rows,),
        in_specs=specs + [state_spec] + chunk_specs,
        out_specs=[pl.BlockSpec((nb, rows, WIDTH), lambda c: (0, c, 0)), state_spec],
        out_shape=[jax.ShapeDtypeStruct((nb, length, WIDTH), BF16),
                   jax.ShapeDtypeStruct(state_shape, F32)],
        scratch_shapes=[pltpu.VMEM(state_shape, F32)],
        compiler_params=pltpu.CompilerParams(
            dimension_semantics=("arbitrary",), vmem_limit_bytes=VMEM_LIMIT),
        name=name,
    )(*operands, state0, *([proj] * 4))


def _hgrn(proj, state0, norm_g, nc):
    const = lambda c: (0, 0)
    specs = [pl.BlockSpec((STACK, STACK), const), pl.BlockSpec((1, HEAD_DIM), const)]
    return _recurrence_call(_hgrn_kernel, "hgrn", proj, state0, (_hgrn_level_table(), norm_g), specs, nc)


def _gdn(proj, state0, scal, a_log_row, dt_row, norm_g, nc):
    nb = proj.shape[0]
    const = lambda c: (0, 0)
    const3 = lambda c: (0, 0, 0)
    specs = [pl.BlockSpec((STACK, STACK), const),
             pl.BlockSpec((N_STACKS, 1, STACK), const3),
             pl.BlockSpec((N_STACKS, 1, STACK), const3),
             pl.BlockSpec((1, HEAD_DIM), const),
             pl.BlockSpec((nb, nc, N_STACKS, 2, STACK), lambda c: (0, c, 0, 0, 0))]
    return _recurrence_call(_gdn_kernel, "gdn", proj, state0,
                            (_gdn_cumsum_matrix(), a_log_row, dt_row, norm_g, scal), specs, nc)


ROUTE_LANE0 = N_GROUPS
ROUTE_NEG = -1e30


def _route_tile(lg, ltri, carry):
    lane = lax.broadcasted_iota(jnp.int32, lg.shape, 1).astype(F32)
    first = lambda mask: jnp.min(jnp.where(mask, lane, float(LANES)), axis=1, keepdims=True)
    top = lambda mask: jnp.max(jnp.where(mask, lg, ROUTE_NEG), axis=1, keepdims=True)

    is_g = lane < N_GROUPS
    gmax = top(is_g)
    grp = first(is_g & (lg == gmax))
    p_grp = 1.0 / jnp.sum(jnp.where(is_g, jnp.exp(lg - gmax), 0.0), axis=1, keepdims=True)

    lo = ROUTE_LANE0 + grp * EXPERTS_PER_GROUP
    in_grp = (lane >= lo) & (lane < lo + EXPERTS_PER_GROUP)
    m1 = top(in_grp)
    i1 = first(in_grp & (lg == m1))
    rest = in_grp & (lane != i1)
    m2 = top(rest)
    i2 = first(rest & (lg == m2))
    r = jnp.exp(m2 - m1)
    w0 = p_grp / (1.0 + r)
    w1 = p_grp * r / (1.0 + r)

    hot0 = (lane == i1).astype(F32)
    hot1 = (lane == i2).astype(F32)
    both = hot0 + hot1
    before = jnp.dot(ltri, both.astype(BF16), preferred_element_type=F32) + carry
    rank0 = jnp.sum(hot0 * before, axis=1, keepdims=True)
    rank1 = jnp.sum(hot1 * before, axis=1, keepdims=True)
    cols = (w0, w1, i1 - ROUTE_LANE0, i2 - ROUTE_LANE0, rank0, rank1)
    info = jnp.zeros_like(lg)
    for k, col in enumerate(cols):
        info = jnp.where(lane == k, col, info)
    return info, carry + jnp.sum(both, axis=0, keepdims=True)


def _merge_kernel(oa_ref, ob_ref, x_ref, xprev_ref, mg_ref, wgate_ref, hgup_ref, gdup_ref, wout_ref,
                  ng_ref, rw_ref, rb_ref, ltri_ref, h2_ref, xn_ref, info_ref, ids_ref, cnt_ref,
                  wgate_bf, gates_ref, ua_ref, ub_ref, carry_ref):
    i = pl.program_id(0)

    @pl.when(i == 0)
    def _():
        carry_ref[...] = jnp.zeros_like(carry_ref)
        _cast_rows(wgate_ref, wgate_bf)
        _zero_rows(gates_ref)
        _zero_rows(ua_ref)
        _zero_rows(ub_ref)

    gates = _sigmoid(gates_ref[...])
    merged = (gates[:, :D_MODEL] * ua_ref[...] + gates[:, D_MODEL:] * ub_ref[...]).astype(BF16)
    gates_ref[...] = jnp.dot(_normed_bf16(x_ref, mg_ref), wgate_bf[...], preferred_element_type=F32)
    h2 = xprev_ref[...] + jnp.dot(merged, wout_ref[...], preferred_element_type=F32)
    h2_ref[...] = h2
    xn = h2 * lax.rsqrt(jnp.mean(h2 * h2, axis=-1, keepdims=True) + EPS) * ng_ref[...]
    xn_ref[...] = _pack_halves(xn)
    lg = jnp.dot(xn.astype(BF16), rw_ref[...], preferred_element_type=F32) + rb_ref[...]
    ua_ref[...] = jnp.dot(oa_ref[...], hgup_ref[...], preferred_element_type=F32)
    ub_ref[...] = jnp.dot(ob_ref[...], gdup_ref[...], preferred_element_type=F32)
    old = carry_ref[...]
    info, carry = _route_tile(lg, ltri_ref[...], old)
    info_ref[...] = info
    ids_ref[...] = info.T[:SUBLANES].astype(jnp.int32)
    carry = jnp.where(i > 0, carry, old)
    carry_ref[...] = carry
    cnt_ref[...] = carry


def _merge(o_a, o_b, x2d, mix_g, w_gates, hg_up, gd_up, w_out, norm_g, rw, rb):
    t = x2d.shape[0]
    tm = TOKEN_TM
    tiles = t // tm
    new = lambda i: (jnp.minimum(i, tiles - 1), 0)
    row = lambda i: (jnp.maximum(i - 1, 0), 0)
    const = lambda i: (0, 0)
    ltri = jnp.asarray(np.tril(np.ones((tm, tm), np.float32), -1), dtype=BF16)
    return pl.pallas_call(
        _merge_kernel,
        grid=(tiles + 1,),
        in_specs=[pl.BlockSpec((tm, WIDTH), new),
                  pl.BlockSpec((tm, WIDTH), new),
                  pl.BlockSpec((tm, D_MODEL), new),
                  pl.BlockSpec((tm, D_MODEL), row),
                  pl.BlockSpec((1, D_MODEL), const),
                  pl.BlockSpec((D_MODEL, 2 * D_MODEL), const, pipeline_mode=pl.Buffered(1)),
                  pl.BlockSpec((WIDTH, D_MODEL), const),
                  pl.BlockSpec((WIDTH, D_MODEL), const),
                  pl.BlockSpec((D_MODEL, D_MODEL), const),
                  pl.BlockSpec((1, D_MODEL), const),
                  pl.BlockSpec((D_MODEL, LANES), const),
                  pl.BlockSpec((1, LANES), const),
                  pl.BlockSpec((tm, tm), const)],
        out_specs=[pl.BlockSpec((tm, D_MODEL), row),
                   pl.BlockSpec((tm, HALF), row),
                   pl.BlockSpec((tm, LANES), row),
                   pl.BlockSpec((SUBLANES, tm), lambda i: (0, jnp.maximum(i - 1, 0))),
                   pl.BlockSpec((1, LANES), const)],
        out_shape=[jax.ShapeDtypeStruct((t, D_MODEL), F32),
                   jax.ShapeDtypeStruct((t, HALF), U32),
                   jax.ShapeDtypeStruct((t, LANES), F32),
                   jax.ShapeDtypeStruct((SUBLANES, t), jnp.int32),
                   jax.ShapeDtypeStruct((1, LANES), F32)],
        scratch_shapes=[pltpu.VMEM((D_MODEL, 2 * D_MODEL), BF16), pltpu.VMEM((tm, 2 * D_MODEL), F32),
                        pltpu.VMEM((tm, D_MODEL), F32), pltpu.VMEM((tm, D_MODEL), F32),
                        pltpu.VMEM((1, LANES), F32)],
        compiler_params=pltpu.CompilerParams(
            dimension_semantics=("arbitrary",), vmem_limit_bytes=VMEM_LIMIT),
        name="merge",
    )(o_a, o_b, x2d, x2d, mix_g, w_gates, hg_up, gd_up, w_out, norm_g, rw, rb, ltri)


def _sc_gather(table, idx):
    n_idx = idx.shape[0]
    cols = table.shape[1]
    per_worker = n_idx // SC_WORKERS
    assert n_idx % (SC_WORKERS * SC_CHUNK) == 0
    mesh = plsc.VectorSubcoreMesh(core_axis_name="c", subcore_axis_name="s")

    @functools.partial(
        pl.kernel, mesh=mesh,
        out_type=jax.ShapeDtypeStruct((n_idx, cols), table.dtype),
        scratch_types=[pltpu.VMEM((SC_CHUNK,), jnp.int32),
                       pltpu.VMEM((SC_CHUNK, cols), table.dtype),
                       pltpu.SemaphoreType.DMA],
    )
    def gather(table_hbm, idx_hbm, out_hbm, idx_v, rows_v, sem):
        worker = lax.axis_index("s") * SC_CORES + lax.axis_index("c")
        base = worker * per_worker

        @pl.loop(0, per_worker // SC_CHUNK)
        def _(c):
            off = pl.multiple_of(base + c * SC_CHUNK, SC_CHUNK)
            pltpu.sync_copy(idx_hbm.at[pl.ds(off, SC_CHUNK)], idx_v)
            pltpu.async_copy(table_hbm.at[idx_v], rows_v, sem).wait()
            pltpu.sync_copy(rows_v, out_hbm.at[pl.ds(off, SC_CHUNK)])

    return gather(table, idx)


def _sc_dispatch(rows, dest, n_out):
    t, cols = rows.shape
    per_worker = t // SC_WORKERS
    assert t % (SC_WORKERS * SC_CHUNK) == 0 and dest.shape[0] == 2 * t
    mesh = plsc.VectorSubcoreMesh(core_axis_name="c", subcore_axis_name="s")

    @functools.partial(
        pl.kernel, mesh=mesh,
        out_type=jax.ShapeDtypeStruct((n_out, cols), rows.dtype),
        scratch_types=[pltpu.VMEM((SC_CHUNK,), jnp.int32),
                       pltpu.VMEM((SC_CHUNK, cols), rows.dtype)],
    )
    def dispatch(rows_hbm, dest_hbm, out_hbm, idx_v, rows_v):
        worker = lax.axis_index("s") * SC_CORES + lax.axis_index("c")
        base = worker * per_worker

        @pl.loop(0, per_worker // SC_CHUNK)
        def _(c):
            off = pl.multiple_of(base + c * SC_CHUNK, SC_CHUNK)
            pltpu.sync_copy(rows_hbm.at[pl.ds(off, SC_CHUNK)], rows_v)
            for slot in range(2):
                pltpu.sync_copy(dest_hbm.at[pl.ds(slot * t + off, SC_CHUNK)], idx_v)
                pltpu.sync_copy(rows_v, out_hbm.at[idx_v])

    return dispatch(rows, dest)


def _moe_kernel(be_ref, nu_ref, x_ref, wg_ref, wu_ref, wd_ref, y_ref, wg_bf, wu_bf, wd_bf):
    i = pl.program_id(0)

    @pl.when((i == 0) | (be_ref[i] != be_ref[jnp.maximum(i - 1, 0)]))
    def _():
        _cast_rows(wg_ref, wg_bf)
        _cast_rows(wu_ref, wu_bf)
        _cast_rows(wd_ref, wd_bf)

    @pl.when(i < nu_ref[0])
    def _():
        xb = _unpack_halves(x_ref[...]).astype(BF16)
        a = jnp.dot(xb, wg_bf[...], preferred_element_type=F32)
        u = jnp.dot(xb, wu_bf[...], preferred_element_type=F32)
        y = jnp.dot((_silu(a) * u).astype(BF16), wd_bf[...], preferred_element_type=F32)
        y_ref[...] = _pack_halves(y)

    @pl.when(i >= nu_ref[0])
    def _():
        y_ref[...] = jnp.zeros_like(y_ref)


def _moe(blk_expert, n_used, x_sorted, w_gate, w_up, w_down):
    n_blocks = blk_expert.shape[0]
    wspec = lambda shape: pl.BlockSpec((None,) + shape, lambda i, be, nu: (be[i], 0, 0))
    rows = pl.BlockSpec((MOE_ROWS, HALF), lambda i, be, nu: (i, 0))
    grid_spec = pltpu.PrefetchScalarGridSpec(
        num_scalar_prefetch=2,
        grid=(n_blocks,),
        in_specs=[rows, wspec((D_MODEL, D_FF)), wspec((D_MODEL, D_FF)), wspec((D_FF, D_MODEL))],
        out_specs=rows,
        scratch_shapes=[pltpu.VMEM((D_MODEL, D_FF), BF16), pltpu.VMEM((D_MODEL, D_FF), BF16),
                        pltpu.VMEM((D_FF, D_MODEL), BF16)],
    )
    return pl.pallas_call(
        _moe_kernel,
        grid_spec=grid_spec,
        out_shape=jax.ShapeDtypeStruct(x_sorted.shape, U32),
        compiler_params=pltpu.CompilerParams(
            dimension_semantics=("arbitrary",), vmem_limit_bytes=VMEM_LIMIT),
        name="moe",
    )(blk_expert, n_used, x_sorted, w_gate, w_up, w_down)


def _combine_kernel(h2_ref, rw_ref, g_ref, y0_ref, y1_ref, o_ref):
    rw = rw_ref[...]
    h = h2_ref[...] + rw[:, 0:1] * _unpack_halves(y0_ref[...]) + rw[:, 1:2] * _unpack_halves(y1_ref[...])
    o_ref[...] = h * lax.rsqrt(jnp.mean(h * h, axis=-1, keepdims=True) + EPS) * g_ref[...]


def _combine(h2, rweights, final_g, y):
    t = h2.shape[0]
    tm = min(COMBINE_TM, t)
    row = lambda i: (i, 0)
    return pl.pallas_call(
        _combine_kernel,
        grid=(t // tm,),
        in_specs=[pl.BlockSpec((tm, D_MODEL), row),
                  pl.BlockSpec((tm, LANES), row),
                  pl.BlockSpec((1, D_MODEL), lambda i: (0, 0)),
                  pl.BlockSpec((tm, HALF), row),
                  pl.BlockSpec((tm, HALF), lambda i: (i + t // tm, 0))],
        out_specs=pl.BlockSpec((tm, D_MODEL), row),
        out_shape=jax.ShapeDtypeStruct((t, D_MODEL), F32),
        compiler_params=pltpu.CompilerParams(
            dimension_semantics=("arbitrary",), vmem_limit_bytes=VMEM_LIMIT),
        name="combine",
    )(h2, rweights, final_g, y, y)


def _block_layout(ids, counts_row, t):
    n_blocks = 2 * t // MOE_ROWS + N_EXPERTS
    counts = counts_row[0, ROUTE_LANE0:ROUTE_LANE0 + N_EXPERTS].astype(jnp.int32)
    padded = ((counts + MOE_ROWS - 1) // MOE_ROWS) * MOE_ROWS
    pend = jnp.cumsum(padded)
    pstart = pend - padded
    blk_first = jnp.arange(n_blocks, dtype=jnp.int32) * MOE_ROWS
    blk_expert = jnp.minimum(jnp.sum((pend[None, :] <= blk_first[:, None]).astype(jnp.int32), axis=1),
                             N_EXPERTS - 1).astype(jnp.int32)
    n_used = (pend[-1] // MOE_ROWS).astype(jnp.int32).reshape(1)
    expert, rank = ids[2:4], ids[4:6]
    experts = jnp.arange(N_EXPERTS, dtype=jnp.int32)
    first_row = jnp.sum(jnp.where(expert[:, :, None] == experts, pstart, 0), axis=-1)
    dest = (first_row + rank).reshape(2 * t)
    return dest, blk_expert, n_used, n_blocks * MOE_ROWS


def kernel(x, meta_tokens, hg_lb_logits, norm_mix_g, w_in, gd_conv_w, gd_A_log, gd_dt_bias, hg_norm_g, gd_norm_g, hg_up, gd_up, w_out, norm_ffn_g, router_group_w, router_group_b, router_expert_w, router_expert_b, w_gate, w_up, w_down, final_norm_g):
    bsz, seq, d = x.shape
    t = bsz * seq
    x2d = x.reshape(t, d)

    lb = jnp.cumsum(jax.nn.softmax(hg_lb_logits.astype(F32), axis=0), axis=0)[0].reshape(1, WIDTH)
    w_all = w_in.astype(F32)
    w_gates = w_all[0, :, W_SMALL + 2 * HEADS:]
    g_mix = norm_mix_g[0].reshape(1, d)
    meta_blk = jnp.concatenate([jnp.zeros((CHUNK - N_META, d), F32), meta_tokens.astype(F32)], axis=0)

    conv_w = gd_conv_w[0].astype(F32)
    no_history = jnp.zeros((SUBLANES, 3 * WIDTH), F32)
    proj, proj_meta = {}, {}
    proj["hgrn"] = _norm_proj_hgrn(x2d, g_mix, lb, w_all, PROJ_TM)
    proj_meta["hgrn"] = _norm_proj_hgrn(meta_blk, g_mix, lb, w_all, CHUNK)
    proj_meta["gdn"], meta_tail = _norm_proj_conv(meta_blk, g_mix, w_all, conv_w, no_history, CHUNK, CHUNK)
    proj["gdn"], _ = _norm_proj_conv(x2d, g_mix, w_all, conv_w, meta_tail, PROJ_TM, seq)

    def scalar_rows(p, nb, nc):
        s = p[:, 4 * WIDTH:4 * WIDTH + 2 * HEADS].reshape(nb, nc, CHUNK, 2, N_STACKS, HEADS_PER_STACK)
        return s.transpose(0, 1, 4, 3, 5, 2).reshape(nb, nc, N_STACKS, 2, STACK)

    a_log_row = jnp.repeat(gd_A_log[0].astype(F32), CHUNK).reshape(N_STACKS, 1, STACK)
    dt_row = jnp.repeat(gd_dt_bias[0].astype(F32), CHUNK).reshape(N_STACKS, 1, STACK)
    hg_g, gd_g = hg_norm_g[0].reshape(1, HEAD_DIM), gd_norm_g[0].reshape(1, HEAD_DIM)

    zero_state = jnp.zeros((HEADS, HEAD_DIM, HEAD_DIM), F32)
    per_seq = lambda st: jnp.tile(st, (bsz, 1, 1))
    _, hg_state = _hgrn(proj_meta["hgrn"][None], zero_state, hg_g, 1)
    o_a, _ = _hgrn(proj["hgrn"].reshape(bsz, seq, -1), per_seq(hg_state), hg_g, HGRN_CHUNKS)
    _, gd_state = _gdn(proj_meta["gdn"][None], zero_state, scalar_rows(proj_meta["gdn"], 1, 1),
                       a_log_row, dt_row, gd_g, 1)
    o_b, _ = _gdn(proj["gdn"].reshape(bsz, seq, -1), per_seq(gd_state),
                  scalar_rows(proj["gdn"], bsz, seq // CHUNK), a_log_row, dt_row, gd_g, GDN_CHUNKS)

    rw = jnp.zeros((d, LANES), F32)
    rw = rw.at[:, :N_GROUPS].set(router_group_w[0]).at[:, N_GROUPS:N_GROUPS + N_EXPERTS].set(router_expert_w[0])
    rb = jnp.zeros((1, LANES), F32)
    rb = rb.at[0, :N_GROUPS].set(router_group_b[0]).at[0, N_GROUPS:N_GROUPS + N_EXPERTS].set(router_expert_b[0])
    h2, xn, info, ids, counts = _merge(o_a.reshape(t, WIDTH), o_b.reshape(t, WIDTH), x2d, g_mix, w_gates,
                                  hg_up[0].astype(BF16), gd_up[0].astype(BF16), w_out[0].astype(BF16),
                                  norm_ffn_g[0].reshape(1, d), rw.astype(BF16), rb)

    dest, blk_expert, n_used, n_rows = _block_layout(ids, counts, t)
    x_sorted = _sc_dispatch(xn, dest, n_rows)
    y_sorted = _moe(blk_expert, n_used, x_sorted,
                    w_gate[0].astype(F32), w_up[0].astype(F32), w_down[0].astype(F32))
    y_tok = _sc_gather(y_sorted, dest)
    out = _combine(h2, info, final_norm_g.reshape(1, d), y_tok)
    return out.reshape(bsz, seq, d)
```

```python
import functools

import numpy as np
import jax
import jax.numpy as jnp
from jax import lax
from jax.experimental import pallas as pl
from jax.experimental.pallas import tpu as pltpu
from jax.experimental.pallas import tpu_sc as plsc

F32 = jnp.float32
BF16 = jnp.bfloat16
U32 = jnp.uint32

D_MODEL = 1024
N_META = 16
CHUNK_BITS = 6
CHUNK = 1 << CHUNK_BITS
EPS = 1e-6
HEADS = 4
HEAD_DIM = 128
WIDTH = HEADS * HEAD_DIM
HEADS_PER_STACK = 2
STACK = HEADS_PER_STACK * CHUNK
N_STACKS = HEADS // HEADS_PER_STACK
CONV_W = 4
NEUMANN_BITS = 3
N_GROUPS = 4
EXPERTS_PER_GROUP = 8
N_EXPERTS = N_GROUPS * EXPERTS_PER_GROUP
D_FF = 512
LANES = 128
SUBLANES = 8
HALF = D_MODEL // 2
MOE_ROWS = 512
CAST_SLAB = 128
SC_CORES, SC_SUBCORES = 2, 16
SC_WORKERS = SC_CORES * SC_SUBCORES
SC_CHUNK = 128
PROJ_TM = 1024
W_HGRN, W_GDN, W_SMALL = 0, 4 * WIDTH, 8 * WIDTH
HGRN_CHUNKS, GDN_CHUNKS = 4, 4
TOKEN_TM = 512
COMBINE_TM = 1024
VMEM_LIMIT = 56 * 1024 * 1024


def _bdot(a, b):
    return lax.dot_general(a.astype(BF16), b.astype(BF16), (((2,), (1,)), ((0,), (0,))),
                           preferred_element_type=F32)


def _bdot_nt(a, b):
    return lax.dot_general(a.astype(BF16), b.astype(BF16), (((2,), (2,)), ((0,), (0,))),
                           preferred_element_type=F32)


def _bdot_tn(a, b):
    return lax.dot_general(a.astype(BF16), b.astype(BF16), (((1,), (1,)), ((0,), (0,))),
                           preferred_element_type=F32)


def _split3(x):
    hi = x.astype(BF16)
    r1 = x - hi.astype(F32)
    mid = r1.astype(BF16)
    lo = (r1 - mid.astype(F32)).astype(BF16)
    return hi, mid, lo


def _sigmoid(x):
    return 1.0 / (1.0 + jnp.exp(-x))


def _silu(x):
    return x * _sigmoid(x)


def _softplus(x):
    return jnp.maximum(x, 0.0) + jnp.log(1.0 + jnp.exp(-jnp.abs(x)))


def _stack_heads(x, bsz):
    x3 = x.reshape(bsz, CHUNK, WIDTH)
    y = jnp.concatenate([x3[:, :, h * HEAD_DIM:(h + 1) * HEAD_DIM] for h in range(HEADS)], axis=1)
    return y.reshape(bsz * HEADS * CHUNK, HEAD_DIM)


def _unstack_heads(x, bsz):
    x3 = x.reshape(bsz, HEADS * CHUNK, HEAD_DIM)
    return jnp.concatenate([x3[:, h * CHUNK:(h + 1) * CHUNK, :] for h in range(HEADS)], axis=2)


def _to_stacks(a, bsz):
    return a.reshape(bsz * N_STACKS, STACK, a.shape[-1])


def _to_heads(a, bsz):
    return a.reshape(bsz * HEADS, CHUNK, a.shape[-1])


def _roll_rows(x, shift):
    n = x.shape[0]
    return pltpu.roll(x, shift % n, axis=0)


def _chunks_first(x, nc):
    return jnp.concatenate([x[:, j * CHUNK:(j + 1) * CHUNK] for j in range(nc)], axis=0)


def _chunks_first_scalars(s, nc):
    return jnp.concatenate([s[:, j] for j in range(nc)], axis=0)


def _load_chunks(ref, nc):
    x = _chunks_first(ref[...], nc)
    return x.reshape(x.shape[0] * CHUNK, WIDTH)


def _store_chunks(ref, x, nb, nc):
    ref[...] = jnp.concatenate([x[j * nb:(j + 1) * nb] for j in range(nc)], axis=1).astype(ref.dtype)


def _cast_rows(src_ref, dst_ref, col0=0):
    slab = CAST_SLAB
    rows, cols = src_ref.shape
    assert rows % slab == 0

    def body(k, carry):
        r = pl.multiple_of(k * slab, slab)
        dst_ref[pl.ds(r, slab), col0:col0 + cols] = src_ref[pl.ds(r, slab), :].astype(dst_ref.dtype)
        return carry

    lax.fori_loop(0, rows // slab, body, 0)


def _pack_halves(x):
    lo = lax.bitcast_convert_type(x[:, :HALF].astype(BF16).astype(F32), U32)
    hi = lax.bitcast_convert_type(x[:, HALF:].astype(BF16).astype(F32), U32)
    return (lo >> 16) | (hi & jnp.uint32(0xFFFF0000))


def _unpack_halves(p):
    lo = lax.bitcast_convert_type(p << 16, F32)
    hi = lax.bitcast_convert_type(p & jnp.uint32(0xFFFF0000), F32)
    return jnp.concatenate([lo, hi], axis=1)


def _normed_bf16(x_ref, g_ref):
    x = x_ref[...]
    return (x * lax.rsqrt(jnp.mean(x * x, axis=-1, keepdims=True) + EPS) * g_ref[...]).astype(BF16)


def _norm_proj_hgrn_kernel(x_ref, g_ref, lb_ref, w_ref, o_ref, wbf_ref):
    @pl.when(pl.program_id(0) == 0)
    def _():
        _cast_rows(w_ref, wbf_ref)

    acc = jnp.dot(_normed_bf16(x_ref, g_ref), wbf_ref[...], preferred_element_type=F32)
    w1, w2, w3 = WIDTH, 2 * WIDTH, 3 * WIDTH
    lb = lb_ref[...]
    o_ref[:, :w1] = _silu(acc[:, :w1])
    o_ref[:, w1:w2] = lb + (1.0 - lb) * _sigmoid(acc[:, w1:w2])
    o_ref[:, w2:w3] = acc[:, w2:w3]
    o_ref[:, w3:] = _silu(acc[:, w3:])


def _zero_rows(ref):
    slab = min(CAST_SLAB, ref.shape[0])
    assert ref.shape[0] % slab == 0

    def body(k, carry):
        ref[pl.ds(pl.multiple_of(k * slab, slab), slab), :] = jnp.zeros((slab, ref.shape[1]), ref.dtype)
        return carry

    lax.fori_loop(0, ref.shape[0] // slab, body, 0)


def _norm_proj_conv_kernel(x_ref, g_ref, w_ref, ws_ref, cw_ref, hist_ref, o_ref, tail_ref,
                           wbf_ref, acc_ref, carry_ref, *, steps_per_seq):
    i = pl.program_id(0)

    @pl.when(i == 0)
    def _():
        _cast_rows(w_ref, wbf_ref)
        _cast_rows(ws_ref, wbf_ref, 4 * WIDTH)
        _zero_rows(acc_ref)
        carry_ref[...] = jnp.zeros_like(carry_ref)

    xn = _normed_bf16(x_ref, g_ref)
    w3, w4 = 3 * WIDTH, 4 * WIDTH
    rows = x_ref.shape[0]
    first_of_seq = (i - 1) % steps_per_seq == 0
    cw = cw_ref[...]
    r8 = lax.broadcasted_iota(jnp.int32, (SUBLANES, WIDTH), 0)
    for c0 in range(0, w3, WIDTH):
        cols = slice(c0, c0 + WIDTH)
        raw = acc_ref[:, cols]
        hist = jnp.where(first_of_seq, hist_ref[:, cols], carry_ref[:, cols])
        tap = lambda d: cw[CONV_W - 1 - d:CONV_W - d, cols]
        body = raw * tap(0)
        head = raw[:SUBLANES]
        first = head * tap(0)
        for d in range(1, CONV_W):
            body = body + _roll_rows(raw, d) * tap(d)
            first = first + jnp.where(r8 >= d, _roll_rows(head, d), _roll_rows(hist, d)) * tap(d)
        o_ref[:SUBLANES, cols] = _silu(first)
        o_ref[SUBLANES:, cols] = _silu(body[SUBLANES:])
        carry_ref[:, cols] = raw[rows - SUBLANES:]
        acc_ref[:, cols] = jnp.dot(xn, wbf_ref[:, cols], preferred_element_type=F32)
    rest = acc_ref[:, w3:]
    o_ref[:, w3:w4] = _silu(rest[:, :WIDTH])
    o_ref[:, w4:] = rest[:, WIDTH:]
    acc_ref[:, w3:] = jnp.dot(xn, wbf_ref[:, w3:], preferred_element_type=F32)
    tail_ref[...] = carry_ref[...]


def _proj_call(kernel_fn, name, x2d, tm, n, operands, specs, extra_out=(), extra_scratch=(), delayed=False):
    m, k = x2d.shape
    tm = min(tm, m)
    tiles = m // tm
    x_map = (lambda i: (jnp.minimum(i, tiles - 1), 0)) if delayed else (lambda i: (i, 0))
    o_map = (lambda i: (jnp.maximum(i - 1, 0), 0)) if delayed else (lambda i: (i, 0))
    out_specs = [pl.BlockSpec((tm, n), o_map)] + [s for s, _ in extra_out]
    out_shape = [jax.ShapeDtypeStruct((m, n), F32)] + [s for _, s in extra_out]
    return pl.pallas_call(
        kernel_fn,
        grid=(tiles + int(delayed),),
        in_specs=[pl.BlockSpec((tm, k), x_map)] + specs,
        out_specs=out_specs,
        out_shape=out_shape,
        scratch_shapes=[pltpu.VMEM((k, n), BF16)] + list(extra_scratch),
        compiler_params=pltpu.CompilerParams(
            dimension_semantics=("arbitrary",), vmem_limit_bytes=VMEM_LIMIT),
        name=name,
    )(x2d, *operands)


def _weight_spec(k, n, col0):
    assert col0 % n == 0
    return pl.BlockSpec((None, k, n), lambda i: (0, 0, col0 // n), pipeline_mode=pl.Buffered(1))


def _norm_proj_hgrn(x2d, g, lb, w, tm):
    k, n = w.shape[1], 4 * WIDTH
    const = lambda i: (0, 0)
    specs = [pl.BlockSpec((1, k), const), pl.BlockSpec((1, WIDTH), const), _weight_spec(k, n, W_HGRN)]
    return _proj_call(_norm_proj_hgrn_kernel, "norm_proj_hgrn", x2d, tm, n, (g, lb, w), specs)[0]


def _norm_proj_conv(x2d, g, w, conv_w, hist, tm, rows_per_seq):
    k, n = w.shape[1], 4 * WIDTH + LANES
    const = lambda i: (0, 0)
    tm = min(tm, rows_per_seq)
    assert rows_per_seq % tm == 0 and x2d.shape[0] % rows_per_seq == 0
    tail_shape = (SUBLANES, 3 * WIDTH)
    specs = [pl.BlockSpec((1, k), const), _weight_spec(k, 4 * WIDTH, W_GDN), _weight_spec(k, LANES, W_SMALL),
             pl.BlockSpec((CONV_W, 3 * WIDTH), const), pl.BlockSpec(tail_shape, const)]
    return _proj_call(functools.partial(_norm_proj_conv_kernel, steps_per_seq=rows_per_seq // tm),
                      "norm_proj_gdn", x2d, tm, n, (g, w, w, conv_w, hist), specs,
                      extra_out=[(pl.BlockSpec(tail_shape, const), jax.ShapeDtypeStruct(tail_shape, F32))],
                      extra_scratch=[pltpu.VMEM((tm, n), F32), pltpu.VMEM(tail_shape, F32)], delayed=True)


HGRN_LEVELS = 6
HGRN_DIAG = HGRN_LEVELS


def _hgrn_level_table():
    r = np.arange(STACK)
    t, hd = r % CHUNK, r // CHUNK
    x = t[:, None] ^ t[None, :]
    lv = np.floor(np.log2(np.maximum(x, 1))).astype(np.int32)
    valid = (hd[:, None] == hd[None, :]) & (t[:, None] > t[None, :])
    lv = np.where(valid, lv, -1)
    lv = np.where(r[:, None] == r[None, :], HGRN_DIAG, lv)
    return jnp.asarray(lv, dtype=jnp.int32)


def _hgrn_kernel(lv_ref, g_ref, st0_ref, q_ref, f_ref, v_ref, gate_ref, o_ref, stout_ref, st_ref, *, nb, nc):
    @pl.when(pl.program_id(0) == 0)
    def _():
        st_ref[...] = st0_ref[...]

    bsz = nc * nb
    q = _load_chunks(q_ref, nc)
    f = _load_chunks(f_ref, nc)
    v = _load_chunks(v_ref, nc)
    gate = _load_chunks(gate_ref, nc)

    per_seq = lambda a: a.reshape(bsz, CHUNK, WIDTH)
    trow = lax.broadcasted_iota(jnp.int32, (1, CHUNK, WIDTH), 1)
    b = jnp.log(f)
    s = 1
    while s < CHUNK:
        b = (per_seq(b) + jnp.where(trow >= s, per_seq(_roll_rows(b, s)), 0.0)).reshape(b.shape)
        s *= 2

    qs, ks, vs, bs = (_stack_heads(a, bsz) for a in (q, 1.0 - f, v, b))
    lv = lv_ref[...][None]
    to3 = functools.partial(_to_stacks, bsz=bsz)
    to_g = functools.partial(_to_heads, bsz=bsz)
    roll_g = lambda a, shift: to_g(_roll_rows(a.reshape(qs.shape), shift))
    t = lax.broadcasted_iota(jnp.int32, (1, CHUNK, HEAD_DIM), 1)
    qs_g, ks_g, bs_g = to_g(qs), to_g(ks), to_g(bs)

    attn = jnp.where(lv == HGRN_DIAG, _bdot_nt(to3(qs), to3(ks)), 0.0)
    bref = bs_g
    bnext = roll_g(bs_g, -1)
    for li in range(HGRN_LEVELS):
        m = 1 << li
        if li > 0:
            half = m // 2
            upper = (t & (m - 1)) >= half
            bref = jnp.where(upper, roll_g(bref, half), bref)
            bnext = jnp.where(upper, bnext, roll_g(bnext, -half))
        odd = ((t >> li) & 1) == 1
        both = jnp.where(odd, qs_g, ks_g) * jnp.exp(jnp.where(odd, bs_g - bref, bnext - bs_g))
        both = to3(both.reshape(qs.shape)).astype(BF16)
        attn = jnp.where(lv == li, _bdot_nt(both, both), attn)

    o = _bdot(attn, to3(vs))

    n_g = nb * HEADS
    st = st_ref[...]
    qe = qs_g * jnp.exp(bs_g)
    b_end = bs_g[:, CHUNK - 1:CHUNK, :]
    ke = ks_g * jnp.exp(b_end - bs_g)
    vs_g = to_g(vs)
    o_inter = []
    for j in range(nc):
        sl = slice(j * n_g, (j + 1) * n_g)
        o_inter.append(_bdot_nt(qe[sl], st))
        st = st * jnp.exp(b_end[sl]) + _bdot_tn(vs_g[sl], ke[sl])
    st_ref[...] = st
    stout_ref[...] = st
    o = (to_g(o) + jnp.concatenate(o_inter, axis=0)).reshape(qs.shape)

    o = o * lax.rsqrt(jnp.mean(o * o, axis=-1, keepdims=True) + EPS) * g_ref[...]
    o = o * _stack_heads(gate, bsz)
    _store_chunks(o_ref, _unstack_heads(o, bsz), nb, nc)


def _gdn_cumsum_matrix():
    r = np.arange(STACK)
    u = (r[:, None] // CHUNK == r[None, :] // CHUNK) & (r[:, None] <= r[None, :])
    return jnp.asarray(u, dtype=BF16)


def _gdn_kernel(u_ref, alog_ref, dt_ref, g_ref, s_ref, st0_ref, q_ref, k_ref, v_ref, z_ref,
                o_ref, stout_ref, st_ref, *, nb, nc):
    @pl.when(pl.program_id(0) == 0)
    def _():
        st_ref[...] = st0_ref[...]

    bsz = nc * nb
    qs = _stack_heads(_load_chunks(q_ref, nc), bsz)
    ks = _stack_heads(_load_chunks(k_ref, nc), bsz)
    vs = _stack_heads(_load_chunks(v_ref, nc), bsz)
    gate = _load_chunks(z_ref, nc)
    qs = qs * lax.rsqrt(jnp.sum(qs * qs, axis=-1, keepdims=True) + EPS) * (HEAD_DIM ** -0.5)
    ks = ks * lax.rsqrt(jnp.sum(ks * ks, axis=-1, keepdims=True) + EPS)
    to3 = functools.partial(_to_stacks, bsz=bsz)
    qs, ks, vs = to3(qs), to3(ks), to3(vs)
    n_st = bsz * N_STACKS

    srow = _chunks_first_scalars(s_ref[...], nc).reshape(n_st, 2, STACK)
    per_stack = lambda ref: jnp.broadcast_to(ref[...][None], (bsz, N_STACKS, 1, STACK)).reshape(n_st, 1, STACK)
    beta_row = _sigmoid(srow[:, 0:1, :])
    g_row = -jnp.exp(per_stack(alog_ref)) * _softplus(srow[:, 1:2, :] + per_stack(dt_ref))
    n8 = n_st * SUBLANES
    g8 = jnp.broadcast_to(g_row, (n_st, SUBLANES, STACK)).reshape(n8, STACK)
    pieces = jnp.concatenate(_split3(g8), axis=0)
    cum = jnp.dot(pieces, u_ref[...], preferred_element_type=F32)
    cum = cum[0:n8] + cum[n8:2 * n8] + cum[2 * n8:]
    cum_row = cum.reshape(n_st, SUBLANES, STACK)[:, 0:1, :]

    r = lax.broadcasted_iota(jnp.int32, (STACK, STACK), 0)
    cc = lax.broadcasted_iota(jnp.int32, (STACK, STACK), 1)
    eye = (r == cc)[None]
    cum_col = jnp.sum(jnp.where(eye, cum_row, 0.0), axis=2, keepdims=True)
    beta_col = jnp.sum(jnp.where(eye, beta_row, 0.0), axis=2, keepdims=True)
    same = (r >> CHUNK_BITS) == (cc >> CHUNK_BITS)
    lower = (same & (r >= cc))[None]
    strict = (same & (r > cc))[None]
    ratio = jnp.exp(jnp.minimum(cum_col - cum_row, 0.0))

    kkt = _bdot_nt(ks, ks)
    qkt = _bdot_nt(qs, ks)
    lm = jnp.where(strict, beta_col * ratio * kkt, 0.0)

    ident = eye.astype(F32)
    l0 = jnp.where(((r >> NEUMANN_BITS) == (cc >> NEUMANN_BITS))[None], lm, 0.0)
    l2 = _bdot(l0, l0)
    l4 = _bdot(l2, l2)
    xinv = _bdot(_bdot(ident - l0, ident + l2), ident + l4)
    for bits in range(NEUMANN_BITS, CHUNK_BITS):
        off = (((r >> (bits + 1)) == (cc >> (bits + 1))) & ((r >> bits) != (cc >> bits)))[None]
        xinv = xinv - _bdot(_bdot(xinv, jnp.where(off, lm, 0.0)), xinv)

    decay_col = jnp.exp(cum_col)
    rhs = jnp.concatenate([beta_col * vs, (beta_col * decay_col) * ks], axis=2)
    sol = _bdot(xinv, rhs)

    to_g = functools.partial(_to_heads, bsz=bsz)
    n_g = nb * HEADS
    st = st_ref[...]
    u0_g, w_g, q_g = to_g(sol[:, :, :HEAD_DIM]), to_g(sol[:, :, HEAD_DIM:]), to_g(qs)
    cum_g = to_g(cum_col)
    cum_end = cum_g[:, CHUNK - 1:CHUNK, :]
    kdec = to_g(ks) * jnp.exp(cum_end - cum_g)
    decay_g = to_g(decay_col)
    us, o_inter = [], []
    for j in range(nc):
        sl = slice(j * n_g, (j + 1) * n_g)
        u_j = u0_g[sl] - _bdot(w_g[sl], st)
        us.append(u_j)
        o_inter.append(decay_g[sl] * _bdot(q_g[sl], st))
        st = jnp.exp(cum_end[sl]) * st + _bdot_tn(kdec[sl], u_j)
    st_ref[...] = st
    stout_ref[...] = st
    attn = jnp.where(lower, qkt * ratio, 0.0)
    u = jnp.concatenate(us, axis=0).reshape(bsz * HEADS * CHUNK, HEAD_DIM)
    o = jnp.concatenate(o_inter, axis=0) + to_g(_bdot(attn, to3(u)))
    o = o.reshape(bsz * HEADS * CHUNK, HEAD_DIM)

    o = o * lax.rsqrt(jnp.mean(o * o, axis=-1, keepdims=True) + EPS) * g_ref[...]
    o = o * _stack_heads(gate, bsz)
    _store_chunks(o_ref, _unstack_heads(o, bsz), nb, nc)


def _recurrence_call(kernel_fn, name, proj, state0, operands, specs, nc):
    nb, length, _ = proj.shape
    rows = nc * CHUNK
    state_shape = (nb * HEADS, HEAD_DIM, HEAD_DIM)
    state_spec = pl.BlockSpec(state_shape, lambda c: (0, 0, 0))
    chunk_specs = [pl.BlockSpec((nb, rows, WIDTH), functools.partial(lambda c, j: (0, c, j), j=j))
                   for j in range(4)]
    return pl.pallas_call(
        functools.partial(kernel_fn, nb=nb, nc=nc),
        grid=(length // rows,),
        in_specs=specs + [state_spec] + chunk_specs,
        out_specs=[pl.BlockSpec((nb, rows, WIDTH), lambda c: (0, c, 0)), state_spec],
        out_shape=[jax.ShapeDtypeStruct((nb, length, WIDTH), BF16),
                   jax.ShapeDtypeStruct(state_shape, F32)],
        scratch_shapes=[pltpu.VMEM(state_shape, F32)],
        compiler_params=pltpu.CompilerParams(
            dimension_semantics=("arbitrary",), vmem_limit_bytes=VMEM_LIMIT),
        name=name,
    )(*operands, state0, *([proj] * 4))


def _hgrn(proj, state0, norm_g, nc):
    const = lambda c: (0, 0)
    specs = [pl.BlockSpec((STACK, STACK), const), pl.BlockSpec((1, HEAD_DIM), const)]
    return _recurrence_call(_hgrn_kernel, "hgrn", proj, state0, (_hgrn_level_table(), norm_g), specs, nc)


def _gdn(proj, state0, scal, a_log_row, dt_row, norm_g, nc):
    nb = proj.shape[0]
    const = lambda c: (0, 0)
    const3 = lambda c: (0, 0, 0)
    specs = [pl.BlockSpec((STACK, STACK), const),
             pl.BlockSpec((N_STACKS, 1, STACK), const3),
             pl.BlockSpec((N_STACKS, 1, STACK), const3),
             pl.BlockSpec((1, HEAD_DIM), const),
             pl.BlockSpec((nb, nc, N_STACKS, 2, STACK), lambda c: (0, c, 0, 0, 0))]
    return _recurrence_call(_gdn_kernel, "gdn", proj, state0,
                            (_gdn_cumsum_matrix(), a_log_row, dt_row, norm_g, scal), specs, nc)


ROUTE_LANE0 = N_GROUPS
ROUTE_NEG = -1e30


def _route_tile(lg, ltri, carry):
    lane = lax.broadcasted_iota(jnp.int32, lg.shape, 1).astype(F32)
    first = lambda mask: jnp.min(jnp.where(mask, lane, float(LANES)), axis=1, keepdims=True)
    top = lambda mask: jnp.max(jnp.where(mask, lg, ROUTE_NEG), axis=1, keepdims=True)

    is_g = lane < N_GROUPS
    gmax = top(is_g)
    grp = first(is_g & (lg == gmax))
    p_grp = 1.0 / jnp.sum(jnp.where(is_g, jnp.exp(lg - gmax), 0.0), axis=1, keepdims=True)

    lo = ROUTE_LANE0 + grp * EXPERTS_PER_GROUP
    in_grp = (lane >= lo) & (lane < lo + EXPERTS_PER_GROUP)
    m1 = top(in_grp)
    i1 = first(in_grp & (lg == m1))
    rest = in_grp & (lane != i1)
    m2 = top(rest)
    i2 = first(rest & (lg == m2))
    r = jnp.exp(m2 - m1)
    w0 = p_grp / (1.0 + r)
    w1 = p_grp * r / (1.0 + r)

    hot0 = (lane == i1).astype(F32)
    hot1 = (lane == i2).astype(F32)
    both = hot0 + hot1
    before = jnp.dot(ltri, both.astype(BF16), preferred_element_type=F32) + carry
    rank0 = jnp.sum(hot0 * before, axis=1, keepdims=True)
    rank1 = jnp.sum(hot1 * before, axis=1, keepdims=True)
    cols = (w0, w1, i1 - ROUTE_LANE0, i2 - ROUTE_LANE0, rank0, rank1)
    info = jnp.zeros_like(lg)
    for k, col in enumerate(cols):
        info = jnp.where(lane == k, col, info)
    return info, carry + jnp.sum(both, axis=0, keepdims=True)


def _merge_kernel(oa_ref, ob_ref, x_ref, xprev_ref, mg_ref, wgate_ref, hgup_ref, gdup_ref, wout_ref,
                  ng_ref, rw_ref, rb_ref, ltri_ref, h2_ref, xn_ref, info_ref, ids_ref, cnt_ref,
                  wgate_bf, gates_ref, ua_ref, ub_ref, carry_ref):
    i = pl.program_id(0)

    @pl.when(i == 0)
    def _():
        carry_ref[...] = jnp.zeros_like(carry_ref)
        _cast_rows(wgate_ref, wgate_bf)
        _zero_rows(gates_ref)
        _zero_rows(ua_ref)
        _zero_rows(ub_ref)

    gates = _sigmoid(gates_ref[...])
    merged = (gates[:, :D_MODEL] * ua_ref[...] + gates[:, D_MODEL:] * ub_ref[...]).astype(BF16)
    gates_ref[...] = jnp.dot(_normed_bf16(x_ref, mg_ref), wgate_bf[...], preferred_element_type=F32)
    h2 = xprev_ref[...] + jnp.dot(merged, wout_ref[...], preferred_element_type=F32)
    h2_ref[...] = h2
    xn = h2 * lax.rsqrt(jnp.mean(h2 * h2, axis=-1, keepdims=True) + EPS) * ng_ref[...]
    xn_ref[...] = _pack_halves(xn)
    lg = jnp.dot(xn.astype(BF16), rw_ref[...], preferred_element_type=F32) + rb_ref[...]
    ua_ref[...] = jnp.dot(oa_ref[...], hgup_ref[...], preferred_element_type=F32)
    ub_ref[...] = jnp.dot(ob_ref[...], gdup_ref[...], preferred_element_type=F32)
    old = carry_ref[...]
    info, carry = _route_tile(lg, ltri_ref[...], old)
    info_ref[...] = info
    ids_ref[...] = info.T[:SUBLANES].astype(jnp.int32)
    carry = jnp.where(i > 0, carry, old)
    carry_ref[...] = carry
    cnt_ref[...] = carry


def _merge(o_a, o_b, x2d, mix_g, w_gates, hg_up, gd_up, w_out, norm_g, rw, rb):
    t = x2d.shape[0]
    tm = TOKEN_TM
    tiles = t // tm
    new = lambda i: (jnp.minimum(i, tiles - 1), 0)
    row = lambda i: (jnp.maximum(i - 1, 0), 0)
    const = lambda i: (0, 0)
    ltri = jnp.asarray(np.tril(np.ones((tm, tm), np.float32), -1), dtype=BF16)
    return pl.pallas_call(
        _merge_kernel,
        grid=(tiles + 1,),
        in_specs=[pl.BlockSpec((tm, WIDTH), new),
                  pl.BlockSpec((tm, WIDTH), new),
                  pl.BlockSpec((tm, D_MODEL), new),
                  pl.BlockSpec((tm, D_MODEL), row),
                  pl.BlockSpec((1, D_MODEL), const),
                  pl.BlockSpec((D_MODEL, 2 * D_MODEL), const, pipeline_mode=pl.Buffered(1)),
                  pl.BlockSpec((WIDTH, D_MODEL), const),
                  pl.BlockSpec((WIDTH, D_MODEL), const),
                  pl.BlockSpec((D_MODEL, D_MODEL), const),
                  pl.BlockSpec((1, D_MODEL), const),
                  pl.BlockSpec((D_MODEL, LANES), const),
                  pl.BlockSpec((1, LANES), const),
                  pl.BlockSpec((tm, tm), const)],
        out_specs=[pl.BlockSpec((tm, D_MODEL), row),
                   pl.BlockSpec((tm, HALF), row),
                   pl.BlockSpec((tm, LANES), row),
                   pl.BlockSpec((SUBLANES, tm), lambda i: (0, jnp.maximum(i - 1, 0))),
                   pl.BlockSpec((1, LANES), const)],
        out_shape=[jax.ShapeDtypeStruct((t, D_MODEL), F32),
                   jax.ShapeDtypeStruct((t, HALF), U32),
                   jax.ShapeDtypeStruct((t, LANES), F32),
                   jax.ShapeDtypeStruct((SUBLANES, t), jnp.int32),
                   jax.ShapeDtypeStruct((1, LANES), F32)],
        scratch_shapes=[pltpu.VMEM((D_MODEL, 2 * D_MODEL), BF16), pltpu.VMEM((tm, 2 * D_MODEL), F32),
                        pltpu.VMEM((tm, D_MODEL), F32), pltpu.VMEM((tm, D_MODEL), F32),
                        pltpu.VMEM((1, LANES), F32)],
        compiler_params=pltpu.CompilerParams(
            dimension_semantics=("arbitrary",), vmem_limit_bytes=VMEM_LIMIT),
        name="merge",
    )(o_a, o_b, x2d, x2d, mix_g, w_gates, hg_up, gd_up, w_out, norm_g, rw, rb, ltri)


def _sc_gather(table, idx):
    n_idx = idx.shape[0]
    cols = table.shape[1]
    per_worker = n_idx // SC_WORKERS
    assert n_idx % (SC_WORKERS * SC_CHUNK) == 0
    mesh = plsc.VectorSubcoreMesh(core_axis_name="c", subcore_axis_name="s")

    @functools.partial(
        pl.kernel, mesh=mesh,
        out_type=jax.ShapeDtypeStruct((n_idx, cols), table.dtype),
        scratch_types=[pltpu.VMEM((SC_CHUNK,), jnp.int32),
                       pltpu.VMEM((SC_CHUNK, cols), table.dtype),
                       pltpu.SemaphoreType.DMA],
    )
    def gather(table_hbm, idx_hbm, out_hbm, idx_v, rows_v, sem):
        worker = lax.axis_index("s") * SC_CORES + lax.axis_index("c")
        base = worker * per_worker

        @pl.loop(0, per_worker // SC_CHUNK)
        def _(c):
            off = pl.multiple_of(base + c * SC_CHUNK, SC_CHUNK)
            pltpu.sync_copy(idx_hbm.at[pl.ds(off, SC_CHUNK)], idx_v)
            pltpu.async_copy(table_hbm.at[idx_v], rows_v, sem).wait()
            pltpu.sync_copy(rows_v, out_hbm.at[pl.ds(off, SC_CHUNK)])

    return gather(table, idx)


def _sc_dispatch(rows, dest, n_out):
    t, cols = rows.shape
    per_worker = t // SC_WORKERS
    assert t % (SC_WORKERS * SC_CHUNK) == 0 and dest.shape[0] == 2 * t
    mesh = plsc.VectorSubcoreMesh(core_axis_name="c", subcore_axis_name="s")

    @functools.partial(
        pl.kernel, mesh=mesh,
        out_type=jax.ShapeDtypeStruct((n_out, cols), rows.dtype),
        scratch_types=[pltpu.VMEM((SC_CHUNK,), jnp.int32),
                       pltpu.VMEM((SC_CHUNK, cols), rows.dtype)],
    )
    def dispatch(rows_hbm, dest_hbm, out_hbm, idx_v, rows_v):
        worker = lax.axis_index("s") * SC_CORES + lax.axis_index("c")
        base = worker * per_worker

        @pl.loop(0, per_worker // SC_CHUNK)
        def _(c):
            off = pl.multiple_of(base + c * SC_CHUNK, SC_CHUNK)
            pltpu.sync_copy(rows_hbm.at[pl.ds(off, SC_CHUNK)], rows_v)
            for slot in range(2):
                pltpu.sync_copy(dest_hbm.at[pl.ds(slot * t + off, SC_CHUNK)], idx_v)
                pltpu.sync_copy(rows_v, out_hbm.at[idx_v])

    return dispatch(rows, dest)


def _moe_kernel(be_ref, nu_ref, x_ref, wg_ref, wu_ref, wd_ref, y_ref, wg_bf, wu_bf, wd_bf):
    i = pl.program_id(0)

    @pl.when((i == 0) | (be_ref[i] != be_ref[jnp.maximum(i - 1, 0)]))
    def _():
        _cast_rows(wg_ref, wg_bf)
        _cast_rows(wu_ref, wu_bf)
        _cast_rows(wd_ref, wd_bf)

    @pl.when(i < nu_ref[0])
    def _():
        xb = _unpack_halves(x_ref[...]).astype(BF16)
        a = jnp.dot(xb, wg_bf[...], preferred_element_type=F32)
        u = jnp.dot(xb, wu_bf[...], preferred_element_type=F32)
        y = jnp.dot((_silu(a) * u).astype(BF16), wd_bf[...], preferred_element_type=F32)
        y_ref[...] = _pack_halves(y)

    @pl.when(i >= nu_ref[0])
    def _():
        y_ref[...] = jnp.zeros_like(y_ref)


def _moe(blk_expert, n_used, x_sorted, w_gate, w_up, w_down):
    n_blocks = blk_expert.shape[0]
    wspec = lambda shape: pl.BlockSpec((None,) + shape, lambda i, be, nu: (be[i], 0, 0))
    rows = pl.BlockSpec((MOE_ROWS, HALF), lambda i, be, nu: (i, 0))
    grid_spec = pltpu.PrefetchScalarGridSpec(
        num_scalar_prefetch=2,
        grid=(n_blocks,),
        in_specs=[rows, wspec((D_MODEL, D_FF)), wspec((D_MODEL, D_FF)), wspec((D_FF, D_MODEL))],
        out_specs=rows,
        scratch_shapes=[pltpu.VMEM((D_MODEL, D_FF), BF16), pltpu.VMEM((D_MODEL, D_FF), BF16),
                        pltpu.VMEM((D_FF, D_MODEL), BF16)],
    )
    return pl.pallas_call(
        _moe_kernel,
        grid_spec=grid_spec,
        out_shape=jax.ShapeDtypeStruct(x_sorted.shape, U32),
        compiler_params=pltpu.CompilerParams(
            dimension_semantics=("arbitrary",), vmem_limit_bytes=VMEM_LIMIT),
        name="moe",
    )(blk_expert, n_used, x_sorted, w_gate, w_up, w_down)


def _combine_kernel(h2_ref, rw_ref, g_ref, y0_ref, y1_ref, o_ref):
    rw = rw_ref[...]
    h = h2_ref[...] + rw[:, 0:1] * _unpack_halves(y0_ref[...]) + rw[:, 1:2] * _unpack_halves(y1_ref[...])
    o_ref[...] = h * lax.rsqrt(jnp.mean(h * h, axis=-1, keepdims=True) + EPS) * g_ref[...]


def _combine(h2, rweights, final_g, y):
    t = h2.shape[0]
    tm = min(COMBINE_TM, t)
    row = lambda i: (i, 0)
    return pl.pallas_call(
        _combine_kernel,
        grid=(t // tm,),
        in_specs=[pl.BlockSpec((tm, D_MODEL), row),
                  pl.BlockSpec((tm, LANES), row),
                  pl.BlockSpec((1, D_MODEL), lambda i: (0, 0)),
                  pl.BlockSpec((tm, HALF), row),
                  pl.BlockSpec((tm, HALF), lambda i: (i + t // tm, 0))],
        out_specs=pl.BlockSpec((tm, D_MODEL), row),
        out_shape=jax.ShapeDtypeStruct((t, D_MODEL), F32),
        compiler_params=pltpu.CompilerParams(
            dimension_semantics=("arbitrary",), vmem_limit_bytes=VMEM_LIMIT),
        name="combine",
    )(h2, rweights, final_g, y, y)


def _block_layout(ids, counts_row, t):
    n_blocks = 2 * t // MOE_ROWS + N_EXPERTS
    counts = counts_row[0, ROUTE_LANE0:ROUTE_LANE0 + N_EXPERTS].astype(jnp.int32)
    padded = ((counts + MOE_ROWS - 1) // MOE_ROWS) * MOE_ROWS
    pend = jnp.cumsum(padded)
    pstart = pend - padded
    blk_first = jnp.arange(n_blocks, dtype=jnp.int32) * MOE_ROWS
    blk_expert = jnp.minimum(jnp.sum((pend[None, :] <= blk_first[:, None]).astype(jnp.int32), axis=1),
                             N_EXPERTS - 1).astype(jnp.int32)
    n_used = (pend[-1] // MOE_ROWS).astype(jnp.int32).reshape(1)
    expert, rank = ids[2:4], ids[4:6]
    experts = jnp.arange(N_EXPERTS, dtype=jnp.int32)
    first_row = jnp.sum(jnp.where(expert[:, :, None] == experts, pstart, 0), axis=-1)
    dest = (first_row + rank).reshape(2 * t)
    return dest, blk_expert, n_used, n_blocks * MOE_ROWS


def kernel(x, meta_tokens, hg_lb_logits, norm_mix_g, w_in, gd_conv_w, gd_A_log, gd_dt_bias, hg_norm_g, gd_norm_g, hg_up, gd_up, w_out, norm_ffn_g, router_group_w, router_group_b, router_expert_w, router_expert_b, w_gate, w_up, w_down, final_norm_g):
    bsz, seq, d = x.shape
    t = bsz * seq
    x2d = x.reshape(t, d)

    lb = jnp.cumsum(jax.nn.softmax(hg_lb_logits.astype(F32), axis=0), axis=0)[0].reshape(1, WIDTH)
    w_all = w_in.astype(F32)
    w_gates = w_all[0, :, W_SMALL + 2 * HEADS:]
    g_mix = norm_mix_g[0].reshape(1, d)
    meta_blk = jnp.concatenate([jnp.zeros((CHUNK - N_META, d), F32), meta_tokens.astype(F32)], axis=0)

    conv_w = gd_conv_w[0].astype(F32)
    no_history = jnp.zeros((SUBLANES, 3 * WIDTH), F32)
    proj, proj_meta = {}, {}
    proj["hgrn"] = _norm_proj_hgrn(x2d, g_mix, lb, w_all, PROJ_TM)
    proj_meta["hgrn"] = _norm_proj_hgrn(meta_blk, g_mix, lb, w_all, CHUNK)
    proj_meta["gdn"], meta_tail = _norm_proj_conv(meta_blk, g_mix, w_all, conv_w, no_history, CHUNK, CHUNK)
    proj["gdn"], _ = _norm_proj_conv(x2d, g_mix, w_all, conv_w, meta_tail, PROJ_TM, seq)

    def scalar_rows(p, nb, nc):
        s = p[:, 4 * WIDTH:4 * WIDTH + 2 * HEADS].reshape(nb, nc, CHUNK, 2, N_STACKS, HEADS_PER_STACK)
        return s.transpose(0, 1, 4, 3, 5, 2).reshape(nb, nc, N_STACKS, 2, STACK)

    a_log_row = jnp.repeat(gd_A_log[0].astype(F32), CHUNK).reshape(N_STACKS, 1, STACK)
    dt_row = jnp.repeat(gd_dt_bias[0].astype(F32), CHUNK).reshape(N_STACKS, 1, STACK)
    hg_g, gd_g = hg_norm_g[0].reshape(1, HEAD_DIM), gd_norm_g[0].reshape(1, HEAD_DIM)

    zero_state = jnp.zeros((HEADS, HEAD_DIM, HEAD_DIM), F32)
    per_seq = lambda st: jnp.tile(st, (bsz, 1, 1))
    _, hg_state = _hgrn(proj_meta["hgrn"][None], zero_state, hg_g, 1)
    o_a, _ = _hgrn(proj["hgrn"].reshape(bsz, seq, -1), per_seq(hg_state), hg_g, HGRN_CHUNKS)
    _, gd_state = _gdn(proj_meta["gdn"][None], zero_state, scalar_rows(proj_meta["gdn"], 1, 1),
                       a_log_row, dt_row, gd_g, 1)
    o_b, _ = _gdn(proj["gdn"].reshape(bsz, seq, -1), per_seq(gd_state),
                  scalar_rows(proj["gdn"], bsz, seq // CHUNK), a_log_row, dt_row, gd_g, GDN_CHUNKS)

    rw = jnp.zeros((d, LANES), F32)
    rw = rw.at[:, :N_GROUPS].set(router_group_w[0]).at[:, N_GROUPS:N_GROUPS + N_EXPERTS].set(router_expert_w[0])
    rb = jnp.zeros((1, LANES), F32)
    rb = rb.at[0, :N_GROUPS].set(router_group_b[0]).at[0, N_GROUPS:N_GROUPS + N_EXPERTS].set(router_expert_b[0])
    h2, xn, info, ids, counts = _merge(o_a.reshape(t, WIDTH), o_b.reshape(t, WIDTH), x2d, g_mix, w_gates,
                                  hg_up[0].astype(BF16), gd_up[0].astype(BF16), w_out[0].astype(BF16),
                                  norm_ffn_g[0].reshape(1, d), rw.astype(BF16), rb)

    dest, blk_expert, n_used, n_rows = _block_layout(ids, counts, t)
    x_sorted = _sc_dispatch(xn, dest, n_rows)
    y_sorted = _moe(blk_expert, n_used, x_sorted,
                    w_gate[0].astype(F32), w_up[0].astype(F32), w_down[0].astype(F32))
    y_tok = _sc_gather(y_sorted, dest)
    out = _combine(h2, info, final_norm_g.reshape(1, d), y_tok)
    return out.reshape(bsz, seq, d)
```

```python
import functools

import numpy as np
import jax
import jax.numpy as jnp
from jax import lax
from jax.experimental import pallas as pl
from jax.experimental.pallas import tpu as pltpu
from jax.experimental.pallas import tpu_sc as plsc

F32 = jnp.float32
BF16 = jnp.bfloat16
U32 = jnp.uint32

D_MODEL = 1024
N_META = 16
CHUNK_BITS = 6
CHUNK = 1 << CHUNK_BITS
EPS = 1e-6
HEADS = 4
HEAD_DIM = 128
WIDTH = HEADS * HEAD_DIM
HEADS_PER_STACK = 2
STACK = HEADS_PER_STACK * CHUNK
N_STACKS = HEADS // HEADS_PER_STACK
CONV_W = 4
NEUMANN_BITS = 3
N_GROUPS = 4
EXPERTS_PER_GROUP = 8
N_EXPERTS = N_GROUPS * EXPERTS_PER_GROUP
D_FF = 512
LANES = 128
SUBLANES = 8
HALF = D_MODEL // 2
MOE_ROWS = 512
CAST_SLAB = 128
SC_CORES, SC_SUBCORES = 2, 16
SC_WORKERS = SC_CORES * SC_SUBCORES
SC_CHUNK = 128
PROJ_TM = 1024
W_HGRN, W_GDN, W_SMALL = 0, 4 * WIDTH, 8 * WIDTH
HGRN_CHUNKS, GDN_CHUNKS = 4, 2
TOKEN_TM = 512
COMBINE_TM = 1024
VMEM_LIMIT = 56 * 1024 * 1024


def _bdot(a, b):
    return lax.dot_general(a.astype(BF16), b.astype(BF16), (((2,), (1,)), ((0,), (0,))),
                           preferred_element_type=F32)


def _bdot_nt(a, b):
    return lax.dot_general(a.astype(BF16), b.astype(BF16), (((2,), (2,)), ((0,), (0,))),
                           preferred_element_type=F32)


def _bdot_tn(a, b):
    return lax.dot_general(a.astype(BF16), b.astype(BF16), (((1,), (1,)), ((0,), (0,))),
                           preferred_element_type=F32)


def _split3(x):
    hi = x.astype(BF16)
    r1 = x - hi.astype(F32)
    mid = r1.astype(BF16)
    lo = (r1 - mid.astype(F32)).astype(BF16)
    return hi, mid, lo


def _sigmoid(x):
    return 1.0 / (1.0 + jnp.exp(-x))


def _silu(x):
    return x * _sigmoid(x)


def _softplus(x):
    return jnp.maximum(x, 0.0) + jnp.log(1.0 + jnp.exp(-jnp.abs(x)))


def _stack_heads(x, bsz):
    x3 = x.reshape(bsz, CHUNK, WIDTH)
    y = jnp.concatenate([x3[:, :, h * HEAD_DIM:(h + 1) * HEAD_DIM] for h in range(HEADS)], axis=1)
    return y.reshape(bsz * HEADS * CHUNK, HEAD_DIM)


def _unstack_heads(x, bsz):
    x3 = x.reshape(bsz, HEADS * CHUNK, HEAD_DIM)
    return jnp.concatenate([x3[:, h * CHUNK:(h + 1) * CHUNK, :] for h in range(HEADS)], axis=2)


def _to_stacks(a, bsz):
    return a.reshape(bsz * N_STACKS, STACK, a.shape[-1])


def _to_heads(a, bsz):
    return a.reshape(bsz * HEADS, CHUNK, a.shape[-1])


def _roll_rows(x, shift):
    n = x.shape[0]
    return pltpu.roll(x, shift % n, axis=0)


def _chunks_first(x, nc):
    return jnp.concatenate([x[:, j * CHUNK:(j + 1) * CHUNK] for j in range(nc)], axis=0)


def _chunks_first_scalars(s, nc):
    return jnp.concatenate([s[:, j] for j in range(nc)], axis=0)


def _load_chunks(ref, nc):
    x = _chunks_first(ref[...], nc)
    return x.reshape(x.shape[0] * CHUNK, WIDTH)


def _store_chunks(ref, x, nb, nc):
    ref[...] = jnp.concatenate([x[j * nb:(j + 1) * nb] for j in range(nc)], axis=1).astype(ref.dtype)


def _cast_rows(src_ref, dst_ref, col0=0):
    slab = CAST_SLAB
    rows, cols = src_ref.shape
    assert rows % slab == 0

    def body(k, carry):
        r = pl.multiple_of(k * slab, slab)
        dst_ref[pl.ds(r, slab), col0:col0 + cols] = src_ref[pl.ds(r, slab), :].astype(dst_ref.dtype)
        return carry

    lax.fori_loop(0, rows // slab, body, 0)


def _pack_halves(x):
    lo = lax.bitcast_convert_type(x[:, :HALF].astype(BF16).astype(F32), U32)
    hi = lax.bitcast_convert_type(x[:, HALF:].astype(BF16).astype(F32), U32)
    return (lo >> 16) | (hi & jnp.uint32(0xFFFF0000))


def _unpack_halves(p):
    lo = lax.bitcast_convert_type(p << 16, F32)
    hi = lax.bitcast_convert_type(p & jnp.uint32(0xFFFF0000), F32)
    return jnp.concatenate([lo, hi], axis=1)


def _normed_bf16(x_ref, g_ref):
    x = x_ref[...]
    return (x * lax.rsqrt(jnp.mean(x * x, axis=-1, keepdims=True) + EPS) * g_ref[...]).astype(BF16)


def _norm_proj_hgrn_kernel(x_ref, g_ref, lb_ref, w_ref, o_ref, wbf_ref):
    @pl.when(pl.program_id(0) == 0)
    def _():
        _cast_rows(w_ref, wbf_ref)

    acc = jnp.dot(_normed_bf16(x_ref, g_ref), wbf_ref[...], preferred_element_type=F32)
    w1, w2, w3 = WIDTH, 2 * WIDTH, 3 * WIDTH
    lb = lb_ref[...]
    o_ref[:, :w1] = _silu(acc[:, :w1])
    o_ref[:, w1:w2] = lb + (1.0 - lb) * _sigmoid(acc[:, w1:w2])
    o_ref[:, w2:w3] = acc[:, w2:w3]
    o_ref[:, w3:] = _silu(acc[:, w3:])


def _zero_rows(ref):
    slab = min(CAST_SLAB, ref.shape[0])
    assert ref.shape[0] % slab == 0

    def body(k, carry):
        ref[pl.ds(pl.multiple_of(k * slab, slab), slab), :] = jnp.zeros((slab, ref.shape[1]), ref.dtype)
        return carry

    lax.fori_loop(0, ref.shape[0] // slab, body, 0)


def _norm_proj_conv_kernel(x_ref, g_ref, w_ref, ws_ref, cw_ref, hist_ref, o_ref, tail_ref, logit_ref,
                           wbf_ref, acc_ref, carry_ref, *, steps_per_seq):
    i = pl.program_id(0)

    @pl.when(i == 0)
    def _():
        _cast_rows(w_ref, wbf_ref)
        _cast_rows(ws_ref, wbf_ref, 4 * WIDTH)
        _zero_rows(acc_ref)
        carry_ref[...] = jnp.zeros_like(carry_ref)

    xn = _normed_bf16(x_ref, g_ref)
    w3, w4 = 3 * WIDTH, 4 * WIDTH
    rows = x_ref.shape[0]
    first_of_seq = (i - 1) % steps_per_seq == 0
    cw = cw_ref[...]
    r8 = lax.broadcasted_iota(jnp.int32, (SUBLANES, WIDTH), 0)
    for c0 in range(0, w3, WIDTH):
        cols = slice(c0, c0 + WIDTH)
        raw = acc_ref[:, cols]
        hist = jnp.where(first_of_seq, hist_ref[:, cols], carry_ref[:, cols])
        tap = lambda d: cw[CONV_W - 1 - d:CONV_W - d, cols]
        body = raw * tap(0)
        head = raw[:SUBLANES]
        first = head * tap(0)
        for d in range(1, CONV_W):
            body = body + _roll_rows(raw, d) * tap(d)
            first = first + jnp.where(r8 >= d, _roll_rows(head, d), _roll_rows(hist, d)) * tap(d)
        o_ref[:SUBLANES, cols] = _silu(first)
        o_ref[SUBLANES:, cols] = _silu(body[SUBLANES:])
        carry_ref[:, cols] = raw[rows - SUBLANES:]
        acc_ref[:, cols] = jnp.dot(xn, wbf_ref[:, cols], preferred_element_type=F32)
    rest = acc_ref[:, w3:]
    o_ref[:, w3:] = _silu(rest[:, :WIDTH])
    logit_ref[...] = rest[:, WIDTH:].T[:SUBLANES]
    acc_ref[:, w3:] = jnp.dot(xn, wbf_ref[:, w3:], preferred_element_type=F32)
    tail_ref[...] = carry_ref[...]


def _proj_call(kernel_fn, name, x2d, tm, n, operands, specs, extra_out=(), extra_scratch=(),
               weight_cols=None, delayed=False):
    m, k = x2d.shape
    weight_cols = n if weight_cols is None else weight_cols
    tm = min(tm, m)
    tiles = m // tm
    x_map = (lambda i: (jnp.minimum(i, tiles - 1), 0)) if delayed else (lambda i: (i, 0))
    o_map = (lambda i: (jnp.maximum(i - 1, 0), 0)) if delayed else (lambda i: (i, 0))
    out_specs = [pl.BlockSpec((tm, n), o_map)] + [s for s, _ in extra_out]
    out_shape = [jax.ShapeDtypeStruct((m, n), F32)] + [s for _, s in extra_out]
    return pl.pallas_call(
        kernel_fn,
        grid=(tiles + int(delayed),),
        in_specs=[pl.BlockSpec((tm, k), x_map)] + specs,
        out_specs=out_specs,
        out_shape=out_shape,
        scratch_shapes=[pltpu.VMEM((k, weight_cols), BF16)] + list(extra_scratch),
        compiler_params=pltpu.CompilerParams(
            dimension_semantics=("arbitrary",), vmem_limit_bytes=VMEM_LIMIT),
        name=name,
    )(x2d, *operands)


def _weight_spec(k, n, col0):
    assert col0 % n == 0
    return pl.BlockSpec((None, k, n), lambda i: (0, 0, col0 // n), pipeline_mode=pl.Buffered(1))


def _norm_proj_hgrn(x2d, g, lb, w, tm):
    k, n = w.shape[1], 4 * WIDTH
    const = lambda i: (0, 0)
    specs = [pl.BlockSpec((1, k), const), pl.BlockSpec((1, WIDTH), const), _weight_spec(k, n, W_HGRN)]
    return _proj_call(_norm_proj_hgrn_kernel, "norm_proj_hgrn", x2d, tm, n, (g, lb, w), specs)[0]


def _norm_proj_conv(x2d, g, w, conv_w, hist, tm, rows_per_seq):
    m, k = x2d.shape[0], w.shape[1]
    n_out, n_w = 4 * WIDTH, 4 * WIDTH + LANES
    const = lambda i: (0, 0)
    tm = min(tm, rows_per_seq)
    assert rows_per_seq % tm == 0 and m % rows_per_seq == 0
    tail_shape = (SUBLANES, 3 * WIDTH)
    specs = [pl.BlockSpec((1, k), const), _weight_spec(k, 4 * WIDTH, W_GDN), _weight_spec(k, LANES, W_SMALL),
             pl.BlockSpec((CONV_W, 3 * WIDTH), const), pl.BlockSpec(tail_shape, const)]
    logit_spec = pl.BlockSpec((SUBLANES, tm), lambda i: (0, jnp.maximum(i - 1, 0)))
    return _proj_call(functools.partial(_norm_proj_conv_kernel, steps_per_seq=rows_per_seq // tm),
                      "norm_proj_gdn", x2d, tm, n_out, (g, w, w, conv_w, hist), specs,
                      extra_out=[(pl.BlockSpec(tail_shape, const), jax.ShapeDtypeStruct(tail_shape, F32)),
                                 (logit_spec, jax.ShapeDtypeStruct((SUBLANES, m), F32))],
                      extra_scratch=[pltpu.VMEM((tm, n_w), F32), pltpu.VMEM(tail_shape, F32)],
                      weight_cols=n_w, delayed=True)


HGRN_LEVELS = 6
HGRN_DIAG = HGRN_LEVELS


def _hgrn_level_table():
    r = np.arange(STACK)
    t, hd = r % CHUNK, r // CHUNK
    x = t[:, None] ^ t[None, :]
    lv = np.floor(np.log2(np.maximum(x, 1))).astype(np.int32)
    valid = (hd[:, None] == hd[None, :]) & (t[:, None] > t[None, :])
    lv = np.where(valid, lv, -1)
    lv = np.where(r[:, None] == r[None, :], HGRN_DIAG, lv)
    return jnp.asarray(lv, dtype=jnp.int32)


def _hgrn_kernel(lv_ref, g_ref, st0_ref, q_ref, f_ref, v_ref, gate_ref, o_ref, stout_ref, st_ref, *, nb, nc):
    @pl.when(pl.program_id(0) == 0)
    def _():
        st_ref[...] = st0_ref[...]

    bsz = nc * nb
    q = _load_chunks(q_ref, nc)
    f = _load_chunks(f_ref, nc)
    v = _load_chunks(v_ref, nc)
    gate = _load_chunks(gate_ref, nc)

    per_seq = lambda a: a.reshape(bsz, CHUNK, WIDTH)
    trow = lax.broadcasted_iota(jnp.int32, (1, CHUNK, WIDTH), 1)
    b = jnp.log(f)
    s = 1
    while s < CHUNK:
        b = (per_seq(b) + jnp.where(trow >= s, per_seq(_roll_rows(b, s)), 0.0)).reshape(b.shape)
        s *= 2

    qs, ks, vs, bs = (_stack_heads(a, bsz) for a in (q, 1.0 - f, v, b))
    lv = lv_ref[...][None]
    to3 = functools.partial(_to_stacks, bsz=bsz)
    to_g = functools.partial(_to_heads, bsz=bsz)
    roll_g = lambda a, shift: to_g(_roll_rows(a.reshape(qs.shape), shift))
    t = lax.broadcasted_iota(jnp.int32, (1, CHUNK, HEAD_DIM), 1)
    qs_g, ks_g, bs_g = to_g(qs), to_g(ks), to_g(bs)

    attn = jnp.where(lv == HGRN_DIAG, _bdot_nt(to3(qs), to3(ks)), 0.0)
    bref = bs_g
    bnext = roll_g(bs_g, -1)
    for li in range(HGRN_LEVELS):
        m = 1 << li
        if li > 0:
            half = m // 2
            upper = (t & (m - 1)) >= half
            bref = jnp.where(upper, roll_g(bref, half), bref)
            bnext = jnp.where(upper, bnext, roll_g(bnext, -half))
        odd = ((t >> li) & 1) == 1
        both = jnp.where(odd, qs_g, ks_g) * jnp.exp(jnp.where(odd, bs_g - bref, bnext - bs_g))
        both = to3(both.reshape(qs.shape)).astype(BF16)
        attn = jnp.where(lv == li, _bdot_nt(both, both), attn)

    o = _bdot(attn, to3(vs))

    n_g = nb * HEADS
    st = st_ref[...]
    qe = qs_g * jnp.exp(bs_g)
    b_end = bs_g[:, CHUNK - 1:CHUNK, :]
    ke = ks_g * jnp.exp(b_end - bs_g)
    vs_g = to_g(vs)
    o_inter = []
    for j in range(nc):
        sl = slice(j * n_g, (j + 1) * n_g)
        o_inter.append(_bdot_nt(qe[sl], st))
        st = st * jnp.exp(b_end[sl]) + _bdot_tn(vs_g[sl], ke[sl])
    st_ref[...] = st
    stout_ref[...] = st
    o = (to_g(o) + jnp.concatenate(o_inter, axis=0)).reshape(qs.shape)

    o = o * lax.rsqrt(jnp.mean(o * o, axis=-1, keepdims=True) + EPS) * g_ref[...]
    o = o * _stack_heads(gate, bsz)
    _store_chunks(o_ref, _unstack_heads(o, bsz), nb, nc)


def _gdn_cumsum_matrix():
    r = np.arange(STACK)
    u = (r[:, None] // CHUNK == r[None, :] // CHUNK) & (r[:, None] <= r[None, :])
    return jnp.asarray(u, dtype=BF16)


def _gdn_kernel(u_ref, alog_ref, dt_ref, g_ref, s_ref, st0_ref, q_ref, k_ref, v_ref, z_ref,
                o_ref, stout_ref, st_ref, *, nb, nc):
    @pl.when(pl.program_id(0) == 0)
    def _():
        st_ref[...] = st0_ref[...]

    bsz = nc * nb
    qs = _stack_heads(_load_chunks(q_ref, nc), bsz)
    ks = _stack_heads(_load_chunks(k_ref, nc), bsz)
    vs = _stack_heads(_load_chunks(v_ref, nc), bsz)
    gate = _load_chunks(z_ref, nc)
    qs = qs * lax.rsqrt(jnp.sum(qs * qs, axis=-1, keepdims=True) + EPS) * (HEAD_DIM ** -0.5)
    ks = ks * lax.rsqrt(jnp.sum(ks * ks, axis=-1, keepdims=True) + EPS)
    to3 = functools.partial(_to_stacks, bsz=bsz)
    qs, ks, vs = to3(qs), to3(ks), to3(vs)
    n_st = bsz * N_STACKS

    srow = _chunks_first_scalars(s_ref[...], nc).reshape(n_st, 2, STACK)
    per_stack = lambda ref: jnp.broadcast_to(ref[...][None], (bsz, N_STACKS, 1, STACK)).reshape(n_st, 1, STACK)
    beta_row = _sigmoid(srow[:, 0:1, :])
    g_row = -jnp.exp(per_stack(alog_ref)) * _softplus(srow[:, 1:2, :] + per_stack(dt_ref))
    n8 = n_st * SUBLANES
    g8 = jnp.broadcast_to(g_row, (n_st, SUBLANES, STACK)).reshape(n8, STACK)
    pieces = jnp.concatenate(_split3(g8), axis=0)
    cum = jnp.dot(pieces, u_ref[...], preferred_element_type=F32)
    cum = cum[0:n8] + cum[n8:2 * n8] + cum[2 * n8:]
    cum_row = cum.reshape(n_st, SUBLANES, STACK)[:, 0:1, :]

    r = lax.broadcasted_iota(jnp.int32, (STACK, STACK), 0)
    cc = lax.broadcasted_iota(jnp.int32, (STACK, STACK), 1)
    eye = (r == cc)[None]
    cum_col = jnp.sum(jnp.where(eye, cum_row, 0.0), axis=2, keepdims=True)
    beta_col = jnp.sum(jnp.where(eye, beta_row, 0.0), axis=2, keepdims=True)
    same = (r >> CHUNK_BITS) == (cc >> CHUNK_BITS)
    lower = (same & (r >= cc))[None]
    strict = (same & (r > cc))[None]
    ratio = jnp.exp(jnp.minimum(cum_col - cum_row, 0.0))

    kkt = _bdot_nt(ks, ks)
    qkt = _bdot_nt(qs, ks)
    lm = jnp.where(strict, beta_col * ratio * kkt, 0.0)

    ident = eye.astype(F32)
    l0 = jnp.where(((r >> NEUMANN_BITS) == (cc >> NEUMANN_BITS))[None], lm, 0.0)
    l2 = _bdot(l0, l0)
    l4 = _bdot(l2, l2)
    xinv = _bdot(_bdot(ident - l0, ident + l2), ident + l4)
    for bits in range(NEUMANN_BITS, CHUNK_BITS):
        off = (((r >> (bits + 1)) == (cc >> (bits + 1))) & ((r >> bits) != (cc >> bits)))[None]
        xinv = xinv - _bdot(_bdot(xinv, jnp.where(off, lm, 0.0)), xinv)

    decay_col = jnp.exp(cum_col)
    rhs = jnp.concatenate([beta_col * vs, (beta_col * decay_col) * ks], axis=2)
    sol = _bdot(xinv, rhs)

    to_g = functools.partial(_to_heads, bsz=bsz)
    n_g = nb * HEADS
    st = st_ref[...]
    u0_g, w_g, q_g = to_g(sol[:, :, :HEAD_DIM]), to_g(sol[:, :, HEAD_DIM:]), to_g(qs)
    cum_g = to_g(cum_col)
    cum_end = cum_g[:, CHUNK - 1:CHUNK, :]
    kdec = to_g(ks) * jnp.exp(cum_end - cum_g)
    decay_g = to_g(decay_col)
    us, o_inter = [], []
    for j in range(nc):
        sl = slice(j * n_g, (j + 1) * n_g)
        u_j = u0_g[sl] - _bdot(w_g[sl], st)
        us.append(u_j)
        o_inter.append(decay_g[sl] * _bdot(q_g[sl], st))
        st = jnp.exp(cum_end[sl]) * st + _bdot_tn(kdec[sl], u_j)
    st_ref[...] = st
    stout_ref[...] = st
    attn = jnp.where(lower, qkt * ratio, 0.0)
    u = jnp.concatenate(us, axis=0).reshape(bsz * HEADS * CHUNK, HEAD_DIM)
    o = jnp.concatenate(o_inter, axis=0) + to_g(_bdot(attn, to3(u)))
    o = o.reshape(bsz * HEADS * CHUNK, HEAD_DIM)

    o = o * lax.rsqrt(jnp.mean(o * o, axis=-1, keepdims=True) + EPS) * g_ref[...]
    o = o * _stack_heads(gate, bsz)
    _store_chunks(o_ref, _unstack_heads(o, bsz), nb, nc)


def _recurrence_call(kernel_fn, name, proj, state0, operands, specs, nc):
    nb, length, _ = proj.shape
    rows = nc * CHUNK
    state_shape = (nb * HEADS, HEAD_DIM, HEAD_DIM)
    state_spec = pl.BlockSpec(state_shape, lambda c: (0, 0, 0))
    chunk_specs = [pl.BlockSpec((nb, rows, WIDTH), functools.partial(lambda c, j: (0, c, j), j=j))
                   for j in range(4)]
    return pl.pallas_call(
        functools.partial(kernel_fn, nb=nb, nc=nc),
        grid=(length // rows,),
        in_specs=specs + [state_spec] + chunk_specs,
        out_specs=[pl.BlockSpec((nb, rows, WIDTH), lambda c: (0, c, 0)), state_spec],
        out_shape=[jax.ShapeDtypeStruct((nb, length, WIDTH), BF16),
                   jax.ShapeDtypeStruct(state_shape, F32)],
        scratch_shapes=[pltpu.VMEM(state_shape, F32)],
        compiler_params=pltpu.CompilerParams(
            dimension_semantics=("arbitrary",), vmem_limit_bytes=VMEM_LIMIT),
        name=name,
    )(*operands, state0, *([proj] * 4))


def _hgrn(proj, state0, norm_g, nc):
    const = lambda c: (0, 0)
    specs = [pl.BlockSpec((STACK, STACK), const), pl.BlockSpec((1, HEAD_DIM), const)]
    return _recurrence_call(_hgrn_kernel, "hgrn", proj, state0, (_hgrn_level_table(), norm_g), specs, nc)


def _gdn(proj, state0, scal, a_log_row, dt_row, norm_g, nc):
    nb = proj.shape[0]
    const = lambda c: (0, 0)
    const3 = lambda c: (0, 0, 0)
    specs = [pl.BlockSpec((STACK, STACK), const),
             pl.BlockSpec((N_STACKS, 1, STACK), const3),
             pl.BlockSpec((N_STACKS, 1, STACK), const3),
             pl.BlockSpec((1, HEAD_DIM), const),
             pl.BlockSpec((nb, nc, N_STACKS, 2, STACK), lambda c: (0, c, 0, 0, 0))]
    return _recurrence_call(_gdn_kernel, "gdn", proj, state0,
                            (_gdn_cumsum_matrix(), a_log_row, dt_row, norm_g, scal), specs, nc)


ROUTE_LANE0 = N_GROUPS
ROUTE_NEG = -1e30


def _route_tile(lg, ltri, carry):
    lane = lax.broadcasted_iota(jnp.int32, lg.shape, 1).astype(F32)
    first = lambda mask: jnp.min(jnp.where(mask, lane, float(LANES)), axis=1, keepdims=True)
    top = lambda mask: jnp.max(jnp.where(mask, lg, ROUTE_NEG), axis=1, keepdims=True)

    is_g = lane < N_GROUPS
    gmax = top(is_g)
    grp = first(is_g & (lg == gmax))
    p_grp = 1.0 / jnp.sum(jnp.where(is_g, jnp.exp(lg - gmax), 0.0), axis=1, keepdims=True)

    lo = ROUTE_LANE0 + grp * EXPERTS_PER_GROUP
    in_grp = (lane >= lo) & (lane < lo + EXPERTS_PER_GROUP)
    m1 = top(in_grp)
    i1 = first(in_grp & (lg == m1))
    rest = in_grp & (lane != i1)
    m2 = top(rest)
    i2 = first(rest & (lg == m2))
    r = jnp.exp(m2 - m1)
    w0 = p_grp / (1.0 + r)
    w1 = p_grp * r / (1.0 + r)

    hot0 = (lane == i1).astype(F32)
    hot1 = (lane == i2).astype(F32)
    both = hot0 + hot1
    before = jnp.dot(ltri, both.astype(BF16), preferred_element_type=F32) + carry
    rank0 = jnp.sum(hot0 * before, axis=1, keepdims=True)
    rank1 = jnp.sum(hot1 * before, axis=1, keepdims=True)
    cols = (w0, w1, i1 - ROUTE_LANE0, i2 - ROUTE_LANE0, rank0, rank1)
    info = jnp.zeros_like(lg)
    for k, col in enumerate(cols):
        info = jnp.where(lane == k, col, info)
    return info, carry + jnp.sum(both, axis=0, keepdims=True)


def _merge_kernel(oa_ref, ob_ref, x_ref, xprev_ref, mg_ref, wgate_ref, hgup_ref, gdup_ref, wout_ref,
                  ng_ref, rw_ref, rb_ref, ltri_ref, h2_ref, xn_ref, info_ref, ids_ref, cnt_ref,
                  wgate_bf, gates_ref, ua_ref, ub_ref, carry_ref):
    i = pl.program_id(0)

    @pl.when(i == 0)
    def _():
        carry_ref[...] = jnp.zeros_like(carry_ref)
        _cast_rows(wgate_ref, wgate_bf)
        _zero_rows(gates_ref)
        _zero_rows(ua_ref)
        _zero_rows(ub_ref)

    gates = _sigmoid(gates_ref[...])
    merged = (gates[:, :D_MODEL] * ua_ref[...] + gates[:, D_MODEL:] * ub_ref[...]).astype(BF16)
    gates_ref[...] = jnp.dot(_normed_bf16(x_ref, mg_ref), wgate_bf[...], preferred_element_type=F32)
    h2 = xprev_ref[...] + jnp.dot(merged, wout_ref[...], preferred_element_type=F32)
    h2_ref[...] = h2
    xn = h2 * lax.rsqrt(jnp.mean(h2 * h2, axis=-1, keepdims=True) + EPS) * ng_ref[...]
    xn_ref[...] = _pack_halves(xn)
    lg = jnp.dot(xn.astype(BF16), rw_ref[...], preferred_element_type=F32) + rb_ref[...]
    ua_ref[...] = jnp.dot(oa_ref[...], hgup_ref[...], preferred_element_type=F32)
    ub_ref[...] = jnp.dot(ob_ref[...], gdup_ref[...], preferred_element_type=F32)
    old = carry_ref[...]
    info, carry = _route_tile(lg, ltri_ref[...], old)
    info_ref[...] = info
    ids_ref[...] = info.T[:SUBLANES].astype(jnp.int32)
    carry = jnp.where(i > 0, carry, old)
    carry_ref[...] = carry
    cnt_ref[...] = carry


def _merge(o_a, o_b, x2d, mix_g, w_gates, hg_up, gd_up, w_out, norm_g, rw, rb):
    t = x2d.shape[0]
    tm = TOKEN_TM
    tiles = t // tm
    new = lambda i: (jnp.minimum(i, tiles - 1), 0)
    row = lambda i: (jnp.maximum(i - 1, 0), 0)
    const = lambda i: (0, 0)
    ltri = jnp.asarray(np.tril(np.ones((tm, tm), np.float32), -1), dtype=BF16)
    return pl.pallas_call(
        _merge_kernel,
        grid=(tiles + 1,),
        in_specs=[pl.BlockSpec((tm, WIDTH), new),
                  pl.BlockSpec((tm, WIDTH), new),
                  pl.BlockSpec((tm, D_MODEL), new),
                  pl.BlockSpec((tm, D_MODEL), row),
                  pl.BlockSpec((1, D_MODEL), const),
                  pl.BlockSpec((D_MODEL, 2 * D_MODEL), const, pipeline_mode=pl.Buffered(1)),
                  pl.BlockSpec((WIDTH, D_MODEL), const),
                  pl.BlockSpec((WIDTH, D_MODEL), const),
                  pl.BlockSpec((D_MODEL, D_MODEL), const),
                  pl.BlockSpec((1, D_MODEL), const),
                  pl.BlockSpec((D_MODEL, LANES), const),
                  pl.BlockSpec((1, LANES), const),
                  pl.BlockSpec((tm, tm), const)],
        out_specs=[pl.BlockSpec((tm, D_MODEL), row),
                   pl.BlockSpec((tm, HALF), row),
                   pl.BlockSpec((tm, LANES), row),
                   pl.BlockSpec((SUBLANES, tm), lambda i: (0, jnp.maximum(i - 1, 0))),
                   pl.BlockSpec((1, LANES), const)],
        out_shape=[jax.ShapeDtypeStruct((t, D_MODEL), F32),
                   jax.ShapeDtypeStruct((t, HALF), U32),
                   jax.ShapeDtypeStruct((t, LANES), F32),
                   jax.ShapeDtypeStruct((SUBLANES, t), jnp.int32),
                   jax.ShapeDtypeStruct((1, LANES), F32)],
        scratch_shapes=[pltpu.VMEM((D_MODEL, 2 * D_MODEL), BF16), pltpu.VMEM((tm, 2 * D_MODEL), F32),
                        pltpu.VMEM((tm, D_MODEL), F32), pltpu.VMEM((tm, D_MODEL), F32),
                        pltpu.VMEM((1, LANES), F32)],
        compiler_params=pltpu.CompilerParams(
            dimension_semantics=("arbitrary",), vmem_limit_bytes=VMEM_LIMIT),
        name="merge",
    )(o_a, o_b, x2d, x2d, mix_g, w_gates, hg_up, gd_up, w_out, norm_g, rw, rb, ltri)


def _sc_gather(table, idx):
    n_idx = idx.shape[0]
    cols = table.shape[1]
    per_worker = n_idx // SC_WORKERS
    assert n_idx % (SC_WORKERS * SC_CHUNK) == 0
    mesh = plsc.VectorSubcoreMesh(core_axis_name="c", subcore_axis_name="s")

    @functools.partial(
        pl.kernel, mesh=mesh,
        out_type=jax.ShapeDtypeStruct((n_idx, cols), table.dtype),
        scratch_types=[pltpu.VMEM((SC_CHUNK,), jnp.int32),
                       pltpu.VMEM((SC_CHUNK, cols), table.dtype),
                       pltpu.SemaphoreType.DMA],
    )
    def gather(table_hbm, idx_hbm, out_hbm, idx_v, rows_v, sem):
        worker = lax.axis_index("s") * SC_CORES + lax.axis_index("c")
        base = worker * per_worker

        @pl.loop(0, per_worker // SC_CHUNK)
        def _(c):
            off = pl.multiple_of(base + c * SC_CHUNK, SC_CHUNK)
            pltpu.sync_copy(idx_hbm.at[pl.ds(off, SC_CHUNK)], idx_v)
            pltpu.async_copy(table_hbm.at[idx_v], rows_v, sem).wait()
            pltpu.sync_copy(rows_v, out_hbm.at[pl.ds(off, SC_CHUNK)])

    return gather(table, idx)


def _sc_dispatch(rows, dest, n_out):
    t, cols = rows.shape
    per_worker = t // SC_WORKERS
    assert t % (SC_WORKERS * SC_CHUNK) == 0 and dest.shape[0] == 2 * t
    mesh = plsc.VectorSubcoreMesh(core_axis_name="c", subcore_axis_name="s")

    @functools.partial(
        pl.kernel, mesh=mesh,
        out_type=jax.ShapeDtypeStruct((n_out, cols), rows.dtype),
        scratch_types=[pltpu.VMEM((SC_CHUNK,), jnp.int32),
                       pltpu.VMEM((SC_CHUNK, cols), rows.dtype)],
    )
    def dispatch(rows_hbm, dest_hbm, out_hbm, idx_v, rows_v):
        worker = lax.axis_index("s") * SC_CORES + lax.axis_index("c")
        base = worker * per_worker

        @pl.loop(0, per_worker // SC_CHUNK)
        def _(c):
            off = pl.multiple_of(base + c * SC_CHUNK, SC_CHUNK)
            pltpu.sync_copy(rows_hbm.at[pl.ds(off, SC_CHUNK)], rows_v)
            for slot in range(2):
                pltpu.sync_copy(dest_hbm.at[pl.ds(slot * t + off, SC_CHUNK)], idx_v)
                pltpu.sync_copy(rows_v, out_hbm.at[idx_v])

    return dispatch(rows, dest)


def _moe_kernel(be_ref, nu_ref, x_ref, wg_ref, wu_ref, wd_ref, y_ref, wg_bf, wu_bf, wd_bf):
    i = pl.program_id(0)

    @pl.when((i == 0) | (be_ref[i] != be_ref[jnp.maximum(i - 1, 0)]))
    def _():
        _cast_rows(wg_ref, wg_bf)
        _cast_rows(wu_ref, wu_bf)
        _cast_rows(wd_ref, wd_bf)

    @pl.when(i < nu_ref[0])
    def _():
        xb = _unpack_halves(x_ref[...]).astype(BF16)
        a = jnp.dot(xb, wg_bf[...], preferred_element_type=F32)
        u = jnp.dot(xb, wu_bf[...], preferred_element_type=F32)
        y = jnp.dot((_silu(a) * u).astype(BF16), wd_bf[...], preferred_element_type=F32)
        y_ref[...] = _pack_halves(y)

    @pl.when(i >= nu_ref[0])
    def _():
        y_ref[...] = jnp.zeros_like(y_ref)


def _moe(blk_expert, n_used, x_sorted, w_gate, w_up, w_down):
    n_blocks = blk_expert.shape[0]
    wspec = lambda shape: pl.BlockSpec((None,) + shape, lambda i, be, nu: (be[i], 0, 0))
    rows = pl.BlockSpec((MOE_ROWS, HALF), lambda i, be, nu: (i, 0))
    grid_spec = pltpu.PrefetchScalarGridSpec(
        num_scalar_prefetch=2,
        grid=(n_blocks,),
        in_specs=[rows, wspec((D_MODEL, D_FF)), wspec((D_MODEL, D_FF)), wspec((D_FF, D_MODEL))],
        out_specs=rows,
        scratch_shapes=[pltpu.VMEM((D_MODEL, D_FF), BF16), pltpu.VMEM((D_MODEL, D_FF), BF16),
                        pltpu.VMEM((D_FF, D_MODEL), BF16)],
    )
    return pl.pallas_call(
        _moe_kernel,
        grid_spec=grid_spec,
        out_shape=jax.ShapeDtypeStruct(x_sorted.shape, U32),
        compiler_params=pltpu.CompilerParams(
            dimension_semantics=("arbitrary",), vmem_limit_bytes=VMEM_LIMIT),
        name="moe",
    )(blk_expert, n_used, x_sorted, w_gate, w_up, w_down)


def _combine_kernel(h2_ref, rw_ref, g_ref, y0_ref, y1_ref, o_ref):
    rw = rw_ref[...]
    h = h2_ref[...] + rw[:, 0:1] * _unpack_halves(y0_ref[...]) + rw[:, 1:2] * _unpack_halves(y1_ref[...])
    o_ref[...] = h * lax.rsqrt(jnp.mean(h * h, axis=-1, keepdims=True) + EPS) * g_ref[...]


def _combine(h2, rweights, final_g, y):
    t = h2.shape[0]
    tm = min(COMBINE_TM, t)
    row = lambda i: (i, 0)
    return pl.pallas_call(
        _combine_kernel,
        grid=(t // tm,),
        in_specs=[pl.BlockSpec((tm, D_MODEL), row),
                  pl.BlockSpec((tm, LANES), row),
                  pl.BlockSpec((1, D_MODEL), lambda i: (0, 0)),
                  pl.BlockSpec((tm, HALF), row),
                  pl.BlockSpec((tm, HALF), lambda i: (i + t // tm, 0))],
        out_specs=pl.BlockSpec((tm, D_MODEL), row),
        out_shape=jax.ShapeDtypeStruct((t, D_MODEL), F32),
        compiler_params=pltpu.CompilerParams(
            dimension_semantics=("arbitrary",), vmem_limit_bytes=VMEM_LIMIT),
        name="combine",
    )(h2, rweights, final_g, y, y)


def _block_layout(ids, counts_row, t):
    n_blocks = 2 * t // MOE_ROWS + N_EXPERTS
    counts = counts_row[0, ROUTE_LANE0:ROUTE_LANE0 + N_EXPERTS].astype(jnp.int32)
    padded = ((counts + MOE_ROWS - 1) // MOE_ROWS) * MOE_ROWS
    pend = jnp.cumsum(padded)
    pstart = pend - padded
    blk_first = jnp.arange(n_blocks, dtype=jnp.int32) * MOE_ROWS
    blk_expert = jnp.minimum(jnp.sum((pend[None, :] <= blk_first[:, None]).astype(jnp.int32), axis=1),
                             N_EXPERTS - 1).astype(jnp.int32)
    n_used = (pend[-1] // MOE_ROWS).astype(jnp.int32).reshape(1)
    expert, rank = ids[2:4], ids[4:6]
    experts = jnp.arange(N_EXPERTS, dtype=jnp.int32)
    first_row = jnp.sum(jnp.where(expert[:, :, None] == experts, pstart, 0), axis=-1)
    dest = (first_row + rank).reshape(2 * t)
    return dest, blk_expert, n_used, n_blocks * MOE_ROWS


def kernel(x, meta_tokens, hg_lb_logits, norm_mix_g, w_in, gd_conv_w, gd_A_log, gd_dt_bias, hg_norm_g, gd_norm_g, hg_up, gd_up, w_out, norm_ffn_g, router_group_w, router_group_b, router_expert_w, router_expert_b, w_gate, w_up, w_down, final_norm_g):
    bsz, seq, d = x.shape
    t = bsz * seq
    x2d = x.reshape(t, d)

    lb = jnp.cumsum(jax.nn.softmax(hg_lb_logits.astype(F32), axis=0), axis=0)[0].reshape(1, WIDTH)
    w_all = w_in.astype(F32)
    w_gates = w_all[0, :, W_SMALL + 2 * HEADS:]
    g_mix = norm_mix_g[0].reshape(1, d)
    meta_blk = jnp.concatenate([jnp.zeros((CHUNK - N_META, d), F32), meta_tokens.astype(F32)], axis=0)

    conv_w = gd_conv_w[0].astype(F32)
    no_history = jnp.zeros((SUBLANES, 3 * WIDTH), F32)
    proj, proj_meta = {}, {}
    proj["hgrn"] = _norm_proj_hgrn(x2d, g_mix, lb, w_all, PROJ_TM)
    proj_meta["hgrn"] = _norm_proj_hgrn(meta_blk, g_mix, lb, w_all, CHUNK)
    proj_meta["gdn"], meta_tail, logits_meta = _norm_proj_conv(meta_blk, g_mix, w_all, conv_w, no_history,
                                                               CHUNK, CHUNK)
    proj["gdn"], _, logits = _norm_proj_conv(x2d, g_mix, w_all, conv_w, meta_tail, PROJ_TM, seq)

    def scalar_rows(lg, nb, nc):
        s = lg.reshape(2, N_STACKS, HEADS_PER_STACK, nb, nc, CHUNK)
        return s.transpose(3, 4, 1, 0, 2, 5).reshape(nb, nc, N_STACKS, 2, STACK)

    a_log_row = jnp.repeat(gd_A_log[0].astype(F32), CHUNK).reshape(N_STACKS, 1, STACK)
    dt_row = jnp.repeat(gd_dt_bias[0].astype(F32), CHUNK).reshape(N_STACKS, 1, STACK)
    hg_g, gd_g = hg_norm_g[0].reshape(1, HEAD_DIM), gd_norm_g[0].reshape(1, HEAD_DIM)

    zero_state = jnp.zeros((HEADS, HEAD_DIM, HEAD_DIM), F32)
    per_seq = lambda st: jnp.tile(st, (bsz, 1, 1))
    _, hg_state = _hgrn(proj_meta["hgrn"][None], zero_state, hg_g, 1)
    o_a, _ = _hgrn(proj["hgrn"].reshape(bsz, seq, -1), per_seq(hg_state), hg_g, HGRN_CHUNKS)
    _, gd_state = _gdn(proj_meta["gdn"][None], zero_state, scalar_rows(logits_meta, 1, 1),
                       a_log_row, dt_row, gd_g, 1)
    o_b, _ = _gdn(proj["gdn"].reshape(bsz, seq, -1), per_seq(gd_state),
                  scalar_rows(logits, bsz, seq // CHUNK), a_log_row, dt_row, gd_g, GDN_CHUNKS)

    rw = jnp.zeros((d, LANES), F32)
    rw = rw.at[:, :N_GROUPS].set(router_group_w[0]).at[:, N_GROUPS:N_GROUPS + N_EXPERTS].set(router_expert_w[0])
    rb = jnp.zeros((1, LANES), F32)
    rb = rb.at[0, :N_GROUPS].set(router_group_b[0]).at[0, N_GROUPS:N_GROUPS + N_EXPERTS].set(router_expert_b[0])
    h2, xn, info, ids, counts = _merge(o_a.reshape(t, WIDTH), o_b.reshape(t, WIDTH), x2d, g_mix, w_gates,
                                  hg_up[0].astype(BF16), gd_up[0].astype(BF16), w_out[0].astype(BF16),
                                  norm_ffn_g[0].reshape(1, d), rw.astype(BF16), rb)

    dest, blk_expert, n_used, n_rows = _block_layout(ids, counts, t)
    x_sorted = _sc_dispatch(xn, dest, n_rows)
    y_sorted = _moe(blk_expert, n_used, x_sorted,
                    w_gate[0].astype(F32), w_up[0].astype(F32), w_down[0].astype(F32))
    y_tok = _sc_gather(y_sorted, dest)
    out = _combine(h2, info, final_norm_g.reshape(1, d), y_tok)
    return out.reshape(bsz, seq, d)
```

```python
import functools

import numpy as np
import jax
import jax.numpy as jnp
from jax import lax
from jax.experimental import pallas as pl
from jax.experimental.pallas import tpu as pltpu
from jax.experimental.pallas import tpu_sc as plsc

F32 = jnp.float32
BF16 = jnp.bfloat16
U32 = jnp.uint32

D_MODEL = 1024
N_META = 16
CHUNK_BITS = 6
CHUNK = 1 << CHUNK_BITS
EPS = 1e-6
HEADS = 4
HEAD_DIM = 128
WIDTH = HEADS * HEAD_DIM
HEADS_PER_STACK = 2
STACK = HEADS_PER_STACK * CHUNK
N_STACKS = HEADS // HEADS_PER_STACK
CONV_W = 4
NEUMANN_BITS = 3
N_GROUPS = 4
EXPERTS_PER_GROUP = 8
N_EXPERTS = N_GROUPS * EXPERTS_PER_GROUP
D_FF = 512
LANES = 128
SUBLANES = 8
HALF = D_MODEL // 2
MOE_ROWS = 512
CAST_SLAB = 128
SC_CORES, SC_SUBCORES = 2, 16
SC_WORKERS = SC_CORES * SC_SUBCORES
SC_CHUNK = 128
PROJ_TM = 1024
W_HGRN, W_GDN, W_SMALL = 0, 4 * WIDTH, 8 * WIDTH
HGRN_CHUNKS, GDN_CHUNKS = 4, 2
TOKEN_TM = 512
COMBINE_TM = 1024
VMEM_LIMIT = 56 * 1024 * 1024


def _bdot(a, b):
    return lax.dot_general(a.astype(BF16), b.astype(BF16), (((2,), (1,)), ((0,), (0,))),
                           preferred_element_type=F32)


def _bdot_nt(a, b):
    return lax.dot_general(a.astype(BF16), b.astype(BF16), (((2,), (2,)), ((0,), (0,))),
                           preferred_element_type=F32)


def _bdot_tn(a, b):
    return lax.dot_general(a.astype(BF16), b.astype(BF16), (((1,), (1,)), ((0,), (0,))),
                           preferred_element_type=F32)


def _split3(x):
    hi = x.astype(BF16)
    r1 = x - hi.astype(F32)
    mid = r1.astype(BF16)
    lo = (r1 - mid.astype(F32)).astype(BF16)
    return hi, mid, lo


def _sigmoid(x):
    return 1.0 / (1.0 + jnp.exp(-x))


def _silu(x):
    return x * _sigmoid(x)


def _softplus(x):
    return jnp.maximum(x, 0.0) + jnp.log(1.0 + jnp.exp(-jnp.abs(x)))


def _stack_heads(x, bsz):
    x3 = x.reshape(bsz, CHUNK, WIDTH)
    y = jnp.concatenate([x3[:, :, h * HEAD_DIM:(h + 1) * HEAD_DIM] for h in range(HEADS)], axis=1)
    return y.reshape(bsz * HEADS * CHUNK, HEAD_DIM)


def _unstack_heads(x, bsz):
    x3 = x.reshape(bsz, HEADS * CHUNK, HEAD_DIM)
    return jnp.concatenate([x3[:, h * CHUNK:(h + 1) * CHUNK, :] for h in range(HEADS)], axis=2)


def _to_stacks(a, bsz):
    return a.reshape(bsz * N_STACKS, STACK, a.shape[-1])


def _to_heads(a, bsz):
    return a.reshape(bsz * HEADS, CHUNK, a.shape[-1])


def _roll_rows(x, shift):
    n = x.shape[0]
    return pltpu.roll(x, shift % n, axis=0)


def _chunks_first(x, nc):
    return jnp.concatenate([x[:, j * CHUNK:(j + 1) * CHUNK] for j in range(nc)], axis=0)


def _chunks_first_scalars(s, nc):
    return jnp.concatenate([s[:, j] for j in range(nc)], axis=0)


def _load_chunks(ref, nc):
    x = _chunks_first(ref[...], nc)
    return x.reshape(x.shape[0] * CHUNK, WIDTH)


def _store_chunks(ref, x, nb, nc):
    ref[...] = jnp.concatenate([x[j * nb:(j + 1) * nb] for j in range(nc)], axis=1).astype(ref.dtype)


def _cast_rows(src_ref, dst_ref, col0=0):
    slab = CAST_SLAB
    rows, cols = src_ref.shape
    assert rows % slab == 0

    def body(k, carry):
        r = pl.multiple_of(k * slab, slab)
        dst_ref[pl.ds(r, slab), col0:col0 + cols] = src_ref[pl.ds(r, slab), :].astype(dst_ref.dtype)
        return carry

    lax.fori_loop(0, rows // slab, body, 0)


def _pack_halves(x):
    lo = lax.bitcast_convert_type(x[:, :HALF].astype(BF16).astype(F32), U32)
    hi = lax.bitcast_convert_type(x[:, HALF:].astype(BF16).astype(F32), U32)
    return (lo >> 16) | (hi & jnp.uint32(0xFFFF0000))


def _unpack_halves(p):
    lo = lax.bitcast_convert_type(p << 16, F32)
    hi = lax.bitcast_convert_type(p & jnp.uint32(0xFFFF0000), F32)
    return jnp.concatenate([lo, hi], axis=1)


def _normed_bf16(x_ref, g_ref):
    x = x_ref[...]
    return (x * lax.rsqrt(jnp.mean(x * x, axis=-1, keepdims=True) + EPS) * g_ref[...]).astype(BF16)


def _norm_proj_hgrn_kernel(x_ref, g_ref, lb_ref, w_ref, o_ref, wbf_ref):
    @pl.when(pl.program_id(0) == 0)
    def _():
        _cast_rows(w_ref, wbf_ref)

    acc = jnp.dot(_normed_bf16(x_ref, g_ref), wbf_ref[...], preferred_element_type=F32)
    w1, w2, w3 = WIDTH, 2 * WIDTH, 3 * WIDTH
    lb = lb_ref[...]
    o_ref[:, :w1] = _silu(acc[:, :w1])
    o_ref[:, w1:w2] = lb + (1.0 - lb) * _sigmoid(acc[:, w1:w2])
    o_ref[:, w2:w3] = acc[:, w2:w3]
    o_ref[:, w3:] = _silu(acc[:, w3:])


def _zero_rows(ref):
    slab = min(CAST_SLAB, ref.shape[0])
    assert ref.shape[0] % slab == 0

    def body(k, carry):
        ref[pl.ds(pl.multiple_of(k * slab, slab), slab), :] = jnp.zeros((slab, ref.shape[1]), ref.dtype)
        return carry

    lax.fori_loop(0, ref.shape[0] // slab, body, 0)


def _norm_proj_conv_kernel(x_ref, g_ref, w_ref, ws_ref, cw_ref, hist_ref, o_ref, tail_ref, logit_ref,
                           wbf_ref, acc_ref, carry_ref, *, steps_per_seq):
    i = pl.program_id(0)

    @pl.when(i == 0)
    def _():
        _cast_rows(w_ref, wbf_ref)
        _cast_rows(ws_ref, wbf_ref, 4 * WIDTH)
        _zero_rows(acc_ref)
        carry_ref[...] = jnp.zeros_like(carry_ref)

    xn = _normed_bf16(x_ref, g_ref)
    w3, w4 = 3 * WIDTH, 4 * WIDTH
    rows = x_ref.shape[0]
    first_of_seq = (i - 1) % steps_per_seq == 0
    cw = cw_ref[...]
    r8 = lax.broadcasted_iota(jnp.int32, (SUBLANES, WIDTH), 0)
    for c0 in range(0, w3, WIDTH):
        cols = slice(c0, c0 + WIDTH)
        raw = acc_ref[:, cols]
        hist = jnp.where(first_of_seq, hist_ref[:, cols], carry_ref[:, cols])
        tap = lambda d: cw[CONV_W - 1 - d:CONV_W - d, cols]
        body = raw * tap(0)
        head = raw[:SUBLANES]
        first = head * tap(0)
        for d in range(1, CONV_W):
            body = body + _roll_rows(raw, d) * tap(d)
            first = first + jnp.where(r8 >= d, _roll_rows(head, d), _roll_rows(hist, d)) * tap(d)
        o_ref[:SUBLANES, cols] = _silu(first)
        o_ref[SUBLANES:, cols] = _silu(body[SUBLANES:])
        carry_ref[:, cols] = raw[rows - SUBLANES:]
        acc_ref[:, cols] = jnp.dot(xn, wbf_ref[:, cols], preferred_element_type=F32)
    rest = acc_ref[:, w3:]
    o_ref[:, w3:] = _silu(rest[:, :WIDTH])
    logit_ref[...] = rest[:, WIDTH:].T[:SUBLANES]
    acc_ref[:, w3:] = jnp.dot(xn, wbf_ref[:, w3:], preferred_element_type=F32)
    tail_ref[...] = carry_ref[...]


def _proj_call(kernel_fn, name, x2d, tm, n, operands, specs, extra_out=(), extra_scratch=(),
               weight_cols=None, delayed=False):
    m, k = x2d.shape
    weight_cols = n if weight_cols is None else weight_cols
    tm = min(tm, m)
    tiles = m // tm
    x_map = (lambda i: (jnp.minimum(i, tiles - 1), 0)) if delayed else (lambda i: (i, 0))
    o_map = (lambda i: (jnp.maximum(i - 1, 0), 0)) if delayed else (lambda i: (i, 0))
    out_specs = [pl.BlockSpec((tm, n), o_map)] + [s for s, _ in extra_out]
    out_shape = [jax.ShapeDtypeStruct((m, n), F32)] + [s for _, s in extra_out]
    return pl.pallas_call(
        kernel_fn,
        grid=(tiles + int(delayed),),
        in_specs=[pl.BlockSpec((tm, k), x_map)] + specs,
        out_specs=out_specs,
        out_shape=out_shape,
        scratch_shapes=[pltpu.VMEM((k, weight_cols), BF16)] + list(extra_scratch),
        compiler_params=pltpu.CompilerParams(
            dimension_semantics=("arbitrary",), vmem_limit_bytes=VMEM_LIMIT),
        name=name,
    )(x2d, *operands)


def _weight_spec(k, n, col0):
    assert col0 % n == 0
    return pl.BlockSpec((None, k, n), lambda i: (0, 0, col0 // n), pipeline_mode=pl.Buffered(1))


def _norm_proj_hgrn(x2d, g, lb, w, tm):
    k, n = w.shape[1], 4 * WIDTH
    const = lambda i: (0, 0)
    specs = [pl.BlockSpec((1, k), const), pl.BlockSpec((1, WIDTH), const), _weight_spec(k, n, W_HGRN)]
    return _proj_call(_norm_proj_hgrn_kernel, "norm_proj_hgrn", x2d, tm, n, (g, lb, w), specs)[0]


def _norm_proj_conv(x2d, g, w, conv_w, hist, tm, rows_per_seq):
    m, k = x2d.shape[0], w.shape[1]
    n_out, n_w = 4 * WIDTH, 4 * WIDTH + LANES
    const = lambda i: (0, 0)
    tm = min(tm, rows_per_seq)
    assert rows_per_seq % tm == 0 and m % rows_per_seq == 0
    tail_shape = (SUBLANES, 3 * WIDTH)
    specs = [pl.BlockSpec((1, k), const), _weight_spec(k, 4 * WIDTH, W_GDN), _weight_spec(k, LANES, W_SMALL),
             pl.BlockSpec((CONV_W, 3 * WIDTH), const), pl.BlockSpec(tail_shape, const)]
    logit_spec = pl.BlockSpec((SUBLANES, tm), lambda i: (0, jnp.maximum(i - 1, 0)))
    return _proj_call(functools.partial(_norm_proj_conv_kernel, steps_per_seq=rows_per_seq // tm),
                      "norm_proj_gdn", x2d, tm, n_out, (g, w, w, conv_w, hist), specs,
                      extra_out=[(pl.BlockSpec(tail_shape, const), jax.ShapeDtypeStruct(tail_shape, F32)),
                                 (logit_spec, jax.ShapeDtypeStruct((SUBLANES, m), F32))],
                      extra_scratch=[pltpu.VMEM((tm, n_w), F32), pltpu.VMEM(tail_shape, F32)],
                      weight_cols=n_w, delayed=True)


HGRN_LEVELS = 6
HGRN_DIAG = HGRN_LEVELS


def _hgrn_level_table():
    r = np.arange(STACK)
    t, hd = r % CHUNK, r // CHUNK
    x = t[:, None] ^ t[None, :]
    lv = np.floor(np.log2(np.maximum(x, 1))).astype(np.int32)
    valid = (hd[:, None] == hd[None, :]) & (t[:, None] > t[None, :])
    lv = np.where(valid, lv, -1)
    lv = np.where(r[:, None] == r[None, :], HGRN_DIAG, lv)
    return jnp.asarray(lv, dtype=jnp.int32)


def _hgrn_kernel(lv_ref, g_ref, st0_ref, q_ref, f_ref, v_ref, gate_ref, o_ref, stout_ref, st_ref, *, nb, nc):
    @pl.when(pl.program_id(0) == 0)
    def _():
        st_ref[...] = st0_ref[...]

    bsz = nc * nb
    q = _load_chunks(q_ref, nc)
    f = _load_chunks(f_ref, nc)
    v = _load_chunks(v_ref, nc)
    gate = _load_chunks(gate_ref, nc)

    per_seq = lambda a: a.reshape(bsz, CHUNK, WIDTH)
    trow = lax.broadcasted_iota(jnp.int32, (1, CHUNK, WIDTH), 1)
    b = jnp.log(f)
    s = 1
    while s < CHUNK:
        b = (per_seq(b) + jnp.where(trow >= s, per_seq(_roll_rows(b, s)), 0.0)).reshape(b.shape)
        s *= 2

    qs, ks, vs, bs = (_stack_heads(a, bsz) for a in (q, 1.0 - f, v, b))
    lv = lv_ref[...][None]
    to3 = functools.partial(_to_stacks, bsz=bsz)
    to_g = functools.partial(_to_heads, bsz=bsz)
    roll_g = lambda a, shift: to_g(_roll_rows(a.reshape(qs.shape), shift))
    t = lax.broadcasted_iota(jnp.int32, (1, CHUNK, HEAD_DIM), 1)
    qs_g, ks_g, bs_g = to_g(qs), to_g(ks), to_g(bs)

    attn = jnp.where(lv == HGRN_DIAG, _bdot_nt(to3(qs), to3(ks)), 0.0)
    bref = bs_g
    bnext = roll_g(bs_g, -1)
    for li in range(HGRN_LEVELS):
        m = 1 << li
        if li > 0:
            half = m // 2
            upper = (t & (m - 1)) >= half
            if half < SUBLANES:
                grp = lambda a: a.reshape(-1, SUBLANES, HEAD_DIM)
                local = lambda a, sh: pltpu.roll(grp(a), sh % SUBLANES, axis=1).reshape(a.shape)
                bref = jnp.where(upper, local(bref, half), bref)
                bnext = jnp.where(upper, bnext, local(bnext, -half))
            else:
                bref = jnp.where(upper, roll_g(bref, half), bref)
                bnext = jnp.where(upper, bnext, roll_g(bnext, -half))
        odd = ((t >> li) & 1) == 1
        both = jnp.where(odd, qs_g, ks_g) * jnp.exp(jnp.where(odd, bs_g - bref, bnext - bs_g))
        both = to3(both.reshape(qs.shape)).astype(BF16)
        attn = jnp.where(lv == li, _bdot_nt(both, both), attn)

    o = _bdot(attn, to3(vs))

    n_g = nb * HEADS
    st = st_ref[...]
    qe = qs_g * jnp.exp(bs_g)
    b_end = bs_g[:, CHUNK - 1:CHUNK, :]
    ke = ks_g * jnp.exp(b_end - bs_g)
    vs_g = to_g(vs)
    o_inter = []
    for j in range(nc):
        sl = slice(j * n_g, (j + 1) * n_g)
        o_inter.append(_bdot_nt(qe[sl], st))
        st = st * jnp.exp(b_end[sl]) + _bdot_tn(vs_g[sl], ke[sl])
    st_ref[...] = st
    stout_ref[...] = st
    o = (to_g(o) + jnp.concatenate(o_inter, axis=0)).reshape(qs.shape)

    o = o * lax.rsqrt(jnp.mean(o * o, axis=-1, keepdims=True) + EPS) * g_ref[...]
    o = o * _stack_heads(gate, bsz)
    _store_chunks(o_ref, _unstack_heads(o, bsz), nb, nc)


def _gdn_cumsum_matrix():
    r = np.arange(STACK)
    u = (r[:, None] // CHUNK == r[None, :] // CHUNK) & (r[:, None] <= r[None, :])
    return jnp.asarray(u, dtype=BF16)


def _gdn_kernel(u_ref, alog_ref, dt_ref, g_ref, s_ref, st0_ref, q_ref, k_ref, v_ref, z_ref,
                o_ref, stout_ref, st_ref, *, nb, nc):
    @pl.when(pl.program_id(0) == 0)
    def _():
        st_ref[...] = st0_ref[...]

    bsz = nc * nb
    qs = _stack_heads(_load_chunks(q_ref, nc), bsz)
    ks = _stack_heads(_load_chunks(k_ref, nc), bsz)
    vs = _stack_heads(_load_chunks(v_ref, nc), bsz)
    gate = _load_chunks(z_ref, nc)
    qs = qs * lax.rsqrt(jnp.sum(qs * qs, axis=-1, keepdims=True) + EPS) * (HEAD_DIM ** -0.5)
    ks = ks * lax.rsqrt(jnp.sum(ks * ks, axis=-1, keepdims=True) + EPS)
    to3 = functools.partial(_to_stacks, bsz=bsz)
    qs, ks, vs = to3(qs), to3(ks), to3(vs)
    n_st = bsz * N_STACKS

    srow = _chunks_first_scalars(s_ref[...], nc).reshape(n_st, 2, STACK)
    per_stack = lambda ref: jnp.broadcast_to(ref[...][None], (bsz, N_STACKS, 1, STACK)).reshape(n_st, 1, STACK)
    beta_row = _sigmoid(srow[:, 0:1, :])
    g_row = -jnp.exp(per_stack(alog_ref)) * _softplus(srow[:, 1:2, :] + per_stack(dt_ref))
    n8 = n_st * SUBLANES
    g8 = jnp.broadcast_to(g_row, (n_st, SUBLANES, STACK)).reshape(n8, STACK)
    pieces = jnp.concatenate(_split3(g8), axis=0)
    cum = jnp.dot(pieces, u_ref[...], preferred_element_type=F32)
    cum = cum[0:n8] + cum[n8:2 * n8] + cum[2 * n8:]
    cum_row = cum.reshape(n_st, SUBLANES, STACK)[:, 0:1, :]

    r = lax.broadcasted_iota(jnp.int32, (STACK, STACK), 0)
    cc = lax.broadcasted_iota(jnp.int32, (STACK, STACK), 1)
    eye = (r == cc)[None]
    cum_col = jnp.sum(jnp.where(eye, cum_row, 0.0), axis=2, keepdims=True)
    beta_col = jnp.sum(jnp.where(eye, beta_row, 0.0), axis=2, keepdims=True)
    same = (r >> CHUNK_BITS) == (cc >> CHUNK_BITS)
    lower = (same & (r >= cc))[None]
    strict = (same & (r > cc))[None]
    ratio = jnp.exp(jnp.minimum(cum_col - cum_row, 0.0))

    kkt = _bdot_nt(ks, ks)
    qkt = _bdot_nt(qs, ks)
    lm = jnp.where(strict, beta_col * ratio * kkt, 0.0)

    ident = eye.astype(F32)
    l0 = jnp.where(((r >> NEUMANN_BITS) == (cc >> NEUMANN_BITS))[None], lm, 0.0)
    l2 = _bdot(l0, l0)
    l4 = _bdot(l2, l2)
    xinv = _bdot(_bdot(ident - l0, ident + l2), ident + l4)
    for bits in range(NEUMANN_BITS, CHUNK_BITS):
        off = (((r >> (bits + 1)) == (cc >> (bits + 1))) & ((r >> bits) != (cc >> bits)))[None]
        xinv = xinv - _bdot(_bdot(xinv, jnp.where(off, lm, 0.0)), xinv)

    decay_col = jnp.exp(cum_col)
    rhs = jnp.concatenate([beta_col * vs, (beta_col * decay_col) * ks], axis=2)
    sol = _bdot(xinv, rhs)

    to_g = functools.partial(_to_heads, bsz=bsz)
    n_g = nb * HEADS
    st = st_ref[...]
    u0_g, w_g, q_g = to_g(sol[:, :, :HEAD_DIM]), to_g(sol[:, :, HEAD_DIM:]), to_g(qs)
    cum_g = to_g(cum_col)
    cum_end = cum_g[:, CHUNK - 1:CHUNK, :]
    kdec = to_g(ks) * jnp.exp(cum_end - cum_g)
    decay_g = to_g(decay_col)
    us, o_inter = [], []
    for j in range(nc):
        sl = slice(j * n_g, (j + 1) * n_g)
        u_j = u0_g[sl] - _bdot(w_g[sl], st)
        us.append(u_j)
        o_inter.append(decay_g[sl] * _bdot(q_g[sl], st))
        st = jnp.exp(cum_end[sl]) * st + _bdot_tn(kdec[sl], u_j)
    st_ref[...] = st
    stout_ref[...] = st
    attn = jnp.where(lower, qkt * ratio, 0.0)
    u = jnp.concatenate(us, axis=0).reshape(bsz * HEADS * CHUNK, HEAD_DIM)
    o = jnp.concatenate(o_inter, axis=0) + to_g(_bdot(attn, to3(u)))
    o = o.reshape(bsz * HEADS * CHUNK, HEAD_DIM)

    o = o * lax.rsqrt(jnp.mean(o * o, axis=-1, keepdims=True) + EPS) * g_ref[...]
    o = o * _stack_heads(gate, bsz)
    _store_chunks(o_ref, _unstack_heads(o, bsz), nb, nc)


def _recurrence_call(kernel_fn, name, proj, state0, operands, specs, nc):
    nb, length, _ = proj.shape
    rows = nc * CHUNK
    state_shape = (nb * HEADS, HEAD_DIM, HEAD_DIM)
    state_spec = pl.BlockSpec(state_shape, lambda c: (0, 0, 0))
    chunk_specs = [pl.BlockSpec((nb, rows, WIDTH), functools.partial(lambda c, j: (0, c, j), j=j))
                   for j in range(4)]
    return pl.pallas_call(
        functools.partial(kernel_fn, nb=nb, nc=nc),
        grid=(length // rows,),
        in_specs=specs + [state_spec] + chunk_specs,
        out_specs=[pl.BlockSpec((nb, rows, WIDTH), lambda c: (0, c, 0)), state_spec],
        out_shape=[jax.ShapeDtypeStruct((nb, length, WIDTH), BF16),
                   jax.ShapeDtypeStruct(state_shape, F32)],
        scratch_shapes=[pltpu.VMEM(state_shape, F32)],
        compiler_params=pltpu.CompilerParams(
            dimension_semantics=("arbitrary",), vmem_limit_bytes=VMEM_LIMIT),
        name=name,
    )(*operands, state0, *([proj] * 4))


def _hgrn(proj, state0, norm_g, nc):
    const = lambda c: (0, 0)
    specs = [pl.BlockSpec((STACK, STACK), const), pl.BlockSpec((1, HEAD_DIM), const)]
    return _recurrence_call(_hgrn_kernel, "hgrn", proj, state0, (_hgrn_level_table(), norm_g), specs, nc)


def _gdn(proj, state0, scal, a_log_row, dt_row, norm_g, nc):
    nb = proj.shape[0]
    const = lambda c: (0, 0)
    const3 = lambda c: (0, 0, 0)
    specs = [pl.BlockSpec((STACK, STACK), const),
             pl.BlockSpec((N_STACKS, 1, STACK), const3),
             pl.BlockSpec((N_STACKS, 1, STACK), const3),
             pl.BlockSpec((1, HEAD_DIM), const),
             pl.BlockSpec((nb, nc, N_STACKS, 2, STACK), lambda c: (0, c, 0, 0, 0))]
    return _recurrence_call(_gdn_kernel, "gdn", proj, state0,
                            (_gdn_cumsum_matrix(), a_log_row, dt_row, norm_g, scal), specs, nc)


ROUTE_LANE0 = N_GROUPS
ROUTE_NEG = -1e30


def _route_tile(lg, ltri, carry):
    lane = lax.broadcasted_iota(jnp.int32, lg.shape, 1).astype(F32)
    first = lambda mask: jnp.min(jnp.where(mask, lane, float(LANES)), axis=1, keepdims=True)
    top = lambda mask: jnp.max(jnp.where(mask, lg, ROUTE_NEG), axis=1, keepdims=True)

    is_g = lane < N_GROUPS
    gmax = top(is_g)
    grp = first(is_g & (lg == gmax))
    p_grp = 1.0 / jnp.sum(jnp.where(is_g, jnp.exp(lg - gmax), 0.0), axis=1, keepdims=True)

    lo = ROUTE_LANE0 + grp * EXPERTS_PER_GROUP
    in_grp = (lane >= lo) & (lane < lo + EXPERTS_PER_GROUP)
    m1 = top(in_grp)
    i1 = first(in_grp & (lg == m1))
    rest = in_grp & (lane != i1)
    m2 = top(rest)
    i2 = first(rest & (lg == m2))
    r = jnp.exp(m2 - m1)
    w0 = p_grp / (1.0 + r)
    w1 = p_grp * r / (1.0 + r)

    hot0 = (lane == i1).astype(F32)
    hot1 = (lane == i2).astype(F32)
    both = hot0 + hot1
    before = jnp.dot(ltri, both.astype(BF16), preferred_element_type=F32) + carry
    rank0 = jnp.sum(hot0 * before, axis=1, keepdims=True)
    rank1 = jnp.sum(hot1 * before, axis=1, keepdims=True)
    cols = (w0, w1, i1 - ROUTE_LANE0, i2 - ROUTE_LANE0, rank0, rank1)
    info = jnp.zeros_like(lg)
    for k, col in enumerate(cols):
        info = jnp.where(lane == k, col, info)
    return info, carry + jnp.sum(both, axis=0, keepdims=True)


def _merge_kernel(oa_ref, ob_ref, x_ref, xprev_ref, mg_ref, wgate_ref, hgup_ref, gdup_ref, wout_ref,
                  ng_ref, rw_ref, rb_ref, ltri_ref, h2_ref, xn_ref, info_ref, ids_ref, cnt_ref,
                  wgate_bf, gates_ref, ua_ref, ub_ref, carry_ref):
    i = pl.program_id(0)

    @pl.when(i == 0)
    def _():
        carry_ref[...] = jnp.zeros_like(carry_ref)
        _cast_rows(wgate_ref, wgate_bf)
        _zero_rows(gates_ref)
        _zero_rows(ua_ref)
        _zero_rows(ub_ref)

    gates = _sigmoid(gates_ref[...])
    merged = (gates[:, :D_MODEL] * ua_ref[...] + gates[:, D_MODEL:] * ub_ref[...]).astype(BF16)
    gates_ref[...] = jnp.dot(_normed_bf16(x_ref, mg_ref), wgate_bf[...], preferred_element_type=F32)
    h2 = xprev_ref[...] + jnp.dot(merged, wout_ref[...], preferred_element_type=F32)
    h2_ref[...] = h2
    xn = h2 * lax.rsqrt(jnp.mean(h2 * h2, axis=-1, keepdims=True) + EPS) * ng_ref[...]
    xn_ref[...] = _pack_halves(xn)
    lg = jnp.dot(xn.astype(BF16), rw_ref[...], preferred_element_type=F32) + rb_ref[...]
    ua_ref[...] = jnp.dot(oa_ref[...], hgup_ref[...], preferred_element_type=F32)
    ub_ref[...] = jnp.dot(ob_ref[...], gdup_ref[...], preferred_element_type=F32)
    old = carry_ref[...]
    info, carry = _route_tile(lg, ltri_ref[...], old)
    info_ref[...] = info
    ids_ref[...] = info.T[:SUBLANES].astype(jnp.int32)
    carry = jnp.where(i > 0, carry, old)
    carry_ref[...] = carry
    cnt_ref[...] = carry


def _merge(o_a, o_b, x2d, mix_g, w_gates, hg_up, gd_up, w_out, norm_g, rw, rb):
    t = x2d.shape[0]
    tm = TOKEN_TM
    tiles = t // tm
    new = lambda i: (jnp.minimum(i, tiles - 1), 0)
    row = lambda i: (jnp.maximum(i - 1, 0), 0)
    const = lambda i: (0, 0)
    ltri = jnp.asarray(np.tril(np.ones((tm, tm), np.float32), -1), dtype=BF16)
    return pl.pallas_call(
        _merge_kernel,
        grid=(tiles + 1,),
        in_specs=[pl.BlockSpec((tm, WIDTH), new),
                  pl.BlockSpec((tm, WIDTH), new),
                  pl.BlockSpec((tm, D_MODEL), new),
                  pl.BlockSpec((tm, D_MODEL), row),
                  pl.BlockSpec((1, D_MODEL), const),
                  pl.BlockSpec((D_MODEL, 2 * D_MODEL), const, pipeline_mode=pl.Buffered(1)),
                  pl.BlockSpec((WIDTH, D_MODEL), const),
                  pl.BlockSpec((WIDTH, D_MODEL), const),
                  pl.BlockSpec((D_MODEL, D_MODEL), const),
                  pl.BlockSpec((1, D_MODEL), const),
                  pl.BlockSpec((D_MODEL, LANES), const),
                  pl.BlockSpec((1, LANES), const),
                  pl.BlockSpec((tm, tm), const)],
        out_specs=[pl.BlockSpec((tm, D_MODEL), row),
                   pl.BlockSpec((tm, HALF), row),
                   pl.BlockSpec((tm, LANES), row),
                   pl.BlockSpec((SUBLANES, tm), lambda i: (0, jnp.maximum(i - 1, 0))),
                   pl.BlockSpec((1, LANES), const)],
        out_shape=[jax.ShapeDtypeStruct((t, D_MODEL), F32),
                   jax.ShapeDtypeStruct((t, HALF), U32),
                   jax.ShapeDtypeStruct((t, LANES), F32),
                   jax.ShapeDtypeStruct((SUBLANES, t), jnp.int32),
                   jax.ShapeDtypeStruct((1, LANES), F32)],
        scratch_shapes=[pltpu.VMEM((D_MODEL, 2 * D_MODEL), BF16), pltpu.VMEM((tm, 2 * D_MODEL), F32),
                        pltpu.VMEM((tm, D_MODEL), F32), pltpu.VMEM((tm, D_MODEL), F32),
                        pltpu.VMEM((1, LANES), F32)],
        compiler_params=pltpu.CompilerParams(
            dimension_semantics=("arbitrary",), vmem_limit_bytes=VMEM_LIMIT),
        name="merge",
    )(o_a, o_b, x2d, x2d, mix_g, w_gates, hg_up, gd_up, w_out, norm_g, rw, rb, ltri)


def _sc_gather(table, idx):
    n_idx = idx.shape[0]
    cols = table.shape[1]
    per_worker = n_idx // SC_WORKERS
    assert n_idx % (SC_WORKERS * SC_CHUNK) == 0
    mesh = plsc.VectorSubcoreMesh(core_axis_name="c", subcore_axis_name="s")

    @functools.partial(
        pl.kernel, mesh=mesh,
        out_type=jax.ShapeDtypeStruct((n_idx, cols), table.dtype),
        scratch_types=[pltpu.VMEM((SC_CHUNK,), jnp.int32),
                       pltpu.VMEM((SC_CHUNK, cols), table.dtype),
                       pltpu.SemaphoreType.DMA],
    )
    def gather(table_hbm, idx_hbm, out_hbm, idx_v, rows_v, sem):
        worker = lax.axis_index("s") * SC_CORES + lax.axis_index("c")
        base = worker * per_worker

        @pl.loop(0, per_worker // SC_CHUNK)
        def _(c):
            off = pl.multiple_of(base + c * SC_CHUNK, SC_CHUNK)
            pltpu.sync_copy(idx_hbm.at[pl.ds(off, SC_CHUNK)], idx_v)
            pltpu.async_copy(table_hbm.at[idx_v], rows_v, sem).wait()
            pltpu.sync_copy(rows_v, out_hbm.at[pl.ds(off, SC_CHUNK)])

    return gather(table, idx)


def _sc_dispatch(rows, dest, n_out):
    t, cols = rows.shape
    per_worker = t // SC_WORKERS
    assert t % (SC_WORKERS * SC_CHUNK) == 0 and dest.shape[0] == 2 * t
    mesh = plsc.VectorSubcoreMesh(core_axis_name="c", subcore_axis_name="s")

    @functools.partial(
        pl.kernel, mesh=mesh,
        out_type=jax.ShapeDtypeStruct((n_out, cols), rows.dtype),
        scratch_types=[pltpu.VMEM((SC_CHUNK,), jnp.int32),
                       pltpu.VMEM((SC_CHUNK, cols), rows.dtype)],
    )
    def dispatch(rows_hbm, dest_hbm, out_hbm, idx_v, rows_v):
        worker = lax.axis_index("s") * SC_CORES + lax.axis_index("c")
        base = worker * per_worker

        @pl.loop(0, per_worker // SC_CHUNK)
        def _(c):
            off = pl.multiple_of(base + c * SC_CHUNK, SC_CHUNK)
            pltpu.sync_copy(rows_hbm.at[pl.ds(off, SC_CHUNK)], rows_v)
            for slot in range(2):
                pltpu.sync_copy(dest_hbm.at[pl.ds(slot * t + off, SC_CHUNK)], idx_v)
                pltpu.sync_copy(rows_v, out_hbm.at[idx_v])

    return dispatch(rows, dest)


def _moe_kernel(be_ref, nu_ref, x_ref, wg_ref, wu_ref, wd_ref, y_ref, wg_bf, wu_bf, wd_bf):
    i = pl.program_id(0)

    @pl.when((i == 0) | (be_ref[i] != be_ref[jnp.maximum(i - 1, 0)]))
    def _():
        _cast_rows(wg_ref, wg_bf)
        _cast_rows(wu_ref, wu_bf)
        _cast_rows(wd_ref, wd_bf)

    @pl.when(i < nu_ref[0])
    def _():
        xb = _unpack_halves(x_ref[...]).astype(BF16)
        a = jnp.dot(xb, wg_bf[...], preferred_element_type=F32)
        u = jnp.dot(xb, wu_bf[...], preferred_element_type=F32)
        y = jnp.dot((_silu(a) * u).astype(BF16), wd_bf[...], preferred_element_type=F32)
        y_ref[...] = _pack_halves(y)

    @pl.when(i >= nu_ref[0])
    def _():
        y_ref[...] = jnp.zeros_like(y_ref)


def _moe(blk_expert, n_used, x_sorted, w_gate, w_up, w_down):
    n_blocks = blk_expert.shape[0]
    wspec = lambda shape: pl.BlockSpec((None,) + shape, lambda i, be, nu: (be[i], 0, 0))
    rows = pl.BlockSpec((MOE_ROWS, HALF), lambda i, be, nu: (i, 0))
    grid_spec = pltpu.PrefetchScalarGridSpec(
        num_scalar_prefetch=2,
        grid=(n_blocks,),
        in_specs=[rows, wspec((D_MODEL, D_FF)), wspec((D_MODEL, D_FF)), wspec((D_FF, D_MODEL))],
        out_specs=rows,
        scratch_shapes=[pltpu.VMEM((D_MODEL, D_FF), BF16), pltpu.VMEM((D_MODEL, D_FF), BF16),
                        pltpu.VMEM((D_FF, D_MODEL), BF16)],
    )
    return pl.pallas_call(
        _moe_kernel,
        grid_spec=grid_spec,
        out_shape=jax.ShapeDtypeStruct(x_sorted.shape, U32),
        compiler_params=pltpu.CompilerParams(
            dimension_semantics=("arbitrary",), vmem_limit_bytes=VMEM_LIMIT),
        name="moe",
    )(blk_expert, n_used, x_sorted, w_gate, w_up, w_down)


def _combine_kernel(h2_ref, rw_ref, g_ref, y0_ref, y1_ref, o_ref):
    rw = rw_ref[...]
    h = h2_ref[...] + rw[:, 0:1] * _unpack_halves(y0_ref[...]) + rw[:, 1:2] * _unpack_halves(y1_ref[...])
    o_ref[...] = h * lax.rsqrt(jnp.mean(h * h, axis=-1, keepdims=True) + EPS) * g_ref[...]


def _combine(h2, rweights, final_g, y):
    t = h2.shape[0]
    tm = min(COMBINE_TM, t)
    row = lambda i: (i, 0)
    return pl.pallas_call(
        _combine_kernel,
        grid=(t // tm,),
        in_specs=[pl.BlockSpec((tm, D_MODEL), row),
                  pl.BlockSpec((tm, LANES), row),
                  pl.BlockSpec((1, D_MODEL), lambda i: (0, 0)),
                  pl.BlockSpec((tm, HALF), row),
                  pl.BlockSpec((tm, HALF), lambda i: (i + t // tm, 0))],
        out_specs=pl.BlockSpec((tm, D_MODEL), row),
        out_shape=jax.ShapeDtypeStruct((t, D_MODEL), F32),
        compiler_params=pltpu.CompilerParams(
            dimension_semantics=("arbitrary",), vmem_limit_bytes=VMEM_LIMIT),
        name="combine",
    )(h2, rweights, final_g, y, y)


def _block_layout(ids, counts_row, t):
    n_blocks = 2 * t // MOE_ROWS + N_EXPERTS
    counts = counts_row[0, ROUTE_LANE0:ROUTE_LANE0 + N_EXPERTS].astype(jnp.int32)
    padded = ((counts + MOE_ROWS - 1) // MOE_ROWS) * MOE_ROWS
    pend = jnp.cumsum(padded)
    pstart = pend - padded
    blk_first = jnp.arange(n_blocks, dtype=jnp.int32) * MOE_ROWS
    blk_expert = jnp.minimum(jnp.sum((pend[None, :] <= blk_first[:, None]).astype(jnp.int32), axis=1),
                             N_EXPERTS - 1).astype(jnp.int32)
    n_used = (pend[-1] // MOE_ROWS).astype(jnp.int32).reshape(1)
    expert, rank = ids[2:4], ids[4:6]
    experts = jnp.arange(N_EXPERTS, dtype=jnp.int32)
    first_row = jnp.sum(jnp.where(expert[:, :, None] == experts, pstart, 0), axis=-1)
    dest = (first_row + rank).reshape(2 * t)
    return dest, blk_expert, n_used, n_blocks * MOE_ROWS


def kernel(x, meta_tokens, hg_lb_logits, norm_mix_g, w_in, gd_conv_w, gd_A_log, gd_dt_bias, hg_norm_g, gd_norm_g, hg_up, gd_up, w_out, norm_ffn_g, router_group_w, router_group_b, router_expert_w, router_expert_b, w_gate, w_up, w_down, final_norm_g):
    bsz, seq, d = x.shape
    t = bsz * seq
    x2d = x.reshape(t, d)

    lb = jnp.cumsum(jax.nn.softmax(hg_lb_logits.astype(F32), axis=0), axis=0)[0].reshape(1, WIDTH)
    w_all = w_in.astype(F32)
    w_gates = w_all[0, :, W_SMALL + 2 * HEADS:]
    g_mix = norm_mix_g[0].reshape(1, d)
    meta_blk = jnp.concatenate([jnp.zeros((CHUNK - N_META, d), F32), meta_tokens.astype(F32)], axis=0)

    conv_w = gd_conv_w[0].astype(F32)
    no_history = jnp.zeros((SUBLANES, 3 * WIDTH), F32)
    proj, proj_meta = {}, {}
    proj["hgrn"] = _norm_proj_hgrn(x2d, g_mix, lb, w_all, PROJ_TM)
    proj_meta["hgrn"] = _norm_proj_hgrn(meta_blk, g_mix, lb, w_all, CHUNK)
    proj_meta["gdn"], meta_tail, logits_meta = _norm_proj_conv(meta_blk, g_mix, w_all, conv_w, no_history,
                                                               CHUNK, CHUNK)
    proj["gdn"], _, logits = _norm_proj_conv(x2d, g_mix, w_all, conv_w, meta_tail, PROJ_TM, seq)

    def scalar_rows(lg, nb, nc):
        s = lg.reshape(2, N_STACKS, HEADS_PER_STACK, nb, nc, CHUNK)
        return s.transpose(3, 4, 1, 0, 2, 5).reshape(nb, nc, N_STACKS, 2, STACK)

    a_log_row = jnp.repeat(gd_A_log[0].astype(F32), CHUNK).reshape(N_STACKS, 1, STACK)
    dt_row = jnp.repeat(gd_dt_bias[0].astype(F32), CHUNK).reshape(N_STACKS, 1, STACK)
    hg_g, gd_g = hg_norm_g[0].reshape(1, HEAD_DIM), gd_norm_g[0].reshape(1, HEAD_DIM)

    zero_state = jnp.zeros((HEADS, HEAD_DIM, HEAD_DIM), F32)
    per_seq = lambda st: jnp.tile(st, (bsz, 1, 1))
    _, hg_state = _hgrn(proj_meta["hgrn"][None], zero_state, hg_g, 1)
    o_a, _ = _hgrn(proj["hgrn"].reshape(bsz, seq, -1), per_seq(hg_state), hg_g, HGRN_CHUNKS)
    _, gd_state = _gdn(proj_meta["gdn"][None], zero_state, scalar_rows(logits_meta, 1, 1),
                       a_log_row, dt_row, gd_g, 1)
    o_b, _ = _gdn(proj["gdn"].reshape(bsz, seq, -1), per_seq(gd_state),
                  scalar_rows(logits, bsz, seq // CHUNK), a_log_row, dt_row, gd_g, GDN_CHUNKS)

    rw = jnp.zeros((d, LANES), F32)
    rw = rw.at[:, :N_GROUPS].set(router_group_w[0]).at[:, N_GROUPS:N_GROUPS + N_EXPERTS].set(router_expert_w[0])
    rb = jnp.zeros((1, LANES), F32)
    rb = rb.at[0, :N_GROUPS].set(router_group_b[0]).at[0, N_GROUPS:N_GROUPS + N_EXPERTS].set(router_expert_b[0])
    h2, xn, info, ids, counts = _merge(o_a.reshape(t, WIDTH), o_b.reshape(t, WIDTH), x2d, g_mix, w_gates,
                                  hg_up[0].astype(BF16), gd_up[0].astype(BF16), w_out[0].astype(BF16),
                                  norm_ffn_g[0].reshape(1, d), rw.astype(BF16), rb)

    dest, blk_expert, n_used, n_rows = _block_layout(ids, counts, t)
    x_sorted = _sc_dispatch(xn, dest, n_rows)
    y_sorted = _moe(blk_expert, n_used, x_sorted,
                    w_gate[0].astype(F32), w_up[0].astype(F32), w_down[0].astype(F32))
    y_tok = _sc_gather(y_sorted, dest)
    out = _combine(h2, info, final_norm_g.reshape(1, d), y_tok)
    return out.reshape(bsz, seq, d)
```

```python
import functools

import numpy as np
import jax
import jax.numpy as jnp
from jax import lax
from jax.experimental import pallas as pl
from jax.experimental.pallas import tpu as pltpu
from jax.experimental.pallas import tpu_sc as plsc

F32 = jnp.float32
BF16 = jnp.bfloat16
U32 = jnp.uint32

D_MODEL = 1024
N_META = 16
CHUNK_BITS = 6
CHUNK = 1 << CHUNK_BITS
EPS = 1e-6
LOG2_E = 1.4426950408889634
HEADS = 4
HEAD_DIM = 128
WIDTH = HEADS * HEAD_DIM
HEADS_PER_STACK = 2
STACK = HEADS_PER_STACK * CHUNK
N_STACKS = HEADS // HEADS_PER_STACK
CONV_W = 4
NEUMANN_BITS = 3
N_GROUPS = 4
EXPERTS_PER_GROUP = 8
N_EXPERTS = N_GROUPS * EXPERTS_PER_GROUP
D_FF = 512
LANES = 128
SUBLANES = 8
HALF = D_MODEL // 2
MOE_ROWS = 512
CAST_SLAB = 128
SC_CORES, SC_SUBCORES = 2, 16
SC_WORKERS = SC_CORES * SC_SUBCORES
SC_CHUNK = 128
PROJ_TM = 1024
W_HGRN, W_GDN, W_SMALL = 0, 4 * WIDTH, 8 * WIDTH
HGRN_CHUNKS, GDN_CHUNKS = 4, 2
TOKEN_TM = 512
COMBINE_TM = 1024
VMEM_LIMIT = 56 * 1024 * 1024


def _bdot(a, b):
    return lax.dot_general(a.astype(BF16), b.astype(BF16), (((2,), (1,)), ((0,), (0,))),
                           preferred_element_type=F32)


def _bdot_nt(a, b):
    return lax.dot_general(a.astype(BF16), b.astype(BF16), (((2,), (2,)), ((0,), (0,))),
                           preferred_element_type=F32)


def _bdot_tn(a, b):
    return lax.dot_general(a.astype(BF16), b.astype(BF16), (((1,), (1,)), ((0,), (0,))),
                           preferred_element_type=F32)


def _split3(x):
    hi = x.astype(BF16)
    r1 = x - hi.astype(F32)
    mid = r1.astype(BF16)
    lo = (r1 - mid.astype(F32)).astype(BF16)
    return hi, mid, lo


def _sigmoid(x):
    return 1.0 / (1.0 + jnp.exp(-x))


def _silu(x):
    return x * _sigmoid(x)


def _softplus(x):
    return jnp.maximum(x, 0.0) + jnp.log(1.0 + jnp.exp(-jnp.abs(x)))


def _stack_heads(x, bsz):
    x3 = x.reshape(bsz, CHUNK, WIDTH)
    y = jnp.concatenate([x3[:, :, h * HEAD_DIM:(h + 1) * HEAD_DIM] for h in range(HEADS)], axis=1)
    return y.reshape(bsz * HEADS * CHUNK, HEAD_DIM)


def _unstack_heads(x, bsz):
    x3 = x.reshape(bsz, HEADS * CHUNK, HEAD_DIM)
    return jnp.concatenate([x3[:, h * CHUNK:(h + 1) * CHUNK, :] for h in range(HEADS)], axis=2)


def _to_stacks(a, bsz):
    return a.reshape(bsz * N_STACKS, STACK, a.shape[-1])


def _to_heads(a, bsz):
    return a.reshape(bsz * HEADS, CHUNK, a.shape[-1])


def _roll_rows(x, shift):
    n = x.shape[0]
    return pltpu.roll(x, shift % n, axis=0)


def _chunks_first(x, nc):
    return jnp.concatenate([x[:, j * CHUNK:(j + 1) * CHUNK] for j in range(nc)], axis=0)


def _chunks_first_scalars(s, nc):
    return jnp.concatenate([s[:, j] for j in range(nc)], axis=0)


def _load_chunks(ref, nc):
    x = _chunks_first(ref[...], nc)
    return x.reshape(x.shape[0] * CHUNK, WIDTH)


def _store_chunks(ref, x, nb, nc):
    ref[...] = jnp.concatenate([x[j * nb:(j + 1) * nb] for j in range(nc)], axis=1).astype(ref.dtype)


def _cast_rows(src_ref, dst_ref, col0=0):
    slab = CAST_SLAB
    rows, cols = src_ref.shape
    assert rows % slab == 0

    def body(k, carry):
        r = pl.multiple_of(k * slab, slab)
        dst_ref[pl.ds(r, slab), col0:col0 + cols] = src_ref[pl.ds(r, slab), :].astype(dst_ref.dtype)
        return carry

    lax.fori_loop(0, rows // slab, body, 0)


def _pack_halves(x):
    lo = lax.bitcast_convert_type(x[:, :HALF].astype(BF16).astype(F32), U32)
    hi = lax.bitcast_convert_type(x[:, HALF:].astype(BF16).astype(F32), U32)
    return (lo >> 16) | (hi & jnp.uint32(0xFFFF0000))


def _unpack_halves(p):
    lo = lax.bitcast_convert_type(p << 16, F32)
    hi = lax.bitcast_convert_type(p & jnp.uint32(0xFFFF0000), F32)
    return jnp.concatenate([lo, hi], axis=1)


def _normed_bf16(x_ref, g_ref):
    x = x_ref[...]
    return (x * lax.rsqrt(jnp.mean(x * x, axis=-1, keepdims=True) + EPS) * g_ref[...]).astype(BF16)


def _norm_proj_hgrn_kernel(x_ref, g_ref, lb_ref, w_ref, o_ref, wbf_ref):
    @pl.when(pl.program_id(0) == 0)
    def _():
        _cast_rows(w_ref, wbf_ref)

    acc = jnp.dot(_normed_bf16(x_ref, g_ref), wbf_ref[...], preferred_element_type=F32)
    w1, w2, w3 = WIDTH, 2 * WIDTH, 3 * WIDTH
    lb = lb_ref[...]
    o_ref[:, :w1] = _silu(acc[:, :w1])
    o_ref[:, w1:w2] = lb + (1.0 - lb) * _sigmoid(acc[:, w1:w2])
    o_ref[:, w2:w3] = acc[:, w2:w3]
    o_ref[:, w3:] = _silu(acc[:, w3:])


def _zero_rows(ref):
    slab = min(CAST_SLAB, ref.shape[0])
    assert ref.shape[0] % slab == 0

    def body(k, carry):
        ref[pl.ds(pl.multiple_of(k * slab, slab), slab), :] = jnp.zeros((slab, ref.shape[1]), ref.dtype)
        return carry

    lax.fori_loop(0, ref.shape[0] // slab, body, 0)


def _norm_proj_conv_kernel(x_ref, g_ref, w_ref, ws_ref, cw_ref, hist_ref, o_ref, tail_ref, logit_ref,
                           wbf_ref, acc_ref, carry_ref, *, steps_per_seq):
    i = pl.program_id(0)

    @pl.when(i == 0)
    def _():
        _cast_rows(w_ref, wbf_ref)
        _cast_rows(ws_ref, wbf_ref, 4 * WIDTH)
        _zero_rows(acc_ref)
        carry_ref[...] = jnp.zeros_like(carry_ref)

    xn = _normed_bf16(x_ref, g_ref)
    w3, w4 = 3 * WIDTH, 4 * WIDTH
    rows = x_ref.shape[0]
    first_of_seq = (i - 1) % steps_per_seq == 0
    cw = cw_ref[...]
    r8 = lax.broadcasted_iota(jnp.int32, (SUBLANES, WIDTH), 0)
    for c0 in range(0, w3, WIDTH):
        cols = slice(c0, c0 + WIDTH)
        raw = acc_ref[:, cols]
        hist = jnp.where(first_of_seq, hist_ref[:, cols], carry_ref[:, cols])
        tap = lambda d: cw[CONV_W - 1 - d:CONV_W - d, cols]
        body = raw * tap(0)
        head = raw[:SUBLANES]
        first = head * tap(0)
        for d in range(1, CONV_W):
            body = body + _roll_rows(raw, d) * tap(d)
            first = first + jnp.where(r8 >= d, _roll_rows(head, d), _roll_rows(hist, d)) * tap(d)
        o_ref[:SUBLANES, cols] = _silu(first)
        o_ref[SUBLANES:, cols] = _silu(body[SUBLANES:])
        carry_ref[:, cols] = raw[rows - SUBLANES:]
        acc_ref[:, cols] = jnp.dot(xn, wbf_ref[:, cols], preferred_element_type=F32)
    rest = acc_ref[:, w3:]
    o_ref[:, w3:] = _silu(rest[:, :WIDTH])
    logit_ref[...] = rest[:, WIDTH:].T[:SUBLANES]
    acc_ref[:, w3:] = jnp.dot(xn, wbf_ref[:, w3:], preferred_element_type=F32)
    tail_ref[...] = carry_ref[...]


def _proj_call(kernel_fn, name, x2d, tm, n, operands, specs, extra_out=(), extra_scratch=(),
               weight_cols=None, delayed=False):
    m, k = x2d.shape
    weight_cols = n if weight_cols is None else weight_cols
    tm = min(tm, m)
    tiles = m // tm
    x_map = (lambda i: (jnp.minimum(i, tiles - 1), 0)) if delayed else (lambda i: (i, 0))
    o_map = (lambda i: (jnp.maximum(i - 1, 0), 0)) if delayed else (lambda i: (i, 0))
    out_specs = [pl.BlockSpec((tm, n), o_map)] + [s for s, _ in extra_out]
    out_shape = [jax.ShapeDtypeStruct((m, n), F32)] + [s for _, s in extra_out]
    return pl.pallas_call(
        kernel_fn,
        grid=(tiles + int(delayed),),
        in_specs=[pl.BlockSpec((tm, k), x_map)] + specs,
        out_specs=out_specs,
        out_shape=out_shape,
        scratch_shapes=[pltpu.VMEM((k, weight_cols), BF16)] + list(extra_scratch),
        compiler_params=pltpu.CompilerParams(
            dimension_semantics=("arbitrary",), vmem_limit_bytes=VMEM_LIMIT),
        name=name,
    )(x2d, *operands)


def _weight_spec(k, n, col0):
    assert col0 % n == 0
    return pl.BlockSpec((None, k, n), lambda i: (0, 0, col0 // n), pipeline_mode=pl.Buffered(1))


def _norm_proj_hgrn(x2d, g, lb, w, tm):
    k, n = w.shape[1], 4 * WIDTH
    const = lambda i: (0, 0)
    specs = [pl.BlockSpec((1, k), const), pl.BlockSpec((1, WIDTH), const), _weight_spec(k, n, W_HGRN)]
    return _proj_call(_norm_proj_hgrn_kernel, "norm_proj_hgrn", x2d, tm, n, (g, lb, w), specs)[0]


def _norm_proj_conv(x2d, g, w, conv_w, hist, tm, rows_per_seq):
    m, k = x2d.shape[0], w.shape[1]
    n_out, n_w = 4 * WIDTH, 4 * WIDTH + LANES
    const = lambda i: (0, 0)
    tm = min(tm, rows_per_seq)
    assert rows_per_seq % tm == 0 and m % rows_per_seq == 0
    tail_shape = (SUBLANES, 3 * WIDTH)
    specs = [pl.BlockSpec((1, k), const), _weight_spec(k, 4 * WIDTH, W_GDN), _weight_spec(k, LANES, W_SMALL),
             pl.BlockSpec((CONV_W, 3 * WIDTH), const), pl.BlockSpec(tail_shape, const)]
    logit_spec = pl.BlockSpec((SUBLANES, tm), lambda i: (0, jnp.maximum(i - 1, 0)))
    return _proj_call(functools.partial(_norm_proj_conv_kernel, steps_per_seq=rows_per_seq // tm),
                      "norm_proj_gdn", x2d, tm, n_out, (g, w, w, conv_w, hist), specs,
                      extra_out=[(pl.BlockSpec(tail_shape, const), jax.ShapeDtypeStruct(tail_shape, F32)),
                                 (logit_spec, jax.ShapeDtypeStruct((SUBLANES, m), F32))],
                      extra_scratch=[pltpu.VMEM((tm, n_w), F32), pltpu.VMEM(tail_shape, F32)],
                      weight_cols=n_w, delayed=True)


HGRN_LEVELS = 6
HGRN_DIAG = HGRN_LEVELS


def _hgrn_level_table():
    r = np.arange(STACK)
    t, hd = r % CHUNK, r // CHUNK
    x = t[:, None] ^ t[None, :]
    lv = np.floor(np.log2(np.maximum(x, 1))).astype(np.int32)
    valid = (hd[:, None] == hd[None, :]) & (t[:, None] > t[None, :])
    lv = np.where(valid, lv, -1)
    lv = np.where(r[:, None] == r[None, :], HGRN_DIAG, lv)
    return jnp.asarray(lv, dtype=jnp.int32)


def _hgrn_kernel(lv_ref, g_ref, st0_ref, q_ref, f_ref, v_ref, gate_ref, o_ref, stout_ref, st_ref, *, nb, nc):
    @pl.when(pl.program_id(0) == 0)
    def _():
        st_ref[...] = st0_ref[...]

    bsz = nc * nb
    q = _load_chunks(q_ref, nc)
    f = _load_chunks(f_ref, nc)
    v = _load_chunks(v_ref, nc)
    gate = _load_chunks(gate_ref, nc)

    per_seq = lambda a: a.reshape(bsz, CHUNK, WIDTH)
    trow = lax.broadcasted_iota(jnp.int32, (1, CHUNK, WIDTH), 1)
    b = jnp.log2(f)
    s = 1
    while s < CHUNK:
        b = (per_seq(b) + jnp.where(trow >= s, per_seq(_roll_rows(b, s)), 0.0)).reshape(b.shape)
        s *= 2

    qs, ks, vs, bs = (_stack_heads(a, bsz) for a in (q, 1.0 - f, v, b))
    lv = lv_ref[...][None]
    to3 = functools.partial(_to_stacks, bsz=bsz)
    to_g = functools.partial(_to_heads, bsz=bsz)
    roll_g = lambda a, shift: to_g(_roll_rows(a.reshape(qs.shape), shift))
    t = lax.broadcasted_iota(jnp.int32, (1, CHUNK, HEAD_DIM), 1)
    qs_g, ks_g, bs_g = to_g(qs), to_g(ks), to_g(bs)

    attn = jnp.where(lv == HGRN_DIAG, _bdot_nt(to3(qs), to3(ks)), 0.0)
    bref = bs_g
    bnext = roll_g(bs_g, -1)
    for li in range(HGRN_LEVELS):
        m = 1 << li
        if li > 0:
            half = m // 2
            upper = (t & (m - 1)) >= half
            if half < SUBLANES:
                grp = lambda a: a.reshape(-1, SUBLANES, HEAD_DIM)
                local = lambda a, sh: pltpu.roll(grp(a), sh % SUBLANES, axis=1).reshape(a.shape)
                bref = jnp.where(upper, local(bref, half), bref)
                bnext = jnp.where(upper, bnext, local(bnext, -half))
            else:
                bref = jnp.where(upper, roll_g(bref, half), bref)
                bnext = jnp.where(upper, bnext, roll_g(bnext, -half))
        odd = ((t >> li) & 1) == 1
        both = jnp.where(odd, qs_g, ks_g) * jnp.exp2(jnp.where(odd, bs_g - bref, bnext - bs_g))
        both = to3(both.reshape(qs.shape)).astype(BF16)
        attn = jnp.where(lv == li, _bdot_nt(both, both), attn)

    o = _bdot(attn, to3(vs))

    n_g = nb * HEADS
    st = st_ref[...]
    qe = qs_g * jnp.exp2(bs_g)
    b_end = bs_g[:, CHUNK - 1:CHUNK, :]
    ke = ks_g * jnp.exp2(b_end - bs_g)
    vs_g = to_g(vs)
    o_inter = []
    for j in range(nc):
        sl = slice(j * n_g, (j + 1) * n_g)
        o_inter.append(_bdot_nt(qe[sl], st))
        st = st * jnp.exp2(b_end[sl]) + _bdot_tn(vs_g[sl], ke[sl])
    st_ref[...] = st
    stout_ref[...] = st
    o = (to_g(o) + jnp.concatenate(o_inter, axis=0)).reshape(qs.shape)

    o = o * lax.rsqrt(jnp.mean(o * o, axis=-1, keepdims=True) + EPS) * g_ref[...]
    o = o * _stack_heads(gate, bsz)
    _store_chunks(o_ref, _unstack_heads(o, bsz), nb, nc)


def _gdn_cumsum_matrix():
    r = np.arange(STACK)
    u = (r[:, None] // CHUNK == r[None, :] // CHUNK) & (r[:, None] <= r[None, :])
    return jnp.asarray(u, dtype=BF16)


def _gdn_kernel(u_ref, alog_ref, dt_ref, g_ref, s_ref, st0_ref, q_ref, k_ref, v_ref, z_ref,
                o_ref, stout_ref, st_ref, *, nb, nc):
    @pl.when(pl.program_id(0) == 0)
    def _():
        st_ref[...] = st0_ref[...]

    bsz = nc * nb
    qs = _stack_heads(_load_chunks(q_ref, nc), bsz)
    ks = _stack_heads(_load_chunks(k_ref, nc), bsz)
    vs = _stack_heads(_load_chunks(v_ref, nc), bsz)
    gate = _load_chunks(z_ref, nc)
    qs = qs * lax.rsqrt(jnp.sum(qs * qs, axis=-1, keepdims=True) + EPS) * (HEAD_DIM ** -0.5)
    ks = ks * lax.rsqrt(jnp.sum(ks * ks, axis=-1, keepdims=True) + EPS)
    to3 = functools.partial(_to_stacks, bsz=bsz)
    qs, ks, vs = to3(qs), to3(ks), to3(vs)
    n_st = bsz * N_STACKS

    srow = _chunks_first_scalars(s_ref[...], nc).reshape(n_st, 2, STACK)
    per_stack = lambda ref: jnp.broadcast_to(ref[...][None], (bsz, N_STACKS, 1, STACK)).reshape(n_st, 1, STACK)
    beta_row = _sigmoid(srow[:, 0:1, :])
    g_row = (-LOG2_E) * jnp.exp(per_stack(alog_ref)) * _softplus(srow[:, 1:2, :] + per_stack(dt_ref))
    n8 = n_st * SUBLANES
    g8 = jnp.broadcast_to(g_row, (n_st, SUBLANES, STACK)).reshape(n8, STACK)
    pieces = jnp.concatenate(_split3(g8), axis=0)
    cum = jnp.dot(pieces, u_ref[...], preferred_element_type=F32)
    cum = cum[0:n8] + cum[n8:2 * n8] + cum[2 * n8:]
    cum_row = cum.reshape(n_st, SUBLANES, STACK)[:, 0:1, :]

    r = lax.broadcasted_iota(jnp.int32, (STACK, STACK), 0)
    cc = lax.broadcasted_iota(jnp.int32, (STACK, STACK), 1)
    eye = (r == cc)[None]
    cum_col = jnp.sum(jnp.where(eye, cum_row, 0.0), axis=2, keepdims=True)
    beta_col = jnp.sum(jnp.where(eye, beta_row, 0.0), axis=2, keepdims=True)
    same = (r >> CHUNK_BITS) == (cc >> CHUNK_BITS)
    lower = (same & (r >= cc))[None]
    strict = (same & (r > cc))[None]
    ratio = jnp.exp2(jnp.minimum(cum_col - cum_row, 0.0))

    kkt = _bdot_nt(ks, ks)
    qkt = _bdot_nt(qs, ks)
    lm = jnp.where(strict, beta_col * ratio * kkt, 0.0)

    ident = eye.astype(F32)
    l0 = jnp.where(((r >> NEUMANN_BITS) == (cc >> NEUMANN_BITS))[None], lm, 0.0)
    l2 = _bdot(l0, l0)
    l4 = _bdot(l2, l2)
    xinv = _bdot(_bdot(ident - l0, ident + l2), ident + l4)
    for bits in range(NEUMANN_BITS, CHUNK_BITS):
        off = (((r >> (bits + 1)) == (cc >> (bits + 1))) & ((r >> bits) != (cc >> bits)))[None]
        xinv = xinv - _bdot(_bdot(xinv, jnp.where(off, lm, 0.0)), xinv)

    decay_col = jnp.exp2(cum_col)
    rhs = jnp.concatenate([beta_col * vs, (beta_col * decay_col) * ks], axis=2)
    sol = _bdot(xinv, rhs)

    to_g = functools.partial(_to_heads, bsz=bsz)
    n_g = nb * HEADS
    st = st_ref[...]
    u0_g, w_g, q_g = to_g(sol[:, :, :HEAD_DIM]), to_g(sol[:, :, HEAD_DIM:]), to_g(qs)
    cum_g = to_g(cum_col)
    cum_end = cum_g[:, CHUNK - 1:CHUNK, :]
    kdec = to_g(ks) * jnp.exp2(cum_end - cum_g)
    decay_g = to_g(decay_col)
    us, o_inter = [], []
    for j in range(nc):
        sl = slice(j * n_g, (j + 1) * n_g)
        u_j = u0_g[sl] - _bdot(w_g[sl], st)
        us.append(u_j)
        o_inter.append(decay_g[sl] * _bdot(q_g[sl], st))
        st = jnp.exp2(cum_end[sl]) * st + _bdot_tn(kdec[sl], u_j)
    st_ref[...] = st
    stout_ref[...] = st
    attn = jnp.where(lower, qkt * ratio, 0.0)
    u = jnp.concatenate(us, axis=0).reshape(bsz * HEADS * CHUNK, HEAD_DIM)
    o = jnp.concatenate(o_inter, axis=0) + to_g(_bdot(attn, to3(u)))
    o = o.reshape(bsz * HEADS * CHUNK, HEAD_DIM)

    o = o * lax.rsqrt(jnp.mean(o * o, axis=-1, keepdims=True) + EPS) * g_ref[...]
    o = o * _stack_heads(gate, bsz)
    _store_chunks(o_ref, _unstack_heads(o, bsz), nb, nc)


def _recurrence_call(kernel_fn, name, proj, state0, operands, specs, nc):
    nb, length, _ = proj.shape
    rows = nc * CHUNK
    state_shape = (nb * HEADS, HEAD_DIM, HEAD_DIM)
    state_spec = pl.BlockSpec(state_shape, lambda c: (0, 0, 0))
    chunk_specs = [pl.BlockSpec((nb, rows, WIDTH), functools.partial(lambda c, j: (0, c, j), j=j))
                   for j in range(4)]
    return pl.pallas_call(
        functools.partial(kernel_fn, nb=nb, nc=nc),
        grid=(length // rows,),
        in_specs=specs + [state_spec] + chunk_specs,
        out_specs=[pl.BlockSpec((nb, rows, WIDTH), lambda c: (0, c, 0)), state_spec],
        out_shape=[jax.ShapeDtypeStruct((nb, length, WIDTH), BF16),
                   jax.ShapeDtypeStruct(state_shape, F32)],
        scratch_shapes=[pltpu.VMEM(state_shape, F32)],
        compiler_params=pltpu.CompilerParams(
            dimension_semantics=("arbitrary",), vmem_limit_bytes=VMEM_LIMIT),
        name=name,
    )(*operands, state0, *([proj] * 4))


def _hgrn(proj, state0, norm_g, nc):
    const = lambda c: (0, 0)
    specs = [pl.BlockSpec((STACK, STACK), const), pl.BlockSpec((1, HEAD_DIM), const)]
    return _recurrence_call(_hgrn_kernel, "hgrn", proj, state0, (_hgrn_level_table(), norm_g), specs, nc)


def _gdn(proj, state0, scal, a_log_row, dt_row, norm_g, nc):
    nb = proj.shape[0]
    const = lambda c: (0, 0)
    const3 = lambda c: (0, 0, 0)
    specs = [pl.BlockSpec((STACK, STACK), const),
             pl.BlockSpec((N_STACKS, 1, STACK), const3),
             pl.BlockSpec((N_STACKS, 1, STACK), const3),
             pl.BlockSpec((1, HEAD_DIM), const),
             pl.BlockSpec((nb, nc, N_STACKS, 2, STACK), lambda c: (0, c, 0, 0, 0))]
    return _recurrence_call(_gdn_kernel, "gdn", proj, state0,
                            (_gdn_cumsum_matrix(), a_log_row, dt_row, norm_g, scal), specs, nc)


ROUTE_LANE0 = N_GROUPS
ROUTE_NEG = -1e30


def _route_tile(lg, ltri, carry):
    lane = lax.broadcasted_iota(jnp.int32, lg.shape, 1).astype(F32)
    first = lambda mask: jnp.min(jnp.where(mask, lane, float(LANES)), axis=1, keepdims=True)
    top = lambda mask: jnp.max(jnp.where(mask, lg, ROUTE_NEG), axis=1, keepdims=True)

    is_g = lane < N_GROUPS
    gmax = top(is_g)
    grp = first(is_g & (lg == gmax))
    p_grp = 1.0 / jnp.sum(jnp.where(is_g, jnp.exp(lg - gmax), 0.0), axis=1, keepdims=True)

    lo = ROUTE_LANE0 + grp * EXPERTS_PER_GROUP
    in_grp = (lane >= lo) & (lane < lo + EXPERTS_PER_GROUP)
    m1 = top(in_grp)
    i1 = first(in_grp & (lg == m1))
    rest = in_grp & (lane != i1)
    m2 = top(rest)
    i2 = first(rest & (lg == m2))
    r = jnp.exp(m2 - m1)
    w0 = p_grp / (1.0 + r)
    w1 = p_grp * r / (1.0 + r)

    hot0 = (lane == i1).astype(F32)
    hot1 = (lane == i2).astype(F32)
    both = hot0 + hot1
    before = jnp.dot(ltri, both.astype(BF16), preferred_element_type=F32) + carry
    rank0 = jnp.sum(hot0 * before, axis=1, keepdims=True)
    rank1 = jnp.sum(hot1 * before, axis=1, keepdims=True)
    cols = (w0, w1, i1 - ROUTE_LANE0, i2 - ROUTE_LANE0, rank0, rank1)
    info = jnp.zeros_like(lg)
    for k, col in enumerate(cols):
        info = jnp.where(lane == k, col, info)
    return info, carry + jnp.sum(both, axis=0, keepdims=True)


def _merge_kernel(oa_ref, ob_ref, x_ref, xprev_ref, mg_ref, wgate_ref, hgup_ref, gdup_ref, wout_ref,
                  ng_ref, rw_ref, rb_ref, ltri_ref, h2_ref, xn_ref, info_ref, ids_ref, cnt_ref,
                  wgate_bf, gates_ref, ua_ref, ub_ref, carry_ref):
    i = pl.program_id(0)

    @pl.when(i == 0)
    def _():
        carry_ref[...] = jnp.zeros_like(carry_ref)
        _cast_rows(wgate_ref, wgate_bf)
        _zero_rows(gates_ref)
        _zero_rows(ua_ref)
        _zero_rows(ub_ref)

    gates = _sigmoid(gates_ref[...])
    merged = (gates[:, :D_MODEL] * ua_ref[...] + gates[:, D_MODEL:] * ub_ref[...]).astype(BF16)
    gates_ref[...] = jnp.dot(_normed_bf16(x_ref, mg_ref), wgate_bf[...], preferred_element_type=F32)
    h2 = xprev_ref[...] + jnp.dot(merged, wout_ref[...], preferred_element_type=F32)
    h2_ref[...] = h2
    xn = h2 * lax.rsqrt(jnp.mean(h2 * h2, axis=-1, keepdims=True) + EPS) * ng_ref[...]
    xn_ref[...] = _pack_halves(xn)
    lg = jnp.dot(xn.astype(BF16), rw_ref[...], preferred_element_type=F32) + rb_ref[...]
    ua_ref[...] = jnp.dot(oa_ref[...], hgup_ref[...], preferred_element_type=F32)
    ub_ref[...] = jnp.dot(ob_ref[...], gdup_ref[...], preferred_element_type=F32)
    old = carry_ref[...]
    info, carry = _route_tile(lg, ltri_ref[...], old)
    info_ref[...] = info
    ids_ref[...] = info.T[:SUBLANES].astype(jnp.int32)
    carry = jnp.where(i > 0, carry, old)
    carry_ref[...] = carry
    cnt_ref[...] = carry


def _merge(o_a, o_b, x2d, mix_g, w_gates, hg_up, gd_up, w_out, norm_g, rw, rb):
    t = x2d.shape[0]
    tm = TOKEN_TM
    tiles = t // tm
    new = lambda i: (jnp.minimum(i, tiles - 1), 0)
    row = lambda i: (jnp.maximum(i - 1, 0), 0)
    const = lambda i: (0, 0)
    ltri = jnp.asarray(np.tril(np.ones((tm, tm), np.float32), -1), dtype=BF16)
    return pl.pallas_call(
        _merge_kernel,
        grid=(tiles + 1,),
        in_specs=[pl.BlockSpec((tm, WIDTH), new),
                  pl.BlockSpec((tm, WIDTH), new),
                  pl.BlockSpec((tm, D_MODEL), new),
                  pl.BlockSpec((tm, D_MODEL), row),
                  pl.BlockSpec((1, D_MODEL), const),
                  pl.BlockSpec((D_MODEL, 2 * D_MODEL), const, pipeline_mode=pl.Buffered(1)),
                  pl.BlockSpec((WIDTH, D_MODEL), const),
                  pl.BlockSpec((WIDTH, D_MODEL), const),
                  pl.BlockSpec((D_MODEL, D_MODEL), const),
                  pl.BlockSpec((1, D_MODEL), const),
                  pl.BlockSpec((D_MODEL, LANES), const),
                  pl.BlockSpec((1, LANES), const),
                  pl.BlockSpec((tm, tm), const)],
        out_specs=[pl.BlockSpec((tm, D_MODEL), row),
                   pl.BlockSpec((tm, HALF), row),
                   pl.BlockSpec((tm, LANES), row),
                   pl.BlockSpec((SUBLANES, tm), lambda i: (0, jnp.maximum(i - 1, 0))),
                   pl.BlockSpec((1, LANES), const)],
        out_shape=[jax.ShapeDtypeStruct((t, D_MODEL), F32),
                   jax.ShapeDtypeStruct((t, HALF), U32),
                   jax.ShapeDtypeStruct((t, LANES), F32),
                   jax.ShapeDtypeStruct((SUBLANES, t), jnp.int32),
                   jax.ShapeDtypeStruct((1, LANES), F32)],
        scratch_shapes=[pltpu.VMEM((D_MODEL, 2 * D_MODEL), BF16), pltpu.VMEM((tm, 2 * D_MODEL), F32),
                        pltpu.VMEM((tm, D_MODEL), F32), pltpu.VMEM((tm, D_MODEL), F32),
                        pltpu.VMEM((1, LANES), F32)],
        compiler_params=pltpu.CompilerParams(
            dimension_semantics=("arbitrary",), vmem_limit_bytes=VMEM_LIMIT),
        name="merge",
    )(o_a, o_b, x2d, x2d, mix_g, w_gates, hg_up, gd_up, w_out, norm_g, rw, rb, ltri)


def _sc_gather(table, idx):
    n_idx = idx.shape[0]
    cols = table.shape[1]
    per_worker = n_idx // SC_WORKERS
    assert n_idx % (SC_WORKERS * SC_CHUNK) == 0
    mesh = plsc.VectorSubcoreMesh(core_axis_name="c", subcore_axis_name="s")

    @functools.partial(
        pl.kernel, mesh=mesh,
        out_type=jax.ShapeDtypeStruct((n_idx, cols), table.dtype),
        scratch_types=[pltpu.VMEM((SC_CHUNK,), jnp.int32),
                       pltpu.VMEM((SC_CHUNK, cols), table.dtype),
                       pltpu.SemaphoreType.DMA],
    )
    def gather(table_hbm, idx_hbm, out_hbm, idx_v, rows_v, sem):
        worker = lax.axis_index("s") * SC_CORES + lax.axis_index("c")
        base = worker * per_worker

        @pl.loop(0, per_worker // SC_CHUNK)
        def _(c):
            off = pl.multiple_of(base + c * SC_CHUNK, SC_CHUNK)
            pltpu.sync_copy(idx_hbm.at[pl.ds(off, SC_CHUNK)], idx_v)
            pltpu.async_copy(table_hbm.at[idx_v], rows_v, sem).wait()
            pltpu.sync_copy(rows_v, out_hbm.at[pl.ds(off, SC_CHUNK)])

    return gather(table, idx)


def _sc_dispatch(rows, dest, n_out):
    t, cols = rows.shape
    per_worker = t // SC_WORKERS
    assert t % (SC_WORKERS * SC_CHUNK) == 0 and dest.shape[0] == 2 * t
    mesh = plsc.VectorSubcoreMesh(core_axis_name="c", subcore_axis_name="s")

    @functools.partial(
        pl.kernel, mesh=mesh,
        out_type=jax.ShapeDtypeStruct((n_out, cols), rows.dtype),
        scratch_types=[pltpu.VMEM((SC_CHUNK,), jnp.int32),
                       pltpu.VMEM((SC_CHUNK, cols), rows.dtype)],
    )
    def dispatch(rows_hbm, dest_hbm, out_hbm, idx_v, rows_v):
        worker = lax.axis_index("s") * SC_CORES + lax.axis_index("c")
        base = worker * per_worker

        @pl.loop(0, per_worker // SC_CHUNK)
        def _(c):
            off = pl.multiple_of(base + c * SC_CHUNK, SC_CHUNK)
            pltpu.sync_copy(rows_hbm.at[pl.ds(off, SC_CHUNK)], rows_v)
            for slot in range(2):
                pltpu.sync_copy(dest_hbm.at[pl.ds(slot * t + off, SC_CHUNK)], idx_v)
                pltpu.sync_copy(rows_v, out_hbm.at[idx_v])

    return dispatch(rows, dest)


def _moe_kernel(be_ref, nu_ref, x_ref, wg_ref, wu_ref, wd_ref, y_ref, wg_bf, wu_bf, wd_bf):
    i = pl.program_id(0)

    @pl.when((i == 0) | (be_ref[i] != be_ref[jnp.maximum(i - 1, 0)]))
    def _():
        _cast_rows(wg_ref, wg_bf)
        _cast_rows(wu_ref, wu_bf)
        _cast_rows(wd_ref, wd_bf)

    @pl.when(i < nu_ref[0])
    def _():
        xb = _unpack_halves(x_ref[...]).astype(BF16)
        a = jnp.dot(xb, wg_bf[...], preferred_element_type=F32)
        u = jnp.dot(xb, wu_bf[...], preferred_element_type=F32)
        y = jnp.dot((_silu(a) * u).astype(BF16), wd_bf[...], preferred_element_type=F32)
        y_ref[...] = _pack_halves(y)

    @pl.when(i >= nu_ref[0])
    def _():
        y_ref[...] = jnp.zeros_like(y_ref)


def _moe(blk_expert, n_used, x_sorted, w_gate, w_up, w_down):
    n_blocks = blk_expert.shape[0]
    wspec = lambda shape: pl.BlockSpec((None,) + shape, lambda i, be, nu: (be[i], 0, 0))
    rows = pl.BlockSpec((MOE_ROWS, HALF), lambda i, be, nu: (i, 0))
    grid_spec = pltpu.PrefetchScalarGridSpec(
        num_scalar_prefetch=2,
        grid=(n_blocks,),
        in_specs=[rows, wspec((D_MODEL, D_FF)), wspec((D_MODEL, D_FF)), wspec((D_FF, D_MODEL))],
        out_specs=rows,
        scratch_shapes=[pltpu.VMEM((D_MODEL, D_FF), BF16), pltpu.VMEM((D_MODEL, D_FF), BF16),
                        pltpu.VMEM((D_FF, D_MODEL), BF16)],
    )
    return pl.pallas_call(
        _moe_kernel,
        grid_spec=grid_spec,
        out_shape=jax.ShapeDtypeStruct(x_sorted.shape, U32),
        compiler_params=pltpu.CompilerParams(
            dimension_semantics=("arbitrary",), vmem_limit_bytes=VMEM_LIMIT),
        name="moe",
    )(blk_expert, n_used, x_sorted, w_gate, w_up, w_down)


def _combine_kernel(h2_ref, rw_ref, g_ref, y0_ref, y1_ref, o_ref):
    rw = rw_ref[...]
    h = h2_ref[...] + rw[:, 0:1] * _unpack_halves(y0_ref[...]) + rw[:, 1:2] * _unpack_halves(y1_ref[...])
    o_ref[...] = h * lax.rsqrt(jnp.mean(h * h, axis=-1, keepdims=True) + EPS) * g_ref[...]


def _combine(h2, rweights, final_g, y):
    t = h2.shape[0]
    tm = min(COMBINE_TM, t)
    row = lambda i: (i, 0)
    return pl.pallas_call(
        _combine_kernel,
        grid=(t // tm,),
        in_specs=[pl.BlockSpec((tm, D_MODEL), row),
                  pl.BlockSpec((tm, LANES), row),
                  pl.BlockSpec((1, D_MODEL), lambda i: (0, 0)),
                  pl.BlockSpec((tm, HALF), row),
                  pl.BlockSpec((tm, HALF), lambda i: (i + t // tm, 0))],
        out_specs=pl.BlockSpec((tm, D_MODEL), row),
        out_shape=jax.ShapeDtypeStruct((t, D_MODEL), F32),
        compiler_params=pltpu.CompilerParams(
            dimension_semantics=("arbitrary",), vmem_limit_bytes=VMEM_LIMIT),
        name="combine",
    )(h2, rweights, final_g, y, y)


def _block_layout(ids, counts_row, t):
    n_blocks = 2 * t // MOE_ROWS + N_EXPERTS
    counts = counts_row[0, ROUTE_LANE0:ROUTE_LANE0 + N_EXPERTS].astype(jnp.int32)
    padded = ((counts + MOE_ROWS - 1) // MOE_ROWS) * MOE_ROWS
    pend = jnp.cumsum(padded)
    pstart = pend - padded
    blk_first = jnp.arange(n_blocks, dtype=jnp.int32) * MOE_ROWS
    blk_expert = jnp.minimum(jnp.sum((pend[None, :] <= blk_first[:, None]).astype(jnp.int32), axis=1),
                             N_EXPERTS - 1).astype(jnp.int32)
    n_used = (pend[-1] // MOE_ROWS).astype(jnp.int32).reshape(1)
    expert, rank = ids[2:4], ids[4:6]
    experts = jnp.arange(N_EXPERTS, dtype=jnp.int32)
    first_row = jnp.sum(jnp.where(expert[:, :, None] == experts, pstart, 0), axis=-1)
    dest = (first_row + rank).reshape(2 * t)
    return dest, blk_expert, n_used, n_blocks * MOE_ROWS


def kernel(x, meta_tokens, hg_lb_logits, norm_mix_g, w_in, gd_conv_w, gd_A_log, gd_dt_bias, hg_norm_g, gd_norm_g, hg_up, gd_up, w_out, norm_ffn_g, router_group_w, router_group_b, router_expert_w, router_expert_b, w_gate, w_up, w_down, final_norm_g):
    bsz, seq, d = x.shape
    t = bsz * seq
    x2d = x.reshape(t, d)

    lb = jnp.cumsum(jax.nn.softmax(hg_lb_logits.astype(F32), axis=0), axis=0)[0].reshape(1, WIDTH)
    w_all = w_in.astype(F32)
    w_gates = w_all[0, :, W_SMALL + 2 * HEADS:]
    g_mix = norm_mix_g[0].reshape(1, d)
    meta_blk = jnp.concatenate([jnp.zeros((CHUNK - N_META, d), F32), meta_tokens.astype(F32)], axis=0)

    conv_w = gd_conv_w[0].astype(F32)
    no_history = jnp.zeros((SUBLANES, 3 * WIDTH), F32)
    proj, proj_meta = {}, {}
    proj["hgrn"] = _norm_proj_hgrn(x2d, g_mix, lb, w_all, PROJ_TM)
    proj_meta["hgrn"] = _norm_proj_hgrn(meta_blk, g_mix, lb, w_all, CHUNK)
    proj_meta["gdn"], meta_tail, logits_meta = _norm_proj_conv(meta_blk, g_mix, w_all, conv_w, no_history,
                                                               CHUNK, CHUNK)
    proj["gdn"], _, logits = _norm_proj_conv(x2d, g_mix, w_all, conv_w, meta_tail, PROJ_TM, seq)

    def scalar_rows(lg, nb, nc):
        s = lg.reshape(2, N_STACKS, HEADS_PER_STACK, nb, nc, CHUNK)
        return s.transpose(3, 4, 1, 0, 2, 5).reshape(nb, nc, N_STACKS, 2, STACK)

    a_log_row = jnp.repeat(gd_A_log[0].astype(F32), CHUNK).reshape(N_STACKS, 1, STACK)
    dt_row = jnp.repeat(gd_dt_bias[0].astype(F32), CHUNK).reshape(N_STACKS, 1, STACK)
    hg_g, gd_g = hg_norm_g[0].reshape(1, HEAD_DIM), gd_norm_g[0].reshape(1, HEAD_DIM)

    zero_state = jnp.zeros((HEADS, HEAD_DIM, HEAD_DIM), F32)
    per_seq = lambda st: jnp.tile(st, (bsz, 1, 1))
    _, hg_state = _hgrn(proj_meta["hgrn"][None], zero_state, hg_g, 1)
    o_a, _ = _hgrn(proj["hgrn"].reshape(bsz, seq, -1), per_seq(hg_state), hg_g, HGRN_CHUNKS)
    _, gd_state = _gdn(proj_meta["gdn"][None], zero_state, scalar_rows(logits_meta, 1, 1),
                       a_log_row, dt_row, gd_g, 1)
    o_b, _ = _gdn(proj["gdn"].reshape(bsz, seq, -1), per_seq(gd_state),
                  scalar_rows(logits, bsz, seq // CHUNK), a_log_row, dt_row, gd_g, GDN_CHUNKS)

    rw = jnp.zeros((d, LANES), F32)
    rw = rw.at[:, :N_GROUPS].set(router_group_w[0]).at[:, N_GROUPS:N_GROUPS + N_EXPERTS].set(router_expert_w[0])
    rb = jnp.zeros((1, LANES), F32)
    rb = rb.at[0, :N_GROUPS].set(router_group_b[0]).at[0, N_GROUPS:N_GROUPS + N_EXPERTS].set(router_expert_b[0])
    h2, xn, info, ids, counts = _merge(o_a.reshape(t, WIDTH), o_b.reshape(t, WIDTH), x2d, g_mix, w_gates,
                                  hg_up[0].astype(BF16), gd_up[0].astype(BF16), w_out[0].astype(BF16),
                                  norm_ffn_g[0].reshape(1, d), rw.astype(BF16), rb)

    dest, blk_expert, n_used, n_rows = _block_layout(ids, counts, t)
    x_sorted = _sc_dispatch(xn, dest, n_rows)
    y_sorted = _moe(blk_expert, n_used, x_sorted,
                    w_gate[0].astype(F32), w_up[0].astype(F32), w_down[0].astype(F32))
    y_tok = _sc_gather(y_sorted, dest)
    out = _combine(h2, info, final_norm_g.reshape(1, d), y_tok)
    return out.reshape(bsz, seq, d)
```

```python
import functools

import numpy as np
import jax
import jax.numpy as jnp
from jax import lax
from jax.experimental import pallas as pl
from jax.experimental.pallas import tpu as pltpu
from jax.experimental.pallas import tpu_sc as plsc

F32 = jnp.float32
BF16 = jnp.bfloat16
U32 = jnp.uint32

D_MODEL = 1024
N_META = 16
CHUNK_BITS = 6
CHUNK = 1 << CHUNK_BITS
EPS = 1e-6
HEADS = 4
HEAD_DIM = 128
WIDTH = HEADS * HEAD_DIM
HEADS_PER_STACK = 2
STACK = HEADS_PER_STACK * CHUNK
N_STACKS = HEADS // HEADS_PER_STACK
CONV_W = 4
NEUMANN_BITS = 3
N_GROUPS = 4
EXPERTS_PER_GROUP = 8
N_EXPERTS = N_GROUPS * EXPERTS_PER_GROUP
D_FF = 512
LANES = 128
SUBLANES = 8
HALF = D_MODEL // 2
MOE_ROWS = 512
CAST_SLAB = 128
SC_CORES, SC_SUBCORES = 2, 16
SC_WORKERS = SC_CORES * SC_SUBCORES
SC_CHUNK = 128
PROJ_TM = 1024
W_HGRN, W_GDN, W_SMALL = 0, 4 * WIDTH, 8 * WIDTH
HGRN_CHUNKS, GDN_CHUNKS = 4, 2
TOKEN_TM = 512
COMBINE_TM = 1024
VMEM_LIMIT = 56 * 1024 * 1024


def _bdot(a, b):
    return lax.dot_general(a.astype(BF16), b.astype(BF16), (((2,), (1,)), ((0,), (0,))),
                           preferred_element_type=F32)


def _bdot_nt(a, b):
    return lax.dot_general(a.astype(BF16), b.astype(BF16), (((2,), (2,)), ((0,), (0,))),
                           preferred_element_type=F32)


def _bdot_tn(a, b):
    return lax.dot_general(a.astype(BF16), b.astype(BF16), (((1,), (1,)), ((0,), (0,))),
                           preferred_element_type=F32)


def _split3(x):
    hi = x.astype(BF16)
    r1 = x - hi.astype(F32)
    mid = r1.astype(BF16)
    lo = (r1 - mid.astype(F32)).astype(BF16)
    return hi, mid, lo


def _sigmoid(x):
    return 1.0 / (1.0 + jnp.exp(-x))


def _silu(x):
    return x * _sigmoid(x)


def _softplus(x):
    return jnp.maximum(x, 0.0) + jnp.log(1.0 + jnp.exp(-jnp.abs(x)))


def _stack_heads(x, bsz):
    x3 = x.reshape(bsz, CHUNK, WIDTH)
    y = jnp.concatenate([x3[:, :, h * HEAD_DIM:(h + 1) * HEAD_DIM] for h in range(HEADS)], axis=1)
    return y.reshape(bsz * HEADS * CHUNK, HEAD_DIM)


def _unstack_heads(x, bsz):
    x3 = x.reshape(bsz, HEADS * CHUNK, HEAD_DIM)
    return jnp.concatenate([x3[:, h * CHUNK:(h + 1) * CHUNK, :] for h in range(HEADS)], axis=2)


def _to_stacks(a, bsz):
    return a.reshape(bsz * N_STACKS, STACK, a.shape[-1])


def _to_heads(a, bsz):
    return a.reshape(bsz * HEADS, CHUNK, a.shape[-1])


def _roll_rows(x, shift):
    n = x.shape[0]
    return pltpu.roll(x, shift % n, axis=0)


def _chunks_first(x, nc):
    return jnp.concatenate([x[:, j * CHUNK:(j + 1) * CHUNK] for j in range(nc)], axis=0)


def _chunks_first_scalars(s, nc):
    return jnp.concatenate([s[:, j] for j in range(nc)], axis=0)


def _load_chunks(ref, nc):
    x = _chunks_first(ref[...], nc)
    return x.reshape(x.shape[0] * CHUNK, WIDTH)


def _store_chunks(ref, x, nb, nc):
    ref[...] = jnp.concatenate([x[j * nb:(j + 1) * nb] for j in range(nc)], axis=1).astype(ref.dtype)


def _cast_rows(src_ref, dst_ref, col0=0):
    slab = CAST_SLAB
    rows, cols = src_ref.shape
    assert rows % slab == 0

    def body(k, carry):
        r = pl.multiple_of(k * slab, slab)
        dst_ref[pl.ds(r, slab), col0:col0 + cols] = src_ref[pl.ds(r, slab), :].astype(dst_ref.dtype)
        return carry

    lax.fori_loop(0, rows // slab, body, 0)


def _pack_halves(x):
    lo = lax.bitcast_convert_type(x[:, :HALF].astype(BF16).astype(F32), U32)
    hi = lax.bitcast_convert_type(x[:, HALF:].astype(BF16).astype(F32), U32)
    return (lo >> 16) | (hi & jnp.uint32(0xFFFF0000))


def _unpack_halves(p):
    lo = lax.bitcast_convert_type(p << 16, F32)
    hi = lax.bitcast_convert_type(p & jnp.uint32(0xFFFF0000), F32)
    return jnp.concatenate([lo, hi], axis=1)


def _normed_bf16(x_ref, g_ref):
    x = x_ref[...]
    return (x * lax.rsqrt(jnp.mean(x * x, axis=-1, keepdims=True) + EPS) * g_ref[...]).astype(BF16)


def _norm_proj_hgrn_kernel(x_ref, g_ref, lb_ref, w_ref, o_ref, wbf_ref):
    @pl.when(pl.program_id(0) == 0)
    def _():
        _cast_rows(w_ref, wbf_ref)

    acc = jnp.dot(_normed_bf16(x_ref, g_ref), wbf_ref[...], preferred_element_type=F32)
    w1, w2, w3 = WIDTH, 2 * WIDTH, 3 * WIDTH
    lb = lb_ref[...]
    o_ref[:, :w1] = _silu(acc[:, :w1])
    o_ref[:, w1:w2] = lb + (1.0 - lb) * _sigmoid(acc[:, w1:w2])
    o_ref[:, w2:w3] = acc[:, w2:w3]
    o_ref[:, w3:] = _silu(acc[:, w3:])


def _zero_rows(ref):
    slab = min(CAST_SLAB, ref.shape[0])
    assert ref.shape[0] % slab == 0

    def body(k, carry):
        ref[pl.ds(pl.multiple_of(k * slab, slab), slab), :] = jnp.zeros((slab, ref.shape[1]), ref.dtype)
        return carry

    lax.fori_loop(0, ref.shape[0] // slab, body, 0)


def _norm_proj_conv_kernel(x_ref, g_ref, w_ref, ws_ref, cw_ref, hist_ref, o_ref, tail_ref, logit_ref,
                           wbf_ref, acc_ref, carry_ref, *, steps_per_seq):
    i = pl.program_id(0)

    @pl.when(i == 0)
    def _():
        _cast_rows(w_ref, wbf_ref)
        _cast_rows(ws_ref, wbf_ref, 4 * WIDTH)
        _zero_rows(acc_ref)
        carry_ref[...] = jnp.zeros_like(carry_ref)

    xn = _normed_bf16(x_ref, g_ref)
    w3, w4 = 3 * WIDTH, 4 * WIDTH
    rows = x_ref.shape[0]
    first_of_seq = (i - 1) % steps_per_seq == 0
    cw = cw_ref[...]
    r8 = lax.broadcasted_iota(jnp.int32, (SUBLANES, WIDTH), 0)
    for c0 in range(0, w3, WIDTH):
        cols = slice(c0, c0 + WIDTH)
        raw = acc_ref[:, cols]
        hist = jnp.where(first_of_seq, hist_ref[:, cols], carry_ref[:, cols])
        tap = lambda d: cw[CONV_W - 1 - d:CONV_W - d, cols]
        body = raw * tap(0)
        head = raw[:SUBLANES]
        first = head * tap(0)
        for d in range(1, CONV_W):
            body = body + _roll_rows(raw, d) * tap(d)
            first = first + jnp.where(r8 >= d, _roll_rows(head, d), _roll_rows(hist, d)) * tap(d)
        o_ref[:SUBLANES, cols] = _silu(first)
        o_ref[SUBLANES:, cols] = _silu(body[SUBLANES:])
        carry_ref[:, cols] = raw[rows - SUBLANES:]
        acc_ref[:, cols] = jnp.dot(xn, wbf_ref[:, cols], preferred_element_type=F32)
    rest = acc_ref[:, w3:]
    o_ref[:, w3:] = _silu(rest[:, :WIDTH])
    logit_ref[...] = rest[:, WIDTH:].T[:SUBLANES]
    acc_ref[:, w3:] = jnp.dot(xn, wbf_ref[:, w3:], preferred_element_type=F32)
    tail_ref[...] = carry_ref[...]


def _proj_call(kernel_fn, name, x2d, tm, n, operands, specs, extra_out=(), extra_scratch=(),
               weight_cols=None, delayed=False):
    m, k = x2d.shape
    weight_cols = n if weight_cols is None else weight_cols
    tm = min(tm, m)
    tiles = m // tm
    x_map = (lambda i: (jnp.minimum(i, tiles - 1), 0)) if delayed else (lambda i: (i, 0))
    o_map = (lambda i: (jnp.maximum(i - 1, 0), 0)) if delayed else (lambda i: (i, 0))
    out_specs = [pl.BlockSpec((tm, n), o_map)] + [s for s, _ in extra_out]
    out_shape = [jax.ShapeDtypeStruct((m, n), F32)] + [s for _, s in extra_out]
    return pl.pallas_call(
        kernel_fn,
        grid=(tiles + int(delayed),),
        in_specs=[pl.BlockSpec((tm, k), x_map)] + specs,
        out_specs=out_specs,
        out_shape=out_shape,
        scratch_shapes=[pltpu.VMEM((k, weight_cols), BF16)] + list(extra_scratch),
        compiler_params=pltpu.CompilerParams(
            dimension_semantics=("arbitrary",), vmem_limit_bytes=VMEM_LIMIT),
        name=name,
    )(x2d, *operands)


def _weight_spec(k, n, col0):
    assert col0 % n == 0
    return pl.BlockSpec((None, k, n), lambda i: (0, 0, col0 // n), pipeline_mode=pl.Buffered(1))


def _norm_proj_hgrn(x2d, g, lb, w, tm):
    k, n = w.shape[1], 4 * WIDTH
    const = lambda i: (0, 0)
    specs = [pl.BlockSpec((1, k), const), pl.BlockSpec((1, WIDTH), const), _weight_spec(k, n, W_HGRN)]
    return _proj_call(_norm_proj_hgrn_kernel, "norm_proj_hgrn", x2d, tm, n, (g, lb, w), specs)[0]


def _norm_proj_conv(x2d, g, w, conv_w, hist, tm, rows_per_seq):
    m, k = x2d.shape[0], w.shape[1]
    n_out, n_w = 4 * WIDTH, 4 * WIDTH + LANES
    const = lambda i: (0, 0)
    tm = min(tm, rows_per_seq)
    assert rows_per_seq % tm == 0 and m % rows_per_seq == 0
    tail_shape = (SUBLANES, 3 * WIDTH)
    specs = [pl.BlockSpec((1, k), const), _weight_spec(k, 4 * WIDTH, W_GDN), _weight_spec(k, LANES, W_SMALL),
             pl.BlockSpec((CONV_W, 3 * WIDTH), const), pl.BlockSpec(tail_shape, const)]
    logit_spec = pl.BlockSpec((SUBLANES, tm), lambda i: (0, jnp.maximum(i - 1, 0)))
    return _proj_call(functools.partial(_norm_proj_conv_kernel, steps_per_seq=rows_per_seq // tm),
                      "norm_proj_gdn", x2d, tm, n_out, (g, w, w, conv_w, hist), specs,
                      extra_out=[(pl.BlockSpec(tail_shape, const), jax.ShapeDtypeStruct(tail_shape, F32)),
                                 (logit_spec, jax.ShapeDtypeStruct((SUBLANES, m), F32))],
                      extra_scratch=[pltpu.VMEM((tm, n_w), F32), pltpu.VMEM(tail_shape, F32)],
                      weight_cols=n_w, delayed=True)


HGRN_LEVELS = 6
HGRN_DIAG = HGRN_LEVELS


def _hgrn_level_table():
    r = np.arange(STACK)
    t, hd = r % CHUNK, r // CHUNK
    x = t[:, None] ^ t[None, :]
    lv = np.floor(np.log2(np.maximum(x, 1))).astype(np.int32)
    valid = (hd[:, None] == hd[None, :]) & (t[:, None] > t[None, :])
    lv = np.where(valid, lv, -1)
    lv = np.where(r[:, None] == r[None, :], HGRN_DIAG, lv)
    return jnp.asarray(lv, dtype=jnp.int32)


def _hgrn_kernel(lv_ref, g_ref, st0_ref, q_ref, f_ref, v_ref, gate_ref, o_ref, stout_ref, st_ref, *, nb, nc):
    @pl.when(pl.program_id(0) == 0)
    def _():
        st_ref[...] = st0_ref[...]

    bsz = nc * nb
    q = _load_chunks(q_ref, nc)
    f = _load_chunks(f_ref, nc)
    v = _load_chunks(v_ref, nc)
    gate = _load_chunks(gate_ref, nc)

    per_seq = lambda a: a.reshape(bsz, CHUNK, WIDTH)
    trow = lax.broadcasted_iota(jnp.int32, (1, CHUNK, WIDTH), 1)
    b = jnp.log2(f)
    s = 1
    while s < CHUNK:
        b = (per_seq(b) + jnp.where(trow >= s, per_seq(_roll_rows(b, s)), 0.0)).reshape(b.shape)
        s *= 2

    qs, ks, vs, bs = (_stack_heads(a, bsz) for a in (q, 1.0 - f, v, b))
    lv = lv_ref[...][None]
    to3 = functools.partial(_to_stacks, bsz=bsz)
    to_g = functools.partial(_to_heads, bsz=bsz)
    roll_g = lambda a, shift: to_g(_roll_rows(a.reshape(qs.shape), shift))
    t = lax.broadcasted_iota(jnp.int32, (1, CHUNK, HEAD_DIM), 1)
    qs_g, ks_g, bs_g = to_g(qs), to_g(ks), to_g(bs)

    attn = jnp.where(lv == HGRN_DIAG, _bdot_nt(to3(qs), to3(ks)), 0.0)
    bref = bs_g
    bnext = roll_g(bs_g, -1)
    for li in range(HGRN_LEVELS):
        m = 1 << li
        if li > 0:
            half = m // 2
            upper = (t & (m - 1)) >= half
            if half < SUBLANES:
                grp = lambda a: a.reshape(-1, SUBLANES, HEAD_DIM)
                local = lambda a, sh: pltpu.roll(grp(a), sh % SUBLANES, axis=1).reshape(a.shape)
                bref = jnp.where(upper, local(bref, half), bref)
                bnext = jnp.where(upper, bnext, local(bnext, -half))
            else:
                bref = jnp.where(upper, roll_g(bref, half), bref)
                bnext = jnp.where(upper, bnext, roll_g(bnext, -half))
        odd = ((t >> li) & 1) == 1
        both = jnp.where(odd, qs_g, ks_g) * jnp.exp2(jnp.where(odd, bs_g - bref, bnext - bs_g))
        both = to3(both.reshape(qs.shape)).astype(BF16)
        attn = jnp.where(lv == li, _bdot_nt(both, both), attn)

    o = _bdot(attn, to3(vs))

    n_g = nb * HEADS
    st = st_ref[...]
    qe = qs_g * jnp.exp2(bs_g)
    b_end = bs_g[:, CHUNK - 1:CHUNK, :]
    ke = ks_g * jnp.exp2(b_end - bs_g)
    vs_g = to_g(vs)
    o_inter = []
    for j in range(nc):
        sl = slice(j * n_g, (j + 1) * n_g)
        o_inter.append(_bdot_nt(qe[sl], st))
        st = st * jnp.exp2(b_end[sl]) + _bdot_tn(vs_g[sl], ke[sl])
    st_ref[...] = st
    stout_ref[...] = st
    o = (to_g(o) + jnp.concatenate(o_inter, axis=0)).reshape(qs.shape)

    o = o * lax.rsqrt(jnp.mean(o * o, axis=-1, keepdims=True) + EPS) * g_ref[...]
    o = o * _stack_heads(gate, bsz)
    _store_chunks(o_ref, _unstack_heads(o, bsz), nb, nc)


def _gdn_cumsum_matrix():
    r = np.arange(STACK)
    u = (r[:, None] // CHUNK == r[None, :] // CHUNK) & (r[:, None] <= r[None, :])
    return jnp.asarray(u, dtype=BF16)


def _gdn_kernel(u_ref, alog_ref, dt_ref, g_ref, s_ref, st0_ref, q_ref, k_ref, v_ref, z_ref,
                o_ref, stout_ref, st_ref, *, nb, nc):
    @pl.when(pl.program_id(0) == 0)
    def _():
        st_ref[...] = st0_ref[...]

    bsz = nc * nb
    qs = _stack_heads(_load_chunks(q_ref, nc), bsz)
    ks = _stack_heads(_load_chunks(k_ref, nc), bsz)
    vs = _stack_heads(_load_chunks(v_ref, nc), bsz)
    gate = _load_chunks(z_ref, nc)
    qs = qs * lax.rsqrt(jnp.sum(qs * qs, axis=-1, keepdims=True) + EPS) * (HEAD_DIM ** -0.5)
    ks = ks * lax.rsqrt(jnp.sum(ks * ks, axis=-1, keepdims=True) + EPS)
    to3 = functools.partial(_to_stacks, bsz=bsz)
    qs, ks, vs = to3(qs), to3(ks), to3(vs)
    n_st = bsz * N_STACKS

    srow = _chunks_first_scalars(s_ref[...], nc).reshape(n_st, 2, STACK)
    per_stack = lambda ref: jnp.broadcast_to(ref[...][None], (bsz, N_STACKS, 1, STACK)).reshape(n_st, 1, STACK)
    beta_row = _sigmoid(srow[:, 0:1, :])
    g_row = -jnp.exp(per_stack(alog_ref)) * _softplus(srow[:, 1:2, :] + per_stack(dt_ref))
    n8 = n_st * SUBLANES
    g8 = jnp.broadcast_to(g_row, (n_st, SUBLANES, STACK)).reshape(n8, STACK)
    pieces = jnp.concatenate(_split3(g8), axis=0)
    cum = jnp.dot(pieces, u_ref[...], preferred_element_type=F32)
    cum = cum[0:n8] + cum[n8:2 * n8] + cum[2 * n8:]
    cum_row = cum.reshape(n_st, SUBLANES, STACK)[:, 0:1, :]

    r = lax.broadcasted_iota(jnp.int32, (STACK, STACK), 0)
    cc = lax.broadcasted_iota(jnp.int32, (STACK, STACK), 1)
    eye = (r == cc)[None]
    cum_col = jnp.sum(jnp.where(eye, cum_row, 0.0), axis=2, keepdims=True)
    beta_col = jnp.sum(jnp.where(eye, beta_row, 0.0), axis=2, keepdims=True)
    same = (r >> CHUNK_BITS) == (cc >> CHUNK_BITS)
    lower = (same & (r >= cc))[None]
    strict = (same & (r > cc))[None]
    ratio = jnp.exp(jnp.minimum(cum_col - cum_row, 0.0))

    kkt = _bdot_nt(ks, ks)
    qkt = _bdot_nt(qs, ks)
    lm = jnp.where(strict, beta_col * ratio * kkt, 0.0)

    ident = eye.astype(F32)
    l0 = jnp.where(((r >> NEUMANN_BITS) == (cc >> NEUMANN_BITS))[None], lm, 0.0)
    l2 = _bdot(l0, l0)
    l4 = _bdot(l2, l2)
    xinv = _bdot(_bdot(ident - l0, ident + l2), ident + l4)
    for bits in range(NEUMANN_BITS, CHUNK_BITS):
        off = (((r >> (bits + 1)) == (cc >> (bits + 1))) & ((r >> bits) != (cc >> bits)))[None]
        xinv = xinv - _bdot(_bdot(xinv, jnp.where(off, lm, 0.0)), xinv)

    decay_col = jnp.exp(cum_col)
    rhs = jnp.concatenate([beta_col * vs, (beta_col * decay_col) * ks], axis=2)
    sol = _bdot(xinv, rhs)

    to_g = functools.partial(_to_heads, bsz=bsz)
    n_g = nb * HEADS
    st = st_ref[...]
    u0_g, w_g, q_g = to_g(sol[:, :, :HEAD_DIM]), to_g(sol[:, :, HEAD_DIM:]), to_g(qs)
    cum_g = to_g(cum_col)
    cum_end = cum_g[:, CHUNK - 1:CHUNK, :]
    kdec = to_g(ks) * jnp.exp(cum_end - cum_g)
    decay_g = to_g(decay_col)
    us, o_inter = [], []
    for j in range(nc):
        sl = slice(j * n_g, (j + 1) * n_g)
        u_j = u0_g[sl] - _bdot(w_g[sl], st)
        us.append(u_j)
        o_inter.append(decay_g[sl] * _bdot(q_g[sl], st))
        st = jnp.exp(cum_end[sl]) * st + _bdot_tn(kdec[sl], u_j)
    st_ref[...] = st
    stout_ref[...] = st
    attn = jnp.where(lower, qkt * ratio, 0.0)
    u = jnp.concatenate(us, axis=0).reshape(bsz * HEADS * CHUNK, HEAD_DIM)
    o = jnp.concatenate(o_inter, axis=0) + to_g(_bdot(attn, to3(u)))
    o = o.reshape(bsz * HEADS * CHUNK, HEAD_DIM)

    o = o * lax.rsqrt(jnp.mean(o * o, axis=-1, keepdims=True) + EPS) * g_ref[...]
    o = o * _stack_heads(gate, bsz)
    _store_chunks(o_ref, _unstack_heads(o, bsz), nb, nc)


def _recurrence_call(kernel_fn, name, proj, state0, operands, specs, nc):
    nb, length, _ = proj.shape
    rows = nc * CHUNK
    state_shape = (nb * HEADS, HEAD_DIM, HEAD_DIM)
    state_spec = pl.BlockSpec(state_shape, lambda c: (0, 0, 0))
    chunk_specs = [pl.BlockSpec((nb, rows, WIDTH), functools.partial(lambda c, j: (0, c, j), j=j))
                   for j in range(4)]
    return pl.pallas_call(
        functools.partial(kernel_fn, nb=nb, nc=nc),
        grid=(length // rows,),
        in_specs=specs + [state_spec] + chunk_specs,
        out_specs=[pl.BlockSpec((nb, rows, WIDTH), lambda c: (0, c, 0)), state_spec],
        out_shape=[jax.ShapeDtypeStruct((nb, length, WIDTH), BF16),
                   jax.ShapeDtypeStruct(state_shape, F32)],
        scratch_shapes=[pltpu.VMEM(state_shape, F32)],
        compiler_params=pltpu.CompilerParams(
            dimension_semantics=("arbitrary",), vmem_limit_bytes=VMEM_LIMIT),
        name=name,
    )(*operands, state0, *([proj] * 4))


def _hgrn(proj, state0, norm_g, nc):
    const = lambda c: (0, 0)
    specs = [pl.BlockSpec((STACK, STACK), const), pl.BlockSpec((1, HEAD_DIM), const)]
    return _recurrence_call(_hgrn_kernel, "hgrn", proj, state0, (_hgrn_level_table(), norm_g), specs, nc)


def _gdn(proj, state0, scal, a_log_row, dt_row, norm_g, nc):
    nb = proj.shape[0]
    const = lambda c: (0, 0)
    const3 = lambda c: (0, 0, 0)
    specs = [pl.BlockSpec((STACK, STACK), const),
             pl.BlockSpec((N_STACKS, 1, STACK), const3),
             pl.BlockSpec((N_STACKS, 1, STACK), const3),
             pl.BlockSpec((1, HEAD_DIM), const),
             pl.BlockSpec((nb, nc, N_STACKS, 2, STACK), lambda c: (0, c, 0, 0, 0))]
    return _recurrence_call(_gdn_kernel, "gdn", proj, state0,
                            (_gdn_cumsum_matrix(), a_log_row, dt_row, norm_g, scal), specs, nc)


ROUTE_LANE0 = N_GROUPS
ROUTE_NEG = -1e30


def _route_tile(lg, ltri, carry):
    lane = lax.broadcasted_iota(jnp.int32, lg.shape, 1).astype(F32)
    first = lambda mask: jnp.min(jnp.where(mask, lane, float(LANES)), axis=1, keepdims=True)
    top = lambda mask: jnp.max(jnp.where(mask, lg, ROUTE_NEG), axis=1, keepdims=True)

    is_g = lane < N_GROUPS
    gmax = top(is_g)
    grp = first(is_g & (lg == gmax))
    p_grp = 1.0 / jnp.sum(jnp.where(is_g, jnp.exp(lg - gmax), 0.0), axis=1, keepdims=True)

    lo = ROUTE_LANE0 + grp * EXPERTS_PER_GROUP
    in_grp = (lane >= lo) & (lane < lo + EXPERTS_PER_GROUP)
    m1 = top(in_grp)
    i1 = first(in_grp & (lg == m1))
    rest = in_grp & (lane != i1)
    m2 = top(rest)
    i2 = first(rest & (lg == m2))
    r = jnp.exp(m2 - m1)
    w0 = p_grp / (1.0 + r)
    w1 = p_grp * r / (1.0 + r)

    hot0 = (lane == i1).astype(F32)
    hot1 = (lane == i2).astype(F32)
    both = hot0 + hot1
    before = jnp.dot(ltri, both.astype(BF16), preferred_element_type=F32) + carry
    rank0 = jnp.sum(hot0 * before, axis=1, keepdims=True)
    rank1 = jnp.sum(hot1 * before, axis=1, keepdims=True)
    cols = (w0, w1, i1 - ROUTE_LANE0, i2 - ROUTE_LANE0, rank0, rank1)
    info = jnp.zeros_like(lg)
    for k, col in enumerate(cols):
        info = jnp.where(lane == k, col, info)
    return info, carry + jnp.sum(both, axis=0, keepdims=True)


def _merge_kernel(oa_ref, ob_ref, x_ref, xprev_ref, mg_ref, wgate_ref, hgup_ref, gdup_ref, wout_ref,
                  ng_ref, rw_ref, rb_ref, ltri_ref, h2_ref, xn_ref, info_ref, ids_ref, cnt_ref,
                  wgate_bf, gates_ref, ua_ref, ub_ref, carry_ref):
    i = pl.program_id(0)

    @pl.when(i == 0)
    def _():
        carry_ref[...] = jnp.zeros_like(carry_ref)
        _cast_rows(wgate_ref, wgate_bf)
        _zero_rows(gates_ref)
        _zero_rows(ua_ref)
        _zero_rows(ub_ref)

    gates = _sigmoid(gates_ref[...])
    merged = (gates[:, :D_MODEL] * ua_ref[...] + gates[:, D_MODEL:] * ub_ref[...]).astype(BF16)
    gates_ref[...] = jnp.dot(_normed_bf16(x_ref, mg_ref), wgate_bf[...], preferred_element_type=F32)
    h2 = xprev_ref[...] + jnp.dot(merged, wout_ref[...], preferred_element_type=F32)
    h2_ref[...] = h2
    xn = h2 * lax.rsqrt(jnp.mean(h2 * h2, axis=-1, keepdims=True) + EPS) * ng_ref[...]
    xn_ref[...] = _pack_halves(xn)
    lg = jnp.dot(xn.astype(BF16), rw_ref[...], preferred_element_type=F32) + rb_ref[...]
    ua_ref[...] = jnp.dot(oa_ref[...], hgup_ref[...], preferred_element_type=F32)
    ub_ref[...] = jnp.dot(ob_ref[...], gdup_ref[...], preferred_element_type=F32)
    old = carry_ref[...]
    info, carry = _route_tile(lg, ltri_ref[...], old)
    info_ref[...] = info
    ids_ref[...] = info.T[:SUBLANES].astype(jnp.int32)
    carry = jnp.where(i > 0, carry, old)
    carry_ref[...] = carry
    cnt_ref[...] = carry


def _merge(o_a, o_b, x2d, mix_g, w_gates, hg_up, gd_up, w_out, norm_g, rw, rb):
    t = x2d.shape[0]
    tm = TOKEN_TM
    tiles = t // tm
    new = lambda i: (jnp.minimum(i, tiles - 1), 0)
    row = lambda i: (jnp.maximum(i - 1, 0), 0)
    const = lambda i: (0, 0)
    ltri = jnp.asarray(np.tril(np.ones((tm, tm), np.float32), -1), dtype=BF16)
    return pl.pallas_call(
        _merge_kernel,
        grid=(tiles + 1,),
        in_specs=[pl.BlockSpec((tm, WIDTH), new),
                  pl.BlockSpec((tm, WIDTH), new),
                  pl.BlockSpec((tm, D_MODEL), new),
                  pl.BlockSpec((tm, D_MODEL), row),
                  pl.BlockSpec((1, D_MODEL), const),
                  pl.BlockSpec((D_MODEL, 2 * D_MODEL), const, pipeline_mode=pl.Buffered(1)),
                  pl.BlockSpec((WIDTH, D_MODEL), const),
                  pl.BlockSpec((WIDTH, D_MODEL), const),
                  pl.BlockSpec((D_MODEL, D_MODEL), const),
                  pl.BlockSpec((1, D_MODEL), const),
                  pl.BlockSpec((D_MODEL, LANES), const),
                  pl.BlockSpec((1, LANES), const),
                  pl.BlockSpec((tm, tm), const)],
        out_specs=[pl.BlockSpec((tm, D_MODEL), row),
                   pl.BlockSpec((tm, HALF), row),
                   pl.BlockSpec((tm, LANES), row),
                   pl.BlockSpec((SUBLANES, tm), lambda i: (0, jnp.maximum(i - 1, 0))),
                   pl.BlockSpec((1, LANES), const)],
        out_shape=[jax.ShapeDtypeStruct((t, D_MODEL), F32),
                   jax.ShapeDtypeStruct((t, HALF), U32),
                   jax.ShapeDtypeStruct((t, LANES), F32),
                   jax.ShapeDtypeStruct((SUBLANES, t), jnp.int32),
                   jax.ShapeDtypeStruct((1, LANES), F32)],
        scratch_shapes=[pltpu.VMEM((D_MODEL, 2 * D_MODEL), BF16), pltpu.VMEM((tm, 2 * D_MODEL), F32),
                        pltpu.VMEM((tm, D_MODEL), F32), pltpu.VMEM((tm, D_MODEL), F32),
                        pltpu.VMEM((1, LANES), F32)],
        compiler_params=pltpu.CompilerParams(
            dimension_semantics=("arbitrary",), vmem_limit_bytes=VMEM_LIMIT),
        name="merge",
    )(o_a, o_b, x2d, x2d, mix_g, w_gates, hg_up, gd_up, w_out, norm_g, rw, rb, ltri)


def _sc_gather(table, idx):
    n_idx = idx.shape[0]
    cols = table.shape[1]
    per_worker = n_idx // SC_WORKERS
    assert n_idx % (SC_WORKERS * SC_CHUNK) == 0
    mesh = plsc.VectorSubcoreMesh(core_axis_name="c", subcore_axis_name="s")

    @functools.partial(
        pl.kernel, mesh=mesh,
        out_type=jax.ShapeDtypeStruct((n_idx, cols), table.dtype),
        scratch_types=[pltpu.VMEM((SC_CHUNK,), jnp.int32),
                       pltpu.VMEM((SC_CHUNK, cols), table.dtype),
                       pltpu.SemaphoreType.DMA],
    )
    def gather(table_hbm, idx_hbm, out_hbm, idx_v, rows_v, sem):
        worker = lax.axis_index("s") * SC_CORES + lax.axis_index("c")
        base = worker * per_worker

        @pl.loop(0, per_worker // SC_CHUNK)
        def _(c):
            off = pl.multiple_of(base + c * SC_CHUNK, SC_CHUNK)
            pltpu.sync_copy(idx_hbm.at[pl.ds(off, SC_CHUNK)], idx_v)
            pltpu.async_copy(table_hbm.at[idx_v], rows_v, sem).wait()
            pltpu.sync_copy(rows_v, out_hbm.at[pl.ds(off, SC_CHUNK)])

    return gather(table, idx)


def _sc_dispatch(rows, dest, n_out):
    t, cols = rows.shape
    per_worker = t // SC_WORKERS
    assert t % (SC_WORKERS * SC_CHUNK) == 0 and dest.shape[0] == 2 * t
    mesh = plsc.VectorSubcoreMesh(core_axis_name="c", subcore_axis_name="s")

    @functools.partial(
        pl.kernel, mesh=mesh,
        out_type=jax.ShapeDtypeStruct((n_out, cols), rows.dtype),
        scratch_types=[pltpu.VMEM((SC_CHUNK,), jnp.int32),
                       pltpu.VMEM((SC_CHUNK, cols), rows.dtype)],
    )
    def dispatch(rows_hbm, dest_hbm, out_hbm, idx_v, rows_v):
        worker = lax.axis_index("s") * SC_CORES + lax.axis_index("c")
        base = worker * per_worker

        @pl.loop(0, per_worker // SC_CHUNK)
        def _(c):
            off = pl.multiple_of(base + c * SC_CHUNK, SC_CHUNK)
            pltpu.sync_copy(rows_hbm.at[pl.ds(off, SC_CHUNK)], rows_v)
            for slot in range(2):
                pltpu.sync_copy(dest_hbm.at[pl.ds(slot * t + off, SC_CHUNK)], idx_v)
                pltpu.sync_copy(rows_v, out_hbm.at[idx_v])

    return dispatch(rows, dest)


def _moe_kernel(be_ref, nu_ref, x_ref, wg_ref, wu_ref, wd_ref, y_ref, wg_bf, wu_bf, wd_bf):
    i = pl.program_id(0)

    @pl.when((i == 0) | (be_ref[i] != be_ref[jnp.maximum(i - 1, 0)]))
    def _():
        _cast_rows(wg_ref, wg_bf)
        _cast_rows(wu_ref, wu_bf)
        _cast_rows(wd_ref, wd_bf)

    @pl.when(i < nu_ref[0])
    def _():
        xb = _unpack_halves(x_ref[...]).astype(BF16)
        a = jnp.dot(xb, wg_bf[...], preferred_element_type=F32)
        u = jnp.dot(xb, wu_bf[...], preferred_element_type=F32)
        y = jnp.dot((_silu(a) * u).astype(BF16), wd_bf[...], preferred_element_type=F32)
        y_ref[...] = _pack_halves(y)

    @pl.when(i >= nu_ref[0])
    def _():
        y_ref[...] = jnp.zeros_like(y_ref)


def _moe(blk_expert, n_used, x_sorted, w_gate, w_up, w_down):
    n_blocks = blk_expert.shape[0]
    wspec = lambda shape: pl.BlockSpec((None,) + shape, lambda i, be, nu: (be[i], 0, 0))
    rows = pl.BlockSpec((MOE_ROWS, HALF), lambda i, be, nu: (i, 0))
    grid_spec = pltpu.PrefetchScalarGridSpec(
        num_scalar_prefetch=2,
        grid=(n_blocks,),
        in_specs=[rows, wspec((D_MODEL, D_FF)), wspec((D_MODEL, D_FF)), wspec((D_FF, D_MODEL))],
        out_specs=rows,
        scratch_shapes=[pltpu.VMEM((D_MODEL, D_FF), BF16), pltpu.VMEM((D_MODEL, D_FF), BF16),
                        pltpu.VMEM((D_FF, D_MODEL), BF16)],
    )
    return pl.pallas_call(
        _moe_kernel,
        grid_spec=grid_spec,
        out_shape=jax.ShapeDtypeStruct(x_sorted.shape, U32),
        compiler_params=pltpu.CompilerParams(
            dimension_semantics=("arbitrary",), vmem_limit_bytes=VMEM_LIMIT),
        name="moe",
    )(blk_expert, n_used, x_sorted, w_gate, w_up, w_down)


def _combine_kernel(h2_ref, rw_ref, g_ref, y0_ref, y1_ref, o_ref):
    rw = rw_ref[...]
    h = h2_ref[...] + rw[:, 0:1] * _unpack_halves(y0_ref[...]) + rw[:, 1:2] * _unpack_halves(y1_ref[...])
    o_ref[...] = h * lax.rsqrt(jnp.mean(h * h, axis=-1, keepdims=True) + EPS) * g_ref[...]


def _combine(h2, rweights, final_g, y):
    t = h2.shape[0]
    tm = min(COMBINE_TM, t)
    row = lambda i: (i, 0)
    return pl.pallas_call(
        _combine_kernel,
        grid=(t // tm,),
        in_specs=[pl.BlockSpec((tm, D_MODEL), row),
                  pl.BlockSpec((tm, LANES), row),
                  pl.BlockSpec((1, D_MODEL), lambda i: (0, 0)),
                  pl.BlockSpec((tm, HALF), row),
                  pl.BlockSpec((tm, HALF), lambda i: (i + t // tm, 0))],
        out_specs=pl.BlockSpec((tm, D_MODEL), row),
        out_shape=jax.ShapeDtypeStruct((t, D_MODEL), F32),
        compiler_params=pltpu.CompilerParams(
            dimension_semantics=("arbitrary",), vmem_limit_bytes=VMEM_LIMIT),
        name="combine",
    )(h2, rweights, final_g, y, y)


def _block_layout(ids, counts_row, t):
    n_blocks = 2 * t // MOE_ROWS + N_EXPERTS
    counts = counts_row[0, ROUTE_LANE0:ROUTE_LANE0 + N_EXPERTS].astype(jnp.int32)
    padded = ((counts + MOE_ROWS - 1) // MOE_ROWS) * MOE_ROWS
    pend = jnp.cumsum(padded)
    pstart = pend - padded
    blk_first = jnp.arange(n_blocks, dtype=jnp.int32) * MOE_ROWS
    blk_expert = jnp.minimum(jnp.sum((pend[None, :] <= blk_first[:, None]).astype(jnp.int32), axis=1),
                             N_EXPERTS - 1).astype(jnp.int32)
    n_used = (pend[-1] // MOE_ROWS).astype(jnp.int32).reshape(1)
    expert, rank = ids[2:4], ids[4:6]
    experts = jnp.arange(N_EXPERTS, dtype=jnp.int32)
    first_row = jnp.sum(jnp.where(expert[:, :, None] == experts, pstart, 0), axis=-1)
    dest = (first_row + rank).reshape(2 * t)
    return dest, blk_expert, n_used, n_blocks * MOE_ROWS


def kernel(x, meta_tokens, hg_lb_logits, norm_mix_g, w_in, gd_conv_w, gd_A_log, gd_dt_bias, hg_norm_g, gd_norm_g, hg_up, gd_up, w_out, norm_ffn_g, router_group_w, router_group_b, router_expert_w, router_expert_b, w_gate, w_up, w_down, final_norm_g):
    bsz, seq, d = x.shape
    t = bsz * seq
    x2d = x.reshape(t, d)

    lb = jnp.cumsum(jax.nn.softmax(hg_lb_logits.astype(F32), axis=0), axis=0)[0].reshape(1, WIDTH)
    w_all = w_in.astype(F32)
    w_gates = w_all[0, :, W_SMALL + 2 * HEADS:]
    g_mix = norm_mix_g[0].reshape(1, d)
    meta_blk = jnp.concatenate([jnp.zeros((CHUNK - N_META, d), F32), meta_tokens.astype(F32)], axis=0)

    conv_w = gd_conv_w[0].astype(F32)
    no_history = jnp.zeros((SUBLANES, 3 * WIDTH), F32)
    proj, proj_meta = {}, {}
    proj["hgrn"] = _norm_proj_hgrn(x2d, g_mix, lb, w_all, PROJ_TM)
    proj_meta["hgrn"] = _norm_proj_hgrn(meta_blk, g_mix, lb, w_all, CHUNK)
    proj_meta["gdn"], meta_tail, logits_meta = _norm_proj_conv(meta_blk, g_mix, w_all, conv_w, no_history,
                                                               CHUNK, CHUNK)
    proj["gdn"], _, logits = _norm_proj_conv(x2d, g_mix, w_all, conv_w, meta_tail, PROJ_TM, seq)

    def scalar_rows(lg, nb, nc):
        s = lg.reshape(2, N_STACKS, HEADS_PER_STACK, nb, nc, CHUNK)
        return s.transpose(3, 4, 1, 0, 2, 5).reshape(nb, nc, N_STACKS, 2, STACK)

    a_log_row = jnp.repeat(gd_A_log[0].astype(F32), CHUNK).reshape(N_STACKS, 1, STACK)
    dt_row = jnp.repeat(gd_dt_bias[0].astype(F32), CHUNK).reshape(N_STACKS, 1, STACK)
    hg_g, gd_g = hg_norm_g[0].reshape(1, HEAD_DIM), gd_norm_g[0].reshape(1, HEAD_DIM)

    zero_state = jnp.zeros((HEADS, HEAD_DIM, HEAD_DIM), F32)
    per_seq = lambda st: jnp.tile(st, (bsz, 1, 1))
    _, hg_state = _hgrn(proj_meta["hgrn"][None], zero_state, hg_g, 1)
    o_a, _ = _hgrn(proj["hgrn"].reshape(bsz, seq, -1), per_seq(hg_state), hg_g, HGRN_CHUNKS)
    _, gd_state = _gdn(proj_meta["gdn"][None], zero_state, scalar_rows(logits_meta, 1, 1),
                       a_log_row, dt_row, gd_g, 1)
    o_b, _ = _gdn(proj["gdn"].reshape(bsz, seq, -1), per_seq(gd_state),
                  scalar_rows(logits, bsz, seq // CHUNK), a_log_row, dt_row, gd_g, GDN_CHUNKS)

    rw = jnp.zeros((d, LANES), F32)
    rw = rw.at[:, :N_GROUPS].set(router_group_w[0]).at[:, N_GROUPS:N_GROUPS + N_EXPERTS].set(router_expert_w[0])
    rb = jnp.zeros((1, LANES), F32)
    rb = rb.at[0, :N_GROUPS].set(router_group_b[0]).at[0, N_GROUPS:N_GROUPS + N_EXPERTS].set(router_expert_b[0])
    h2, xn, info, ids, counts = _merge(o_a.reshape(t, WIDTH), o_b.reshape(t, WIDTH), x2d, g_mix, w_gates,
                                  hg_up[0].astype(BF16), gd_up[0].astype(BF16), w_out[0].astype(BF16),
                                  norm_ffn_g[0].reshape(1, d), rw.astype(BF16), rb)

    dest, blk_expert, n_used, n_rows = _block_layout(ids, counts, t)
    x_sorted = _sc_dispatch(xn, dest, n_rows)
    y_sorted = _moe(blk_expert, n_used, x_sorted,
                    w_gate[0].astype(F32), w_up[0].astype(F32), w_down[0].astype(F32))
    y_tok = _sc_gather(y_sorted, dest)
    out = _combine(h2, info, final_norm_g.reshape(1, d), y_tok)
    return out.reshape(bsz, seq, d)
```

```python
import functools

import numpy as np
import jax
import jax.numpy as jnp
from jax import lax
from jax.experimental import pallas as pl
from jax.experimental.pallas import tpu as pltpu
from jax.experimental.pallas import tpu_sc as plsc

F32 = jnp.float32
BF16 = jnp.bfloat16
U32 = jnp.uint32

D_MODEL = 1024
N_META = 16
CHUNK_BITS = 6
CHUNK = 1 << CHUNK_BITS
EPS = 1e-6
HEADS = 4
HEAD_DIM = 128
WIDTH = HEADS * HEAD_DIM
HEADS_PER_STACK = 2
STACK = HEADS_PER_STACK * CHUNK
N_STACKS = HEADS // HEADS_PER_STACK
CONV_W = 4
NEUMANN_BITS = 3
N_GROUPS = 4
EXPERTS_PER_GROUP = 8
N_EXPERTS = N_GROUPS * EXPERTS_PER_GROUP
D_FF = 512
LANES = 128
SUBLANES = 8
HALF = D_MODEL // 2
MOE_ROWS = 512
CAST_SLAB = 128
SC_CORES, SC_SUBCORES = 2, 16
SC_WORKERS = SC_CORES * SC_SUBCORES
SC_CHUNK = 128
PROJ_TM = 1024
W_HGRN, W_GDN, W_SMALL = 0, 4 * WIDTH, 8 * WIDTH
HGRN_CHUNKS, GDN_CHUNKS = 4, 2
TOKEN_TM = 512
COMBINE_TM = 1024
VMEM_LIMIT = 56 * 1024 * 1024


def _bdot(a, b):
    return lax.dot_general(a.astype(BF16), b.astype(BF16), (((2,), (1,)), ((0,), (0,))),
                           preferred_element_type=F32)


def _bdot_nt(a, b):
    return lax.dot_general(a.astype(BF16), b.astype(BF16), (((2,), (2,)), ((0,), (0,))),
                           preferred_element_type=F32)


def _bdot_tn(a, b):
    return lax.dot_general(a.astype(BF16), b.astype(BF16), (((1,), (1,)), ((0,), (0,))),
                           preferred_element_type=F32)


def _split3(x):
    hi = x.astype(BF16)
    r1 = x - hi.astype(F32)
    mid = r1.astype(BF16)
    lo = (r1 - mid.astype(F32)).astype(BF16)
    return hi, mid, lo


def _sigmoid(x):
    return 1.0 / (1.0 + jnp.exp(-x))


def _silu(x):
    return x * _sigmoid(x)


def _softplus(x):
    return jnp.maximum(x, 0.0) + jnp.log(1.0 + jnp.exp(-jnp.abs(x)))


def _stack_heads(x, bsz):
    x3 = x.reshape(bsz, CHUNK, WIDTH)
    y = jnp.concatenate([x3[:, :, h * HEAD_DIM:(h + 1) * HEAD_DIM] for h in range(HEADS)], axis=1)
    return y.reshape(bsz * HEADS * CHUNK, HEAD_DIM)


def _unstack_heads(x, bsz):
    x3 = x.reshape(bsz, HEADS * CHUNK, HEAD_DIM)
    return jnp.concatenate([x3[:, h * CHUNK:(h + 1) * CHUNK, :] for h in range(HEADS)], axis=2)


def _to_stacks(a, bsz):
    return a.reshape(bsz * N_STACKS, STACK, a.shape[-1])


def _to_heads(a, bsz):
    return a.reshape(bsz * HEADS, CHUNK, a.shape[-1])


def _roll_rows(x, shift):
    n = x.shape[0]
    return pltpu.roll(x, shift % n, axis=0)


def _chunks_first(x, nc):
    return jnp.concatenate([x[:, j * CHUNK:(j + 1) * CHUNK] for j in range(nc)], axis=0)


def _chunks_first_scalars(s, nc):
    return jnp.concatenate([s[:, j] for j in range(nc)], axis=0)


def _load_chunks(ref, nc):
    x = _chunks_first(ref[...], nc)
    return x.reshape(x.shape[0] * CHUNK, WIDTH)


def _store_chunks(ref, x, nb, nc):
    ref[...] = jnp.concatenate([x[j * nb:(j + 1) * nb] for j in range(nc)], axis=1).astype(ref.dtype)


def _cast_rows(src_ref, dst_ref, col0=0):
    slab = CAST_SLAB
    rows, cols = src_ref.shape
    assert rows % slab == 0

    def body(k, carry):
        r = pl.multiple_of(k * slab, slab)
        dst_ref[pl.ds(r, slab), col0:col0 + cols] = src_ref[pl.ds(r, slab), :].astype(dst_ref.dtype)
        return carry

    lax.fori_loop(0, rows // slab, body, 0)


def _pack_halves(x):
    lo = lax.bitcast_convert_type(x[:, :HALF].astype(BF16).astype(F32), U32)
    hi = lax.bitcast_convert_type(x[:, HALF:].astype(BF16).astype(F32), U32)
    return (lo >> 16) | (hi & jnp.uint32(0xFFFF0000))


def _unpack_halves(p):
    lo = lax.bitcast_convert_type(p << 16, F32)
    hi = lax.bitcast_convert_type(p & jnp.uint32(0xFFFF0000), F32)
    return jnp.concatenate([lo, hi], axis=1)


def _normed_bf16(x_ref, g_ref):
    x = x_ref[...]
    return (x * lax.rsqrt(jnp.mean(x * x, axis=-1, keepdims=True) + EPS) * g_ref[...]).astype(BF16)


def _norm_proj_hgrn_kernel(x_ref, g_ref, lb_ref, w_ref, o_ref, wbf_ref):
    @pl.when(pl.program_id(0) == 0)
    def _():
        _cast_rows(w_ref, wbf_ref)

    acc = jnp.dot(_normed_bf16(x_ref, g_ref), wbf_ref[...], preferred_element_type=F32)
    w1, w2, w3 = WIDTH, 2 * WIDTH, 3 * WIDTH
    lb = lb_ref[...]
    o_ref[:, :w1] = _silu(acc[:, :w1])
    o_ref[:, w1:w2] = lb + (1.0 - lb) * _sigmoid(acc[:, w1:w2])
    o_ref[:, w2:w3] = acc[:, w2:w3]
    o_ref[:, w3:] = _silu(acc[:, w3:])


def _zero_rows(ref):
    slab = min(CAST_SLAB, ref.shape[0])
    assert ref.shape[0] % slab == 0

    def body(k, carry):
        ref[pl.ds(pl.multiple_of(k * slab, slab), slab), :] = jnp.zeros((slab, ref.shape[1]), ref.dtype)
        return carry

    lax.fori_loop(0, ref.shape[0] // slab, body, 0)


def _norm_proj_conv_kernel(x_ref, g_ref, w_ref, ws_ref, cw_ref, hist_ref, o_ref, tail_ref, logit_ref,
                           wbf_ref, acc_ref, carry_ref, *, steps_per_seq):
    i = pl.program_id(0)

    @pl.when(i == 0)
    def _():
        _cast_rows(w_ref, wbf_ref)
        _cast_rows(ws_ref, wbf_ref, 4 * WIDTH)
        _zero_rows(acc_ref)
        carry_ref[...] = jnp.zeros_like(carry_ref)

    xn = _normed_bf16(x_ref, g_ref)
    w3, w4 = 3 * WIDTH, 4 * WIDTH
    rows = x_ref.shape[0]
    first_of_seq = (i - 1) % steps_per_seq == 0
    cw = cw_ref[...]
    r8 = lax.broadcasted_iota(jnp.int32, (SUBLANES, WIDTH), 0)
    for c0 in range(0, w3, WIDTH):
        cols = slice(c0, c0 + WIDTH)
        raw = acc_ref[:, cols]
        hist = jnp.where(first_of_seq, hist_ref[:, cols], carry_ref[:, cols])
        tap = lambda d: cw[CONV_W - 1 - d:CONV_W - d, cols]
        body = raw * tap(0)
        head = raw[:SUBLANES]
        first = head * tap(0)
        for d in range(1, CONV_W):
            body = body + _roll_rows(raw, d) * tap(d)
            first = first + jnp.where(r8 >= d, _roll_rows(head, d), _roll_rows(hist, d)) * tap(d)
        o_ref[:SUBLANES, cols] = _silu(first)
        o_ref[SUBLANES:, cols] = _silu(body[SUBLANES:])
        carry_ref[:, cols] = raw[rows - SUBLANES:]
        acc_ref[:, cols] = jnp.dot(xn, wbf_ref[:, cols], preferred_element_type=F32)
    rest = acc_ref[:, w3:]
    o_ref[:, w3:] = _silu(rest[:, :WIDTH])
    logit_ref[...] = rest[:, WIDTH:].T[:SUBLANES]
    acc_ref[:, w3:] = jnp.dot(xn, wbf_ref[:, w3:], preferred_element_type=F32)
    tail_ref[...] = carry_ref[...]


def _proj_call(kernel_fn, name, x2d, tm, n, operands, specs, extra_out=(), extra_scratch=(),
               weight_cols=None, delayed=False):
    m, k = x2d.shape
    weight_cols = n if weight_cols is None else weight_cols
    tm = min(tm, m)
    tiles = m // tm
    x_map = (lambda i: (jnp.minimum(i, tiles - 1), 0)) if delayed else (lambda i: (i, 0))
    o_map = (lambda i: (jnp.maximum(i - 1, 0), 0)) if delayed else (lambda i: (i, 0))
    out_specs = [pl.BlockSpec((tm, n), o_map)] + [s for s, _ in extra_out]
    out_shape = [jax.ShapeDtypeStruct((m, n), F32)] + [s for _, s in extra_out]
    return pl.pallas_call(
        kernel_fn,
        grid=(tiles + int(delayed),),
        in_specs=[pl.BlockSpec((tm, k), x_map)] + specs,
        out_specs=out_specs,
        out_shape=out_shape,
        scratch_shapes=[pltpu.VMEM((k, weight_cols), BF16)] + list(extra_scratch),
        compiler_params=pltpu.CompilerParams(
            dimension_semantics=("arbitrary",), vmem_limit_bytes=VMEM_LIMIT),
        name=name,
    )(x2d, *operands)


def _weight_spec(k, n, col0):
    assert col0 % n == 0
    return pl.BlockSpec((None, k, n), lambda i: (0, 0, col0 // n), pipeline_mode=pl.Buffered(1))


def _norm_proj_hgrn(x2d, g, lb, w, tm):
    k, n = w.shape[1], 4 * WIDTH
    const = lambda i: (0, 0)
    specs = [pl.BlockSpec((1, k), const), pl.BlockSpec((1, WIDTH), const), _weight_spec(k, n, W_HGRN)]
    return _proj_call(_norm_proj_hgrn_kernel, "norm_proj_hgrn", x2d, tm, n, (g, lb, w), specs)[0]


def _norm_proj_conv(x2d, g, w, conv_w, hist, tm, rows_per_seq):
    m, k = x2d.shape[0], w.shape[1]
    n_out, n_w = 4 * WIDTH, 4 * WIDTH + LANES
    const = lambda i: (0, 0)
    tm = min(tm, rows_per_seq)
    assert rows_per_seq % tm == 0 and m % rows_per_seq == 0
    tail_shape = (SUBLANES, 3 * WIDTH)
    specs = [pl.BlockSpec((1, k), const), _weight_spec(k, 4 * WIDTH, W_GDN), _weight_spec(k, LANES, W_SMALL),
             pl.BlockSpec((CONV_W, 3 * WIDTH), const), pl.BlockSpec(tail_shape, const)]
    logit_spec = pl.BlockSpec((SUBLANES, tm), lambda i: (0, jnp.maximum(i - 1, 0)))
    return _proj_call(functools.partial(_norm_proj_conv_kernel, steps_per_seq=rows_per_seq // tm),
                      "norm_proj_gdn", x2d, tm, n_out, (g, w, w, conv_w, hist), specs,
                      extra_out=[(pl.BlockSpec(tail_shape, const), jax.ShapeDtypeStruct(tail_shape, F32)),
                                 (logit_spec, jax.ShapeDtypeStruct((SUBLANES, m), F32))],
                      extra_scratch=[pltpu.VMEM((tm, n_w), F32), pltpu.VMEM(tail_shape, F32)],
                      weight_cols=n_w, delayed=True)


HGRN_LEVELS = 6
HGRN_DIAG = HGRN_LEVELS


def _hgrn_level_table():
    r = np.arange(STACK)
    t, hd = r % CHUNK, r // CHUNK
    x = t[:, None] ^ t[None, :]
    lv = np.floor(np.log2(np.maximum(x, 1))).astype(np.int32)
    valid = (hd[:, None] == hd[None, :]) & (t[:, None] > t[None, :])
    lv = np.where(valid, lv, -1)
    lv = np.where(r[:, None] == r[None, :], HGRN_DIAG, lv)
    return jnp.asarray(lv, dtype=jnp.int32)


def _drain(stages):
    for _ in stages:
        pass


def _weave(*bodies):
    live = list(bodies)
    while live:
        for body in list(live):
            try:
                next(body)
            except StopIteration:
                live.remove(body)


def _hgrn_kernel(*refs, nb, nc):
    _drain(_hgrn_stages(*refs, nb=nb, nc=nc))


def _hgrn_stages(lv_ref, g_ref, st0_ref, q_ref, f_ref, v_ref, gate_ref, o_ref, stout_ref, st_ref, *, nb, nc):
    @pl.when(pl.program_id(0) == 0)
    def _():
        st_ref[...] = st0_ref[...]

    bsz = nc * nb
    q = _load_chunks(q_ref, nc)
    f = _load_chunks(f_ref, nc)
    v = _load_chunks(v_ref, nc)
    gate = _load_chunks(gate_ref, nc)

    per_seq = lambda a: a.reshape(bsz, CHUNK, WIDTH)
    trow = lax.broadcasted_iota(jnp.int32, (1, CHUNK, WIDTH), 1)
    b = jnp.log2(f)
    s = 1
    while s < CHUNK:
        b = (per_seq(b) + jnp.where(trow >= s, per_seq(_roll_rows(b, s)), 0.0)).reshape(b.shape)
        s *= 2
        if s in (4, 32):
            yield

    qs, ks, vs, bs = (_stack_heads(a, bsz) for a in (q, 1.0 - f, v, b))
    lv = lv_ref[...][None]
    to3 = functools.partial(_to_stacks, bsz=bsz)
    to_g = functools.partial(_to_heads, bsz=bsz)
    roll_g = lambda a, shift: to_g(_roll_rows(a.reshape(qs.shape), shift))
    t = lax.broadcasted_iota(jnp.int32, (1, CHUNK, HEAD_DIM), 1)
    qs_g, ks_g, bs_g = to_g(qs), to_g(ks), to_g(bs)

    attn = jnp.where(lv == HGRN_DIAG, _bdot_nt(to3(qs), to3(ks)), 0.0)
    bref = bs_g
    bnext = roll_g(bs_g, -1)
    yield
    for li in range(HGRN_LEVELS):
        m = 1 << li
        if li > 0:
            half = m // 2
            upper = (t & (m - 1)) >= half
            if half < SUBLANES:
                grp = lambda a: a.reshape(-1, SUBLANES, HEAD_DIM)
                local = lambda a, sh: pltpu.roll(grp(a), sh % SUBLANES, axis=1).reshape(a.shape)
                bref = jnp.where(upper, local(bref, half), bref)
                bnext = jnp.where(upper, bnext, local(bnext, -half))
            else:
                bref = jnp.where(upper, roll_g(bref, half), bref)
                bnext = jnp.where(upper, bnext, roll_g(bnext, -half))
        odd = ((t >> li) & 1) == 1
        both = jnp.where(odd, qs_g, ks_g) * jnp.exp2(jnp.where(odd, bs_g - bref, bnext - bs_g))
        both = to3(both.reshape(qs.shape)).astype(BF16)
        attn = jnp.where(lv == li, _bdot_nt(both, both), attn)
        yield

    o = _bdot(attn, to3(vs))

    n_g = nb * HEADS
    st = st_ref[...]
    qe = qs_g * jnp.exp2(bs_g)
    b_end = bs_g[:, CHUNK - 1:CHUNK, :]
    ke = ks_g * jnp.exp2(b_end - bs_g)
    vs_g = to_g(vs)
    o_inter = []
    for j in range(nc):
        sl = slice(j * n_g, (j + 1) * n_g)
        o_inter.append(_bdot_nt(qe[sl], st))
        st = st * jnp.exp2(b_end[sl]) + _bdot_tn(vs_g[sl], ke[sl])
        yield
    st_ref[...] = st
    stout_ref[...] = st
    o = (to_g(o) + jnp.concatenate(o_inter, axis=0)).reshape(qs.shape)

    o = o * lax.rsqrt(jnp.mean(o * o, axis=-1, keepdims=True) + EPS) * g_ref[...]
    o = o * _stack_heads(gate, bsz)
    _store_chunks(o_ref, _unstack_heads(o, bsz), nb, nc)


def _gdn_cumsum_matrix():
    r = np.arange(STACK)
    u = (r[:, None] // CHUNK == r[None, :] // CHUNK) & (r[:, None] <= r[None, :])
    return jnp.asarray(u, dtype=BF16)


def _gdn_kernel(*refs, nb, nc):
    _drain(_gdn_stages(*refs, nb=nb, nc=nc))


def _gdn_stages(u_ref, alog_ref, dt_ref, g_ref, s_ref, st0_ref, q_ref, k_ref, v_ref, z_ref,
                o_ref, stout_ref, st_ref, *, nb, nc):
    @pl.when(pl.program_id(0) == 0)
    def _():
        st_ref[...] = st0_ref[...]

    bsz = nc * nb
    qs = _stack_heads(_load_chunks(q_ref, nc), bsz)
    ks = _stack_heads(_load_chunks(k_ref, nc), bsz)
    vs = _stack_heads(_load_chunks(v_ref, nc), bsz)
    gate = _load_chunks(z_ref, nc)
    qs = qs * lax.rsqrt(jnp.sum(qs * qs, axis=-1, keepdims=True) + EPS) * (HEAD_DIM ** -0.5)
    ks = ks * lax.rsqrt(jnp.sum(ks * ks, axis=-1, keepdims=True) + EPS)
    to3 = functools.partial(_to_stacks, bsz=bsz)
    qs, ks, vs = to3(qs), to3(ks), to3(vs)
    n_st = bsz * N_STACKS
    yield

    srow = _chunks_first_scalars(s_ref[...], nc).reshape(n_st, 2, STACK)
    per_stack = lambda ref: jnp.broadcast_to(ref[...][None], (bsz, N_STACKS, 1, STACK)).reshape(n_st, 1, STACK)
    beta_row = _sigmoid(srow[:, 0:1, :])
    g_row = -jnp.exp(per_stack(alog_ref)) * _softplus(srow[:, 1:2, :] + per_stack(dt_ref))
    n8 = n_st * SUBLANES
    g8 = jnp.broadcast_to(g_row, (n_st, SUBLANES, STACK)).reshape(n8, STACK)
    pieces = jnp.concatenate(_split3(g8), axis=0)
    cum = jnp.dot(pieces, u_ref[...], preferred_element_type=F32)
    cum = cum[0:n8] + cum[n8:2 * n8] + cum[2 * n8:]
    cum_row = cum.reshape(n_st, SUBLANES, STACK)[:, 0:1, :]

    r = lax.broadcasted_iota(jnp.int32, (STACK, STACK), 0)
    cc = lax.broadcasted_iota(jnp.int32, (STACK, STACK), 1)
    eye = (r == cc)[None]
    cum_col = jnp.sum(jnp.where(eye, cum_row, 0.0), axis=2, keepdims=True)
    beta_col = jnp.sum(jnp.where(eye, beta_row, 0.0), axis=2, keepdims=True)
    same = (r >> CHUNK_BITS) == (cc >> CHUNK_BITS)
    lower = (same & (r >= cc))[None]
    strict = (same & (r > cc))[None]
    ratio = jnp.exp(jnp.minimum(cum_col - cum_row, 0.0))

    kkt = _bdot_nt(ks, ks)
    qkt = _bdot_nt(qs, ks)
    lm = jnp.where(strict, beta_col * ratio * kkt, 0.0)
    yield

    ident = eye.astype(F32)
    l0 = jnp.where(((r >> NEUMANN_BITS) == (cc >> NEUMANN_BITS))[None], lm, 0.0)
    l2 = _bdot(l0, l0)
    yield
    l4 = _bdot(l2, l2)
    yield
    xinv = _bdot(_bdot(ident - l0, ident + l2), ident + l4)
    yield
    for bits in range(NEUMANN_BITS, CHUNK_BITS):
        off = (((r >> (bits + 1)) == (cc >> (bits + 1))) & ((r >> bits) != (cc >> bits)))[None]
        xinv = xinv - _bdot(_bdot(xinv, jnp.where(off, lm, 0.0)), xinv)
        yield

    decay_col = jnp.exp(cum_col)
    rhs = jnp.concatenate([beta_col * vs, (beta_col * decay_col) * ks], axis=2)
    sol = _bdot(xinv, rhs)

    to_g = functools.partial(_to_heads, bsz=bsz)
    n_g = nb * HEADS
    st = st_ref[...]
    u0_g, w_g, q_g = to_g(sol[:, :, :HEAD_DIM]), to_g(sol[:, :, HEAD_DIM:]), to_g(qs)
    cum_g = to_g(cum_col)
    cum_end = cum_g[:, CHUNK - 1:CHUNK, :]
    kdec = to_g(ks) * jnp.exp(cum_end - cum_g)
    decay_g = to_g(decay_col)
    us, o_inter = [], []
    for j in range(nc):
        sl = slice(j * n_g, (j + 1) * n_g)
        u_j = u0_g[sl] - _bdot(w_g[sl], st)
        us.append(u_j)
        o_inter.append(decay_g[sl] * _bdot(q_g[sl], st))
        st = jnp.exp(cum_end[sl]) * st + _bdot_tn(kdec[sl], u_j)
        yield
    st_ref[...] = st
    stout_ref[...] = st
    attn = jnp.where(lower, qkt * ratio, 0.0)
    u = jnp.concatenate(us, axis=0).reshape(bsz * HEADS * CHUNK, HEAD_DIM)
    o = jnp.concatenate(o_inter, axis=0) + to_g(_bdot(attn, to3(u)))
    o = o.reshape(bsz * HEADS * CHUNK, HEAD_DIM)

    o = o * lax.rsqrt(jnp.mean(o * o, axis=-1, keepdims=True) + EPS) * g_ref[...]
    o = o * _stack_heads(gate, bsz)
    _store_chunks(o_ref, _unstack_heads(o, bsz), nb, nc)


def _recurrence_call(kernel_fn, name, proj, state0, operands, specs, nc):
    nb, length, _ = proj.shape
    rows = nc * CHUNK
    state_shape = (nb * HEADS, HEAD_DIM, HEAD_DIM)
    state_spec = pl.BlockSpec(state_shape, lambda c: (0, 0, 0))
    chunk_specs = [pl.BlockSpec((nb, rows, WIDTH), functools.partial(lambda c, j: (0, c, j), j=j))
                   for j in range(4)]
    return pl.pallas_call(
        functools.partial(kernel_fn, nb=nb, nc=nc),
        grid=(length // rows,),
        in_specs=specs + [state_spec] + chunk_specs,
        out_specs=[pl.BlockSpec((nb, rows, WIDTH), lambda c: (0, c, 0)), state_spec],
        out_shape=[jax.ShapeDtypeStruct((nb, length, WIDTH), BF16),
                   jax.ShapeDtypeStruct(state_shape, F32)],
        scratch_shapes=[pltpu.VMEM(state_shape, F32)],
        compiler_params=pltpu.CompilerParams(
            dimension_semantics=("arbitrary",), vmem_limit_bytes=VMEM_LIMIT),
        name=name,
    )(*operands, state0, *([proj] * 4))


def _hgrn(proj, state0, norm_g, nc):
    const = lambda c: (0, 0)
    specs = [pl.BlockSpec((STACK, STACK), const), pl.BlockSpec((1, HEAD_DIM), const)]
    return _recurrence_call(_hgrn_kernel, "hgrn", proj, state0, (_hgrn_level_table(), norm_g), specs, nc)


def _gdn(proj, state0, scal, a_log_row, dt_row, norm_g, nc):
    nb = proj.shape[0]
    const = lambda c: (0, 0)
    const3 = lambda c: (0, 0, 0)
    specs = [pl.BlockSpec((STACK, STACK), const),
             pl.BlockSpec((N_STACKS, 1, STACK), const3),
             pl.BlockSpec((N_STACKS, 1, STACK), const3),
             pl.BlockSpec((1, HEAD_DIM), const),
             pl.BlockSpec((nb, nc, N_STACKS, 2, STACK), lambda c: (0, c, 0, 0, 0))]
    return _recurrence_call(_gdn_kernel, "gdn", proj, state0,
                            (_gdn_cumsum_matrix(), a_log_row, dt_row, norm_g, scal), specs, nc)


def _mixers_kernel(lv_ref, hg_ref, u_ref, alog_ref, dt_ref, gg_ref, s_ref, hst0_ref, gst0_ref,
                   hq_ref, hf_ref, hv_ref, hgate_ref, gq_ref, gk_ref, gv_ref, gz_ref,
                   ho_ref, hstout_ref, go_ref, gstout_ref, hst_ref, gst_ref, *, nb, nc):
    _weave(_hgrn_stages(lv_ref, hg_ref, hst0_ref, hq_ref, hf_ref, hv_ref, hgate_ref,
                        ho_ref, hstout_ref, hst_ref, nb=nb, nc=nc),
           _gdn_stages(u_ref, alog_ref, dt_ref, gg_ref, s_ref, gst0_ref, gq_ref, gk_ref, gv_ref, gz_ref,
                       go_ref, gstout_ref, gst_ref, nb=nb, nc=nc))


def _mixers(proj_h, proj_g, state_h, state_g, scal, a_log_row, dt_row, hg_g, gd_g, nc):
    nb, length, _ = proj_h.shape
    rows = nc * CHUNK
    const = lambda c: (0, 0)
    const3 = lambda c: (0, 0, 0)
    state_shape = (nb * HEADS, HEAD_DIM, HEAD_DIM)
    state_spec = pl.BlockSpec(state_shape, const3)
    chunk_specs = [pl.BlockSpec((nb, rows, WIDTH), functools.partial(lambda c, j: (0, c, j), j=j))
                   for j in range(4)]
    out_spec = pl.BlockSpec((nb, rows, WIDTH), lambda c: (0, c, 0))
    specs = [pl.BlockSpec((STACK, STACK), const), pl.BlockSpec((1, HEAD_DIM), const),
             pl.BlockSpec((STACK, STACK), const),
             pl.BlockSpec((N_STACKS, 1, STACK), const3),
             pl.BlockSpec((N_STACKS, 1, STACK), const3),
             pl.BlockSpec((1, HEAD_DIM), const),
             pl.BlockSpec((nb, nc, N_STACKS, 2, STACK), lambda c: (0, c, 0, 0, 0))]
    o_a, _, o_b, _ = pl.pallas_call(
        functools.partial(_mixers_kernel, nb=nb, nc=nc),
        grid=(length // rows,),
        in_specs=specs + [state_spec, state_spec] + chunk_specs + chunk_specs,
        out_specs=[out_spec, state_spec, out_spec, state_spec],
        out_shape=[jax.ShapeDtypeStruct((nb, length, WIDTH), BF16),
                   jax.ShapeDtypeStruct(state_shape, F32),
                   jax.ShapeDtypeStruct((nb, length, WIDTH), BF16),
                   jax.ShapeDtypeStruct(state_shape, F32)],
        scratch_shapes=[pltpu.VMEM(state_shape, F32), pltpu.VMEM(state_shape, F32)],
        compiler_params=pltpu.CompilerParams(
            dimension_semantics=("arbitrary",), vmem_limit_bytes=VMEM_LIMIT),
        name="mixers",
    )(_hgrn_level_table(), hg_g, _gdn_cumsum_matrix(), a_log_row, dt_row, gd_g, scal,
      state_h, state_g, *([proj_h] * 4), *([proj_g] * 4))
    return o_a, o_b


ROUTE_LANE0 = N_GROUPS
ROUTE_NEG = -1e30


def _route_tile(lg, ltri, carry):
    lane = lax.broadcasted_iota(jnp.int32, lg.shape, 1).astype(F32)
    first = lambda mask: jnp.min(jnp.where(mask, lane, float(LANES)), axis=1, keepdims=True)
    top = lambda mask: jnp.max(jnp.where(mask, lg, ROUTE_NEG), axis=1, keepdims=True)

    is_g = lane < N_GROUPS
    gmax = top(is_g)
    grp = first(is_g & (lg == gmax))
    p_grp = 1.0 / jnp.sum(jnp.where(is_g, jnp.exp(lg - gmax), 0.0), axis=1, keepdims=True)

    lo = ROUTE_LANE0 + grp * EXPERTS_PER_GROUP
    in_grp = (lane >= lo) & (lane < lo + EXPERTS_PER_GROUP)
    m1 = top(in_grp)
    i1 = first(in_grp & (lg == m1))
    rest = in_grp & (lane != i1)
    m2 = top(rest)
    i2 = first(rest & (lg == m2))
    r = jnp.exp(m2 - m1)
    w0 = p_grp / (1.0 + r)
    w1 = p_grp * r / (1.0 + r)

    hot0 = (lane == i1).astype(F32)
    hot1 = (lane == i2).astype(F32)
    both = hot0 + hot1
    before = jnp.dot(ltri, both.astype(BF16), preferred_element_type=F32) + carry
    rank0 = jnp.sum(hot0 * before, axis=1, keepdims=True)
    rank1 = jnp.sum(hot1 * before, axis=1, keepdims=True)
    cols = (w0, w1, i1 - ROUTE_LANE0, i2 - ROUTE_LANE0, rank0, rank1)
    info = jnp.zeros_like(lg)
    for k, col in enumerate(cols):
        info = jnp.where(lane == k, col, info)
    return info, carry + jnp.sum(both, axis=0, keepdims=True)


def _merge_kernel(oa_ref, ob_ref, x_ref, xprev_ref, mg_ref, wgate_ref, hgup_ref, gdup_ref, wout_ref,
                  ng_ref, rw_ref, rb_ref, ltri_ref, h2_ref, xn_ref, info_ref, ids_ref, cnt_ref,
                  wgate_bf, gates_ref, ua_ref, ub_ref, carry_ref):
    i = pl.program_id(0)

    @pl.when(i == 0)
    def _():
        carry_ref[...] = jnp.zeros_like(carry_ref)
        _cast_rows(wgate_ref, wgate_bf)
        _zero_rows(gates_ref)
        _zero_rows(ua_ref)
        _zero_rows(ub_ref)

    gates = _sigmoid(gates_ref[...])
    merged = (gates[:, :D_MODEL] * ua_ref[...] + gates[:, D_MODEL:] * ub_ref[...]).astype(BF16)
    gates_ref[...] = jnp.dot(_normed_bf16(x_ref, mg_ref), wgate_bf[...], preferred_element_type=F32)
    h2 = xprev_ref[...] + jnp.dot(merged, wout_ref[...], preferred_element_type=F32)
    h2_ref[...] = h2
    xn = h2 * lax.rsqrt(jnp.mean(h2 * h2, axis=-1, keepdims=True) + EPS) * ng_ref[...]
    xn_ref[...] = _pack_halves(xn)
    lg = jnp.dot(xn.astype(BF16), rw_ref[...], preferred_element_type=F32) + rb_ref[...]
    ua_ref[...] = jnp.dot(oa_ref[...], hgup_ref[...], preferred_element_type=F32)
    ub_ref[...] = jnp.dot(ob_ref[...], gdup_ref[...], preferred_element_type=F32)
    old = carry_ref[...]
    info, carry = _route_tile(lg, ltri_ref[...], old)
    info_ref[...] = info
    ids_ref[...] = info.T[:SUBLANES].astype(jnp.int32)
    carry = jnp.where(i > 0, carry, old)
    carry_ref[...] = carry
    cnt_ref[...] = carry


def _merge(o_a, o_b, x2d, mix_g, w_gates, hg_up, gd_up, w_out, norm_g, rw, rb):
    t = x2d.shape[0]
    tm = TOKEN_TM
    tiles = t // tm
    new = lambda i: (jnp.minimum(i, tiles - 1), 0)
    row = lambda i: (jnp.maximum(i - 1, 0), 0)
    const = lambda i: (0, 0)
    ltri = jnp.asarray(np.tril(np.ones((tm, tm), np.float32), -1), dtype=BF16)
    return pl.pallas_call(
        _merge_kernel,
        grid=(tiles + 1,),
        in_specs=[pl.BlockSpec((tm, WIDTH), new),
                  pl.BlockSpec((tm, WIDTH), new),
                  pl.BlockSpec((tm, D_MODEL), new),
                  pl.BlockSpec((tm, D_MODEL), row),
                  pl.BlockSpec((1, D_MODEL), const),
                  pl.BlockSpec((D_MODEL, 2 * D_MODEL), const, pipeline_mode=pl.Buffered(1)),
                  pl.BlockSpec((WIDTH, D_MODEL), const),
                  pl.BlockSpec((WIDTH, D_MODEL), const),
                  pl.BlockSpec((D_MODEL, D_MODEL), const),
                  pl.BlockSpec((1, D_MODEL), const),
                  pl.BlockSpec((D_MODEL, LANES), const),
                  pl.BlockSpec((1, LANES), const),
                  pl.BlockSpec((tm, tm), const)],
        out_specs=[pl.BlockSpec((tm, D_MODEL), row),
                   pl.BlockSpec((tm, HALF), row),
                   pl.BlockSpec((tm, LANES), row),
                   pl.BlockSpec((SUBLANES, tm), lambda i: (0, jnp.maximum(i - 1, 0))),
                   pl.BlockSpec((1, LANES), const)],
        out_shape=[jax.ShapeDtypeStruct((t, D_MODEL), F32),
                   jax.ShapeDtypeStruct((t, HALF), U32),
                   jax.ShapeDtypeStruct((t, LANES), F32),
                   jax.ShapeDtypeStruct((SUBLANES, t), jnp.int32),
                   jax.ShapeDtypeStruct((1, LANES), F32)],
        scratch_shapes=[pltpu.VMEM((D_MODEL, 2 * D_MODEL), BF16), pltpu.VMEM((tm, 2 * D_MODEL), F32),
                        pltpu.VMEM((tm, D_MODEL), F32), pltpu.VMEM((tm, D_MODEL), F32),
                        pltpu.VMEM((1, LANES), F32)],
        compiler_params=pltpu.CompilerParams(
            dimension_semantics=("arbitrary",), vmem_limit_bytes=VMEM_LIMIT),
        name="merge",
    )(o_a, o_b, x2d, x2d, mix_g, w_gates, hg_up, gd_up, w_out, norm_g, rw, rb, ltri)


def _sc_gather(table, idx):
    n_idx = idx.shape[0]
    cols = table.shape[1]
    per_worker = n_idx // SC_WORKERS
    assert n_idx % (SC_WORKERS * SC_CHUNK) == 0
    mesh = plsc.VectorSubcoreMesh(core_axis_name="c", subcore_axis_name="s")

    @functools.partial(
        pl.kernel, mesh=mesh,
        out_type=jax.ShapeDtypeStruct((n_idx, cols), table.dtype),
        scratch_types=[pltpu.VMEM((SC_CHUNK,), jnp.int32),
                       pltpu.VMEM((SC_CHUNK, cols), table.dtype),
                       pltpu.SemaphoreType.DMA],
    )
    def gather(table_hbm, idx_hbm, out_hbm, idx_v, rows_v, sem):
        worker = lax.axis_index("s") * SC_CORES + lax.axis_index("c")
        base = worker * per_worker

        @pl.loop(0, per_worker // SC_CHUNK)
        def _(c):
            off = pl.multiple_of(base + c * SC_CHUNK, SC_CHUNK)
            pltpu.sync_copy(idx_hbm.at[pl.ds(off, SC_CHUNK)], idx_v)
            pltpu.async_copy(table_hbm.at[idx_v], rows_v, sem).wait()
            pltpu.sync_copy(rows_v, out_hbm.at[pl.ds(off, SC_CHUNK)])

    return gather(table, idx)


def _sc_dispatch(rows, dest, n_out):
    t, cols = rows.shape
    per_worker = t // SC_WORKERS
    assert t % (SC_WORKERS * SC_CHUNK) == 0 and dest.shape[0] == 2 * t
    mesh = plsc.VectorSubcoreMesh(core_axis_name="c", subcore_axis_name="s")

    @functools.partial(
        pl.kernel, mesh=mesh,
        out_type=jax.ShapeDtypeStruct((n_out, cols), rows.dtype),
        scratch_types=[pltpu.VMEM((SC_CHUNK,), jnp.int32),
                       pltpu.VMEM((SC_CHUNK, cols), rows.dtype)],
    )
    def dispatch(rows_hbm, dest_hbm, out_hbm, idx_v, rows_v):
        worker = lax.axis_index("s") * SC_CORES + lax.axis_index("c")
        base = worker * per_worker

        @pl.loop(0, per_worker // SC_CHUNK)
        def _(c):
            off = pl.multiple_of(base + c * SC_CHUNK, SC_CHUNK)
            pltpu.sync_copy(rows_hbm.at[pl.ds(off, SC_CHUNK)], rows_v)
            for slot in range(2):
                pltpu.sync_copy(dest_hbm.at[pl.ds(slot * t + off, SC_CHUNK)], idx_v)
                pltpu.sync_copy(rows_v, out_hbm.at[idx_v])

    return dispatch(rows, dest)


def _moe_kernel(be_ref, nu_ref, x_ref, wg_ref, wu_ref, wd_ref, y_ref, wg_bf, wu_bf, wd_bf):
    i = pl.program_id(0)

    @pl.when((i == 0) | (be_ref[i] != be_ref[jnp.maximum(i - 1, 0)]))
    def _():
        _cast_rows(wg_ref, wg_bf)
        _cast_rows(wu_ref, wu_bf)
        _cast_rows(wd_ref, wd_bf)

    @pl.when(i < nu_ref[0])
    def _():
        xb = _unpack_halves(x_ref[...]).astype(BF16)
        a = jnp.dot(xb, wg_bf[...], preferred_element_type=F32)
        u = jnp.dot(xb, wu_bf[...], preferred_element_type=F32)
        y = jnp.dot((_silu(a) * u).astype(BF16), wd_bf[...], preferred_element_type=F32)
        y_ref[...] = _pack_halves(y)

    @pl.when(i >= nu_ref[0])
    def _():
        y_ref[...] = jnp.zeros_like(y_ref)


def _moe(blk_expert, n_used, x_sorted, w_gate, w_up, w_down):
    n_blocks = blk_expert.shape[0]
    wspec = lambda shape: pl.BlockSpec((None,) + shape, lambda i, be, nu: (be[i], 0, 0))
    rows = pl.BlockSpec((MOE_ROWS, HALF), lambda i, be, nu: (i, 0))
    grid_spec = pltpu.PrefetchScalarGridSpec(
        num_scalar_prefetch=2,
        grid=(n_blocks,),
        in_specs=[rows, wspec((D_MODEL, D_FF)), wspec((D_MODEL, D_FF)), wspec((D_FF, D_MODEL))],
        out_specs=rows,
        scratch_shapes=[pltpu.VMEM((D_MODEL, D_FF), BF16), pltpu.VMEM((D_MODEL, D_FF), BF16),
                        pltpu.VMEM((D_FF, D_MODEL), BF16)],
    )
    return pl.pallas_call(
        _moe_kernel,
        grid_spec=grid_spec,
        out_shape=jax.ShapeDtypeStruct(x_sorted.shape, U32),
        compiler_params=pltpu.CompilerParams(
            dimension_semantics=("arbitrary",), vmem_limit_bytes=VMEM_LIMIT),
        name="moe",
    )(blk_expert, n_used, x_sorted, w_gate, w_up, w_down)


def _combine_kernel(h2_ref, rw_ref, g_ref, y0_ref, y1_ref, o_ref):
    rw = rw_ref[...]
    h = h2_ref[...] + rw[:, 0:1] * _unpack_halves(y0_ref[...]) + rw[:, 1:2] * _unpack_halves(y1_ref[...])
    o_ref[...] = h * lax.rsqrt(jnp.mean(h * h, axis=-1, keepdims=True) + EPS) * g_ref[...]


def _combine(h2, rweights, final_g, y):
    t = h2.shape[0]
    tm = min(COMBINE_TM, t)
    row = lambda i: (i, 0)
    return pl.pallas_call(
        _combine_kernel,
        grid=(t // tm,),
        in_specs=[pl.BlockSpec((tm, D_MODEL), row),
                  pl.BlockSpec((tm, LANES), row),
                  pl.BlockSpec((1, D_MODEL), lambda i: (0, 0)),
                  pl.BlockSpec((tm, HALF), row),
                  pl.BlockSpec((tm, HALF), lambda i: (i + t // tm, 0))],
        out_specs=pl.BlockSpec((tm, D_MODEL), row),
        out_shape=jax.ShapeDtypeStruct((t, D_MODEL), F32),
        compiler_params=pltpu.CompilerParams(
            dimension_semantics=("arbitrary",), vmem_limit_bytes=VMEM_LIMIT),
        name="combine",
    )(h2, rweights, final_g, y, y)


def _block_layout(ids, counts_row, t):
    n_blocks = 2 * t // MOE_ROWS + N_EXPERTS
    counts = counts_row[0, ROUTE_LANE0:ROUTE_LANE0 + N_EXPERTS].astype(jnp.int32)
    padded = ((counts + MOE_ROWS - 1) // MOE_ROWS) * MOE_ROWS
    pend = jnp.cumsum(padded)
    pstart = pend - padded
    blk_first = jnp.arange(n_blocks, dtype=jnp.int32) * MOE_ROWS
    blk_expert = jnp.minimum(jnp.sum((pend[None, :] <= blk_first[:, None]).astype(jnp.int32), axis=1),
                             N_EXPERTS - 1).astype(jnp.int32)
    n_used = (pend[-1] // MOE_ROWS).astype(jnp.int32).reshape(1)
    expert, rank = ids[2:4], ids[4:6]
    experts = jnp.arange(N_EXPERTS, dtype=jnp.int32)
    first_row = jnp.sum(jnp.where(expert[:, :, None] == experts, pstart, 0), axis=-1)
    dest = (first_row + rank).reshape(2 * t)
    return dest, blk_expert, n_used, n_blocks * MOE_ROWS


def kernel(x, meta_tokens, hg_lb_logits, norm_mix_g, w_in, gd_conv_w, gd_A_log, gd_dt_bias, hg_norm_g, gd_norm_g, hg_up, gd_up, w_out, norm_ffn_g, router_group_w, router_group_b, router_expert_w, router_expert_b, w_gate, w_up, w_down, final_norm_g):
    bsz, seq, d = x.shape
    t = bsz * seq
    x2d = x.reshape(t, d)

    lb = jnp.cumsum(jax.nn.softmax(hg_lb_logits.astype(F32), axis=0), axis=0)[0].reshape(1, WIDTH)
    w_all = w_in.astype(F32)
    w_gates = w_all[0, :, W_SMALL + 2 * HEADS:]
    g_mix = norm_mix_g[0].reshape(1, d)
    meta_blk = jnp.concatenate([jnp.zeros((CHUNK - N_META, d), F32), meta_tokens.astype(F32)], axis=0)

    conv_w = gd_conv_w[0].astype(F32)
    no_history = jnp.zeros((SUBLANES, 3 * WIDTH), F32)
    proj, proj_meta = {}, {}
    proj["hgrn"] = _norm_proj_hgrn(x2d, g_mix, lb, w_all, PROJ_TM)
    proj_meta["hgrn"] = _norm_proj_hgrn(meta_blk, g_mix, lb, w_all, CHUNK)
    proj_meta["gdn"], meta_tail, logits_meta = _norm_proj_conv(meta_blk, g_mix, w_all, conv_w, no_history,
                                                               CHUNK, CHUNK)
    proj["gdn"], _, logits = _norm_proj_conv(x2d, g_mix, w_all, conv_w, meta_tail, PROJ_TM, seq)

    def scalar_rows(lg, nb, nc):
        s = lg.reshape(2, N_STACKS, HEADS_PER_STACK, nb, nc, CHUNK)
        return s.transpose(3, 4, 1, 0, 2, 5).reshape(nb, nc, N_STACKS, 2, STACK)

    a_log_row = jnp.repeat(gd_A_log[0].astype(F32), CHUNK).reshape(N_STACKS, 1, STACK)
    dt_row = jnp.repeat(gd_dt_bias[0].astype(F32), CHUNK).reshape(N_STACKS, 1, STACK)
    hg_g, gd_g = hg_norm_g[0].reshape(1, HEAD_DIM), gd_norm_g[0].reshape(1, HEAD_DIM)

    zero_state = jnp.zeros((HEADS, HEAD_DIM, HEAD_DIM), F32)
    per_seq = lambda st: jnp.tile(st, (bsz, 1, 1))
    _, hg_state = _hgrn(proj_meta["hgrn"][None], zero_state, hg_g, 1)
    _, gd_state = _gdn(proj_meta["gdn"][None], zero_state, scalar_rows(logits_meta, 1, 1),
                       a_log_row, dt_row, gd_g, 1)
    o_a, o_b = _mixers(proj["hgrn"].reshape(bsz, seq, -1), proj["gdn"].reshape(bsz, seq, -1),
                       per_seq(hg_state), per_seq(gd_state), scalar_rows(logits, bsz, seq // CHUNK),
                       a_log_row, dt_row, hg_g, gd_g, GDN_CHUNKS)

    rw = jnp.zeros((d, LANES), F32)
    rw = rw.at[:, :N_GROUPS].set(router_group_w[0]).at[:, N_GROUPS:N_GROUPS + N_EXPERTS].set(router_expert_w[0])
    rb = jnp.zeros((1, LANES), F32)
    rb = rb.at[0, :N_GROUPS].set(router_group_b[0]).at[0, N_GROUPS:N_GROUPS + N_EXPERTS].set(router_expert_b[0])
    h2, xn, info, ids, counts = _merge(o_a.reshape(t, WIDTH), o_b.reshape(t, WIDTH), x2d, g_mix, w_gates,
                                  hg_up[0].astype(BF16), gd_up[0].astype(BF16), w_out[0].astype(BF16),
                                  norm_ffn_g[0].reshape(1, d), rw.astype(BF16), rb)

    dest, blk_expert, n_used, n_rows = _block_layout(ids, counts, t)
    x_sorted = _sc_dispatch(xn, dest, n_rows)
    y_sorted = _moe(blk_expert, n_used, x_sorted,
                    w_gate[0].astype(F32), w_up[0].astype(F32), w_down[0].astype(F32))
    y_tok = _sc_gather(y_sorted, dest)
    out = _combine(h2, info, final_norm_g.reshape(1, d), y_tok)
    return out.reshape(bsz, seq, d)
```

```python
import functools

import numpy as np
import jax
import jax.numpy as jnp
from jax import lax
from jax.experimental import pallas as pl
from jax.experimental.pallas import tpu as pltpu
from jax.experimental.pallas import tpu_sc as plsc

F32 = jnp.float32
BF16 = jnp.bfloat16
U32 = jnp.uint32

D_MODEL = 1024
N_META = 16
CHUNK_BITS = 6
CHUNK = 1 << CHUNK_BITS
EPS = 1e-6
HEADS = 4
HEAD_DIM = 128
WIDTH = HEADS * HEAD_DIM
HEADS_PER_STACK = 2
STACK = HEADS_PER_STACK * CHUNK
N_STACKS = HEADS // HEADS_PER_STACK
CONV_W = 4
NEUMANN_BITS = 3
N_GROUPS = 4
EXPERTS_PER_GROUP = 8
N_EXPERTS = N_GROUPS * EXPERTS_PER_GROUP
D_FF = 512
LANES = 128
SUBLANES = 8
HALF = D_MODEL // 2
MOE_ROWS = 512
CAST_SLAB = 128
SC_CORES, SC_SUBCORES = 2, 16
SC_WORKERS = SC_CORES * SC_SUBCORES
SC_CHUNK = 128
PROJ_TM = 1024
W_HGRN, W_GDN, W_SMALL = 0, 4 * WIDTH, 8 * WIDTH
HGRN_CHUNKS, GDN_CHUNKS = 4, 2
TOKEN_TM = 512
COMBINE_TM = 1024
VMEM_LIMIT = 56 * 1024 * 1024


def _bdot(a, b):
    return lax.dot_general(a.astype(BF16), b.astype(BF16), (((2,), (1,)), ((0,), (0,))),
                           preferred_element_type=F32)


def _bdot_nt(a, b):
    return lax.dot_general(a.astype(BF16), b.astype(BF16), (((2,), (2,)), ((0,), (0,))),
                           preferred_element_type=F32)


def _bdot_tn(a, b):
    return lax.dot_general(a.astype(BF16), b.astype(BF16), (((1,), (1,)), ((0,), (0,))),
                           preferred_element_type=F32)


def _split3(x):
    hi = x.astype(BF16)
    r1 = x - hi.astype(F32)
    mid = r1.astype(BF16)
    lo = (r1 - mid.astype(F32)).astype(BF16)
    return hi, mid, lo


def _sigmoid(x):
    return 1.0 / (1.0 + jnp.exp(-x))


def _silu(x):
    return x * _sigmoid(x)


def _softplus(x):
    return jnp.maximum(x, 0.0) + jnp.log(1.0 + jnp.exp(-jnp.abs(x)))


def _stack_heads(x, bsz):
    x3 = x.reshape(bsz, CHUNK, WIDTH)
    y = jnp.concatenate([x3[:, :, h * HEAD_DIM:(h + 1) * HEAD_DIM] for h in range(HEADS)], axis=1)
    return y.reshape(bsz * HEADS * CHUNK, HEAD_DIM)


def _unstack_heads(x, bsz):
    x3 = x.reshape(bsz, HEADS * CHUNK, HEAD_DIM)
    return jnp.concatenate([x3[:, h * CHUNK:(h + 1) * CHUNK, :] for h in range(HEADS)], axis=2)


def _to_stacks(a, bsz):
    return a.reshape(bsz * N_STACKS, STACK, a.shape[-1])


def _to_heads(a, bsz):
    return a.reshape(bsz * HEADS, CHUNK, a.shape[-1])


def _roll_rows(x, shift):
    n = x.shape[0]
    return pltpu.roll(x, shift % n, axis=0)


def _chunks_first(x, nc):
    return jnp.concatenate([x[:, j * CHUNK:(j + 1) * CHUNK] for j in range(nc)], axis=0)


def _chunks_first_scalars(s, nc):
    return jnp.concatenate([s[:, j] for j in range(nc)], axis=0)


def _load_chunks(ref, nc):
    x = _chunks_first(ref[...], nc)
    return x.reshape(x.shape[0] * CHUNK, WIDTH)


def _store_chunks(ref, x, nb, nc):
    ref[...] = jnp.concatenate([x[j * nb:(j + 1) * nb] for j in range(nc)], axis=1).astype(ref.dtype)


def _cast_rows(src_ref, dst_ref, col0=0):
    slab = CAST_SLAB
    rows, cols = src_ref.shape
    assert rows % slab == 0

    def body(k, carry):
        r = pl.multiple_of(k * slab, slab)
        dst_ref[pl.ds(r, slab), col0:col0 + cols] = src_ref[pl.ds(r, slab), :].astype(dst_ref.dtype)
        return carry

    lax.fori_loop(0, rows // slab, body, 0)


def _pack_halves(x):
    lo = lax.bitcast_convert_type(x[:, :HALF].astype(BF16).astype(F32), U32)
    hi = lax.bitcast_convert_type(x[:, HALF:].astype(BF16).astype(F32), U32)
    return (lo >> 16) | (hi & jnp.uint32(0xFFFF0000))


def _unpack_halves(p):
    lo = lax.bitcast_convert_type(p << 16, F32)
    hi = lax.bitcast_convert_type(p & jnp.uint32(0xFFFF0000), F32)
    return jnp.concatenate([lo, hi], axis=1)


def _normed_bf16(x_ref, g_ref):
    x = x_ref[...]
    return (x * lax.rsqrt(jnp.mean(x * x, axis=-1, keepdims=True) + EPS) * g_ref[...]).astype(BF16)


def _zero_rows(ref):
    slab = min(CAST_SLAB, ref.shape[0])
    assert ref.shape[0] % slab == 0

    def body(k, carry):
        ref[pl.ds(pl.multiple_of(k * slab, slab), slab), :] = jnp.zeros((slab, ref.shape[1]), ref.dtype)
        return carry

    lax.fori_loop(0, ref.shape[0] // slab, body, 0)


def _norm_proj_hgrn_kernel(x_ref, g_ref, lb_ref, w_ref, o_ref, wbf_ref, acc_ref):
    @pl.when(pl.program_id(0) == 0)
    def _():
        _cast_rows(w_ref, wbf_ref)
        _zero_rows(acc_ref)

    xn = _normed_bf16(x_ref, g_ref)
    lb = lb_ref[...]
    finish = (_silu, lambda a: lb + (1.0 - lb) * _sigmoid(a), lambda a: a, _silu)
    for j, fn in enumerate(finish):
        cols = slice(j * WIDTH, (j + 1) * WIDTH)
        o_ref[:, cols] = fn(acc_ref[:, cols])
        acc_ref[:, cols] = jnp.dot(xn, wbf_ref[:, cols], preferred_element_type=F32)


def _norm_proj_conv_kernel(x_ref, g_ref, w_ref, ws_ref, cw_ref, hist_ref, o_ref, tail_ref, logit_ref,
                           wbf_ref, acc_ref, carry_ref, *, steps_per_seq):
    i = pl.program_id(0)

    @pl.when(i == 0)
    def _():
        _cast_rows(w_ref, wbf_ref)
        _cast_rows(ws_ref, wbf_ref, 4 * WIDTH)
        _zero_rows(acc_ref)
        carry_ref[...] = jnp.zeros_like(carry_ref)

    xn = _normed_bf16(x_ref, g_ref)
    w3, w4 = 3 * WIDTH, 4 * WIDTH
    rows = x_ref.shape[0]
    first_of_seq = (i - 1) % steps_per_seq == 0
    cw = cw_ref[...]
    r8 = lax.broadcasted_iota(jnp.int32, (SUBLANES, WIDTH), 0)
    for c0 in range(0, w3, WIDTH):
        cols = slice(c0, c0 + WIDTH)
        raw = acc_ref[:, cols]
        hist = jnp.where(first_of_seq, hist_ref[:, cols], carry_ref[:, cols])
        tap = lambda d: cw[CONV_W - 1 - d:CONV_W - d, cols]
        body = raw * tap(0)
        head = raw[:SUBLANES]
        first = head * tap(0)
        for d in range(1, CONV_W):
            body = body + _roll_rows(raw, d) * tap(d)
            first = first + jnp.where(r8 >= d, _roll_rows(head, d), _roll_rows(hist, d)) * tap(d)
        o_ref[:SUBLANES, cols] = _silu(first)
        o_ref[SUBLANES:, cols] = _silu(body[SUBLANES:])
        carry_ref[:, cols] = raw[rows - SUBLANES:]
        acc_ref[:, cols] = jnp.dot(xn, wbf_ref[:, cols], preferred_element_type=F32)
    rest = acc_ref[:, w3:]
    o_ref[:, w3:] = _silu(rest[:, :WIDTH])
    logit_ref[...] = rest[:, WIDTH:].T[:SUBLANES]
    acc_ref[:, w3:] = jnp.dot(xn, wbf_ref[:, w3:], preferred_element_type=F32)
    tail_ref[...] = carry_ref[...]


def _proj_call(kernel_fn, name, x2d, tm, n, operands, specs, extra_out=(), extra_scratch=(),
               weight_cols=None, delayed=False):
    m, k = x2d.shape
    weight_cols = n if weight_cols is None else weight_cols
    tm = min(tm, m)
    tiles = m // tm
    x_map = (lambda i: (jnp.minimum(i, tiles - 1), 0)) if delayed else (lambda i: (i, 0))
    o_map = (lambda i: (jnp.maximum(i - 1, 0), 0)) if delayed else (lambda i: (i, 0))
    out_specs = [pl.BlockSpec((tm, n), o_map)] + [s for s, _ in extra_out]
    out_shape = [jax.ShapeDtypeStruct((m, n), F32)] + [s for _, s in extra_out]
    return pl.pallas_call(
        kernel_fn,
        grid=(tiles + int(delayed),),
        in_specs=[pl.BlockSpec((tm, k), x_map)] + specs,
        out_specs=out_specs,
        out_shape=out_shape,
        scratch_shapes=[pltpu.VMEM((k, weight_cols), BF16)] + list(extra_scratch),
        compiler_params=pltpu.CompilerParams(
            dimension_semantics=("arbitrary",), vmem_limit_bytes=VMEM_LIMIT),
        name=name,
    )(x2d, *operands)


def _weight_spec(k, n, col0):
    assert col0 % n == 0
    return pl.BlockSpec((None, k, n), lambda i: (0, 0, col0 // n), pipeline_mode=pl.Buffered(1))


def _norm_proj_hgrn(x2d, g, lb, w, tm):
    k, n = w.shape[1], 4 * WIDTH
    const = lambda i: (0, 0)
    specs = [pl.BlockSpec((1, k), const), pl.BlockSpec((1, WIDTH), const), _weight_spec(k, n, W_HGRN)]
    tm = min(tm, x2d.shape[0])
    return _proj_call(_norm_proj_hgrn_kernel, "norm_proj_hgrn", x2d, tm, n, (g, lb, w), specs,
                      extra_scratch=[pltpu.VMEM((tm, n), F32)], delayed=True)[0]


def _norm_proj_conv(x2d, g, w, conv_w, hist, tm, rows_per_seq):
    m, k = x2d.shape[0], w.shape[1]
    n_out, n_w = 4 * WIDTH, 4 * WIDTH + LANES
    const = lambda i: (0, 0)
    tm = min(tm, rows_per_seq)
    assert rows_per_seq % tm == 0 and m % rows_per_seq == 0
    tail_shape = (SUBLANES, 3 * WIDTH)
    specs = [pl.BlockSpec((1, k), const), _weight_spec(k, 4 * WIDTH, W_GDN), _weight_spec(k, LANES, W_SMALL),
             pl.BlockSpec((CONV_W, 3 * WIDTH), const), pl.BlockSpec(tail_shape, const)]
    logit_spec = pl.BlockSpec((SUBLANES, tm), lambda i: (0, jnp.maximum(i - 1, 0)))
    return _proj_call(functools.partial(_norm_proj_conv_kernel, steps_per_seq=rows_per_seq // tm),
                      "norm_proj_gdn", x2d, tm, n_out, (g, w, w, conv_w, hist), specs,
                      extra_out=[(pl.BlockSpec(tail_shape, const), jax.ShapeDtypeStruct(tail_shape, F32)),
                                 (logit_spec, jax.ShapeDtypeStruct((SUBLANES, m), F32))],
                      extra_scratch=[pltpu.VMEM((tm, n_w), F32), pltpu.VMEM(tail_shape, F32)],
                      weight_cols=n_w, delayed=True)


HGRN_LEVELS = 6
HGRN_DIAG = HGRN_LEVELS


def _hgrn_level_table():
    r = np.arange(STACK)
    t, hd = r % CHUNK, r // CHUNK
    x = t[:, None] ^ t[None, :]
    lv = np.floor(np.log2(np.maximum(x, 1))).astype(np.int32)
    valid = (hd[:, None] == hd[None, :]) & (t[:, None] > t[None, :])
    lv = np.where(valid, lv, -1)
    lv = np.where(r[:, None] == r[None, :], HGRN_DIAG, lv)
    return jnp.asarray(lv, dtype=jnp.int32)


def _hgrn_kernel(lv_ref, g_ref, st0_ref, q_ref, f_ref, v_ref, gate_ref, o_ref, stout_ref, st_ref, *, nb, nc):
    @pl.when(pl.program_id(0) == 0)
    def _():
        st_ref[...] = st0_ref[...]

    bsz = nc * nb
    q = _load_chunks(q_ref, nc)
    f = _load_chunks(f_ref, nc)
    v = _load_chunks(v_ref, nc)
    gate = _load_chunks(gate_ref, nc)

    per_seq = lambda a: a.reshape(bsz, CHUNK, WIDTH)
    trow = lax.broadcasted_iota(jnp.int32, (1, CHUNK, WIDTH), 1)
    b = jnp.log2(f)
    s = 1
    while s < CHUNK:
        b = (per_seq(b) + jnp.where(trow >= s, per_seq(_roll_rows(b, s)), 0.0)).reshape(b.shape)
        s *= 2

    qs, ks, vs, bs = (_stack_heads(a, bsz) for a in (q, 1.0 - f, v, b))
    lv = lv_ref[...][None]
    to3 = functools.partial(_to_stacks, bsz=bsz)
    to_g = functools.partial(_to_heads, bsz=bsz)
    roll_g = lambda a, shift: to_g(_roll_rows(a.reshape(qs.shape), shift))
    t = lax.broadcasted_iota(jnp.int32, (1, CHUNK, HEAD_DIM), 1)
    qs_g, ks_g, bs_g = to_g(qs), to_g(ks), to_g(bs)

    attn = jnp.where(lv == HGRN_DIAG, _bdot_nt(to3(qs), to3(ks)), 0.0)
    bref = bs_g
    bnext = roll_g(bs_g, -1)
    for li in range(HGRN_LEVELS):
        m = 1 << li
        if li > 0:
            half = m // 2
            upper = (t & (m - 1)) >= half
            if half < SUBLANES:
                grp = lambda a: a.reshape(-1, SUBLANES, HEAD_DIM)
                local = lambda a, sh: pltpu.roll(grp(a), sh % SUBLANES, axis=1).reshape(a.shape)
                bref = jnp.where(upper, local(bref, half), bref)
                bnext = jnp.where(upper, bnext, local(bnext, -half))
            else:
                bref = jnp.where(upper, roll_g(bref, half), bref)
                bnext = jnp.where(upper, bnext, roll_g(bnext, -half))
        odd = ((t >> li) & 1) == 1
        both = jnp.where(odd, qs_g, ks_g) * jnp.exp2(jnp.where(odd, bs_g - bref, bnext - bs_g))
        both = to3(both.reshape(qs.shape)).astype(BF16)
        attn = jnp.where(lv == li, _bdot_nt(both, both), attn)

    o = _bdot(attn, to3(vs))

    n_g = nb * HEADS
    st = st_ref[...]
    qe = qs_g * jnp.exp2(bs_g)
    b_end = bs_g[:, CHUNK - 1:CHUNK, :]
    ke = ks_g * jnp.exp2(b_end - bs_g)
    vs_g = to_g(vs)
    o_inter = []
    for j in range(nc):
        sl = slice(j * n_g, (j + 1) * n_g)
        o_inter.append(_bdot_nt(qe[sl], st))
        st = st * jnp.exp2(b_end[sl]) + _bdot_tn(vs_g[sl], ke[sl])
    st_ref[...] = st
    stout_ref[...] = st
    o = (to_g(o) + jnp.concatenate(o_inter, axis=0)).reshape(qs.shape)

    o = o * lax.rsqrt(jnp.mean(o * o, axis=-1, keepdims=True) + EPS) * g_ref[...]
    o = o * _stack_heads(gate, bsz)
    _store_chunks(o_ref, _unstack_heads(o, bsz), nb, nc)


def _gdn_cumsum_matrix():
    r = np.arange(STACK)
    u = (r[:, None] // CHUNK == r[None, :] // CHUNK) & (r[:, None] <= r[None, :])
    return jnp.asarray(u, dtype=BF16)


def _gdn_kernel(u_ref, alog_ref, dt_ref, g_ref, s_ref, st0_ref, q_ref, k_ref, v_ref, z_ref,
                o_ref, stout_ref, st_ref, *, nb, nc):
    @pl.when(pl.program_id(0) == 0)
    def _():
        st_ref[...] = st0_ref[...]

    bsz = nc * nb
    qs = _stack_heads(_load_chunks(q_ref, nc), bsz)
    ks = _stack_heads(_load_chunks(k_ref, nc), bsz)
    vs = _stack_heads(_load_chunks(v_ref, nc), bsz)
    gate = _load_chunks(z_ref, nc)
    qs = qs * lax.rsqrt(jnp.sum(qs * qs, axis=-1, keepdims=True) + EPS) * (HEAD_DIM ** -0.5)
    ks = ks * lax.rsqrt(jnp.sum(ks * ks, axis=-1, keepdims=True) + EPS)
    to3 = functools.partial(_to_stacks, bsz=bsz)
    qs, ks, vs = to3(qs), to3(ks), to3(vs)
    n_st = bsz * N_STACKS

    srow = _chunks_first_scalars(s_ref[...], nc).reshape(n_st, 2, STACK)
    per_stack = lambda ref: jnp.broadcast_to(ref[...][None], (bsz, N_STACKS, 1, STACK)).reshape(n_st, 1, STACK)
    beta_row = _sigmoid(srow[:, 0:1, :])
    g_row = -jnp.exp(per_stack(alog_ref)) * _softplus(srow[:, 1:2, :] + per_stack(dt_ref))
    n8 = n_st * SUBLANES
    g8 = jnp.broadcast_to(g_row, (n_st, SUBLANES, STACK)).reshape(n8, STACK)
    pieces = jnp.concatenate(_split3(g8), axis=0)
    cum = jnp.dot(pieces, u_ref[...], preferred_element_type=F32)
    cum = cum[0:n8] + cum[n8:2 * n8] + cum[2 * n8:]
    cum_row = cum.reshape(n_st, SUBLANES, STACK)[:, 0:1, :]

    r = lax.broadcasted_iota(jnp.int32, (STACK, STACK), 0)
    cc = lax.broadcasted_iota(jnp.int32, (STACK, STACK), 1)
    eye = (r == cc)[None]
    cum_col = jnp.sum(jnp.where(eye, cum_row, 0.0), axis=2, keepdims=True)
    beta_col = jnp.sum(jnp.where(eye, beta_row, 0.0), axis=2, keepdims=True)
    same = (r >> CHUNK_BITS) == (cc >> CHUNK_BITS)
    lower = (same & (r >= cc))[None]
    strict = (same & (r > cc))[None]
    ratio = jnp.exp(jnp.minimum(cum_col - cum_row, 0.0))

    kkt = _bdot_nt(ks, ks)
    qkt = _bdot_nt(qs, ks)
    lm = jnp.where(strict, beta_col * ratio * kkt, 0.0)

    ident = eye.astype(F32)
    l0 = jnp.where(((r >> NEUMANN_BITS) == (cc >> NEUMANN_BITS))[None], lm, 0.0)
    l2 = _bdot(l0, l0)
    l4 = _bdot(l2, l2)
    xinv = _bdot(_bdot(ident - l0, ident + l2), ident + l4)
    for bits in range(NEUMANN_BITS, CHUNK_BITS):
        off = (((r >> (bits + 1)) == (cc >> (bits + 1))) & ((r >> bits) != (cc >> bits)))[None]
        xinv = xinv - _bdot(_bdot(xinv, jnp.where(off, lm, 0.0)), xinv)

    decay_col = jnp.exp(cum_col)
    rhs = jnp.concatenate([beta_col * vs, (beta_col * decay_col) * ks], axis=2)
    sol = _bdot(xinv, rhs)

    to_g = functools.partial(_to_heads, bsz=bsz)
    n_g = nb * HEADS
    st = st_ref[...]
    u0_g, w_g, q_g = to_g(sol[:, :, :HEAD_DIM]), to_g(sol[:, :, HEAD_DIM:]), to_g(qs)
    cum_g = to_g(cum_col)
    cum_end = cum_g[:, CHUNK - 1:CHUNK, :]
    kdec = to_g(ks) * jnp.exp(cum_end - cum_g)
    decay_g = to_g(decay_col)
    us, o_inter = [], []
    for j in range(nc):
        sl = slice(j * n_g, (j + 1) * n_g)
        u_j = u0_g[sl] - _bdot(w_g[sl], st)
        us.append(u_j)
        o_inter.append(decay_g[sl] * _bdot(q_g[sl], st))
        st = jnp.exp(cum_end[sl]) * st + _bdot_tn(kdec[sl], u_j)
    st_ref[...] = st
    stout_ref[...] = st
    attn = jnp.where(lower, qkt * ratio, 0.0)
    u = jnp.concatenate(us, axis=0).reshape(bsz * HEADS * CHUNK, HEAD_DIM)
    o = jnp.concatenate(o_inter, axis=0) + to_g(_bdot(attn, to3(u)))
    o = o.reshape(bsz * HEADS * CHUNK, HEAD_DIM)

    o = o * lax.rsqrt(jnp.mean(o * o, axis=-1, keepdims=True) + EPS) * g_ref[...]
    o = o * _stack_heads(gate, bsz)
    _store_chunks(o_ref, _unstack_heads(o, bsz), nb, nc)


def _recurrence_call(kernel_fn, name, proj, state0, operands, specs, nc):
    nb, length, _ = proj.shape
    rows = nc * CHUNK
    state_shape = (nb * HEADS, HEAD_DIM, HEAD_DIM)
    state_spec = pl.BlockSpec(state_shape, lambda c: (0, 0, 0))
    chunk_specs = [pl.BlockSpec((nb, rows, WIDTH), functools.partial(lambda c, j: (0, c, j), j=j))
                   for j in range(4)]
    return pl.pallas_call(
        functools.partial(kernel_fn, nb=nb, nc=nc),
        grid=(length // rows,),
        in_specs=specs + [state_spec] + chunk_specs,
        out_specs=[pl.BlockSpec((nb, rows, WIDTH), lambda c: (0, c, 0)), state_spec],
        out_shape=[jax.ShapeDtypeStruct((nb, length, WIDTH), BF16),
                   jax.ShapeDtypeStruct(state_shape, F32)],
        scratch_shapes=[pltpu.VMEM(state_shape, F32)],
        compiler_params=pltpu.CompilerParams(
            dimension_semantics=("arbitrary",), vmem_limit_bytes=VMEM_LIMIT),
        name=name,
    )(*operands, state0, *([proj] * 4))


def _hgrn(proj, state0, norm_g, nc):
    const = lambda c: (0, 0)
    specs = [pl.BlockSpec((STACK, STACK), const), pl.BlockSpec((1, HEAD_DIM), const)]
    return _recurrence_call(_hgrn_kernel, "hgrn", proj, state0, (_hgrn_level_table(), norm_g), specs, nc)


def _gdn(proj, state0, scal, a_log_row, dt_row, norm_g, nc):
    nb = proj.shape[0]
    const = lambda c: (0, 0)
    const3 = lambda c: (0, 0, 0)
    specs = [pl.BlockSpec((STACK, STACK), const),
             pl.BlockSpec((N_STACKS, 1, STACK), const3),
             pl.BlockSpec((N_STACKS, 1, STACK), const3),
             pl.BlockSpec((1, HEAD_DIM), const),
             pl.BlockSpec((nb, nc, N_STACKS, 2, STACK), lambda c: (0, c, 0, 0, 0))]
    return _recurrence_call(_gdn_kernel, "gdn", proj, state0,
                            (_gdn_cumsum_matrix(), a_log_row, dt_row, norm_g, scal), specs, nc)


ROUTE_LANE0 = N_GROUPS
ROUTE_NEG = -1e30


def _route_tile(lg, ltri, carry):
    lane = lax.broadcasted_iota(jnp.int32, lg.shape, 1).astype(F32)
    first = lambda mask: jnp.min(jnp.where(mask, lane, float(LANES)), axis=1, keepdims=True)
    top = lambda mask: jnp.max(jnp.where(mask, lg, ROUTE_NEG), axis=1, keepdims=True)

    is_g = lane < N_GROUPS
    gmax = top(is_g)
    grp = first(is_g & (lg == gmax))
    p_grp = 1.0 / jnp.sum(jnp.where(is_g, jnp.exp(lg - gmax), 0.0), axis=1, keepdims=True)

    lo = ROUTE_LANE0 + grp * EXPERTS_PER_GROUP
    in_grp = (lane >= lo) & (lane < lo + EXPERTS_PER_GROUP)
    m1 = top(in_grp)
    i1 = first(in_grp & (lg == m1))
    rest = in_grp & (lane != i1)
    m2 = top(rest)
    i2 = first(rest & (lg == m2))
    r = jnp.exp(m2 - m1)
    w0 = p_grp / (1.0 + r)
    w1 = p_grp * r / (1.0 + r)

    hot0 = (lane == i1).astype(F32)
    hot1 = (lane == i2).astype(F32)
    both = hot0 + hot1
    before = jnp.dot(ltri, both.astype(BF16), preferred_element_type=F32) + carry
    rank0 = jnp.sum(hot0 * before, axis=1, keepdims=True)
    rank1 = jnp.sum(hot1 * before, axis=1, keepdims=True)
    cols = (w0, w1, i1 - ROUTE_LANE0, i2 - ROUTE_LANE0, rank0, rank1)
    info = jnp.zeros_like(lg)
    for k, col in enumerate(cols):
        info = jnp.where(lane == k, col, info)
    return info, carry + jnp.sum(both, axis=0, keepdims=True)


def _merge_kernel(oa_ref, ob_ref, x_ref, xprev_ref, mg_ref, wgate_ref, hgup_ref, gdup_ref, wout_ref,
                  ng_ref, rw_ref, rb_ref, ltri_ref, h2_ref, xn_ref, info_ref, ids_ref, cnt_ref,
                  wgate_bf, gates_ref, ua_ref, ub_ref, carry_ref):
    i = pl.program_id(0)

    @pl.when(i == 0)
    def _():
        carry_ref[...] = jnp.zeros_like(carry_ref)
        _cast_rows(wgate_ref, wgate_bf)
        _zero_rows(gates_ref)
        _zero_rows(ua_ref)
        _zero_rows(ub_ref)

    gates = _sigmoid(gates_ref[...])
    merged = (gates[:, :D_MODEL] * ua_ref[...] + gates[:, D_MODEL:] * ub_ref[...]).astype(BF16)
    gates_ref[...] = jnp.dot(_normed_bf16(x_ref, mg_ref), wgate_bf[...], preferred_element_type=F32)
    h2 = xprev_ref[...] + jnp.dot(merged, wout_ref[...], preferred_element_type=F32)
    h2_ref[...] = h2
    xn = h2 * lax.rsqrt(jnp.mean(h2 * h2, axis=-1, keepdims=True) + EPS) * ng_ref[...]
    xn_ref[...] = _pack_halves(xn)
    lg = jnp.dot(xn.astype(BF16), rw_ref[...], preferred_element_type=F32) + rb_ref[...]
    ua_ref[...] = jnp.dot(oa_ref[...], hgup_ref[...], preferred_element_type=F32)
    ub_ref[...] = jnp.dot(ob_ref[...], gdup_ref[...], preferred_element_type=F32)
    old = carry_ref[...]
    info, carry = _route_tile(lg, ltri_ref[...], old)
    info_ref[...] = info
    ids_ref[...] = info.T[:SUBLANES].astype(jnp.int32)
    carry = jnp.where(i > 0, carry, old)
    carry_ref[...] = carry
    cnt_ref[...] = carry


def _merge(o_a, o_b, x2d, mix_g, w_gates, hg_up, gd_up, w_out, norm_g, rw, rb):
    t = x2d.shape[0]
    tm = TOKEN_TM
    tiles = t // tm
    new = lambda i: (jnp.minimum(i, tiles - 1), 0)
    row = lambda i: (jnp.maximum(i - 1, 0), 0)
    const = lambda i: (0, 0)
    ltri = jnp.asarray(np.tril(np.ones((tm, tm), np.float32), -1), dtype=BF16)
    return pl.pallas_call(
        _merge_kernel,
        grid=(tiles + 1,),
        in_specs=[pl.BlockSpec((tm, WIDTH), new),
                  pl.BlockSpec((tm, WIDTH), new),
                  pl.BlockSpec((tm, D_MODEL), new),
                  pl.BlockSpec((tm, D_MODEL), row),
                  pl.BlockSpec((1, D_MODEL), const),
                  pl.BlockSpec((D_MODEL, 2 * D_MODEL), const, pipeline_mode=pl.Buffered(1)),
                  pl.BlockSpec((WIDTH, D_MODEL), const),
                  pl.BlockSpec((WIDTH, D_MODEL), const),
                  pl.BlockSpec((D_MODEL, D_MODEL), const),
                  pl.BlockSpec((1, D_MODEL), const),
                  pl.BlockSpec((D_MODEL, LANES), const),
                  pl.BlockSpec((1, LANES), const),
                  pl.BlockSpec((tm, tm), const)],
        out_specs=[pl.BlockSpec((tm, D_MODEL), row),
                   pl.BlockSpec((tm, HALF), row),
                   pl.BlockSpec((tm, LANES), row),
                   pl.BlockSpec((SUBLANES, tm), lambda i: (0, jnp.maximum(i - 1, 0))),
                   pl.BlockSpec((1, LANES), const)],
        out_shape=[jax.ShapeDtypeStruct((t, D_MODEL), F32),
                   jax.ShapeDtypeStruct((t, HALF), U32),
                   jax.ShapeDtypeStruct((t, LANES), F32),
                   jax.ShapeDtypeStruct((SUBLANES, t), jnp.int32),
                   jax.ShapeDtypeStruct((1, LANES), F32)],
        scratch_shapes=[pltpu.VMEM((D_MODEL, 2 * D_MODEL), BF16), pltpu.VMEM((tm, 2 * D_MODEL), F32),
                        pltpu.VMEM((tm, D_MODEL), F32), pltpu.VMEM((tm, D_MODEL), F32),
                        pltpu.VMEM((1, LANES), F32)],
        compiler_params=pltpu.CompilerParams(
            dimension_semantics=("arbitrary",), vmem_limit_bytes=VMEM_LIMIT),
        name="merge",
    )(o_a, o_b, x2d, x2d, mix_g, w_gates, hg_up, gd_up, w_out, norm_g, rw, rb, ltri)


def _sc_gather(table, idx):
    n_idx = idx.shape[0]
    cols = table.shape[1]
    per_worker = n_idx // SC_WORKERS
    assert n_idx % (SC_WORKERS * SC_CHUNK) == 0
    mesh = plsc.VectorSubcoreMesh(core_axis_name="c", subcore_axis_name="s")

    @functools.partial(
        pl.kernel, mesh=mesh,
        out_type=jax.ShapeDtypeStruct((n_idx, cols), table.dtype),
        scratch_types=[pltpu.VMEM((SC_CHUNK,), jnp.int32),
                       pltpu.VMEM((SC_CHUNK, cols), table.dtype),
                       pltpu.SemaphoreType.DMA],
    )
    def gather(table_hbm, idx_hbm, out_hbm, idx_v, rows_v, sem):
        worker = lax.axis_index("s") * SC_CORES + lax.axis_index("c")
        base = worker * per_worker

        @pl.loop(0, per_worker // SC_CHUNK)
        def _(c):
            off = pl.multiple_of(base + c * SC_CHUNK, SC_CHUNK)
            pltpu.sync_copy(idx_hbm.at[pl.ds(off, SC_CHUNK)], idx_v)
            pltpu.async_copy(table_hbm.at[idx_v], rows_v, sem).wait()
            pltpu.sync_copy(rows_v, out_hbm.at[pl.ds(off, SC_CHUNK)])

    return gather(table, idx)


def _sc_dispatch(rows, dest, n_out):
    t, cols = rows.shape
    per_worker = t // SC_WORKERS
    assert t % (SC_WORKERS * SC_CHUNK) == 0 and dest.shape[0] == 2 * t
    mesh = plsc.VectorSubcoreMesh(core_axis_name="c", subcore_axis_name="s")

    @functools.partial(
        pl.kernel, mesh=mesh,
        out_type=jax.ShapeDtypeStruct((n_out, cols), rows.dtype),
        scratch_types=[pltpu.VMEM((SC_CHUNK,), jnp.int32),
                       pltpu.VMEM((SC_CHUNK, cols), rows.dtype)],
    )
    def dispatch(rows_hbm, dest_hbm, out_hbm, idx_v, rows_v):
        worker = lax.axis_index("s") * SC_CORES + lax.axis_index("c")
        base = worker * per_worker

        @pl.loop(0, per_worker // SC_CHUNK)
        def _(c):
            off = pl.multiple_of(base + c * SC_CHUNK, SC_CHUNK)
            pltpu.sync_copy(rows_hbm.at[pl.ds(off, SC_CHUNK)], rows_v)
            for slot in range(2):
                pltpu.sync_copy(dest_hbm.at[pl.ds(slot * t + off, SC_CHUNK)], idx_v)
                pltpu.sync_copy(rows_v, out_hbm.at[idx_v])

    return dispatch(rows, dest)


def _moe_kernel(be_ref, nu_ref, x_ref, wg_ref, wu_ref, wd_ref, y_ref, wg_bf, wu_bf, wd_bf):
    i = pl.program_id(0)

    @pl.when((i == 0) | (be_ref[i] != be_ref[jnp.maximum(i - 1, 0)]))
    def _():
        _cast_rows(wg_ref, wg_bf)
        _cast_rows(wu_ref, wu_bf)
        _cast_rows(wd_ref, wd_bf)

    @pl.when(i < nu_ref[0])
    def _():
        xb = _unpack_halves(x_ref[...]).astype(BF16)
        a = jnp.dot(xb, wg_bf[...], preferred_element_type=F32)
        u = jnp.dot(xb, wu_bf[...], preferred_element_type=F32)
        y = jnp.dot((_silu(a) * u).astype(BF16), wd_bf[...], preferred_element_type=F32)
        y_ref[...] = _pack_halves(y)

    @pl.when(i >= nu_ref[0])
    def _():
        y_ref[...] = jnp.zeros_like(y_ref)


def _moe(blk_expert, n_used, x_sorted, w_gate, w_up, w_down):
    n_blocks = blk_expert.shape[0]
    wspec = lambda shape: pl.BlockSpec((None,) + shape, lambda i, be, nu: (be[i], 0, 0))
    rows = pl.BlockSpec((MOE_ROWS, HALF), lambda i, be, nu: (i, 0))
    grid_spec = pltpu.PrefetchScalarGridSpec(
        num_scalar_prefetch=2,
        grid=(n_blocks,),
        in_specs=[rows, wspec((D_MODEL, D_FF)), wspec((D_MODEL, D_FF)), wspec((D_FF, D_MODEL))],
        out_specs=rows,
        scratch_shapes=[pltpu.VMEM((D_MODEL, D_FF), BF16), pltpu.VMEM((D_MODEL, D_FF), BF16),
                        pltpu.VMEM((D_FF, D_MODEL), BF16)],
    )
    return pl.pallas_call(
        _moe_kernel,
        grid_spec=grid_spec,
        out_shape=jax.ShapeDtypeStruct(x_sorted.shape, U32),
        compiler_params=pltpu.CompilerParams(
            dimension_semantics=("arbitrary",), vmem_limit_bytes=VMEM_LIMIT),
        name="moe",
    )(blk_expert, n_used, x_sorted, w_gate, w_up, w_down)


def _combine_kernel(h2_ref, rw_ref, g_ref, y0_ref, y1_ref, o_ref):
    rw = rw_ref[...]
    h = h2_ref[...] + rw[:, 0:1] * _unpack_halves(y0_ref[...]) + rw[:, 1:2] * _unpack_halves(y1_ref[...])
    o_ref[...] = h * lax.rsqrt(jnp.mean(h * h, axis=-1, keepdims=True) + EPS) * g_ref[...]


def _combine(h2, rweights, final_g, y):
    t = h2.shape[0]
    tm = min(COMBINE_TM, t)
    row = lambda i: (i, 0)
    return pl.pallas_call(
        _combine_kernel,
        grid=(t // tm,),
        in_specs=[pl.BlockSpec((tm, D_MODEL), row),
                  pl.BlockSpec((tm, LANES), row),
                  pl.BlockSpec((1, D_MODEL), lambda i: (0, 0)),
                  pl.BlockSpec((tm, HALF), row),
                  pl.BlockSpec((tm, HALF), lambda i: (i + t // tm, 0))],
        out_specs=pl.BlockSpec((tm, D_MODEL), row),
        out_shape=jax.ShapeDtypeStruct((t, D_MODEL), F32),
        compiler_params=pltpu.CompilerParams(
            dimension_semantics=("arbitrary",), vmem_limit_bytes=VMEM_LIMIT),
        name="combine",
    )(h2, rweights, final_g, y, y)


def _block_layout(ids, counts_row, t):
    n_blocks = 2 * t // MOE_ROWS + N_EXPERTS
    counts = counts_row[0, ROUTE_LANE0:ROUTE_LANE0 + N_EXPERTS].astype(jnp.int32)
    padded = ((counts + MOE_ROWS - 1) // MOE_ROWS) * MOE_ROWS
    pend = jnp.cumsum(padded)
    pstart = pend - padded
    blk_first = jnp.arange(n_blocks, dtype=jnp.int32) * MOE_ROWS
    blk_expert = jnp.minimum(jnp.sum((pend[None, :] <= blk_first[:, None]).astype(jnp.int32), axis=1),
                             N_EXPERTS - 1).astype(jnp.int32)
    n_used = (pend[-1] // MOE_ROWS).astype(jnp.int32).reshape(1)
    expert, rank = ids[2:4], ids[4:6]
    experts = jnp.arange(N_EXPERTS, dtype=jnp.int32)
    first_row = jnp.sum(jnp.where(expert[:, :, None] == experts, pstart, 0), axis=-1)
    dest = (first_row + rank).reshape(2 * t)
    return dest, blk_expert, n_used, n_blocks * MOE_ROWS


def kernel(x, meta_tokens, hg_lb_logits, norm_mix_g, w_in, gd_conv_w, gd_A_log, gd_dt_bias, hg_norm_g, gd_norm_g, hg_up, gd_up, w_out, norm_ffn_g, router_group_w, router_group_b, router_expert_w, router_expert_b, w_gate, w_up, w_down, final_norm_g):
    bsz, seq, d = x.shape
    t = bsz * seq
    x2d = x.reshape(t, d)

    lb = jnp.cumsum(jax.nn.softmax(hg_lb_logits.astype(F32), axis=0), axis=0)[0].reshape(1, WIDTH)
    w_all = w_in.astype(F32)
    w_gates = w_all[0, :, W_SMALL + 2 * HEADS:]
    g_mix = norm_mix_g[0].reshape(1, d)
    meta_blk = jnp.concatenate([jnp.zeros((CHUNK - N_META, d), F32), meta_tokens.astype(F32)], axis=0)

    conv_w = gd_conv_w[0].astype(F32)
    no_history = jnp.zeros((SUBLANES, 3 * WIDTH), F32)
    proj, proj_meta = {}, {}
    proj["hgrn"] = _norm_proj_hgrn(x2d, g_mix, lb, w_all, PROJ_TM)
    proj_meta["hgrn"] = _norm_proj_hgrn(meta_blk, g_mix, lb, w_all, CHUNK)
    proj_meta["gdn"], meta_tail, logits_meta = _norm_proj_conv(meta_blk, g_mix, w_all, conv_w, no_history,
                                                               CHUNK, CHUNK)
    proj["gdn"], _, logits = _norm_proj_conv(x2d, g_mix, w_all, conv_w, meta_tail, PROJ_TM, seq)

    def scalar_rows(lg, nb, nc):
        s = lg.reshape(2, N_STACKS, HEADS_PER_STACK, nb, nc, CHUNK)
        return s.transpose(3, 4, 1, 0, 2, 5).reshape(nb, nc, N_STACKS, 2, STACK)

    a_log_row = jnp.repeat(gd_A_log[0].astype(F32), CHUNK).reshape(N_STACKS, 1, STACK)
    dt_row = jnp.repeat(gd_dt_bias[0].astype(F32), CHUNK).reshape(N_STACKS, 1, STACK)
    hg_g, gd_g = hg_norm_g[0].reshape(1, HEAD_DIM), gd_norm_g[0].reshape(1, HEAD_DIM)

    zero_state = jnp.zeros((HEADS, HEAD_DIM, HEAD_DIM), F32)
    per_seq = lambda st: jnp.tile(st, (bsz, 1, 1))
    _, hg_state = _hgrn(proj_meta["hgrn"][None], zero_state, hg_g, 1)
    o_a, _ = _hgrn(proj["hgrn"].reshape(bsz, seq, -1), per_seq(hg_state), hg_g, HGRN_CHUNKS)
    _, gd_state = _gdn(proj_meta["gdn"][None], zero_state, scalar_rows(logits_meta, 1, 1),
                       a_log_row, dt_row, gd_g, 1)
    o_b, _ = _gdn(proj["gdn"].reshape(bsz, seq, -1), per_seq(gd_state),
                  scalar_rows(logits, bsz, seq // CHUNK), a_log_row, dt_row, gd_g, GDN_CHUNKS)

    rw = jnp.zeros((d, LANES), F32)
    rw = rw.at[:, :N_GROUPS].set(router_group_w[0]).at[:, N_GROUPS:N_GROUPS + N_EXPERTS].set(router_expert_w[0])
    rb = jnp.zeros((1, LANES), F32)
    rb = rb.at[0, :N_GROUPS].set(router_group_b[0]).at[0, N_GROUPS:N_GROUPS + N_EXPERTS].set(router_expert_b[0])
    h2, xn, info, ids, counts = _merge(o_a.reshape(t, WIDTH), o_b.reshape(t, WIDTH), x2d, g_mix, w_gates,
                                  hg_up[0].astype(BF16), gd_up[0].astype(BF16), w_out[0].astype(BF16),
                                  norm_ffn_g[0].reshape(1, d), rw.astype(BF16), rb)

    dest, blk_expert, n_used, n_rows = _block_layout(ids, counts, t)
    x_sorted = _sc_dispatch(xn, dest, n_rows)
    y_sorted = _moe(blk_expert, n_used, x_sorted,
                    w_gate[0].astype(F32), w_up[0].astype(F32), w_down[0].astype(F32))
    y_tok = _sc_gather(y_sorted, dest)
    out = _combine(h2, info, final_norm_g.reshape(1, d), y_tok)
    return out.reshape(bsz, seq, d)
```

```python
import functools

import numpy as np
import jax
import jax.numpy as jnp
from jax import lax
from jax.experimental import pallas as pl
from jax.experimental.pallas import tpu as pltpu
from jax.experimental.pallas import tpu_sc as plsc

F32 = jnp.float32
BF16 = jnp.bfloat16
U32 = jnp.uint32

D_MODEL = 1024
N_META = 16
CHUNK_BITS = 6
CHUNK = 1 << CHUNK_BITS
EPS = 1e-6
HEADS = 4
HEAD_DIM = 128
WIDTH = HEADS * HEAD_DIM
HEADS_PER_STACK = 2
STACK = HEADS_PER_STACK * CHUNK
N_STACKS = HEADS // HEADS_PER_STACK
CONV_W = 4
NEUMANN_BITS = 3
N_GROUPS = 4
EXPERTS_PER_GROUP = 8
N_EXPERTS = N_GROUPS * EXPERTS_PER_GROUP
D_FF = 512
LANES = 128
SUBLANES = 8
HALF = D_MODEL // 2
MOE_ROWS = 512
CAST_SLAB = 128
SC_CORES, SC_SUBCORES = 2, 16
SC_WORKERS = SC_CORES * SC_SUBCORES
SC_CHUNK = 128
PROJ_TM = 1024
W_HGRN, W_GDN, W_SMALL = 0, 4 * WIDTH, 8 * WIDTH
HGRN_CHUNKS, GDN_CHUNKS = 4, 2
TOKEN_TM = 512
COMBINE_TM = 2048
VMEM_LIMIT = 56 * 1024 * 1024


def _bdot(a, b):
    return lax.dot_general(a.astype(BF16), b.astype(BF16), (((2,), (1,)), ((0,), (0,))),
                           preferred_element_type=F32)


def _bdot_nt(a, b):
    return lax.dot_general(a.astype(BF16), b.astype(BF16), (((2,), (2,)), ((0,), (0,))),
                           preferred_element_type=F32)


def _bdot_tn(a, b):
    return lax.dot_general(a.astype(BF16), b.astype(BF16), (((1,), (1,)), ((0,), (0,))),
                           preferred_element_type=F32)


def _split3(x):
    hi = x.astype(BF16)
    r1 = x - hi.astype(F32)
    mid = r1.astype(BF16)
    lo = (r1 - mid.astype(F32)).astype(BF16)
    return hi, mid, lo


def _sigmoid(x):
    return 1.0 / (1.0 + jnp.exp(-x))


def _silu(x):
    return x * _sigmoid(x)


def _softplus(x):
    return jnp.maximum(x, 0.0) + jnp.log(1.0 + jnp.exp(-jnp.abs(x)))


def _stack_heads(x, bsz):
    x3 = x.reshape(bsz, CHUNK, WIDTH)
    y = jnp.concatenate([x3[:, :, h * HEAD_DIM:(h + 1) * HEAD_DIM] for h in range(HEADS)], axis=1)
    return y.reshape(bsz * HEADS * CHUNK, HEAD_DIM)


def _unstack_heads(x, bsz):
    x3 = x.reshape(bsz, HEADS * CHUNK, HEAD_DIM)
    return jnp.concatenate([x3[:, h * CHUNK:(h + 1) * CHUNK, :] for h in range(HEADS)], axis=2)


def _to_stacks(a, bsz):
    return a.reshape(bsz * N_STACKS, STACK, a.shape[-1])


def _to_heads(a, bsz):
    return a.reshape(bsz * HEADS, CHUNK, a.shape[-1])


def _roll_rows(x, shift):
    n = x.shape[0]
    return pltpu.roll(x, shift % n, axis=0)


def _chunks_first(x, nc):
    return jnp.concatenate([x[:, j * CHUNK:(j + 1) * CHUNK] for j in range(nc)], axis=0)


def _chunks_first_scalars(s, nc):
    return jnp.concatenate([s[:, j] for j in range(nc)], axis=0)


def _load_chunks(ref, nc):
    x = _chunks_first(ref[...], nc)
    return x.reshape(x.shape[0] * CHUNK, WIDTH)


def _store_chunks(ref, x, nb, nc):
    ref[...] = jnp.concatenate([x[j * nb:(j + 1) * nb] for j in range(nc)], axis=1).astype(ref.dtype)


def _cast_rows(src_ref, dst_ref, col0=0):
    slab = CAST_SLAB
    rows, cols = src_ref.shape
    assert rows % slab == 0

    def body(k, carry):
        r = pl.multiple_of(k * slab, slab)
        dst_ref[pl.ds(r, slab), col0:col0 + cols] = src_ref[pl.ds(r, slab), :].astype(dst_ref.dtype)
        return carry

    lax.fori_loop(0, rows // slab, body, 0)


def _pack_halves(x):
    lo = lax.bitcast_convert_type(x[:, :HALF].astype(BF16).astype(F32), U32)
    hi = lax.bitcast_convert_type(x[:, HALF:].astype(BF16).astype(F32), U32)
    return (lo >> 16) | (hi & jnp.uint32(0xFFFF0000))


def _unpack_halves(p):
    lo = lax.bitcast_convert_type(p << 16, F32)
    hi = lax.bitcast_convert_type(p & jnp.uint32(0xFFFF0000), F32)
    return jnp.concatenate([lo, hi], axis=1)


def _normed_bf16(x_ref, g_ref):
    x = x_ref[...]
    return (x * lax.rsqrt(jnp.mean(x * x, axis=-1, keepdims=True) + EPS) * g_ref[...]).astype(BF16)


def _norm_proj_hgrn_kernel(x_ref, g_ref, lb_ref, w_ref, o_ref, wbf_ref):
    @pl.when(pl.program_id(0) == 0)
    def _():
        _cast_rows(w_ref, wbf_ref)

    acc = jnp.dot(_normed_bf16(x_ref, g_ref), wbf_ref[...], preferred_element_type=F32)
    w1, w2, w3 = WIDTH, 2 * WIDTH, 3 * WIDTH
    lb = lb_ref[...]
    o_ref[:, :w1] = _silu(acc[:, :w1])
    o_ref[:, w1:w2] = lb + (1.0 - lb) * _sigmoid(acc[:, w1:w2])
    o_ref[:, w2:w3] = acc[:, w2:w3]
    o_ref[:, w3:] = _silu(acc[:, w3:])


def _zero_rows(ref):
    slab = min(CAST_SLAB, ref.shape[0])
    assert ref.shape[0] % slab == 0

    def body(k, carry):
        ref[pl.ds(pl.multiple_of(k * slab, slab), slab), :] = jnp.zeros((slab, ref.shape[1]), ref.dtype)
        return carry

    lax.fori_loop(0, ref.shape[0] // slab, body, 0)


def _norm_proj_conv_kernel(x_ref, g_ref, w_ref, ws_ref, cw_ref, hist_ref, o_ref, tail_ref, logit_ref,
                           wbf_ref, acc_ref, carry_ref, *, steps_per_seq):
    i = pl.program_id(0)

    @pl.when(i == 0)
    def _():
        _cast_rows(w_ref, wbf_ref)
        _cast_rows(ws_ref, wbf_ref, 4 * WIDTH)
        _zero_rows(acc_ref)
        carry_ref[...] = jnp.zeros_like(carry_ref)

    xn = _normed_bf16(x_ref, g_ref)
    w3, w4 = 3 * WIDTH, 4 * WIDTH
    rows = x_ref.shape[0]
    first_of_seq = (i - 1) % steps_per_seq == 0
    cw = cw_ref[...]
    r8 = lax.broadcasted_iota(jnp.int32, (SUBLANES, WIDTH), 0)
    for c0 in range(0, w3, WIDTH):
        cols = slice(c0, c0 + WIDTH)
        raw = acc_ref[:, cols]
        hist = jnp.where(first_of_seq, hist_ref[:, cols], carry_ref[:, cols])
        tap = lambda d: cw[CONV_W - 1 - d:CONV_W - d, cols]
        body = raw * tap(0)
        head = raw[:SUBLANES]
        first = head * tap(0)
        for d in range(1, CONV_W):
            body = body + _roll_rows(raw, d) * tap(d)
            first = first + jnp.where(r8 >= d, _roll_rows(head, d), _roll_rows(hist, d)) * tap(d)
        o_ref[:SUBLANES, cols] = _silu(first)
        o_ref[SUBLANES:, cols] = _silu(body[SUBLANES:])
        carry_ref[:, cols] = raw[rows - SUBLANES:]
        acc_ref[:, cols] = jnp.dot(xn, wbf_ref[:, cols], preferred_element_type=F32)
    rest = acc_ref[:, w3:]
    o_ref[:, w3:] = _silu(rest[:, :WIDTH])
    logit_ref[...] = rest[:, WIDTH:].T[:SUBLANES]
    acc_ref[:, w3:] = jnp.dot(xn, wbf_ref[:, w3:], preferred_element_type=F32)
    tail_ref[...] = carry_ref[...]


def _proj_call(kernel_fn, name, x2d, tm, n, operands, specs, extra_out=(), extra_scratch=(),
               weight_cols=None, delayed=False):
    m, k = x2d.shape
    weight_cols = n if weight_cols is None else weight_cols
    tm = min(tm, m)
    tiles = m // tm
    x_map = (lambda i: (jnp.minimum(i, tiles - 1), 0)) if delayed else (lambda i: (i, 0))
    o_map = (lambda i: (jnp.maximum(i - 1, 0), 0)) if delayed else (lambda i: (i, 0))
    out_specs = [pl.BlockSpec((tm, n), o_map)] + [s for s, _ in extra_out]
    out_shape = [jax.ShapeDtypeStruct((m, n), F32)] + [s for _, s in extra_out]
    return pl.pallas_call(
        kernel_fn,
        grid=(tiles + int(delayed),),
        in_specs=[pl.BlockSpec((tm, k), x_map)] + specs,
        out_specs=out_specs,
        out_shape=out_shape,
        scratch_shapes=[pltpu.VMEM((k, weight_cols), BF16)] + list(extra_scratch),
        compiler_params=pltpu.CompilerParams(
            dimension_semantics=("arbitrary",), vmem_limit_bytes=VMEM_LIMIT),
        name=name,
    )(x2d, *operands)


def _weight_spec(k, n, col0):
    assert col0 % n == 0
    return pl.BlockSpec((None, k, n), lambda i: (0, 0, col0 // n), pipeline_mode=pl.Buffered(1))


def _norm_proj_hgrn(x2d, g, lb, w, tm):
    k, n = w.shape[1], 4 * WIDTH
    const = lambda i: (0, 0)
    specs = [pl.BlockSpec((1, k), const), pl.BlockSpec((1, WIDTH), const), _weight_spec(k, n, W_HGRN)]
    return _proj_call(_norm_proj_hgrn_kernel, "norm_proj_hgrn", x2d, tm, n, (g, lb, w), specs)[0]


def _norm_proj_conv(x2d, g, w, conv_w, hist, tm, rows_per_seq):
    m, k = x2d.shape[0], w.shape[1]
    n_out, n_w = 4 * WIDTH, 4 * WIDTH + LANES
    const = lambda i: (0, 0)
    tm = min(tm, rows_per_seq)
    assert rows_per_seq % tm == 0 and m % rows_per_seq == 0
    tail_shape = (SUBLANES, 3 * WIDTH)
    specs = [pl.BlockSpec((1, k), const), _weight_spec(k, 4 * WIDTH, W_GDN), _weight_spec(k, LANES, W_SMALL),
             pl.BlockSpec((CONV_W, 3 * WIDTH), const), pl.BlockSpec(tail_shape, const)]
    logit_spec = pl.BlockSpec((SUBLANES, tm), lambda i: (0, jnp.maximum(i - 1, 0)))
    return _proj_call(functools.partial(_norm_proj_conv_kernel, steps_per_seq=rows_per_seq // tm),
                      "norm_proj_gdn", x2d, tm, n_out, (g, w, w, conv_w, hist), specs,
                      extra_out=[(pl.BlockSpec(tail_shape, const), jax.ShapeDtypeStruct(tail_shape, F32)),
                                 (logit_spec, jax.ShapeDtypeStruct((SUBLANES, m), F32))],
                      extra_scratch=[pltpu.VMEM((tm, n_w), F32), pltpu.VMEM(tail_shape, F32)],
                      weight_cols=n_w, delayed=True)


HGRN_LEVELS = 6
HGRN_DIAG = HGRN_LEVELS


def _hgrn_level_table():
    r = np.arange(STACK)
    t, hd = r % CHUNK, r // CHUNK
    x = t[:, None] ^ t[None, :]
    lv = np.floor(np.log2(np.maximum(x, 1))).astype(np.int32)
    valid = (hd[:, None] == hd[None, :]) & (t[:, None] > t[None, :])
    lv = np.where(valid, lv, -1)
    lv = np.where(r[:, None] == r[None, :], HGRN_DIAG, lv)
    return jnp.asarray(lv, dtype=jnp.int32)


def _hgrn_kernel(lv_ref, g_ref, st0_ref, q_ref, f_ref, v_ref, gate_ref, o_ref, stout_ref, st_ref, *, nb, nc):
    @pl.when(pl.program_id(0) == 0)
    def _():
        st_ref[...] = st0_ref[...]

    bsz = nc * nb
    q = _load_chunks(q_ref, nc)
    f = _load_chunks(f_ref, nc)
    v = _load_chunks(v_ref, nc)
    gate = _load_chunks(gate_ref, nc)

    per_seq = lambda a: a.reshape(bsz, CHUNK, WIDTH)
    trow = lax.broadcasted_iota(jnp.int32, (1, CHUNK, WIDTH), 1)
    b = jnp.log2(f)
    s = 1
    while s < CHUNK:
        b = (per_seq(b) + jnp.where(trow >= s, per_seq(_roll_rows(b, s)), 0.0)).reshape(b.shape)
        s *= 2

    qs, ks, vs, bs = (_stack_heads(a, bsz) for a in (q, 1.0 - f, v, b))
    lv = lv_ref[...][None]
    to3 = functools.partial(_to_stacks, bsz=bsz)
    to_g = functools.partial(_to_heads, bsz=bsz)
    roll_g = lambda a, shift: to_g(_roll_rows(a.reshape(qs.shape), shift))
    t = lax.broadcasted_iota(jnp.int32, (1, CHUNK, HEAD_DIM), 1)
    qs_g, ks_g, bs_g = to_g(qs), to_g(ks), to_g(bs)

    attn = jnp.where(lv == HGRN_DIAG, _bdot_nt(to3(qs), to3(ks)), 0.0)
    bref = bs_g
    bnext = roll_g(bs_g, -1)
    for li in range(HGRN_LEVELS):
        m = 1 << li
        if li > 0:
            half = m // 2
            upper = (t & (m - 1)) >= half
            if half < SUBLANES:
                grp = lambda a: a.reshape(-1, SUBLANES, HEAD_DIM)
                local = lambda a, sh: pltpu.roll(grp(a), sh % SUBLANES, axis=1).reshape(a.shape)
                bref = jnp.where(upper, local(bref, half), bref)
                bnext = jnp.where(upper, bnext, local(bnext, -half))
            else:
                bref = jnp.where(upper, roll_g(bref, half), bref)
                bnext = jnp.where(upper, bnext, roll_g(bnext, -half))
        odd = ((t >> li) & 1) == 1
        both = jnp.where(odd, qs_g, ks_g) * jnp.exp2(jnp.where(odd, bs_g - bref, bnext - bs_g))
        both = to3(both.reshape(qs.shape)).astype(BF16)
        attn = jnp.where(lv == li, _bdot_nt(both, both), attn)

    o = _bdot(attn, to3(vs))

    n_g = nb * HEADS
    st = st_ref[...]
    qe = qs_g * jnp.exp2(bs_g)
    b_end = bs_g[:, CHUNK - 1:CHUNK, :]
    ke = ks_g * jnp.exp2(b_end - bs_g)
    vs_g = to_g(vs)
    o_inter = []
    for j in range(nc):
        sl = slice(j * n_g, (j + 1) * n_g)
        o_inter.append(_bdot_nt(qe[sl], st))
        st = st * jnp.exp2(b_end[sl]) + _bdot_tn(vs_g[sl], ke[sl])
    st_ref[...] = st
    stout_ref[...] = st
    o = (to_g(o) + jnp.concatenate(o_inter, axis=0)).reshape(qs.shape)

    o = o * lax.rsqrt(jnp.mean(o * o, axis=-1, keepdims=True) + EPS) * g_ref[...]
    o = o * _stack_heads(gate, bsz)
    _store_chunks(o_ref, _unstack_heads(o, bsz), nb, nc)


def _gdn_cumsum_matrix():
    r = np.arange(STACK)
    u = (r[:, None] // CHUNK == r[None, :] // CHUNK) & (r[:, None] <= r[None, :])
    return jnp.asarray(u, dtype=BF16)


def _gdn_kernel(u_ref, alog_ref, dt_ref, g_ref, s_ref, st0_ref, q_ref, k_ref, v_ref, z_ref,
                o_ref, stout_ref, st_ref, *, nb, nc):
    @pl.when(pl.program_id(0) == 0)
    def _():
        st_ref[...] = st0_ref[...]

    bsz = nc * nb
    qs = _stack_heads(_load_chunks(q_ref, nc), bsz)
    ks = _stack_heads(_load_chunks(k_ref, nc), bsz)
    vs = _stack_heads(_load_chunks(v_ref, nc), bsz)
    gate = _load_chunks(z_ref, nc)
    qs = qs * lax.rsqrt(jnp.sum(qs * qs, axis=-1, keepdims=True) + EPS) * (HEAD_DIM ** -0.5)
    ks = ks * lax.rsqrt(jnp.sum(ks * ks, axis=-1, keepdims=True) + EPS)
    to3 = functools.partial(_to_stacks, bsz=bsz)
    qs, ks, vs = to3(qs), to3(ks), to3(vs)
    n_st = bsz * N_STACKS

    srow = _chunks_first_scalars(s_ref[...], nc).reshape(n_st, 2, STACK)
    per_stack = lambda ref: jnp.broadcast_to(ref[...][None], (bsz, N_STACKS, 1, STACK)).reshape(n_st, 1, STACK)
    beta_row = _sigmoid(srow[:, 0:1, :])
    g_row = -jnp.exp(per_stack(alog_ref)) * _softplus(srow[:, 1:2, :] + per_stack(dt_ref))
    n8 = n_st * SUBLANES
    g8 = jnp.broadcast_to(g_row, (n_st, SUBLANES, STACK)).reshape(n8, STACK)
    pieces = jnp.concatenate(_split3(g8), axis=0)
    cum = jnp.dot(pieces, u_ref[...], preferred_element_type=F32)
    cum = cum[0:n8] + cum[n8:2 * n8] + cum[2 * n8:]
    cum_row = cum.reshape(n_st, SUBLANES, STACK)[:, 0:1, :]

    r = lax.broadcasted_iota(jnp.int32, (STACK, STACK), 0)
    cc = lax.broadcasted_iota(jnp.int32, (STACK, STACK), 1)
    eye = (r == cc)[None]
    cum_col = jnp.sum(jnp.where(eye, cum_row, 0.0), axis=2, keepdims=True)
    beta_col = jnp.sum(jnp.where(eye, beta_row, 0.0), axis=2, keepdims=True)
    same = (r >> CHUNK_BITS) == (cc >> CHUNK_BITS)
    lower = (same & (r >= cc))[None]
    strict = (same & (r > cc))[None]
    ratio = jnp.exp(jnp.minimum(cum_col - cum_row, 0.0))

    kkt = _bdot_nt(ks, ks)
    qkt = _bdot_nt(qs, ks)
    lm = jnp.where(strict, beta_col * ratio * kkt, 0.0)

    ident = eye.astype(F32)
    l0 = jnp.where(((r >> NEUMANN_BITS) == (cc >> NEUMANN_BITS))[None], lm, 0.0)
    l2 = _bdot(l0, l0)
    l4 = _bdot(l2, l2)
    xinv = _bdot(_bdot(ident - l0, ident + l2), ident + l4)
    for bits in range(NEUMANN_BITS, CHUNK_BITS):
        off = (((r >> (bits + 1)) == (cc >> (bits + 1))) & ((r >> bits) != (cc >> bits)))[None]
        xinv = xinv - _bdot(_bdot(xinv, jnp.where(off, lm, 0.0)), xinv)

    decay_col = jnp.exp(cum_col)
    rhs = jnp.concatenate([beta_col * vs, (beta_col * decay_col) * ks], axis=2)
    sol = _bdot(xinv, rhs)

    to_g = functools.partial(_to_heads, bsz=bsz)
    n_g = nb * HEADS
    st = st_ref[...]
    u0_g, w_g, q_g = to_g(sol[:, :, :HEAD_DIM]), to_g(sol[:, :, HEAD_DIM:]), to_g(qs)
    cum_g = to_g(cum_col)
    cum_end = cum_g[:, CHUNK - 1:CHUNK, :]
    kdec = to_g(ks) * jnp.exp(cum_end - cum_g)
    decay_g = to_g(decay_col)
    us, o_inter = [], []
    for j in range(nc):
        sl = slice(j * n_g, (j + 1) * n_g)
        u_j = u0_g[sl] - _bdot(w_g[sl], st)
        us.append(u_j)
        o_inter.append(decay_g[sl] * _bdot(q_g[sl], st))
        st = jnp.exp(cum_end[sl]) * st + _bdot_tn(kdec[sl], u_j)
    st_ref[...] = st
    stout_ref[...] = st
    attn = jnp.where(lower, qkt * ratio, 0.0)
    u = jnp.concatenate(us, axis=0).reshape(bsz * HEADS * CHUNK, HEAD_DIM)
    o = jnp.concatenate(o_inter, axis=0) + to_g(_bdot(attn, to3(u)))
    o = o.reshape(bsz * HEADS * CHUNK, HEAD_DIM)

    o = o * lax.rsqrt(jnp.mean(o * o, axis=-1, keepdims=True) + EPS) * g_ref[...]
    o = o * _stack_heads(gate, bsz)
    _store_chunks(o_ref, _unstack_heads(o, bsz), nb, nc)


def _recurrence_call(kernel_fn, name, proj, state0, operands, specs, nc):
    nb, length, _ = proj.shape
    rows = nc * CHUNK
    state_shape = (nb * HEADS, HEAD_DIM, HEAD_DIM)
    state_spec = pl.BlockSpec(state_shape, lambda c: (0, 0, 0))
    chunk_specs = [pl.BlockSpec((nb, rows, WIDTH), functools.partial(lambda c, j: (0, c, j), j=j))
                   for j in range(4)]
    return pl.pallas_call(
        functools.partial(kernel_fn, nb=nb, nc=nc),
        grid=(length // rows,),
        in_specs=specs + [state_spec] + chunk_specs,
        out_specs=[pl.BlockSpec((nb, rows, WIDTH), lambda c: (0, c, 0)), state_spec],
        out_shape=[jax.ShapeDtypeStruct((nb, length, WIDTH), BF16),
                   jax.ShapeDtypeStruct(state_shape, F32)],
        scratch_shapes=[pltpu.VMEM(state_shape, F32)],
        compiler_params=pltpu.CompilerParams(
            dimension_semantics=("arbitrary",), vmem_limit_bytes=VMEM_LIMIT),
        name=name,
    )(*operands, state0, *([proj] * 4))


def _hgrn(proj, state0, norm_g, nc):
    const = lambda c: (0, 0)
    specs = [pl.BlockSpec((STACK, STACK), const), pl.BlockSpec((1, HEAD_DIM), const)]
    return _recurrence_call(_hgrn_kernel, "hgrn", proj, state0, (_hgrn_level_table(), norm_g), specs, nc)


def _gdn(proj, state0, scal, a_log_row, dt_row, norm_g, nc):
    nb = proj.shape[0]
    const = lambda c: (0, 0)
    const3 = lambda c: (0, 0, 0)
    specs = [pl.BlockSpec((STACK, STACK), const),
             pl.BlockSpec((N_STACKS, 1, STACK), const3),
             pl.BlockSpec((N_STACKS, 1, STACK), const3),
             pl.BlockSpec((1, HEAD_DIM), const),
             pl.BlockSpec((nb, nc, N_STACKS, 2, STACK), lambda c: (0, c, 0, 0, 0))]
    return _recurrence_call(_gdn_kernel, "gdn", proj, state0,
                            (_gdn_cumsum_matrix(), a_log_row, dt_row, norm_g, scal), specs, nc)


ROUTE_LANE0 = N_GROUPS
ROUTE_NEG = -1e30


def _route_tile(lg, ltri, carry):
    lane = lax.broadcasted_iota(jnp.int32, lg.shape, 1).astype(F32)
    first = lambda mask: jnp.min(jnp.where(mask, lane, float(LANES)), axis=1, keepdims=True)
    top = lambda mask: jnp.max(jnp.where(mask, lg, ROUTE_NEG), axis=1, keepdims=True)

    is_g = lane < N_GROUPS
    gmax = top(is_g)
    grp = first(is_g & (lg == gmax))
    p_grp = 1.0 / jnp.sum(jnp.where(is_g, jnp.exp(lg - gmax), 0.0), axis=1, keepdims=True)

    lo = ROUTE_LANE0 + grp * EXPERTS_PER_GROUP
    in_grp = (lane >= lo) & (lane < lo + EXPERTS_PER_GROUP)
    m1 = top(in_grp)
    i1 = first(in_grp & (lg == m1))
    rest = in_grp & (lane != i1)
    m2 = top(rest)
    i2 = first(rest & (lg == m2))
    r = jnp.exp(m2 - m1)
    w0 = p_grp / (1.0 + r)
    w1 = p_grp * r / (1.0 + r)

    hot0 = (lane == i1).astype(F32)
    hot1 = (lane == i2).astype(F32)
    both = hot0 + hot1
    before = jnp.dot(ltri, both.astype(BF16), preferred_element_type=F32) + carry
    rank0 = jnp.sum(hot0 * before, axis=1, keepdims=True)
    rank1 = jnp.sum(hot1 * before, axis=1, keepdims=True)
    cols = (w0, w1, i1 - ROUTE_LANE0, i2 - ROUTE_LANE0, rank0, rank1)
    info = jnp.zeros_like(lg)
    for k, col in enumerate(cols):
        info = jnp.where(lane == k, col, info)
    return info, carry + jnp.sum(both, axis=0, keepdims=True)


def _merge_kernel(oa_ref, ob_ref, x_ref, xprev_ref, mg_ref, wgate_ref, hgup_ref, gdup_ref, wout_ref,
                  ng_ref, rw_ref, rb_ref, ltri_ref, h2_ref, xn_ref, info_ref, ids_ref, cnt_ref,
                  wgate_bf, gates_ref, ua_ref, ub_ref, carry_ref):
    i = pl.program_id(0)

    @pl.when(i == 0)
    def _():
        carry_ref[...] = jnp.zeros_like(carry_ref)
        _cast_rows(wgate_ref, wgate_bf)
        _zero_rows(gates_ref)
        _zero_rows(ua_ref)
        _zero_rows(ub_ref)

    gates = _sigmoid(gates_ref[...])
    merged = (gates[:, :D_MODEL] * ua_ref[...] + gates[:, D_MODEL:] * ub_ref[...]).astype(BF16)
    gates_ref[...] = jnp.dot(_normed_bf16(x_ref, mg_ref), wgate_bf[...], preferred_element_type=F32)
    h2 = xprev_ref[...] + jnp.dot(merged, wout_ref[...], preferred_element_type=F32)
    h2_ref[...] = h2
    xn = h2 * lax.rsqrt(jnp.mean(h2 * h2, axis=-1, keepdims=True) + EPS) * ng_ref[...]
    xn_ref[...] = _pack_halves(xn)
    lg = jnp.dot(xn.astype(BF16), rw_ref[...], preferred_element_type=F32) + rb_ref[...]
    ua_ref[...] = jnp.dot(oa_ref[...], hgup_ref[...], preferred_element_type=F32)
    ub_ref[...] = jnp.dot(ob_ref[...], gdup_ref[...], preferred_element_type=F32)
    old = carry_ref[...]
    info, carry = _route_tile(lg, ltri_ref[...], old)
    info_ref[...] = info
    ids_ref[...] = info.T[:SUBLANES].astype(jnp.int32)
    carry = jnp.where(i > 0, carry, old)
    carry_ref[...] = carry
    cnt_ref[...] = carry


def _merge(o_a, o_b, x2d, mix_g, w_gates, hg_up, gd_up, w_out, norm_g, rw, rb):
    t = x2d.shape[0]
    tm = TOKEN_TM
    tiles = t // tm
    new = lambda i: (jnp.minimum(i, tiles - 1), 0)
    row = lambda i: (jnp.maximum(i - 1, 0), 0)
    const = lambda i: (0, 0)
    ltri = jnp.asarray(np.tril(np.ones((tm, tm), np.float32), -1), dtype=BF16)
    return pl.pallas_call(
        _merge_kernel,
        grid=(tiles + 1,),
        in_specs=[pl.BlockSpec((tm, WIDTH), new),
                  pl.BlockSpec((tm, WIDTH), new),
                  pl.BlockSpec((tm, D_MODEL), new),
                  pl.BlockSpec((tm, D_MODEL), row),
                  pl.BlockSpec((1, D_MODEL), const),
                  pl.BlockSpec((D_MODEL, 2 * D_MODEL), const, pipeline_mode=pl.Buffered(1)),
                  pl.BlockSpec((WIDTH, D_MODEL), const),
                  pl.BlockSpec((WIDTH, D_MODEL), const),
                  pl.BlockSpec((D_MODEL, D_MODEL), const),
                  pl.BlockSpec((1, D_MODEL), const),
                  pl.BlockSpec((D_MODEL, LANES), const),
                  pl.BlockSpec((1, LANES), const),
                  pl.BlockSpec((tm, tm), const)],
        out_specs=[pl.BlockSpec((tm, D_MODEL), row),
                   pl.BlockSpec((tm, HALF), row),
                   pl.BlockSpec((tm, LANES), row),
                   pl.BlockSpec((SUBLANES, tm), lambda i: (0, jnp.maximum(i - 1, 0))),
                   pl.BlockSpec((1, LANES), const)],
        out_shape=[jax.ShapeDtypeStruct((t, D_MODEL), F32),
                   jax.ShapeDtypeStruct((t, HALF), U32),
                   jax.ShapeDtypeStruct((t, LANES), F32),
                   jax.ShapeDtypeStruct((SUBLANES, t), jnp.int32),
                   jax.ShapeDtypeStruct((1, LANES), F32)],
        scratch_shapes=[pltpu.VMEM((D_MODEL, 2 * D_MODEL), BF16), pltpu.VMEM((tm, 2 * D_MODEL), F32),
                        pltpu.VMEM((tm, D_MODEL), F32), pltpu.VMEM((tm, D_MODEL), F32),
                        pltpu.VMEM((1, LANES), F32)],
        compiler_params=pltpu.CompilerParams(
            dimension_semantics=("arbitrary",), vmem_limit_bytes=VMEM_LIMIT),
        name="merge",
    )(o_a, o_b, x2d, x2d, mix_g, w_gates, hg_up, gd_up, w_out, norm_g, rw, rb, ltri)


def _sc_gather(table, idx):
    n_idx = idx.shape[0]
    cols = table.shape[1]
    per_worker = n_idx // SC_WORKERS
    assert n_idx % (SC_WORKERS * SC_CHUNK) == 0
    mesh = plsc.VectorSubcoreMesh(core_axis_name="c", subcore_axis_name="s")

    @functools.partial(
        pl.kernel, mesh=mesh,
        out_type=jax.ShapeDtypeStruct((n_idx, cols), table.dtype),
        scratch_types=[pltpu.VMEM((SC_CHUNK,), jnp.int32),
                       pltpu.VMEM((SC_CHUNK, cols), table.dtype),
                       pltpu.SemaphoreType.DMA],
    )
    def gather(table_hbm, idx_hbm, out_hbm, idx_v, rows_v, sem):
        worker = lax.axis_index("s") * SC_CORES + lax.axis_index("c")
        base = worker * per_worker

        @pl.loop(0, per_worker // SC_CHUNK)
        def _(c):
            off = pl.multiple_of(base + c * SC_CHUNK, SC_CHUNK)
            pltpu.sync_copy(idx_hbm.at[pl.ds(off, SC_CHUNK)], idx_v)
            pltpu.async_copy(table_hbm.at[idx_v], rows_v, sem).wait()
            pltpu.sync_copy(rows_v, out_hbm.at[pl.ds(off, SC_CHUNK)])

    return gather(table, idx)


def _sc_dispatch(rows, dest, n_out):
    t, cols = rows.shape
    per_worker = t // SC_WORKERS
    assert t % (SC_WORKERS * SC_CHUNK) == 0 and dest.shape[0] == 2 * t
    mesh = plsc.VectorSubcoreMesh(core_axis_name="c", subcore_axis_name="s")

    @functools.partial(
        pl.kernel, mesh=mesh,
        out_type=jax.ShapeDtypeStruct((n_out, cols), rows.dtype),
        scratch_types=[pltpu.VMEM((SC_CHUNK,), jnp.int32),
                       pltpu.VMEM((SC_CHUNK, cols), rows.dtype)],
    )
    def dispatch(rows_hbm, dest_hbm, out_hbm, idx_v, rows_v):
        worker = lax.axis_index("s") * SC_CORES + lax.axis_index("c")
        base = worker * per_worker

        @pl.loop(0, per_worker // SC_CHUNK)
        def _(c):
            off = pl.multiple_of(base + c * SC_CHUNK, SC_CHUNK)
            pltpu.sync_copy(rows_hbm.at[pl.ds(off, SC_CHUNK)], rows_v)
            for slot in range(2):
                pltpu.sync_copy(dest_hbm.at[pl.ds(slot * t + off, SC_CHUNK)], idx_v)
                pltpu.sync_copy(rows_v, out_hbm.at[idx_v])

    return dispatch(rows, dest)


def _moe_kernel(be_ref, nu_ref, x_ref, wg_ref, wu_ref, wd_ref, y_ref, wg_bf, wu_bf, wd_bf):
    i = pl.program_id(0)

    @pl.when((i == 0) | (be_ref[i] != be_ref[jnp.maximum(i - 1, 0)]))
    def _():
        _cast_rows(wg_ref, wg_bf)
        _cast_rows(wu_ref, wu_bf)
        _cast_rows(wd_ref, wd_bf)

    @pl.when(i < nu_ref[0])
    def _():
        xb = _unpack_halves(x_ref[...]).astype(BF16)
        a = jnp.dot(xb, wg_bf[...], preferred_element_type=F32)
        u = jnp.dot(xb, wu_bf[...], preferred_element_type=F32)
        y = jnp.dot((_silu(a) * u).astype(BF16), wd_bf[...], preferred_element_type=F32)
        y_ref[...] = _pack_halves(y)

    @pl.when(i >= nu_ref[0])
    def _():
        y_ref[...] = jnp.zeros_like(y_ref)


def _moe(blk_expert, n_used, x_sorted, w_gate, w_up, w_down):
    n_blocks = blk_expert.shape[0]
    wspec = lambda shape: pl.BlockSpec((None,) + shape, lambda i, be, nu: (be[i], 0, 0))
    rows = pl.BlockSpec((MOE_ROWS, HALF), lambda i, be, nu: (i, 0))
    grid_spec = pltpu.PrefetchScalarGridSpec(
        num_scalar_prefetch=2,
        grid=(n_blocks,),
        in_specs=[rows, wspec((D_MODEL, D_FF)), wspec((D_MODEL, D_FF)), wspec((D_FF, D_MODEL))],
        out_specs=rows,
        scratch_shapes=[pltpu.VMEM((D_MODEL, D_FF), BF16), pltpu.VMEM((D_MODEL, D_FF), BF16),
                        pltpu.VMEM((D_FF, D_MODEL), BF16)],
    )
    return pl.pallas_call(
        _moe_kernel,
        grid_spec=grid_spec,
        out_shape=jax.ShapeDtypeStruct(x_sorted.shape, U32),
        compiler_params=pltpu.CompilerParams(
            dimension_semantics=("arbitrary",), vmem_limit_bytes=VMEM_LIMIT),
        name="moe",
    )(blk_expert, n_used, x_sorted, w_gate, w_up, w_down)


def _combine_kernel(h2_ref, rw_ref, g_ref, y0_ref, y1_ref, o_ref):
    rw = rw_ref[...]
    h = h2_ref[...] + rw[:, 0:1] * _unpack_halves(y0_ref[...]) + rw[:, 1:2] * _unpack_halves(y1_ref[...])
    o_ref[...] = h * lax.rsqrt(jnp.mean(h * h, axis=-1, keepdims=True) + EPS) * g_ref[...]


def _combine(h2, rweights, final_g, y):
    t = h2.shape[0]
    tm = min(COMBINE_TM, t)
    row = lambda i: (i, 0)
    return pl.pallas_call(
        _combine_kernel,
        grid=(t // tm,),
        in_specs=[pl.BlockSpec((tm, D_MODEL), row),
                  pl.BlockSpec((tm, LANES), row),
                  pl.BlockSpec((1, D_MODEL), lambda i: (0, 0)),
                  pl.BlockSpec((tm, HALF), row),
                  pl.BlockSpec((tm, HALF), lambda i: (i + t // tm, 0))],
        out_specs=pl.BlockSpec((tm, D_MODEL), row),
        out_shape=jax.ShapeDtypeStruct((t, D_MODEL), F32),
        compiler_params=pltpu.CompilerParams(
            dimension_semantics=("arbitrary",), vmem_limit_bytes=VMEM_LIMIT),
        name="combine",
    )(h2, rweights, final_g, y, y)


def _block_layout(ids, counts_row, t):
    n_blocks = 2 * t // MOE_ROWS + N_EXPERTS
    counts = counts_row[0, ROUTE_LANE0:ROUTE_LANE0 + N_EXPERTS].astype(jnp.int32)
    padded = ((counts + MOE_ROWS - 1) // MOE_ROWS) * MOE_ROWS
    pend = jnp.cumsum(padded)
    pstart = pend - padded
    blk_first = jnp.arange(n_blocks, dtype=jnp.int32) * MOE_ROWS
    blk_expert = jnp.minimum(jnp.sum((pend[None, :] <= blk_first[:, None]).astype(jnp.int32), axis=1),
                             N_EXPERTS - 1).astype(jnp.int32)
    n_used = (pend[-1] // MOE_ROWS).astype(jnp.int32).reshape(1)
    expert, rank = ids[2:4], ids[4:6]
    experts = jnp.arange(N_EXPERTS, dtype=jnp.int32)
    first_row = jnp.sum(jnp.where(expert[:, :, None] == experts, pstart, 0), axis=-1)
    dest = (first_row + rank).reshape(2 * t)
    return dest, blk_expert, n_used, n_blocks * MOE_ROWS


def kernel(x, meta_tokens, hg_lb_logits, norm_mix_g, w_in, gd_conv_w, gd_A_log, gd_dt_bias, hg_norm_g, gd_norm_g, hg_up, gd_up, w_out, norm_ffn_g, router_group_w, router_group_b, router_expert_w, router_expert_b, w_gate, w_up, w_down, final_norm_g):
    bsz, seq, d = x.shape
    t = bsz * seq
    x2d = x.reshape(t, d)

    lb = jnp.cumsum(jax.nn.softmax(hg_lb_logits.astype(F32), axis=0), axis=0)[0].reshape(1, WIDTH)
    w_all = w_in.astype(F32)
    w_gates = w_all[0, :, W_SMALL + 2 * HEADS:]
    g_mix = norm_mix_g[0].reshape(1, d)
    meta_blk = jnp.concatenate([jnp.zeros((CHUNK - N_META, d), F32), meta_tokens.astype(F32)], axis=0)

    conv_w = gd_conv_w[0].astype(F32)
    no_history = jnp.zeros((SUBLANES, 3 * WIDTH), F32)
    proj, proj_meta = {}, {}
    proj["hgrn"] = _norm_proj_hgrn(x2d, g_mix, lb, w_all, PROJ_TM)
    proj_meta["hgrn"] = _norm_proj_hgrn(meta_blk, g_mix, lb, w_all, CHUNK)
    proj_meta["gdn"], meta_tail, logits_meta = _norm_proj_conv(meta_blk, g_mix, w_all, conv_w, no_history,
                                                               CHUNK, CHUNK)
    proj["gdn"], _, logits = _norm_proj_conv(x2d, g_mix, w_all, conv_w, meta_tail, PROJ_TM, seq)

    def scalar_rows(lg, nb, nc):
        s = lg.reshape(2, N_STACKS, HEADS_PER_STACK, nb, nc, CHUNK)
        return s.transpose(3, 4, 1, 0, 2, 5).reshape(nb, nc, N_STACKS, 2, STACK)

    a_log_row = jnp.repeat(gd_A_log[0].astype(F32), CHUNK).reshape(N_STACKS, 1, STACK)
    dt_row = jnp.repeat(gd_dt_bias[0].astype(F32), CHUNK).reshape(N_STACKS, 1, STACK)
    hg_g, gd_g = hg_norm_g[0].reshape(1, HEAD_DIM), gd_norm_g[0].reshape(1, HEAD_DIM)

    zero_state = jnp.zeros((HEADS, HEAD_DIM, HEAD_DIM), F32)
    per_seq = lambda st: jnp.tile(st, (bsz, 1, 1))
    _, hg_state = _hgrn(proj_meta["hgrn"][None], zero_state, hg_g, 1)
    o_a, _ = _hgrn(proj["hgrn"].reshape(bsz, seq, -1), per_seq(hg_state), hg_g, HGRN_CHUNKS)
    _, gd_state = _gdn(proj_meta["gdn"][None], zero_state, scalar_rows(logits_meta, 1, 1),
                       a_log_row, dt_row, gd_g, 1)
    o_b, _ = _gdn(proj["gdn"].reshape(bsz, seq, -1), per_seq(gd_state),
                  scalar_rows(logits, bsz, seq // CHUNK), a_log_row, dt_row, gd_g, GDN_CHUNKS)

    rw = jnp.zeros((d, LANES), F32)
    rw = rw.at[:, :N_GROUPS].set(router_group_w[0]).at[:, N_GROUPS:N_GROUPS + N_EXPERTS].set(router_expert_w[0])
    rb = jnp.zeros((1, LANES), F32)
    rb = rb.at[0, :N_GROUPS].set(router_group_b[0]).at[0, N_GROUPS:N_GROUPS + N_EXPERTS].set(router_expert_b[0])
    h2, xn, info, ids, counts = _merge(o_a.reshape(t, WIDTH), o_b.reshape(t, WIDTH), x2d, g_mix, w_gates,
                                  hg_up[0].astype(BF16), gd_up[0].astype(BF16), w_out[0].astype(BF16),
                                  norm_ffn_g[0].reshape(1, d), rw.astype(BF16), rb)

    dest, blk_expert, n_used, n_rows = _block_layout(ids, counts, t)
    x_sorted = _sc_dispatch(xn, dest, n_rows)
    y_sorted = _moe(blk_expert, n_used, x_sorted,
                    w_gate[0].astype(F32), w_up[0].astype(F32), w_down[0].astype(F32))
    y_tok = _sc_gather(y_sorted, dest)
    out = _combine(h2, info, final_norm_g.reshape(1, d), y_tok)
    return out.reshape(bsz, seq, d)
```
